```python
import jax, jax.numpy as jnp
from jax import lax
import numpy as np

D_MODEL = 1024
BATCH = 4
SEQ = 8192
DEPTH = 2

GRID_W = 64
CTX_LEN = 256
N_Q_HEADS = 8
N_KV_HEADS = 2
GQA_GROUP = N_Q_HEADS // N_KV_HEADS
HEAD_DIM = 64
WINDOW = 128
ATT_BLOCK = 128
ROPE_BASE = 10000.0
Q_DIM = N_Q_HEADS * HEAD_DIM
KV_DIM = N_KV_HEADS * HEAD_DIM
SG_GROUPS = 8
SG_HEAD = 64
SG_WIDTH = SG_GROUPS * SG_HEAD
SG_CHUNK = 128
EVEN_IN = Q_DIM + 2 * KV_DIM + 2 * SG_WIDTH
EVEN_MIX = Q_DIM + SG_WIDTH
CONV_WIDTH = D_MODEL
CONV_K = 3
N_GROUPS = 4
EXPERTS_PER_GROUP = 8
N_EXPERTS = N_GROUPS * EXPERTS_PER_GROUP
TOP_K_IN_GROUP = 2
EXPERT_HIDDEN = D_MODEL // 2
MOE_BLOCK = 128
N_EVEN = (DEPTH + 1) // 2
N_ODD = DEPTH // 2
EPS = 1e-6
NEG_INF = -1e30

kernel_name = 'hybrid_diffusion_swa_sgmlp_shortconv_hmoe'


def rms_norm(x, g):
    xf = x.astype(jnp.float32)
    y = xf * lax.rsqrt(jnp.mean(xf * xf, axis=-1, keepdims=True) + EPS)
    return (y * g.astype(jnp.float32)).astype(x.dtype)


def layer_norm(x, g):
    xf = x.astype(jnp.float32)
    mu = jnp.mean(xf, axis=-1, keepdims=True)
    xc = xf - mu
    y = xc * lax.rsqrt(jnp.mean(xc * xc, axis=-1, keepdims=True) + EPS)
    return (y * g.astype(jnp.float32)).astype(x.dtype)


def modulate(h, shift, scale):
    return h * (1 + scale) + shift


def axial_rope_tables(rows, dtype):
    quarter = HEAD_DIM // 4
    row_ids = jnp.repeat(jnp.arange(rows, dtype=jnp.float32), GRID_W)
    col_ids = jnp.tile(jnp.arange(GRID_W, dtype=jnp.float32), rows)
    inv = ROPE_BASE ** (-jnp.arange(quarter, dtype=jnp.float32) / quarter)
    ang_r = row_ids[:, None] * inv
    ang_c = col_ids[:, None] * inv
    return (jnp.cos(ang_r)[:, None, :].astype(dtype), jnp.sin(ang_r)[:, None, :].astype(dtype),
            jnp.cos(ang_c)[:, None, :].astype(dtype), jnp.sin(ang_c)[:, None, :].astype(dtype))


def apply_axial_rope(t, rope):
    cos_r, sin_r, cos_c, sin_c = rope
    half = HEAD_DIM // 2
    quarter = HEAD_DIM // 4

    def rot(u, cos, sin):
        u1, u2 = u[..., :quarter], u[..., quarter:]
        return jnp.concatenate([u1 * cos - u2 * sin, u2 * cos + u1 * sin], axis=-1)

    return jnp.concatenate([rot(t[..., :half], cos_r, sin_r), rot(t[..., half:], cos_c, sin_c)], axis=-1)


def windowed_attention(q, k, v, kc, vc, sink):
    b, s = q.shape[:2]
    nb = s // ATT_BLOCK
    scale = HEAD_DIM ** -0.5
    f32 = jnp.float32
    qb = q.reshape(b, nb, ATT_BLOCK, N_KV_HEADS, GQA_GROUP, HEAD_DIM)

    def band(t):
        tp = jnp.pad(t, ((0, 0), (ATT_BLOCK, ATT_BLOCK), (0, 0), (0, 0)))
        tp = tp.reshape(b, nb + 2, ATT_BLOCK, N_KV_HEADS, HEAD_DIM)
        return jnp.concatenate([tp[:, :-2], tp[:, 1:-1], tp[:, 2:]], axis=2)

    kb, vb = band(k), band(v)
    s_loc = jnp.einsum('bnqhgd,bnkhd->bnhgqk', qb, kb, preferred_element_type=f32) * scale
    s_ctx = jnp.einsum('bnqhgd,bchd->bnhgqc', qb, kc, preferred_element_type=f32) * scale
    qi = jnp.arange(ATT_BLOCK)[:, None]
    kj = jnp.arange(3 * ATT_BLOCK)[None, :]
    in_win = jnp.abs(kj - qi - ATT_BLOCK) <= WINDOW
    kpos = (jnp.arange(nb)[:, None] - 1) * ATT_BLOCK + jnp.arange(3 * ATT_BLOCK)[None, :]
    mask = in_win[None] & ((kpos >= 0) & (kpos < s))[:, None, :]
    s_loc = jnp.where(mask[None, :, None, None], s_loc, NEG_INF)
    sink_l = jnp.broadcast_to(sink.astype(f32).reshape(1, 1, N_KV_HEADS, GQA_GROUP, 1, 1), s_ctx.shape[:-1] + (1,))
    p = jax.nn.softmax(jnp.concatenate([sink_l, s_ctx, s_loc], axis=-1), axis=-1)
    n_ctx = kc.shape[1]
    p_ctx = p[..., 1:1 + n_ctx].astype(vc.dtype)
    p_loc = p[..., 1 + n_ctx:].astype(vb.dtype)
    o = (jnp.einsum('bnhgqc,bchd->bnqhgd', p_ctx, vc)
         + jnp.einsum('bnhgqk,bnkhd->bnqhgd', p_loc, vb))
    return o.reshape(b, s, Q_DIM)


def context_attention(qc, kc, vc, sink):
    b, n = qc.shape[:2]
    f32 = jnp.float32
    qh = qc.reshape(b, n, N_KV_HEADS, GQA_GROUP, HEAD_DIM)
    sc = jnp.einsum('bqhgd,bkhd->bhgqk', qh, kc, preferred_element_type=f32) * (HEAD_DIM ** -0.5)
    sink_c = jnp.broadcast_to(sink.astype(f32).reshape(1, N_KV_HEADS, GQA_GROUP, 1, 1), sc.shape[:-1] + (1,))
    p = jax.nn.softmax(jnp.concatenate([sink_c, sc], axis=-1), axis=-1)
    o = jnp.einsum('bhgqk,bkhd->bqhgd', p[..., 1:].astype(vc.dtype), vc)
    return o.reshape(b, n, Q_DIM)


def spatial_gating(u, z, g_sgu, w_sp, b_sp):
    b, n = u.shape[:2]
    u = jax.nn.gelu(u, approximate=False)
    z = layer_norm(jax.nn.gelu(z, approximate=False), g_sgu)
    zc = z.reshape(b, n // SG_CHUNK, SG_CHUNK, SG_GROUPS, SG_HEAD)
    sgate = jnp.einsum('hij,bnjhc->bnihc', w_sp, zc) + b_sp.T[:, :, None]
    return (u.reshape(zc.shape) * sgate).reshape(b, n, SG_WIDTH)


def even_mixer(h, hc, w_in, sink, g_sgu, w_sp, b_sp, w_out, rope, need_ctx):
    b, s, _ = h.shape
    n_ctx = hc.shape[1]
    q, k, v, u, z = jnp.split(h @ w_in, [Q_DIM, Q_DIM + KV_DIM, Q_DIM + 2 * KV_DIM, Q_DIM + 2 * KV_DIM + SG_WIDTH], axis=-1)
    q = apply_axial_rope(q.reshape(b, s, N_Q_HEADS, HEAD_DIM), rope)
    k = apply_axial_rope(k.reshape(b, s, N_KV_HEADS, HEAD_DIM), rope)
    v = v.reshape(b, s, N_KV_HEADS, HEAD_DIM)
    kc, vc = jnp.split(hc @ w_in[:, Q_DIM:Q_DIM + 2 * KV_DIM], 2, axis=-1)
    kc = kc.reshape(b, n_ctx, N_KV_HEADS, HEAD_DIM)
    vc = vc.reshape(b, n_ctx, N_KV_HEADS, HEAD_DIM)
    att = windowed_attention(q, k, v, kc, vc, sink)
    sg = spatial_gating(u, z, g_sgu, w_sp, b_sp)
    out_lat = jnp.concatenate([att, sg], axis=-1) @ w_out
    if not need_ctx:
        return out_lat, None
    qc = hc @ w_in[:, :Q_DIM]
    uc, zc = jnp.split(hc @ w_in[:, Q_DIM + 2 * KV_DIM:], 2, axis=-1)
    att_c = context_attention(qc, kc, vc, sink)
    sg_c = spatial_gating(uc, zc, g_sgu, w_sp, b_sp)
    return out_lat, jnp.concatenate([att_c, sg_c], axis=-1) @ w_out


def short_conv_mixer(h, w_in, conv_w, w_out):
    bg, cg, xt = jnp.split(h @ w_in, 3, axis=-1)
    y = cg * xt
    yp = jnp.pad(y, ((0, 0), (1, 1), (0, 0)))
    conv = yp[:, :-2] * conv_w[0] + yp[:, 1:-1] * conv_w[1] + yp[:, 2:] * conv_w[2]
    return (bg * conv) @ w_out


def moe_ffn(h, w_rg, b_rg, w_re, b_re, w_gate, w_up, w_down):
    n, d = h.shape
    f32 = jnp.float32
    g_logits = jnp.einsum('nd,dg->ng', h, w_rg, preferred_element_type=f32) + b_rg.astype(f32)
    g_idx = jnp.argmax(g_logits, axis=-1).astype(jnp.int32)
    g_w = jnp.take_along_axis(jax.nn.softmax(g_logits, axis=-1), g_idx[:, None], axis=-1)
    e_logits = (jnp.einsum('nd,de->ne', h, w_re, preferred_element_type=f32) + b_re.astype(f32))
    e_logits = e_logits.reshape(n, N_GROUPS, EXPERTS_PER_GROUP)
    e_sel = jnp.take_along_axis(e_logits, g_idx[:, None, None], axis=1)[:, 0]
    top_v, top_i = lax.top_k(e_sel, TOP_K_IN_GROUP)
    gate = g_w * jax.nn.softmax(top_v, axis=-1)
    expert = g_idx[:, None] * EXPERTS_PER_GROUP + top_i.astype(jnp.int32)
    n_assign = n * TOP_K_IN_GROUP
    flat_e = expert.reshape(-1)
    flat_tok = jnp.repeat(jnp.arange(n, dtype=jnp.int32), TOP_K_IN_GROUP)
    flat_w = gate.reshape(-1).astype(h.dtype)
    order = jnp.argsort(flat_e)
    sorted_e = flat_e[order]
    counts = jnp.bincount(flat_e, length=N_EXPERTS)
    padded = (counts + MOE_BLOCK - 1) // MOE_BLOCK * MOE_BLOCK
    pad_end = jnp.cumsum(padded)
    pad_start = pad_end - padded
    raw_start = jnp.cumsum(counts) - counts
    dest = pad_start[sorted_e] + jnp.arange(n_assign, dtype=pad_start.dtype) - raw_start[sorted_e]
    n_blocks = -(-n_assign // MOE_BLOCK) + N_EXPERTS
    n_rows = n_blocks * MOE_BLOCK
    row_tok = jnp.zeros((n_rows,), jnp.int32).at[dest].set(flat_tok[order])
    row_w = jnp.zeros((n_rows,), h.dtype).at[dest].set(flat_w[order])
    block_e = jnp.clip(jnp.searchsorted(pad_end, jnp.arange(n_blocks, dtype=pad_end.dtype) * MOE_BLOCK, side='right'), 0, N_EXPERTS - 1)
    xb = h[row_tok].reshape(n_blocks, MOE_BLOCK, d)

    def expert_block(args):
        xblk, e = args
        hid = jax.nn.silu(xblk @ w_gate[e]) * (xblk @ w_up[e])
        return hid @ w_down[e]

    yb = lax.map(expert_block, (xb, block_e)).reshape(n_rows, d)
    return jnp.zeros_like(h).at[row_tok].add(yb * row_w[:, None])


def setup_inputs(seed: int = 0) -> dict:
    key = jax.random.key(seed)
    ks = jax.random.split(key, 25)
    D = D_MODEL

    def nrm(k, shape, s):
        return jax.random.normal(k, shape, jnp.float32) * s

    return {
        'x': nrm(ks[0], (BATCH, SEQ, D), 1.0),
        'c': nrm(ks[1], (BATCH, D), 1.0),
        'ctx': nrm(ks[2], (BATCH, CTX_LEN, D), 1.0),
        'c_ctx': nrm(ks[3], (D,), 1.0),
        'w_ada': nrm(ks[4], (DEPTH, D, 6 * D), 0.5 * D ** -0.5),
        'b_ada': nrm(ks[5], (DEPTH, 6 * D), 0.02),
        'g_norm1': 1.0 + nrm(ks[6], (DEPTH, D), 0.1),
        'g_norm2': 1.0 + nrm(ks[7], (DEPTH, D), 0.1),
        'g_final': 1.0 + nrm(ks[8], (D,), 0.1),
        'w_in_even': nrm(ks[9], (N_EVEN, D, EVEN_IN), D ** -0.5),
        'attn_sink': nrm(ks[10], (N_EVEN, N_Q_HEADS), 1.0),
        'g_sgu': 1.0 + nrm(ks[11], (N_EVEN, SG_WIDTH), 0.1),
        'w_spatial': nrm(ks[12], (N_EVEN, SG_GROUPS, SG_CHUNK, SG_CHUNK), SG_CHUNK ** -0.5),
        'b_spatial': 1.0 + nrm(ks[13], (N_EVEN, SG_GROUPS, SG_CHUNK), 0.1),
        'w_out_even': nrm(ks[14], (N_EVEN, EVEN_MIX, D), EVEN_MIX ** -0.5),
        'w_in_odd': nrm(ks[15], (N_ODD, D, 3 * CONV_WIDTH), D ** -0.5),
        'conv_w': nrm(ks[16], (N_ODD, CONV_K, CONV_WIDTH), CONV_K ** -0.5),
        'w_out_odd': nrm(ks[17], (N_ODD, CONV_WIDTH, D), CONV_WIDTH ** -0.5),
        'w_router_group': nrm(ks[18], (DEPTH, D, N_GROUPS), D ** -0.5),
        'b_router_group': nrm(ks[19], (DEPTH, N_GROUPS), 0.01),
        'w_router_expert': nrm(ks[20], (DEPTH, D, N_EXPERTS), D ** -0.5),
        'b_router_expert': nrm(ks[21], (DEPTH, N_EXPERTS), 0.01),
        'w_gate': nrm(ks[22], (DEPTH, N_EXPERTS, D, EXPERT_HIDDEN), D ** -0.5),
        'w_up': nrm(ks[23], (DEPTH, N_EXPERTS, D, EXPERT_HIDDEN), D ** -0.5),
        'w_down': nrm(ks[24], (DEPTH, N_EXPERTS, EXPERT_HIDDEN, D), EXPERT_HIDDEN ** -0.5),
    }


def reference(x, c, ctx, c_ctx, w_ada, b_ada, g_norm1, g_norm2, g_final, w_in_even, attn_sink, g_sgu,
              w_spatial, b_spatial, w_out_even, w_in_odd, conv_w, w_out_odd, w_router_group,
              b_router_group, w_router_expert, b_router_expert, w_gate, w_up, w_down):
    b, s, d = x.shape
    rows = s // GRID_W
    rope = axial_rope_tables(rows, x.dtype)
    silu_c = jax.nn.silu(c)
    silu_cc = jax.nn.silu(c_ctx)
    lat, cx = x, ctx
    for l in range(DEPTH):
        need_ctx = any(j % 2 == 0 for j in range(l + 1, DEPTH))
        m = jnp.split(silu_c @ w_ada[l] + b_ada[l], 6, axis=-1)
        sh1, sc1, g1, sh2, sc2, g2 = [t[:, None, :] for t in m]
        h = modulate(rms_norm(lat, g_norm1[l]), sh1, sc1)
        mix_c = None
        if l % 2 == 0 or need_ctx:
            csh1, csc1, cg1, csh2, csc2, cg2 = jnp.split(silu_cc @ w_ada[l] + b_ada[l], 6, axis=-1)
            hc = modulate(rms_norm(cx, g_norm1[l]), csh1, csc1)
        i = l // 2
        if l % 2 == 0:
            mix, mix_c = even_mixer(h, hc, w_in_even[i], attn_sink[i], g_sgu[i], w_spatial[i],
                                    b_spatial[i], w_out_even[i], rope, need_ctx)
        else:
            mix = short_conv_mixer(h, w_in_odd[i], conv_w[i], w_out_odd[i])
            if need_ctx:
                mix_c = short_conv_mixer(hc, w_in_odd[i], conv_w[i], w_out_odd[i])
        lat = lat + g1 * mix
        h2 = modulate(rms_norm(lat, g_norm2[l]), sh2, sc2)
        lat = lat + g2 * moe_ffn(h2.reshape(-1, d), w_router_group[l], b_router_group[l], w_router_expert[l],
                                 b_router_expert[l], w_gate[l], w_up[l], w_down[l]).reshape(b, s, d)
        if need_ctx:
            cx = cx + cg1 * mix_c
            hc2 = modulate(rms_norm(cx, g_norm2[l]), csh2, csc2)
            cx = cx + cg2 * moe_ffn(hc2.reshape(-1, d), w_router_group[l], b_router_group[l], w_router_expert[l],
                                    b_router_expert[l], w_gate[l], w_up[l], w_down[l]).reshape(cx.shape)
    return rms_norm(lat, g_final)
```

```python
import functools

import jax
import jax.numpy as jnp
from jax import lax
from jax.experimental import pallas as pl
from jax.experimental.pallas import tpu as pltpu

F32 = jnp.float32
BF16 = jnp.bfloat16
HIGHEST = lax.Precision.HIGHEST

GRID_W = 64
N_Q_HEADS = 8
N_KV_HEADS = 2
HEAD_DIM = 64
ATT_BLOCK = 128
ROPE_BASE = 10000.0
Q_DIM = N_Q_HEADS * HEAD_DIM
KV_DIM = N_KV_HEADS * HEAD_DIM
SG_GROUPS = 8
SG_WIDTH = SG_GROUPS * HEAD_DIM
N_GROUPS = 4
EXPERTS_PER_GROUP = 8
N_EXPERTS = N_GROUPS * EXPERTS_PER_GROUP
EPS = 1e-6
NEG_INF = -1e30

LANES = 128
MOD_ROWS = 8
VMEM_LIMIT = 56 * 1024 * 1024

ADA_TN = 1536
EVEN_TM = 512
ATT_TQ = 512
ODD_TM = 256
ROUTE_TM = 512
MOE_BM = 256
DISPATCH_TM = 512
COMBINE_TM = 256


def _params(sem):
    return pltpu.CompilerParams(dimension_semantics=sem, vmem_limit_bytes=VMEM_LIMIT)


def _rms_mod(x, g, shift, scale):
    ms = jnp.mean(x * x, axis=-1, keepdims=True)
    return (x * lax.rsqrt(ms + EPS) * g) * (1.0 + scale) + shift


def _ada_kernel(a_ref, w_ref, b_ref, o_ref):
    a = a_ref[...]
    s = a * (1.0 / (1.0 + jnp.exp(-a)))
    o_ref[0] = jnp.dot(s, w_ref[0], preferred_element_type=F32, precision=HIGHEST) + b_ref[0]


def _ada(cond, w_ada, b_ada):
    depth, d, six_d = w_ada.shape
    return pl.pallas_call(
        _ada_kernel,
        grid=(depth, six_d // ADA_TN),
        in_specs=[
            pl.BlockSpec((MOD_ROWS, d), lambda l, j: (0, 0)),
            pl.BlockSpec((1, d, ADA_TN), lambda l, j: (l, 0, j)),
            pl.BlockSpec((1, 1, ADA_TN), lambda l, j: (l, 0, j)),
        ],
        out_specs=pl.BlockSpec((1, MOD_ROWS, ADA_TN), lambda l, j: (l, 0, j)),
        out_shape=jax.ShapeDtypeStruct((depth, MOD_ROWS, six_d), F32),
        compiler_params=_params(("arbitrary", "arbitrary")),
        name="ada",
    )(cond, w_ada, b_ada.reshape(depth, 1, six_d))


def _even_in_kernel(x_ref, sh_ref, sc_ref, g_ref, w_ref, cos_ref, sa_ref, sb_ref,
                    q_ref, kx_ref, vx_ref, u_ref, z_ref):
    h = _rms_mod(x_ref[...], g_ref[...], sh_ref[...], sc_ref[...])
    p = jnp.dot(h.astype(BF16), w_ref[...], preferred_element_type=F32)
    cos, sa, sb = cos_ref[...], sa_ref[...], sb_ref[...]

    def rope(t):
        return t * cos + pltpu.roll(t, LANES - 16, 1) * sa + pltpu.roll(t, 16, 1) * sb

    scale = HEAD_DIM ** -0.5
    for cblk in range(Q_DIM // LANES):
        t = p[:, cblk * LANES:(cblk + 1) * LANES]
        q_ref[:, cblk * LANES:(cblk + 1) * LANES] = (rope(t) * scale).astype(BF16)

    low = lax.broadcasted_iota(jnp.int32, (x_ref.shape[0], LANES), 1) < HEAD_DIM

    def spread(t, o_ref):
        sw = pltpu.roll(t, HEAD_DIM, 1)
        zero = jnp.zeros_like(t)
        o_ref[:, 0 * LANES:1 * LANES] = jnp.where(low, t, zero).astype(BF16)
        o_ref[:, 1 * LANES:2 * LANES] = jnp.where(low, zero, sw).astype(BF16)
        o_ref[:, 2 * LANES:3 * LANES] = jnp.where(low, sw, zero).astype(BF16)
        o_ref[:, 3 * LANES:4 * LANES] = jnp.where(low, zero, t).astype(BF16)

    spread(rope(p[:, Q_DIM:Q_DIM + KV_DIM]), kx_ref)
    spread(p[:, Q_DIM + KV_DIM:Q_DIM + 2 * KV_DIM], vx_ref)
    u0 = Q_DIM + 2 * KV_DIM
    u_ref[...] = p[:, u0:u0 + SG_WIDTH]
    z_ref[...] = p[:, u0 + SG_WIDTH:u0 + 2 * SG_WIDTH]


def _even_in(x2d, mods, layer, mod_row_fn, g, w_bf, tabs, tab_blocks, tm):
    n, d = x2d.shape
    ein = w_bf.shape[1]
    cos, sa, sb = tabs
    row = lambda i: layer * MOD_ROWS + mod_row_fn(i)
    tab_spec = pl.BlockSpec((tm, LANES), lambda i: (i % tab_blocks, 0))
    wide = 4 * LANES
    return pl.pallas_call(
        _even_in_kernel,
        grid=(n // tm,),
        in_specs=[
            pl.BlockSpec((tm, d), lambda i: (i, 0)),
            pl.BlockSpec((None, 1, d), lambda i: (row(i), 0, 0)),
            pl.BlockSpec((None, 1, d), lambda i: (row(i), 0, 1)),
            pl.BlockSpec((1, d), lambda i: (0, 0)),
            pl.BlockSpec((d, ein), lambda i: (0, 0)),
            tab_spec, tab_spec, tab_spec,
        ],
        out_specs=[
            pl.BlockSpec((tm, Q_DIM), lambda i: (i, 0)),
            pl.BlockSpec((tm, wide), lambda i: (i, 0)),
            pl.BlockSpec((tm, wide), lambda i: (i, 0)),
            pl.BlockSpec((tm, SG_WIDTH), lambda i: (i, 0)),
            pl.BlockSpec((tm, SG_WIDTH), lambda i: (i, 0)),
        ],
        out_shape=[
            jax.ShapeDtypeStruct((n, Q_DIM), BF16),
            jax.ShapeDtypeStruct((n, wide), BF16),
            jax.ShapeDtypeStruct((n, wide), BF16),
            jax.ShapeDtypeStruct((n, SG_WIDTH), F32),
            jax.ShapeDtypeStruct((n, SG_WIDTH), F32),
        ],
        compiler_params=_params(("parallel",)),
        name="even_in",
    )(x2d, mods, mods, g, w_bf, cos, sa, sb)


def _gelu(x):
    return 0.5 * x * (1.0 + lax.erf(x * (2.0 ** -0.5)))


def _even_mix_kernel(sink_ref, lat_ref, q_ref, kxm_ref, kxp_ref, kxn_ref, vxm_ref, vxp_ref, vxn_ref,
                     kcx_ref, vcx_ref, u_ref, z_ref, gsgu_ref, wsp_ref, bsp_ref, wout_ref, gate_ref,
                     o_ref, kband, vband, mixin, *, tiles_per_seq):
    i = pl.program_id(0)
    tq = q_ref.shape[0]
    blk = ATT_BLOCK
    nsub = tq // blk
    first = (i % tiles_per_seq) == 0
    last = (i % tiles_per_seq) == tiles_per_seq - 1

    kband[0:blk] = kxp_ref[...]
    kband[blk:blk + tq] = kxm_ref[...]
    kband[blk + tq:] = kxn_ref[...]
    vband[0:blk] = vxp_ref[...]
    vband[blk:blk + tq] = vxm_ref[...]
    vband[blk + tq:] = vxn_ref[...]

    row = lax.broadcasted_iota(jnp.int32, (blk, 3 * blk), 0)
    col = lax.broadcasted_iota(jnp.int32, (blk, 3 * blk), 1)
    lane_low = lax.broadcasted_iota(jnp.int32, (blk, LANES), 1) < HEAD_DIM
    nt = (((1,), (1,)), ((), ()))

    def sub_block(j, carry):
        r0 = pl.multiple_of(j * blk, blk)
        ok_prev = jnp.logical_not(jnp.logical_and(first, j == 0))
        ok_next = jnp.logical_not(jnp.logical_and(last, j == nsub - 1))
        mask = (((col >= blk) | ((col >= row) & ok_prev))
                & ((col < 2 * blk) | ((col - 2 * blk <= row) & ok_next)))
        qj = q_ref[pl.ds(r0, blk), :]
        kb = kband[pl.ds(r0, 3 * blk), :]
        vb = vband[pl.ds(r0, 3 * blk), :]
        for pair in range(N_Q_HEADS // 2):
            acc = None
            for par in range(2):
                hd = 2 * pair + par
                var = 2 * (hd // (N_Q_HEADS // N_KV_HEADS)) + par
                qh = qj[:, pair * LANES:(pair + 1) * LANES]
                s_loc = lax.dot_general(qh, kb[:, var * LANES:(var + 1) * LANES], nt,
                                        preferred_element_type=F32)
                s_ctx = lax.dot_general(qh, kcx_ref[:, var * LANES:(var + 1) * LANES], nt,
                                        preferred_element_type=F32)
                s_loc = jnp.where(mask, s_loc, NEG_INF)
                sk = sink_ref[hd]
                m = jnp.maximum(jnp.maximum(jnp.max(s_loc, axis=-1, keepdims=True),
                                            jnp.max(s_ctx, axis=-1, keepdims=True)), sk)
                p_loc = jnp.exp(s_loc - m)
                p_ctx = jnp.exp(s_ctx - m)
                den = (jnp.sum(p_loc, axis=-1, keepdims=True) + jnp.sum(p_ctx, axis=-1, keepdims=True)
                       + jnp.exp(sk - m))
                o = (jnp.dot(p_ctx.astype(BF16), vcx_ref[:, var * LANES:(var + 1) * LANES],
                             preferred_element_type=F32)
                     + jnp.dot(p_loc.astype(BF16), vb[:, var * LANES:(var + 1) * LANES],
                               preferred_element_type=F32))
                o = o / den
                acc = o if acc is None else acc + o
            mixin[pl.ds(r0, blk), pair * LANES:(pair + 1) * LANES] = acc.astype(BF16)

        ug = _gelu(u_ref[pl.ds(r0, blk), :])
        zg = _gelu(z_ref[pl.ds(r0, blk), :])
        mu = jnp.mean(zg, axis=-1, keepdims=True)
        zc = zg - mu
        zn = zc * lax.rsqrt(jnp.mean(zc * zc, axis=-1, keepdims=True) + EPS) * gsgu_ref[...]
        for pair in range(SG_GROUPS // 2):
            zp = zn[:, pair * LANES:(pair + 1) * LANES]
            zero = jnp.zeros_like(zp)
            lo = jnp.where(lane_low, zp, zero).astype(BF16)
            hi = jnp.where(lane_low, zero, zp).astype(BF16)
            sg = (jnp.dot(wsp_ref[2 * pair], lo, preferred_element_type=F32)
                  + jnp.dot(wsp_ref[2 * pair + 1], hi, preferred_element_type=F32)
                  + bsp_ref[:, pair * LANES:(pair + 1) * LANES])
            mixin[pl.ds(r0, blk), Q_DIM + pair * LANES:Q_DIM + (pair + 1) * LANES] = (
                ug[:, pair * LANES:(pair + 1) * LANES] * sg).astype(BF16)
        return carry

    lax.fori_loop(0, nsub, sub_block, 0)
    mix = jnp.dot(mixin[...], wout_ref[...], preferred_element_type=F32)
    o_ref[...] = lat_ref[...] + gate_ref[...] * mix


def _even_mix(lat, q, kx, vx, kcx, vcx, u, z, sink, g_sgu, wsp_bf, bsp_full, wout_bf, mods, seq, ctx_len):
    n, d = lat.shape
    tq = ATT_TQ
    tiles_per_seq = seq // tq
    sub = tq // ATT_BLOCK
    nblk = n // ATT_BLOCK
    wide = 4 * LANES
    main = lambda w: pl.BlockSpec((tq, w), lambda i: (i, 0))
    prev = pl.BlockSpec((ATT_BLOCK, wide), lambda i: (jnp.maximum(i * sub - 1, 0), 0))
    nxt = pl.BlockSpec((ATT_BLOCK, wide), lambda i: (jnp.minimum((i + 1) * sub, nblk - 1), 0))
    ctxs = pl.BlockSpec((ctx_len, wide), lambda i: (i // tiles_per_seq, 0))
    const = lambda shape: pl.BlockSpec(shape, lambda i: (0,) * len(shape))
    return pl.pallas_call(
        functools.partial(_even_mix_kernel, tiles_per_seq=tiles_per_seq),
        grid=(n // tq,),
        in_specs=[
            pl.BlockSpec(memory_space=pltpu.SMEM),
            main(d), main(Q_DIM),
            main(wide), prev, nxt,
            main(wide), prev, nxt,
            ctxs, ctxs,
            main(SG_WIDTH), main(SG_WIDTH),
            const((1, SG_WIDTH)), const(wsp_bf.shape), const(bsp_full.shape), const(wout_bf.shape),
            pl.BlockSpec((None, 1, d), lambda i: (i // tiles_per_seq, 0, 2)),
        ],
        out_specs=pl.BlockSpec((tq, d), lambda i: (i, 0)),
        out_shape=jax.ShapeDtypeStruct((n, d), F32),
        scratch_shapes=[
            pltpu.VMEM((tq + 2 * ATT_BLOCK, wide), BF16),
            pltpu.VMEM((tq + 2 * ATT_BLOCK, wide), BF16),
            pltpu.VMEM((tq, Q_DIM + SG_WIDTH), BF16),
        ],
        compiler_params=_params(("parallel",)),
        name="even_mix",
    )(sink, lat, q, kx, kx, kx, vx, vx, vx, kcx, vcx, u, z, g_sgu, wsp_bf, bsp_full, wout_bf, mods)


def _odd_mix_kernel(x_ref, xp_ref, sh_ref, sc_ref, gate_ref, g_ref, win_ref, cw_ref, wout_ref,
                    o_ref, y_s, bg_s, tail_s, *, tiles_per_seq):
    i = pl.program_id(0)
    n_tiles = pl.num_programs(0) - 1
    tm, d = x_ref.shape
    cur = i % 2
    prv = 1 - cur

    tail_s[...] = y_s[cur, tm - 8:tm, :]

    @pl.when(i < n_tiles)
    def _():
        h = _rms_mod(x_ref[...], g_ref[...], sh_ref[...], sc_ref[...])
        p = jnp.dot(h.astype(BF16), win_ref[...], preferred_element_type=F32)
        bg_s[cur] = p[:, 0:d]
        y_s[cur] = p[:, d:2 * d] * p[:, 2 * d:3 * d]

    @pl.when(i >= 1)
    def _():
        t_prev = i - 1
        first = (t_prev % tiles_per_seq) == 0
        last = (t_prev % tiles_per_seq) == tiles_per_seq - 1
        y = y_s[prv]
        left = jnp.where(first, 0.0, tail_s[7:8, :])
        right = jnp.where(last, 0.0, y_s[cur, 0:1, :])
        ridx = lax.broadcasted_iota(jnp.int32, (tm, d), 0)
        y_dn = jnp.where(ridx == 0, left, pltpu.roll(y, 1, 0))
        y_up = jnp.where(ridx == tm - 1, right, pltpu.roll(y, tm - 1, 0))
        conv = y_dn * cw_ref[0:1, :] + y * cw_ref[1:2, :] + y_up * cw_ref[2:3, :]
        mix = jnp.dot((bg_s[prv] * conv).astype(BF16), wout_ref[...], preferred_element_type=F32)
        o_ref[...] = xp_ref[...] + gate_ref[...] * mix


def _odd_mix(lat, mods, layer, g, win_bf, conv_w8, wout_bf, seq):
    n, d = lat.shape
    tm = ODD_TM
    nt = n // tm
    tiles_per_seq = seq // tm
    cur = lambda i: jnp.minimum(i, nt - 1)
    prv = lambda i: jnp.maximum(i - 1, 0)
    row = lambda t: layer * MOD_ROWS + t // tiles_per_seq
    const = lambda shape: pl.BlockSpec(shape, lambda i: (0,) * len(shape))
    return pl.pallas_call(
        functools.partial(_odd_mix_kernel, tiles_per_seq=tiles_per_seq),
        grid=(nt + 1,),
        in_specs=[
            pl.BlockSpec((tm, d), lambda i: (cur(i), 0)),
            pl.BlockSpec((tm, d), lambda i: (prv(i), 0)),
            pl.BlockSpec((None, 1, d), lambda i: (row(cur(i)), 0, 0)),
            pl.BlockSpec((None, 1, d), lambda i: (row(cur(i)), 0, 1)),
            pl.BlockSpec((None, 1, d), lambda i: (row(prv(i)), 0, 2)),
            const((1, d)), const(win_bf.shape), const(conv_w8.shape), const(wout_bf.shape),
        ],
        out_specs=pl.BlockSpec((tm, d), lambda i: (prv(i), 0)),
        out_shape=jax.ShapeDtypeStruct((n, d), F32),
        scratch_shapes=[
            pltpu.VMEM((2, tm, d), F32),
            pltpu.VMEM((2, tm, d), F32),
            pltpu.VMEM((8, d), F32),
        ],
        compiler_params=_params(("arbitrary",)),
        name="odd_mix",
    )(lat, lat, mods, mods, mods, g, win_bf, conv_w8, wout_bf)


def _router_kernel(x_ref, sh_ref, sc_ref, g_ref, wr_ref, br_ref,
                   h_ref, mi_ref, wc_ref, cnt_ref, carry):
    i = pl.program_id(0)
    tm = x_ref.shape[0]
    epg = EXPERTS_PER_GROUP

    @pl.when(i == 0)
    def _():
        carry[...] = jnp.zeros_like(carry)

    h = _rms_mod(x_ref[...], g_ref[...], sh_ref[...], sc_ref[...])
    h_ref[...] = h
    lg = lax.dot_general(wr_ref[...], h, (((1,), (1,)), ((), ())),
                         preferred_element_type=F32, precision=HIGHEST) + br_ref[...]
    io8 = lax.broadcasted_iota(jnp.int32, (epg, tm), 0)
    gl = lg[0:epg]
    gmax = jnp.max(gl, axis=0, keepdims=True)
    g_idx = jnp.min(jnp.where(gl == gmax, io8, epg), axis=0, keepdims=True)
    g_w = 1.0 / jnp.sum(jnp.exp(gl - gmax), axis=0, keepdims=True)
    e_sel = lg[epg:2 * epg]
    for gi in range(1, N_GROUPS):
        e_sel = jnp.where(g_idx == gi, lg[(gi + 1) * epg:(gi + 2) * epg], e_sel)
    v0 = jnp.max(e_sel, axis=0, keepdims=True)
    i0 = jnp.min(jnp.where(e_sel == v0, io8, epg), axis=0, keepdims=True)
    rest = jnp.where(io8 == i0, -jnp.inf, e_sel)
    v1 = jnp.max(rest, axis=0, keepdims=True)
    i1 = jnp.min(jnp.where(rest == v1, io8, epg), axis=0, keepdims=True)
    t = jnp.exp(v1 - v0)
    w0 = g_w / (1.0 + t)
    w1 = g_w * t / (1.0 + t)
    e0 = g_idx * epg + i0
    e1 = g_idx * epg + i1

    io32 = lax.broadcasted_iota(jnp.int32, (N_EXPERTS, tm), 0)
    hit0 = io32 == e0
    hit1 = io32 == e1
    onehot = jnp.where(hit0 | hit1, 1.0, 0.0)
    r_i = lax.broadcasted_iota(jnp.int32, (tm, tm), 0)
    c_i = lax.broadcasted_iota(jnp.int32, (tm, tm), 1)
    upper = jnp.where(r_i < c_i, 1.0, 0.0).astype(BF16)
    cum = jnp.dot(onehot.astype(BF16), upper, preferred_element_type=F32) + carry[...]
    rank0 = jnp.sum(jnp.where(hit0, cum, 0.0), axis=0, keepdims=True).astype(jnp.int32)
    rank1 = jnp.sum(jnp.where(hit1, cum, 0.0), axis=0, keepdims=True).astype(jnp.int32)
    carry[...] = carry[...] + jnp.sum(onehot, axis=1, keepdims=True)
    cnt_ref[...] = jnp.broadcast_to(carry[...], cnt_ref.shape)

    mi_ref[...] = jnp.where(io8 == 0, e0, jnp.where(io8 == 1, e1, jnp.where(io8 == 2, rank0,
                            jnp.where(io8 == 3, rank1, 0))))
    io128 = lax.broadcasted_iota(jnp.int32, (LANES, tm), 0)
    wrow = jnp.where(io128 == 0, w0, jnp.where(io128 == 1, w1, 0.0))
    wc_ref[...] = wrow.T


def _router(lat, mods, layer, g, wr_t, br_t, seq):
    n, d = lat.shape
    tm = ROUTE_TM
    tiles_per_seq = seq // tm
    row = lambda i: layer * MOD_ROWS + i // tiles_per_seq
    const = lambda shape: pl.BlockSpec(shape, lambda i: (0,) * len(shape))
    return pl.pallas_call(
        _router_kernel,
        grid=(n // tm,),
        in_specs=[
            pl.BlockSpec((tm, d), lambda i: (i, 0)),
            pl.BlockSpec((None, 1, d), lambda i: (row(i), 0, 3)),
            pl.BlockSpec((None, 1, d), lambda i: (row(i), 0, 4)),
            const((1, d)), const(wr_t.shape), const(br_t.shape),
        ],
        out_specs=[
            pl.BlockSpec((tm, d), lambda i: (i, 0)),
            pl.BlockSpec((8, tm), lambda i: (0, i)),
            pl.BlockSpec((tm, LANES), lambda i: (i, 0)),
            const((N_EXPERTS, LANES)),
        ],
        out_shape=[
            jax.ShapeDtypeStruct((n, d), F32),
            jax.ShapeDtypeStruct((8, n), jnp.int32),
            jax.ShapeDtypeStruct((n, LANES), F32),
            jax.ShapeDtypeStruct((N_EXPERTS, LANES), F32),
        ],
        scratch_shapes=[pltpu.VMEM((N_EXPERTS, 1), F32)],
        compiler_params=_params(("arbitrary",)),
        name="router",
    )(lat, mods, mods, g, wr_t, br_t)


def _plan_kernel(cnt_ref, mi_ref, dest_ref, be_ref, nu_ref, ps_ref, *, n_blocks):
    bm = MOE_BM

    def per_expert(e, blk0):
        nb = (cnt_ref[e] + bm - 1) // bm
        ps_ref[e] = blk0 * bm

        def fill(b, c):
            be_ref[b] = e
            return c

        lax.fori_loop(blk0, blk0 + nb, fill, 0)
        return blk0 + nb

    n_used = lax.fori_loop(0, N_EXPERTS, per_expert, 0)
    nu_ref[0] = n_used
    last_e = be_ref[jnp.maximum(n_used - 1, 0)]

    def fill_tail(b, c):
        be_ref[b] = last_e
        return c

    lax.fori_loop(n_used, n_blocks, fill_tail, 0)

    e01 = mi_ref[0:2, :]
    dest = mi_ref[2:4, :]
    for e in range(N_EXPERTS):
        dest = dest + jnp.where(e01 == e, ps_ref[e], 0)
    dest_ref[...] = dest


def _plan(counts, meta_i, n_blocks):
    n = meta_i.shape[1]
    return pl.pallas_call(
        functools.partial(_plan_kernel, n_blocks=n_blocks),
        in_specs=[pl.BlockSpec(memory_space=pltpu.SMEM), pl.BlockSpec(memory_space=pltpu.VMEM)],
        out_specs=[pl.BlockSpec(memory_space=pltpu.VMEM), pl.BlockSpec(memory_space=pltpu.SMEM),
                   pl.BlockSpec(memory_space=pltpu.SMEM)],
        out_shape=[
            jax.ShapeDtypeStruct((2, n), jnp.int32),
            jax.ShapeDtypeStruct((n_blocks,), jnp.int32),
            jax.ShapeDtypeStruct((1,), jnp.int32),
        ],
        scratch_shapes=[pltpu.SMEM((N_EXPERTS,), jnp.int32)],
        compiler_params=pltpu.CompilerParams(vmem_limit_bytes=VMEM_LIMIT),
        name="plan",
    )(counts, meta_i)


def _dispatch_kernel(dest_ref, be_ref, nu_ref, h_hbm, xb_hbm, zbuf, sem, zsem, *, n_tok):
    i = pl.program_id(0)
    tm = DISPATCH_TM
    bm = MOE_BM
    base = i * tm
    n_blocks = be_ref.shape[0]

    @pl.when(i == 0)
    def _():
        zbuf[...] = jnp.zeros_like(zbuf)

        def zero_copy(b):
            return pltpu.make_async_copy(zbuf, xb_hbm.at[pl.ds(b * bm, bm)], zsem)

        def has_padding(b):
            return jnp.logical_or(b >= nu_ref[0] - 1, be_ref[b] != be_ref[jnp.minimum(b + 1, n_blocks - 1)])

        def fill(b, c):
            @pl.when(has_padding(b))
            def _():
                zero_copy(b).start()
            return c

        def drain_fill(b, c):
            @pl.when(has_padding(b))
            def _():
                zero_copy(b).wait()
            return c

        lax.fori_loop(0, n_blocks, fill, 0)
        lax.fori_loop(0, n_blocks, drain_fill, 0)

    def row_copy(src, dst):
        return pltpu.make_async_copy(h_hbm.at[pl.ds(src, 1)], xb_hbm.at[pl.ds(dst, 1)], sem)

    def issue(t, c):
        src = base + t
        row_copy(src, dest_ref[src]).start()
        row_copy(src, dest_ref[n_tok + src]).start()
        return c

    lax.fori_loop(0, tm, issue, 0)

    def drain():
        pltpu.make_async_copy(h_hbm.at[pl.ds(0, 2 * tm)], xb_hbm.at[pl.ds(0, 2 * tm)], sem).wait()

    @pl.when(i > 0)
    def _():
        drain()

    @pl.when(i == pl.num_programs(0) - 1)
    def _():
        drain()


def _dispatch(dest_flat, block_e, n_used, h2):
    n, d = h2.shape
    n_rows = block_e.shape[0] * MOE_BM
    return pl.pallas_call(
        functools.partial(_dispatch_kernel, n_tok=n),
        grid_spec=pltpu.PrefetchScalarGridSpec(
            num_scalar_prefetch=3,
            grid=(n // DISPATCH_TM,),
            in_specs=[pl.BlockSpec(memory_space=pl.ANY)],
            out_specs=pl.BlockSpec(memory_space=pl.ANY),
            scratch_shapes=[pltpu.VMEM((MOE_BM, d), F32), pltpu.SemaphoreType.DMA,
                            pltpu.SemaphoreType.DMA],
        ),
        out_shape=jax.ShapeDtypeStruct((n_rows, d), F32),
        compiler_params=_params(("arbitrary",)),
        name="dispatch",
    )(dest_flat, block_e, n_used, h2)


def _expert_kernel(be_ref, nu_ref, x_ref, wg_ref, wu_ref, wd_ref, y_ref, wgu_s, wd_s):
    b = pl.program_id(0)
    hid = wg_ref.shape[2]
    changed = jnp.logical_or(b == 0, be_ref[b] != be_ref[jnp.maximum(b - 1, 0)])

    @pl.when(changed)
    def _():
        wgu_s[:, 0:hid] = wg_ref[0].astype(BF16)
        wgu_s[:, hid:2 * hid] = wu_ref[0].astype(BF16)
        wd_s[...] = wd_ref[0].astype(BF16)

    @pl.when(b < nu_ref[0])
    def _():
        gu = jnp.dot(x_ref[...].astype(BF16), wgu_s[...], preferred_element_type=F32)
        gate = gu[:, 0:hid]
        act = gate * (1.0 / (1.0 + jnp.exp(-gate))) * gu[:, hid:2 * hid]
        y_ref[...] = jnp.dot(act.astype(BF16), wd_s[...], preferred_element_type=F32)

    @pl.when(b >= nu_ref[0])
    def _():
        y_ref[...] = jnp.zeros_like(y_ref)


def _experts(block_e, n_used, xb, w_gate, w_up, w_down):
    n_rows, d = xb.shape
    hid = w_gate.shape[2]
    bm = MOE_BM
    n_blocks = n_rows // bm
    blk = lambda b, be, nu: (jnp.minimum(b, nu[0] - 1), 0)
    wsel = lambda b, be, nu: (be[b], 0, 0)
    return pl.pallas_call(
        _expert_kernel,
        grid_spec=pltpu.PrefetchScalarGridSpec(
            num_scalar_prefetch=2,
            grid=(n_blocks,),
            in_specs=[
                pl.BlockSpec((bm, d), blk),
                pl.BlockSpec((1, d, hid), wsel),
                pl.BlockSpec((1, d, hid), wsel),
                pl.BlockSpec((1, hid, d), wsel),
            ],
            out_specs=pl.BlockSpec((bm, d), lambda b, be, nu: (b, 0)),
            scratch_shapes=[pltpu.VMEM((d, 2 * hid), BF16), pltpu.VMEM((hid, d), BF16)],
        ),
        out_shape=jax.ShapeDtypeStruct((n_rows, d), F32),
        compiler_params=_params(("arbitrary",)),
        name="experts",
    )(block_e, n_used, xb, w_gate, w_up, w_down)


def _combine_kernel(dest_ref, lat_ref, wc_ref, gate_ref, gf_ref, y_hbm, o_ref, ybuf, sems, *, n_tok, final):
    i = pl.program_id(0)
    nsteps = pl.num_programs(0)
    tm = lat_ref.shape[0]

    def row_copy(k, tok, t, slot):
        return pltpu.make_async_copy(y_hbm.at[pl.ds(dest_ref[k * n_tok + tok], 1)],
                                     ybuf.at[slot, k, pl.ds(t, 1)], sems.at[slot])

    def issue_tile(tile, slot):
        def body(t, c):
            tok = tile * tm + t
            row_copy(0, tok, t, slot).start()
            row_copy(1, tok, t, slot).start()
            return c

        lax.fori_loop(0, tm, body, 0)

    slot = i % 2

    @pl.when(i == 0)
    def _():
        issue_tile(0, 0)

    @pl.when(i + 1 < nsteps)
    def _():
        issue_tile(i + 1, 1 - slot)

    for k in range(2):
        pltpu.make_async_copy(y_hbm.at[pl.ds(0, tm)], ybuf.at[slot, k], sems.at[slot]).wait()

    wc = wc_ref[...]
    moe = wc[:, 0:1] * ybuf[slot, 0] + wc[:, 1:2] * ybuf[slot, 1]
    out = lat_ref[...] + gate_ref[...] * moe
    if final:
        ms = jnp.mean(out * out, axis=-1, keepdims=True)
        out = out * lax.rsqrt(ms + EPS) * gf_ref[...]
    o_ref[...] = out


def _combine(dest_flat, lat, wcol, mods, layer, g_final, yb, seq, final):
    n, d = lat.shape
    tm = COMBINE_TM
    tiles_per_seq = seq // tm
    row = lambda i, dref: layer * MOD_ROWS + i // tiles_per_seq
    return pl.pallas_call(
        functools.partial(_combine_kernel, n_tok=n, final=final),
        grid_spec=pltpu.PrefetchScalarGridSpec(
            num_scalar_prefetch=1,
            grid=(n // tm,),
            in_specs=[
                pl.BlockSpec((tm, d), lambda i, dref: (i, 0)),
                pl.BlockSpec((tm, LANES), lambda i, dref: (i, 0)),
                pl.BlockSpec((None, 1, d), lambda i, dref: (row(i, dref), 0, 5)),
                pl.BlockSpec((1, d), lambda i, dref: (0, 0)),
                pl.BlockSpec(memory_space=pl.ANY),
            ],
            out_specs=pl.BlockSpec((tm, d), lambda i, dref: (i, 0)),
            scratch_shapes=[pltpu.VMEM((2, 2, tm, d), F32), pltpu.SemaphoreType.DMA((2,))],
        ),
        out_shape=jax.ShapeDtypeStruct((n, d), F32),
        compiler_params=_params(("arbitrary",)),
        name="combine",
    )(dest_flat, lat, wcol, mods, g_final, yb)


def _moe(lat, mods, layer, g2, wr_t, br_t, w_gate, w_up, w_down, g_final, seq, final):
    n, d = lat.shape
    n_blocks = (2 * n) // MOE_BM + N_EXPERTS
    h2, meta_i, wcol, counts = _router(lat, mods, layer, g2, wr_t, br_t, seq)
    dest, block_e, n_used = _plan(counts[:, 0].astype(jnp.int32), meta_i, n_blocks)
    dest_flat = dest.reshape(2 * n)
    xb = _dispatch(dest_flat, block_e, n_used, h2)
    yb = _experts(block_e, n_used, xb, w_gate, w_up, w_down)
    return _combine(dest_flat, lat, wcol, mods, layer, g_final, yb, seq, final)


def _rope_tables(seq):
    quarter = HEAD_DIM // 4
    pos = jnp.arange(seq, dtype=F32)
    row_ids = jnp.floor(pos / GRID_W)
    col_ids = pos - row_ids * GRID_W
    inv = ROPE_BASE ** (-jnp.arange(quarter, dtype=F32) / quarter)
    ang_r = row_ids[:, None] * inv
    ang_c = col_ids[:, None] * inv
    zero = jnp.zeros_like(ang_r)
    cos = jnp.concatenate([jnp.cos(ang_r), jnp.cos(ang_r), jnp.cos(ang_c), jnp.cos(ang_c)], axis=-1)
    sa = jnp.concatenate([-jnp.sin(ang_r), zero, -jnp.sin(ang_c), zero], axis=-1)
    sb = jnp.concatenate([zero, jnp.sin(ang_r), zero, jnp.sin(ang_c)], axis=-1)
    rep = LANES // HEAD_DIM
    return tuple(jnp.tile(t, (1, rep)) for t in (cos, sa, sb))


def _router_weights(w_rg, b_rg, w_re, b_re):
    d = w_rg.shape[0]
    pad = EXPERTS_PER_GROUP - N_GROUPS
    wr_t = jnp.concatenate([w_rg.T, jnp.zeros((pad, d), F32), w_re.T], axis=0)
    br_t = jnp.concatenate([b_rg, jnp.full((pad,), NEG_INF, F32), b_re])[:, None]
    return wr_t, br_t


def kernel(x, c, ctx, c_ctx, w_ada, b_ada, g_norm1, g_norm2, g_final, w_in_even, attn_sink, g_sgu,
           w_spatial, b_spatial, w_out_even, w_in_odd, conv_w, w_out_odd, w_router_group,
           b_router_group, w_router_expert, b_router_expert, w_gate, w_up, w_down):
    b, s, d = x.shape
    n = b * s
    n_ctx = ctx.shape[1]
    depth = w_ada.shape[0]
    assert depth == 2 and b + 1 <= MOD_ROWS

    cond = jnp.concatenate([c, c_ctx[None, :], jnp.zeros((MOD_ROWS - b - 1, d), F32)], axis=0)
    mods = _ada(cond, w_ada, b_ada).reshape(depth * MOD_ROWS, 1, 6 * d)
    gf = g_final[None, :]

    lat = x.reshape(n, d)
    w_in_bf = w_in_even[0].astype(BF16)
    tabs = _rope_tables(s)
    q, kx, vx, u, z = _even_in(lat, mods, 0, lambda i: i // (s // EVEN_TM), g_norm1[0][None, :], w_in_bf,
                               tabs, s // EVEN_TM, EVEN_TM)
    ones = jnp.ones((n_ctx, LANES), F32)
    zeros = jnp.zeros((n_ctx, LANES), F32)
    _, kcx, vcx, _, _ = _even_in(ctx.reshape(b * n_ctx, d), mods, 0, lambda i: b, g_norm1[0][None, :],
                                 w_in_bf, (ones, zeros, zeros), 1, n_ctx)
    bsp_full = jnp.repeat(b_spatial[0].T, HEAD_DIM, axis=1)
    lat = _even_mix(lat, q, kx, vx, kcx, vcx, u, z, attn_sink[0], g_sgu[0][None, :],
                    w_spatial[0].astype(BF16), bsp_full, w_out_even[0].astype(BF16), mods, s, n_ctx)
    wr_t, br_t = _router_weights(w_router_group[0], b_router_group[0], w_router_expert[0], b_router_expert[0])
    lat = _moe(lat, mods, 0, g_norm2[0][None, :], wr_t, br_t, w_gate[0], w_up[0], w_down[0], gf, s, False)

    conv_w8 = jnp.concatenate([conv_w[0], jnp.zeros((8 - conv_w.shape[1], d), F32)], axis=0)
    lat = _odd_mix(lat, mods, 1, g_norm1[1][None, :], w_in_odd[0].astype(BF16), conv_w8,
                   w_out_odd[0].astype(BF16), s)
    wr_t, br_t = _router_weights(w_router_group[1], b_router_group[1], w_router_expert[1], b_router_expert[1])
    out = _moe(lat, mods, 1, g_norm2[1][None, :], wr_t, br_t, w_gate[1], w_up[1], w_down[1], gf, s, True)
    return out.reshape(b, s, d)
```

```python
import functools

import jax
import jax.numpy as jnp
from jax import lax
from jax.experimental import pallas as pl
from jax.experimental.pallas import tpu as pltpu

F32 = jnp.float32
BF16 = jnp.bfloat16
HIGHEST = lax.Precision.HIGHEST

GRID_W = 64
N_Q_HEADS = 8
N_KV_HEADS = 2
HEAD_DIM = 64
ATT_BLOCK = 128
ROPE_BASE = 10000.0
Q_DIM = N_Q_HEADS * HEAD_DIM
KV_DIM = N_KV_HEADS * HEAD_DIM
SG_GROUPS = 8
SG_WIDTH = SG_GROUPS * HEAD_DIM
N_GROUPS = 4
EXPERTS_PER_GROUP = 8
N_EXPERTS = N_GROUPS * EXPERTS_PER_GROUP
EPS = 1e-6
NEG_INF = -1e30

LANES = 128
MOD_ROWS = 8
VMEM_LIMIT = 56 * 1024 * 1024

ADA_TN = 1536
EVEN_TM = 512
ATT_TQ = 512
ODD_TM = 256
ROUTE_TM = 512
MOE_BM = 256
DISPATCH_TM = 256
COMBINE_TM = 256


def _params(sem):
    return pltpu.CompilerParams(dimension_semantics=sem, vmem_limit_bytes=VMEM_LIMIT)


def _rms_mod(x, g, shift, scale):
    ms = jnp.mean(x * x, axis=-1, keepdims=True)
    return (x * lax.rsqrt(ms + EPS) * g) * (1.0 + scale) + shift


def _ada_kernel(a_ref, w_ref, b_ref, o_ref):
    a = a_ref[...]
    s = a * (1.0 / (1.0 + jnp.exp(-a)))
    o_ref[0] = jnp.dot(s, w_ref[0], preferred_element_type=F32, precision=HIGHEST) + b_ref[0]


def _ada(cond, w_ada, b_ada):
    depth, d, six_d = w_ada.shape
    return pl.pallas_call(
        _ada_kernel,
        grid=(depth, six_d // ADA_TN),
        in_specs=[
            pl.BlockSpec((MOD_ROWS, d), lambda l, j: (0, 0)),
            pl.BlockSpec((1, d, ADA_TN), lambda l, j: (l, 0, j)),
            pl.BlockSpec((1, 1, ADA_TN), lambda l, j: (l, 0, j)),
        ],
        out_specs=pl.BlockSpec((1, MOD_ROWS, ADA_TN), lambda l, j: (l, 0, j)),
        out_shape=jax.ShapeDtypeStruct((depth, MOD_ROWS, six_d), F32),
        compiler_params=_params(("arbitrary", "arbitrary")),
        name="ada",
    )(cond, w_ada, b_ada.reshape(depth, 1, six_d))


def _even_in_kernel(x_ref, sh_ref, sc_ref, g_ref, w_ref, cos_ref, sa_ref, sb_ref,
                    q_ref, kx_ref, vx_ref, u_ref, z_ref):
    h = _rms_mod(x_ref[...], g_ref[...], sh_ref[...], sc_ref[...])
    p = jnp.dot(h.astype(BF16), w_ref[...], preferred_element_type=F32)
    cos, sa, sb = cos_ref[...], sa_ref[...], sb_ref[...]

    def rope(t):
        return t * cos + pltpu.roll(t, LANES - 16, 1) * sa + pltpu.roll(t, 16, 1) * sb

    scale = HEAD_DIM ** -0.5
    for cblk in range(Q_DIM // LANES):
        t = p[:, cblk * LANES:(cblk + 1) * LANES]
        q_ref[:, cblk * LANES:(cblk + 1) * LANES] = (rope(t) * scale).astype(BF16)

    low = lax.broadcasted_iota(jnp.int32, (x_ref.shape[0], LANES), 1) < HEAD_DIM

    def spread(t, o_ref):
        sw = pltpu.roll(t, HEAD_DIM, 1)
        zero = jnp.zeros_like(t)
        o_ref[:, 0 * LANES:1 * LANES] = jnp.where(low, t, zero).astype(BF16)
        o_ref[:, 1 * LANES:2 * LANES] = jnp.where(low, zero, sw).astype(BF16)
        o_ref[:, 2 * LANES:3 * LANES] = jnp.where(low, sw, zero).astype(BF16)
        o_ref[:, 3 * LANES:4 * LANES] = jnp.where(low, zero, t).astype(BF16)

    spread(rope(p[:, Q_DIM:Q_DIM + KV_DIM]), kx_ref)
    spread(p[:, Q_DIM + KV_DIM:Q_DIM + 2 * KV_DIM], vx_ref)
    u0 = Q_DIM + 2 * KV_DIM
    u_ref[...] = p[:, u0:u0 + SG_WIDTH]
    z_ref[...] = p[:, u0 + SG_WIDTH:u0 + 2 * SG_WIDTH]


def _even_in(x2d, mods, layer, mod_row_fn, g, w_bf, tabs, tab_blocks, tm):
    n, d = x2d.shape
    ein = w_bf.shape[1]
    cos, sa, sb = tabs
    row = lambda i: layer * MOD_ROWS + mod_row_fn(i)
    tab_spec = pl.BlockSpec((tm, LANES), lambda i: (i % tab_blocks, 0))
    wide = 4 * LANES
    return pl.pallas_call(
        _even_in_kernel,
        grid=(n // tm,),
        in_specs=[
            pl.BlockSpec((tm, d), lambda i: (i, 0)),
            pl.BlockSpec((None, 1, d), lambda i: (row(i), 0, 0)),
            pl.BlockSpec((None, 1, d), lambda i: (row(i), 0, 1)),
            pl.BlockSpec((1, d), lambda i: (0, 0)),
            pl.BlockSpec((d, ein), lambda i: (0, 0)),
            tab_spec, tab_spec, tab_spec,
        ],
        out_specs=[
            pl.BlockSpec((tm, Q_DIM), lambda i: (i, 0)),
            pl.BlockSpec((tm, wide), lambda i: (i, 0)),
            pl.BlockSpec((tm, wide), lambda i: (i, 0)),
            pl.BlockSpec((tm, SG_WIDTH), lambda i: (i, 0)),
            pl.BlockSpec((tm, SG_WIDTH), lambda i: (i, 0)),
        ],
        out_shape=[
            jax.ShapeDtypeStruct((n, Q_DIM), BF16),
            jax.ShapeDtypeStruct((n, wide), BF16),
            jax.ShapeDtypeStruct((n, wide), BF16),
            jax.ShapeDtypeStruct((n, SG_WIDTH), F32),
            jax.ShapeDtypeStruct((n, SG_WIDTH), F32),
        ],
        compiler_params=_params(("parallel",)),
        name="even_in",
    )(x2d, mods, mods, g, w_bf, cos, sa, sb)


def _gelu(x):
    return 0.5 * x * (1.0 + lax.erf(x * (2.0 ** -0.5)))


def _even_mix_kernel(sink_ref, lat_ref, q_ref, kxm_ref, kxp_ref, kxn_ref, vxm_ref, vxp_ref, vxn_ref,
                     kcx_ref, vcx_ref, u_ref, z_ref, gsgu_ref, wsp_ref, bsp_ref, wout_ref, gate_ref,
                     o_ref, kband, vband, mixin, *, tiles_per_seq):
    i = pl.program_id(0)
    tq = q_ref.shape[0]
    blk = ATT_BLOCK
    nsub = tq // blk
    first = (i % tiles_per_seq) == 0
    last = (i % tiles_per_seq) == tiles_per_seq - 1

    kband[0:blk] = kxp_ref[...]
    kband[blk:blk + tq] = kxm_ref[...]
    kband[blk + tq:] = kxn_ref[...]
    vband[0:blk] = vxp_ref[...]
    vband[blk:blk + tq] = vxm_ref[...]
    vband[blk + tq:] = vxn_ref[...]

    row = lax.broadcasted_iota(jnp.int32, (blk, 3 * blk), 0)
    col = lax.broadcasted_iota(jnp.int32, (blk, 3 * blk), 1)
    lane_low = lax.broadcasted_iota(jnp.int32, (blk, LANES), 1) < HEAD_DIM
    nt = (((1,), (1,)), ((), ()))

    def sub_block(j, carry):
        r0 = pl.multiple_of(j * blk, blk)
        ok_prev = jnp.logical_not(jnp.logical_and(first, j == 0))
        ok_next = jnp.logical_not(jnp.logical_and(last, j == nsub - 1))
        mask = (((col >= blk) | ((col >= row) & ok_prev))
                & ((col < 2 * blk) | ((col - 2 * blk <= row) & ok_next)))
        qj = q_ref[pl.ds(r0, blk), :]
        kb = kband[pl.ds(r0, 3 * blk), :]
        vb = vband[pl.ds(r0, 3 * blk), :]
        for pair in range(N_Q_HEADS // 2):
            acc = None
            for par in range(2):
                hd = 2 * pair + par
                var = 2 * (hd // (N_Q_HEADS // N_KV_HEADS)) + par
                qh = qj[:, pair * LANES:(pair + 1) * LANES]
                s_loc = lax.dot_general(qh, kb[:, var * LANES:(var + 1) * LANES], nt,
                                        preferred_element_type=F32)
                s_ctx = lax.dot_general(qh, kcx_ref[:, var * LANES:(var + 1) * LANES], nt,
                                        preferred_element_type=F32)
                s_loc = jnp.where(mask, s_loc, NEG_INF)
                sk = sink_ref[hd]
                m = jnp.maximum(jnp.maximum(jnp.max(s_loc, axis=-1, keepdims=True),
                                            jnp.max(s_ctx, axis=-1, keepdims=True)), sk)
                p_loc = jnp.exp(s_loc - m)
                p_ctx = jnp.exp(s_ctx - m)
                den = (jnp.sum(p_loc, axis=-1, keepdims=True) + jnp.sum(p_ctx, axis=-1, keepdims=True)
                       + jnp.exp(sk - m))
                o = (jnp.dot(p_ctx.astype(BF16), vcx_ref[:, var * LANES:(var + 1) * LANES],
                             preferred_element_type=F32)
                     + jnp.dot(p_loc.astype(BF16), vb[:, var * LANES:(var + 1) * LANES],
                               preferred_element_type=F32))
                o = o / den
                acc = o if acc is None else acc + o
            mixin[pl.ds(r0, blk), pair * LANES:(pair + 1) * LANES] = acc.astype(BF16)

        ug = _gelu(u_ref[pl.ds(r0, blk), :])
        zg = _gelu(z_ref[pl.ds(r0, blk), :])
        mu = jnp.mean(zg, axis=-1, keepdims=True)
        zc = zg - mu
        zn = zc * lax.rsqrt(jnp.mean(zc * zc, axis=-1, keepdims=True) + EPS) * gsgu_ref[...]
        for pair in range(SG_GROUPS // 2):
            zp = zn[:, pair * LANES:(pair + 1) * LANES]
            zero = jnp.zeros_like(zp)
            lo = jnp.where(lane_low, zp, zero).astype(BF16)
            hi = jnp.where(lane_low, zero, zp).astype(BF16)
            sg = (jnp.dot(wsp_ref[2 * pair], lo, preferred_element_type=F32)
                  + jnp.dot(wsp_ref[2 * pair + 1], hi, preferred_element_type=F32)
                  + bsp_ref[:, pair * LANES:(pair + 1) * LANES])
            mixin[pl.ds(r0, blk), Q_DIM + pair * LANES:Q_DIM + (pair + 1) * LANES] = (
                ug[:, pair * LANES:(pair + 1) * LANES] * sg).astype(BF16)
        return carry

    lax.fori_loop(0, nsub, sub_block, 0)
    mix = jnp.dot(mixin[...], wout_ref[...], preferred_element_type=F32)
    o_ref[...] = lat_ref[...] + gate_ref[...] * mix


def _even_mix(lat, q, kx, vx, kcx, vcx, u, z, sink, g_sgu, wsp_bf, bsp_full, wout_bf, mods, seq, ctx_len):
    n, d = lat.shape
    tq = ATT_TQ
    tiles_per_seq = seq // tq
    sub = tq // ATT_BLOCK
    nblk = n // ATT_BLOCK
    wide = 4 * LANES
    main = lambda w: pl.BlockSpec((tq, w), lambda i: (i, 0))
    prev = pl.BlockSpec((ATT_BLOCK, wide), lambda i: (jnp.maximum(i * sub - 1, 0), 0))
    nxt = pl.BlockSpec((ATT_BLOCK, wide), lambda i: (jnp.minimum((i + 1) * sub, nblk - 1), 0))
    ctxs = pl.BlockSpec((ctx_len, wide), lambda i: (i // tiles_per_seq, 0))
    const = lambda shape: pl.BlockSpec(shape, lambda i: (0,) * len(shape))
    return pl.pallas_call(
        functools.partial(_even_mix_kernel, tiles_per_seq=tiles_per_seq),
        grid=(n // tq,),
        in_specs=[
            pl.BlockSpec(memory_space=pltpu.SMEM),
            main(d), main(Q_DIM),
            main(wide), prev, nxt,
            main(wide), prev, nxt,
            ctxs, ctxs,
            main(SG_WIDTH), main(SG_WIDTH),
            const((1, SG_WIDTH)), const(wsp_bf.shape), const(bsp_full.shape), const(wout_bf.shape),
            pl.BlockSpec((None, 1, d), lambda i: (i // tiles_per_seq, 0, 2)),
        ],
        out_specs=pl.BlockSpec((tq, d), lambda i: (i, 0)),
        out_shape=jax.ShapeDtypeStruct((n, d), F32),
        scratch_shapes=[
            pltpu.VMEM((tq + 2 * ATT_BLOCK, wide), BF16),
            pltpu.VMEM((tq + 2 * ATT_BLOCK, wide), BF16),
            pltpu.VMEM((tq, Q_DIM + SG_WIDTH), BF16),
        ],
        compiler_params=_params(("parallel",)),
        name="even_mix",
    )(sink, lat, q, kx, kx, kx, vx, vx, vx, kcx, vcx, u, z, g_sgu, wsp_bf, bsp_full, wout_bf, mods)


def _odd_mix_kernel(x_ref, xp_ref, sh_ref, sc_ref, gate_ref, g_ref, win_ref, cw_ref, wout_ref,
                    o_ref, y_s, bg_s, tail_s, *, tiles_per_seq):
    i = pl.program_id(0)
    n_tiles = pl.num_programs(0) - 1
    tm, d = x_ref.shape
    cur = i % 2
    prv = 1 - cur

    tail_s[...] = y_s[cur, tm - 8:tm, :]

    @pl.when(i < n_tiles)
    def _():
        h = _rms_mod(x_ref[...], g_ref[...], sh_ref[...], sc_ref[...])
        p = jnp.dot(h.astype(BF16), win_ref[...], preferred_element_type=F32)
        bg_s[cur] = p[:, 0:d]
        y_s[cur] = p[:, d:2 * d] * p[:, 2 * d:3 * d]

    @pl.when(i >= 1)
    def _():
        t_prev = i - 1
        first = (t_prev % tiles_per_seq) == 0
        last = (t_prev % tiles_per_seq) == tiles_per_seq - 1
        y = y_s[prv]
        left = jnp.where(first, 0.0, tail_s[7:8, :])
        right = jnp.where(last, 0.0, y_s[cur, 0:1, :])
        ridx = lax.broadcasted_iota(jnp.int32, (tm, d), 0)
        y_dn = jnp.where(ridx == 0, left, pltpu.roll(y, 1, 0))
        y_up = jnp.where(ridx == tm - 1, right, pltpu.roll(y, tm - 1, 0))
        conv = y_dn * cw_ref[0:1, :] + y * cw_ref[1:2, :] + y_up * cw_ref[2:3, :]
        mix = jnp.dot((bg_s[prv] * conv).astype(BF16), wout_ref[...], preferred_element_type=F32)
        o_ref[...] = xp_ref[...] + gate_ref[...] * mix


def _odd_mix(lat, mods, layer, g, win_bf, conv_w8, wout_bf, seq):
    n, d = lat.shape
    tm = ODD_TM
    nt = n // tm
    tiles_per_seq = seq // tm
    cur = lambda i: jnp.minimum(i, nt - 1)
    prv = lambda i: jnp.maximum(i - 1, 0)
    row = lambda t: layer * MOD_ROWS + t // tiles_per_seq
    const = lambda shape: pl.BlockSpec(shape, lambda i: (0,) * len(shape))
    return pl.pallas_call(
        functools.partial(_odd_mix_kernel, tiles_per_seq=tiles_per_seq),
        grid=(nt + 1,),
        in_specs=[
            pl.BlockSpec((tm, d), lambda i: (cur(i), 0)),
            pl.BlockSpec((tm, d), lambda i: (prv(i), 0)),
            pl.BlockSpec((None, 1, d), lambda i: (row(cur(i)), 0, 0)),
            pl.BlockSpec((None, 1, d), lambda i: (row(cur(i)), 0, 1)),
            pl.BlockSpec((None, 1, d), lambda i: (row(prv(i)), 0, 2)),
            const((1, d)), const(win_bf.shape), const(conv_w8.shape), const(wout_bf.shape),
        ],
        out_specs=pl.BlockSpec((tm, d), lambda i: (prv(i), 0)),
        out_shape=jax.ShapeDtypeStruct((n, d), F32),
        scratch_shapes=[
            pltpu.VMEM((2, tm, d), F32),
            pltpu.VMEM((2, tm, d), F32),
            pltpu.VMEM((8, d), F32),
        ],
        compiler_params=_params(("arbitrary",)),
        name="odd_mix",
    )(lat, lat, mods, mods, mods, g, win_bf, conv_w8, wout_bf)


def _router_kernel(x_ref, sh_ref, sc_ref, g_ref, wr_ref, br_ref,
                   mi_ref, wc_ref, cnt_ref, carry):
    i = pl.program_id(0)
    tm = x_ref.shape[0]
    epg = EXPERTS_PER_GROUP

    @pl.when(i == 0)
    def _():
        carry[...] = jnp.zeros_like(carry)

    h = _rms_mod(x_ref[...], g_ref[...], sh_ref[...], sc_ref[...])
    lg = lax.dot_general(wr_ref[...], h, (((1,), (1,)), ((), ())),
                         preferred_element_type=F32, precision=HIGHEST) + br_ref[...]
    io8 = lax.broadcasted_iota(jnp.int32, (epg, tm), 0)
    gl = lg[0:epg]
    gmax = jnp.max(gl, axis=0, keepdims=True)
    g_idx = jnp.min(jnp.where(gl == gmax, io8, epg), axis=0, keepdims=True)
    g_w = 1.0 / jnp.sum(jnp.exp(gl - gmax), axis=0, keepdims=True)
    e_sel = lg[epg:2 * epg]
    for gi in range(1, N_GROUPS):
        e_sel = jnp.where(g_idx == gi, lg[(gi + 1) * epg:(gi + 2) * epg], e_sel)
    v0 = jnp.max(e_sel, axis=0, keepdims=True)
    i0 = jnp.min(jnp.where(e_sel == v0, io8, epg), axis=0, keepdims=True)
    rest = jnp.where(io8 == i0, -jnp.inf, e_sel)
    v1 = jnp.max(rest, axis=0, keepdims=True)
    i1 = jnp.min(jnp.where(rest == v1, io8, epg), axis=0, keepdims=True)
    t = jnp.exp(v1 - v0)
    w0 = g_w / (1.0 + t)
    w1 = g_w * t / (1.0 + t)
    e0 = g_idx * epg + i0
    e1 = g_idx * epg + i1

    io32 = lax.broadcasted_iota(jnp.int32, (N_EXPERTS, tm), 0)
    hit0 = io32 == e0
    hit1 = io32 == e1
    onehot = jnp.where(hit0 | hit1, 1.0, 0.0)
    r_i = lax.broadcasted_iota(jnp.int32, (tm, tm), 0)
    c_i = lax.broadcasted_iota(jnp.int32, (tm, tm), 1)
    upper = jnp.where(r_i < c_i, 1.0, 0.0).astype(BF16)
    cum = jnp.dot(onehot.astype(BF16), upper, preferred_element_type=F32) + carry[...]
    rank0 = jnp.sum(jnp.where(hit0, cum, 0.0), axis=0, keepdims=True).astype(jnp.int32)
    rank1 = jnp.sum(jnp.where(hit1, cum, 0.0), axis=0, keepdims=True).astype(jnp.int32)
    carry[...] = carry[...] + jnp.sum(onehot, axis=1, keepdims=True)
    cnt_ref[...] = jnp.broadcast_to(carry[...], cnt_ref.shape)

    mi_ref[...] = jnp.where(io8 == 0, e0, jnp.where(io8 == 1, e1, jnp.where(io8 == 2, rank0,
                            jnp.where(io8 == 3, rank1, 0))))
    io128 = lax.broadcasted_iota(jnp.int32, (LANES, tm), 0)
    wrow = jnp.where(io128 == 0, w0, jnp.where(io128 == 1, w1, 0.0))
    wc_ref[...] = wrow.T


def _router(lat, mods, layer, g, wr_t, br_t, seq):
    n, d = lat.shape
    tm = ROUTE_TM
    tiles_per_seq = seq // tm
    row = lambda i: layer * MOD_ROWS + i // tiles_per_seq
    const = lambda shape: pl.BlockSpec(shape, lambda i: (0,) * len(shape))
    return pl.pallas_call(
        _router_kernel,
        grid=(n // tm,),
        in_specs=[
            pl.BlockSpec((tm, d), lambda i: (i, 0)),
            pl.BlockSpec((None, 1, d), lambda i: (row(i), 0, 3)),
            pl.BlockSpec((None, 1, d), lambda i: (row(i), 0, 4)),
            const((1, d)), const(wr_t.shape), const(br_t.shape),
        ],
        out_specs=[
            pl.BlockSpec((8, tm), lambda i: (0, i)),
            pl.BlockSpec((tm, LANES), lambda i: (i, 0)),
            const((N_EXPERTS, LANES)),
        ],
        out_shape=[
            jax.ShapeDtypeStruct((8, n), jnp.int32),
            jax.ShapeDtypeStruct((n, LANES), F32),
            jax.ShapeDtypeStruct((N_EXPERTS, LANES), F32),
        ],
        scratch_shapes=[pltpu.VMEM((N_EXPERTS, 1), F32)],
        compiler_params=_params(("arbitrary",)),
        name="router",
    )(lat, mods, mods, g, wr_t, br_t)


def _plan_kernel(cnt_ref, mi_ref, dest_ref, be_ref, nu_ref, ps_ref, *, n_blocks):
    bm = MOE_BM

    def per_expert(e, blk0):
        nb = (cnt_ref[e] + bm - 1) // bm
        ps_ref[e] = blk0 * bm

        def fill(b, c):
            be_ref[b] = e
            return c

        lax.fori_loop(blk0, blk0 + nb, fill, 0)
        return blk0 + nb

    n_used = lax.fori_loop(0, N_EXPERTS, per_expert, 0)
    nu_ref[0] = n_used
    last_e = be_ref[jnp.maximum(n_used - 1, 0)]

    def fill_tail(b, c):
        be_ref[b] = last_e
        return c

    lax.fori_loop(n_used, n_blocks, fill_tail, 0)

    e01 = mi_ref[0:2, :]
    dest = mi_ref[2:4, :]
    for e in range(N_EXPERTS):
        dest = dest + jnp.where(e01 == e, ps_ref[e], 0)
    dest_ref[...] = dest


def _plan(counts, meta_i, n_blocks):
    n = meta_i.shape[1]
    return pl.pallas_call(
        functools.partial(_plan_kernel, n_blocks=n_blocks),
        in_specs=[pl.BlockSpec(memory_space=pltpu.SMEM), pl.BlockSpec(memory_space=pltpu.VMEM)],
        out_specs=[pl.BlockSpec(memory_space=pltpu.VMEM), pl.BlockSpec(memory_space=pltpu.SMEM),
                   pl.BlockSpec(memory_space=pltpu.SMEM)],
        out_shape=[
            jax.ShapeDtypeStruct((2, n), jnp.int32),
            jax.ShapeDtypeStruct((n_blocks,), jnp.int32),
            jax.ShapeDtypeStruct((1,), jnp.int32),
        ],
        scratch_shapes=[pltpu.SMEM((N_EXPERTS,), jnp.int32)],
        compiler_params=pltpu.CompilerParams(vmem_limit_bytes=VMEM_LIMIT),
        name="plan",
    )(counts, meta_i)


def _dispatch_kernel(dest_ref, be_ref, nu_ref, x_ref, sh_ref, sc_ref, g_ref, xb_hbm,
                     hbuf, zbuf, sems, zsem, *, n_tok):
    i = pl.program_id(0)
    tm = x_ref.shape[0]
    bm = MOE_BM
    base = i * tm
    n_blocks = be_ref.shape[0]
    slot = i % 2

    @pl.when(i == 0)
    def _():
        zbuf[...] = jnp.zeros_like(zbuf)

        def zero_copy(b):
            return pltpu.make_async_copy(zbuf, xb_hbm.at[pl.ds(b * bm, bm)], zsem)

        def has_padding(b):
            return jnp.logical_or(b >= nu_ref[0] - 1, be_ref[b] != be_ref[jnp.minimum(b + 1, n_blocks - 1)])

        def fill(b, c):
            @pl.when(has_padding(b))
            def _():
                zero_copy(b).start()
            return c

        def drain_fill(b, c):
            @pl.when(has_padding(b))
            def _():
                zero_copy(b).wait()
            return c

        lax.fori_loop(0, n_blocks, fill, 0)
        lax.fori_loop(0, n_blocks, drain_fill, 0)

    hbuf[slot] = _rms_mod(x_ref[...], g_ref[...], sh_ref[...], sc_ref[...])

    def row_copy(t, dst):
        return pltpu.make_async_copy(hbuf.at[slot, pl.ds(t, 1)], xb_hbm.at[pl.ds(dst, 1)], sems.at[slot])

    def issue(t, c):
        row_copy(t, dest_ref[base + t]).start()
        row_copy(t, dest_ref[n_tok + base + t]).start()
        return c

    lax.fori_loop(0, tm, issue, 0)

    def drain(s):
        for _ in range(2):
            pltpu.make_async_copy(hbuf.at[s], xb_hbm.at[pl.ds(0, tm)], sems.at[s]).wait()

    @pl.when(i > 0)
    def _():
        drain(1 - slot)

    @pl.when(i == pl.num_programs(0) - 1)
    def _():
        drain(slot)


def _dispatch(dest_flat, block_e, n_used, lat, mods, layer, g, seq):
    n, d = lat.shape
    tm = DISPATCH_TM
    n_rows = block_e.shape[0] * MOE_BM
    tiles_per_seq = seq // tm
    row = lambda i: layer * MOD_ROWS + i // tiles_per_seq
    return pl.pallas_call(
        functools.partial(_dispatch_kernel, n_tok=n),
        grid_spec=pltpu.PrefetchScalarGridSpec(
            num_scalar_prefetch=3,
            grid=(n // tm,),
            in_specs=[
                pl.BlockSpec((tm, d), lambda i, *_: (i, 0)),
                pl.BlockSpec((None, 1, d), lambda i, *_: (row(i), 0, 3)),
                pl.BlockSpec((None, 1, d), lambda i, *_: (row(i), 0, 4)),
                pl.BlockSpec((1, d), lambda i, *_: (0, 0)),
            ],
            out_specs=pl.BlockSpec(memory_space=pl.ANY),
            scratch_shapes=[pltpu.VMEM((2, tm, d), F32), pltpu.VMEM((MOE_BM, d), F32),
                            pltpu.SemaphoreType.DMA((2,)), pltpu.SemaphoreType.DMA],
        ),
        out_shape=jax.ShapeDtypeStruct((n_rows, d), F32),
        compiler_params=_params(("arbitrary",)),
        name="dispatch",
    )(dest_flat, block_e, n_used, lat, mods, mods, g)


def _expert_kernel(be_ref, nu_ref, x_ref, wg_ref, wu_ref, wd_ref, y_ref, wgu_s, wd_s):
    b = pl.program_id(0)
    hid = wg_ref.shape[2]
    changed = jnp.logical_or(b == 0, be_ref[b] != be_ref[jnp.maximum(b - 1, 0)])

    @pl.when(changed)
    def _():
        wgu_s[:, 0:hid] = wg_ref[0].astype(BF16)
        wgu_s[:, hid:2 * hid] = wu_ref[0].astype(BF16)
        wd_s[...] = wd_ref[0].astype(BF16)

    @pl.when(b < nu_ref[0])
    def _():
        gu = jnp.dot(x_ref[...].astype(BF16), wgu_s[...], preferred_element_type=F32)
        gate = gu[:, 0:hid]
        act = gate * (1.0 / (1.0 + jnp.exp(-gate))) * gu[:, hid:2 * hid]
        y_ref[...] = jnp.dot(act.astype(BF16), wd_s[...], preferred_element_type=F32)

    @pl.when(b >= nu_ref[0])
    def _():
        y_ref[...] = jnp.zeros_like(y_ref)


def _experts(block_e, n_used, xb, w_gate, w_up, w_down, layer):
    n_rows, d = xb.shape
    hid = w_gate.shape[3]
    bm = MOE_BM
    n_blocks = n_rows // bm
    blk = lambda b, be, nu: (jnp.minimum(b, nu[0] - 1), 0)
    wsel = lambda b, be, nu: (layer, be[b], 0, 0)
    return pl.pallas_call(
        _expert_kernel,
        grid_spec=pltpu.PrefetchScalarGridSpec(
            num_scalar_prefetch=2,
            grid=(n_blocks,),
            in_specs=[
                pl.BlockSpec((bm, d), blk),
                pl.BlockSpec((None, 1, d, hid), wsel),
                pl.BlockSpec((None, 1, d, hid), wsel),
                pl.BlockSpec((None, 1, hid, d), wsel),
            ],
            out_specs=pl.BlockSpec((bm, d), lambda b, be, nu: (b, 0)),
            scratch_shapes=[pltpu.VMEM((d, 2 * hid), BF16), pltpu.VMEM((hid, d), BF16)],
        ),
        out_shape=jax.ShapeDtypeStruct((n_rows, d), F32),
        compiler_params=_params(("arbitrary",)),
        name="experts",
    )(block_e, n_used, xb, w_gate, w_up, w_down)


def _combine_kernel(dest_ref, lat_ref, wc_ref, gate_ref, gf_ref, y_hbm, o_ref, ybuf, sems, *, n_tok, final):
    i = pl.program_id(0)
    nsteps = pl.num_programs(0)
    tm = lat_ref.shape[0]

    def row_copy(k, tok, t, slot):
        return pltpu.make_async_copy(y_hbm.at[pl.ds(dest_ref[k * n_tok + tok], 1)],
                                     ybuf.at[slot, k, pl.ds(t, 1)], sems.at[slot])

    def issue_tile(tile, slot):
        def body(t, c):
            tok = tile * tm + t
            row_copy(0, tok, t, slot).start()
            row_copy(1, tok, t, slot).start()
            return c

        lax.fori_loop(0, tm, body, 0)

    slot = i % 2

    @pl.when(i == 0)
    def _():
        issue_tile(0, 0)

    @pl.when(i + 1 < nsteps)
    def _():
        issue_tile(i + 1, 1 - slot)

    for k in range(2):
        pltpu.make_async_copy(y_hbm.at[pl.ds(0, tm)], ybuf.at[slot, k], sems.at[slot]).wait()

    wc = wc_ref[...]
    moe = wc[:, 0:1] * ybuf[slot, 0] + wc[:, 1:2] * ybuf[slot, 1]
    out = lat_ref[...] + gate_ref[...] * moe
    if final:
        ms = jnp.mean(out * out, axis=-1, keepdims=True)
        out = out * lax.rsqrt(ms + EPS) * gf_ref[...]
    o_ref[...] = out


def _combine(dest_flat, lat, wcol, mods, layer, g_final, yb, seq, final):
    n, d = lat.shape
    tm = COMBINE_TM
    tiles_per_seq = seq // tm
    row = lambda i, dref: layer * MOD_ROWS + i // tiles_per_seq
    return pl.pallas_call(
        functools.partial(_combine_kernel, n_tok=n, final=final),
        grid_spec=pltpu.PrefetchScalarGridSpec(
            num_scalar_prefetch=1,
            grid=(n // tm,),
            in_specs=[
                pl.BlockSpec((tm, d), lambda i, dref: (i, 0)),
                pl.BlockSpec((tm, LANES), lambda i, dref: (i, 0)),
                pl.BlockSpec((None, 1, d), lambda i, dref: (row(i, dref), 0, 5)),
                pl.BlockSpec((1, d), lambda i, dref: (0, 0)),
                pl.BlockSpec(memory_space=pl.ANY),
            ],
            out_specs=pl.BlockSpec((tm, d), lambda i, dref: (i, 0)),
            scratch_shapes=[pltpu.VMEM((2, 2, tm, d), F32), pltpu.SemaphoreType.DMA((2,))],
        ),
        out_shape=jax.ShapeDtypeStruct((n, d), F32),
        compiler_params=_params(("arbitrary",)),
        name="combine",
    )(dest_flat, lat, wcol, mods, g_final, yb)


def _moe(lat, mods, layer, g2, wr_t, br_t, w_gate, w_up, w_down, g_final, seq, final):
    n, d = lat.shape
    n_blocks = (2 * n) // MOE_BM + N_EXPERTS
    meta_i, wcol, counts = _router(lat, mods, layer, g2, wr_t, br_t, seq)
    dest, block_e, n_used = _plan(counts[:, 0].astype(jnp.int32), meta_i, n_blocks)
    dest_flat = dest.reshape(2 * n)
    xb = _dispatch(dest_flat, block_e, n_used, lat, mods, layer, g2, seq)
    yb = _experts(block_e, n_used, xb, w_gate, w_up, w_down, layer)
    return _combine(dest_flat, lat, wcol, mods, layer, g_final, yb, seq, final)


def _rope_tables(seq):
    quarter = HEAD_DIM // 4
    pos = jnp.arange(seq, dtype=F32)
    row_ids = jnp.floor(pos / GRID_W)
    col_ids = pos - row_ids * GRID_W
    inv = ROPE_BASE ** (-jnp.arange(quarter, dtype=F32) / quarter)
    ang_r = row_ids[:, None] * inv
    ang_c = col_ids[:, None] * inv
    zero = jnp.zeros_like(ang_r)
    cos = jnp.concatenate([jnp.cos(ang_r), jnp.cos(ang_r), jnp.cos(ang_c), jnp.cos(ang_c)], axis=-1)
    sa = jnp.concatenate([-jnp.sin(ang_r), zero, -jnp.sin(ang_c), zero], axis=-1)
    sb = jnp.concatenate([zero, jnp.sin(ang_r), zero, jnp.sin(ang_c)], axis=-1)
    rep = LANES // HEAD_DIM
    return tuple(jnp.tile(t, (1, rep)) for t in (cos, sa, sb))


def _router_weights(w_rg, b_rg, w_re, b_re):
    d = w_rg.shape[0]
    pad = EXPERTS_PER_GROUP - N_GROUPS
    wr_t = jnp.concatenate([w_rg.T, jnp.zeros((pad, d), F32), w_re.T], axis=0)
    br_t = jnp.concatenate([b_rg, jnp.full((pad,), NEG_INF, F32), b_re])[:, None]
    return wr_t, br_t


def kernel(x, c, ctx, c_ctx, w_ada, b_ada, g_norm1, g_norm2, g_final, w_in_even, attn_sink, g_sgu,
           w_spatial, b_spatial, w_out_even, w_in_odd, conv_w, w_out_odd, w_router_group,
           b_router_group, w_router_expert, b_router_expert, w_gate, w_up, w_down):
    b, s, d = x.shape
    n = b * s
    n_ctx = ctx.shape[1]
    depth = w_ada.shape[0]
    assert depth == 2 and b + 1 <= MOD_ROWS

    cond = jnp.concatenate([c, c_ctx[None, :], jnp.zeros((MOD_ROWS - b - 1, d), F32)], axis=0)
    mods = _ada(cond, w_ada, b_ada).reshape(depth * MOD_ROWS, 1, 6 * d)
    gf = g_final[None, :]

    lat = x.reshape(n, d)
    w_in_bf = w_in_even[0].astype(BF16)
    tabs = _rope_tables(s)
    q, kx, vx, u, z = _even_in(lat, mods, 0, lambda i: i // (s // EVEN_TM), g_norm1[0][None, :], w_in_bf,
                               tabs, s // EVEN_TM, EVEN_TM)
    ones = jnp.ones((n_ctx, LANES), F32)
    zeros = jnp.zeros((n_ctx, LANES), F32)
    _, kcx, vcx, _, _ = _even_in(ctx.reshape(b * n_ctx, d), mods, 0, lambda i: b, g_norm1[0][None, :],
                                 w_in_bf, (ones, zeros, zeros), 1, n_ctx)
    bsp_full = jnp.repeat(b_spatial[0].T, HEAD_DIM, axis=1)
    lat = _even_mix(lat, q, kx, vx, kcx, vcx, u, z, attn_sink[0], g_sgu[0][None, :],
                    w_spatial[0].astype(BF16), bsp_full, w_out_even[0].astype(BF16), mods, s, n_ctx)
    wr_t, br_t = _router_weights(w_router_group[0], b_router_group[0], w_router_expert[0], b_router_expert[0])
    lat = _moe(lat, mods, 0, g_norm2[0][None, :], wr_t, br_t, w_gate, w_up, w_down, gf, s, False)

    conv_w8 = jnp.concatenate([conv_w[0], jnp.zeros((8 - conv_w.shape[1], d), F32)], axis=0)
    lat = _odd_mix(lat, mods, 1, g_norm1[1][None, :], w_in_odd[0].astype(BF16), conv_w8,
                   w_out_odd[0].astype(BF16), s)
    wr_t, br_t = _router_weights(w_router_group[1], b_router_group[1], w_router_expert[1], b_router_expert[1])
    out = _moe(lat, mods, 1, g_norm2[1][None, :], wr_t, br_t, w_gate, w_up, w_down, gf, s, True)
    return out.reshape(b, s, d)
```

```python
import functools

import jax
import jax.numpy as jnp
from jax import lax
from jax.experimental import pallas as pl
from jax.experimental.pallas import tpu as pltpu
from jax.experimental.pallas import tpu_sc as plsc

F32 = jnp.float32
BF16 = jnp.bfloat16
HIGHEST = lax.Precision.HIGHEST

GRID_W = 64
N_Q_HEADS = 8
N_KV_HEADS = 2
HEAD_DIM = 64
ATT_BLOCK = 128
ROPE_BASE = 10000.0
Q_DIM = N_Q_HEADS * HEAD_DIM
KV_DIM = N_KV_HEADS * HEAD_DIM
SG_GROUPS = 8
SG_WIDTH = SG_GROUPS * HEAD_DIM
N_GROUPS = 4
EXPERTS_PER_GROUP = 8
N_EXPERTS = N_GROUPS * EXPERTS_PER_GROUP
EPS = 1e-6
NEG_INF = -1e30

LANES = 128
SC_CORES = 2
SC_SUBCORES = 16
SC_WORKERS = SC_CORES * SC_SUBCORES
SC_CHUNK = 32
MOD_ROWS = 8
VMEM_LIMIT = 56 * 1024 * 1024

ADA_TN = 1536
EVEN_TM = 512
ATT_TQ = 512
ODD_TM = 256
ROUTE_TM = 512
MOE_BM = 256
COMBINE_TM = 512


def _params(sem):
    return pltpu.CompilerParams(dimension_semantics=sem, vmem_limit_bytes=VMEM_LIMIT)


def _rms_mod(x, g, shift, scale):
    ms = jnp.mean(x * x, axis=-1, keepdims=True)
    return (x * lax.rsqrt(ms + EPS) * g) * (1.0 + scale) + shift


def _ada_kernel(a_ref, w_ref, b_ref, o_ref):
    a = a_ref[...]
    s = a * (1.0 / (1.0 + jnp.exp(-a)))
    o_ref[0] = jnp.dot(s, w_ref[0], preferred_element_type=F32, precision=HIGHEST) + b_ref[0]


def _ada(cond, w_ada, b_ada):
    depth, d, six_d = w_ada.shape
    return pl.pallas_call(
        _ada_kernel,
        grid=(depth, six_d // ADA_TN),
        in_specs=[
            pl.BlockSpec((MOD_ROWS, d), lambda l, j: (0, 0)),
            pl.BlockSpec((1, d, ADA_TN), lambda l, j: (l, 0, j)),
            pl.BlockSpec((1, 1, ADA_TN), lambda l, j: (l, 0, j)),
        ],
        out_specs=pl.BlockSpec((1, MOD_ROWS, ADA_TN), lambda l, j: (l, 0, j)),
        out_shape=jax.ShapeDtypeStruct((depth, MOD_ROWS, six_d), F32),
        compiler_params=_params(("arbitrary", "arbitrary")),
        name="ada",
    )(cond, w_ada, b_ada.reshape(depth, 1, six_d))


def _even_in_kernel(x_ref, sh_ref, sc_ref, g_ref, w_ref, cos_ref, sa_ref, sb_ref,
                    q_ref, kx_ref, vx_ref, u_ref, z_ref):
    h = _rms_mod(x_ref[...], g_ref[...], sh_ref[...], sc_ref[...])
    p = jnp.dot(h.astype(BF16), w_ref[...], preferred_element_type=F32)
    cos, sa, sb = cos_ref[...], sa_ref[...], sb_ref[...]

    def rope(t):
        return t * cos + pltpu.roll(t, LANES - 16, 1) * sa + pltpu.roll(t, 16, 1) * sb

    scale = HEAD_DIM ** -0.5
    for cblk in range(Q_DIM // LANES):
        t = p[:, cblk * LANES:(cblk + 1) * LANES]
        q_ref[:, cblk * LANES:(cblk + 1) * LANES] = (rope(t) * scale).astype(BF16)

    low = lax.broadcasted_iota(jnp.int32, (x_ref.shape[0], LANES), 1) < HEAD_DIM

    def spread(t, o_ref):
        sw = pltpu.roll(t, HEAD_DIM, 1)
        zero = jnp.zeros_like(t)
        o_ref[:, 0 * LANES:1 * LANES] = jnp.where(low, t, zero).astype(BF16)
        o_ref[:, 1 * LANES:2 * LANES] = jnp.where(low, zero, sw).astype(BF16)
        o_ref[:, 2 * LANES:3 * LANES] = jnp.where(low, sw, zero).astype(BF16)
        o_ref[:, 3 * LANES:4 * LANES] = jnp.where(low, zero, t).astype(BF16)

    spread(rope(p[:, Q_DIM:Q_DIM + KV_DIM]), kx_ref)
    spread(p[:, Q_DIM + KV_DIM:Q_DIM + 2 * KV_DIM], vx_ref)
    u0 = Q_DIM + 2 * KV_DIM
    u_ref[...] = p[:, u0:u0 + SG_WIDTH]
    z_ref[...] = p[:, u0 + SG_WIDTH:u0 + 2 * SG_WIDTH]


def _even_in(x2d, mods, layer, mod_row_fn, g, w_bf, tabs, tab_blocks, tm):
    n, d = x2d.shape
    ein = w_bf.shape[1]
    cos, sa, sb = tabs
    row = lambda i: layer * MOD_ROWS + mod_row_fn(i)
    tab_spec = pl.BlockSpec((tm, LANES), lambda i: (i % tab_blocks, 0))
    wide = 4 * LANES
    return pl.pallas_call(
        _even_in_kernel,
        grid=(n // tm,),
        in_specs=[
            pl.BlockSpec((tm, d), lambda i: (i, 0)),
            pl.BlockSpec((None, 1, d), lambda i: (row(i), 0, 0)),
            pl.BlockSpec((None, 1, d), lambda i: (row(i), 0, 1)),
            pl.BlockSpec((1, d), lambda i: (0, 0)),
            pl.BlockSpec((d, ein), lambda i: (0, 0)),
            tab_spec, tab_spec, tab_spec,
        ],
        out_specs=[
            pl.BlockSpec((tm, Q_DIM), lambda i: (i, 0)),
            pl.BlockSpec((tm, wide), lambda i: (i, 0)),
            pl.BlockSpec((tm, wide), lambda i: (i, 0)),
            pl.BlockSpec((tm, SG_WIDTH), lambda i: (i, 0)),
            pl.BlockSpec((tm, SG_WIDTH), lambda i: (i, 0)),
        ],
        out_shape=[
            jax.ShapeDtypeStruct((n, Q_DIM), BF16),
            jax.ShapeDtypeStruct((n, wide), BF16),
            jax.ShapeDtypeStruct((n, wide), BF16),
            jax.ShapeDtypeStruct((n, SG_WIDTH), F32),
            jax.ShapeDtypeStruct((n, SG_WIDTH), F32),
        ],
        compiler_params=_params(("parallel",)),
        name="even_in",
    )(x2d, mods, mods, g, w_bf, cos, sa, sb)


def _gelu(x):
    return 0.5 * x * (1.0 + lax.erf(x * (2.0 ** -0.5)))


def _even_mix_kernel(sink_ref, lat_ref, q_ref, kxm_ref, kxp_ref, kxn_ref, vxm_ref, vxp_ref, vxn_ref,
                     kcx_ref, vcx_ref, u_ref, z_ref, gsgu_ref, wsp_ref, bsp_ref, wout_ref, gate_ref,
                     o_ref, kband, vband, mixin, *, tiles_per_seq):
    i = pl.program_id(0)
    tq = q_ref.shape[0]
    blk = ATT_BLOCK
    nsub = tq // blk
    first = (i % tiles_per_seq) == 0
    last = (i % tiles_per_seq) == tiles_per_seq - 1

    kband[0:blk] = kxp_ref[...]
    kband[blk:blk + tq] = kxm_ref[...]
    kband[blk + tq:] = kxn_ref[...]
    vband[0:blk] = vxp_ref[...]
    vband[blk:blk + tq] = vxm_ref[...]
    vband[blk + tq:] = vxn_ref[...]

    row = lax.broadcasted_iota(jnp.int32, (blk, 3 * blk), 0)
    col = lax.broadcasted_iota(jnp.int32, (blk, 3 * blk), 1)
    lane_low = lax.broadcasted_iota(jnp.int32, (blk, LANES), 1) < HEAD_DIM
    nt = (((1,), (1,)), ((), ()))

    def sub_block(j, carry):
        r0 = pl.multiple_of(j * blk, blk)
        ok_prev = jnp.logical_not(jnp.logical_and(first, j == 0))
        ok_next = jnp.logical_not(jnp.logical_and(last, j == nsub - 1))
        mask = (((col >= blk) | ((col >= row) & ok_prev))
                & ((col < 2 * blk) | ((col - 2 * blk <= row) & ok_next)))
        qj = q_ref[pl.ds(r0, blk), :]
        kb = kband[pl.ds(r0, 3 * blk), :]
        vb = vband[pl.ds(r0, 3 * blk), :]
        for pair in range(N_Q_HEADS // 2):
            acc = None
            for par in range(2):
                hd = 2 * pair + par
                var = 2 * (hd // (N_Q_HEADS // N_KV_HEADS)) + par
                qh = qj[:, pair * LANES:(pair + 1) * LANES]
                s_loc = lax.dot_general(qh, kb[:, var * LANES:(var + 1) * LANES], nt,
                                        preferred_element_type=F32)
                s_ctx = lax.dot_general(qh, kcx_ref[:, var * LANES:(var + 1) * LANES], nt,
                                        preferred_element_type=F32)
                s_loc = jnp.where(mask, s_loc, NEG_INF)
                sk = sink_ref[hd]
                m = jnp.maximum(jnp.maximum(jnp.max(s_loc, axis=-1, keepdims=True),
                                            jnp.max(s_ctx, axis=-1, keepdims=True)), sk)
                p_loc = jnp.exp(s_loc - m)
                p_ctx = jnp.exp(s_ctx - m)
                den = (jnp.sum(p_loc, axis=-1, keepdims=True) + jnp.sum(p_ctx, axis=-1, keepdims=True)
                       + jnp.exp(sk - m))
                o = (jnp.dot(p_ctx.astype(BF16), vcx_ref[:, var * LANES:(var + 1) * LANES],
                             preferred_element_type=F32)
                     + jnp.dot(p_loc.astype(BF16), vb[:, var * LANES:(var + 1) * LANES],
                               preferred_element_type=F32))
                o = o / den
                acc = o if acc is None else acc + o
            mixin[pl.ds(r0, blk), pair * LANES:(pair + 1) * LANES] = acc.astype(BF16)

        ug = _gelu(u_ref[pl.ds(r0, blk), :])
        zg = _gelu(z_ref[pl.ds(r0, blk), :])
        mu = jnp.mean(zg, axis=-1, keepdims=True)
        zc = zg - mu
        zn = zc * lax.rsqrt(jnp.mean(zc * zc, axis=-1, keepdims=True) + EPS) * gsgu_ref[...]
        for pair in range(SG_GROUPS // 2):
            zp = zn[:, pair * LANES:(pair + 1) * LANES]
            zero = jnp.zeros_like(zp)
            lo = jnp.where(lane_low, zp, zero).astype(BF16)
            hi = jnp.where(lane_low, zero, zp).astype(BF16)
            sg = (jnp.dot(wsp_ref[2 * pair], lo, preferred_element_type=F32)
                  + jnp.dot(wsp_ref[2 * pair + 1], hi, preferred_element_type=F32)
                  + bsp_ref[:, pair * LANES:(pair + 1) * LANES])
            mixin[pl.ds(r0, blk), Q_DIM + pair * LANES:Q_DIM + (pair + 1) * LANES] = (
                ug[:, pair * LANES:(pair + 1) * LANES] * sg).astype(BF16)
        return carry

    lax.fori_loop(0, nsub, sub_block, 0)
    mix = jnp.dot(mixin[...], wout_ref[...], preferred_element_type=F32)
    o_ref[...] = lat_ref[...] + gate_ref[...] * mix


def _even_mix(lat, q, kx, vx, kcx, vcx, u, z, sink, g_sgu, wsp_bf, bsp_full, wout_bf, mods, seq, ctx_len):
    n, d = lat.shape
    tq = ATT_TQ
    tiles_per_seq = seq // tq
    sub = tq // ATT_BLOCK
    nblk = n // ATT_BLOCK
    wide = 4 * LANES
    main = lambda w: pl.BlockSpec((tq, w), lambda i: (i, 0))
    prev = pl.BlockSpec((ATT_BLOCK, wide), lambda i: (jnp.maximum(i * sub - 1, 0), 0))
    nxt = pl.BlockSpec((ATT_BLOCK, wide), lambda i: (jnp.minimum((i + 1) * sub, nblk - 1), 0))
    ctxs = pl.BlockSpec((ctx_len, wide), lambda i: (i // tiles_per_seq, 0))
    const = lambda shape: pl.BlockSpec(shape, lambda i: (0,) * len(shape))
    return pl.pallas_call(
        functools.partial(_even_mix_kernel, tiles_per_seq=tiles_per_seq),
        grid=(n // tq,),
        in_specs=[
            pl.BlockSpec(memory_space=pltpu.SMEM),
            main(d), main(Q_DIM),
            main(wide), prev, nxt,
            main(wide), prev, nxt,
            ctxs, ctxs,
            main(SG_WIDTH), main(SG_WIDTH),
            const((1, SG_WIDTH)), const(wsp_bf.shape), const(bsp_full.shape), const(wout_bf.shape),
            pl.BlockSpec((None, 1, d), lambda i: (i // tiles_per_seq, 0, 2)),
        ],
        out_specs=pl.BlockSpec((tq, d), lambda i: (i, 0)),
        out_shape=jax.ShapeDtypeStruct((n, d), F32),
        scratch_shapes=[
            pltpu.VMEM((tq + 2 * ATT_BLOCK, wide), BF16),
            pltpu.VMEM((tq + 2 * ATT_BLOCK, wide), BF16),
            pltpu.VMEM((tq, Q_DIM + SG_WIDTH), BF16),
        ],
        compiler_params=_params(("parallel",)),
        name="even_mix",
    )(sink, lat, q, kx, kx, kx, vx, vx, vx, kcx, vcx, u, z, g_sgu, wsp_bf, bsp_full, wout_bf, mods)


def _odd_mix_kernel(x_ref, xp_ref, sh_ref, sc_ref, gate_ref, g_ref, win_ref, cw_ref, wout_ref,
                    o_ref, y_s, bg_s, tail_s, *, tiles_per_seq):
    i = pl.program_id(0)
    n_tiles = pl.num_programs(0) - 1
    tm, d = x_ref.shape
    cur = i % 2
    prv = 1 - cur

    tail_s[...] = y_s[cur, tm - 8:tm, :]

    @pl.when(i < n_tiles)
    def _():
        h = _rms_mod(x_ref[...], g_ref[...], sh_ref[...], sc_ref[...])
        p = jnp.dot(h.astype(BF16), win_ref[...], preferred_element_type=F32)
        bg_s[cur] = p[:, 0:d]
        y_s[cur] = p[:, d:2 * d] * p[:, 2 * d:3 * d]

    @pl.when(i >= 1)
    def _():
        t_prev = i - 1
        first = (t_prev % tiles_per_seq) == 0
        last = (t_prev % tiles_per_seq) == tiles_per_seq - 1
        y = y_s[prv]
        left = jnp.where(first, 0.0, tail_s[7:8, :])
        right = jnp.where(last, 0.0, y_s[cur, 0:1, :])
        ridx = lax.broadcasted_iota(jnp.int32, (tm, d), 0)
        y_dn = jnp.where(ridx == 0, left, pltpu.roll(y, 1, 0))
        y_up = jnp.where(ridx == tm - 1, right, pltpu.roll(y, tm - 1, 0))
        conv = y_dn * cw_ref[0:1, :] + y * cw_ref[1:2, :] + y_up * cw_ref[2:3, :]
        mix = jnp.dot((bg_s[prv] * conv).astype(BF16), wout_ref[...], preferred_element_type=F32)
        o_ref[...] = xp_ref[...] + gate_ref[...] * mix


def _odd_mix(lat, mods, layer, g, win_bf, conv_w8, wout_bf, seq):
    n, d = lat.shape
    tm = ODD_TM
    nt = n // tm
    tiles_per_seq = seq // tm
    cur = lambda i: jnp.minimum(i, nt - 1)
    prv = lambda i: jnp.maximum(i - 1, 0)
    row = lambda t: layer * MOD_ROWS + t // tiles_per_seq
    const = lambda shape: pl.BlockSpec(shape, lambda i: (0,) * len(shape))
    return pl.pallas_call(
        functools.partial(_odd_mix_kernel, tiles_per_seq=tiles_per_seq),
        grid=(nt + 1,),
        in_specs=[
            pl.BlockSpec((tm, d), lambda i: (cur(i), 0)),
            pl.BlockSpec((tm, d), lambda i: (prv(i), 0)),
            pl.BlockSpec((None, 1, d), lambda i: (row(cur(i)), 0, 0)),
            pl.BlockSpec((None, 1, d), lambda i: (row(cur(i)), 0, 1)),
            pl.BlockSpec((None, 1, d), lambda i: (row(prv(i)), 0, 2)),
            const((1, d)), const(win_bf.shape), const(conv_w8.shape), const(wout_bf.shape),
        ],
        out_specs=pl.BlockSpec((tm, d), lambda i: (prv(i), 0)),
        out_shape=jax.ShapeDtypeStruct((n, d), F32),
        scratch_shapes=[
            pltpu.VMEM((2, tm, d), F32),
            pltpu.VMEM((2, tm, d), F32),
            pltpu.VMEM((8, d), F32),
        ],
        compiler_params=_params(("arbitrary",)),
        name="odd_mix",
    )(lat, lat, mods, mods, mods, g, win_bf, conv_w8, wout_bf)


def _router_kernel(x_ref, sh_ref, sc_ref, g_ref, wr_ref, br_ref,
                   h_ref, mi_ref, wc_ref, cnt_ref, carry):
    i = pl.program_id(0)
    tm = x_ref.shape[0]
    epg = EXPERTS_PER_GROUP

    @pl.when(i == 0)
    def _():
        carry[...] = jnp.zeros_like(carry)

    h = _rms_mod(x_ref[...], g_ref[...], sh_ref[...], sc_ref[...])
    h_ref[...] = h
    lg = lax.dot_general(wr_ref[...], h, (((1,), (1,)), ((), ())),
                         preferred_element_type=F32, precision=HIGHEST) + br_ref[...]
    io8 = lax.broadcasted_iota(jnp.int32, (epg, tm), 0)
    gl = lg[0:epg]
    gmax = jnp.max(gl, axis=0, keepdims=True)
    g_idx = jnp.min(jnp.where(gl == gmax, io8, epg), axis=0, keepdims=True)
    g_w = 1.0 / jnp.sum(jnp.exp(gl - gmax), axis=0, keepdims=True)
    e_sel = lg[epg:2 * epg]
    for gi in range(1, N_GROUPS):
        e_sel = jnp.where(g_idx == gi, lg[(gi + 1) * epg:(gi + 2) * epg], e_sel)
    v0 = jnp.max(e_sel, axis=0, keepdims=True)
    i0 = jnp.min(jnp.where(e_sel == v0, io8, epg), axis=0, keepdims=True)
    rest = jnp.where(io8 == i0, -jnp.inf, e_sel)
    v1 = jnp.max(rest, axis=0, keepdims=True)
    i1 = jnp.min(jnp.where(rest == v1, io8, epg), axis=0, keepdims=True)
    t = jnp.exp(v1 - v0)
    w0 = g_w / (1.0 + t)
    w1 = g_w * t / (1.0 + t)
    e0 = g_idx * epg + i0
    e1 = g_idx * epg + i1

    io32 = lax.broadcasted_iota(jnp.int32, (N_EXPERTS, tm), 0)
    hit0 = io32 == e0
    hit1 = io32 == e1
    onehot = jnp.where(hit0 | hit1, 1.0, 0.0)
    r_i = lax.broadcasted_iota(jnp.int32, (tm, tm), 0)
    c_i = lax.broadcasted_iota(jnp.int32, (tm, tm), 1)
    upper = jnp.where(r_i < c_i, 1.0, 0.0).astype(BF16)
    cum = jnp.dot(onehot.astype(BF16), upper, preferred_element_type=F32) + carry[...]
    rank0 = jnp.sum(jnp.where(hit0, cum, 0.0), axis=0, keepdims=True).astype(jnp.int32)
    rank1 = jnp.sum(jnp.where(hit1, cum, 0.0), axis=0, keepdims=True).astype(jnp.int32)
    carry[...] = carry[...] + jnp.sum(onehot, axis=1, keepdims=True)
    cnt_ref[...] = jnp.broadcast_to(carry[...], cnt_ref.shape)

    mi_ref[...] = jnp.where(io8 == 0, e0, jnp.where(io8 == 1, e1, jnp.where(io8 == 2, rank0,
                            jnp.where(io8 == 3, rank1, 0))))
    io128 = lax.broadcasted_iota(jnp.int32, (LANES, tm), 0)
    wrow = jnp.where(io128 == 0, w0, jnp.where(io128 == 1, w1, 0.0))
    wc_ref[...] = wrow.T


def _router(lat, mods, layer, g, wr_t, br_t, seq):
    n, d = lat.shape
    tm = ROUTE_TM
    tiles_per_seq = seq // tm
    row = lambda i: layer * MOD_ROWS + i // tiles_per_seq
    const = lambda shape: pl.BlockSpec(shape, lambda i: (0,) * len(shape))
    return pl.pallas_call(
        _router_kernel,
        grid=(n // tm,),
        in_specs=[
            pl.BlockSpec((tm, d), lambda i: (i, 0)),
            pl.BlockSpec((None, 1, d), lambda i: (row(i), 0, 3)),
            pl.BlockSpec((None, 1, d), lambda i: (row(i), 0, 4)),
            const((1, d)), const(wr_t.shape), const(br_t.shape),
        ],
        out_specs=[
            pl.BlockSpec((tm, d), lambda i: (i, 0)),
            pl.BlockSpec((8, tm), lambda i: (0, i)),
            pl.BlockSpec((tm, LANES), lambda i: (i, 0)),
            const((N_EXPERTS, LANES)),
        ],
        out_shape=[
            jax.ShapeDtypeStruct((n, d), F32),
            jax.ShapeDtypeStruct((8, n), jnp.int32),
            jax.ShapeDtypeStruct((n, LANES), F32),
            jax.ShapeDtypeStruct((N_EXPERTS, LANES), F32),
        ],
        scratch_shapes=[pltpu.VMEM((N_EXPERTS, 1), F32)],
        compiler_params=_params(("arbitrary",)),
        name="router",
    )(lat, mods, mods, g, wr_t, br_t)


def _plan_kernel(cnt_ref, mi_ref, dest_ref, be_ref, nv_ref, nu_ref, ps_ref, *, n_blocks):
    bm = MOE_BM

    def per_expert(e, blk0):
        cnt = cnt_ref[e]
        nb = (cnt + bm - 1) // bm
        ps_ref[e] = blk0 * bm

        def fill(b, c):
            be_ref[b] = e
            nv_ref[b] = jnp.minimum(cnt - (b - blk0) * bm, bm)
            return c

        lax.fori_loop(blk0, blk0 + nb, fill, 0)
        return blk0 + nb

    n_used = lax.fori_loop(0, N_EXPERTS, per_expert, 0)
    nu_ref[0] = n_used
    last_e = be_ref[jnp.maximum(n_used - 1, 0)]

    def fill_tail(b, c):
        be_ref[b] = last_e
        nv_ref[b] = 0
        return c

    lax.fori_loop(n_used, n_blocks, fill_tail, 0)

    e01 = mi_ref[0:2, :]
    dest = mi_ref[2:4, :]
    for e in range(N_EXPERTS):
        dest = dest + jnp.where(e01 == e, ps_ref[e], 0)
    dest_ref[...] = dest


def _plan(counts, meta_i, n_blocks):
    n = meta_i.shape[1]
    return pl.pallas_call(
        functools.partial(_plan_kernel, n_blocks=n_blocks),
        in_specs=[pl.BlockSpec(memory_space=pltpu.SMEM), pl.BlockSpec(memory_space=pltpu.VMEM)],
        out_specs=[pl.BlockSpec(memory_space=pltpu.VMEM), pl.BlockSpec(memory_space=pltpu.SMEM),
                   pl.BlockSpec(memory_space=pltpu.SMEM), pl.BlockSpec(memory_space=pltpu.SMEM)],
        out_shape=[
            jax.ShapeDtypeStruct((2, n), jnp.int32),
            jax.ShapeDtypeStruct((n_blocks,), jnp.int32),
            jax.ShapeDtypeStruct((n_blocks,), jnp.int32),
            jax.ShapeDtypeStruct((1,), jnp.int32),
        ],
        scratch_shapes=[pltpu.SMEM((N_EXPERTS,), jnp.int32)],
        compiler_params=pltpu.CompilerParams(vmem_limit_bytes=VMEM_LIMIT),
        name="plan",
    )(counts, meta_i)


def _sc_mesh():
    return plsc.VectorSubcoreMesh(core_axis_name="c", subcore_axis_name="s",
                                  num_cores=SC_CORES, num_subcores=SC_SUBCORES)


def _sc_worker():
    return lax.axis_index("s") * SC_CORES + lax.axis_index("c")


def _sc_dispatch(h2, dest, n_rows):
    n, d = h2.shape
    c = SC_CHUNK
    per_w = n // SC_WORKERS
    nchunk = per_w // c
    idx = dest.reshape(2, SC_WORKERS, nchunk, c)

    @functools.partial(
        pl.kernel, mesh=_sc_mesh(), out_type=jax.ShapeDtypeStruct((n_rows, d), h2.dtype),
        scratch_types=[pltpu.VMEM((nchunk, c), jnp.int32), pltpu.VMEM((nchunk, c), jnp.int32),
                       pltpu.VMEM((2, c, d), h2.dtype),
                       pltpu.SemaphoreType.DMA((2,)), pltpu.SemaphoreType.DMA((2,))])
    def k(h_hbm, idx_hbm, xb_hbm, idx0_v, idx1_v, rows_v, gsem, ssem):
        wid = _sc_worker()
        base = wid * per_w
        idx_v = (idx0_v, idx1_v)
        for kk in range(2):
            pltpu.sync_copy(idx_hbm.at[kk, wid], idx_v[kk])

        def get(j, slot):
            return pltpu.make_async_copy(h_hbm.at[pl.ds(base + j * c, c)], rows_v.at[slot], gsem.at[slot])

        def put(j, slot, kk):
            return pltpu.make_async_copy(rows_v.at[slot], xb_hbm.at[idx_v[kk].at[j]], ssem.at[slot])

        get(0, 0).start()

        @pl.loop(0, nchunk, step=2)
        def _(j):
            for slot in range(2):
                jj = j + slot
                get(jj, slot).wait()

                @pl.when(jj >= 1)
                def _():
                    for kk in range(2):
                        put(jj - 1, 1 - slot, kk).wait()

                @pl.when(jj + 1 < nchunk)
                def _():
                    get(jj + 1, 1 - slot).start()

                for kk in range(2):
                    put(jj, slot, kk).start()

        for kk in range(2):
            put(nchunk - 1, (nchunk - 1) % 2, kk).wait()

    return k(h2, idx)


def _sc_gather(y, dest):
    d = y.shape[1]
    total = dest.shape[0] * dest.shape[1]
    c = SC_CHUNK
    per_w = total // SC_WORKERS
    nchunk = per_w // c
    idx = dest.reshape(SC_WORKERS, nchunk, c)

    @functools.partial(
        pl.kernel, mesh=_sc_mesh(), out_type=jax.ShapeDtypeStruct((total, d), y.dtype),
        scratch_types=[pltpu.VMEM((nchunk, c), jnp.int32), pltpu.VMEM((2, c, d), y.dtype),
                       pltpu.SemaphoreType.DMA((2,)), pltpu.SemaphoreType.DMA((2,))])
    def k(y_hbm, idx_hbm, out_hbm, idx_v, rows_v, gsem, ssem):
        wid = _sc_worker()
        base = wid * per_w
        pltpu.sync_copy(idx_hbm.at[wid], idx_v)

        def get(j, slot):
            return pltpu.make_async_copy(y_hbm.at[idx_v.at[j]], rows_v.at[slot], gsem.at[slot])

        def put(j, slot):
            return pltpu.make_async_copy(rows_v.at[slot], out_hbm.at[pl.ds(base + j * c, c)], ssem.at[slot])

        get(0, 0).start()

        @pl.loop(0, nchunk, step=2)
        def _(j):
            for slot in range(2):
                jj = j + slot
                get(jj, slot).wait()

                @pl.when(jj >= 1)
                def _():
                    put(jj - 1, 1 - slot).wait()

                @pl.when(jj + 1 < nchunk)
                def _():
                    get(jj + 1, 1 - slot).start()

                put(jj, slot).start()

        put(nchunk - 1, (nchunk - 1) % 2).wait()

    return k(y, idx)


def _expert_kernel(be_ref, nv_ref, nu_ref, x_ref, wg_ref, wu_ref, wd_ref, y_ref, wgu_s, wd_s):
    b = pl.program_id(0)
    hid = wg_ref.shape[2]
    changed = jnp.logical_or(b == 0, be_ref[b] != be_ref[jnp.maximum(b - 1, 0)])

    @pl.when(changed)
    def _():
        wgu_s[:, 0:hid] = wg_ref[0].astype(BF16)
        wgu_s[:, hid:2 * hid] = wu_ref[0].astype(BF16)
        wd_s[...] = wd_ref[0].astype(BF16)

    @pl.when(b < nu_ref[0])
    def _():
        live = lax.broadcasted_iota(jnp.int32, x_ref.shape, 0) < nv_ref[b]
        x = jnp.where(live, x_ref[...], 0.0)
        gu = jnp.dot(x.astype(BF16), wgu_s[...], preferred_element_type=F32)
        gate = gu[:, 0:hid]
        act = gate * (1.0 / (1.0 + jnp.exp(-gate))) * gu[:, hid:2 * hid]
        y_ref[...] = jnp.dot(act.astype(BF16), wd_s[...], preferred_element_type=F32)

    @pl.when(b >= nu_ref[0])
    def _():
        y_ref[...] = jnp.zeros_like(y_ref)


def _experts(block_e, n_valid, n_used, xb, w_gate, w_up, w_down, layer):
    n_rows, d = xb.shape
    hid = w_gate.shape[3]
    bm = MOE_BM
    n_blocks = n_rows // bm
    blk = lambda b, be, nv, nu: (jnp.minimum(b, nu[0] - 1), 0)
    wsel = lambda b, be, nv, nu: (layer, be[b], 0, 0)
    return pl.pallas_call(
        _expert_kernel,
        grid_spec=pltpu.PrefetchScalarGridSpec(
            num_scalar_prefetch=3,
            grid=(n_blocks,),
            in_specs=[
                pl.BlockSpec((bm, d), blk),
                pl.BlockSpec((None, 1, d, hid), wsel),
                pl.BlockSpec((None, 1, d, hid), wsel),
                pl.BlockSpec((None, 1, hid, d), wsel),
            ],
            out_specs=pl.BlockSpec((bm, d), lambda b, be, nv, nu: (b, 0)),
            scratch_shapes=[pltpu.VMEM((d, 2 * hid), BF16), pltpu.VMEM((hid, d), BF16)],
        ),
        out_shape=jax.ShapeDtypeStruct((n_rows, d), F32),
        compiler_params=_params(("arbitrary",)),
        name="experts",
    )(block_e, n_valid, n_used, xb, w_gate, w_up, w_down)


def _combine_kernel(lat_ref, y0_ref, y1_ref, wc_ref, gate_ref, gf_ref, o_ref, *, final):
    wc = wc_ref[...]
    moe = wc[:, 0:1] * y0_ref[...] + wc[:, 1:2] * y1_ref[...]
    out = lat_ref[...] + gate_ref[...] * moe
    if final:
        ms = jnp.mean(out * out, axis=-1, keepdims=True)
        out = out * lax.rsqrt(ms + EPS) * gf_ref[...]
    o_ref[...] = out


def _combine(lat, yg, wcol, mods, layer, g_final, seq, final):
    n, d = lat.shape
    tm = COMBINE_TM
    nt = n // tm
    tiles_per_seq = seq // tm
    row = lambda i: layer * MOD_ROWS + i // tiles_per_seq
    return pl.pallas_call(
        functools.partial(_combine_kernel, final=final),
        grid=(nt,),
        in_specs=[
            pl.BlockSpec((tm, d), lambda i: (i, 0)),
            pl.BlockSpec((tm, d), lambda i: (i, 0)),
            pl.BlockSpec((tm, d), lambda i: (nt + i, 0)),
            pl.BlockSpec((tm, LANES), lambda i: (i, 0)),
            pl.BlockSpec((None, 1, d), lambda i: (row(i), 0, 5)),
            pl.BlockSpec((1, d), lambda i: (0, 0)),
        ],
        out_specs=pl.BlockSpec((tm, d), lambda i: (i, 0)),
        out_shape=jax.ShapeDtypeStruct((n, d), F32),
        compiler_params=_params(("parallel",)),
        name="combine",
    )(lat, yg, yg, wcol, mods, g_final)


def _moe(lat, mods, layer, g2, wr_t, br_t, w_gate, w_up, w_down, g_final, seq, final):
    n, d = lat.shape
    n_blocks = (2 * n) // MOE_BM + N_EXPERTS
    h2, meta_i, wcol, counts = _router(lat, mods, layer, g2, wr_t, br_t, seq)
    dest, block_e, n_valid, n_used = _plan(counts[:, 0].astype(jnp.int32), meta_i, n_blocks)
    xb = _sc_dispatch(h2, dest, n_blocks * MOE_BM)
    yb = _experts(block_e, n_valid, n_used, xb, w_gate, w_up, w_down, layer)
    yg = _sc_gather(yb, dest)
    return _combine(lat, yg, wcol, mods, layer, g_final, seq, final)


def _rope_tables(seq):
    quarter = HEAD_DIM // 4
    pos = jnp.arange(seq, dtype=F32)
    row_ids = jnp.floor(pos / GRID_W)
    col_ids = pos - row_ids * GRID_W
    inv = ROPE_BASE ** (-jnp.arange(quarter, dtype=F32) / quarter)
    ang_r = row_ids[:, None] * inv
    ang_c = col_ids[:, None] * inv
    zero = jnp.zeros_like(ang_r)
    cos = jnp.concatenate([jnp.cos(ang_r), jnp.cos(ang_r), jnp.cos(ang_c), jnp.cos(ang_c)], axis=-1)
    sa = jnp.concatenate([-jnp.sin(ang_r), zero, -jnp.sin(ang_c), zero], axis=-1)
    sb = jnp.concatenate([zero, jnp.sin(ang_r), zero, jnp.sin(ang_c)], axis=-1)
    rep = LANES // HEAD_DIM
    return tuple(jnp.tile(t, (1, rep)) for t in (cos, sa, sb))


def _router_weights(w_rg, b_rg, w_re, b_re):
    d = w_rg.shape[0]
    pad = EXPERTS_PER_GROUP - N_GROUPS
    wr_t = jnp.concatenate([w_rg.T, jnp.zeros((pad, d), F32), w_re.T], axis=0)
    br_t = jnp.concatenate([b_rg, jnp.full((pad,), NEG_INF, F32), b_re])[:, None]
    return wr_t, br_t


def kernel(x, c, ctx, c_ctx, w_ada, b_ada, g_norm1, g_norm2, g_final, w_in_even, attn_sink, g_sgu,
           w_spatial, b_spatial, w_out_even, w_in_odd, conv_w, w_out_odd, w_router_group,
           b_router_group, w_router_expert, b_router_expert, w_gate, w_up, w_down):
    b, s, d = x.shape
    n = b * s
    n_ctx = ctx.shape[1]
    depth = w_ada.shape[0]
    assert depth == 2 and b + 1 <= MOD_ROWS

    cond = jnp.concatenate([c, c_ctx[None, :], jnp.zeros((MOD_ROWS - b - 1, d), F32)], axis=0)
    mods = _ada(cond, w_ada, b_ada).reshape(depth * MOD_ROWS, 1, 6 * d)
    gf = g_final[None, :]

    lat = x.reshape(n, d)
    w_in_bf = w_in_even[0].astype(BF16)
    tabs = _rope_tables(s)
    q, kx, vx, u, z = _even_in(lat, mods, 0, lambda i: i // (s // EVEN_TM), g_norm1[0][None, :], w_in_bf,
                               tabs, s // EVEN_TM, EVEN_TM)
    ones = jnp.ones((n_ctx, LANES), F32)
    zeros = jnp.zeros((n_ctx, LANES), F32)
    _, kcx, vcx, _, _ = _even_in(ctx.reshape(b * n_ctx, d), mods, 0, lambda i: b, g_norm1[0][None, :],
                                 w_in_bf, (ones, zeros, zeros), 1, n_ctx)
    bsp_full = jnp.repeat(b_spatial[0].T, HEAD_DIM, axis=1)
    lat = _even_mix(lat, q, kx, vx, kcx, vcx, u, z, attn_sink[0], g_sgu[0][None, :],
                    w_spatial[0].astype(BF16), bsp_full, w_out_even[0].astype(BF16), mods, s, n_ctx)
    wr_t, br_t = _router_weights(w_router_group[0], b_router_group[0], w_router_expert[0], b_router_expert[0])
    lat = _moe(lat, mods, 0, g_norm2[0][None, :], wr_t, br_t, w_gate, w_up, w_down, gf, s, False)

    conv_w8 = jnp.concatenate([conv_w[0], jnp.zeros((8 - conv_w.shape[1], d), F32)], axis=0)
    lat = _odd_mix(lat, mods, 1, g_norm1[1][None, :], w_in_odd[0].astype(BF16), conv_w8,
                   w_out_odd[0].astype(BF16), s)
    wr_t, br_t = _router_weights(w_router_group[1], b_router_group[1], w_router_expert[1], b_router_expert[1])
    out = _moe(lat, mods, 1, g_norm2[1][None, :], wr_t, br_t, w_gate, w_up, w_down, gf, s, True)
    return out.reshape(b, s, d)
```

```python
import functools

import jax
import jax.numpy as jnp
from jax import lax
from jax.experimental import pallas as pl
from jax.experimental.pallas import tpu as pltpu
from jax.experimental.pallas import tpu_sc as plsc

F32 = jnp.float32
BF16 = jnp.bfloat16
HIGHEST = lax.Precision.HIGHEST

GRID_W = 64
N_Q_HEADS = 8
N_KV_HEADS = 2
HEAD_DIM = 64
ATT_BLOCK = 128
ROPE_BASE = 10000.0
Q_DIM = N_Q_HEADS * HEAD_DIM
KV_DIM = N_KV_HEADS * HEAD_DIM
SG_GROUPS = 8
SG_WIDTH = SG_GROUPS * HEAD_DIM
N_GROUPS = 4
EXPERTS_PER_GROUP = 8
N_EXPERTS = N_GROUPS * EXPERTS_PER_GROUP
EPS = 1e-6
NEG_INF = -1e30

LANES = 128
SC_CORES = 2
SC_SUBCORES = 16
SC_WORKERS = SC_CORES * SC_SUBCORES
SC_CHUNK = 32
MOD_ROWS = 8
VMEM_LIMIT = 56 * 1024 * 1024

ADA_TN = 1536
EVEN_TM = 512
ATT_TQ = 512
ODD_TM = 512
ROUTE_TM = 512
MOE_BM = 512
COMBINE_TM = 512


def _params(sem):
    return pltpu.CompilerParams(dimension_semantics=sem, vmem_limit_bytes=VMEM_LIMIT)


def _rms_mod(x, g, shift, scale):
    ms = jnp.mean(x * x, axis=-1, keepdims=True)
    return (x * lax.rsqrt(ms + EPS) * g) * (1.0 + scale) + shift


def _ada_kernel(a_ref, w_ref, b_ref, o_ref):
    a = a_ref[...]
    s = a * (1.0 / (1.0 + jnp.exp(-a)))
    o_ref[0] = jnp.dot(s, w_ref[0], preferred_element_type=F32, precision=HIGHEST) + b_ref[0]


def _ada(cond, w_ada, b_ada):
    depth, d, six_d = w_ada.shape
    return pl.pallas_call(
        _ada_kernel,
        grid=(depth, six_d // ADA_TN),
        in_specs=[
            pl.BlockSpec((MOD_ROWS, d), lambda l, j: (0, 0)),
            pl.BlockSpec((1, d, ADA_TN), lambda l, j: (l, 0, j)),
            pl.BlockSpec((1, 1, ADA_TN), lambda l, j: (l, 0, j)),
        ],
        out_specs=pl.BlockSpec((1, MOD_ROWS, ADA_TN), lambda l, j: (l, 0, j)),
        out_shape=jax.ShapeDtypeStruct((depth, MOD_ROWS, six_d), F32),
        compiler_params=_params(("arbitrary", "arbitrary")),
        name="ada",
    )(cond, w_ada, b_ada.reshape(depth, 1, six_d))


def _even_in_kernel(x_ref, sh_ref, sc_ref, g_ref, w_ref, cos_ref, sa_ref, sb_ref,
                    q_ref, kx_ref, vx_ref, u_ref, z_ref):
    h = _rms_mod(x_ref[...], g_ref[...], sh_ref[...], sc_ref[...])
    p = jnp.dot(h.astype(BF16), w_ref[...], preferred_element_type=F32)
    cos, sa, sb = cos_ref[...], sa_ref[...], sb_ref[...]

    def rope(t):
        return t * cos + pltpu.roll(t, LANES - 16, 1) * sa + pltpu.roll(t, 16, 1) * sb

    scale = HEAD_DIM ** -0.5
    for cblk in range(Q_DIM // LANES):
        t = p[:, cblk * LANES:(cblk + 1) * LANES]
        q_ref[:, cblk * LANES:(cblk + 1) * LANES] = (rope(t) * scale).astype(BF16)

    low = lax.broadcasted_iota(jnp.int32, (x_ref.shape[0], LANES), 1) < HEAD_DIM

    def spread(t, o_ref):
        sw = pltpu.roll(t, HEAD_DIM, 1)
        zero = jnp.zeros_like(t)
        o_ref[:, 0 * LANES:1 * LANES] = jnp.where(low, t, zero).astype(BF16)
        o_ref[:, 1 * LANES:2 * LANES] = jnp.where(low, zero, sw).astype(BF16)
        o_ref[:, 2 * LANES:3 * LANES] = jnp.where(low, sw, zero).astype(BF16)
        o_ref[:, 3 * LANES:4 * LANES] = jnp.where(low, zero, t).astype(BF16)

    spread(rope(p[:, Q_DIM:Q_DIM + KV_DIM]), kx_ref)
    spread(p[:, Q_DIM + KV_DIM:Q_DIM + 2 * KV_DIM], vx_ref)
    u0 = Q_DIM + 2 * KV_DIM
    u_ref[...] = p[:, u0:u0 + SG_WIDTH]
    z_ref[...] = p[:, u0 + SG_WIDTH:u0 + 2 * SG_WIDTH]


def _even_in(x2d, mods, layer, mod_row_fn, g, w_bf, tabs, tab_blocks, tm):
    n, d = x2d.shape
    ein = w_bf.shape[1]
    cos, sa, sb = tabs
    row = lambda i: layer * MOD_ROWS + mod_row_fn(i)
    tab_spec = pl.BlockSpec((tm, LANES), lambda i: (i % tab_blocks, 0))
    wide = 4 * LANES
    return pl.pallas_call(
        _even_in_kernel,
        grid=(n // tm,),
        in_specs=[
            pl.BlockSpec((tm, d), lambda i: (i, 0)),
            pl.BlockSpec((None, 1, d), lambda i: (row(i), 0, 0)),
            pl.BlockSpec((None, 1, d), lambda i: (row(i), 0, 1)),
            pl.BlockSpec((1, d), lambda i: (0, 0)),
            pl.BlockSpec((d, ein), lambda i: (0, 0)),
            tab_spec, tab_spec, tab_spec,
        ],
        out_specs=[
            pl.BlockSpec((tm, Q_DIM), lambda i: (i, 0)),
            pl.BlockSpec((tm, wide), lambda i: (i, 0)),
            pl.BlockSpec((tm, wide), lambda i: (i, 0)),
            pl.BlockSpec((tm, SG_WIDTH), lambda i: (i, 0)),
            pl.BlockSpec((tm, SG_WIDTH), lambda i: (i, 0)),
        ],
        out_shape=[
            jax.ShapeDtypeStruct((n, Q_DIM), BF16),
            jax.ShapeDtypeStruct((n, wide), BF16),
            jax.ShapeDtypeStruct((n, wide), BF16),
            jax.ShapeDtypeStruct((n, SG_WIDTH), F32),
            jax.ShapeDtypeStruct((n, SG_WIDTH), F32),
        ],
        compiler_params=_params(("parallel",)),
        name="even_in",
    )(x2d, mods, mods, g, w_bf, cos, sa, sb)


def _gelu(x):
    return 0.5 * x * (1.0 + lax.erf(x * (2.0 ** -0.5)))


def _even_mix_kernel(sink_ref, lat_ref, q_ref, kxm_ref, kxp_ref, kxn_ref, vxm_ref, vxp_ref, vxn_ref,
                     kcx_ref, vcx_ref, u_ref, z_ref, gsgu_ref, wsp_ref, bsp_ref, wout_ref, gate_ref,
                     o_ref, kband, vband, mixin, *, tiles_per_seq):
    i = pl.program_id(0)
    tq = q_ref.shape[0]
    blk = ATT_BLOCK
    nsub = tq // blk
    first = (i % tiles_per_seq) == 0
    last = (i % tiles_per_seq) == tiles_per_seq - 1

    kband[0:blk] = kxp_ref[...]
    kband[blk:blk + tq] = kxm_ref[...]
    kband[blk + tq:] = kxn_ref[...]
    vband[0:blk] = vxp_ref[...]
    vband[blk:blk + tq] = vxm_ref[...]
    vband[blk + tq:] = vxn_ref[...]

    row = lax.broadcasted_iota(jnp.int32, (blk, 3 * blk), 0)
    col = lax.broadcasted_iota(jnp.int32, (blk, 3 * blk), 1)
    lane_low = lax.broadcasted_iota(jnp.int32, (blk, LANES), 1) < HEAD_DIM
    nt = (((1,), (1,)), ((), ()))

    def sub_block(j, carry):
        r0 = pl.multiple_of(j * blk, blk)
        ok_prev = jnp.logical_not(jnp.logical_and(first, j == 0))
        ok_next = jnp.logical_not(jnp.logical_and(last, j == nsub - 1))
        mask = (((col >= blk) | ((col >= row) & ok_prev))
                & ((col < 2 * blk) | ((col - 2 * blk <= row) & ok_next)))
        qj = q_ref[pl.ds(r0, blk), :]
        kb = kband[pl.ds(r0, 3 * blk), :]
        vb = vband[pl.ds(r0, 3 * blk), :]
        for pair in range(N_Q_HEADS // 2):
            acc = None
            for par in range(2):
                hd = 2 * pair + par
                var = 2 * (hd // (N_Q_HEADS // N_KV_HEADS)) + par
                qh = qj[:, pair * LANES:(pair + 1) * LANES]
                s_loc = lax.dot_general(qh, kb[:, var * LANES:(var + 1) * LANES], nt,
                                        preferred_element_type=F32)
                s_ctx = lax.dot_general(qh, kcx_ref[:, var * LANES:(var + 1) * LANES], nt,
                                        preferred_element_type=F32)
                s_loc = jnp.where(mask, s_loc, NEG_INF)
                sk = sink_ref[hd]
                m = jnp.maximum(jnp.maximum(jnp.max(s_loc, axis=-1, keepdims=True),
                                            jnp.max(s_ctx, axis=-1, keepdims=True)), sk)
                p_loc = jnp.exp(s_loc - m)
                p_ctx = jnp.exp(s_ctx - m)
                den = (jnp.sum(p_loc, axis=-1, keepdims=True) + jnp.sum(p_ctx, axis=-1, keepdims=True)
                       + jnp.exp(sk - m))
                o = (jnp.dot(p_ctx.astype(BF16), vcx_ref[:, var * LANES:(var + 1) * LANES],
                             preferred_element_type=F32)
                     + jnp.dot(p_loc.astype(BF16), vb[:, var * LANES:(var + 1) * LANES],
                               preferred_element_type=F32))
                o = o / den
                acc = o if acc is None else acc + o
            mixin[pl.ds(r0, blk), pair * LANES:(pair + 1) * LANES] = acc.astype(BF16)

        ug = _gelu(u_ref[pl.ds(r0, blk), :])
        zg = _gelu(z_ref[pl.ds(r0, blk), :])
        mu = jnp.mean(zg, axis=-1, keepdims=True)
        zc = zg - mu
        zn = zc * lax.rsqrt(jnp.mean(zc * zc, axis=-1, keepdims=True) + EPS) * gsgu_ref[...]
        for pair in range(SG_GROUPS // 2):
            zp = zn[:, pair * LANES:(pair + 1) * LANES]
            zero = jnp.zeros_like(zp)
            lo = jnp.where(lane_low, zp, zero).astype(BF16)
            hi = jnp.where(lane_low, zero, zp).astype(BF16)
            sg = (jnp.dot(wsp_ref[2 * pair], lo, preferred_element_type=F32)
                  + jnp.dot(wsp_ref[2 * pair + 1], hi, preferred_element_type=F32)
                  + bsp_ref[:, pair * LANES:(pair + 1) * LANES])
            mixin[pl.ds(r0, blk), Q_DIM + pair * LANES:Q_DIM + (pair + 1) * LANES] = (
                ug[:, pair * LANES:(pair + 1) * LANES] * sg).astype(BF16)
        return carry

    lax.fori_loop(0, nsub, sub_block, 0)
    mix = jnp.dot(mixin[...], wout_ref[...], preferred_element_type=F32)
    o_ref[...] = lat_ref[...] + gate_ref[...] * mix


def _even_mix(lat, q, kx, vx, kcx, vcx, u, z, sink, g_sgu, wsp_bf, bsp_full, wout_bf, mods, seq, ctx_len):
    n, d = lat.shape
    tq = ATT_TQ
    tiles_per_seq = seq // tq
    sub = tq // ATT_BLOCK
    nblk = n // ATT_BLOCK
    wide = 4 * LANES
    main = lambda w: pl.BlockSpec((tq, w), lambda i: (i, 0))
    prev = pl.BlockSpec((ATT_BLOCK, wide), lambda i: (jnp.maximum(i * sub - 1, 0), 0))
    nxt = pl.BlockSpec((ATT_BLOCK, wide), lambda i: (jnp.minimum((i + 1) * sub, nblk - 1), 0))
    ctxs = pl.BlockSpec((ctx_len, wide), lambda i: (i // tiles_per_seq, 0))
    const = lambda shape: pl.BlockSpec(shape, lambda i: (0,) * len(shape), pipeline_mode=pl.Buffered(1))
    return pl.pallas_call(
        functools.partial(_even_mix_kernel, tiles_per_seq=tiles_per_seq),
        grid=(n // tq,),
        in_specs=[
            pl.BlockSpec(memory_space=pltpu.SMEM),
            main(d), main(Q_DIM),
            main(wide), prev, nxt,
            main(wide), prev, nxt,
            ctxs, ctxs,
            main(SG_WIDTH), main(SG_WIDTH),
            const((1, SG_WIDTH)), const(wsp_bf.shape), const(bsp_full.shape), const(wout_bf.shape),
            pl.BlockSpec((None, 1, d), lambda i: (i // tiles_per_seq, 0, 2)),
        ],
        out_specs=pl.BlockSpec((tq, d), lambda i: (i, 0)),
        out_shape=jax.ShapeDtypeStruct((n, d), F32),
        scratch_shapes=[
            pltpu.VMEM((tq + 2 * ATT_BLOCK, wide), BF16),
            pltpu.VMEM((tq + 2 * ATT_BLOCK, wide), BF16),
            pltpu.VMEM((tq, Q_DIM + SG_WIDTH), BF16),
        ],
        compiler_params=_params(("parallel",)),
        name="even_mix",
    )(sink, lat, q, kx, kx, kx, vx, vx, vx, kcx, vcx, u, z, g_sgu, wsp_bf, bsp_full, wout_bf, mods)


def _odd_mix_kernel(x_ref, xp_ref, sh_ref, sc_ref, gate_ref, g_ref, win_ref, cw_ref, wout_ref,
                    o_ref, y_s, bg_s, tail_s, *, tiles_per_seq):
    i = pl.program_id(0)
    n_tiles = pl.num_programs(0) - 1
    tm, d = x_ref.shape
    cur = i % 2
    prv = 1 - cur

    tail_s[...] = y_s[cur, tm - 8:tm, :]

    @pl.when(i < n_tiles)
    def _():
        h = _rms_mod(x_ref[...], g_ref[...], sh_ref[...], sc_ref[...])
        p = jnp.dot(h.astype(BF16), win_ref[...], preferred_element_type=F32)
        bg_s[cur] = p[:, 0:d]
        y_s[cur] = p[:, d:2 * d] * p[:, 2 * d:3 * d]

    @pl.when(i >= 1)
    def _():
        t_prev = i - 1
        first = (t_prev % tiles_per_seq) == 0
        last = (t_prev % tiles_per_seq) == tiles_per_seq - 1
        y = y_s[prv]
        left = jnp.where(first, 0.0, tail_s[7:8, :])
        right = jnp.where(last, 0.0, y_s[cur, 0:1, :])
        ridx = lax.broadcasted_iota(jnp.int32, (tm, d), 0)
        y_dn = jnp.where(ridx == 0, left, pltpu.roll(y, 1, 0))
        y_up = jnp.where(ridx == tm - 1, right, pltpu.roll(y, tm - 1, 0))
        conv = y_dn * cw_ref[0:1, :] + y * cw_ref[1:2, :] + y_up * cw_ref[2:3, :]
        mix = jnp.dot((bg_s[prv] * conv).astype(BF16), wout_ref[...], preferred_element_type=F32)
        o_ref[...] = xp_ref[...] + gate_ref[...] * mix


def _odd_mix(lat, mods, layer, g, win_bf, conv_w8, wout_bf, seq):
    n, d = lat.shape
    tm = ODD_TM
    nt = n // tm
    tiles_per_seq = seq // tm
    cur = lambda i: jnp.minimum(i, nt - 1)
    prv = lambda i: jnp.maximum(i - 1, 0)
    row = lambda t: layer * MOD_ROWS + t // tiles_per_seq
    const = lambda shape: pl.BlockSpec(shape, lambda i: (0,) * len(shape), pipeline_mode=pl.Buffered(1))
    return pl.pallas_call(
        functools.partial(_odd_mix_kernel, tiles_per_seq=tiles_per_seq),
        grid=(nt + 1,),
        in_specs=[
            pl.BlockSpec((tm, d), lambda i: (cur(i), 0)),
            pl.BlockSpec((tm, d), lambda i: (prv(i), 0)),
            pl.BlockSpec((None, 1, d), lambda i: (row(cur(i)), 0, 0)),
            pl.BlockSpec((None, 1, d), lambda i: (row(cur(i)), 0, 1)),
            pl.BlockSpec((None, 1, d), lambda i: (row(prv(i)), 0, 2)),
            const((1, d)), const(win_bf.shape), const(conv_w8.shape), const(wout_bf.shape),
        ],
        out_specs=pl.BlockSpec((tm, d), lambda i: (prv(i), 0)),
        out_shape=jax.ShapeDtypeStruct((n, d), F32),
        scratch_shapes=[
            pltpu.VMEM((2, tm, d), F32),
            pltpu.VMEM((2, tm, d), F32),
            pltpu.VMEM((8, d), F32),
        ],
        compiler_params=_params(("arbitrary",)),
        name="odd_mix",
    )(lat, lat, mods, mods, mods, g, win_bf, conv_w8, wout_bf)


def _router_kernel(x_ref, sh_ref, sc_ref, g_ref, wr_ref, br_ref,
                   h_ref, mi_ref, wc_ref, cnt_ref, carry):
    i = pl.program_id(0)
    tm = x_ref.shape[0]
    epg = EXPERTS_PER_GROUP

    @pl.when(i == 0)
    def _():
        carry[...] = jnp.zeros_like(carry)

    h = _rms_mod(x_ref[...], g_ref[...], sh_ref[...], sc_ref[...])
    h_ref[...] = h
    lg = lax.dot_general(wr_ref[...], h, (((1,), (1,)), ((), ())),
                         preferred_element_type=F32, precision=HIGHEST) + br_ref[...]
    io8 = lax.broadcasted_iota(jnp.int32, (epg, tm), 0)
    gl = lg[0:epg]
    gmax = jnp.max(gl, axis=0, keepdims=True)
    g_idx = jnp.min(jnp.where(gl == gmax, io8, epg), axis=0, keepdims=True)
    g_w = 1.0 / jnp.sum(jnp.exp(gl - gmax), axis=0, keepdims=True)
    e_sel = lg[epg:2 * epg]
    for gi in range(1, N_GROUPS):
        e_sel = jnp.where(g_idx == gi, lg[(gi + 1) * epg:(gi + 2) * epg], e_sel)
    v0 = jnp.max(e_sel, axis=0, keepdims=True)
    i0 = jnp.min(jnp.where(e_sel == v0, io8, epg), axis=0, keepdims=True)
    rest = jnp.where(io8 == i0, -jnp.inf, e_sel)
    v1 = jnp.max(rest, axis=0, keepdims=True)
    i1 = jnp.min(jnp.where(rest == v1, io8, epg), axis=0, keepdims=True)
    t = jnp.exp(v1 - v0)
    w0 = g_w / (1.0 + t)
    w1 = g_w * t / (1.0 + t)
    e0 = g_idx * epg + i0
    e1 = g_idx * epg + i1

    io32 = lax.broadcasted_iota(jnp.int32, (N_EXPERTS, tm), 0)
    hit0 = io32 == e0
    hit1 = io32 == e1
    onehot = jnp.where(hit0 | hit1, 1.0, 0.0)
    r_i = lax.broadcasted_iota(jnp.int32, (tm, tm), 0)
    c_i = lax.broadcasted_iota(jnp.int32, (tm, tm), 1)
    upper = jnp.where(r_i < c_i, 1.0, 0.0).astype(BF16)
    cum = jnp.dot(onehot.astype(BF16), upper, preferred_element_type=F32) + carry[...]
    rank0 = jnp.sum(jnp.where(hit0, cum, 0.0), axis=0, keepdims=True).astype(jnp.int32)
    rank1 = jnp.sum(jnp.where(hit1, cum, 0.0), axis=0, keepdims=True).astype(jnp.int32)
    carry[...] = carry[...] + jnp.sum(onehot, axis=1, keepdims=True)
    cnt_ref[...] = jnp.broadcast_to(carry[...], cnt_ref.shape)

    mi_ref[...] = jnp.where(io8 == 0, e0, jnp.where(io8 == 1, e1, jnp.where(io8 == 2, rank0,
                            jnp.where(io8 == 3, rank1, 0))))
    io128 = lax.broadcasted_iota(jnp.int32, (LANES, tm), 0)
    wrow = jnp.where(io128 == 0, w0, jnp.where(io128 == 1, w1, 0.0))
    wc_ref[...] = wrow.T


def _router(lat, mods, layer, g, wr_t, br_t, seq):
    n, d = lat.shape
    tm = ROUTE_TM
    tiles_per_seq = seq // tm
    row = lambda i: layer * MOD_ROWS + i // tiles_per_seq
    const = lambda shape: pl.BlockSpec(shape, lambda i: (0,) * len(shape), pipeline_mode=pl.Buffered(1))
    return pl.pallas_call(
        _router_kernel,
        grid=(n // tm,),
        in_specs=[
            pl.BlockSpec((tm, d), lambda i: (i, 0)),
            pl.BlockSpec((None, 1, d), lambda i: (row(i), 0, 3)),
            pl.BlockSpec((None, 1, d), lambda i: (row(i), 0, 4)),
            const((1, d)), const(wr_t.shape), const(br_t.shape),
        ],
        out_specs=[
            pl.BlockSpec((tm, d), lambda i: (i, 0)),
            pl.BlockSpec((8, tm), lambda i: (0, i)),
            pl.BlockSpec((tm, LANES), lambda i: (i, 0)),
            pl.BlockSpec((N_EXPERTS, LANES), lambda i: (0, 0)),
        ],
        out_shape=[
            jax.ShapeDtypeStruct((n, d), F32),
            jax.ShapeDtypeStruct((8, n), jnp.int32),
            jax.ShapeDtypeStruct((n, LANES), F32),
            jax.ShapeDtypeStruct((N_EXPERTS, LANES), F32),
        ],
        scratch_shapes=[pltpu.VMEM((N_EXPERTS, 1), F32)],
        compiler_params=_params(("arbitrary",)),
        name="router",
    )(lat, mods, mods, g, wr_t, br_t)


def _plan_kernel(cnt_ref, mi_ref, dest_ref, be_ref, nv_ref, nu_ref, ps_ref, *, n_blocks):
    bm = MOE_BM

    def per_expert(e, blk0):
        cnt = cnt_ref[e]
        nb = (cnt + bm - 1) // bm
        ps_ref[e] = blk0 * bm

        def fill(b, c):
            be_ref[b] = e
            nv_ref[b] = jnp.minimum(cnt - (b - blk0) * bm, bm)
            return c

        lax.fori_loop(blk0, blk0 + nb, fill, 0)
        return blk0 + nb

    n_used = lax.fori_loop(0, N_EXPERTS, per_expert, 0)
    nu_ref[0] = n_used
    last_e = be_ref[jnp.maximum(n_used - 1, 0)]

    def fill_tail(b, c):
        be_ref[b] = last_e
        nv_ref[b] = 0
        return c

    lax.fori_loop(n_used, n_blocks, fill_tail, 0)

    e01 = mi_ref[0:2, :]
    dest = mi_ref[2:4, :]
    for e in range(N_EXPERTS):
        dest = dest + jnp.where(e01 == e, ps_ref[e], 0)
    dest_ref[...] = dest


def _plan(counts, meta_i, n_blocks):
    n = meta_i.shape[1]
    return pl.pallas_call(
        functools.partial(_plan_kernel, n_blocks=n_blocks),
        in_specs=[pl.BlockSpec(memory_space=pltpu.SMEM), pl.BlockSpec(memory_space=pltpu.VMEM)],
        out_specs=[pl.BlockSpec(memory_space=pltpu.VMEM), pl.BlockSpec(memory_space=pltpu.SMEM),
                   pl.BlockSpec(memory_space=pltpu.SMEM), pl.BlockSpec(memory_space=pltpu.SMEM)],
        out_shape=[
            jax.ShapeDtypeStruct((2, n), jnp.int32),
            jax.ShapeDtypeStruct((n_blocks,), jnp.int32),
            jax.ShapeDtypeStruct((n_blocks,), jnp.int32),
            jax.ShapeDtypeStruct((1,), jnp.int32),
        ],
        scratch_shapes=[pltpu.SMEM((N_EXPERTS,), jnp.int32)],
        compiler_params=pltpu.CompilerParams(vmem_limit_bytes=VMEM_LIMIT),
        name="plan",
    )(counts, meta_i)


def _sc_mesh():
    return plsc.VectorSubcoreMesh(core_axis_name="c", subcore_axis_name="s",
                                  num_cores=SC_CORES, num_subcores=SC_SUBCORES)


def _sc_worker():
    return lax.axis_index("s") * SC_CORES + lax.axis_index("c")


def _sc_dispatch(h2, dest, n_rows):
    n, d = h2.shape
    c = SC_CHUNK
    per_w = n // SC_WORKERS
    nchunk = per_w // c
    idx = dest.reshape(2, SC_WORKERS, nchunk, c)

    @functools.partial(
        pl.kernel, mesh=_sc_mesh(), out_type=jax.ShapeDtypeStruct((n_rows, d), h2.dtype),
        scratch_types=[pltpu.VMEM((nchunk, c), jnp.int32), pltpu.VMEM((nchunk, c), jnp.int32),
                       pltpu.VMEM((2, c, d), h2.dtype),
                       pltpu.SemaphoreType.DMA((2,)), pltpu.SemaphoreType.DMA((2,))])
    def k(h_hbm, idx_hbm, xb_hbm, idx0_v, idx1_v, rows_v, gsem, ssem):
        wid = _sc_worker()
        base = wid * per_w
        idx_v = (idx0_v, idx1_v)
        for kk in range(2):
            pltpu.sync_copy(idx_hbm.at[kk, wid], idx_v[kk])

        def get(j, slot):
            return pltpu.make_async_copy(h_hbm.at[pl.ds(base + j * c, c)], rows_v.at[slot], gsem.at[slot])

        def put(j, slot, kk):
            return pltpu.make_async_copy(rows_v.at[slot], xb_hbm.at[idx_v[kk].at[j]], ssem.at[slot])

        get(0, 0).start()

        @pl.loop(0, nchunk, step=2)
        def _(j):
            for slot in range(2):
                jj = j + slot
                get(jj, slot).wait()

                @pl.when(jj >= 1)
                def _():
                    for kk in range(2):
                        put(jj - 1, 1 - slot, kk).wait()

                @pl.when(jj + 1 < nchunk)
                def _():
                    get(jj + 1, 1 - slot).start()

                for kk in range(2):
                    put(jj, slot, kk).start()

        for kk in range(2):
            put(nchunk - 1, (nchunk - 1) % 2, kk).wait()

    return k(h2, idx)


def _sc_gather(y, dest):
    d = y.shape[1]
    total = dest.shape[0] * dest.shape[1]
    c = SC_CHUNK
    per_w = total // SC_WORKERS
    nchunk = per_w // c
    idx = dest.reshape(SC_WORKERS, nchunk, c)

    @functools.partial(
        pl.kernel, mesh=_sc_mesh(), out_type=jax.ShapeDtypeStruct((total, d), y.dtype),
        scratch_types=[pltpu.VMEM((nchunk, c), jnp.int32), pltpu.VMEM((2, c, d), y.dtype),
                       pltpu.SemaphoreType.DMA((2,)), pltpu.SemaphoreType.DMA((2,))])
    def k(y_hbm, idx_hbm, out_hbm, idx_v, rows_v, gsem, ssem):
        wid = _sc_worker()
        base = wid * per_w
        pltpu.sync_copy(idx_hbm.at[wid], idx_v)

        def get(j, slot):
            return pltpu.make_async_copy(y_hbm.at[idx_v.at[j]], rows_v.at[slot], gsem.at[slot])

        def put(j, slot):
            return pltpu.make_async_copy(rows_v.at[slot], out_hbm.at[pl.ds(base + j * c, c)], ssem.at[slot])

        get(0, 0).start()

        @pl.loop(0, nchunk, step=2)
        def _(j):
            for slot in range(2):
                jj = j + slot
                get(jj, slot).wait()

                @pl.when(jj >= 1)
                def _():
                    put(jj - 1, 1 - slot).wait()

                @pl.when(jj + 1 < nchunk)
                def _():
                    get(jj + 1, 1 - slot).start()

                put(jj, slot).start()

        put(nchunk - 1, (nchunk - 1) % 2).wait()

    return k(y, idx)


def _expert_kernel(be_ref, nv_ref, nu_ref, x_ref, wg_ref, wu_ref, wd_ref, y_ref, wgu_s, wd_s):
    b = pl.program_id(0)
    hid = wg_ref.shape[2]
    changed = jnp.logical_or(b == 0, be_ref[b] != be_ref[jnp.maximum(b - 1, 0)])

    @pl.when(changed)
    def _():
        wgu_s[:, 0:hid] = wg_ref[0].astype(BF16)
        wgu_s[:, hid:2 * hid] = wu_ref[0].astype(BF16)
        wd_s[...] = wd_ref[0].astype(BF16)

    @pl.when(b < nu_ref[0])
    def _():
        live = lax.broadcasted_iota(jnp.int32, x_ref.shape, 0) < nv_ref[b]
        x = jnp.where(live, x_ref[...], 0.0)
        gu = jnp.dot(x.astype(BF16), wgu_s[...], preferred_element_type=F32)
        gate = gu[:, 0:hid]
        act = gate * (1.0 / (1.0 + jnp.exp(-gate))) * gu[:, hid:2 * hid]
        y_ref[...] = jnp.dot(act.astype(BF16), wd_s[...], preferred_element_type=F32)

    @pl.when(b >= nu_ref[0])
    def _():
        y_ref[...] = jnp.zeros_like(y_ref)


def _experts(block_e, n_valid, n_used, xb, w_gate, w_up, w_down, layer):
    n_rows, d = xb.shape
    hid = w_gate.shape[3]
    bm = MOE_BM
    n_blocks = n_rows // bm
    blk = lambda b, be, nv, nu: (jnp.minimum(b, nu[0] - 1), 0)
    wsel = lambda b, be, nv, nu: (layer, be[b], 0, 0)
    return pl.pallas_call(
        _expert_kernel,
        grid_spec=pltpu.PrefetchScalarGridSpec(
            num_scalar_prefetch=3,
            grid=(n_blocks,),
            in_specs=[
                pl.BlockSpec((bm, d), blk),
                pl.BlockSpec((None, 1, d, hid), wsel),
                pl.BlockSpec((None, 1, d, hid), wsel),
                pl.BlockSpec((None, 1, hid, d), wsel),
            ],
            out_specs=pl.BlockSpec((bm, d), lambda b, be, nv, nu: (b, 0)),
            scratch_shapes=[pltpu.VMEM((d, 2 * hid), BF16), pltpu.VMEM((hid, d), BF16)],
        ),
        out_shape=jax.ShapeDtypeStruct((n_rows, d), F32),
        compiler_params=_params(("arbitrary",)),
        name="experts",
    )(block_e, n_valid, n_used, xb, w_gate, w_up, w_down)


def _combine_kernel(lat_ref, y0_ref, y1_ref, wc_ref, gate_ref, gf_ref, o_ref, *, final):
    wc = wc_ref[...]
    moe = wc[:, 0:1] * y0_ref[...] + wc[:, 1:2] * y1_ref[...]
    out = lat_ref[...] + gate_ref[...] * moe
    if final:
        ms = jnp.mean(out * out, axis=-1, keepdims=True)
        out = out * lax.rsqrt(ms + EPS) * gf_ref[...]
    o_ref[...] = out


def _combine(lat, yg, wcol, mods, layer, g_final, seq, final):
    n, d = lat.shape
    tm = COMBINE_TM
    nt = n // tm
    tiles_per_seq = seq // tm
    row = lambda i: layer * MOD_ROWS + i // tiles_per_seq
    return pl.pallas_call(
        functools.partial(_combine_kernel, final=final),
        grid=(nt,),
        in_specs=[
            pl.BlockSpec((tm, d), lambda i: (i, 0)),
            pl.BlockSpec((tm, d), lambda i: (i, 0)),
            pl.BlockSpec((tm, d), lambda i: (nt + i, 0)),
            pl.BlockSpec((tm, LANES), lambda i: (i, 0)),
            pl.BlockSpec((None, 1, d), lambda i: (row(i), 0, 5)),
            pl.BlockSpec((1, d), lambda i: (0, 0)),
        ],
        out_specs=pl.BlockSpec((tm, d), lambda i: (i, 0)),
        out_shape=jax.ShapeDtypeStruct((n, d), F32),
        compiler_params=_params(("parallel",)),
        name="combine",
    )(lat, yg, yg, wcol, mods, g_final)


def _moe(lat, mods, layer, g2, wr_t, br_t, w_gate, w_up, w_down, g_final, seq, final):
    n, d = lat.shape
    n_blocks = (2 * n) // MOE_BM + N_EXPERTS
    h2, meta_i, wcol, counts = _router(lat, mods, layer, g2, wr_t, br_t, seq)
    dest, block_e, n_valid, n_used = _plan(counts[:, 0].astype(jnp.int32), meta_i, n_blocks)
    xb = _sc_dispatch(h2, dest, n_blocks * MOE_BM)
    yb = _experts(block_e, n_valid, n_used, xb, w_gate, w_up, w_down, layer)
    yg = _sc_gather(yb, dest)
    return _combine(lat, yg, wcol, mods, layer, g_final, seq, final)


def _rope_tables(seq):
    quarter = HEAD_DIM // 4
    pos = jnp.arange(seq, dtype=F32)
    row_ids = jnp.floor(pos / GRID_W)
    col_ids = pos - row_ids * GRID_W
    inv = ROPE_BASE ** (-jnp.arange(quarter, dtype=F32) / quarter)
    ang_r = row_ids[:, None] * inv
    ang_c = col_ids[:, None] * inv
    zero = jnp.zeros_like(ang_r)
    cos = jnp.concatenate([jnp.cos(ang_r), jnp.cos(ang_r), jnp.cos(ang_c), jnp.cos(ang_c)], axis=-1)
    sa = jnp.concatenate([-jnp.sin(ang_r), zero, -jnp.sin(ang_c), zero], axis=-1)
    sb = jnp.concatenate([zero, jnp.sin(ang_r), zero, jnp.sin(ang_c)], axis=-1)
    rep = LANES // HEAD_DIM
    return tuple(jnp.tile(t, (1, rep)) for t in (cos, sa, sb))


def _router_weights(w_rg, b_rg, w_re, b_re):
    d = w_rg.shape[0]
    pad = EXPERTS_PER_GROUP - N_GROUPS
    wr_t = jnp.concatenate([w_rg.T, jnp.zeros((pad, d), F32), w_re.T], axis=0)
    br_t = jnp.concatenate([b_rg, jnp.full((pad,), NEG_INF, F32), b_re])[:, None]
    return wr_t, br_t


def kernel(x, c, ctx, c_ctx, w_ada, b_ada, g_norm1, g_norm2, g_final, w_in_even, attn_sink, g_sgu,
           w_spatial, b_spatial, w_out_even, w_in_odd, conv_w, w_out_odd, w_router_group,
           b_router_group, w_router_expert, b_router_expert, w_gate, w_up, w_down):
    b, s, d = x.shape
    n = b * s
    n_ctx = ctx.shape[1]
    depth = w_ada.shape[0]
    assert depth == 2 and b + 1 <= MOD_ROWS

    cond = jnp.concatenate([c, c_ctx[None, :], jnp.zeros((MOD_ROWS - b - 1, d), F32)], axis=0)
    mods = _ada(cond, w_ada, b_ada).reshape(depth * MOD_ROWS, 1, 6 * d)
    gf = g_final[None, :]

    lat = x.reshape(n, d)
    w_in_bf = w_in_even[0].astype(BF16)
    tabs = _rope_tables(s)
    q, kx, vx, u, z = _even_in(lat, mods, 0, lambda i: i // (s // EVEN_TM), g_norm1[0][None, :], w_in_bf,
                               tabs, s // EVEN_TM, EVEN_TM)
    ones = jnp.ones((n_ctx, LANES), F32)
    zeros = jnp.zeros((n_ctx, LANES), F32)
    _, kcx, vcx, _, _ = _even_in(ctx.reshape(b * n_ctx, d), mods, 0, lambda i: b, g_norm1[0][None, :],
                                 w_in_bf, (ones, zeros, zeros), 1, n_ctx)
    bsp_full = jnp.repeat(b_spatial[0].T, HEAD_DIM, axis=1)
    lat = _even_mix(lat, q, kx, vx, kcx, vcx, u, z, attn_sink[0], g_sgu[0][None, :],
                    w_spatial[0].astype(BF16), bsp_full, w_out_even[0].astype(BF16), mods, s, n_ctx)
    wr_t, br_t = _router_weights(w_router_group[0], b_router_group[0], w_router_expert[0], b_router_expert[0])
    lat = _moe(lat, mods, 0, g_norm2[0][None, :], wr_t, br_t, w_gate, w_up, w_down, gf, s, False)

    conv_w8 = jnp.concatenate([conv_w[0], jnp.zeros((8 - conv_w.shape[1], d), F32)], axis=0)
    lat = _odd_mix(lat, mods, 1, g_norm1[1][None, :], w_in_odd[0].astype(BF16), conv_w8,
                   w_out_odd[0].astype(BF16), s)
    wr_t, br_t = _router_weights(w_router_group[1], b_router_group[1], w_router_expert[1], b_router_expert[1])
    out = _moe(lat, mods, 1, g_norm2[1][None, :], wr_t, br_t, w_gate, w_up, w_down, gf, s, True)
    return out.reshape(b, s, d)
```

```python
import functools

import jax
import jax.numpy as jnp
from jax import lax
from jax.experimental import pallas as pl
from jax.experimental.pallas import tpu as pltpu
from jax.experimental.pallas import tpu_sc as plsc

F32 = jnp.float32
BF16 = jnp.bfloat16
HIGHEST = lax.Precision.HIGHEST

GRID_W = 64
N_Q_HEADS = 8
N_KV_HEADS = 2
HEAD_DIM = 64
ATT_BLOCK = 128
ROPE_BASE = 10000.0
Q_DIM = N_Q_HEADS * HEAD_DIM
KV_DIM = N_KV_HEADS * HEAD_DIM
SG_GROUPS = 8
SG_WIDTH = SG_GROUPS * HEAD_DIM
N_GROUPS = 4
EXPERTS_PER_GROUP = 8
N_EXPERTS = N_GROUPS * EXPERTS_PER_GROUP
EPS = 1e-6
NEG_INF = -1e30

LANES = 128
SC_CORES = 2
SC_SUBCORES = 16
SC_WORKERS = SC_CORES * SC_SUBCORES
SC_CHUNK = 32
MOD_ROWS = 8
VMEM_LIMIT = 56 * 1024 * 1024

ADA_TN = 1536
EVEN_TM = 512
ATT_TQ = 512
ODD_TM = 512
ROUTE_TM = 512
MOE_BM = 512
COMBINE_TM = 512


def _params(sem):
    return pltpu.CompilerParams(dimension_semantics=sem, vmem_limit_bytes=VMEM_LIMIT)


def _rms_mod(x, g, shift, scale):
    ms = jnp.mean(x * x, axis=-1, keepdims=True)
    return (x * lax.rsqrt(ms + EPS) * g) * (1.0 + scale) + shift


def _pack_rows(a):
    w = a.shape[1] // 2
    hi = pltpu.bitcast(a[:, :w].astype(BF16).astype(F32), jnp.uint32)
    lo = pltpu.bitcast(a[:, w:].astype(BF16).astype(F32), jnp.uint32)
    return hi | (lo >> 16)


def _unpack_rows(p):
    hi = pltpu.bitcast(p & jnp.uint32(0xFFFF0000), F32)
    lo = pltpu.bitcast(p << 16, F32)
    return jnp.concatenate([hi, lo], axis=1)


def _ada_kernel(a_ref, w_ref, b_ref, o_ref):
    a = a_ref[...]
    s = a * (1.0 / (1.0 + jnp.exp(-a)))
    o_ref[0] = jnp.dot(s, w_ref[0], preferred_element_type=F32, precision=HIGHEST) + b_ref[0]


def _ada(cond, w_ada, b_ada):
    depth, d, six_d = w_ada.shape
    return pl.pallas_call(
        _ada_kernel,
        grid=(depth, six_d // ADA_TN),
        in_specs=[
            pl.BlockSpec((MOD_ROWS, d), lambda l, j: (0, 0)),
            pl.BlockSpec((1, d, ADA_TN), lambda l, j: (l, 0, j)),
            pl.BlockSpec((1, 1, ADA_TN), lambda l, j: (l, 0, j)),
        ],
        out_specs=pl.BlockSpec((1, MOD_ROWS, ADA_TN), lambda l, j: (l, 0, j)),
        out_shape=jax.ShapeDtypeStruct((depth, MOD_ROWS, six_d), F32),
        compiler_params=_params(("arbitrary", "arbitrary")),
        name="ada",
    )(cond, w_ada, b_ada.reshape(depth, 1, six_d))


def _even_in_kernel(x_ref, sh_ref, sc_ref, g_ref, w_ref, cos_ref, sa_ref, sb_ref,
                    q_ref, kx_ref, vx_ref, u_ref, z_ref):
    h = _rms_mod(x_ref[...], g_ref[...], sh_ref[...], sc_ref[...])
    p = jnp.dot(h.astype(BF16), w_ref[...], preferred_element_type=F32)
    cos, sa, sb = cos_ref[...], sa_ref[...], sb_ref[...]

    def rope(t):
        return t * cos + pltpu.roll(t, LANES - 16, 1) * sa + pltpu.roll(t, 16, 1) * sb

    scale = HEAD_DIM ** -0.5
    for cblk in range(Q_DIM // LANES):
        t = p[:, cblk * LANES:(cblk + 1) * LANES]
        q_ref[:, cblk * LANES:(cblk + 1) * LANES] = (rope(t) * scale).astype(BF16)

    low = lax.broadcasted_iota(jnp.int32, (x_ref.shape[0], LANES), 1) < HEAD_DIM

    def spread(t, o_ref):
        sw = pltpu.roll(t, HEAD_DIM, 1)
        zero = jnp.zeros_like(t)
        o_ref[:, 0 * LANES:1 * LANES] = jnp.where(low, t, zero).astype(BF16)
        o_ref[:, 1 * LANES:2 * LANES] = jnp.where(low, zero, sw).astype(BF16)
        o_ref[:, 2 * LANES:3 * LANES] = jnp.where(low, sw, zero).astype(BF16)
        o_ref[:, 3 * LANES:4 * LANES] = jnp.where(low, zero, t).astype(BF16)

    spread(rope(p[:, Q_DIM:Q_DIM + KV_DIM]), kx_ref)
    spread(p[:, Q_DIM + KV_DIM:Q_DIM + 2 * KV_DIM], vx_ref)
    u0 = Q_DIM + 2 * KV_DIM
    u_ref[...] = p[:, u0:u0 + SG_WIDTH]
    z_ref[...] = p[:, u0 + SG_WIDTH:u0 + 2 * SG_WIDTH]


def _even_in(x2d, mods, layer, mod_row_fn, g, w_bf, tabs, tab_blocks, tm):
    n, d = x2d.shape
    ein = w_bf.shape[1]
    cos, sa, sb = tabs
    row = lambda i: layer * MOD_ROWS + mod_row_fn(i)
    tab_spec = pl.BlockSpec((tm, LANES), lambda i: (i % tab_blocks, 0))
    wide = 4 * LANES
    return pl.pallas_call(
        _even_in_kernel,
        grid=(n // tm,),
        in_specs=[
            pl.BlockSpec((tm, d), lambda i: (i, 0)),
            pl.BlockSpec((None, 1, d), lambda i: (row(i), 0, 0)),
            pl.BlockSpec((None, 1, d), lambda i: (row(i), 0, 1)),
            pl.BlockSpec((1, d), lambda i: (0, 0)),
            pl.BlockSpec((d, ein), lambda i: (0, 0)),
            tab_spec, tab_spec, tab_spec,
        ],
        out_specs=[
            pl.BlockSpec((tm, Q_DIM), lambda i: (i, 0)),
            pl.BlockSpec((tm, wide), lambda i: (i, 0)),
            pl.BlockSpec((tm, wide), lambda i: (i, 0)),
            pl.BlockSpec((tm, SG_WIDTH), lambda i: (i, 0)),
            pl.BlockSpec((tm, SG_WIDTH), lambda i: (i, 0)),
        ],
        out_shape=[
            jax.ShapeDtypeStruct((n, Q_DIM), BF16),
            jax.ShapeDtypeStruct((n, wide), BF16),
            jax.ShapeDtypeStruct((n, wide), BF16),
            jax.ShapeDtypeStruct((n, SG_WIDTH), F32),
            jax.ShapeDtypeStruct((n, SG_WIDTH), F32),
        ],
        compiler_params=_params(("parallel",)),
        name="even_in",
    )(x2d, mods, mods, g, w_bf, cos, sa, sb)


def _gelu(x):
    return 0.5 * x * (1.0 + lax.erf(x * (2.0 ** -0.5)))


def _even_mix_kernel(sink_ref, lat_ref, q_ref, kxm_ref, kxp_ref, kxn_ref, vxm_ref, vxp_ref, vxn_ref,
                     kcx_ref, vcx_ref, u_ref, z_ref, gsgu_ref, wsp_ref, bsp_ref, wout_ref, gate_ref,
                     o_ref, kband, vband, mixin, *, tiles_per_seq):
    i = pl.program_id(0)
    tq = q_ref.shape[0]
    blk = ATT_BLOCK
    nsub = tq // blk
    first = (i % tiles_per_seq) == 0
    last = (i % tiles_per_seq) == tiles_per_seq - 1

    kband[0:blk] = kxp_ref[...]
    kband[blk:blk + tq] = kxm_ref[...]
    kband[blk + tq:] = kxn_ref[...]
    vband[0:blk] = vxp_ref[...]
    vband[blk:blk + tq] = vxm_ref[...]
    vband[blk + tq:] = vxn_ref[...]

    row = lax.broadcasted_iota(jnp.int32, (blk, 3 * blk), 0)
    col = lax.broadcasted_iota(jnp.int32, (blk, 3 * blk), 1)
    lane_low = lax.broadcasted_iota(jnp.int32, (blk, LANES), 1) < HEAD_DIM
    nt = (((1,), (1,)), ((), ()))

    def sub_block(j, carry):
        r0 = pl.multiple_of(j * blk, blk)
        ok_prev = jnp.logical_not(jnp.logical_and(first, j == 0))
        ok_next = jnp.logical_not(jnp.logical_and(last, j == nsub - 1))
        mask = (((col >= blk) | ((col >= row) & ok_prev))
                & ((col < 2 * blk) | ((col - 2 * blk <= row) & ok_next)))
        qj = q_ref[pl.ds(r0, blk), :]
        kb = kband[pl.ds(r0, 3 * blk), :]
        vb = vband[pl.ds(r0, 3 * blk), :]
        for pair in range(N_Q_HEADS // 2):
            acc = None
            for par in range(2):
                hd = 2 * pair + par
                var = 2 * (hd // (N_Q_HEADS // N_KV_HEADS)) + par
                qh = qj[:, pair * LANES:(pair + 1) * LANES]
                s_loc = lax.dot_general(qh, kb[:, var * LANES:(var + 1) * LANES], nt,
                                        preferred_element_type=F32)
                s_ctx = lax.dot_general(qh, kcx_ref[:, var * LANES:(var + 1) * LANES], nt,
                                        preferred_element_type=F32)
                s_loc = jnp.where(mask, s_loc, NEG_INF)
                sk = sink_ref[hd]
                m = jnp.maximum(jnp.maximum(jnp.max(s_loc, axis=-1, keepdims=True),
                                            jnp.max(s_ctx, axis=-1, keepdims=True)), sk)
                p_loc = jnp.exp(s_loc - m)
                p_ctx = jnp.exp(s_ctx - m)
                den = (jnp.sum(p_loc, axis=-1, keepdims=True) + jnp.sum(p_ctx, axis=-1, keepdims=True)
                       + jnp.exp(sk - m))
                o = (jnp.dot(p_ctx.astype(BF16), vcx_ref[:, var * LANES:(var + 1) * LANES],
                             preferred_element_type=F32)
                     + jnp.dot(p_loc.astype(BF16), vb[:, var * LANES:(var + 1) * LANES],
                               preferred_element_type=F32))
                o = o / den
                acc = o if acc is None else acc + o
            mixin[pl.ds(r0, blk), pair * LANES:(pair + 1) * LANES] = acc.astype(BF16)

        ug = _gelu(u_ref[pl.ds(r0, blk), :])
        zg = _gelu(z_ref[pl.ds(r0, blk), :])
        mu = jnp.mean(zg, axis=-1, keepdims=True)
        zc = zg - mu
        zn = zc * lax.rsqrt(jnp.mean(zc * zc, axis=-1, keepdims=True) + EPS) * gsgu_ref[...]
        for pair in range(SG_GROUPS // 2):
            zp = zn[:, pair * LANES:(pair + 1) * LANES]
            zero = jnp.zeros_like(zp)
            lo = jnp.where(lane_low, zp, zero).astype(BF16)
            hi = jnp.where(lane_low, zero, zp).astype(BF16)
            sg = (jnp.dot(wsp_ref[2 * pair], lo, preferred_element_type=F32)
                  + jnp.dot(wsp_ref[2 * pair + 1], hi, preferred_element_type=F32)
                  + bsp_ref[:, pair * LANES:(pair + 1) * LANES])
            mixin[pl.ds(r0, blk), Q_DIM + pair * LANES:Q_DIM + (pair + 1) * LANES] = (
                ug[:, pair * LANES:(pair + 1) * LANES] * sg).astype(BF16)
        return carry

    lax.fori_loop(0, nsub, sub_block, 0)
    mix = jnp.dot(mixin[...], wout_ref[...], preferred_element_type=F32)
    o_ref[...] = lat_ref[...] + gate_ref[...] * mix


def _even_mix(lat, q, kx, vx, kcx, vcx, u, z, sink, g_sgu, wsp_bf, bsp_full, wout_bf, mods, seq, ctx_len):
    n, d = lat.shape
    tq = ATT_TQ
    tiles_per_seq = seq // tq
    sub = tq // ATT_BLOCK
    nblk = n // ATT_BLOCK
    wide = 4 * LANES
    main = lambda w: pl.BlockSpec((tq, w), lambda i: (i, 0))
    prev = pl.BlockSpec((ATT_BLOCK, wide), lambda i: (jnp.maximum(i * sub - 1, 0), 0))
    nxt = pl.BlockSpec((ATT_BLOCK, wide), lambda i: (jnp.minimum((i + 1) * sub, nblk - 1), 0))
    ctxs = pl.BlockSpec((ctx_len, wide), lambda i: (i // tiles_per_seq, 0))
    const = lambda shape: pl.BlockSpec(shape, lambda i: (0,) * len(shape), pipeline_mode=pl.Buffered(1))
    return pl.pallas_call(
        functools.partial(_even_mix_kernel, tiles_per_seq=tiles_per_seq),
        grid=(n // tq,),
        in_specs=[
            pl.BlockSpec(memory_space=pltpu.SMEM),
            main(d), main(Q_DIM),
            main(wide), prev, nxt,
            main(wide), prev, nxt,
            ctxs, ctxs,
            main(SG_WIDTH), main(SG_WIDTH),
            const((1, SG_WIDTH)), const(wsp_bf.shape), const(bsp_full.shape), const(wout_bf.shape),
            pl.BlockSpec((None, 1, d), lambda i: (i // tiles_per_seq, 0, 2)),
        ],
        out_specs=pl.BlockSpec((tq, d), lambda i: (i, 0)),
        out_shape=jax.ShapeDtypeStruct((n, d), F32),
        scratch_shapes=[
            pltpu.VMEM((tq + 2 * ATT_BLOCK, wide), BF16),
            pltpu.VMEM((tq + 2 * ATT_BLOCK, wide), BF16),
            pltpu.VMEM((tq, Q_DIM + SG_WIDTH), BF16),
        ],
        compiler_params=_params(("parallel",)),
        name="even_mix",
    )(sink, lat, q, kx, kx, kx, vx, vx, vx, kcx, vcx, u, z, g_sgu, wsp_bf, bsp_full, wout_bf, mods)


def _odd_mix_kernel(x_ref, xp_ref, sh_ref, sc_ref, gate_ref, g_ref, win_ref, cw_ref, wout_ref,
                    o_ref, y_s, bg_s, tail_s, *, tiles_per_seq):
    i = pl.program_id(0)
    n_tiles = pl.num_programs(0) - 1
    tm, d = x_ref.shape
    cur = i % 2
    prv = 1 - cur

    tail_s[...] = y_s[cur, tm - 8:tm, :]

    @pl.when(i < n_tiles)
    def _():
        h = _rms_mod(x_ref[...], g_ref[...], sh_ref[...], sc_ref[...])
        p = jnp.dot(h.astype(BF16), win_ref[...], preferred_element_type=F32)
        bg_s[cur] = p[:, 0:d]
        y_s[cur] = p[:, d:2 * d] * p[:, 2 * d:3 * d]

    @pl.when(i >= 1)
    def _():
        t_prev = i - 1
        first = (t_prev % tiles_per_seq) == 0
        last = (t_prev % tiles_per_seq) == tiles_per_seq - 1
        y = y_s[prv]
        left = jnp.where(first, 0.0, tail_s[7:8, :])
        right = jnp.where(last, 0.0, y_s[cur, 0:1, :])
        ridx = lax.broadcasted_iota(jnp.int32, (tm, d), 0)
        y_dn = jnp.where(ridx == 0, left, pltpu.roll(y, 1, 0))
        y_up = jnp.where(ridx == tm - 1, right, pltpu.roll(y, tm - 1, 0))
        conv = y_dn * cw_ref[0:1, :] + y * cw_ref[1:2, :] + y_up * cw_ref[2:3, :]
        mix = jnp.dot((bg_s[prv] * conv).astype(BF16), wout_ref[...], preferred_element_type=F32)
        o_ref[...] = xp_ref[...] + gate_ref[...] * mix


def _odd_mix(lat, mods, layer, g, win_bf, conv_w8, wout_bf, seq):
    n, d = lat.shape
    tm = ODD_TM
    nt = n // tm
    tiles_per_seq = seq // tm
    cur = lambda i: jnp.minimum(i, nt - 1)
    prv = lambda i: jnp.maximum(i - 1, 0)
    row = lambda t: layer * MOD_ROWS + t // tiles_per_seq
    const = lambda shape: pl.BlockSpec(shape, lambda i: (0,) * len(shape), pipeline_mode=pl.Buffered(1))
    return pl.pallas_call(
        functools.partial(_odd_mix_kernel, tiles_per_seq=tiles_per_seq),
        grid=(nt + 1,),
        in_specs=[
            pl.BlockSpec((tm, d), lambda i: (cur(i), 0)),
            pl.BlockSpec((tm, d), lambda i: (prv(i), 0)),
            pl.BlockSpec((None, 1, d), lambda i: (row(cur(i)), 0, 0)),
            pl.BlockSpec((None, 1, d), lambda i: (row(cur(i)), 0, 1)),
            pl.BlockSpec((None, 1, d), lambda i: (row(prv(i)), 0, 2)),
            const((1, d)), const(win_bf.shape), const(conv_w8.shape), const(wout_bf.shape),
        ],
        out_specs=pl.BlockSpec((tm, d), lambda i: (prv(i), 0)),
        out_shape=jax.ShapeDtypeStruct((n, d), F32),
        scratch_shapes=[
            pltpu.VMEM((2, tm, d), F32),
            pltpu.VMEM((2, tm, d), F32),
            pltpu.VMEM((8, d), F32),
        ],
        compiler_params=_params(("arbitrary",)),
        name="odd_mix",
    )(lat, lat, mods, mods, mods, g, win_bf, conv_w8, wout_bf)


def _router_kernel(x_ref, sh_ref, sc_ref, g_ref, wr_ref, br_ref,
                   h_ref, mi_ref, wc_ref, cnt_ref, carry):
    i = pl.program_id(0)
    tm = x_ref.shape[0]
    epg = EXPERTS_PER_GROUP

    @pl.when(i == 0)
    def _():
        carry[...] = jnp.zeros_like(carry)

    h = _rms_mod(x_ref[...], g_ref[...], sh_ref[...], sc_ref[...])
    h_ref[...] = _pack_rows(h)
    lg = lax.dot_general(wr_ref[...], h, (((1,), (1,)), ((), ())),
                         preferred_element_type=F32, precision=HIGHEST) + br_ref[...]
    io8 = lax.broadcasted_iota(jnp.int32, (epg, tm), 0)
    gl = lg[0:epg]
    gmax = jnp.max(gl, axis=0, keepdims=True)
    g_idx = jnp.min(jnp.where(gl == gmax, io8, epg), axis=0, keepdims=True)
    g_w = 1.0 / jnp.sum(jnp.exp(gl - gmax), axis=0, keepdims=True)
    e_sel = lg[epg:2 * epg]
    for gi in range(1, N_GROUPS):
        e_sel = jnp.where(g_idx == gi, lg[(gi + 1) * epg:(gi + 2) * epg], e_sel)
    v0 = jnp.max(e_sel, axis=0, keepdims=True)
    i0 = jnp.min(jnp.where(e_sel == v0, io8, epg), axis=0, keepdims=True)
    rest = jnp.where(io8 == i0, -jnp.inf, e_sel)
    v1 = jnp.max(rest, axis=0, keepdims=True)
    i1 = jnp.min(jnp.where(rest == v1, io8, epg), axis=0, keepdims=True)
    t = jnp.exp(v1 - v0)
    w0 = g_w / (1.0 + t)
    w1 = g_w * t / (1.0 + t)
    e0 = g_idx * epg + i0
    e1 = g_idx * epg + i1

    io32 = lax.broadcasted_iota(jnp.int32, (N_EXPERTS, tm), 0)
    hit0 = io32 == e0
    hit1 = io32 == e1
    onehot = jnp.where(hit0 | hit1, 1.0, 0.0)
    r_i = lax.broadcasted_iota(jnp.int32, (tm, tm), 0)
    c_i = lax.broadcasted_iota(jnp.int32, (tm, tm), 1)
    upper = jnp.where(r_i < c_i, 1.0, 0.0).astype(BF16)
    cum = jnp.dot(onehot.astype(BF16), upper, preferred_element_type=F32) + carry[...]
    rank0 = jnp.sum(jnp.where(hit0, cum, 0.0), axis=0, keepdims=True).astype(jnp.int32)
    rank1 = jnp.sum(jnp.where(hit1, cum, 0.0), axis=0, keepdims=True).astype(jnp.int32)
    carry[...] = carry[...] + jnp.sum(onehot, axis=1, keepdims=True)
    cnt_ref[...] = jnp.broadcast_to(carry[...], cnt_ref.shape)

    mi_ref[...] = jnp.where(io8 == 0, e0, jnp.where(io8 == 1, e1, jnp.where(io8 == 2, rank0,
                            jnp.where(io8 == 3, rank1, 0))))
    io128 = lax.broadcasted_iota(jnp.int32, (LANES, tm), 0)
    wrow = jnp.where(io128 == 0, w0, jnp.where(io128 == 1, w1, 0.0))
    wc_ref[...] = wrow.T


def _router(lat, mods, layer, g, wr_t, br_t, seq):
    n, d = lat.shape
    tm = ROUTE_TM
    tiles_per_seq = seq // tm
    row = lambda i: layer * MOD_ROWS + i // tiles_per_seq
    const = lambda shape: pl.BlockSpec(shape, lambda i: (0,) * len(shape), pipeline_mode=pl.Buffered(1))
    return pl.pallas_call(
        _router_kernel,
        grid=(n // tm,),
        in_specs=[
            pl.BlockSpec((tm, d), lambda i: (i, 0)),
            pl.BlockSpec((None, 1, d), lambda i: (row(i), 0, 3)),
            pl.BlockSpec((None, 1, d), lambda i: (row(i), 0, 4)),
            const((1, d)), const(wr_t.shape), const(br_t.shape),
        ],
        out_specs=[
            pl.BlockSpec((tm, d // 2), lambda i: (i, 0)),
            pl.BlockSpec((8, tm), lambda i: (0, i)),
            pl.BlockSpec((tm, LANES), lambda i: (i, 0)),
            pl.BlockSpec((N_EXPERTS, LANES), lambda i: (0, 0)),
        ],
        out_shape=[
            jax.ShapeDtypeStruct((n, d // 2), jnp.uint32),
            jax.ShapeDtypeStruct((8, n), jnp.int32),
            jax.ShapeDtypeStruct((n, LANES), F32),
            jax.ShapeDtypeStruct((N_EXPERTS, LANES), F32),
        ],
        scratch_shapes=[pltpu.VMEM((N_EXPERTS, 1), F32)],
        compiler_params=_params(("arbitrary",)),
        name="router",
    )(lat, mods, mods, g, wr_t, br_t)


def _plan_kernel(cnt_ref, mi_ref, dest_ref, be_ref, nv_ref, nu_ref, ps_ref, *, n_blocks):
    bm = MOE_BM

    def per_expert(e, blk0):
        cnt = cnt_ref[e]
        nb = (cnt + bm - 1) // bm
        ps_ref[e] = blk0 * bm

        def fill(b, c):
            be_ref[b] = e
            nv_ref[b] = jnp.minimum(cnt - (b - blk0) * bm, bm)
            return c

        lax.fori_loop(blk0, blk0 + nb, fill, 0)
        return blk0 + nb

    n_used = lax.fori_loop(0, N_EXPERTS, per_expert, 0)
    nu_ref[0] = n_used
    last_e = be_ref[jnp.maximum(n_used - 1, 0)]

    def fill_tail(b, c):
        be_ref[b] = last_e
        nv_ref[b] = 0
        return c

    lax.fori_loop(n_used, n_blocks, fill_tail, 0)

    e01 = mi_ref[0:2, :]
    dest = mi_ref[2:4, :]
    for e in range(N_EXPERTS):
        dest = dest + jnp.where(e01 == e, ps_ref[e], 0)
    dest_ref[...] = dest


def _plan(counts, meta_i, n_blocks):
    n = meta_i.shape[1]
    return pl.pallas_call(
        functools.partial(_plan_kernel, n_blocks=n_blocks),
        in_specs=[pl.BlockSpec(memory_space=pltpu.SMEM), pl.BlockSpec(memory_space=pltpu.VMEM)],
        out_specs=[pl.BlockSpec(memory_space=pltpu.VMEM), pl.BlockSpec(memory_space=pltpu.SMEM),
                   pl.BlockSpec(memory_space=pltpu.SMEM), pl.BlockSpec(memory_space=pltpu.SMEM)],
        out_shape=[
            jax.ShapeDtypeStruct((2, n), jnp.int32),
            jax.ShapeDtypeStruct((n_blocks,), jnp.int32),
            jax.ShapeDtypeStruct((n_blocks,), jnp.int32),
            jax.ShapeDtypeStruct((1,), jnp.int32),
        ],
        scratch_shapes=[pltpu.SMEM((N_EXPERTS,), jnp.int32)],
        compiler_params=pltpu.CompilerParams(vmem_limit_bytes=VMEM_LIMIT),
        name="plan",
    )(counts, meta_i)


def _sc_mesh():
    return plsc.VectorSubcoreMesh(core_axis_name="c", subcore_axis_name="s",
                                  num_cores=SC_CORES, num_subcores=SC_SUBCORES)


def _sc_worker():
    return lax.axis_index("s") * SC_CORES + lax.axis_index("c")


def _sc_dispatch(h2, dest, n_rows):
    n, d = h2.shape
    c = SC_CHUNK
    per_w = n // SC_WORKERS
    nchunk = per_w // c
    idx = dest.reshape(2, SC_WORKERS, nchunk, c)

    @functools.partial(
        pl.kernel, mesh=_sc_mesh(), out_type=jax.ShapeDtypeStruct((n_rows, d), h2.dtype),
        scratch_types=[pltpu.VMEM((nchunk, c), jnp.int32), pltpu.VMEM((nchunk, c), jnp.int32),
                       pltpu.VMEM((2, c, d), h2.dtype),
                       pltpu.SemaphoreType.DMA((2,)), pltpu.SemaphoreType.DMA((2,))])
    def k(h_hbm, idx_hbm, xb_hbm, idx0_v, idx1_v, rows_v, gsem, ssem):
        wid = _sc_worker()
        base = wid * per_w
        idx_v = (idx0_v, idx1_v)
        for kk in range(2):
            pltpu.sync_copy(idx_hbm.at[kk, wid], idx_v[kk])

        def get(j, slot):
            return pltpu.make_async_copy(h_hbm.at[pl.ds(base + j * c, c)], rows_v.at[slot], gsem.at[slot])

        def put(j, slot, kk):
            return pltpu.make_async_copy(rows_v.at[slot], xb_hbm.at[idx_v[kk].at[j]], ssem.at[slot])

        get(0, 0).start()

        @pl.loop(0, nchunk, step=2)
        def _(j):
            for slot in range(2):
                jj = j + slot
                get(jj, slot).wait()

                @pl.when(jj >= 1)
                def _():
                    for kk in range(2):
                        put(jj - 1, 1 - slot, kk).wait()

                @pl.when(jj + 1 < nchunk)
                def _():
                    get(jj + 1, 1 - slot).start()

                for kk in range(2):
                    put(jj, slot, kk).start()

        for kk in range(2):
            put(nchunk - 1, (nchunk - 1) % 2, kk).wait()

    return k(h2, idx)


def _sc_gather(y, dest):
    d = y.shape[1]
    total = dest.shape[0] * dest.shape[1]
    c = SC_CHUNK
    per_w = total // SC_WORKERS
    nchunk = per_w // c
    idx = dest.reshape(SC_WORKERS, nchunk, c)

    @functools.partial(
        pl.kernel, mesh=_sc_mesh(), out_type=jax.ShapeDtypeStruct((total, d), y.dtype),
        scratch_types=[pltpu.VMEM((nchunk, c), jnp.int32), pltpu.VMEM((2, c, d), y.dtype),
                       pltpu.SemaphoreType.DMA((2,)), pltpu.SemaphoreType.DMA((2,))])
    def k(y_hbm, idx_hbm, out_hbm, idx_v, rows_v, gsem, ssem):
        wid = _sc_worker()
        base = wid * per_w
        pltpu.sync_copy(idx_hbm.at[wid], idx_v)

        def get(j, slot):
            return pltpu.make_async_copy(y_hbm.at[idx_v.at[j]], rows_v.at[slot], gsem.at[slot])

        def put(j, slot):
            return pltpu.make_async_copy(rows_v.at[slot], out_hbm.at[pl.ds(base + j * c, c)], ssem.at[slot])

        get(0, 0).start()

        @pl.loop(0, nchunk, step=2)
        def _(j):
            for slot in range(2):
                jj = j + slot
                get(jj, slot).wait()

                @pl.when(jj >= 1)
                def _():
                    put(jj - 1, 1 - slot).wait()

                @pl.when(jj + 1 < nchunk)
                def _():
                    get(jj + 1, 1 - slot).start()

                put(jj, slot).start()

        put(nchunk - 1, (nchunk - 1) % 2).wait()

    return k(y, idx)


def _expert_kernel(be_ref, nv_ref, nu_ref, x_ref, wg_ref, wu_ref, wd_ref, y_ref, wgu_s, wd_s):
    b = pl.program_id(0)
    hid = wg_ref.shape[2]
    changed = jnp.logical_or(b == 0, be_ref[b] != be_ref[jnp.maximum(b - 1, 0)])

    @pl.when(changed)
    def _():
        wgu_s[:, 0:hid] = wg_ref[0].astype(BF16)
        wgu_s[:, hid:2 * hid] = wu_ref[0].astype(BF16)
        wd_s[...] = wd_ref[0].astype(BF16)

    @pl.when(b < nu_ref[0])
    def _():
        live = lax.broadcasted_iota(jnp.int32, x_ref.shape, 0) < nv_ref[b]
        x = _unpack_rows(jnp.where(live, x_ref[...], jnp.uint32(0)))
        gu = jnp.dot(x.astype(BF16), wgu_s[...], preferred_element_type=F32)
        gate = gu[:, 0:hid]
        act = gate * (1.0 / (1.0 + jnp.exp(-gate))) * gu[:, hid:2 * hid]
        y_ref[...] = _pack_rows(jnp.dot(act.astype(BF16), wd_s[...], preferred_element_type=F32))

    @pl.when(b >= nu_ref[0])
    def _():
        y_ref[...] = jnp.zeros_like(y_ref)


def _experts(block_e, n_valid, n_used, xb, w_gate, w_up, w_down, layer):
    n_rows, dp = xb.shape
    d, hid = w_gate.shape[2], w_gate.shape[3]
    bm = MOE_BM
    n_blocks = n_rows // bm
    blk = lambda b, be, nv, nu: (jnp.minimum(b, nu[0] - 1), 0)
    wsel = lambda b, be, nv, nu: (layer, be[b], 0, 0)
    return pl.pallas_call(
        _expert_kernel,
        grid_spec=pltpu.PrefetchScalarGridSpec(
            num_scalar_prefetch=3,
            grid=(n_blocks,),
            in_specs=[
                pl.BlockSpec((bm, dp), blk),
                pl.BlockSpec((None, 1, d, hid), wsel),
                pl.BlockSpec((None, 1, d, hid), wsel),
                pl.BlockSpec((None, 1, hid, d), wsel),
            ],
            out_specs=pl.BlockSpec((bm, dp), lambda b, be, nv, nu: (b, 0)),
            scratch_shapes=[pltpu.VMEM((d, 2 * hid), BF16), pltpu.VMEM((hid, d), BF16)],
        ),
        out_shape=jax.ShapeDtypeStruct((n_rows, dp), jnp.uint32),
        compiler_params=_params(("arbitrary",)),
        name="experts",
    )(block_e, n_valid, n_used, xb, w_gate, w_up, w_down)


def _combine_kernel(lat_ref, y0_ref, y1_ref, wc_ref, gate_ref, gf_ref, o_ref, *, final):
    wc = wc_ref[...]
    moe = wc[:, 0:1] * _unpack_rows(y0_ref[...]) + wc[:, 1:2] * _unpack_rows(y1_ref[...])
    out = lat_ref[...] + gate_ref[...] * moe
    if final:
        ms = jnp.mean(out * out, axis=-1, keepdims=True)
        out = out * lax.rsqrt(ms + EPS) * gf_ref[...]
    o_ref[...] = out


def _combine(lat, yg, wcol, mods, layer, g_final, seq, final):
    n, d = lat.shape
    tm = COMBINE_TM
    nt = n // tm
    tiles_per_seq = seq // tm
    row = lambda i: layer * MOD_ROWS + i // tiles_per_seq
    return pl.pallas_call(
        functools.partial(_combine_kernel, final=final),
        grid=(nt,),
        in_specs=[
            pl.BlockSpec((tm, d), lambda i: (i, 0)),
            pl.BlockSpec((tm, d // 2), lambda i: (i, 0)),
            pl.BlockSpec((tm, d // 2), lambda i: (nt + i, 0)),
            pl.BlockSpec((tm, LANES), lambda i: (i, 0)),
            pl.BlockSpec((None, 1, d), lambda i: (row(i), 0, 5)),
            pl.BlockSpec((1, d), lambda i: (0, 0)),
        ],
        out_specs=pl.BlockSpec((tm, d), lambda i: (i, 0)),
        out_shape=jax.ShapeDtypeStruct((n, d), F32),
        compiler_params=_params(("parallel",)),
        name="combine",
    )(lat, yg, yg, wcol, mods, g_final)


def _moe(lat, mods, layer, g2, wr_t, br_t, w_gate, w_up, w_down, g_final, seq, final):
    n, d = lat.shape
    n_blocks = (2 * n) // MOE_BM + N_EXPERTS
    h2, meta_i, wcol, counts = _router(lat, mods, layer, g2, wr_t, br_t, seq)
    dest, block_e, n_valid, n_used = _plan(counts[:, 0].astype(jnp.int32), meta_i, n_blocks)
    xb = _sc_dispatch(h2, dest, n_blocks * MOE_BM)
    yb = _experts(block_e, n_valid, n_used, xb, w_gate, w_up, w_down, layer)
    yg = _sc_gather(yb, dest)
    return _combine(lat, yg, wcol, mods, layer, g_final, seq, final)


def _rope_tables(seq):
    quarter = HEAD_DIM // 4
    pos = jnp.arange(seq, dtype=F32)
    row_ids = jnp.floor(pos / GRID_W)
    col_ids = pos - row_ids * GRID_W
    inv = ROPE_BASE ** (-jnp.arange(quarter, dtype=F32) / quarter)
    ang_r = row_ids[:, None] * inv
    ang_c = col_ids[:, None] * inv
    zero = jnp.zeros_like(ang_r)
    cos = jnp.concatenate([jnp.cos(ang_r), jnp.cos(ang_r), jnp.cos(ang_c), jnp.cos(ang_c)], axis=-1)
    sa = jnp.concatenate([-jnp.sin(ang_r), zero, -jnp.sin(ang_c), zero], axis=-1)
    sb = jnp.concatenate([zero, jnp.sin(ang_r), zero, jnp.sin(ang_c)], axis=-1)
    rep = LANES // HEAD_DIM
    return tuple(jnp.tile(t, (1, rep)) for t in (cos, sa, sb))


def _router_weights(w_rg, b_rg, w_re, b_re):
    d = w_rg.shape[0]
    pad = EXPERTS_PER_GROUP - N_GROUPS
    wr_t = jnp.concatenate([w_rg.T, jnp.zeros((pad, d), F32), w_re.T], axis=0)
    br_t = jnp.concatenate([b_rg, jnp.full((pad,), NEG_INF, F32), b_re])[:, None]
    return wr_t, br_t


def kernel(x, c, ctx, c_ctx, w_ada, b_ada, g_norm1, g_norm2, g_final, w_in_even, attn_sink, g_sgu,
           w_spatial, b_spatial, w_out_even, w_in_odd, conv_w, w_out_odd, w_router_group,
           b_router_group, w_router_expert, b_router_expert, w_gate, w_up, w_down):
    b, s, d = x.shape
    n = b * s
    n_ctx = ctx.shape[1]
    depth = w_ada.shape[0]
    assert depth == 2 and b + 1 <= MOD_ROWS

    cond = jnp.concatenate([c, c_ctx[None, :], jnp.zeros((MOD_ROWS - b - 1, d), F32)], axis=0)
    mods = _ada(cond, w_ada, b_ada).reshape(depth * MOD_ROWS, 1, 6 * d)
    gf = g_final[None, :]

    lat = x.reshape(n, d)
    w_in_bf = w_in_even[0].astype(BF16)
    tabs = _rope_tables(s)
    q, kx, vx, u, z = _even_in(lat, mods, 0, lambda i: i // (s // EVEN_TM), g_norm1[0][None, :], w_in_bf,
                               tabs, s // EVEN_TM, EVEN_TM)
    ones = jnp.ones((n_ctx, LANES), F32)
    zeros = jnp.zeros((n_ctx, LANES), F32)
    _, kcx, vcx, _, _ = _even_in(ctx.reshape(b * n_ctx, d), mods, 0, lambda i: b, g_norm1[0][None, :],
                                 w_in_bf, (ones, zeros, zeros), 1, n_ctx)
    bsp_full = jnp.repeat(b_spatial[0].T, HEAD_DIM, axis=1)
    lat = _even_mix(lat, q, kx, vx, kcx, vcx, u, z, attn_sink[0], g_sgu[0][None, :],
                    w_spatial[0].astype(BF16), bsp_full, w_out_even[0].astype(BF16), mods, s, n_ctx)
    wr_t, br_t = _router_weights(w_router_group[0], b_router_group[0], w_router_expert[0], b_router_expert[0])
    lat = _moe(lat, mods, 0, g_norm2[0][None, :], wr_t, br_t, w_gate, w_up, w_down, gf, s, False)

    conv_w8 = jnp.concatenate([conv_w[0], jnp.zeros((8 - conv_w.shape[1], d), F32)], axis=0)
    lat = _odd_mix(lat, mods, 1, g_norm1[1][None, :], w_in_odd[0].astype(BF16), conv_w8,
                   w_out_odd[0].astype(BF16), s)
    wr_t, br_t = _router_weights(w_router_group[1], b_router_group[1], w_router_expert[1], b_router_expert[1])
    out = _moe(lat, mods, 1, g_norm2[1][None, :], wr_t, br_t, w_gate, w_up, w_down, gf, s, True)
    return out.reshape(b, s, d)
```

```python
import functools

import jax
import jax.numpy as jnp
from jax import lax
from jax.experimental import pallas as pl
from jax.experimental.pallas import tpu as pltpu
from jax.experimental.pallas import tpu_sc as plsc

F32 = jnp.float32
BF16 = jnp.bfloat16
HIGHEST = lax.Precision.HIGHEST

GRID_W = 64
N_Q_HEADS = 8
N_KV_HEADS = 2
HEAD_DIM = 64
ATT_BLOCK = 128
ROPE_BASE = 10000.0
Q_DIM = N_Q_HEADS * HEAD_DIM
KV_DIM = N_KV_HEADS * HEAD_DIM
SG_GROUPS = 8
SG_WIDTH = SG_GROUPS * HEAD_DIM
N_GROUPS = 4
EXPERTS_PER_GROUP = 8
N_EXPERTS = N_GROUPS * EXPERTS_PER_GROUP
EPS = 1e-6
NEG_INF = -1e30

LANES = 128
SC_CORES = 2
SC_SUBCORES = 16
SC_WORKERS = SC_CORES * SC_SUBCORES
SC_CHUNK = 32
MOD_ROWS = 8
VMEM_LIMIT = 56 * 1024 * 1024

ADA_TN = 1536
EVEN_TM = 512
ATT_TQ = 512
ODD_TM = 512
ROUTE_TM = 512
MOE_BM = 512
MOE_PARTS = 2
COMBINE_TM = 512


def _params(sem):
    return pltpu.CompilerParams(dimension_semantics=sem, vmem_limit_bytes=VMEM_LIMIT)


def _rms_mod(x, g, shift, scale):
    ms = jnp.mean(x * x, axis=-1, keepdims=True)
    return (x * lax.rsqrt(ms + EPS) * g) * (1.0 + scale) + shift


def _pack_rows(a):
    w = a.shape[1] // 2
    hi = pltpu.bitcast(a[:, :w].astype(BF16).astype(F32), jnp.uint32)
    lo = pltpu.bitcast(a[:, w:].astype(BF16).astype(F32), jnp.uint32)
    return hi | (lo >> 16)


def _unpack_rows(p):
    hi = pltpu.bitcast(p & jnp.uint32(0xFFFF0000), F32)
    lo = pltpu.bitcast(p << 16, F32)
    return jnp.concatenate([hi, lo], axis=1)


def _ada_kernel(a_ref, w_ref, b_ref, o_ref):
    a = a_ref[...]
    s = a * (1.0 / (1.0 + jnp.exp(-a)))
    o_ref[0] = jnp.dot(s, w_ref[0], preferred_element_type=F32, precision=HIGHEST) + b_ref[0]


def _ada(cond, w_ada, b_ada):
    depth, d, six_d = w_ada.shape
    return pl.pallas_call(
        _ada_kernel,
        grid=(depth, six_d // ADA_TN),
        in_specs=[
            pl.BlockSpec((MOD_ROWS, d), lambda l, j: (0, 0)),
            pl.BlockSpec((1, d, ADA_TN), lambda l, j: (l, 0, j)),
            pl.BlockSpec((1, 1, ADA_TN), lambda l, j: (l, 0, j)),
        ],
        out_specs=pl.BlockSpec((1, MOD_ROWS, ADA_TN), lambda l, j: (l, 0, j)),
        out_shape=jax.ShapeDtypeStruct((depth, MOD_ROWS, six_d), F32),
        compiler_params=_params(("arbitrary", "arbitrary")),
        name="ada",
    )(cond, w_ada, b_ada.reshape(depth, 1, six_d))


def _even_in_kernel(x_ref, sh_ref, sc_ref, g_ref, w_ref, cos_ref, sa_ref, sb_ref,
                    q_ref, kx_ref, vx_ref, u_ref, z_ref):
    h = _rms_mod(x_ref[...], g_ref[...], sh_ref[...], sc_ref[...])
    p = jnp.dot(h.astype(BF16), w_ref[...], preferred_element_type=F32)
    cos, sa, sb = cos_ref[...], sa_ref[...], sb_ref[...]

    def rope(t):
        return t * cos + pltpu.roll(t, LANES - 16, 1) * sa + pltpu.roll(t, 16, 1) * sb

    scale = HEAD_DIM ** -0.5
    for cblk in range(Q_DIM // LANES):
        t = p[:, cblk * LANES:(cblk + 1) * LANES]
        q_ref[:, cblk * LANES:(cblk + 1) * LANES] = (rope(t) * scale).astype(BF16)

    low = lax.broadcasted_iota(jnp.int32, (x_ref.shape[0], LANES), 1) < HEAD_DIM

    def spread(t, o_ref):
        sw = pltpu.roll(t, HEAD_DIM, 1)
        zero = jnp.zeros_like(t)
        o_ref[:, 0 * LANES:1 * LANES] = jnp.where(low, t, zero).astype(BF16)
        o_ref[:, 1 * LANES:2 * LANES] = jnp.where(low, zero, sw).astype(BF16)
        o_ref[:, 2 * LANES:3 * LANES] = jnp.where(low, sw, zero).astype(BF16)
        o_ref[:, 3 * LANES:4 * LANES] = jnp.where(low, zero, t).astype(BF16)

    spread(rope(p[:, Q_DIM:Q_DIM + KV_DIM]), kx_ref)
    spread(p[:, Q_DIM + KV_DIM:Q_DIM + 2 * KV_DIM], vx_ref)
    u0 = Q_DIM + 2 * KV_DIM
    u_ref[...] = p[:, u0:u0 + SG_WIDTH]
    z_ref[...] = p[:, u0 + SG_WIDTH:u0 + 2 * SG_WIDTH]


def _even_in(x2d, mods, layer, mod_row_fn, g, w_bf, tabs, tab_blocks, tm):
    n, d = x2d.shape
    ein = w_bf.shape[1]
    cos, sa, sb = tabs
    row = lambda i: layer * MOD_ROWS + mod_row_fn(i)
    tab_spec = pl.BlockSpec((tm, LANES), lambda i: (i % tab_blocks, 0))
    wide = 4 * LANES
    return pl.pallas_call(
        _even_in_kernel,
        grid=(n // tm,),
        in_specs=[
            pl.BlockSpec((tm, d), lambda i: (i, 0)),
            pl.BlockSpec((None, 1, d), lambda i: (row(i), 0, 0)),
            pl.BlockSpec((None, 1, d), lambda i: (row(i), 0, 1)),
            pl.BlockSpec((1, d), lambda i: (0, 0)),
            pl.BlockSpec((d, ein), lambda i: (0, 0)),
            tab_spec, tab_spec, tab_spec,
        ],
        out_specs=[
            pl.BlockSpec((tm, Q_DIM), lambda i: (i, 0)),
            pl.BlockSpec((tm, wide), lambda i: (i, 0)),
            pl.BlockSpec((tm, wide), lambda i: (i, 0)),
            pl.BlockSpec((tm, SG_WIDTH), lambda i: (i, 0)),
            pl.BlockSpec((tm, SG_WIDTH), lambda i: (i, 0)),
        ],
        out_shape=[
            jax.ShapeDtypeStruct((n, Q_DIM), BF16),
            jax.ShapeDtypeStruct((n, wide), BF16),
            jax.ShapeDtypeStruct((n, wide), BF16),
            jax.ShapeDtypeStruct((n, SG_WIDTH), F32),
            jax.ShapeDtypeStruct((n, SG_WIDTH), F32),
        ],
        compiler_params=_params(("parallel",)),
        name="even_in",
    )(x2d, mods, mods, g, w_bf, cos, sa, sb)


def _gelu(x):
    return 0.5 * x * (1.0 + lax.erf(x * (2.0 ** -0.5)))


def _even_mix_kernel(sink_ref, lat_ref, q_ref, kxm_ref, kxp_ref, kxn_ref, vxm_ref, vxp_ref, vxn_ref,
                     kcx_ref, vcx_ref, u_ref, z_ref, gsgu_ref, wsp_ref, bsp_ref, wout_ref, gate_ref,
                     o_ref, kband, vband, mixin, *, tiles_per_seq):
    i = pl.program_id(0)
    tq = q_ref.shape[0]
    blk = ATT_BLOCK
    nsub = tq // blk
    first = (i % tiles_per_seq) == 0
    last = (i % tiles_per_seq) == tiles_per_seq - 1

    kband[0:blk] = kxp_ref[...]
    kband[blk:blk + tq] = kxm_ref[...]
    kband[blk + tq:] = kxn_ref[...]
    vband[0:blk] = vxp_ref[...]
    vband[blk:blk + tq] = vxm_ref[...]
    vband[blk + tq:] = vxn_ref[...]

    row = lax.broadcasted_iota(jnp.int32, (blk, 3 * blk), 0)
    col = lax.broadcasted_iota(jnp.int32, (blk, 3 * blk), 1)
    lane_low = lax.broadcasted_iota(jnp.int32, (blk, LANES), 1) < HEAD_DIM
    nt = (((1,), (1,)), ((), ()))

    def sub_block(j, carry):
        r0 = pl.multiple_of(j * blk, blk)
        ok_prev = jnp.logical_not(jnp.logical_and(first, j == 0))
        ok_next = jnp.logical_not(jnp.logical_and(last, j == nsub - 1))
        mask = (((col >= blk) | ((col >= row) & ok_prev))
                & ((col < 2 * blk) | ((col - 2 * blk <= row) & ok_next)))
        qj = q_ref[pl.ds(r0, blk), :]
        kb = kband[pl.ds(r0, 3 * blk), :]
        vb = vband[pl.ds(r0, 3 * blk), :]
        for pair in range(N_Q_HEADS // 2):
            acc = None
            for par in range(2):
                hd = 2 * pair + par
                var = 2 * (hd // (N_Q_HEADS // N_KV_HEADS)) + par
                qh = qj[:, pair * LANES:(pair + 1) * LANES]
                s_loc = lax.dot_general(qh, kb[:, var * LANES:(var + 1) * LANES], nt,
                                        preferred_element_type=F32)
                s_ctx = lax.dot_general(qh, kcx_ref[:, var * LANES:(var + 1) * LANES], nt,
                                        preferred_element_type=F32)
                s_loc = jnp.where(mask, s_loc, NEG_INF)
                sk = sink_ref[hd]
                m = jnp.maximum(jnp.maximum(jnp.max(s_loc, axis=-1, keepdims=True),
                                            jnp.max(s_ctx, axis=-1, keepdims=True)), sk)
                p_loc = jnp.exp(s_loc - m)
                p_ctx = jnp.exp(s_ctx - m)
                den = (jnp.sum(p_loc, axis=-1, keepdims=True) + jnp.sum(p_ctx, axis=-1, keepdims=True)
                       + jnp.exp(sk - m))
                o = (jnp.dot(p_ctx.astype(BF16), vcx_ref[:, var * LANES:(var + 1) * LANES],
                             preferred_element_type=F32)
                     + jnp.dot(p_loc.astype(BF16), vb[:, var * LANES:(var + 1) * LANES],
                               preferred_element_type=F32))
                o = o / den
                acc = o if acc is None else acc + o
            mixin[pl.ds(r0, blk), pair * LANES:(pair + 1) * LANES] = acc.astype(BF16)

        ug = _gelu(u_ref[pl.ds(r0, blk), :])
        zg = _gelu(z_ref[pl.ds(r0, blk), :])
        mu = jnp.mean(zg, axis=-1, keepdims=True)
        zc = zg - mu
        zn = zc * lax.rsqrt(jnp.mean(zc * zc, axis=-1, keepdims=True) + EPS) * gsgu_ref[...]
        for pair in range(SG_GROUPS // 2):
            zp = zn[:, pair * LANES:(pair + 1) * LANES]
            zero = jnp.zeros_like(zp)
            lo = jnp.where(lane_low, zp, zero).astype(BF16)
            hi = jnp.where(lane_low, zero, zp).astype(BF16)
            sg = (jnp.dot(wsp_ref[2 * pair], lo, preferred_element_type=F32)
                  + jnp.dot(wsp_ref[2 * pair + 1], hi, preferred_element_type=F32)
                  + bsp_ref[:, pair * LANES:(pair + 1) * LANES])
            mixin[pl.ds(r0, blk), Q_DIM + pair * LANES:Q_DIM + (pair + 1) * LANES] = (
                ug[:, pair * LANES:(pair + 1) * LANES] * sg).astype(BF16)
        return carry

    lax.fori_loop(0, nsub, sub_block, 0)
    mix = jnp.dot(mixin[...], wout_ref[...], preferred_element_type=F32)
    o_ref[...] = lat_ref[...] + gate_ref[...] * mix


def _even_mix(lat, q, kx, vx, kcx, vcx, u, z, sink, g_sgu, wsp_bf, bsp_full, wout_bf, mods, seq, ctx_len):
    n, d = lat.shape
    tq = ATT_TQ
    tiles_per_seq = seq // tq
    sub = tq // ATT_BLOCK
    nblk = n // ATT_BLOCK
    wide = 4 * LANES
    main = lambda w: pl.BlockSpec((tq, w), lambda i: (i, 0))
    prev = pl.BlockSpec((ATT_BLOCK, wide), lambda i: (jnp.maximum(i * sub - 1, 0), 0))
    nxt = pl.BlockSpec((ATT_BLOCK, wide), lambda i: (jnp.minimum((i + 1) * sub, nblk - 1), 0))
    ctxs = pl.BlockSpec((ctx_len, wide), lambda i: (i // tiles_per_seq, 0))
    const = lambda shape: pl.BlockSpec(shape, lambda i: (0,) * len(shape), pipeline_mode=pl.Buffered(1))
    return pl.pallas_call(
        functools.partial(_even_mix_kernel, tiles_per_seq=tiles_per_seq),
        grid=(n // tq,),
        in_specs=[
            pl.BlockSpec(memory_space=pltpu.SMEM),
            main(d), main(Q_DIM),
            main(wide), prev, nxt,
            main(wide), prev, nxt,
            ctxs, ctxs,
            main(SG_WIDTH), main(SG_WIDTH),
            const((1, SG_WIDTH)), const(wsp_bf.shape), const(bsp_full.shape), const(wout_bf.shape),
            pl.BlockSpec((None, 1, d), lambda i: (i // tiles_per_seq, 0, 2)),
        ],
        out_specs=pl.BlockSpec((tq, d), lambda i: (i, 0)),
        out_shape=jax.ShapeDtypeStruct((n, d), F32),
        scratch_shapes=[
            pltpu.VMEM((tq + 2 * ATT_BLOCK, wide), BF16),
            pltpu.VMEM((tq + 2 * ATT_BLOCK, wide), BF16),
            pltpu.VMEM((tq, Q_DIM + SG_WIDTH), BF16),
        ],
        compiler_params=_params(("parallel",)),
        name="even_mix",
    )(sink, lat, q, kx, kx, kx, vx, vx, vx, kcx, vcx, u, z, g_sgu, wsp_bf, bsp_full, wout_bf, mods)


def _odd_mix_kernel(x_ref, xp_ref, sh_ref, sc_ref, gate_ref, g_ref, win_ref, cw_ref, wout_ref,
                    o_ref, y_s, bg_s, tail_s, *, tiles_per_seq):
    i = pl.program_id(0)
    n_tiles = pl.num_programs(0) - 1
    tm, d = x_ref.shape
    cur = i % 2
    prv = 1 - cur

    tail_s[...] = y_s[cur, tm - 8:tm, :]

    @pl.when(i < n_tiles)
    def _():
        h = _rms_mod(x_ref[...], g_ref[...], sh_ref[...], sc_ref[...])
        p = jnp.dot(h.astype(BF16), win_ref[...], preferred_element_type=F32)
        bg_s[cur] = p[:, 0:d]
        y_s[cur] = p[:, d:2 * d] * p[:, 2 * d:3 * d]

    @pl.when(i >= 1)
    def _():
        t_prev = i - 1
        first = (t_prev % tiles_per_seq) == 0
        last = (t_prev % tiles_per_seq) == tiles_per_seq - 1
        y = y_s[prv]
        left = jnp.where(first, 0.0, tail_s[7:8, :])
        right = jnp.where(last, 0.0, y_s[cur, 0:1, :])
        ridx = lax.broadcasted_iota(jnp.int32, (tm, d), 0)
        y_dn = jnp.where(ridx == 0, left, pltpu.roll(y, 1, 0))
        y_up = jnp.where(ridx == tm - 1, right, pltpu.roll(y, tm - 1, 0))
        conv = y_dn * cw_ref[0:1, :] + y * cw_ref[1:2, :] + y_up * cw_ref[2:3, :]
        mix = jnp.dot((bg_s[prv] * conv).astype(BF16), wout_ref[...], preferred_element_type=F32)
        o_ref[...] = xp_ref[...] + gate_ref[...] * mix


def _odd_mix(lat, mods, layer, g, win_bf, conv_w8, wout_bf, seq):
    n, d = lat.shape
    tm = ODD_TM
    nt = n // tm
    tiles_per_seq = seq // tm
    cur = lambda i: jnp.minimum(i, nt - 1)
    prv = lambda i: jnp.maximum(i - 1, 0)
    row = lambda t: layer * MOD_ROWS + t // tiles_per_seq
    const = lambda shape: pl.BlockSpec(shape, lambda i: (0,) * len(shape), pipeline_mode=pl.Buffered(1))
    return pl.pallas_call(
        functools.partial(_odd_mix_kernel, tiles_per_seq=tiles_per_seq),
        grid=(nt + 1,),
        in_specs=[
            pl.BlockSpec((tm, d), lambda i: (cur(i), 0)),
            pl.BlockSpec((tm, d), lambda i: (prv(i), 0)),
            pl.BlockSpec((None, 1, d), lambda i: (row(cur(i)), 0, 0)),
            pl.BlockSpec((None, 1, d), lambda i: (row(cur(i)), 0, 1)),
            pl.BlockSpec((None, 1, d), lambda i: (row(prv(i)), 0, 2)),
            const((1, d)), const(win_bf.shape), const(conv_w8.shape), const(wout_bf.shape),
        ],
        out_specs=pl.BlockSpec((tm, d), lambda i: (prv(i), 0)),
        out_shape=jax.ShapeDtypeStruct((n, d), F32),
        scratch_shapes=[
            pltpu.VMEM((2, tm, d), F32),
            pltpu.VMEM((2, tm, d), F32),
            pltpu.VMEM((8, d), F32),
        ],
        compiler_params=_params(("arbitrary",)),
        name="odd_mix",
    )(lat, lat, mods, mods, mods, g, win_bf, conv_w8, wout_bf)


def _router_kernel(x_ref, sh_ref, sc_ref, g_ref, wr_ref, br_ref,
                   h_ref, mi_ref, wc_ref, cnt_ref, carry):
    i = pl.program_id(0)
    tm = x_ref.shape[0]
    epg = EXPERTS_PER_GROUP

    @pl.when(i == 0)
    def _():
        carry[...] = jnp.zeros_like(carry)

    h = _rms_mod(x_ref[...], g_ref[...], sh_ref[...], sc_ref[...])
    h_ref[...] = _pack_rows(h)
    lg = lax.dot_general(wr_ref[...], h, (((1,), (1,)), ((), ())),
                         preferred_element_type=F32, precision=HIGHEST) + br_ref[...]
    io8 = lax.broadcasted_iota(jnp.int32, (epg, tm), 0)
    gl = lg[0:epg]
    gmax = jnp.max(gl, axis=0, keepdims=True)
    g_idx = jnp.min(jnp.where(gl == gmax, io8, epg), axis=0, keepdims=True)
    g_w = 1.0 / jnp.sum(jnp.exp(gl - gmax), axis=0, keepdims=True)
    e_sel = lg[epg:2 * epg]
    for gi in range(1, N_GROUPS):
        e_sel = jnp.where(g_idx == gi, lg[(gi + 1) * epg:(gi + 2) * epg], e_sel)
    v0 = jnp.max(e_sel, axis=0, keepdims=True)
    i0 = jnp.min(jnp.where(e_sel == v0, io8, epg), axis=0, keepdims=True)
    rest = jnp.where(io8 == i0, -jnp.inf, e_sel)
    v1 = jnp.max(rest, axis=0, keepdims=True)
    i1 = jnp.min(jnp.where(rest == v1, io8, epg), axis=0, keepdims=True)
    t = jnp.exp(v1 - v0)
    w0 = g_w / (1.0 + t)
    w1 = g_w * t / (1.0 + t)
    e0 = g_idx * epg + i0
    e1 = g_idx * epg + i1

    io32 = lax.broadcasted_iota(jnp.int32, (N_EXPERTS, tm), 0)
    hit0 = io32 == e0
    hit1 = io32 == e1
    onehot = jnp.where(hit0 | hit1, 1.0, 0.0)
    r_i = lax.broadcasted_iota(jnp.int32, (tm, tm), 0)
    c_i = lax.broadcasted_iota(jnp.int32, (tm, tm), 1)
    upper = jnp.where(r_i < c_i, 1.0, 0.0).astype(BF16)
    cum = jnp.dot(onehot.astype(BF16), upper, preferred_element_type=F32) + carry[...]
    rank0 = jnp.sum(jnp.where(hit0, cum, 0.0), axis=0, keepdims=True).astype(jnp.int32)
    rank1 = jnp.sum(jnp.where(hit1, cum, 0.0), axis=0, keepdims=True).astype(jnp.int32)
    carry[...] = carry[...] + jnp.sum(onehot, axis=1, keepdims=True)
    cnt_ref[...] = jnp.broadcast_to(carry[...], cnt_ref.shape)

    mi_ref[...] = jnp.where(io8 == 0, e0, jnp.where(io8 == 1, e1, jnp.where(io8 == 2, rank0,
                            jnp.where(io8 == 3, rank1, 0))))
    io128 = lax.broadcasted_iota(jnp.int32, (LANES, tm), 0)
    wrow = jnp.where(io128 == 0, w0, jnp.where(io128 == 1, w1, 0.0))
    wc_ref[...] = wrow.T


def _router(lat, mods, layer, g, wr_t, br_t, seq, tok0, n):
    d = lat.shape[1]
    tm = ROUTE_TM
    tiles_per_seq = seq // tm
    t0 = tok0 // tm
    row = lambda i: layer * MOD_ROWS + (t0 + i) // tiles_per_seq
    const = lambda shape: pl.BlockSpec(shape, lambda i: (0,) * len(shape), pipeline_mode=pl.Buffered(1))
    return pl.pallas_call(
        _router_kernel,
        grid=(n // tm,),
        in_specs=[
            pl.BlockSpec((tm, d), lambda i: (t0 + i, 0)),
            pl.BlockSpec((None, 1, d), lambda i: (row(i), 0, 3)),
            pl.BlockSpec((None, 1, d), lambda i: (row(i), 0, 4)),
            const((1, d)), const(wr_t.shape), const(br_t.shape),
        ],
        out_specs=[
            pl.BlockSpec((tm, d // 2), lambda i: (i, 0)),
            pl.BlockSpec((8, tm), lambda i: (0, i)),
            pl.BlockSpec((tm, LANES), lambda i: (i, 0)),
            pl.BlockSpec((N_EXPERTS, LANES), lambda i: (0, 0)),
        ],
        out_shape=[
            jax.ShapeDtypeStruct((n, d // 2), jnp.uint32),
            jax.ShapeDtypeStruct((8, n), jnp.int32),
            jax.ShapeDtypeStruct((n, LANES), F32),
            jax.ShapeDtypeStruct((N_EXPERTS, LANES), F32),
        ],
        scratch_shapes=[pltpu.VMEM((N_EXPERTS, 1), F32)],
        compiler_params=_params(("arbitrary",)),
        name="router",
    )(lat, mods, mods, g, wr_t, br_t)


def _plan_kernel(cnt_ref, mi_ref, dest_ref, be_ref, nv_ref, nu_ref, ps_ref, *, n_blocks):
    bm = MOE_BM

    def per_expert(e, blk0):
        cnt = cnt_ref[e]
        nb = (cnt + bm - 1) // bm
        ps_ref[e] = blk0 * bm

        def fill(b, c):
            be_ref[b] = e
            nv_ref[b] = jnp.minimum(cnt - (b - blk0) * bm, bm)
            return c

        lax.fori_loop(blk0, blk0 + nb, fill, 0)
        return blk0 + nb

    n_used = lax.fori_loop(0, N_EXPERTS, per_expert, 0)
    nu_ref[0] = n_used
    last_e = be_ref[jnp.maximum(n_used - 1, 0)]

    def fill_tail(b, c):
        be_ref[b] = last_e
        nv_ref[b] = 0
        return c

    lax.fori_loop(n_used, n_blocks, fill_tail, 0)

    e01 = mi_ref[0:2, :]
    dest = mi_ref[2:4, :]
    for e in range(N_EXPERTS):
        dest = dest + jnp.where(e01 == e, ps_ref[e], 0)
    dest_ref[...] = dest


def _plan(counts, meta_i, n_blocks):
    n = meta_i.shape[1]
    return pl.pallas_call(
        functools.partial(_plan_kernel, n_blocks=n_blocks),
        in_specs=[pl.BlockSpec(memory_space=pltpu.SMEM), pl.BlockSpec(memory_space=pltpu.VMEM)],
        out_specs=[pl.BlockSpec(memory_space=pltpu.VMEM), pl.BlockSpec(memory_space=pltpu.SMEM),
                   pl.BlockSpec(memory_space=pltpu.SMEM), pl.BlockSpec(memory_space=pltpu.SMEM)],
        out_shape=[
            jax.ShapeDtypeStruct((2, n), jnp.int32),
            jax.ShapeDtypeStruct((n_blocks,), jnp.int32),
            jax.ShapeDtypeStruct((n_blocks,), jnp.int32),
            jax.ShapeDtypeStruct((1,), jnp.int32),
        ],
        scratch_shapes=[pltpu.SMEM((N_EXPERTS,), jnp.int32)],
        compiler_params=pltpu.CompilerParams(vmem_limit_bytes=VMEM_LIMIT),
        name="plan",
    )(counts, meta_i)


def _sc_mesh():
    return plsc.VectorSubcoreMesh(core_axis_name="c", subcore_axis_name="s",
                                  num_cores=SC_CORES, num_subcores=SC_SUBCORES)


def _sc_worker():
    return lax.axis_index("s") * SC_CORES + lax.axis_index("c")


def _sc_dispatch(h2, dest, n_rows):
    n, d = h2.shape
    c = SC_CHUNK
    per_w = n // SC_WORKERS
    nchunk = per_w // c
    idx = dest.reshape(2, SC_WORKERS, nchunk, c)

    @functools.partial(
        pl.kernel, mesh=_sc_mesh(), out_type=jax.ShapeDtypeStruct((n_rows, d), h2.dtype),
        scratch_types=[pltpu.VMEM((nchunk, c), jnp.int32), pltpu.VMEM((nchunk, c), jnp.int32),
                       pltpu.VMEM((2, c, d), h2.dtype),
                       pltpu.SemaphoreType.DMA((2,)), pltpu.SemaphoreType.DMA((2,))])
    def k(h_hbm, idx_hbm, xb_hbm, idx0_v, idx1_v, rows_v, gsem, ssem):
        wid = _sc_worker()
        base = wid * per_w
        idx_v = (idx0_v, idx1_v)
        for kk in range(2):
            pltpu.sync_copy(idx_hbm.at[kk, wid], idx_v[kk])

        def get(j, slot):
            return pltpu.make_async_copy(h_hbm.at[pl.ds(base + j * c, c)], rows_v.at[slot], gsem.at[slot])

        def put(j, slot, kk):
            return pltpu.make_async_copy(rows_v.at[slot], xb_hbm.at[idx_v[kk].at[j]], ssem.at[slot])

        get(0, 0).start()

        @pl.loop(0, nchunk, step=2)
        def _(j):
            for slot in range(2):
                jj = j + slot
                get(jj, slot).wait()

                @pl.when(jj >= 1)
                def _():
                    for kk in range(2):
                        put(jj - 1, 1 - slot, kk).wait()

                @pl.when(jj + 1 < nchunk)
                def _():
                    get(jj + 1, 1 - slot).start()

                for kk in range(2):
                    put(jj, slot, kk).start()

        for kk in range(2):
            put(nchunk - 1, (nchunk - 1) % 2, kk).wait()

    return k(h2, idx)


def _sc_gather(y, dest):
    d = y.shape[1]
    total = dest.shape[0] * dest.shape[1]
    c = SC_CHUNK
    per_w = total // SC_WORKERS
    nchunk = per_w // c
    idx = dest.reshape(SC_WORKERS, nchunk, c)

    @functools.partial(
        pl.kernel, mesh=_sc_mesh(), out_type=jax.ShapeDtypeStruct((total, d), y.dtype),
        scratch_types=[pltpu.VMEM((nchunk, c), jnp.int32), pltpu.VMEM((2, c, d), y.dtype),
                       pltpu.SemaphoreType.DMA((2,)), pltpu.SemaphoreType.DMA((2,))])
    def k(y_hbm, idx_hbm, out_hbm, idx_v, rows_v, gsem, ssem):
        wid = _sc_worker()
        base = wid * per_w
        pltpu.sync_copy(idx_hbm.at[wid], idx_v)

        def get(j, slot):
            return pltpu.make_async_copy(y_hbm.at[idx_v.at[j]], rows_v.at[slot], gsem.at[slot])

        def put(j, slot):
            return pltpu.make_async_copy(rows_v.at[slot], out_hbm.at[pl.ds(base + j * c, c)], ssem.at[slot])

        get(0, 0).start()

        @pl.loop(0, nchunk, step=2)
        def _(j):
            for slot in range(2):
                jj = j + slot
                get(jj, slot).wait()

                @pl.when(jj >= 1)
                def _():
                    put(jj - 1, 1 - slot).wait()

                @pl.when(jj + 1 < nchunk)
                def _():
                    get(jj + 1, 1 - slot).start()

                put(jj, slot).start()

        put(nchunk - 1, (nchunk - 1) % 2).wait()

    return k(y, idx)


def _expert_kernel(be_ref, nv_ref, nu_ref, x_ref, wg_ref, wu_ref, wd_ref, y_ref, wgu_s, wd_s):
    b = pl.program_id(0)
    hid = wg_ref.shape[2]
    changed = jnp.logical_or(b == 0, be_ref[b] != be_ref[jnp.maximum(b - 1, 0)])

    @pl.when(changed)
    def _():
        wgu_s[:, 0:hid] = wg_ref[0].astype(BF16)
        wgu_s[:, hid:2 * hid] = wu_ref[0].astype(BF16)
        wd_s[...] = wd_ref[0].astype(BF16)

    @pl.when(b < nu_ref[0])
    def _():
        live = lax.broadcasted_iota(jnp.int32, x_ref.shape, 0) < nv_ref[b]
        x = _unpack_rows(jnp.where(live, x_ref[...], jnp.uint32(0)))
        gu = jnp.dot(x.astype(BF16), wgu_s[...], preferred_element_type=F32)
        gate = gu[:, 0:hid]
        act = gate * (1.0 / (1.0 + jnp.exp(-gate))) * gu[:, hid:2 * hid]
        y_ref[...] = _pack_rows(jnp.dot(act.astype(BF16), wd_s[...], preferred_element_type=F32))

    @pl.when(b >= nu_ref[0])
    def _():
        y_ref[...] = jnp.zeros_like(y_ref)


def _experts(block_e, n_valid, n_used, xb, w_gate, w_up, w_down, layer):
    n_rows, dp = xb.shape
    d, hid = w_gate.shape[2], w_gate.shape[3]
    bm = MOE_BM
    n_blocks = n_rows // bm
    blk = lambda b, be, nv, nu: (jnp.minimum(b, nu[0] - 1), 0)
    wsel = lambda b, be, nv, nu: (layer, be[b], 0, 0)
    return pl.pallas_call(
        _expert_kernel,
        grid_spec=pltpu.PrefetchScalarGridSpec(
            num_scalar_prefetch=3,
            grid=(n_blocks,),
            in_specs=[
                pl.BlockSpec((bm, dp), blk),
                pl.BlockSpec((None, 1, d, hid), wsel),
                pl.BlockSpec((None, 1, d, hid), wsel),
                pl.BlockSpec((None, 1, hid, d), wsel),
            ],
            out_specs=pl.BlockSpec((bm, dp), lambda b, be, nv, nu: (b, 0)),
            scratch_shapes=[pltpu.VMEM((d, 2 * hid), BF16), pltpu.VMEM((hid, d), BF16)],
        ),
        out_shape=jax.ShapeDtypeStruct((n_rows, dp), jnp.uint32),
        compiler_params=_params(("arbitrary",)),
        name="experts",
    )(block_e, n_valid, n_used, xb, w_gate, w_up, w_down)


def _combine_kernel(lat_ref, y0_ref, y1_ref, wc_ref, gate_ref, gf_ref, *rest, final):
    o_ref = rest[-1]
    wc = wc_ref[...]
    moe = wc[:, 0:1] * _unpack_rows(y0_ref[...]) + wc[:, 1:2] * _unpack_rows(y1_ref[...])
    out = lat_ref[...] + gate_ref[...] * moe
    if final:
        ms = jnp.mean(out * out, axis=-1, keepdims=True)
        out = out * lax.rsqrt(ms + EPS) * gf_ref[...]
    o_ref[...] = out


def _combine(lat, yg, wcol, mods, layer, g_final, seq, final, tok0, prev_out):
    n, d = lat.shape
    tm = COMBINE_TM
    nt = wcol.shape[0] // tm
    t0 = tok0 // tm
    tiles_per_seq = seq // tm
    row = lambda i: layer * MOD_ROWS + (t0 + i) // tiles_per_seq
    in_specs = [
        pl.BlockSpec((tm, d), lambda i: (t0 + i, 0)),
        pl.BlockSpec((tm, d // 2), lambda i: (i, 0)),
        pl.BlockSpec((tm, d // 2), lambda i: (nt + i, 0)),
        pl.BlockSpec((tm, LANES), lambda i: (i, 0)),
        pl.BlockSpec((None, 1, d), lambda i: (row(i), 0, 5)),
        pl.BlockSpec((1, d), lambda i: (0, 0)),
    ]
    args = [lat, yg, yg, wcol, mods, g_final]
    aliases = {}
    if prev_out is not None:
        in_specs.append(pl.BlockSpec(memory_space=pl.ANY))
        args.append(prev_out)
        aliases = {len(args) - 1: 0}
    return pl.pallas_call(
        functools.partial(_combine_kernel, final=final),
        grid=(nt,),
        in_specs=in_specs,
        out_specs=pl.BlockSpec((tm, d), lambda i: (t0 + i, 0)),
        out_shape=jax.ShapeDtypeStruct((n, d), F32),
        input_output_aliases=aliases,
        compiler_params=_params(("parallel",)),
        name="combine",
    )(*args)


def _moe(lat, mods, layer, g2, wr_t, br_t, w_gate, w_up, w_down, g_final, seq, final):
    n, d = lat.shape
    part = n // MOE_PARTS
    n_blocks = (2 * part) // MOE_BM + N_EXPERTS
    out = None
    for p in range(MOE_PARTS):
        tok0 = p * part
        h2, meta_i, wcol, counts = _router(lat, mods, layer, g2, wr_t, br_t, seq, tok0, part)
        dest, block_e, n_valid, n_used = _plan(counts[:, 0].astype(jnp.int32), meta_i, n_blocks)
        xb = _sc_dispatch(h2, dest, n_blocks * MOE_BM)
        yb = _experts(block_e, n_valid, n_used, xb, w_gate, w_up, w_down, layer)
        yg = _sc_gather(yb, dest)
        out = _combine(lat, yg, wcol, mods, layer, g_final, seq, final, tok0, out)
    return out


def _rope_tables(seq):
    quarter = HEAD_DIM // 4
    pos = jnp.arange(seq, dtype=F32)
    row_ids = jnp.floor(pos / GRID_W)
    col_ids = pos - row_ids * GRID_W
    inv = ROPE_BASE ** (-jnp.arange(quarter, dtype=F32) / quarter)
    ang_r = row_ids[:, None] * inv
    ang_c = col_ids[:, None] * inv
    zero = jnp.zeros_like(ang_r)
    cos = jnp.concatenate([jnp.cos(ang_r), jnp.cos(ang_r), jnp.cos(ang_c), jnp.cos(ang_c)], axis=-1)
    sa = jnp.concatenate([-jnp.sin(ang_r), zero, -jnp.sin(ang_c), zero], axis=-1)
    sb = jnp.concatenate([zero, jnp.sin(ang_r), zero, jnp.sin(ang_c)], axis=-1)
    rep = LANES // HEAD_DIM
    return tuple(jnp.tile(t, (1, rep)) for t in (cos, sa, sb))


def _router_weights(w_rg, b_rg, w_re, b_re):
    d = w_rg.shape[0]
    pad = EXPERTS_PER_GROUP - N_GROUPS
    wr_t = jnp.concatenate([w_rg.T, jnp.zeros((pad, d), F32), w_re.T], axis=0)
    br_t = jnp.concatenate([b_rg, jnp.full((pad,), NEG_INF, F32), b_re])[:, None]
    return wr_t, br_t


def kernel(x, c, ctx, c_ctx, w_ada, b_ada, g_norm1, g_norm2, g_final, w_in_even, attn_sink, g_sgu,
           w_spatial, b_spatial, w_out_even, w_in_odd, conv_w, w_out_odd, w_router_group,
           b_router_group, w_router_expert, b_router_expert, w_gate, w_up, w_down):
    b, s, d = x.shape
    n = b * s
    n_ctx = ctx.shape[1]
    depth = w_ada.shape[0]
    assert depth == 2 and b + 1 <= MOD_ROWS

    cond = jnp.concatenate([c, c_ctx[None, :], jnp.zeros((MOD_ROWS - b - 1, d), F32)], axis=0)
    mods = _ada(cond, w_ada, b_ada).reshape(depth * MOD_ROWS, 1, 6 * d)
    gf = g_final[None, :]

    lat = x.reshape(n, d)
    w_in_bf = w_in_even[0].astype(BF16)
    tabs = _rope_tables(s)
    q, kx, vx, u, z = _even_in(lat, mods, 0, lambda i: i // (s // EVEN_TM), g_norm1[0][None, :], w_in_bf,
                               tabs, s // EVEN_TM, EVEN_TM)
    ones = jnp.ones((n_ctx, LANES), F32)
    zeros = jnp.zeros((n_ctx, LANES), F32)
    _, kcx, vcx, _, _ = _even_in(ctx.reshape(b * n_ctx, d), mods, 0, lambda i: b, g_norm1[0][None, :],
                                 w_in_bf, (ones, zeros, zeros), 1, n_ctx)
    bsp_full = jnp.repeat(b_spatial[0].T, HEAD_DIM, axis=1)
    lat = _even_mix(lat, q, kx, vx, kcx, vcx, u, z, attn_sink[0], g_sgu[0][None, :],
                    w_spatial[0].astype(BF16), bsp_full, w_out_even[0].astype(BF16), mods, s, n_ctx)
    wr_t, br_t = _router_weights(w_router_group[0], b_router_group[0], w_router_expert[0], b_router_expert[0])
    lat = _moe(lat, mods, 0, g_norm2[0][None, :], wr_t, br_t, w_gate, w_up, w_down, gf, s, False)

    conv_w8 = jnp.concatenate([conv_w[0], jnp.zeros((8 - conv_w.shape[1], d), F32)], axis=0)
    lat = _odd_mix(lat, mods, 1, g_norm1[1][None, :], w_in_odd[0].astype(BF16), conv_w8,
                   w_out_odd[0].astype(BF16), s)
    wr_t, br_t = _router_weights(w_router_group[1], b_router_group[1], w_router_expert[1], b_router_expert[1])
    out = _moe(lat, mods, 1, g_norm2[1][None, :], wr_t, br_t, w_gate, w_up, w_down, gf, s, True)
    return out.reshape(b, s, d)
```

```python
import functools

import jax
import jax.numpy as jnp
from jax import lax
from jax.experimental import pallas as pl
from jax.experimental.pallas import tpu as pltpu
from jax.experimental.pallas import tpu_sc as plsc

F32 = jnp.float32
BF16 = jnp.bfloat16
HIGHEST = lax.Precision.HIGHEST

GRID_W = 64
N_Q_HEADS = 8
N_KV_HEADS = 2
HEAD_DIM = 64
ATT_BLOCK = 128
ROPE_BASE = 10000.0
Q_DIM = N_Q_HEADS * HEAD_DIM
KV_DIM = N_KV_HEADS * HEAD_DIM
SG_GROUPS = 8
SG_WIDTH = SG_GROUPS * HEAD_DIM
N_GROUPS = 4
EXPERTS_PER_GROUP = 8
N_EXPERTS = N_GROUPS * EXPERTS_PER_GROUP
EPS = 1e-6
NEG_INF = -1e30

LANES = 128
SC_CORES = 2
SC_SUBCORES = 16
SC_WORKERS = SC_CORES * SC_SUBCORES
SC_CHUNK = 32
MOD_ROWS = 8
VMEM_LIMIT = 56 * 1024 * 1024

ADA_TN = 1536
EVEN_TM = 512
ATT_TQ = 512
ODD_TM = 512
ROUTE_TM = 512
MOE_BM = 512
MOE_PARTS = 2
COMBINE_TM = 512


def _params(sem):
    return pltpu.CompilerParams(dimension_semantics=sem, vmem_limit_bytes=VMEM_LIMIT)


def _rms_mod(x, g, shift, scale):
    ms = jnp.mean(x * x, axis=-1, keepdims=True)
    return (x * lax.rsqrt(ms + EPS) * g) * (1.0 + scale) + shift


def _pack_rows(a):
    w = a.shape[1] // 2
    hi = pltpu.bitcast(a[:, :w].astype(BF16).astype(F32), jnp.uint32)
    lo = pltpu.bitcast(a[:, w:].astype(BF16).astype(F32), jnp.uint32)
    return hi | (lo >> 16)


def _unpack_rows(p):
    hi = pltpu.bitcast(p & jnp.uint32(0xFFFF0000), F32)
    lo = pltpu.bitcast(p << 16, F32)
    return jnp.concatenate([hi, lo], axis=1)


def _ada_kernel(a_ref, w_ref, b_ref, o_ref):
    a = a_ref[...]
    s = a * (1.0 / (1.0 + jnp.exp(-a)))
    o_ref[0] = jnp.dot(s, w_ref[0], preferred_element_type=F32, precision=HIGHEST) + b_ref[0]


def _ada(cond, w_ada, b_ada):
    depth, d, six_d = w_ada.shape
    return pl.pallas_call(
        _ada_kernel,
        grid=(depth, six_d // ADA_TN),
        in_specs=[
            pl.BlockSpec((MOD_ROWS, d), lambda l, j: (0, 0)),
            pl.BlockSpec((1, d, ADA_TN), lambda l, j: (l, 0, j)),
            pl.BlockSpec((1, 1, ADA_TN), lambda l, j: (l, 0, j)),
        ],
        out_specs=pl.BlockSpec((1, MOD_ROWS, ADA_TN), lambda l, j: (l, 0, j)),
        out_shape=jax.ShapeDtypeStruct((depth, MOD_ROWS, six_d), F32),
        compiler_params=_params(("arbitrary", "arbitrary")),
        name="ada",
    )(cond, w_ada, b_ada.reshape(depth, 1, six_d))


def _even_in_kernel(x_ref, sh_ref, sc_ref, g_ref, w_ref, cos_ref, sa_ref, sb_ref,
                    q_ref, kx_ref, vx_ref, u_ref, z_ref):
    h = _rms_mod(x_ref[...], g_ref[...], sh_ref[...], sc_ref[...])
    p = jnp.dot(h.astype(BF16), w_ref[...], preferred_element_type=F32)
    cos, sa, sb = cos_ref[...], sa_ref[...], sb_ref[...]

    def rope(t):
        return t * cos + pltpu.roll(t, LANES - 16, 1) * sa + pltpu.roll(t, 16, 1) * sb

    scale = HEAD_DIM ** -0.5
    for cblk in range(Q_DIM // LANES):
        t = p[:, cblk * LANES:(cblk + 1) * LANES]
        q_ref[:, cblk * LANES:(cblk + 1) * LANES] = (rope(t) * scale).astype(BF16)

    low = lax.broadcasted_iota(jnp.int32, (x_ref.shape[0], LANES), 1) < HEAD_DIM

    def spread(t, o_ref):
        sw = pltpu.roll(t, HEAD_DIM, 1)
        zero = jnp.zeros_like(t)
        o_ref[:, 0 * LANES:1 * LANES] = jnp.where(low, t, zero).astype(BF16)
        o_ref[:, 1 * LANES:2 * LANES] = jnp.where(low, zero, sw).astype(BF16)
        o_ref[:, 2 * LANES:3 * LANES] = jnp.where(low, sw, zero).astype(BF16)
        o_ref[:, 3 * LANES:4 * LANES] = jnp.where(low, zero, t).astype(BF16)

    spread(rope(p[:, Q_DIM:Q_DIM + KV_DIM]), kx_ref)
    spread(p[:, Q_DIM + KV_DIM:Q_DIM + 2 * KV_DIM], vx_ref)
    u0 = Q_DIM + 2 * KV_DIM
    u_ref[...] = p[:, u0:u0 + SG_WIDTH]
    z_ref[...] = p[:, u0 + SG_WIDTH:u0 + 2 * SG_WIDTH]


def _even_in(x2d, mods, layer, mod_row_fn, g, w_bf, tabs, tab_blocks, tm):
    n, d = x2d.shape
    ein = w_bf.shape[1]
    cos, sa, sb = tabs
    row = lambda i: layer * MOD_ROWS + mod_row_fn(i)
    tab_spec = pl.BlockSpec((tm, LANES), lambda i: (i % tab_blocks, 0))
    wide = 4 * LANES
    return pl.pallas_call(
        _even_in_kernel,
        grid=(n // tm,),
        in_specs=[
            pl.BlockSpec((tm, d), lambda i: (i, 0)),
            pl.BlockSpec((None, 1, d), lambda i: (row(i), 0, 0)),
            pl.BlockSpec((None, 1, d), lambda i: (row(i), 0, 1)),
            pl.BlockSpec((1, d), lambda i: (0, 0)),
            pl.BlockSpec((d, ein), lambda i: (0, 0)),
            tab_spec, tab_spec, tab_spec,
        ],
        out_specs=[
            pl.BlockSpec((tm, Q_DIM), lambda i: (i, 0)),
            pl.BlockSpec((tm, wide), lambda i: (i, 0)),
            pl.BlockSpec((tm, wide), lambda i: (i, 0)),
            pl.BlockSpec((tm, SG_WIDTH), lambda i: (i, 0)),
            pl.BlockSpec((tm, SG_WIDTH), lambda i: (i, 0)),
        ],
        out_shape=[
            jax.ShapeDtypeStruct((n, Q_DIM), BF16),
            jax.ShapeDtypeStruct((n, wide), BF16),
            jax.ShapeDtypeStruct((n, wide), BF16),
            jax.ShapeDtypeStruct((n, SG_WIDTH), F32),
            jax.ShapeDtypeStruct((n, SG_WIDTH), F32),
        ],
        compiler_params=_params(("parallel",)),
        name="even_in",
    )(x2d, mods, mods, g, w_bf, cos, sa, sb)


def _gelu(x):
    return 0.5 * x * (1.0 + lax.erf(x * (2.0 ** -0.5)))


def _even_mix_kernel(sink_ref, lat_ref, q_ref, kxm_ref, kxp_ref, kxn_ref, vxm_ref, vxp_ref, vxn_ref,
                     kcx_ref, vcx_ref, u_ref, z_ref, gsgu_ref, wsp_ref, bsp_ref, wout_ref, gate_ref,
                     o_ref, kband, vband, mixin, *, tiles_per_seq):
    i = pl.program_id(0)
    tq = q_ref.shape[0]
    blk = ATT_BLOCK
    nsub = tq // blk
    first = (i % tiles_per_seq) == 0
    last = (i % tiles_per_seq) == tiles_per_seq - 1

    kband[0:blk] = kxp_ref[...]
    kband[blk:blk + tq] = kxm_ref[...]
    kband[blk + tq:] = kxn_ref[...]
    vband[0:blk] = vxp_ref[...]
    vband[blk:blk + tq] = vxm_ref[...]
    vband[blk + tq:] = vxn_ref[...]

    row = lax.broadcasted_iota(jnp.int32, (blk, 3 * blk), 0)
    col = lax.broadcasted_iota(jnp.int32, (blk, 3 * blk), 1)
    lane_low = lax.broadcasted_iota(jnp.int32, (blk, LANES), 1) < HEAD_DIM
    nt = (((1,), (1,)), ((), ()))

    def sub_block(j, carry):
        r0 = pl.multiple_of(j * blk, blk)
        ok_prev = jnp.logical_not(jnp.logical_and(first, j == 0))
        ok_next = jnp.logical_not(jnp.logical_and(last, j == nsub - 1))
        mask = (((col >= blk) | ((col >= row) & ok_prev))
                & ((col < 2 * blk) | ((col - 2 * blk <= row) & ok_next)))
        qj = q_ref[pl.ds(r0, blk), :]
        kb = kband[pl.ds(r0, 3 * blk), :]
        vb = vband[pl.ds(r0, 3 * blk), :]
        for pair in range(N_Q_HEADS // 2):
            acc = None
            for par in range(2):
                hd = 2 * pair + par
                var = 2 * (hd // (N_Q_HEADS // N_KV_HEADS)) + par
                qh = qj[:, pair * LANES:(pair + 1) * LANES]
                s_loc = lax.dot_general(qh, kb[:, var * LANES:(var + 1) * LANES], nt,
                                        preferred_element_type=F32)
                s_ctx = lax.dot_general(qh, kcx_ref[:, var * LANES:(var + 1) * LANES], nt,
                                        preferred_element_type=F32)
                s_loc = jnp.where(mask, s_loc, NEG_INF)
                sk = sink_ref[hd]
                m = jnp.maximum(jnp.maximum(jnp.max(s_loc, axis=-1, keepdims=True),
                                            jnp.max(s_ctx, axis=-1, keepdims=True)), sk)
                p_loc = jnp.exp(s_loc - m)
                p_ctx = jnp.exp(s_ctx - m)
                den = (jnp.sum(p_loc, axis=-1, keepdims=True) + jnp.sum(p_ctx, axis=-1, keepdims=True)
                       + jnp.exp(sk - m))
                o = (jnp.dot(p_ctx.astype(BF16), vcx_ref[:, var * LANES:(var + 1) * LANES],
                             preferred_element_type=F32)
                     + jnp.dot(p_loc.astype(BF16), vb[:, var * LANES:(var + 1) * LANES],
                               preferred_element_type=F32))
                o = o / den
                acc = o if acc is None else acc + o
            mixin[pl.ds(r0, blk), pair * LANES:(pair + 1) * LANES] = acc.astype(BF16)

        ug = _gelu(u_ref[pl.ds(r0, blk), :])
        zg = _gelu(z_ref[pl.ds(r0, blk), :])
        mu = jnp.mean(zg, axis=-1, keepdims=True)
        zc = zg - mu
        zn = zc * lax.rsqrt(jnp.mean(zc * zc, axis=-1, keepdims=True) + EPS) * gsgu_ref[...]
        for pair in range(SG_GROUPS // 2):
            zp = zn[:, pair * LANES:(pair + 1) * LANES]
            zero = jnp.zeros_like(zp)
            lo = jnp.where(lane_low, zp, zero).astype(BF16)
            hi = jnp.where(lane_low, zero, zp).astype(BF16)
            sg = (jnp.dot(wsp_ref[2 * pair], lo, preferred_element_type=F32)
                  + jnp.dot(wsp_ref[2 * pair + 1], hi, preferred_element_type=F32)
                  + bsp_ref[:, pair * LANES:(pair + 1) * LANES])
            mixin[pl.ds(r0, blk), Q_DIM + pair * LANES:Q_DIM + (pair + 1) * LANES] = (
                ug[:, pair * LANES:(pair + 1) * LANES] * sg).astype(BF16)
        return carry

    lax.fori_loop(0, nsub, sub_block, 0)
    mix = jnp.dot(mixin[...], wout_ref[...], preferred_element_type=F32)
    o_ref[...] = lat_ref[...] + gate_ref[...] * mix


def _even_mix(lat, q, kx, vx, kcx, vcx, u, z, sink, g_sgu, wsp_bf, bsp_full, wout_bf, mods, seq, ctx_len):
    n, d = lat.shape
    tq = ATT_TQ
    tiles_per_seq = seq // tq
    sub = tq // ATT_BLOCK
    nblk = n // ATT_BLOCK
    wide = 4 * LANES
    main = lambda w: pl.BlockSpec((tq, w), lambda i: (i, 0))
    prev = pl.BlockSpec((ATT_BLOCK, wide), lambda i: (jnp.maximum(i * sub - 1, 0), 0))
    nxt = pl.BlockSpec((ATT_BLOCK, wide), lambda i: (jnp.minimum((i + 1) * sub, nblk - 1), 0))
    ctxs = pl.BlockSpec((ctx_len, wide), lambda i: (i // tiles_per_seq, 0))
    const = lambda shape: pl.BlockSpec(shape, lambda i: (0,) * len(shape), pipeline_mode=pl.Buffered(1))
    return pl.pallas_call(
        functools.partial(_even_mix_kernel, tiles_per_seq=tiles_per_seq),
        grid=(n // tq,),
        in_specs=[
            pl.BlockSpec(memory_space=pltpu.SMEM),
            main(d), main(Q_DIM),
            main(wide), prev, nxt,
            main(wide), prev, nxt,
            ctxs, ctxs,
            main(SG_WIDTH), main(SG_WIDTH),
            const((1, SG_WIDTH)), const(wsp_bf.shape), const(bsp_full.shape), const(wout_bf.shape),
            pl.BlockSpec((None, 1, d), lambda i: (i // tiles_per_seq, 0, 2)),
        ],
        out_specs=pl.BlockSpec((tq, d), lambda i: (i, 0)),
        out_shape=jax.ShapeDtypeStruct((n, d), F32),
        scratch_shapes=[
            pltpu.VMEM((tq + 2 * ATT_BLOCK, wide), BF16),
            pltpu.VMEM((tq + 2 * ATT_BLOCK, wide), BF16),
            pltpu.VMEM((tq, Q_DIM + SG_WIDTH), BF16),
        ],
        compiler_params=_params(("parallel",)),
        name="even_mix",
    )(sink, lat, q, kx, kx, kx, vx, vx, vx, kcx, vcx, u, z, g_sgu, wsp_bf, bsp_full, wout_bf, mods)


def _odd_mix_kernel(x_ref, xp_ref, sh_ref, sc_ref, gate_ref, g_ref, win_ref, cw_ref, wout_ref,
                    o_ref, y_s, bg_s, tail_s, *, tiles_per_seq):
    i = pl.program_id(0)
    n_tiles = pl.num_programs(0) - 1
    tm, d = x_ref.shape
    cur = i % 2
    prv = 1 - cur

    tail_s[...] = y_s[cur, tm - 8:tm, :]

    @pl.when(i < n_tiles)
    def _():
        h = _rms_mod(x_ref[...], g_ref[...], sh_ref[...], sc_ref[...])
        p = jnp.dot(h.astype(BF16), win_ref[...], preferred_element_type=F32)
        bg_s[cur] = p[:, 0:d]
        y_s[cur] = p[:, d:2 * d] * p[:, 2 * d:3 * d]

    @pl.when(i >= 1)
    def _():
        t_prev = i - 1
        first = (t_prev % tiles_per_seq) == 0
        last = (t_prev % tiles_per_seq) == tiles_per_seq - 1
        y = y_s[prv]
        left = jnp.where(first, 0.0, tail_s[7:8, :])
        right = jnp.where(last, 0.0, y_s[cur, 0:1, :])
        ridx = lax.broadcasted_iota(jnp.int32, (tm, d), 0)
        y_dn = jnp.where(ridx == 0, left, pltpu.roll(y, 1, 0))
        y_up = jnp.where(ridx == tm - 1, right, pltpu.roll(y, tm - 1, 0))
        conv = y_dn * cw_ref[0:1, :] + y * cw_ref[1:2, :] + y_up * cw_ref[2:3, :]
        mix = jnp.dot((bg_s[prv] * conv).astype(BF16), wout_ref[...], preferred_element_type=F32)
        o_ref[...] = xp_ref[...] + gate_ref[...] * mix


def _odd_mix(lat, mods, layer, g, win_bf, conv_w8, wout_bf, seq):
    n, d = lat.shape
    tm = ODD_TM
    nt = n // tm
    tiles_per_seq = seq // tm
    cur = lambda i: jnp.minimum(i, nt - 1)
    prv = lambda i: jnp.maximum(i - 1, 0)
    row = lambda t: layer * MOD_ROWS + t // tiles_per_seq
    const = lambda shape: pl.BlockSpec(shape, lambda i: (0,) * len(shape), pipeline_mode=pl.Buffered(1))
    return pl.pallas_call(
        functools.partial(_odd_mix_kernel, tiles_per_seq=tiles_per_seq),
        grid=(nt + 1,),
        in_specs=[
            pl.BlockSpec((tm, d), lambda i: (cur(i), 0)),
            pl.BlockSpec((tm, d), lambda i: (prv(i), 0)),
            pl.BlockSpec((None, 1, d), lambda i: (row(cur(i)), 0, 0)),
            pl.BlockSpec((None, 1, d), lambda i: (row(cur(i)), 0, 1)),
            pl.BlockSpec((None, 1, d), lambda i: (row(prv(i)), 0, 2)),
            const((1, d)), const(win_bf.shape), const(conv_w8.shape), const(wout_bf.shape),
        ],
        out_specs=pl.BlockSpec((tm, d), lambda i: (prv(i), 0)),
        out_shape=jax.ShapeDtypeStruct((n, d), F32),
        scratch_shapes=[
            pltpu.VMEM((2, tm, d), F32),
            pltpu.VMEM((2, tm, d), F32),
            pltpu.VMEM((8, d), F32),
        ],
        compiler_params=_params(("arbitrary",)),
        name="odd_mix",
    )(lat, lat, mods, mods, mods, g, win_bf, conv_w8, wout_bf)


def _router_kernel(x_ref, sh_ref, sc_ref, g_ref, wr_ref, br_ref,
                   h_ref, mi_ref, wc_ref, cnt_ref, carry):
    i = pl.program_id(0)
    tm = x_ref.shape[0]
    epg = EXPERTS_PER_GROUP

    @pl.when(i == 0)
    def _():
        carry[...] = jnp.zeros_like(carry)

    h = _rms_mod(x_ref[...], g_ref[...], sh_ref[...], sc_ref[...])
    h_ref[...] = _pack_rows(h)
    lg = lax.dot_general(wr_ref[...], h, (((1,), (1,)), ((), ())),
                         preferred_element_type=F32, precision=HIGHEST) + br_ref[...]
    io8 = lax.broadcasted_iota(jnp.int32, (epg, tm), 0)
    gl = lg[0:epg]
    gmax = jnp.max(gl, axis=0, keepdims=True)
    g_idx = jnp.min(jnp.where(gl == gmax, io8, epg), axis=0, keepdims=True)
    g_w = 1.0 / jnp.sum(jnp.exp(gl - gmax), axis=0, keepdims=True)
    e_sel = lg[epg:2 * epg]
    for gi in range(1, N_GROUPS):
        e_sel = jnp.where(g_idx == gi, lg[(gi + 1) * epg:(gi + 2) * epg], e_sel)
    v0 = jnp.max(e_sel, axis=0, keepdims=True)
    i0 = jnp.min(jnp.where(e_sel == v0, io8, epg), axis=0, keepdims=True)
    rest = jnp.where(io8 == i0, -jnp.inf, e_sel)
    v1 = jnp.max(rest, axis=0, keepdims=True)
    i1 = jnp.min(jnp.where(rest == v1, io8, epg), axis=0, keepdims=True)
    t = jnp.exp(v1 - v0)
    w0 = g_w / (1.0 + t)
    w1 = g_w * t / (1.0 + t)
    e0 = g_idx * epg + i0
    e1 = g_idx * epg + i1

    io32 = lax.broadcasted_iota(jnp.int32, (N_EXPERTS, tm), 0)
    hit0 = io32 == e0
    hit1 = io32 == e1
    onehot = jnp.where(hit0 | hit1, 1.0, 0.0)
    r_i = lax.broadcasted_iota(jnp.int32, (tm, tm), 0)
    c_i = lax.broadcasted_iota(jnp.int32, (tm, tm), 1)
    upper = jnp.where(r_i < c_i, 1.0, 0.0).astype(BF16)
    cum = jnp.dot(onehot.astype(BF16), upper, preferred_element_type=F32) + carry[...]
    rank0 = jnp.sum(jnp.where(hit0, cum, 0.0), axis=0, keepdims=True).astype(jnp.int32)
    rank1 = jnp.sum(jnp.where(hit1, cum, 0.0), axis=0, keepdims=True).astype(jnp.int32)
    carry[...] = carry[...] + jnp.sum(onehot, axis=1, keepdims=True)
    cnt_ref[...] = jnp.broadcast_to(carry[...], cnt_ref.shape)

    mi_ref[...] = jnp.where(io8 == 0, e0, jnp.where(io8 == 1, e1, jnp.where(io8 == 2, rank0,
                            jnp.where(io8 == 3, rank1, 0))))
    io128 = lax.broadcasted_iota(jnp.int32, (LANES, tm), 0)
    wrow = jnp.where(io128 == 0, w0, jnp.where(io128 == 1, w1, 0.0))
    wc_ref[...] = wrow.T


def _router(lat, mods, layer, g, wr_t, br_t, seq, tok0, n):
    d = lat.shape[1]
    tm = ROUTE_TM
    tiles_per_seq = seq // tm
    t0 = tok0 // tm
    row = lambda i: layer * MOD_ROWS + (t0 + i) // tiles_per_seq
    const = lambda shape: pl.BlockSpec(shape, lambda i: (0,) * len(shape), pipeline_mode=pl.Buffered(1))
    return pl.pallas_call(
        _router_kernel,
        grid=(n // tm,),
        in_specs=[
            pl.BlockSpec((tm, d), lambda i: (t0 + i, 0)),
            pl.BlockSpec((None, 1, d), lambda i: (row(i), 0, 3)),
            pl.BlockSpec((None, 1, d), lambda i: (row(i), 0, 4)),
            const((1, d)), const(wr_t.shape), const(br_t.shape),
        ],
        out_specs=[
            pl.BlockSpec((tm, d // 2), lambda i: (i, 0)),
            pl.BlockSpec((8, tm), lambda i: (0, i)),
            pl.BlockSpec((tm, LANES), lambda i: (i, 0)),
            pl.BlockSpec((N_EXPERTS, LANES), lambda i: (0, 0)),
        ],
        out_shape=[
            jax.ShapeDtypeStruct((n, d // 2), jnp.uint32),
            jax.ShapeDtypeStruct((8, n), jnp.int32),
            jax.ShapeDtypeStruct((n, LANES), F32),
            jax.ShapeDtypeStruct((N_EXPERTS, LANES), F32),
        ],
        scratch_shapes=[pltpu.VMEM((N_EXPERTS, 1), F32)],
        compiler_params=_params(("arbitrary",)),
        name="router",
    )(lat, mods, mods, g, wr_t, br_t)


def _plan_kernel(cnt_ref, mi_ref, dest_ref, be_ref, nx_ref, nv_ref, nu_ref, ps_ref, *, n_blocks):
    bm = MOE_BM

    def per_expert(e, blk0):
        cnt = cnt_ref[e]
        nb = (cnt + bm - 1) // bm
        ps_ref[e] = blk0 * bm

        def fill(b, c):
            be_ref[b] = e
            nv_ref[b] = jnp.minimum(cnt - (b - blk0) * bm, bm)
            return c

        lax.fori_loop(blk0, blk0 + nb, fill, 0)
        return blk0 + nb

    n_used = lax.fori_loop(0, N_EXPERTS, per_expert, 0)
    nu_ref[0] = n_used
    last_e = be_ref[jnp.maximum(n_used - 1, 0)]

    def fill_tail(b, c):
        be_ref[b] = last_e
        nv_ref[b] = 0
        return c

    lax.fori_loop(n_used, n_blocks, fill_tail, 0)

    nx_ref[n_blocks - 1] = last_e

    def link(k, c):
        b = n_blocks - 2 - k
        nx_ref[b] = jnp.where(be_ref[b + 1] != be_ref[b], be_ref[b + 1], nx_ref[b + 1])
        return c

    lax.fori_loop(0, n_blocks - 1, link, 0)

    e01 = mi_ref[0:2, :]
    dest = mi_ref[2:4, :]
    for e in range(N_EXPERTS):
        dest = dest + jnp.where(e01 == e, ps_ref[e], 0)
    dest_ref[...] = dest


def _plan(counts, meta_i, n_blocks):
    n = meta_i.shape[1]
    return pl.pallas_call(
        functools.partial(_plan_kernel, n_blocks=n_blocks),
        in_specs=[pl.BlockSpec(memory_space=pltpu.SMEM), pl.BlockSpec(memory_space=pltpu.VMEM)],
        out_specs=[pl.BlockSpec(memory_space=pltpu.VMEM)] + [pl.BlockSpec(memory_space=pltpu.SMEM)] * 4,
        out_shape=[
            jax.ShapeDtypeStruct((2, n), jnp.int32),
            jax.ShapeDtypeStruct((n_blocks,), jnp.int32),
            jax.ShapeDtypeStruct((n_blocks,), jnp.int32),
            jax.ShapeDtypeStruct((n_blocks,), jnp.int32),
            jax.ShapeDtypeStruct((1,), jnp.int32),
        ],
        scratch_shapes=[pltpu.SMEM((N_EXPERTS,), jnp.int32)],
        compiler_params=pltpu.CompilerParams(vmem_limit_bytes=VMEM_LIMIT),
        name="plan",
    )(counts, meta_i)


def _sc_mesh():
    return plsc.VectorSubcoreMesh(core_axis_name="c", subcore_axis_name="s",
                                  num_cores=SC_CORES, num_subcores=SC_SUBCORES)


def _sc_worker():
    return lax.axis_index("s") * SC_CORES + lax.axis_index("c")


def _sc_dispatch(h2, dest, n_rows):
    n, d = h2.shape
    c = SC_CHUNK
    per_w = n // SC_WORKERS
    nchunk = per_w // c
    idx = dest.reshape(2, SC_WORKERS, nchunk, c)

    @functools.partial(
        pl.kernel, mesh=_sc_mesh(), out_type=jax.ShapeDtypeStruct((n_rows, d), h2.dtype),
        scratch_types=[pltpu.VMEM((nchunk, c), jnp.int32), pltpu.VMEM((nchunk, c), jnp.int32),
                       pltpu.VMEM((2, c, d), h2.dtype),
                       pltpu.SemaphoreType.DMA((2,)), pltpu.SemaphoreType.DMA((2,))])
    def k(h_hbm, idx_hbm, xb_hbm, idx0_v, idx1_v, rows_v, gsem, ssem):
        wid = _sc_worker()
        base = wid * per_w
        idx_v = (idx0_v, idx1_v)
        for kk in range(2):
            pltpu.sync_copy(idx_hbm.at[kk, wid], idx_v[kk])

        def get(j, slot):
            return pltpu.make_async_copy(h_hbm.at[pl.ds(base + j * c, c)], rows_v.at[slot], gsem.at[slot])

        def put(j, slot, kk):
            return pltpu.make_async_copy(rows_v.at[slot], xb_hbm.at[idx_v[kk].at[j]], ssem.at[slot])

        get(0, 0).start()

        @pl.loop(0, nchunk, step=2)
        def _(j):
            for slot in range(2):
                jj = j + slot
                get(jj, slot).wait()

                @pl.when(jj >= 1)
                def _():
                    for kk in range(2):
                        put(jj - 1, 1 - slot, kk).wait()

                @pl.when(jj + 1 < nchunk)
                def _():
                    get(jj + 1, 1 - slot).start()

                for kk in range(2):
                    put(jj, slot, kk).start()

        for kk in range(2):
            put(nchunk - 1, (nchunk - 1) % 2, kk).wait()

    return k(h2, idx)


def _sc_gather(y, dest):
    d = y.shape[1]
    total = dest.shape[0] * dest.shape[1]
    c = SC_CHUNK
    per_w = total // SC_WORKERS
    nchunk = per_w // c
    idx = dest.reshape(SC_WORKERS, nchunk, c)

    @functools.partial(
        pl.kernel, mesh=_sc_mesh(), out_type=jax.ShapeDtypeStruct((total, d), y.dtype),
        scratch_types=[pltpu.VMEM((nchunk, c), jnp.int32), pltpu.VMEM((2, c, d), y.dtype),
                       pltpu.SemaphoreType.DMA((2,)), pltpu.SemaphoreType.DMA((2,))])
    def k(y_hbm, idx_hbm, out_hbm, idx_v, rows_v, gsem, ssem):
        wid = _sc_worker()
        base = wid * per_w
        pltpu.sync_copy(idx_hbm.at[wid], idx_v)

        def get(j, slot):
            return pltpu.make_async_copy(y_hbm.at[idx_v.at[j]], rows_v.at[slot], gsem.at[slot])

        def put(j, slot):
            return pltpu.make_async_copy(rows_v.at[slot], out_hbm.at[pl.ds(base + j * c, c)], ssem.at[slot])

        get(0, 0).start()

        @pl.loop(0, nchunk, step=2)
        def _(j):
            for slot in range(2):
                jj = j + slot
                get(jj, slot).wait()

                @pl.when(jj >= 1)
                def _():
                    put(jj - 1, 1 - slot).wait()

                @pl.when(jj + 1 < nchunk)
                def _():
                    get(jj + 1, 1 - slot).start()

                put(jj, slot).start()

        put(nchunk - 1, (nchunk - 1) % 2).wait()

    return k(y, idx)


def _expert_kernel(be_ref, nx_ref, nv_ref, nu_ref, x_ref, wg_hbm, wu_hbm, wd_hbm, y_ref,
                   wgu_s, wd_s, stg_g, stg_u, stg_d, run_s, sems, *, layer):
    b = pl.program_id(0)
    hid = stg_g.shape[2]
    e = be_ref[b]
    changed = jnp.logical_or(b == 0, e != be_ref[jnp.maximum(b - 1, 0)])

    def fetch(expert, slot):
        return (pltpu.make_async_copy(wg_hbm.at[layer, expert], stg_g.at[slot], sems.at[slot]),
                pltpu.make_async_copy(wu_hbm.at[layer, expert], stg_u.at[slot], sems.at[slot]),
                pltpu.make_async_copy(wd_hbm.at[layer, expert], stg_d.at[slot], sems.at[slot]))

    @pl.when(b == 0)
    def _():
        run_s[0] = 0
        for cp in fetch(e, 0):
            cp.start()

    @pl.when(changed)
    def _():
        run = jnp.where(b == 0, 0, run_s[0] + 1)
        run_s[0] = run
        slot = run % 2
        for cp in fetch(e, slot):
            cp.wait()

        @pl.when(nx_ref[b] != e)
        def _():
            for cp in fetch(nx_ref[b], 1 - slot):
                cp.start()

        wgu_s[:, 0:hid] = stg_g[slot].astype(BF16)
        wgu_s[:, hid:2 * hid] = stg_u[slot].astype(BF16)
        wd_s[...] = stg_d[slot].astype(BF16)

    @pl.when(b < nu_ref[0])
    def _():
        live = lax.broadcasted_iota(jnp.int32, x_ref.shape, 0) < nv_ref[b]
        x = _unpack_rows(jnp.where(live, x_ref[...], jnp.uint32(0)))
        gu = jnp.dot(x.astype(BF16), wgu_s[...], preferred_element_type=F32)
        gate = gu[:, 0:hid]
        act = gate * (1.0 / (1.0 + jnp.exp(-gate))) * gu[:, hid:2 * hid]
        y_ref[...] = _pack_rows(jnp.dot(act.astype(BF16), wd_s[...], preferred_element_type=F32))

    @pl.when(b >= nu_ref[0])
    def _():
        y_ref[...] = jnp.zeros_like(y_ref)


def _experts(block_e, next_e, n_valid, n_used, xb, w_gate, w_up, w_down, layer):
    n_rows, dp = xb.shape
    d, hid = w_gate.shape[2], w_gate.shape[3]
    bm = MOE_BM
    n_blocks = n_rows // bm
    hbm = pl.BlockSpec(memory_space=pl.ANY)
    return pl.pallas_call(
        functools.partial(_expert_kernel, layer=layer),
        grid_spec=pltpu.PrefetchScalarGridSpec(
            num_scalar_prefetch=4,
            grid=(n_blocks,),
            in_specs=[
                pl.BlockSpec((bm, dp), lambda b, be, nx, nv, nu: (jnp.minimum(b, nu[0] - 1), 0)),
                hbm, hbm, hbm,
            ],
            out_specs=pl.BlockSpec((bm, dp), lambda b, be, nx, nv, nu: (b, 0)),
            scratch_shapes=[
                pltpu.VMEM((d, 2 * hid), BF16), pltpu.VMEM((hid, d), BF16),
                pltpu.VMEM((2, d, hid), F32), pltpu.VMEM((2, d, hid), F32), pltpu.VMEM((2, hid, d), F32),
                pltpu.SMEM((1,), jnp.int32), pltpu.SemaphoreType.DMA((2,)),
            ],
        ),
        out_shape=jax.ShapeDtypeStruct((n_rows, dp), jnp.uint32),
        compiler_params=_params(("arbitrary",)),
        name="experts",
    )(block_e, next_e, n_valid, n_used, xb, w_gate, w_up, w_down)


def _combine_kernel(lat_ref, y0_ref, y1_ref, wc_ref, gate_ref, gf_ref, *rest, final):
    o_ref = rest[-1]
    wc = wc_ref[...]
    moe = wc[:, 0:1] * _unpack_rows(y0_ref[...]) + wc[:, 1:2] * _unpack_rows(y1_ref[...])
    out = lat_ref[...] + gate_ref[...] * moe
    if final:
        ms = jnp.mean(out * out, axis=-1, keepdims=True)
        out = out * lax.rsqrt(ms + EPS) * gf_ref[...]
    o_ref[...] = out


def _combine(lat, yg, wcol, mods, layer, g_final, seq, final, tok0, prev_out):
    n, d = lat.shape
    tm = COMBINE_TM
    nt = wcol.shape[0] // tm
    t0 = tok0 // tm
    tiles_per_seq = seq // tm
    row = lambda i: layer * MOD_ROWS + (t0 + i) // tiles_per_seq
    in_specs = [
        pl.BlockSpec((tm, d), lambda i: (t0 + i, 0)),
        pl.BlockSpec((tm, d // 2), lambda i: (i, 0)),
        pl.BlockSpec((tm, d // 2), lambda i: (nt + i, 0)),
        pl.BlockSpec((tm, LANES), lambda i: (i, 0)),
        pl.BlockSpec((None, 1, d), lambda i: (row(i), 0, 5)),
        pl.BlockSpec((1, d), lambda i: (0, 0)),
    ]
    args = [lat, yg, yg, wcol, mods, g_final]
    aliases = {}
    if prev_out is not None:
        in_specs.append(pl.BlockSpec(memory_space=pl.ANY))
        args.append(prev_out)
        aliases = {len(args) - 1: 0}
    return pl.pallas_call(
        functools.partial(_combine_kernel, final=final),
        grid=(nt,),
        in_specs=in_specs,
        out_specs=pl.BlockSpec((tm, d), lambda i: (t0 + i, 0)),
        out_shape=jax.ShapeDtypeStruct((n, d), F32),
        input_output_aliases=aliases,
        compiler_params=_params(("parallel",)),
        name="combine",
    )(*args)


def _moe(lat, mods, layer, g2, wr_t, br_t, w_gate, w_up, w_down, g_final, seq, final):
    n, d = lat.shape
    part = n // MOE_PARTS
    n_blocks = (2 * part) // MOE_BM + N_EXPERTS
    out = None
    for p in range(MOE_PARTS):
        tok0 = p * part
        h2, meta_i, wcol, counts = _router(lat, mods, layer, g2, wr_t, br_t, seq, tok0, part)
        dest, block_e, next_e, n_valid, n_used = _plan(counts[:, 0].astype(jnp.int32), meta_i, n_blocks)
        xb = _sc_dispatch(h2, dest, n_blocks * MOE_BM)
        yb = _experts(block_e, next_e, n_valid, n_used, xb, w_gate, w_up, w_down, layer)
        yg = _sc_gather(yb, dest)
        out = _combine(lat, yg, wcol, mods, layer, g_final, seq, final, tok0, out)
    return out


def _rope_tables(seq):
    quarter = HEAD_DIM // 4
    pos = jnp.arange(seq, dtype=F32)
    row_ids = jnp.floor(pos / GRID_W)
    col_ids = pos - row_ids * GRID_W
    inv = ROPE_BASE ** (-jnp.arange(quarter, dtype=F32) / quarter)
    ang_r = row_ids[:, None] * inv
    ang_c = col_ids[:, None] * inv
    zero = jnp.zeros_like(ang_r)
    cos = jnp.concatenate([jnp.cos(ang_r), jnp.cos(ang_r), jnp.cos(ang_c), jnp.cos(ang_c)], axis=-1)
    sa = jnp.concatenate([-jnp.sin(ang_r), zero, -jnp.sin(ang_c), zero], axis=-1)
    sb = jnp.concatenate([zero, jnp.sin(ang_r), zero, jnp.sin(ang_c)], axis=-1)
    rep = LANES // HEAD_DIM
    return tuple(jnp.tile(t, (1, rep)) for t in (cos, sa, sb))


def _router_weights(w_rg, b_rg, w_re, b_re):
    d = w_rg.shape[0]
    pad = EXPERTS_PER_GROUP - N_GROUPS
    wr_t = jnp.concatenate([w_rg.T, jnp.zeros((pad, d), F32), w_re.T], axis=0)
    br_t = jnp.concatenate([b_rg, jnp.full((pad,), NEG_INF, F32), b_re])[:, None]
    return wr_t, br_t


def kernel(x, c, ctx, c_ctx, w_ada, b_ada, g_norm1, g_norm2, g_final, w_in_even, attn_sink, g_sgu,
           w_spatial, b_spatial, w_out_even, w_in_odd, conv_w, w_out_odd, w_router_group,
           b_router_group, w_router_expert, b_router_expert, w_gate, w_up, w_down):
    b, s, d = x.shape
    n = b * s
    n_ctx = ctx.shape[1]
    depth = w_ada.shape[0]
    assert depth == 2 and b + 1 <= MOD_ROWS

    cond = jnp.concatenate([c, c_ctx[None, :], jnp.zeros((MOD_ROWS - b - 1, d), F32)], axis=0)
    mods = _ada(cond, w_ada, b_ada).reshape(depth * MOD_ROWS, 1, 6 * d)
    gf = g_final[None, :]

    lat = x.reshape(n, d)
    w_in_bf = w_in_even[0].astype(BF16)
    tabs = _rope_tables(s)
    q, kx, vx, u, z = _even_in(lat, mods, 0, lambda i: i // (s // EVEN_TM), g_norm1[0][None, :], w_in_bf,
                               tabs, s // EVEN_TM, EVEN_TM)
    ones = jnp.ones((n_ctx, LANES), F32)
    zeros = jnp.zeros((n_ctx, LANES), F32)
    _, kcx, vcx, _, _ = _even_in(ctx.reshape(b * n_ctx, d), mods, 0, lambda i: b, g_norm1[0][None, :],
                                 w_in_bf, (ones, zeros, zeros), 1, n_ctx)
    bsp_full = jnp.repeat(b_spatial[0].T, HEAD_DIM, axis=1)
    lat = _even_mix(lat, q, kx, vx, kcx, vcx, u, z, attn_sink[0], g_sgu[0][None, :],
                    w_spatial[0].astype(BF16), bsp_full, w_out_even[0].astype(BF16), mods, s, n_ctx)
    wr_t, br_t = _router_weights(w_router_group[0], b_router_group[0], w_router_expert[0], b_router_expert[0])
    lat = _moe(lat, mods, 0, g_norm2[0][None, :], wr_t, br_t, w_gate, w_up, w_down, gf, s, False)

    conv_w8 = jnp.concatenate([conv_w[0], jnp.zeros((8 - conv_w.shape[1], d), F32)], axis=0)
    lat = _odd_mix(lat, mods, 1, g_norm1[1][None, :], w_in_odd[0].astype(BF16), conv_w8,
                   w_out_odd[0].astype(BF16), s)
    wr_t, br_t = _router_weights(w_router_group[1], b_router_group[1], w_router_expert[1], b_router_expert[1])
    out = _moe(lat, mods, 1, g_norm2[1][None, :], wr_t, br_t, w_gate, w_up, w_down, gf, s, True)
    return out.reshape(b, s, d)
```

```python
import functools

import jax
import jax.numpy as jnp
from jax import lax
from jax.experimental import pallas as pl
from jax.experimental.pallas import tpu as pltpu
from jax.experimental.pallas import tpu_sc as plsc

F32 = jnp.float32
BF16 = jnp.bfloat16
HIGHEST = lax.Precision.HIGHEST

GRID_W = 64
N_Q_HEADS = 8
N_KV_HEADS = 2
HEAD_DIM = 64
ATT_BLOCK = 128
ROPE_BASE = 10000.0
Q_DIM = N_Q_HEADS * HEAD_DIM
KV_DIM = N_KV_HEADS * HEAD_DIM
SG_GROUPS = 8
SG_WIDTH = SG_GROUPS * HEAD_DIM
N_GROUPS = 4
EXPERTS_PER_GROUP = 8
N_EXPERTS = N_GROUPS * EXPERTS_PER_GROUP
EPS = 1e-6
NEG_INF = -1e30

LANES = 128
SC_CORES = 2
SC_SUBCORES = 16
SC_WORKERS = SC_CORES * SC_SUBCORES
SC_CHUNK = 32
MOD_ROWS = 8
VMEM_LIMIT = 56 * 1024 * 1024

ADA_TN = 1536
EVEN_TM = 512
ATT_TQ = 512
ODD_TM = 512
ROUTE_TM = 512
MOE_BM = 512
MOE_PARTS = 2
COMBINE_TM = 512


def _params(sem):
    return pltpu.CompilerParams(dimension_semantics=sem, vmem_limit_bytes=VMEM_LIMIT)


def _rms_mod(x, g, shift, scale):
    ms = jnp.mean(x * x, axis=-1, keepdims=True)
    return (x * lax.rsqrt(ms + EPS) * g) * (1.0 + scale) + shift


def _pack_rows(a):
    w = a.shape[1] // 2
    hi = pltpu.bitcast(a[:, :w].astype(BF16).astype(F32), jnp.uint32)
    lo = pltpu.bitcast(a[:, w:].astype(BF16).astype(F32), jnp.uint32)
    return hi | (lo >> 16)


def _unpack_rows(p):
    hi = pltpu.bitcast(p & jnp.uint32(0xFFFF0000), F32)
    lo = pltpu.bitcast(p << 16, F32)
    return jnp.concatenate([hi, lo], axis=1)


def _ada_kernel(a_ref, w_ref, b_ref, o_ref):
    a = a_ref[...]
    s = a * (1.0 / (1.0 + jnp.exp(-a)))
    o_ref[0] = jnp.dot(s, w_ref[0], preferred_element_type=F32, precision=HIGHEST) + b_ref[0]


def _ada(cond, w_ada, b_ada):
    depth, d, six_d = w_ada.shape
    return pl.pallas_call(
        _ada_kernel,
        grid=(depth, six_d // ADA_TN),
        in_specs=[
            pl.BlockSpec((MOD_ROWS, d), lambda l, j: (0, 0)),
            pl.BlockSpec((1, d, ADA_TN), lambda l, j: (l, 0, j)),
            pl.BlockSpec((1, 1, ADA_TN), lambda l, j: (l, 0, j)),
        ],
        out_specs=pl.BlockSpec((1, MOD_ROWS, ADA_TN), lambda l, j: (l, 0, j)),
        out_shape=jax.ShapeDtypeStruct((depth, MOD_ROWS, six_d), F32),
        compiler_params=_params(("arbitrary", "arbitrary")),
        name="ada",
    )(cond, w_ada, b_ada.reshape(depth, 1, six_d))


def _even_in_kernel(x_ref, sh_ref, sc_ref, g_ref, w_ref, cos_ref, sa_ref, sb_ref,
                    qx_ref, k_ref, v_ref, u_ref, z_ref):
    h = _rms_mod(x_ref[...], g_ref[...], sh_ref[...], sc_ref[...])
    p = jnp.dot(h.astype(BF16), w_ref[...], preferred_element_type=F32)
    cos, sa, sb = cos_ref[...], sa_ref[...], sb_ref[...]

    def rope(t):
        return t * cos + pltpu.roll(t, LANES - 16, 1) * sa + pltpu.roll(t, 16, 1) * sb

    scale = HEAD_DIM ** -0.5
    low = lax.broadcasted_iota(jnp.int32, (x_ref.shape[0], LANES), 1) < HEAD_DIM
    heads_per_kv = N_Q_HEADS // N_KV_HEADS
    for cblk in range(Q_DIM // LANES):
        t = rope(p[:, cblk * LANES:(cblk + 1) * LANES]) * scale
        sw = pltpu.roll(t, HEAD_DIM, 1)
        zero = jnp.zeros_like(t)
        if (2 * cblk) // heads_per_kv == 0:
            first, second = jnp.where(low, t, zero), jnp.where(low, sw, zero)
        else:
            first, second = jnp.where(low, zero, sw), jnp.where(low, zero, t)
        qx_ref[:, (2 * cblk) * LANES:(2 * cblk + 1) * LANES] = first.astype(BF16)
        qx_ref[:, (2 * cblk + 1) * LANES:(2 * cblk + 2) * LANES] = second.astype(BF16)

    k_ref[...] = rope(p[:, Q_DIM:Q_DIM + KV_DIM]).astype(BF16)
    v_ref[...] = p[:, Q_DIM + KV_DIM:Q_DIM + 2 * KV_DIM].astype(BF16)
    u0 = Q_DIM + 2 * KV_DIM
    u_ref[...] = p[:, u0:u0 + SG_WIDTH]
    z_ref[...] = p[:, u0 + SG_WIDTH:u0 + 2 * SG_WIDTH]


def _even_in(x2d, mods, layer, mod_row_fn, g, w_bf, tabs, tab_blocks, tm):
    n, d = x2d.shape
    ein = w_bf.shape[1]
    cos, sa, sb = tabs
    row = lambda i: layer * MOD_ROWS + mod_row_fn(i)
    tab_spec = pl.BlockSpec((tm, LANES), lambda i: (i % tab_blocks, 0))
    qx_dim = N_Q_HEADS * LANES
    return pl.pallas_call(
        _even_in_kernel,
        grid=(n // tm,),
        in_specs=[
            pl.BlockSpec((tm, d), lambda i: (i, 0)),
            pl.BlockSpec((None, 1, d), lambda i: (row(i), 0, 0)),
            pl.BlockSpec((None, 1, d), lambda i: (row(i), 0, 1)),
            pl.BlockSpec((1, d), lambda i: (0, 0)),
            pl.BlockSpec((d, ein), lambda i: (0, 0)),
            tab_spec, tab_spec, tab_spec,
        ],
        out_specs=[
            pl.BlockSpec((tm, qx_dim), lambda i: (i, 0)),
            pl.BlockSpec((tm, KV_DIM), lambda i: (i, 0)),
            pl.BlockSpec((tm, KV_DIM), lambda i: (i, 0)),
            pl.BlockSpec((tm, SG_WIDTH), lambda i: (i, 0)),
            pl.BlockSpec((tm, SG_WIDTH), lambda i: (i, 0)),
        ],
        out_shape=[
            jax.ShapeDtypeStruct((n, qx_dim), BF16),
            jax.ShapeDtypeStruct((n, KV_DIM), BF16),
            jax.ShapeDtypeStruct((n, KV_DIM), BF16),
            jax.ShapeDtypeStruct((n, SG_WIDTH), F32),
            jax.ShapeDtypeStruct((n, SG_WIDTH), F32),
        ],
        compiler_params=_params(("parallel",)),
        name="even_in",
    )(x2d, mods, mods, g, w_bf, cos, sa, sb)


def _gelu(x):
    return 0.5 * x * (1.0 + lax.erf(x * (2.0 ** -0.5)))


def _even_mix_kernel(sink_ref, lat_ref, qx_ref, km_ref, kp_ref, kn_ref, vm_ref, vp_ref, vn_ref,
                     kc_ref, vc_ref, u_ref, z_ref, gsgu_ref, wsp_ref, bsp_ref, wout_ref, gate_ref,
                     o_ref, kband, vband, mixin, *, tiles_per_seq):
    i = pl.program_id(0)
    tq = qx_ref.shape[0]
    blk = ATT_BLOCK
    nsub = tq // blk
    n_ctx = kc_ref.shape[0]
    first = (i % tiles_per_seq) == 0
    last = (i % tiles_per_seq) == tiles_per_seq - 1

    kband[0:blk] = kp_ref[...]
    kband[blk:blk + tq] = km_ref[...]
    kband[blk + tq:] = kn_ref[...]
    vband[0:blk] = vp_ref[...]
    vband[blk:blk + tq] = vm_ref[...]
    vband[blk + tq:] = vn_ref[...]

    rows = N_Q_HEADS * blk
    tok = lax.broadcasted_iota(jnp.int32, (rows, blk), 0) & (blk - 1)
    col = lax.broadcasted_iota(jnp.int32, (rows, blk), 1)
    tri_prev = col >= tok
    tri_next = col <= tok
    head = lax.broadcasted_iota(jnp.int32, (rows, 1), 0) // blk
    sink_col = jnp.zeros((rows, 1), F32)
    for hd in range(N_Q_HEADS):
        sink_col = jnp.where(head == hd, sink_ref[hd], sink_col)
    lane_low = lax.broadcasted_iota(jnp.int32, (blk, LANES), 1) < HEAD_DIM
    ones = jnp.ones((n_ctx + 3 * blk, LANES), BF16)
    nt = (((1,), (1,)), ((), ()))

    def sub_block(j, carry):
        r0 = pl.multiple_of(j * blk, blk)
        ok_prev = jnp.logical_not(jnp.logical_and(first, j == 0))
        ok_next = jnp.logical_not(jnp.logical_and(last, j == nsub - 1))
        qs = jnp.concatenate([qx_ref[pl.ds(r0, blk), hd * LANES:(hd + 1) * LANES]
                              for hd in range(N_Q_HEADS)], axis=0)
        kall = jnp.concatenate([kc_ref[...], kband[pl.ds(r0, 3 * blk), :]], axis=0)
        vall = jnp.concatenate([vc_ref[...], vband[pl.ds(r0, 3 * blk), :]], axis=0)
        s = lax.dot_general(qs, kall, nt, preferred_element_type=F32)
        c0 = n_ctx
        s = jnp.concatenate([
            s[:, :c0],
            jnp.where(jnp.logical_and(tri_prev, ok_prev), s[:, c0:c0 + blk], NEG_INF),
            s[:, c0 + blk:c0 + 2 * blk],
            jnp.where(jnp.logical_and(tri_next, ok_next), s[:, c0 + 2 * blk:], NEG_INF),
        ], axis=1)
        m = jnp.maximum(jnp.max(s, axis=-1, keepdims=True), sink_col)
        p = jnp.exp(s - m).astype(BF16)
        o = jnp.dot(p, jnp.concatenate([vall, ones], axis=1), preferred_element_type=F32)
        att = o[:, :LANES] / (o[:, LANES:] + jnp.exp(sink_col - m))
        half = N_Q_HEADS // 2
        for hd in range(half):
            pair = jnp.where(lane_low, att[hd * blk:(hd + 1) * blk], att[(hd + half) * blk:(hd + half + 1) * blk])
            mixin[pl.ds(r0, blk), hd * LANES:(hd + 1) * LANES] = pair.astype(BF16)

        ug = _gelu(u_ref[pl.ds(r0, blk), :])
        zg = _gelu(z_ref[pl.ds(r0, blk), :])
        mu = jnp.mean(zg, axis=-1, keepdims=True)
        zc = zg - mu
        zn = zc * lax.rsqrt(jnp.mean(zc * zc, axis=-1, keepdims=True) + EPS) * gsgu_ref[...]
        for pair in range(SG_GROUPS // 2):
            zp = zn[:, pair * LANES:(pair + 1) * LANES]
            zero = jnp.zeros_like(zp)
            lo = jnp.where(lane_low, zp, zero).astype(BF16)
            hi = jnp.where(lane_low, zero, zp).astype(BF16)
            sg = (jnp.dot(wsp_ref[2 * pair], lo, preferred_element_type=F32)
                  + jnp.dot(wsp_ref[2 * pair + 1], hi, preferred_element_type=F32)
                  + bsp_ref[:, pair * LANES:(pair + 1) * LANES])
            mixin[pl.ds(r0, blk), Q_DIM + pair * LANES:Q_DIM + (pair + 1) * LANES] = (
                ug[:, pair * LANES:(pair + 1) * LANES] * sg).astype(BF16)
        return carry

    lax.fori_loop(0, nsub, sub_block, 0)
    mix = jnp.dot(mixin[...], wout_ref[...], preferred_element_type=F32)
    o_ref[...] = lat_ref[...] + gate_ref[...] * mix


def _even_mix(lat, qx, k, v, kc, vc, u, z, sink, g_sgu, wsp_bf, bsp_full, wout_bf, mods, seq, ctx_len):
    n, d = lat.shape
    tq = ATT_TQ
    tiles_per_seq = seq // tq
    sub = tq // ATT_BLOCK
    nblk = n // ATT_BLOCK
    main = lambda w: pl.BlockSpec((tq, w), lambda i: (i, 0))
    prev = pl.BlockSpec((ATT_BLOCK, KV_DIM), lambda i: (jnp.maximum(i * sub - 1, 0), 0))
    nxt = pl.BlockSpec((ATT_BLOCK, KV_DIM), lambda i: (jnp.minimum((i + 1) * sub, nblk - 1), 0))
    ctxs = pl.BlockSpec((ctx_len, KV_DIM), lambda i: (i // tiles_per_seq, 0))
    const = lambda shape: pl.BlockSpec(shape, lambda i: (0,) * len(shape), pipeline_mode=pl.Buffered(1))
    return pl.pallas_call(
        functools.partial(_even_mix_kernel, tiles_per_seq=tiles_per_seq),
        grid=(n // tq,),
        in_specs=[
            pl.BlockSpec(memory_space=pltpu.SMEM),
            main(d), main(qx.shape[1]),
            main(KV_DIM), prev, nxt,
            main(KV_DIM), prev, nxt,
            ctxs, ctxs,
            main(SG_WIDTH), main(SG_WIDTH),
            const((1, SG_WIDTH)), const(wsp_bf.shape), const(bsp_full.shape), const(wout_bf.shape),
            pl.BlockSpec((None, 1, d), lambda i: (i // tiles_per_seq, 0, 2)),
        ],
        out_specs=pl.BlockSpec((tq, d), lambda i: (i, 0)),
        out_shape=jax.ShapeDtypeStruct((n, d), F32),
        scratch_shapes=[
            pltpu.VMEM((tq + 2 * ATT_BLOCK, KV_DIM), BF16),
            pltpu.VMEM((tq + 2 * ATT_BLOCK, KV_DIM), BF16),
            pltpu.VMEM((tq, Q_DIM + SG_WIDTH), BF16),
        ],
        compiler_params=_params(("parallel",)),
        name="even_mix",
    )(sink, lat, qx, k, k, k, v, v, v, kc, vc, u, z, g_sgu, wsp_bf, bsp_full, wout_bf, mods)


def _odd_mix_kernel(x_ref, xp_ref, sh_ref, sc_ref, gate_ref, g_ref, win_ref, cw_ref, wout_ref,
                    o_ref, y_s, bg_s, tail_s, *, tiles_per_seq):
    i = pl.program_id(0)
    n_tiles = pl.num_programs(0) - 1
    tm, d = x_ref.shape
    cur = i % 2
    prv = 1 - cur

    tail_s[...] = y_s[cur, tm - 8:tm, :]

    @pl.when(i < n_tiles)
    def _():
        h = _rms_mod(x_ref[...], g_ref[...], sh_ref[...], sc_ref[...])
        p = jnp.dot(h.astype(BF16), win_ref[...], preferred_element_type=F32)
        bg_s[cur] = p[:, 0:d]
        y_s[cur] = p[:, d:2 * d] * p[:, 2 * d:3 * d]

    @pl.when(i >= 1)
    def _():
        t_prev = i - 1
        first = (t_prev % tiles_per_seq) == 0
        last = (t_prev % tiles_per_seq) == tiles_per_seq - 1
        y = y_s[prv]
        left = jnp.where(first, 0.0, tail_s[7:8, :])
        right = jnp.where(last, 0.0, y_s[cur, 0:1, :])
        ridx = lax.broadcasted_iota(jnp.int32, (tm, d), 0)
        y_dn = jnp.where(ridx == 0, left, pltpu.roll(y, 1, 0))
        y_up = jnp.where(ridx == tm - 1, right, pltpu.roll(y, tm - 1, 0))
        conv = y_dn * cw_ref[0:1, :] + y * cw_ref[1:2, :] + y_up * cw_ref[2:3, :]
        mix = jnp.dot((bg_s[prv] * conv).astype(BF16), wout_ref[...], preferred_element_type=F32)
        o_ref[...] = xp_ref[...] + gate_ref[...] * mix


def _odd_mix(lat, mods, layer, g, win_bf, conv_w8, wout_bf, seq):
    n, d = lat.shape
    tm = ODD_TM
    nt = n // tm
    tiles_per_seq = seq // tm
    cur = lambda i: jnp.minimum(i, nt - 1)
    prv = lambda i: jnp.maximum(i - 1, 0)
    row = lambda t: layer * MOD_ROWS + t // tiles_per_seq
    const = lambda shape: pl.BlockSpec(shape, lambda i: (0,) * len(shape), pipeline_mode=pl.Buffered(1))
    return pl.pallas_call(
        functools.partial(_odd_mix_kernel, tiles_per_seq=tiles_per_seq),
        grid=(nt + 1,),
        in_specs=[
            pl.BlockSpec((tm, d), lambda i: (cur(i), 0)),
            pl.BlockSpec((tm, d), lambda i: (prv(i), 0)),
            pl.BlockSpec((None, 1, d), lambda i: (row(cur(i)), 0, 0)),
            pl.BlockSpec((None, 1, d), lambda i: (row(cur(i)), 0, 1)),
            pl.BlockSpec((None, 1, d), lambda i: (row(prv(i)), 0, 2)),
            const((1, d)), const(win_bf.shape), const(conv_w8.shape), const(wout_bf.shape),
        ],
        out_specs=pl.BlockSpec((tm, d), lambda i: (prv(i), 0)),
        out_shape=jax.ShapeDtypeStruct((n, d), F32),
        scratch_shapes=[
            pltpu.VMEM((2, tm, d), F32),
            pltpu.VMEM((2, tm, d), F32),
            pltpu.VMEM((8, d), F32),
        ],
        compiler_params=_params(("arbitrary",)),
        name="odd_mix",
    )(lat, lat, mods, mods, mods, g, win_bf, conv_w8, wout_bf)


def _router_kernel(x_ref, sh_ref, sc_ref, g_ref, wr_ref, br_ref,
                   h_ref, mi_ref, wc_ref, cnt_ref, carry):
    i = pl.program_id(0)
    tm = x_ref.shape[0]
    epg = EXPERTS_PER_GROUP

    @pl.when(i == 0)
    def _():
        carry[...] = jnp.zeros_like(carry)

    h = _rms_mod(x_ref[...], g_ref[...], sh_ref[...], sc_ref[...])
    h_ref[...] = _pack_rows(h)
    lg = lax.dot_general(wr_ref[...], h, (((1,), (1,)), ((), ())),
                         preferred_element_type=F32, precision=HIGHEST) + br_ref[...]
    io8 = lax.broadcasted_iota(jnp.int32, (epg, tm), 0)
    gl = lg[0:epg]
    gmax = jnp.max(gl, axis=0, keepdims=True)
    g_idx = jnp.min(jnp.where(gl == gmax, io8, epg), axis=0, keepdims=True)
    g_w = 1.0 / jnp.sum(jnp.exp(gl - gmax), axis=0, keepdims=True)
    e_sel = lg[epg:2 * epg]
    for gi in range(1, N_GROUPS):
        e_sel = jnp.where(g_idx == gi, lg[(gi + 1) * epg:(gi + 2) * epg], e_sel)
    v0 = jnp.max(e_sel, axis=0, keepdims=True)
    i0 = jnp.min(jnp.where(e_sel == v0, io8, epg), axis=0, keepdims=True)
    rest = jnp.where(io8 == i0, -jnp.inf, e_sel)
    v1 = jnp.max(rest, axis=0, keepdims=True)
    i1 = jnp.min(jnp.where(rest == v1, io8, epg), axis=0, keepdims=True)
    t = jnp.exp(v1 - v0)
    w0 = g_w / (1.0 + t)
    w1 = g_w * t / (1.0 + t)
    e0 = g_idx * epg + i0
    e1 = g_idx * epg + i1

    io32 = lax.broadcasted_iota(jnp.int32, (N_EXPERTS, tm), 0)
    hit0 = io32 == e0
    hit1 = io32 == e1
    onehot = jnp.where(hit0 | hit1, 1.0, 0.0)
    r_i = lax.broadcasted_iota(jnp.int32, (tm, tm), 0)
    c_i = lax.broadcasted_iota(jnp.int32, (tm, tm), 1)
    upper = jnp.where(r_i < c_i, 1.0, 0.0).astype(BF16)
    cum = jnp.dot(onehot.astype(BF16), upper, preferred_element_type=F32) + carry[...]
    rank0 = jnp.sum(jnp.where(hit0, cum, 0.0), axis=0, keepdims=True).astype(jnp.int32)
    rank1 = jnp.sum(jnp.where(hit1, cum, 0.0), axis=0, keepdims=True).astype(jnp.int32)
    carry[...] = carry[...] + jnp.sum(onehot, axis=1, keepdims=True)
    cnt_ref[...] = jnp.broadcast_to(carry[...], cnt_ref.shape)

    mi_ref[...] = jnp.where(io8 == 0, e0, jnp.where(io8 == 1, e1, jnp.where(io8 == 2, rank0,
                            jnp.where(io8 == 3, rank1, 0))))
    io128 = lax.broadcasted_iota(jnp.int32, (LANES, tm), 0)
    wrow = jnp.where(io128 == 0, w0, jnp.where(io128 == 1, w1, 0.0))
    wc_ref[...] = wrow.T


def _router(lat, mods, layer, g, wr_t, br_t, seq, tok0, n):
    d = lat.shape[1]
    tm = ROUTE_TM
    tiles_per_seq = seq // tm
    t0 = tok0 // tm
    row = lambda i: layer * MOD_ROWS + (t0 + i) // tiles_per_seq
    const = lambda shape: pl.BlockSpec(shape, lambda i: (0,) * len(shape), pipeline_mode=pl.Buffered(1))
    return pl.pallas_call(
        _router_kernel,
        grid=(n // tm,),
        in_specs=[
            pl.BlockSpec((tm, d), lambda i: (t0 + i, 0)),
            pl.BlockSpec((None, 1, d), lambda i: (row(i), 0, 3)),
            pl.BlockSpec((None, 1, d), lambda i: (row(i), 0, 4)),
            const((1, d)), const(wr_t.shape), const(br_t.shape),
        ],
        out_specs=[
            pl.BlockSpec((tm, d // 2), lambda i: (i, 0)),
            pl.BlockSpec((8, tm), lambda i: (0, i)),
            pl.BlockSpec((tm, LANES), lambda i: (i, 0)),
            pl.BlockSpec((N_EXPERTS, LANES), lambda i: (0, 0)),
        ],
        out_shape=[
            jax.ShapeDtypeStruct((n, d // 2), jnp.uint32),
            jax.ShapeDtypeStruct((8, n), jnp.int32),
            jax.ShapeDtypeStruct((n, LANES), F32),
            jax.ShapeDtypeStruct((N_EXPERTS, LANES), F32),
        ],
        scratch_shapes=[pltpu.VMEM((N_EXPERTS, 1), F32)],
        compiler_params=_params(("arbitrary",)),
        name="router",
    )(lat, mods, mods, g, wr_t, br_t)


def _plan_kernel(cnt_ref, mi_ref, dest_ref, be_ref, nx_ref, nv_ref, nu_ref, ps_ref, *, n_blocks):
    bm = MOE_BM

    def per_expert(e, blk0):
        cnt = cnt_ref[e]
        nb = (cnt + bm - 1) // bm
        ps_ref[e] = blk0 * bm

        def fill(b, c):
            be_ref[b] = e
            nv_ref[b] = jnp.minimum(cnt - (b - blk0) * bm, bm)
            return c

        lax.fori_loop(blk0, blk0 + nb, fill, 0)
        return blk0 + nb

    n_used = lax.fori_loop(0, N_EXPERTS, per_expert, 0)
    nu_ref[0] = n_used
    last_e = be_ref[jnp.maximum(n_used - 1, 0)]

    def fill_tail(b, c):
        be_ref[b] = last_e
        nv_ref[b] = 0
        return c

    lax.fori_loop(n_used, n_blocks, fill_tail, 0)

    nx_ref[n_blocks - 1] = last_e

    def link(k, c):
        b = n_blocks - 2 - k
        nx_ref[b] = jnp.where(be_ref[b + 1] != be_ref[b], be_ref[b + 1], nx_ref[b + 1])
        return c

    lax.fori_loop(0, n_blocks - 1, link, 0)

    e01 = mi_ref[0:2, :]
    dest = mi_ref[2:4, :]
    for e in range(N_EXPERTS):
        dest = dest + jnp.where(e01 == e, ps_ref[e], 0)
    dest_ref[...] = dest


def _plan(counts, meta_i, n_blocks):
    n = meta_i.shape[1]
    return pl.pallas_call(
        functools.partial(_plan_kernel, n_blocks=n_blocks),
        in_specs=[pl.BlockSpec(memory_space=pltpu.SMEM), pl.BlockSpec(memory_space=pltpu.VMEM)],
        out_specs=[pl.BlockSpec(memory_space=pltpu.VMEM)] + [pl.BlockSpec(memory_space=pltpu.SMEM)] * 4,
        out_shape=[
            jax.ShapeDtypeStruct((2, n), jnp.int32),
            jax.ShapeDtypeStruct((n_blocks,), jnp.int32),
            jax.ShapeDtypeStruct((n_blocks,), jnp.int32),
            jax.ShapeDtypeStruct((n_blocks,), jnp.int32),
            jax.ShapeDtypeStruct((1,), jnp.int32),
        ],
        scratch_shapes=[pltpu.SMEM((N_EXPERTS,), jnp.int32)],
        compiler_params=pltpu.CompilerParams(vmem_limit_bytes=VMEM_LIMIT),
        name="plan",
    )(counts, meta_i)


def _sc_mesh():
    return plsc.VectorSubcoreMesh(core_axis_name="c", subcore_axis_name="s",
                                  num_cores=SC_CORES, num_subcores=SC_SUBCORES)


def _sc_worker():
    return lax.axis_index("s") * SC_CORES + lax.axis_index("c")


def _sc_dispatch(h2, dest, n_rows):
    n, d = h2.shape
    c = SC_CHUNK
    per_w = n // SC_WORKERS
    nchunk = per_w // c
    idx = dest.reshape(2, SC_WORKERS, nchunk, c)

    @functools.partial(
        pl.kernel, mesh=_sc_mesh(), out_type=jax.ShapeDtypeStruct((n_rows, d), h2.dtype),
        scratch_types=[pltpu.VMEM((nchunk, c), jnp.int32), pltpu.VMEM((nchunk, c), jnp.int32),
                       pltpu.VMEM((2, c, d), h2.dtype),
                       pltpu.SemaphoreType.DMA((2,)), pltpu.SemaphoreType.DMA((2,))])
    def k(h_hbm, idx_hbm, xb_hbm, idx0_v, idx1_v, rows_v, gsem, ssem):
        wid = _sc_worker()
        base = wid * per_w
        idx_v = (idx0_v, idx1_v)
        for kk in range(2):
            pltpu.sync_copy(idx_hbm.at[kk, wid], idx_v[kk])

        def get(j, slot):
            return pltpu.make_async_copy(h_hbm.at[pl.ds(base + j * c, c)], rows_v.at[slot], gsem.at[slot])

        def put(j, slot, kk):
            return pltpu.make_async_copy(rows_v.at[slot], xb_hbm.at[idx_v[kk].at[j]], ssem.at[slot])

        get(0, 0).start()

        @pl.loop(0, nchunk, step=2)
        def _(j):
            for slot in range(2):
                jj = j + slot
                get(jj, slot).wait()

                @pl.when(jj >= 1)
                def _():
                    for kk in range(2):
                        put(jj - 1, 1 - slot, kk).wait()

                @pl.when(jj + 1 < nchunk)
                def _():
                    get(jj + 1, 1 - slot).start()

                for kk in range(2):
                    put(jj, slot, kk).start()

        for kk in range(2):
            put(nchunk - 1, (nchunk - 1) % 2, kk).wait()

    return k(h2, idx)


def _sc_gather(y, dest):
    d = y.shape[1]
    total = dest.shape[0] * dest.shape[1]
    c = SC_CHUNK
    per_w = total // SC_WORKERS
    nchunk = per_w // c
    idx = dest.reshape(SC_WORKERS, nchunk, c)

    @functools.partial(
        pl.kernel, mesh=_sc_mesh(), out_type=jax.ShapeDtypeStruct((total, d), y.dtype),
        scratch_types=[pltpu.VMEM((nchunk, c), jnp.int32), pltpu.VMEM((2, c, d), y.dtype),
                       pltpu.SemaphoreType.DMA((2,)), pltpu.SemaphoreType.DMA((2,))])
    def k(y_hbm, idx_hbm, out_hbm, idx_v, rows_v, gsem, ssem):
        wid = _sc_worker()
        base = wid * per_w
        pltpu.sync_copy(idx_hbm.at[wid], idx_v)

        def get(j, slot):
            return pltpu.make_async_copy(y_hbm.at[idx_v.at[j]], rows_v.at[slot], gsem.at[slot])

        def put(j, slot):
            return pltpu.make_async_copy(rows_v.at[slot], out_hbm.at[pl.ds(base + j * c, c)], ssem.at[slot])

        get(0, 0).start()

        @pl.loop(0, nchunk, step=2)
        def _(j):
            for slot in range(2):
                jj = j + slot
                get(jj, slot).wait()

                @pl.when(jj >= 1)
                def _():
                    put(jj - 1, 1 - slot).wait()

                @pl.when(jj + 1 < nchunk)
                def _():
                    get(jj + 1, 1 - slot).start()

                put(jj, slot).start()

        put(nchunk - 1, (nchunk - 1) % 2).wait()

    return k(y, idx)


def _expert_kernel(be_ref, nx_ref, nv_ref, nu_ref, x_ref, wg_hbm, wu_hbm, wd_hbm, y_ref,
                   wgu_s, wd_s, stg_g, stg_u, stg_d, run_s, sems, *, layer):
    b = pl.program_id(0)
    hid = stg_g.shape[2]
    e = be_ref[b]
    changed = jnp.logical_or(b == 0, e != be_ref[jnp.maximum(b - 1, 0)])

    def fetch(expert, slot):
        return (pltpu.make_async_copy(wg_hbm.at[layer, expert], stg_g.at[slot], sems.at[slot]),
                pltpu.make_async_copy(wu_hbm.at[layer, expert], stg_u.at[slot], sems.at[slot]),
                pltpu.make_async_copy(wd_hbm.at[layer, expert], stg_d.at[slot], sems.at[slot]))

    @pl.when(b == 0)
    def _():
        run_s[0] = 0
        for cp in fetch(e, 0):
            cp.start()

    @pl.when(changed)
    def _():
        run = jnp.where(b == 0, 0, run_s[0] + 1)
        run_s[0] = run
        slot = run % 2
        for cp in fetch(e, slot):
            cp.wait()

        @pl.when(nx_ref[b] != e)
        def _():
            for cp in fetch(nx_ref[b], 1 - slot):
                cp.start()

        wgu_s[:, 0:hid] = stg_g[slot].astype(BF16)
        wgu_s[:, hid:2 * hid] = stg_u[slot].astype(BF16)
        wd_s[...] = stg_d[slot].astype(BF16)

    @pl.when(b < nu_ref[0])
    def _():
        live = lax.broadcasted_iota(jnp.int32, x_ref.shape, 0) < nv_ref[b]
        x = _unpack_rows(jnp.where(live, x_ref[...], jnp.uint32(0)))
        gu = jnp.dot(x.astype(BF16), wgu_s[...], preferred_element_type=F32)
        gate = gu[:, 0:hid]
        act = gate * (1.0 / (1.0 + jnp.exp(-gate))) * gu[:, hid:2 * hid]
        y_ref[...] = _pack_rows(jnp.dot(act.astype(BF16), wd_s[...], preferred_element_type=F32))

    @pl.when(b >= nu_ref[0])
    def _():
        y_ref[...] = jnp.zeros_like(y_ref)


def _experts(block_e, next_e, n_valid, n_used, xb, w_gate, w_up, w_down, layer):
    n_rows, dp = xb.shape
    d, hid = w_gate.shape[2], w_gate.shape[3]
    bm = MOE_BM
    n_blocks = n_rows // bm
    hbm = pl.BlockSpec(memory_space=pl.ANY)
    return pl.pallas_call(
        functools.partial(_expert_kernel, layer=layer),
        grid_spec=pltpu.PrefetchScalarGridSpec(
            num_scalar_prefetch=4,
            grid=(n_blocks,),
            in_specs=[
                pl.BlockSpec((bm, dp), lambda b, be, nx, nv, nu: (jnp.minimum(b, nu[0] - 1), 0)),
                hbm, hbm, hbm,
            ],
            out_specs=pl.BlockSpec((bm, dp), lambda b, be, nx, nv, nu: (b, 0)),
            scratch_shapes=[
                pltpu.VMEM((d, 2 * hid), BF16), pltpu.VMEM((hid, d), BF16),
                pltpu.VMEM((2, d, hid), F32), pltpu.VMEM((2, d, hid), F32), pltpu.VMEM((2, hid, d), F32),
                pltpu.SMEM((1,), jnp.int32), pltpu.SemaphoreType.DMA((2,)),
            ],
        ),
        out_shape=jax.ShapeDtypeStruct((n_rows, dp), jnp.uint32),
        compiler_params=_params(("arbitrary",)),
        name="experts",
    )(block_e, next_e, n_valid, n_used, xb, w_gate, w_up, w_down)


def _combine_kernel(lat_ref, y0_ref, y1_ref, wc_ref, gate_ref, gf_ref, *rest, final):
    o_ref = rest[-1]
    wc = wc_ref[...]
    moe = wc[:, 0:1] * _unpack_rows(y0_ref[...]) + wc[:, 1:2] * _unpack_rows(y1_ref[...])
    out = lat_ref[...] + gate_ref[...] * moe
    if final:
        ms = jnp.mean(out * out, axis=-1, keepdims=True)
        out = out * lax.rsqrt(ms + EPS) * gf_ref[...]
    o_ref[...] = out


def _combine(lat, yg, wcol, mods, layer, g_final, seq, final, tok0, prev_out):
    n, d = lat.shape
    tm = COMBINE_TM
    nt = wcol.shape[0] // tm
    t0 = tok0 // tm
    tiles_per_seq = seq // tm
    row = lambda i: layer * MOD_ROWS + (t0 + i) // tiles_per_seq
    in_specs = [
        pl.BlockSpec((tm, d), lambda i: (t0 + i, 0)),
        pl.BlockSpec((tm, d // 2), lambda i: (i, 0)),
        pl.BlockSpec((tm, d // 2), lambda i: (nt + i, 0)),
        pl.BlockSpec((tm, LANES), lambda i: (i, 0)),
        pl.BlockSpec((None, 1, d), lambda i: (row(i), 0, 5)),
        pl.BlockSpec((1, d), lambda i: (0, 0)),
    ]
    args = [lat, yg, yg, wcol, mods, g_final]
    aliases = {}
    if prev_out is not None:
        in_specs.append(pl.BlockSpec(memory_space=pl.ANY))
        args.append(prev_out)
        aliases = {len(args) - 1: 0}
    return pl.pallas_call(
        functools.partial(_combine_kernel, final=final),
        grid=(nt,),
        in_specs=in_specs,
        out_specs=pl.BlockSpec((tm, d), lambda i: (t0 + i, 0)),
        out_shape=jax.ShapeDtypeStruct((n, d), F32),
        input_output_aliases=aliases,
        compiler_params=_params(("parallel",)),
        name="combine",
    )(*args)


def _moe(lat, mods, layer, g2, wr_t, br_t, w_gate, w_up, w_down, g_final, seq, final):
    n, d = lat.shape
    part = n // MOE_PARTS
    n_blocks = (2 * part) // MOE_BM + N_EXPERTS
    out = None
    for p in range(MOE_PARTS):
        tok0 = p * part
        h2, meta_i, wcol, counts = _router(lat, mods, layer, g2, wr_t, br_t, seq, tok0, part)
        dest, block_e, next_e, n_valid, n_used = _plan(counts[:, 0].astype(jnp.int32), meta_i, n_blocks)
        xb = _sc_dispatch(h2, dest, n_blocks * MOE_BM)
        yb = _experts(block_e, next_e, n_valid, n_used, xb, w_gate, w_up, w_down, layer)
        yg = _sc_gather(yb, dest)
        out = _combine(lat, yg, wcol, mods, layer, g_final, seq, final, tok0, out)
    return out


def _rope_tables(seq):
    quarter = HEAD_DIM // 4
    pos = jnp.arange(seq, dtype=F32)
    row_ids = jnp.floor(pos / GRID_W)
    col_ids = pos - row_ids * GRID_W
    inv = ROPE_BASE ** (-jnp.arange(quarter, dtype=F32) / quarter)
    ang_r = row_ids[:, None] * inv
    ang_c = col_ids[:, None] * inv
    zero = jnp.zeros_like(ang_r)
    cos = jnp.concatenate([jnp.cos(ang_r), jnp.cos(ang_r), jnp.cos(ang_c), jnp.cos(ang_c)], axis=-1)
    sa = jnp.concatenate([-jnp.sin(ang_r), zero, -jnp.sin(ang_c), zero], axis=-1)
    sb = jnp.concatenate([zero, jnp.sin(ang_r), zero, jnp.sin(ang_c)], axis=-1)
    rep = LANES // HEAD_DIM
    return tuple(jnp.tile(t, (1, rep)) for t in (cos, sa, sb))


def _router_weights(w_rg, b_rg, w_re, b_re):
    d = w_rg.shape[0]
    pad = EXPERTS_PER_GROUP - N_GROUPS
    wr_t = jnp.concatenate([w_rg.T, jnp.zeros((pad, d), F32), w_re.T], axis=0)
    br_t = jnp.concatenate([b_rg, jnp.full((pad,), NEG_INF, F32), b_re])[:, None]
    return wr_t, br_t


def kernel(x, c, ctx, c_ctx, w_ada, b_ada, g_norm1, g_norm2, g_final, w_in_even, attn_sink, g_sgu,
           w_spatial, b_spatial, w_out_even, w_in_odd, conv_w, w_out_odd, w_router_group,
           b_router_group, w_router_expert, b_router_expert, w_gate, w_up, w_down):
    b, s, d = x.shape
    n = b * s
    n_ctx = ctx.shape[1]
    depth = w_ada.shape[0]
    assert depth == 2 and b + 1 <= MOD_ROWS

    cond = jnp.concatenate([c, c_ctx[None, :], jnp.zeros((MOD_ROWS - b - 1, d), F32)], axis=0)
    mods = _ada(cond, w_ada, b_ada).reshape(depth * MOD_ROWS, 1, 6 * d)
    gf = g_final[None, :]

    lat = x.reshape(n, d)
    w_in_bf = w_in_even[0].astype(BF16)
    tabs = _rope_tables(s)
    qx, k, v, u, z = _even_in(lat, mods, 0, lambda i: i // (s // EVEN_TM), g_norm1[0][None, :], w_in_bf,
                              tabs, s // EVEN_TM, EVEN_TM)
    ones = jnp.ones((n_ctx, LANES), F32)
    zeros = jnp.zeros((n_ctx, LANES), F32)
    _, kc, vc, _, _ = _even_in(ctx.reshape(b * n_ctx, d), mods, 0, lambda i: b, g_norm1[0][None, :],
                               w_in_bf, (ones, zeros, zeros), 1, n_ctx)
    bsp_full = jnp.repeat(b_spatial[0].T, HEAD_DIM, axis=1)
    half = N_Q_HEADS // 2
    w_att = w_out_even[0][:Q_DIM].reshape(2, half, HEAD_DIM, d).transpose(1, 0, 2, 3).reshape(Q_DIM, d)
    w_out_bf = jnp.concatenate([w_att, w_out_even[0][Q_DIM:]], axis=0).astype(BF16)
    lat = _even_mix(lat, qx, k, v, kc, vc, u, z, attn_sink[0], g_sgu[0][None, :],
                    w_spatial[0].astype(BF16), bsp_full, w_out_bf, mods, s, n_ctx)
    wr_t, br_t = _router_weights(w_router_group[0], b_router_group[0], w_router_expert[0], b_router_expert[0])
    lat = _moe(lat, mods, 0, g_norm2[0][None, :], wr_t, br_t, w_gate, w_up, w_down, gf, s, False)

    conv_w8 = jnp.concatenate([conv_w[0], jnp.zeros((8 - conv_w.shape[1], d), F32)], axis=0)
    lat = _odd_mix(lat, mods, 1, g_norm1[1][None, :], w_in_odd[0].astype(BF16), conv_w8,
                   w_out_odd[0].astype(BF16), s)
    wr_t, br_t = _router_weights(w_router_group[1], b_router_group[1], w_router_expert[1], b_router_expert[1])
    out = _moe(lat, mods, 1, g_norm2[1][None, :], wr_t, br_t, w_gate, w_up, w_down, gf, s, True)
    return out.reshape(b, s, d)
```

```python
import functools

import jax
import jax.numpy as jnp
from jax import lax
from jax.experimental import pallas as pl
from jax.experimental.pallas import tpu as pltpu
from jax.experimental.pallas import tpu_sc as plsc

F32 = jnp.float32
BF16 = jnp.bfloat16
HIGHEST = lax.Precision.HIGHEST

GRID_W = 64
N_Q_HEADS = 8
N_KV_HEADS = 2
HEAD_DIM = 64
ATT_BLOCK = 128
ROPE_BASE = 10000.0
Q_DIM = N_Q_HEADS * HEAD_DIM
KV_DIM = N_KV_HEADS * HEAD_DIM
SG_GROUPS = 8
SG_WIDTH = SG_GROUPS * HEAD_DIM
N_GROUPS = 4
EXPERTS_PER_GROUP = 8
N_EXPERTS = N_GROUPS * EXPERTS_PER_GROUP
EPS = 1e-6
NEG_INF = -1e30

LANES = 128
SC_CORES = 2
SC_SUBCORES = 16
SC_WORKERS = SC_CORES * SC_SUBCORES
SC_CHUNK = 32
MOD_ROWS = 8
VMEM_LIMIT = 56 * 1024 * 1024

ADA_TN = 1536
EVEN_TM = 512
ATT_TQ = 512
ODD_TM = 512
ROUTE_TM = 512
MOE_BM = 512
MOE_PARTS = 2
COMBINE_TM = 512


def _params(sem):
    return pltpu.CompilerParams(dimension_semantics=sem, vmem_limit_bytes=VMEM_LIMIT)


def _rms_mod(x, g, shift, scale):
    ms = jnp.mean(x * x, axis=-1, keepdims=True)
    return (x * lax.rsqrt(ms + EPS) * g) * (1.0 + scale) + shift


def _pack_rows(a):
    w = a.shape[1] // 2
    hi = pltpu.bitcast(a[:, :w].astype(BF16).astype(F32), jnp.uint32)
    lo = pltpu.bitcast(a[:, w:].astype(BF16).astype(F32), jnp.uint32)
    return hi | (lo >> 16)


def _unpack_rows(p):
    hi = pltpu.bitcast(p & jnp.uint32(0xFFFF0000), F32)
    lo = pltpu.bitcast(p << 16, F32)
    return jnp.concatenate([hi, lo], axis=1)


def _ada_kernel(a_ref, w_ref, b_ref, o_ref):
    a = a_ref[...]
    s = a * (1.0 / (1.0 + jnp.exp(-a)))
    o_ref[0] = jnp.dot(s, w_ref[0], preferred_element_type=F32, precision=HIGHEST) + b_ref[0]


def _ada(cond, w_ada, b_ada):
    depth, d, six_d = w_ada.shape
    return pl.pallas_call(
        _ada_kernel,
        grid=(depth, six_d // ADA_TN),
        in_specs=[
            pl.BlockSpec((MOD_ROWS, d), lambda l, j: (0, 0)),
            pl.BlockSpec((1, d, ADA_TN), lambda l, j: (l, 0, j)),
            pl.BlockSpec((1, 1, ADA_TN), lambda l, j: (l, 0, j)),
        ],
        out_specs=pl.BlockSpec((1, MOD_ROWS, ADA_TN), lambda l, j: (l, 0, j)),
        out_shape=jax.ShapeDtypeStruct((depth, MOD_ROWS, six_d), F32),
        compiler_params=_params(("arbitrary", "arbitrary")),
        name="ada",
    )(cond, w_ada, b_ada.reshape(depth, 1, six_d))


def _even_in_kernel(x_ref, sh_ref, sc_ref, g_ref, w_ref, cos_ref, sa_ref, sb_ref,
                    qx_ref, k_ref, v_ref, u_ref, z_ref):
    h = _rms_mod(x_ref[...], g_ref[...], sh_ref[...], sc_ref[...])
    p = jnp.dot(h.astype(BF16), w_ref[...], preferred_element_type=F32)
    cos, sa, sb = cos_ref[...], sa_ref[...], sb_ref[...]

    def rope(t):
        return t * cos + pltpu.roll(t, LANES - 16, 1) * sa + pltpu.roll(t, 16, 1) * sb

    scale = HEAD_DIM ** -0.5
    low = lax.broadcasted_iota(jnp.int32, (x_ref.shape[0], LANES), 1) < HEAD_DIM
    heads_per_kv = N_Q_HEADS // N_KV_HEADS
    for cblk in range(Q_DIM // LANES):
        t = rope(p[:, cblk * LANES:(cblk + 1) * LANES]) * scale
        sw = pltpu.roll(t, HEAD_DIM, 1)
        zero = jnp.zeros_like(t)
        if (2 * cblk) // heads_per_kv == 0:
            first, second = jnp.where(low, t, zero), jnp.where(low, sw, zero)
        else:
            first, second = jnp.where(low, zero, sw), jnp.where(low, zero, t)
        qx_ref[:, (2 * cblk) * LANES:(2 * cblk + 1) * LANES] = first.astype(BF16)
        qx_ref[:, (2 * cblk + 1) * LANES:(2 * cblk + 2) * LANES] = second.astype(BF16)

    k_ref[...] = rope(p[:, Q_DIM:Q_DIM + KV_DIM]).astype(BF16)
    v_ref[...] = p[:, Q_DIM + KV_DIM:Q_DIM + 2 * KV_DIM].astype(BF16)
    u0 = Q_DIM + 2 * KV_DIM
    u_ref[...] = p[:, u0:u0 + SG_WIDTH]
    z_ref[...] = p[:, u0 + SG_WIDTH:u0 + 2 * SG_WIDTH]


def _even_in(x2d, mods, layer, mod_row_fn, g, w_bf, tabs, tab_blocks, tm):
    n, d = x2d.shape
    ein = w_bf.shape[1]
    cos, sa, sb = tabs
    row = lambda i: layer * MOD_ROWS + mod_row_fn(i)
    tab_spec = pl.BlockSpec((tm, LANES), lambda i: (i % tab_blocks, 0))
    qx_dim = N_Q_HEADS * LANES
    return pl.pallas_call(
        _even_in_kernel,
        grid=(n // tm,),
        in_specs=[
            pl.BlockSpec((tm, d), lambda i: (i, 0)),
            pl.BlockSpec((None, 1, d), lambda i: (row(i), 0, 0)),
            pl.BlockSpec((None, 1, d), lambda i: (row(i), 0, 1)),
            pl.BlockSpec((1, d), lambda i: (0, 0)),
            pl.BlockSpec((d, ein), lambda i: (0, 0)),
            tab_spec, tab_spec, tab_spec,
        ],
        out_specs=[
            pl.BlockSpec((tm, qx_dim), lambda i: (i, 0)),
            pl.BlockSpec((tm, KV_DIM), lambda i: (i, 0)),
            pl.BlockSpec((tm, KV_DIM), lambda i: (i, 0)),
            pl.BlockSpec((tm, SG_WIDTH), lambda i: (i, 0)),
            pl.BlockSpec((tm, SG_WIDTH), lambda i: (i, 0)),
        ],
        out_shape=[
            jax.ShapeDtypeStruct((n, qx_dim), BF16),
            jax.ShapeDtypeStruct((n, KV_DIM), BF16),
            jax.ShapeDtypeStruct((n, KV_DIM), BF16),
            jax.ShapeDtypeStruct((n, SG_WIDTH), F32),
            jax.ShapeDtypeStruct((n, SG_WIDTH), F32),
        ],
        compiler_params=_params(("parallel",)),
        name="even_in",
    )(x2d, mods, mods, g, w_bf, cos, sa, sb)


def _gelu(x):
    return 0.5 * x * (1.0 + lax.erf(x * (2.0 ** -0.5)))


def _even_mix_kernel(sink_ref, lat_ref, qx_ref, km_ref, kp_ref, kn_ref, vm_ref, vp_ref, vn_ref,
                     kc_ref, vc_ref, u_ref, z_ref, gsgu_ref, wsp_ref, bsp_ref, wout_ref, gate_ref,
                     o_ref, kband, vband, mixin, *, tiles_per_seq):
    i = pl.program_id(0)
    tq = qx_ref.shape[0]
    blk = ATT_BLOCK
    nsub = tq // blk
    n_ctx = kc_ref.shape[0]
    first = (i % tiles_per_seq) == 0
    last = (i % tiles_per_seq) == tiles_per_seq - 1

    kband[0:blk] = kp_ref[...]
    kband[blk:blk + tq] = km_ref[...]
    kband[blk + tq:] = kn_ref[...]
    vband[0:blk] = vp_ref[...]
    vband[blk:blk + tq] = vm_ref[...]
    vband[blk + tq:] = vn_ref[...]

    rows = N_Q_HEADS * blk
    tok = lax.broadcasted_iota(jnp.int32, (rows, blk), 0) & (blk - 1)
    col = lax.broadcasted_iota(jnp.int32, (rows, blk), 1)
    tri_prev = col >= tok
    tri_next = col <= tok
    head = lax.broadcasted_iota(jnp.int32, (rows, 1), 0) // blk
    sink_col = jnp.zeros((rows, 1), F32)
    for hd in range(N_Q_HEADS):
        sink_col = jnp.where(head == hd, sink_ref[hd], sink_col)
    lane_low = lax.broadcasted_iota(jnp.int32, (blk, LANES), 1) < HEAD_DIM
    ones = jnp.ones((n_ctx + 3 * blk, LANES), BF16)
    nt = (((1,), (1,)), ((), ()))

    def sub_block(j, carry):
        r0 = pl.multiple_of(j * blk, blk)
        ok_prev = jnp.logical_not(jnp.logical_and(first, j == 0))
        ok_next = jnp.logical_not(jnp.logical_and(last, j == nsub - 1))
        qs = jnp.concatenate([qx_ref[pl.ds(r0, blk), hd * LANES:(hd + 1) * LANES]
                              for hd in range(N_Q_HEADS)], axis=0)
        kall = jnp.concatenate([kc_ref[...], kband[pl.ds(r0, 3 * blk), :]], axis=0)
        vall = jnp.concatenate([vc_ref[...], vband[pl.ds(r0, 3 * blk), :]], axis=0)
        s = lax.dot_general(qs, kall, nt, preferred_element_type=F32)
        c0 = n_ctx
        s = jnp.concatenate([
            s[:, :c0],
            jnp.where(jnp.logical_and(tri_prev, ok_prev), s[:, c0:c0 + blk], NEG_INF),
            s[:, c0 + blk:c0 + 2 * blk],
            jnp.where(jnp.logical_and(tri_next, ok_next), s[:, c0 + 2 * blk:], NEG_INF),
        ], axis=1)
        m = jnp.maximum(jnp.max(s, axis=-1, keepdims=True), sink_col)
        p = jnp.exp(s - m).astype(BF16)
        o = jnp.dot(p, jnp.concatenate([vall, ones], axis=1), preferred_element_type=F32)
        att = o[:, :LANES] / (o[:, LANES:] + jnp.exp(sink_col - m))
        half = N_Q_HEADS // 2
        for hd in range(half):
            pair = jnp.where(lane_low, att[hd * blk:(hd + 1) * blk], att[(hd + half) * blk:(hd + half + 1) * blk])
            mixin[pl.ds(r0, blk), hd * LANES:(hd + 1) * LANES] = pair.astype(BF16)

        ug = _gelu(u_ref[pl.ds(r0, blk), :])
        zg = _gelu(z_ref[pl.ds(r0, blk), :])
        mu = jnp.mean(zg, axis=-1, keepdims=True)
        zc = zg - mu
        zn = zc * lax.rsqrt(jnp.mean(zc * zc, axis=-1, keepdims=True) + EPS) * gsgu_ref[...]
        for pair in range(SG_GROUPS // 2):
            zp = zn[:, pair * LANES:(pair + 1) * LANES]
            zero = jnp.zeros_like(zp)
            lo = jnp.where(lane_low, zp, zero).astype(BF16)
            hi = jnp.where(lane_low, zero, zp).astype(BF16)
            sg = (jnp.dot(wsp_ref[2 * pair], lo, preferred_element_type=F32)
                  + jnp.dot(wsp_ref[2 * pair + 1], hi, preferred_element_type=F32)
                  + bsp_ref[:, pair * LANES:(pair + 1) * LANES])
            mixin[pl.ds(r0, blk), Q_DIM + pair * LANES:Q_DIM + (pair + 1) * LANES] = (
                ug[:, pair * LANES:(pair + 1) * LANES] * sg).astype(BF16)
        return carry

    lax.fori_loop(0, nsub, sub_block, 0)
    mix = jnp.dot(mixin[...], wout_ref[...], preferred_element_type=F32)
    o_ref[...] = lat_ref[...] + gate_ref[...] * mix


def _even_mix(lat, qx, k, v, kc, vc, u, z, sink, g_sgu, wsp_bf, bsp_full, wout_bf, mods, seq, ctx_len):
    n, d = lat.shape
    tq = ATT_TQ
    tiles_per_seq = seq // tq
    sub = tq // ATT_BLOCK
    nblk = n // ATT_BLOCK
    main = lambda w: pl.BlockSpec((tq, w), lambda i: (i, 0))
    prev = pl.BlockSpec((ATT_BLOCK, KV_DIM), lambda i: (jnp.maximum(i * sub - 1, 0), 0))
    nxt = pl.BlockSpec((ATT_BLOCK, KV_DIM), lambda i: (jnp.minimum((i + 1) * sub, nblk - 1), 0))
    ctxs = pl.BlockSpec((ctx_len, KV_DIM), lambda i: (i // tiles_per_seq, 0))
    const = lambda shape: pl.BlockSpec(shape, lambda i: (0,) * len(shape), pipeline_mode=pl.Buffered(1))
    return pl.pallas_call(
        functools.partial(_even_mix_kernel, tiles_per_seq=tiles_per_seq),
        grid=(n // tq,),
        in_specs=[
            pl.BlockSpec(memory_space=pltpu.SMEM),
            main(d), main(qx.shape[1]),
            main(KV_DIM), prev, nxt,
            main(KV_DIM), prev, nxt,
            ctxs, ctxs,
            main(SG_WIDTH), main(SG_WIDTH),
            const((1, SG_WIDTH)), const(wsp_bf.shape), const(bsp_full.shape), const(wout_bf.shape),
            pl.BlockSpec((None, 1, d), lambda i: (i // tiles_per_seq, 0, 2)),
        ],
        out_specs=pl.BlockSpec((tq, d), lambda i: (i, 0)),
        out_shape=jax.ShapeDtypeStruct((n, d), F32),
        scratch_shapes=[
            pltpu.VMEM((tq + 2 * ATT_BLOCK, KV_DIM), BF16),
            pltpu.VMEM((tq + 2 * ATT_BLOCK, KV_DIM), BF16),
            pltpu.VMEM((tq, Q_DIM + SG_WIDTH), BF16),
        ],
        compiler_params=_params(("parallel",)),
        name="even_mix",
    )(sink, lat, qx, k, k, k, v, v, v, kc, vc, u, z, g_sgu, wsp_bf, bsp_full, wout_bf, mods)


def _odd_mix_kernel(x_ref, xp_ref, sh_ref, sc_ref, gate_ref, g_ref, win_ref, cw_ref, wout_ref,
                    o_ref, y_s, bg_s, tail_s, *, tiles_per_seq):
    i = pl.program_id(0)
    n_tiles = pl.num_programs(0) - 1
    tm, d = x_ref.shape
    cur = i % 2
    prv = 1 - cur

    tail_s[...] = y_s[cur, tm - 8:tm, :]

    @pl.when(i < n_tiles)
    def _():
        h = _rms_mod(x_ref[...], g_ref[...], sh_ref[...], sc_ref[...])
        p = jnp.dot(h.astype(BF16), win_ref[...], preferred_element_type=F32)
        bg_s[cur] = p[:, 0:d]
        y_s[cur] = p[:, d:2 * d] * p[:, 2 * d:3 * d]

    @pl.when(i >= 1)
    def _():
        t_prev = i - 1
        first = (t_prev % tiles_per_seq) == 0
        last = (t_prev % tiles_per_seq) == tiles_per_seq - 1
        y = y_s[prv]
        left = jnp.where(first, 0.0, tail_s[7:8, :])
        right = jnp.where(last, 0.0, y_s[cur, 0:1, :])
        ridx = lax.broadcasted_iota(jnp.int32, (tm, d), 0)
        y_dn = jnp.where(ridx == 0, left, pltpu.roll(y, 1, 0))
        y_up = jnp.where(ridx == tm - 1, right, pltpu.roll(y, tm - 1, 0))
        conv = y_dn * cw_ref[0:1, :] + y * cw_ref[1:2, :] + y_up * cw_ref[2:3, :]
        mix = jnp.dot((bg_s[prv] * conv).astype(BF16), wout_ref[...], preferred_element_type=F32)
        o_ref[...] = xp_ref[...] + gate_ref[...] * mix


def _odd_mix(lat, mods, layer, g, win_bf, conv_w8, wout_bf, seq):
    n, d = lat.shape
    tm = ODD_TM
    nt = n // tm
    tiles_per_seq = seq // tm
    cur = lambda i: jnp.minimum(i, nt - 1)
    prv = lambda i: jnp.maximum(i - 1, 0)
    row = lambda t: layer * MOD_ROWS + t // tiles_per_seq
    const = lambda shape: pl.BlockSpec(shape, lambda i: (0,) * len(shape), pipeline_mode=pl.Buffered(1))
    return pl.pallas_call(
        functools.partial(_odd_mix_kernel, tiles_per_seq=tiles_per_seq),
        grid=(nt + 1,),
        in_specs=[
            pl.BlockSpec((tm, d), lambda i: (cur(i), 0)),
            pl.BlockSpec((tm, d), lambda i: (prv(i), 0)),
            pl.BlockSpec((None, 1, d), lambda i: (row(cur(i)), 0, 0)),
            pl.BlockSpec((None, 1, d), lambda i: (row(cur(i)), 0, 1)),
            pl.BlockSpec((None, 1, d), lambda i: (row(prv(i)), 0, 2)),
            const((1, d)), const(win_bf.shape), const(conv_w8.shape), const(wout_bf.shape),
        ],
        out_specs=pl.BlockSpec((tm, d), lambda i: (prv(i), 0)),
        out_shape=jax.ShapeDtypeStruct((n, d), F32),
        scratch_shapes=[
            pltpu.VMEM((2, tm, d), F32),
            pltpu.VMEM((2, tm, d), F32),
            pltpu.VMEM((8, d), F32),
        ],
        compiler_params=_params(("arbitrary",)),
        name="odd_mix",
    )(lat, lat, mods, mods, mods, g, win_bf, conv_w8, wout_bf)


def _router_kernel(x_ref, sh_ref, sc_ref, g_ref, wr_ref, br_ref,
                   h_ref, mi_ref, wc_ref, cnt_ref, carry, upper):
    i = pl.program_id(0)
    tm = x_ref.shape[0]
    epg = EXPERTS_PER_GROUP

    @pl.when(i == 0)
    def _():
        carry[...] = jnp.zeros_like(carry)
        r_i = lax.broadcasted_iota(jnp.int32, (tm, tm), 0)
        c_i = lax.broadcasted_iota(jnp.int32, (tm, tm), 1)
        upper[...] = jnp.where(r_i < c_i, 1.0, 0.0).astype(BF16)

    h = _rms_mod(x_ref[...], g_ref[...], sh_ref[...], sc_ref[...])
    h_ref[...] = _pack_rows(h)
    h_hi = h.astype(BF16)
    h_lo = (h - h_hi.astype(F32)).astype(BF16)
    w = wr_ref[...]
    w1 = w.astype(BF16).astype(F32)
    r1 = w - w1
    w2 = r1.astype(BF16).astype(F32)
    w3 = r1 - w2
    nt = (((1,), (1,)), ((), ()))
    nr = w.shape[0]
    w123 = jnp.concatenate([w1, w2, w3, jnp.zeros((8, w.shape[1]), F32)], axis=0).astype(BF16)
    w12 = jnp.concatenate([w1, w2], axis=0).astype(BF16)
    p_hi = lax.dot_general(w123, h_hi, nt, preferred_element_type=F32)
    p_lo = lax.dot_general(w12, h_lo, nt, preferred_element_type=F32)
    lg = ((p_hi[2 * nr:3 * nr] + p_lo[nr:2 * nr]) + (p_hi[nr:2 * nr] + p_lo[0:nr])) + p_hi[0:nr] + br_ref[...]
    io8 = lax.broadcasted_iota(jnp.int32, (epg, tm), 0)
    gl = lg[0:epg]
    gmax = jnp.max(gl, axis=0, keepdims=True)
    g_idx = jnp.min(jnp.where(gl == gmax, io8, epg), axis=0, keepdims=True)
    g_w = 1.0 / jnp.sum(jnp.exp(gl - gmax), axis=0, keepdims=True)
    e_sel = lg[epg:2 * epg]
    for gi in range(1, N_GROUPS):
        e_sel = jnp.where(g_idx == gi, lg[(gi + 1) * epg:(gi + 2) * epg], e_sel)
    v0 = jnp.max(e_sel, axis=0, keepdims=True)
    i0 = jnp.min(jnp.where(e_sel == v0, io8, epg), axis=0, keepdims=True)
    rest = jnp.where(io8 == i0, -jnp.inf, e_sel)
    v1 = jnp.max(rest, axis=0, keepdims=True)
    i1 = jnp.min(jnp.where(rest == v1, io8, epg), axis=0, keepdims=True)
    t = jnp.exp(v1 - v0)
    w0 = g_w / (1.0 + t)
    w1 = g_w * t / (1.0 + t)
    e0 = g_idx * epg + i0
    e1 = g_idx * epg + i1

    io32 = lax.broadcasted_iota(jnp.int32, (N_EXPERTS, tm), 0)
    hit0 = io32 == e0
    hit1 = io32 == e1
    onehot = jnp.where(hit0 | hit1, 1.0, 0.0)
    cum = jnp.dot(onehot.astype(BF16), upper[...], preferred_element_type=F32) + carry[...]
    rank0 = jnp.sum(jnp.where(hit0, cum, 0.0), axis=0, keepdims=True).astype(jnp.int32)
    rank1 = jnp.sum(jnp.where(hit1, cum, 0.0), axis=0, keepdims=True).astype(jnp.int32)
    carry[...] = carry[...] + jnp.sum(onehot, axis=1, keepdims=True)
    cnt_ref[...] = jnp.broadcast_to(carry[...], cnt_ref.shape)

    mi_ref[...] = jnp.where(io8 == 0, e0, jnp.where(io8 == 1, e1, jnp.where(io8 == 2, rank0,
                            jnp.where(io8 == 3, rank1, 0))))
    io128 = lax.broadcasted_iota(jnp.int32, (LANES, tm), 0)
    wrow = jnp.where(io128 == 0, w0, jnp.where(io128 == 1, w1, 0.0))
    wc_ref[...] = wrow.T


def _router(lat, mods, layer, g, wr_t, br_t, seq, tok0, n):
    d = lat.shape[1]
    tm = ROUTE_TM
    tiles_per_seq = seq // tm
    t0 = tok0 // tm
    row = lambda i: layer * MOD_ROWS + (t0 + i) // tiles_per_seq
    const = lambda shape: pl.BlockSpec(shape, lambda i: (0,) * len(shape), pipeline_mode=pl.Buffered(1))
    return pl.pallas_call(
        _router_kernel,
        grid=(n // tm,),
        in_specs=[
            pl.BlockSpec((tm, d), lambda i: (t0 + i, 0)),
            pl.BlockSpec((None, 1, d), lambda i: (row(i), 0, 3)),
            pl.BlockSpec((None, 1, d), lambda i: (row(i), 0, 4)),
            const((1, d)), const(wr_t.shape), const(br_t.shape),
        ],
        out_specs=[
            pl.BlockSpec((tm, d // 2), lambda i: (i, 0)),
            pl.BlockSpec((8, tm), lambda i: (0, i)),
            pl.BlockSpec((tm, LANES), lambda i: (i, 0)),
            pl.BlockSpec((N_EXPERTS, LANES), lambda i: (0, 0)),
        ],
        out_shape=[
            jax.ShapeDtypeStruct((n, d // 2), jnp.uint32),
            jax.ShapeDtypeStruct((8, n), jnp.int32),
            jax.ShapeDtypeStruct((n, LANES), F32),
            jax.ShapeDtypeStruct((N_EXPERTS, LANES), F32),
        ],
        scratch_shapes=[pltpu.VMEM((N_EXPERTS, 1), F32), pltpu.VMEM((tm, tm), BF16)],
        compiler_params=_params(("arbitrary",)),
        name="router",
    )(lat, mods, mods, g, wr_t, br_t)


def _plan_kernel(cnt_ref, mi_ref, dest_ref, be_ref, nx_ref, nv_ref, nu_ref, ps_ref, *, n_blocks):
    bm = MOE_BM

    def per_expert(e, blk0):
        cnt = cnt_ref[e]
        nb = (cnt + bm - 1) // bm
        ps_ref[e] = blk0 * bm

        def fill(b, c):
            be_ref[b] = e
            nv_ref[b] = jnp.minimum(cnt - (b - blk0) * bm, bm)
            return c

        lax.fori_loop(blk0, blk0 + nb, fill, 0)
        return blk0 + nb

    n_used = lax.fori_loop(0, N_EXPERTS, per_expert, 0)
    nu_ref[0] = n_used
    last_e = be_ref[jnp.maximum(n_used - 1, 0)]

    def fill_tail(b, c):
        be_ref[b] = last_e
        nv_ref[b] = 0
        return c

    lax.fori_loop(n_used, n_blocks, fill_tail, 0)

    nx_ref[n_blocks - 1] = last_e

    def link(k, c):
        b = n_blocks - 2 - k
        nx_ref[b] = jnp.where(be_ref[b + 1] != be_ref[b], be_ref[b + 1], nx_ref[b + 1])
        return c

    lax.fori_loop(0, n_blocks - 1, link, 0)

    e01 = mi_ref[0:2, :]
    dest = mi_ref[2:4, :]
    for e in range(N_EXPERTS):
        dest = dest + jnp.where(e01 == e, ps_ref[e], 0)
    dest_ref[...] = dest


def _plan(counts, meta_i, n_blocks):
    n = meta_i.shape[1]
    return pl.pallas_call(
        functools.partial(_plan_kernel, n_blocks=n_blocks),
        in_specs=[pl.BlockSpec(memory_space=pltpu.SMEM), pl.BlockSpec(memory_space=pltpu.VMEM)],
        out_specs=[pl.BlockSpec(memory_space=pltpu.VMEM)] + [pl.BlockSpec(memory_space=pltpu.SMEM)] * 4,
        out_shape=[
            jax.ShapeDtypeStruct((2, n), jnp.int32),
            jax.ShapeDtypeStruct((n_blocks,), jnp.int32),
            jax.ShapeDtypeStruct((n_blocks,), jnp.int32),
            jax.ShapeDtypeStruct((n_blocks,), jnp.int32),
            jax.ShapeDtypeStruct((1,), jnp.int32),
        ],
        scratch_shapes=[pltpu.SMEM((N_EXPERTS,), jnp.int32)],
        compiler_params=pltpu.CompilerParams(vmem_limit_bytes=VMEM_LIMIT),
        name="plan",
    )(counts, meta_i)


def _sc_mesh():
    return plsc.VectorSubcoreMesh(core_axis_name="c", subcore_axis_name="s",
                                  num_cores=SC_CORES, num_subcores=SC_SUBCORES)


def _sc_worker():
    return lax.axis_index("s") * SC_CORES + lax.axis_index("c")


def _sc_dispatch(h2, dest, n_rows):
    n, d = h2.shape
    c = SC_CHUNK
    per_w = n // SC_WORKERS
    nchunk = per_w // c
    idx = dest.reshape(2, SC_WORKERS, nchunk, c)

    @functools.partial(
        pl.kernel, mesh=_sc_mesh(), out_type=jax.ShapeDtypeStruct((n_rows, d), h2.dtype),
        scratch_types=[pltpu.VMEM((nchunk, c), jnp.int32), pltpu.VMEM((nchunk, c), jnp.int32),
                       pltpu.VMEM((2, c, d), h2.dtype),
                       pltpu.SemaphoreType.DMA((2,)), pltpu.SemaphoreType.DMA((2,))])
    def k(h_hbm, idx_hbm, xb_hbm, idx0_v, idx1_v, rows_v, gsem, ssem):
        wid = _sc_worker()
        base = wid * per_w
        idx_v = (idx0_v, idx1_v)
        for kk in range(2):
            pltpu.sync_copy(idx_hbm.at[kk, wid], idx_v[kk])

        def get(j, slot):
            return pltpu.make_async_copy(h_hbm.at[pl.ds(base + j * c, c)], rows_v.at[slot], gsem.at[slot])

        def put(j, slot, kk):
            return pltpu.make_async_copy(rows_v.at[slot], xb_hbm.at[idx_v[kk].at[j]], ssem.at[slot])

        get(0, 0).start()

        @pl.loop(0, nchunk, step=2)
        def _(j):
            for slot in range(2):
                jj = j + slot
                get(jj, slot).wait()

                @pl.when(jj >= 1)
                def _():
                    for kk in range(2):
                        put(jj - 1, 1 - slot, kk).wait()

                @pl.when(jj + 1 < nchunk)
                def _():
                    get(jj + 1, 1 - slot).start()

                for kk in range(2):
                    put(jj, slot, kk).start()

        for kk in range(2):
            put(nchunk - 1, (nchunk - 1) % 2, kk).wait()

    return k(h2, idx)


def _sc_gather(y, dest):
    d = y.shape[1]
    total = dest.shape[0] * dest.shape[1]
    c = SC_CHUNK
    per_w = total // SC_WORKERS
    nchunk = per_w // c
    idx = dest.reshape(SC_WORKERS, nchunk, c)

    @functools.partial(
        pl.kernel, mesh=_sc_mesh(), out_type=jax.ShapeDtypeStruct((total, d), y.dtype),
        scratch_types=[pltpu.VMEM((nchunk, c), jnp.int32), pltpu.VMEM((2, c, d), y.dtype),
                       pltpu.SemaphoreType.DMA((2,)), pltpu.SemaphoreType.DMA((2,))])
    def k(y_hbm, idx_hbm, out_hbm, idx_v, rows_v, gsem, ssem):
        wid = _sc_worker()
        base = wid * per_w
        pltpu.sync_copy(idx_hbm.at[wid], idx_v)

        def get(j, slot):
            return pltpu.make_async_copy(y_hbm.at[idx_v.at[j]], rows_v.at[slot], gsem.at[slot])

        def put(j, slot):
            return pltpu.make_async_copy(rows_v.at[slot], out_hbm.at[pl.ds(base + j * c, c)], ssem.at[slot])

        get(0, 0).start()

        @pl.loop(0, nchunk, step=2)
        def _(j):
            for slot in range(2):
                jj = j + slot
                get(jj, slot).wait()

                @pl.when(jj >= 1)
                def _():
                    put(jj - 1, 1 - slot).wait()

                @pl.when(jj + 1 < nchunk)
                def _():
                    get(jj + 1, 1 - slot).start()

                put(jj, slot).start()

        put(nchunk - 1, (nchunk - 1) % 2).wait()

    return k(y, idx)


def _expert_kernel(be_ref, nx_ref, nv_ref, nu_ref, x_ref, wg_hbm, wu_hbm, wd_hbm, y_ref,
                   wgu_s, wd_s, stg_g, stg_u, stg_d, run_s, sems, *, layer):
    b = pl.program_id(0)
    hid = stg_g.shape[2]
    e = be_ref[b]
    changed = jnp.logical_or(b == 0, e != be_ref[jnp.maximum(b - 1, 0)])

    def fetch(expert, slot):
        return (pltpu.make_async_copy(wg_hbm.at[layer, expert], stg_g.at[slot], sems.at[slot]),
                pltpu.make_async_copy(wu_hbm.at[layer, expert], stg_u.at[slot], sems.at[slot]),
                pltpu.make_async_copy(wd_hbm.at[layer, expert], stg_d.at[slot], sems.at[slot]))

    @pl.when(b == 0)
    def _():
        run_s[0] = 0
        for cp in fetch(e, 0):
            cp.start()

    @pl.when(changed)
    def _():
        run = jnp.where(b == 0, 0, run_s[0] + 1)
        run_s[0] = run
        slot = run % 2
        for cp in fetch(e, slot):
            cp.wait()

        @pl.when(nx_ref[b] != e)
        def _():
            for cp in fetch(nx_ref[b], 1 - slot):
                cp.start()

        wgu_s[:, 0:hid] = stg_g[slot].astype(BF16)
        wgu_s[:, hid:2 * hid] = stg_u[slot].astype(BF16)
        wd_s[...] = stg_d[slot].astype(BF16)

    bm, dp = x_ref.shape
    nv = nv_ref[b]
    in_use = b < nu_ref[0]

    def run(rows):
        live = lax.broadcasted_iota(jnp.int32, (rows, dp), 0) < nv
        x = _unpack_rows(jnp.where(live, x_ref[0:rows, :], jnp.uint32(0)))
        gu = jnp.dot(x.astype(BF16), wgu_s[...], preferred_element_type=F32)
        gate = gu[:, 0:hid]
        act = gate * (1.0 / (1.0 + jnp.exp(-gate))) * gu[:, hid:2 * hid]
        y_ref[0:rows, :] = _pack_rows(jnp.dot(act.astype(BF16), wd_s[...], preferred_element_type=F32))

    @pl.when(jnp.logical_and(in_use, nv > bm // 2))
    def _():
        run(bm)

    @pl.when(jnp.logical_and(in_use, nv <= bm // 2))
    def _():
        run(bm // 2)
        y_ref[bm // 2:bm, :] = jnp.zeros((bm - bm // 2, dp), y_ref.dtype)

    @pl.when(jnp.logical_not(in_use))
    def _():
        y_ref[...] = jnp.zeros_like(y_ref)


def _experts(block_e, next_e, n_valid, n_used, xb, w_gate, w_up, w_down, layer):
    n_rows, dp = xb.shape
    d, hid = w_gate.shape[2], w_gate.shape[3]
    bm = MOE_BM
    n_blocks = n_rows // bm
    hbm = pl.BlockSpec(memory_space=pl.ANY)
    return pl.pallas_call(
        functools.partial(_expert_kernel, layer=layer),
        grid_spec=pltpu.PrefetchScalarGridSpec(
            num_scalar_prefetch=4,
            grid=(n_blocks,),
            in_specs=[
                pl.BlockSpec((bm, dp), lambda b, be, nx, nv, nu: (jnp.minimum(b, nu[0] - 1), 0)),
                hbm, hbm, hbm,
            ],
            out_specs=pl.BlockSpec((bm, dp), lambda b, be, nx, nv, nu: (b, 0)),
            scratch_shapes=[
                pltpu.VMEM((d, 2 * hid), BF16), pltpu.VMEM((hid, d), BF16),
                pltpu.VMEM((2, d, hid), F32), pltpu.VMEM((2, d, hid), F32), pltpu.VMEM((2, hid, d), F32),
                pltpu.SMEM((1,), jnp.int32), pltpu.SemaphoreType.DMA((2,)),
            ],
        ),
        out_shape=jax.ShapeDtypeStruct((n_rows, dp), jnp.uint32),
        compiler_params=_params(("arbitrary",)),
        name="experts",
    )(block_e, next_e, n_valid, n_used, xb, w_gate, w_up, w_down)


def _combine_kernel(lat_ref, y0_ref, y1_ref, wc_ref, gate_ref, gf_ref, *rest, final):
    o_ref = rest[-1]
    wc = wc_ref[...]
    moe = wc[:, 0:1] * _unpack_rows(y0_ref[...]) + wc[:, 1:2] * _unpack_rows(y1_ref[...])
    out = lat_ref[...] + gate_ref[...] * moe
    if final:
        ms = jnp.mean(out * out, axis=-1, keepdims=True)
        out = out * lax.rsqrt(ms + EPS) * gf_ref[...]
    o_ref[...] = out


def _combine(lat, yg, wcol, mods, layer, g_final, seq, final, tok0, prev_out):
    n, d = lat.shape
    tm = COMBINE_TM
    nt = wcol.shape[0] // tm
    t0 = tok0 // tm
    tiles_per_seq = seq // tm
    row = lambda i: layer * MOD_ROWS + (t0 + i) // tiles_per_seq
    in_specs = [
        pl.BlockSpec((tm, d), lambda i: (t0 + i, 0)),
        pl.BlockSpec((tm, d // 2), lambda i: (i, 0)),
        pl.BlockSpec((tm, d // 2), lambda i: (nt + i, 0)),
        pl.BlockSpec((tm, LANES), lambda i: (i, 0)),
        pl.BlockSpec((None, 1, d), lambda i: (row(i), 0, 5)),
        pl.BlockSpec((1, d), lambda i: (0, 0)),
    ]
    args = [lat, yg, yg, wcol, mods, g_final]
    aliases = {}
    if prev_out is not None:
        in_specs.append(pl.BlockSpec(memory_space=pl.ANY))
        args.append(prev_out)
        aliases = {len(args) - 1: 0}
    return pl.pallas_call(
        functools.partial(_combine_kernel, final=final),
        grid=(nt,),
        in_specs=in_specs,
        out_specs=pl.BlockSpec((tm, d), lambda i: (t0 + i, 0)),
        out_shape=jax.ShapeDtypeStruct((n, d), F32),
        input_output_aliases=aliases,
        compiler_params=_params(("parallel",)),
        name="combine",
    )(*args)


def _moe(lat, mods, layer, g2, wr_t, br_t, w_gate, w_up, w_down, g_final, seq, final):
    n, d = lat.shape
    part = n // MOE_PARTS
    n_blocks = (2 * part) // MOE_BM + N_EXPERTS
    out = None
    for p in range(MOE_PARTS):
        tok0 = p * part
        h2, meta_i, wcol, counts = _router(lat, mods, layer, g2, wr_t, br_t, seq, tok0, part)
        dest, block_e, next_e, n_valid, n_used = _plan(counts[:, 0].astype(jnp.int32), meta_i, n_blocks)
        xb = _sc_dispatch(h2, dest, n_blocks * MOE_BM)
        yb = _experts(block_e, next_e, n_valid, n_used, xb, w_gate, w_up, w_down, layer)
        yg = _sc_gather(yb, dest)
        out = _combine(lat, yg, wcol, mods, layer, g_final, seq, final, tok0, out)
    return out


def _rope_tables(seq):
    quarter = HEAD_DIM // 4
    pos = jnp.arange(seq, dtype=F32)
    row_ids = jnp.floor(pos / GRID_W)
    col_ids = pos - row_ids * GRID_W
    inv = ROPE_BASE ** (-jnp.arange(quarter, dtype=F32) / quarter)
    ang_r = row_ids[:, None] * inv
    ang_c = col_ids[:, None] * inv
    zero = jnp.zeros_like(ang_r)
    cos = jnp.concatenate([jnp.cos(ang_r), jnp.cos(ang_r), jnp.cos(ang_c), jnp.cos(ang_c)], axis=-1)
    sa = jnp.concatenate([-jnp.sin(ang_r), zero, -jnp.sin(ang_c), zero], axis=-1)
    sb = jnp.concatenate([zero, jnp.sin(ang_r), zero, jnp.sin(ang_c)], axis=-1)
    rep = LANES // HEAD_DIM
    return tuple(jnp.tile(t, (1, rep)) for t in (cos, sa, sb))


def _router_weights(w_rg, b_rg, w_re, b_re):
    d = w_rg.shape[0]
    pad = EXPERTS_PER_GROUP - N_GROUPS
    wr_t = jnp.concatenate([w_rg.T, jnp.zeros((pad, d), F32), w_re.T], axis=0)
    br_t = jnp.concatenate([b_rg, jnp.full((pad,), NEG_INF, F32), b_re])[:, None]
    return wr_t, br_t


def kernel(x, c, ctx, c_ctx, w_ada, b_ada, g_norm1, g_norm2, g_final, w_in_even, attn_sink, g_sgu,
           w_spatial, b_spatial, w_out_even, w_in_odd, conv_w, w_out_odd, w_router_group,
           b_router_group, w_router_expert, b_router_expert, w_gate, w_up, w_down):
    b, s, d = x.shape
    n = b * s
    n_ctx = ctx.shape[1]
    depth = w_ada.shape[0]
    assert depth == 2 and b + 1 <= MOD_ROWS

    cond = jnp.concatenate([c, c_ctx[None, :], jnp.zeros((MOD_ROWS - b - 1, d), F32)], axis=0)
    mods = _ada(cond, w_ada, b_ada).reshape(depth * MOD_ROWS, 1, 6 * d)
    gf = g_final[None, :]

    lat = x.reshape(n, d)
    w_in_bf = w_in_even[0].astype(BF16)
    tabs = _rope_tables(s)
    qx, k, v, u, z = _even_in(lat, mods, 0, lambda i: i // (s // EVEN_TM), g_norm1[0][None, :], w_in_bf,
                              tabs, s // EVEN_TM, EVEN_TM)
    ones = jnp.ones((n_ctx, LANES), F32)
    zeros = jnp.zeros((n_ctx, LANES), F32)
    _, kc, vc, _, _ = _even_in(ctx.reshape(b * n_ctx, d), mods, 0, lambda i: b, g_norm1[0][None, :],
                               w_in_bf, (ones, zeros, zeros), 1, n_ctx)
    bsp_full = jnp.repeat(b_spatial[0].T, HEAD_DIM, axis=1)
    half = N_Q_HEADS // 2
    w_att = w_out_even[0][:Q_DIM].reshape(2, half, HEAD_DIM, d).transpose(1, 0, 2, 3).reshape(Q_DIM, d)
    w_out_bf = jnp.concatenate([w_att, w_out_even[0][Q_DIM:]], axis=0).astype(BF16)
    lat = _even_mix(lat, qx, k, v, kc, vc, u, z, attn_sink[0], g_sgu[0][None, :],
                    w_spatial[0].astype(BF16), bsp_full, w_out_bf, mods, s, n_ctx)
    wr_t, br_t = _router_weights(w_router_group[0], b_router_group[0], w_router_expert[0], b_router_expert[0])
    lat = _moe(lat, mods, 0, g_norm2[0][None, :], wr_t, br_t, w_gate, w_up, w_down, gf, s, False)

    conv_w8 = jnp.concatenate([conv_w[0], jnp.zeros((8 - conv_w.shape[1], d), F32)], axis=0)
    lat = _odd_mix(lat, mods, 1, g_norm1[1][None, :], w_in_odd[0].astype(BF16), conv_w8,
                   w_out_odd[0].astype(BF16), s)
    wr_t, br_t = _router_weights(w_router_group[1], b_router_group[1], w_router_expert[1], b_router_expert[1])
    out = _moe(lat, mods, 1, g_norm2[1][None, :], wr_t, br_t, w_gate, w_up, w_down, gf, s, True)
    return out.reshape(b, s, d)
```

```python
import functools

import jax
import jax.numpy as jnp
from jax import lax
from jax.experimental import pallas as pl
from jax.experimental.pallas import tpu as pltpu
from jax.experimental.pallas import tpu_sc as plsc

F32 = jnp.float32
BF16 = jnp.bfloat16
HIGHEST = lax.Precision.HIGHEST

GRID_W = 64
N_Q_HEADS = 8
N_KV_HEADS = 2
HEAD_DIM = 64
ATT_BLOCK = 128
ROPE_BASE = 10000.0
Q_DIM = N_Q_HEADS * HEAD_DIM
KV_DIM = N_KV_HEADS * HEAD_DIM
SG_GROUPS = 8
SG_WIDTH = SG_GROUPS * HEAD_DIM
N_GROUPS = 4
EXPERTS_PER_GROUP = 8
N_EXPERTS = N_GROUPS * EXPERTS_PER_GROUP
EPS = 1e-6
NEG_INF = -1e30

LANES = 128
SC_CORES = 2
SC_SUBCORES = 16
SC_WORKERS = SC_CORES * SC_SUBCORES
SC_CHUNK = 32
MOD_ROWS = 8
VMEM_LIMIT = 56 * 1024 * 1024

ADA_TN = 1536
EVEN_TM = 512
ATT_TQ = 512
ODD_TM = 512
ROUTE_TM = 512
MOE_BM = 512
MOE_PARTS = 2
COMBINE_TM = 512


def _params(sem):
    return pltpu.CompilerParams(dimension_semantics=sem, vmem_limit_bytes=VMEM_LIMIT)


def _rms_mod(x, g, shift, scale):
    ms = jnp.mean(x * x, axis=-1, keepdims=True)
    return (x * lax.rsqrt(ms + EPS) * g) * (1.0 + scale) + shift


def _pack_rows(a):
    w = a.shape[1] // 2
    hi = pltpu.bitcast(a[:, :w].astype(BF16).astype(F32), jnp.uint32)
    lo = pltpu.bitcast(a[:, w:].astype(BF16).astype(F32), jnp.uint32)
    return hi | (lo >> 16)


def _unpack_rows(p):
    hi = pltpu.bitcast(p & jnp.uint32(0xFFFF0000), F32)
    lo = pltpu.bitcast(p << 16, F32)
    return jnp.concatenate([hi, lo], axis=1)


def _ada_kernel(a_ref, w_ref, b_ref, o_ref):
    a = a_ref[...]
    s = a * (1.0 / (1.0 + jnp.exp(-a)))
    o_ref[0] = jnp.dot(s, w_ref[0], preferred_element_type=F32, precision=HIGHEST) + b_ref[0]


def _ada(cond, w_ada, b_ada):
    depth, d, six_d = w_ada.shape
    return pl.pallas_call(
        _ada_kernel,
        grid=(depth, six_d // ADA_TN),
        in_specs=[
            pl.BlockSpec((MOD_ROWS, d), lambda l, j: (0, 0)),
            pl.BlockSpec((1, d, ADA_TN), lambda l, j: (l, 0, j)),
            pl.BlockSpec((1, 1, ADA_TN), lambda l, j: (l, 0, j)),
        ],
        out_specs=pl.BlockSpec((1, MOD_ROWS, ADA_TN), lambda l, j: (l, 0, j)),
        out_shape=jax.ShapeDtypeStruct((depth, MOD_ROWS, six_d), F32),
        compiler_params=_params(("arbitrary", "arbitrary")),
        name="ada",
    )(cond, w_ada, b_ada.reshape(depth, 1, six_d))


def _even_in_kernel(x_ref, sh_ref, sc_ref, g_ref, w_ref, cos_ref, sa_ref, sb_ref,
                    qx_ref, k_ref, v_ref, u_ref, z_ref):
    h = _rms_mod(x_ref[...], g_ref[...], sh_ref[...], sc_ref[...])
    p = jnp.dot(h.astype(BF16), w_ref[...], preferred_element_type=F32)
    cos, sa, sb = cos_ref[...], sa_ref[...], sb_ref[...]

    def rope(t):
        return t * cos + pltpu.roll(t, LANES - 16, 1) * sa + pltpu.roll(t, 16, 1) * sb

    scale = HEAD_DIM ** -0.5
    low = lax.broadcasted_iota(jnp.int32, (x_ref.shape[0], LANES), 1) < HEAD_DIM
    heads_per_kv = N_Q_HEADS // N_KV_HEADS
    for cblk in range(Q_DIM // LANES):
        t = rope(p[:, cblk * LANES:(cblk + 1) * LANES]) * scale
        sw = pltpu.roll(t, HEAD_DIM, 1)
        zero = jnp.zeros_like(t)
        if (2 * cblk) // heads_per_kv == 0:
            first, second = jnp.where(low, t, zero), jnp.where(low, sw, zero)
        else:
            first, second = jnp.where(low, zero, sw), jnp.where(low, zero, t)
        qx_ref[:, (2 * cblk) * LANES:(2 * cblk + 1) * LANES] = first.astype(BF16)
        qx_ref[:, (2 * cblk + 1) * LANES:(2 * cblk + 2) * LANES] = second.astype(BF16)

    k_ref[...] = rope(p[:, Q_DIM:Q_DIM + KV_DIM]).astype(BF16)
    v_ref[...] = p[:, Q_DIM + KV_DIM:Q_DIM + 2 * KV_DIM].astype(BF16)
    u0 = Q_DIM + 2 * KV_DIM
    u_ref[...] = p[:, u0:u0 + SG_WIDTH]
    z_ref[...] = p[:, u0 + SG_WIDTH:u0 + 2 * SG_WIDTH]


def _even_in(x2d, mods, layer, mod_row_fn, g, w_bf, tabs, tab_blocks, tm):
    n, d = x2d.shape
    ein = w_bf.shape[1]
    cos, sa, sb = tabs
    row = lambda i: layer * MOD_ROWS + mod_row_fn(i)
    tab_spec = pl.BlockSpec((tm, LANES), lambda i: (i % tab_blocks, 0))
    qx_dim = N_Q_HEADS * LANES
    return pl.pallas_call(
        _even_in_kernel,
        grid=(n // tm,),
        in_specs=[
            pl.BlockSpec((tm, d), lambda i: (i, 0)),
            pl.BlockSpec((None, 1, d), lambda i: (row(i), 0, 0)),
            pl.BlockSpec((None, 1, d), lambda i: (row(i), 0, 1)),
            pl.BlockSpec((1, d), lambda i: (0, 0)),
            pl.BlockSpec((d, ein), lambda i: (0, 0)),
            tab_spec, tab_spec, tab_spec,
        ],
        out_specs=[
            pl.BlockSpec((tm, qx_dim), lambda i: (i, 0)),
            pl.BlockSpec((tm, KV_DIM), lambda i: (i, 0)),
            pl.BlockSpec((tm, KV_DIM), lambda i: (i, 0)),
            pl.BlockSpec((tm, SG_WIDTH), lambda i: (i, 0)),
            pl.BlockSpec((tm, SG_WIDTH), lambda i: (i, 0)),
        ],
        out_shape=[
            jax.ShapeDtypeStruct((n, qx_dim), BF16),
            jax.ShapeDtypeStruct((n, KV_DIM), BF16),
            jax.ShapeDtypeStruct((n, KV_DIM), BF16),
            jax.ShapeDtypeStruct((n, SG_WIDTH), F32),
            jax.ShapeDtypeStruct((n, SG_WIDTH), F32),
        ],
        compiler_params=_params(("parallel",)),
        name="even_in",
    )(x2d, mods, mods, g, w_bf, cos, sa, sb)


def _gelu(x):
    return 0.5 * x * (1.0 + lax.erf(x * (2.0 ** -0.5)))


def _even_mix_kernel(sink_ref, lat_ref, qx_ref, km_ref, kp_ref, kn_ref, vm_ref, vp_ref, vn_ref,
                     kc_ref, vc_ref, u_ref, z_ref, gsgu_ref, wsp_ref, bsp_ref, wout_ref, gate_ref,
                     o_ref, kband, vband, mixin, *, tiles_per_seq):
    i = pl.program_id(0)
    tq = qx_ref.shape[0]
    blk = ATT_BLOCK
    nsub = tq // blk
    n_ctx = kc_ref.shape[0]
    first = (i % tiles_per_seq) == 0
    last = (i % tiles_per_seq) == tiles_per_seq - 1

    kband[0:blk] = kp_ref[...]
    kband[blk:blk + tq] = km_ref[...]
    kband[blk + tq:] = kn_ref[...]
    vband[0:blk] = vp_ref[...]
    vband[blk:blk + tq] = vm_ref[...]
    vband[blk + tq:] = vn_ref[...]

    rows = N_Q_HEADS * blk
    tok = lax.broadcasted_iota(jnp.int32, (rows, blk), 0) & (blk - 1)
    col = lax.broadcasted_iota(jnp.int32, (rows, blk), 1)
    tri_prev = col >= tok
    tri_next = col <= tok
    head = lax.broadcasted_iota(jnp.int32, (rows, 1), 0) // blk
    sink_col = jnp.zeros((rows, 1), F32)
    for hd in range(N_Q_HEADS):
        sink_col = jnp.where(head == hd, sink_ref[hd], sink_col)
    lane_low = lax.broadcasted_iota(jnp.int32, (blk, LANES), 1) < HEAD_DIM
    ones = jnp.ones((n_ctx + 3 * blk, LANES), BF16)
    nt = (((1,), (1,)), ((), ()))

    def sub_block(j, carry):
        r0 = pl.multiple_of(j * blk, blk)
        ok_prev = jnp.logical_not(jnp.logical_and(first, j == 0))
        ok_next = jnp.logical_not(jnp.logical_and(last, j == nsub - 1))
        qs = jnp.concatenate([qx_ref[pl.ds(r0, blk), hd * LANES:(hd + 1) * LANES]
                              for hd in range(N_Q_HEADS)], axis=0)
        kall = jnp.concatenate([kc_ref[...], kband[pl.ds(r0, 3 * blk), :]], axis=0)
        vall = jnp.concatenate([vc_ref[...], vband[pl.ds(r0, 3 * blk), :]], axis=0)
        s = lax.dot_general(qs, kall, nt, preferred_element_type=F32)
        c0 = n_ctx
        s = jnp.concatenate([
            s[:, :c0],
            jnp.where(jnp.logical_and(tri_prev, ok_prev), s[:, c0:c0 + blk], NEG_INF),
            s[:, c0 + blk:c0 + 2 * blk],
            jnp.where(jnp.logical_and(tri_next, ok_next), s[:, c0 + 2 * blk:], NEG_INF),
        ], axis=1)
        m = jnp.maximum(jnp.max(s, axis=-1, keepdims=True), sink_col)
        p = jnp.exp(s - m).astype(BF16)
        o = jnp.dot(p, jnp.concatenate([vall, ones], axis=1), preferred_element_type=F32)
        att = o[:, :LANES] / (o[:, LANES:] + jnp.exp(sink_col - m))
        half = N_Q_HEADS // 2
        for hd in range(half):
            pair = jnp.where(lane_low, att[hd * blk:(hd + 1) * blk], att[(hd + half) * blk:(hd + half + 1) * blk])
            mixin[pl.ds(r0, blk), hd * LANES:(hd + 1) * LANES] = pair.astype(BF16)

        ug = _gelu(u_ref[pl.ds(r0, blk), :])
        zg = _gelu(z_ref[pl.ds(r0, blk), :])
        mu = jnp.mean(zg, axis=-1, keepdims=True)
        zc = zg - mu
        zn = zc * lax.rsqrt(jnp.mean(zc * zc, axis=-1, keepdims=True) + EPS) * gsgu_ref[...]
        for pair in range(SG_GROUPS // 2):
            zp = zn[:, pair * LANES:(pair + 1) * LANES]
            zero = jnp.zeros_like(zp)
            lo = jnp.where(lane_low, zp, zero).astype(BF16)
            hi = jnp.where(lane_low, zero, zp).astype(BF16)
            sg = (jnp.dot(wsp_ref[2 * pair], lo, preferred_element_type=F32)
                  + jnp.dot(wsp_ref[2 * pair + 1], hi, preferred_element_type=F32)
                  + bsp_ref[:, pair * LANES:(pair + 1) * LANES])
            mixin[pl.ds(r0, blk), Q_DIM + pair * LANES:Q_DIM + (pair + 1) * LANES] = (
                ug[:, pair * LANES:(pair + 1) * LANES] * sg).astype(BF16)
        return carry

    lax.fori_loop(0, nsub, sub_block, 0, unroll=True)
    mix = jnp.dot(mixin[...], wout_ref[...], preferred_element_type=F32)
    o_ref[...] = lat_ref[...] + gate_ref[...] * mix


def _even_mix(lat, qx, k, v, kc, vc, u, z, sink, g_sgu, wsp_bf, bsp_full, wout_bf, mods, seq, ctx_len):
    n, d = lat.shape
    tq = ATT_TQ
    tiles_per_seq = seq // tq
    sub = tq // ATT_BLOCK
    nblk = n // ATT_BLOCK
    main = lambda w: pl.BlockSpec((tq, w), lambda i: (i, 0))
    prev = pl.BlockSpec((ATT_BLOCK, KV_DIM), lambda i: (jnp.maximum(i * sub - 1, 0), 0))
    nxt = pl.BlockSpec((ATT_BLOCK, KV_DIM), lambda i: (jnp.minimum((i + 1) * sub, nblk - 1), 0))
    ctxs = pl.BlockSpec((ctx_len, KV_DIM), lambda i: (i // tiles_per_seq, 0))
    const = lambda shape: pl.BlockSpec(shape, lambda i: (0,) * len(shape), pipeline_mode=pl.Buffered(1))
    return pl.pallas_call(
        functools.partial(_even_mix_kernel, tiles_per_seq=tiles_per_seq),
        grid=(n // tq,),
        in_specs=[
            pl.BlockSpec(memory_space=pltpu.SMEM),
            main(d), main(qx.shape[1]),
            main(KV_DIM), prev, nxt,
            main(KV_DIM), prev, nxt,
            ctxs, ctxs,
            main(SG_WIDTH), main(SG_WIDTH),
            const((1, SG_WIDTH)), const(wsp_bf.shape), const(bsp_full.shape), const(wout_bf.shape),
            pl.BlockSpec((None, 1, d), lambda i: (i // tiles_per_seq, 0, 2)),
        ],
        out_specs=pl.BlockSpec((tq, d), lambda i: (i, 0)),
        out_shape=jax.ShapeDtypeStruct((n, d), F32),
        scratch_shapes=[
            pltpu.VMEM((tq + 2 * ATT_BLOCK, KV_DIM), BF16),
            pltpu.VMEM((tq + 2 * ATT_BLOCK, KV_DIM), BF16),
            pltpu.VMEM((tq, Q_DIM + SG_WIDTH), BF16),
        ],
        compiler_params=_params(("parallel",)),
        name="even_mix",
    )(sink, lat, qx, k, k, k, v, v, v, kc, vc, u, z, g_sgu, wsp_bf, bsp_full, wout_bf, mods)


def _odd_mix_kernel(x_ref, xp_ref, sh_ref, sc_ref, gate_ref, g_ref, win_ref, cw_ref, wout_ref,
                    o_ref, y_s, bg_s, tail_s, *, tiles_per_seq):
    i = pl.program_id(0)
    n_tiles = pl.num_programs(0) - 1
    tm, d = x_ref.shape
    cur = i % 2
    prv = 1 - cur

    tail_s[...] = y_s[cur, tm - 8:tm, :]

    @pl.when(i < n_tiles)
    def _():
        h = _rms_mod(x_ref[...], g_ref[...], sh_ref[...], sc_ref[...])
        p = jnp.dot(h.astype(BF16), win_ref[...], preferred_element_type=F32)
        bg_s[cur] = p[:, 0:d]
        y_s[cur] = p[:, d:2 * d] * p[:, 2 * d:3 * d]

    @pl.when(i >= 1)
    def _():
        t_prev = i - 1
        first = (t_prev % tiles_per_seq) == 0
        last = (t_prev % tiles_per_seq) == tiles_per_seq - 1
        y = y_s[prv]
        left = jnp.where(first, 0.0, tail_s[7:8, :])
        right = jnp.where(last, 0.0, y_s[cur, 0:1, :])
        ridx = lax.broadcasted_iota(jnp.int32, (tm, d), 0)
        y_dn = jnp.where(ridx == 0, left, pltpu.roll(y, 1, 0))
        y_up = jnp.where(ridx == tm - 1, right, pltpu.roll(y, tm - 1, 0))
        conv = y_dn * cw_ref[0:1, :] + y * cw_ref[1:2, :] + y_up * cw_ref[2:3, :]
        mix = jnp.dot((bg_s[prv] * conv).astype(BF16), wout_ref[...], preferred_element_type=F32)
        o_ref[...] = xp_ref[...] + gate_ref[...] * mix


def _odd_mix(lat, mods, layer, g, win_bf, conv_w8, wout_bf, seq):
    n, d = lat.shape
    tm = ODD_TM
    nt = n // tm
    tiles_per_seq = seq // tm
    cur = lambda i: jnp.minimum(i, nt - 1)
    prv = lambda i: jnp.maximum(i - 1, 0)
    row = lambda t: layer * MOD_ROWS + t // tiles_per_seq
    const = lambda shape: pl.BlockSpec(shape, lambda i: (0,) * len(shape), pipeline_mode=pl.Buffered(1))
    return pl.pallas_call(
        functools.partial(_odd_mix_kernel, tiles_per_seq=tiles_per_seq),
        grid=(nt + 1,),
        in_specs=[
            pl.BlockSpec((tm, d), lambda i: (cur(i), 0)),
            pl.BlockSpec((tm, d), lambda i: (prv(i), 0)),
            pl.BlockSpec((None, 1, d), lambda i: (row(cur(i)), 0, 0)),
            pl.BlockSpec((None, 1, d), lambda i: (row(cur(i)), 0, 1)),
            pl.BlockSpec((None, 1, d), lambda i: (row(prv(i)), 0, 2)),
            const((1, d)), const(win_bf.shape), const(conv_w8.shape), const(wout_bf.shape),
        ],
        out_specs=pl.BlockSpec((tm, d), lambda i: (prv(i), 0)),
        out_shape=jax.ShapeDtypeStruct((n, d), F32),
        scratch_shapes=[
            pltpu.VMEM((2, tm, d), F32),
            pltpu.VMEM((2, tm, d), F32),
            pltpu.VMEM((8, d), F32),
        ],
        compiler_params=_params(("arbitrary",)),
        name="odd_mix",
    )(lat, lat, mods, mods, mods, g, win_bf, conv_w8, wout_bf)


def _router_kernel(x_ref, sh_ref, sc_ref, g_ref, wr_ref, br_ref,
                   h_ref, mi_ref, wc_ref, cnt_ref, carry, upper):
    i = pl.program_id(0)
    tm = x_ref.shape[0]
    epg = EXPERTS_PER_GROUP

    @pl.when(i == 0)
    def _():
        carry[...] = jnp.zeros_like(carry)
        r_i = lax.broadcasted_iota(jnp.int32, (tm, tm), 0)
        c_i = lax.broadcasted_iota(jnp.int32, (tm, tm), 1)
        upper[...] = jnp.where(r_i < c_i, 1.0, 0.0).astype(BF16)

    h = _rms_mod(x_ref[...], g_ref[...], sh_ref[...], sc_ref[...])
    h_ref[...] = _pack_rows(h)
    h_hi = h.astype(BF16)
    h_lo = (h - h_hi.astype(F32)).astype(BF16)
    w = wr_ref[...]
    w1 = w.astype(BF16).astype(F32)
    r1 = w - w1
    w2 = r1.astype(BF16).astype(F32)
    w3 = r1 - w2
    nt = (((1,), (1,)), ((), ()))
    nr = w.shape[0]
    w123 = jnp.concatenate([w1, w2, w3, jnp.zeros((8, w.shape[1]), F32)], axis=0).astype(BF16)
    w12 = jnp.concatenate([w1, w2], axis=0).astype(BF16)
    p_hi = lax.dot_general(w123, h_hi, nt, preferred_element_type=F32)
    p_lo = lax.dot_general(w12, h_lo, nt, preferred_element_type=F32)
    lg = ((p_hi[2 * nr:3 * nr] + p_lo[nr:2 * nr]) + (p_hi[nr:2 * nr] + p_lo[0:nr])) + p_hi[0:nr] + br_ref[...]
    io8 = lax.broadcasted_iota(jnp.int32, (epg, tm), 0)
    gl = lg[0:epg]
    gmax = jnp.max(gl, axis=0, keepdims=True)
    g_idx = jnp.min(jnp.where(gl == gmax, io8, epg), axis=0, keepdims=True)
    g_w = 1.0 / jnp.sum(jnp.exp(gl - gmax), axis=0, keepdims=True)
    e_sel = lg[epg:2 * epg]
    for gi in range(1, N_GROUPS):
        e_sel = jnp.where(g_idx == gi, lg[(gi + 1) * epg:(gi + 2) * epg], e_sel)
    v0 = jnp.max(e_sel, axis=0, keepdims=True)
    i0 = jnp.min(jnp.where(e_sel == v0, io8, epg), axis=0, keepdims=True)
    rest = jnp.where(io8 == i0, -jnp.inf, e_sel)
    v1 = jnp.max(rest, axis=0, keepdims=True)
    i1 = jnp.min(jnp.where(rest == v1, io8, epg), axis=0, keepdims=True)
    t = jnp.exp(v1 - v0)
    w0 = g_w / (1.0 + t)
    w1 = g_w * t / (1.0 + t)
    e0 = g_idx * epg + i0
    e1 = g_idx * epg + i1

    io32 = lax.broadcasted_iota(jnp.int32, (N_EXPERTS, tm), 0)
    hit0 = io32 == e0
    hit1 = io32 == e1
    onehot = jnp.where(hit0 | hit1, 1.0, 0.0)
    cum = jnp.dot(onehot.astype(BF16), upper[...], preferred_element_type=F32) + carry[...]
    rank0 = jnp.sum(jnp.where(hit0, cum, 0.0), axis=0, keepdims=True).astype(jnp.int32)
    rank1 = jnp.sum(jnp.where(hit1, cum, 0.0), axis=0, keepdims=True).astype(jnp.int32)
    carry[...] = carry[...] + jnp.sum(onehot, axis=1, keepdims=True)
    cnt_ref[...] = jnp.broadcast_to(carry[...], cnt_ref.shape)

    mi_ref[...] = jnp.where(io8 == 0, e0, jnp.where(io8 == 1, e1, jnp.where(io8 == 2, rank0,
                            jnp.where(io8 == 3, rank1, 0))))
    io128 = lax.broadcasted_iota(jnp.int32, (LANES, tm), 0)
    wrow = jnp.where(io128 == 0, w0, jnp.where(io128 == 1, w1, 0.0))
    wc_ref[...] = wrow.T


def _router(lat, mods, layer, g, wr_t, br_t, seq, tok0, n):
    d = lat.shape[1]
    tm = ROUTE_TM
    tiles_per_seq = seq // tm
    t0 = tok0 // tm
    row = lambda i: layer * MOD_ROWS + (t0 + i) // tiles_per_seq
    const = lambda shape: pl.BlockSpec(shape, lambda i: (0,) * len(shape), pipeline_mode=pl.Buffered(1))
    return pl.pallas_call(
        _router_kernel,
        grid=(n // tm,),
        in_specs=[
            pl.BlockSpec((tm, d), lambda i: (t0 + i, 0)),
            pl.BlockSpec((None, 1, d), lambda i: (row(i), 0, 3)),
            pl.BlockSpec((None, 1, d), lambda i: (row(i), 0, 4)),
            const((1, d)), const(wr_t.shape), const(br_t.shape),
        ],
        out_specs=[
            pl.BlockSpec((tm, d // 2), lambda i: (i, 0)),
            pl.BlockSpec((8, tm), lambda i: (0, i)),
            pl.BlockSpec((tm, LANES), lambda i: (i, 0)),
            pl.BlockSpec((N_EXPERTS, LANES), lambda i: (0, 0)),
        ],
        out_shape=[
            jax.ShapeDtypeStruct((n, d // 2), jnp.uint32),
            jax.ShapeDtypeStruct((8, n), jnp.int32),
            jax.ShapeDtypeStruct((n, LANES), F32),
            jax.ShapeDtypeStruct((N_EXPERTS, LANES), F32),
        ],
        scratch_shapes=[pltpu.VMEM((N_EXPERTS, 1), F32), pltpu.VMEM((tm, tm), BF16)],
        compiler_params=_params(("arbitrary",)),
        name="router",
    )(lat, mods, mods, g, wr_t, br_t)


def _plan_kernel(cnt_ref, mi_ref, dest_ref, be_ref, nx_ref, nv_ref, nu_ref, ps_ref, *, n_blocks):
    bm = MOE_BM

    def per_expert(e, blk0):
        cnt = cnt_ref[e]
        nb = (cnt + bm - 1) // bm
        ps_ref[e] = blk0 * bm

        def fill(b, c):
            be_ref[b] = e
            nv_ref[b] = jnp.minimum(cnt - (b - blk0) * bm, bm)
            return c

        lax.fori_loop(blk0, blk0 + nb, fill, 0)
        return blk0 + nb

    n_used = lax.fori_loop(0, N_EXPERTS, per_expert, 0)
    nu_ref[0] = n_used
    last_e = be_ref[jnp.maximum(n_used - 1, 0)]

    def fill_tail(b, c):
        be_ref[b] = last_e
        nv_ref[b] = 0
        return c

    lax.fori_loop(n_used, n_blocks, fill_tail, 0)

    nx_ref[n_blocks - 1] = last_e

    def link(k, c):
        b = n_blocks - 2 - k
        nx_ref[b] = jnp.where(be_ref[b + 1] != be_ref[b], be_ref[b + 1], nx_ref[b + 1])
        return c

    lax.fori_loop(0, n_blocks - 1, link, 0)

    e01 = mi_ref[0:2, :]
    dest = mi_ref[2:4, :]
    for e in range(N_EXPERTS):
        dest = dest + jnp.where(e01 == e, ps_ref[e], 0)
    dest_ref[...] = dest


def _plan(counts, meta_i, n_blocks):
    n = meta_i.shape[1]
    return pl.pallas_call(
        functools.partial(_plan_kernel, n_blocks=n_blocks),
        in_specs=[pl.BlockSpec(memory_space=pltpu.SMEM), pl.BlockSpec(memory_space=pltpu.VMEM)],
        out_specs=[pl.BlockSpec(memory_space=pltpu.VMEM)] + [pl.BlockSpec(memory_space=pltpu.SMEM)] * 4,
        out_shape=[
            jax.ShapeDtypeStruct((2, n), jnp.int32),
            jax.ShapeDtypeStruct((n_blocks,), jnp.int32),
            jax.ShapeDtypeStruct((n_blocks,), jnp.int32),
            jax.ShapeDtypeStruct((n_blocks,), jnp.int32),
            jax.ShapeDtypeStruct((1,), jnp.int32),
        ],
        scratch_shapes=[pltpu.SMEM((N_EXPERTS,), jnp.int32)],
        compiler_params=pltpu.CompilerParams(vmem_limit_bytes=VMEM_LIMIT),
        name="plan",
    )(counts, meta_i)


def _sc_mesh():
    return plsc.VectorSubcoreMesh(core_axis_name="c", subcore_axis_name="s",
                                  num_cores=SC_CORES, num_subcores=SC_SUBCORES)


def _sc_worker():
    return lax.axis_index("s") * SC_CORES + lax.axis_index("c")


def _sc_dispatch(h2, dest, n_rows):
    n, d = h2.shape
    c = SC_CHUNK
    per_w = n // SC_WORKERS
    nchunk = per_w // c
    idx = dest.reshape(2, SC_WORKERS, nchunk, c)

    @functools.partial(
        pl.kernel, mesh=_sc_mesh(), out_type=jax.ShapeDtypeStruct((n_rows, d), h2.dtype),
        scratch_types=[pltpu.VMEM((nchunk, c), jnp.int32), pltpu.VMEM((nchunk, c), jnp.int32),
                       pltpu.VMEM((2, c, d), h2.dtype),
                       pltpu.SemaphoreType.DMA((2,)), pltpu.SemaphoreType.DMA((2,))])
    def k(h_hbm, idx_hbm, xb_hbm, idx0_v, idx1_v, rows_v, gsem, ssem):
        wid = _sc_worker()
        base = wid * per_w
        idx_v = (idx0_v, idx1_v)
        for kk in range(2):
            pltpu.sync_copy(idx_hbm.at[kk, wid], idx_v[kk])

        def get(j, slot):
            return pltpu.make_async_copy(h_hbm.at[pl.ds(base + j * c, c)], rows_v.at[slot], gsem.at[slot])

        def put(j, slot, kk):
            return pltpu.make_async_copy(rows_v.at[slot], xb_hbm.at[idx_v[kk].at[j]], ssem.at[slot])

        get(0, 0).start()

        @pl.loop(0, nchunk, step=2)
        def _(j):
            for slot in range(2):
                jj = j + slot
                get(jj, slot).wait()

                @pl.when(jj >= 1)
                def _():
                    for kk in range(2):
                        put(jj - 1, 1 - slot, kk).wait()

                @pl.when(jj + 1 < nchunk)
                def _():
                    get(jj + 1, 1 - slot).start()

                for kk in range(2):
                    put(jj, slot, kk).start()

        for kk in range(2):
            put(nchunk - 1, (nchunk - 1) % 2, kk).wait()

    return k(h2, idx)


def _sc_gather(y, dest):
    d = y.shape[1]
    total = dest.shape[0] * dest.shape[1]
    c = SC_CHUNK
    per_w = total // SC_WORKERS
    nchunk = per_w // c
    idx = dest.reshape(SC_WORKERS, nchunk, c)

    @functools.partial(
        pl.kernel, mesh=_sc_mesh(), out_type=jax.ShapeDtypeStruct((total, d), y.dtype),
        scratch_types=[pltpu.VMEM((nchunk, c), jnp.int32), pltpu.VMEM((2, c, d), y.dtype),
                       pltpu.SemaphoreType.DMA((2,)), pltpu.SemaphoreType.DMA((2,))])
    def k(y_hbm, idx_hbm, out_hbm, idx_v, rows_v, gsem, ssem):
        wid = _sc_worker()
        base = wid * per_w
        pltpu.sync_copy(idx_hbm.at[wid], idx_v)

        def get(j, slot):
            return pltpu.make_async_copy(y_hbm.at[idx_v.at[j]], rows_v.at[slot], gsem.at[slot])

        def put(j, slot):
            return pltpu.make_async_copy(rows_v.at[slot], out_hbm.at[pl.ds(base + j * c, c)], ssem.at[slot])

        get(0, 0).start()

        @pl.loop(0, nchunk, step=2)
        def _(j):
            for slot in range(2):
                jj = j + slot
                get(jj, slot).wait()

                @pl.when(jj >= 1)
                def _():
                    put(jj - 1, 1 - slot).wait()

                @pl.when(jj + 1 < nchunk)
                def _():
                    get(jj + 1, 1 - slot).start()

                put(jj, slot).start()

        put(nchunk - 1, (nchunk - 1) % 2).wait()

    return k(y, idx)


def _expert_kernel(be_ref, nx_ref, nv_ref, nu_ref, x_ref, wg_hbm, wu_hbm, wd_hbm, y_ref,
                   wgu_s, wd_s, stg_g, stg_u, stg_d, run_s, sems, *, layer):
    b = pl.program_id(0)
    hid = stg_g.shape[2]
    e = be_ref[b]
    changed = jnp.logical_or(b == 0, e != be_ref[jnp.maximum(b - 1, 0)])

    def fetch(expert, slot):
        return (pltpu.make_async_copy(wg_hbm.at[layer, expert], stg_g.at[slot], sems.at[slot]),
                pltpu.make_async_copy(wu_hbm.at[layer, expert], stg_u.at[slot], sems.at[slot]),
                pltpu.make_async_copy(wd_hbm.at[layer, expert], stg_d.at[slot], sems.at[slot]))

    @pl.when(b == 0)
    def _():
        run_s[0] = 0
        for cp in fetch(e, 0):
            cp.start()

    @pl.when(changed)
    def _():
        run = jnp.where(b == 0, 0, run_s[0] + 1)
        run_s[0] = run
        slot = run % 2
        for cp in fetch(e, slot):
            cp.wait()

        @pl.when(nx_ref[b] != e)
        def _():
            for cp in fetch(nx_ref[b], 1 - slot):
                cp.start()

        wgu_s[:, 0:hid] = stg_g[slot].astype(BF16)
        wgu_s[:, hid:2 * hid] = stg_u[slot].astype(BF16)
        wd_s[...] = stg_d[slot].astype(BF16)

    bm, dp = x_ref.shape
    nv = nv_ref[b]
    in_use = b < nu_ref[0]

    def run(rows):
        live = lax.broadcasted_iota(jnp.int32, (rows, dp), 0) < nv
        x = _unpack_rows(jnp.where(live, x_ref[0:rows, :], jnp.uint32(0)))
        gu = jnp.dot(x.astype(BF16), wgu_s[...], preferred_element_type=F32)
        gate = gu[:, 0:hid]
        act = gate * (1.0 / (1.0 + jnp.exp(-gate))) * gu[:, hid:2 * hid]
        y_ref[0:rows, :] = _pack_rows(jnp.dot(act.astype(BF16), wd_s[...], preferred_element_type=F32))

    @pl.when(jnp.logical_and(in_use, nv > bm // 2))
    def _():
        run(bm)

    @pl.when(jnp.logical_and(in_use, nv <= bm // 2))
    def _():
        run(bm // 2)
        y_ref[bm // 2:bm, :] = jnp.zeros((bm - bm // 2, dp), y_ref.dtype)

    @pl.when(jnp.logical_not(in_use))
    def _():
        y_ref[...] = jnp.zeros_like(y_ref)


def _experts(block_e, next_e, n_valid, n_used, xb, w_gate, w_up, w_down, layer):
    n_rows, dp = xb.shape
    d, hid = w_gate.shape[2], w_gate.shape[3]
    bm = MOE_BM
    n_blocks = n_rows // bm
    hbm = pl.BlockSpec(memory_space=pl.ANY)
    return pl.pallas_call(
        functools.partial(_expert_kernel, layer=layer),
        grid_spec=pltpu.PrefetchScalarGridSpec(
            num_scalar_prefetch=4,
            grid=(n_blocks,),
            in_specs=[
                pl.BlockSpec((bm, dp), lambda b, be, nx, nv, nu: (jnp.minimum(b, nu[0] - 1), 0)),
                hbm, hbm, hbm,
            ],
            out_specs=pl.BlockSpec((bm, dp), lambda b, be, nx, nv, nu: (b, 0)),
            scratch_shapes=[
                pltpu.VMEM((d, 2 * hid), BF16), pltpu.VMEM((hid, d), BF16),
                pltpu.VMEM((2, d, hid), F32), pltpu.VMEM((2, d, hid), F32), pltpu.VMEM((2, hid, d), F32),
                pltpu.SMEM((1,), jnp.int32), pltpu.SemaphoreType.DMA((2,)),
            ],
        ),
        out_shape=jax.ShapeDtypeStruct((n_rows, dp), jnp.uint32),
        compiler_params=_params(("arbitrary",)),
        name="experts",
    )(block_e, next_e, n_valid, n_used, xb, w_gate, w_up, w_down)


def _combine_kernel(lat_ref, y0_ref, y1_ref, wc_ref, gate_ref, gf_ref, *rest, final):
    o_ref = rest[-1]
    wc = wc_ref[...]
    moe = wc[:, 0:1] * _unpack_rows(y0_ref[...]) + wc[:, 1:2] * _unpack_rows(y1_ref[...])
    out = lat_ref[...] + gate_ref[...] * moe
    if final:
        ms = jnp.mean(out * out, axis=-1, keepdims=True)
        out = out * lax.rsqrt(ms + EPS) * gf_ref[...]
    o_ref[...] = out


def _combine(lat, yg, wcol, mods, layer, g_final, seq, final, tok0, prev_out):
    n, d = lat.shape
    tm = COMBINE_TM
    nt = wcol.shape[0] // tm
    t0 = tok0 // tm
    tiles_per_seq = seq // tm
    row = lambda i: layer * MOD_ROWS + (t0 + i) // tiles_per_seq
    in_specs = [
        pl.BlockSpec((tm, d), lambda i: (t0 + i, 0)),
        pl.BlockSpec((tm, d // 2), lambda i: (i, 0)),
        pl.BlockSpec((tm, d // 2), lambda i: (nt + i, 0)),
        pl.BlockSpec((tm, LANES), lambda i: (i, 0)),
        pl.BlockSpec((None, 1, d), lambda i: (row(i), 0, 5)),
        pl.BlockSpec((1, d), lambda i: (0, 0)),
    ]
    args = [lat, yg, yg, wcol, mods, g_final]
    aliases = {}
    if prev_out is not None:
        in_specs.append(pl.BlockSpec(memory_space=pl.ANY))
        args.append(prev_out)
        aliases = {len(args) - 1: 0}
    return pl.pallas_call(
        functools.partial(_combine_kernel, final=final),
        grid=(nt,),
        in_specs=in_specs,
        out_specs=pl.BlockSpec((tm, d), lambda i: (t0 + i, 0)),
        out_shape=jax.ShapeDtypeStruct((n, d), F32),
        input_output_aliases=aliases,
        compiler_params=_params(("parallel",)),
        name="combine",
    )(*args)


def _moe(lat, mods, layer, g2, wr_t, br_t, w_gate, w_up, w_down, g_final, seq, final):
    n, d = lat.shape
    part = n // MOE_PARTS
    n_blocks = (2 * part) // MOE_BM + N_EXPERTS
    out = None
    for p in range(MOE_PARTS):
        tok0 = p * part
        h2, meta_i, wcol, counts = _router(lat, mods, layer, g2, wr_t, br_t, seq, tok0, part)
        dest, block_e, next_e, n_valid, n_used = _plan(counts[:, 0].astype(jnp.int32), meta_i, n_blocks)
        xb = _sc_dispatch(h2, dest, n_blocks * MOE_BM)
        yb = _experts(block_e, next_e, n_valid, n_used, xb, w_gate, w_up, w_down, layer)
        yg = _sc_gather(yb, dest)
        out = _combine(lat, yg, wcol, mods, layer, g_final, seq, final, tok0, out)
    return out


def _rope_tables(seq):
    quarter = HEAD_DIM // 4
    pos = jnp.arange(seq, dtype=F32)
    row_ids = jnp.floor(pos / GRID_W)
    col_ids = pos - row_ids * GRID_W
    inv = ROPE_BASE ** (-jnp.arange(quarter, dtype=F32) / quarter)
    ang_r = row_ids[:, None] * inv
    ang_c = col_ids[:, None] * inv
    zero = jnp.zeros_like(ang_r)
    cos = jnp.concatenate([jnp.cos(ang_r), jnp.cos(ang_r), jnp.cos(ang_c), jnp.cos(ang_c)], axis=-1)
    sa = jnp.concatenate([-jnp.sin(ang_r), zero, -jnp.sin(ang_c), zero], axis=-1)
    sb = jnp.concatenate([zero, jnp.sin(ang_r), zero, jnp.sin(ang_c)], axis=-1)
    rep = LANES // HEAD_DIM
    return tuple(jnp.tile(t, (1, rep)) for t in (cos, sa, sb))


def _router_weights(w_rg, b_rg, w_re, b_re):
    d = w_rg.shape[0]
    pad = EXPERTS_PER_GROUP - N_GROUPS
    wr_t = jnp.concatenate([w_rg.T, jnp.zeros((pad, d), F32), w_re.T], axis=0)
    br_t = jnp.concatenate([b_rg, jnp.full((pad,), NEG_INF, F32), b_re])[:, None]
    return wr_t, br_t


def kernel(x, c, ctx, c_ctx, w_ada, b_ada, g_norm1, g_norm2, g_final, w_in_even, attn_sink, g_sgu,
           w_spatial, b_spatial, w_out_even, w_in_odd, conv_w, w_out_odd, w_router_group,
           b_router_group, w_router_expert, b_router_expert, w_gate, w_up, w_down):
    b, s, d = x.shape
    n = b * s
    n_ctx = ctx.shape[1]
    depth = w_ada.shape[0]
    assert depth == 2 and b + 1 <= MOD_ROWS

    cond = jnp.concatenate([c, c_ctx[None, :], jnp.zeros((MOD_ROWS - b - 1, d), F32)], axis=0)
    mods = _ada(cond, w_ada, b_ada).reshape(depth * MOD_ROWS, 1, 6 * d)
    gf = g_final[None, :]

    lat = x.reshape(n, d)
    w_in_bf = w_in_even[0].astype(BF16)
    tabs = _rope_tables(s)
    qx, k, v, u, z = _even_in(lat, mods, 0, lambda i: i // (s // EVEN_TM), g_norm1[0][None, :], w_in_bf,
                              tabs, s // EVEN_TM, EVEN_TM)
    ones = jnp.ones((n_ctx, LANES), F32)
    zeros = jnp.zeros((n_ctx, LANES), F32)
    _, kc, vc, _, _ = _even_in(ctx.reshape(b * n_ctx, d), mods, 0, lambda i: b, g_norm1[0][None, :],
                               w_in_bf, (ones, zeros, zeros), 1, n_ctx)
    bsp_full = jnp.repeat(b_spatial[0].T, HEAD_DIM, axis=1)
    half = N_Q_HEADS // 2
    w_att = w_out_even[0][:Q_DIM].reshape(2, half, HEAD_DIM, d).transpose(1, 0, 2, 3).reshape(Q_DIM, d)
    w_out_bf = jnp.concatenate([w_att, w_out_even[0][Q_DIM:]], axis=0).astype(BF16)
    lat = _even_mix(lat, qx, k, v, kc, vc, u, z, attn_sink[0], g_sgu[0][None, :],
                    w_spatial[0].astype(BF16), bsp_full, w_out_bf, mods, s, n_ctx)
    wr_t, br_t = _router_weights(w_router_group[0], b_router_group[0], w_router_expert[0], b_router_expert[0])
    lat = _moe(lat, mods, 0, g_norm2[0][None, :], wr_t, br_t, w_gate, w_up, w_down, gf, s, False)

    conv_w8 = jnp.concatenate([conv_w[0], jnp.zeros((8 - conv_w.shape[1], d), F32)], axis=0)
    lat = _odd_mix(lat, mods, 1, g_norm1[1][None, :], w_in_odd[0].astype(BF16), conv_w8,
                   w_out_odd[0].astype(BF16), s)
    wr_t, br_t = _router_weights(w_router_group[1], b_router_group[1], w_router_expert[1], b_router_expert[1])
    out = _moe(lat, mods, 1, g_norm2[1][None, :], wr_t, br_t, w_gate, w_up, w_down, gf, s, True)
    return out.reshape(b, s, d)
```

```python
import functools

import jax
import jax.numpy as jnp
from jax import lax
from jax.experimental import pallas as pl
from jax.experimental.pallas import tpu as pltpu
from jax.experimental.pallas import tpu_sc as plsc

F32 = jnp.float32
BF16 = jnp.bfloat16
HIGHEST = lax.Precision.HIGHEST

GRID_W = 64
N_Q_HEADS = 8
N_KV_HEADS = 2
HEAD_DIM = 64
ATT_BLOCK = 128
ROPE_BASE = 10000.0
Q_DIM = N_Q_HEADS * HEAD_DIM
KV_DIM = N_KV_HEADS * HEAD_DIM
SG_GROUPS = 8
SG_WIDTH = SG_GROUPS * HEAD_DIM
N_GROUPS = 4
EXPERTS_PER_GROUP = 8
N_EXPERTS = N_GROUPS * EXPERTS_PER_GROUP
EPS = 1e-6
NEG_INF = -1e30

LANES = 128
SC_CORES = 2
SC_SUBCORES = 16
SC_WORKERS = SC_CORES * SC_SUBCORES
SC_CHUNK = 32
MOD_ROWS = 8
VMEM_LIMIT = 56 * 1024 * 1024

ADA_TN = 1536
EVEN_TM = 512
ATT_TQ = 512
ODD_TM = 512
ROUTE_TM = 512
MOE_BM = 512
MOE_PARTS = 2
WEIGHT_SLOTS = 3
COMBINE_TM = 512


def _params(sem):
    return pltpu.CompilerParams(dimension_semantics=sem, vmem_limit_bytes=VMEM_LIMIT)


def _rms_mod(x, g, shift, scale):
    ms = jnp.mean(x * x, axis=-1, keepdims=True)
    return (x * lax.rsqrt(ms + EPS) * g) * (1.0 + scale) + shift


def _pack_rows(a):
    w = a.shape[1] // 2
    hi = pltpu.bitcast(a[:, :w].astype(BF16).astype(F32), jnp.uint32)
    lo = pltpu.bitcast(a[:, w:].astype(BF16).astype(F32), jnp.uint32)
    return hi | (lo >> 16)


def _unpack_rows(p):
    hi = pltpu.bitcast(p & jnp.uint32(0xFFFF0000), F32)
    lo = pltpu.bitcast(p << 16, F32)
    return jnp.concatenate([hi, lo], axis=1)


def _ada_kernel(a_ref, w_ref, b_ref, o_ref):
    a = a_ref[...]
    s = a * (1.0 / (1.0 + jnp.exp(-a)))
    o_ref[0] = jnp.dot(s, w_ref[0], preferred_element_type=F32, precision=HIGHEST) + b_ref[0]


def _ada(cond, w_ada, b_ada):
    depth, d, six_d = w_ada.shape
    return pl.pallas_call(
        _ada_kernel,
        grid=(depth, six_d // ADA_TN),
        in_specs=[
            pl.BlockSpec((MOD_ROWS, d), lambda l, j: (0, 0)),
            pl.BlockSpec((1, d, ADA_TN), lambda l, j: (l, 0, j)),
            pl.BlockSpec((1, 1, ADA_TN), lambda l, j: (l, 0, j)),
        ],
        out_specs=pl.BlockSpec((1, MOD_ROWS, ADA_TN), lambda l, j: (l, 0, j)),
        out_shape=jax.ShapeDtypeStruct((depth, MOD_ROWS, six_d), F32),
        compiler_params=_params(("arbitrary", "arbitrary")),
        name="ada",
    )(cond, w_ada, b_ada.reshape(depth, 1, six_d))


def _even_in_kernel(x_ref, sh_ref, sc_ref, g_ref, w_ref, cos_ref, sa_ref, sb_ref,
                    qx_ref, k_ref, v_ref, u_ref, z_ref):
    h = _rms_mod(x_ref[...], g_ref[...], sh_ref[...], sc_ref[...])
    p = jnp.dot(h.astype(BF16), w_ref[...], preferred_element_type=F32)
    cos, sa, sb = cos_ref[...], sa_ref[...], sb_ref[...]

    def rope(t):
        return t * cos + pltpu.roll(t, LANES - 16, 1) * sa + pltpu.roll(t, 16, 1) * sb

    scale = HEAD_DIM ** -0.5
    low = lax.broadcasted_iota(jnp.int32, (x_ref.shape[0], LANES), 1) < HEAD_DIM
    heads_per_kv = N_Q_HEADS // N_KV_HEADS
    for cblk in range(Q_DIM // LANES):
        t = rope(p[:, cblk * LANES:(cblk + 1) * LANES]) * scale
        sw = pltpu.roll(t, HEAD_DIM, 1)
        zero = jnp.zeros_like(t)
        if (2 * cblk) // heads_per_kv == 0:
            first, second = jnp.where(low, t, zero), jnp.where(low, sw, zero)
        else:
            first, second = jnp.where(low, zero, sw), jnp.where(low, zero, t)
        qx_ref[:, (2 * cblk) * LANES:(2 * cblk + 1) * LANES] = first.astype(BF16)
        qx_ref[:, (2 * cblk + 1) * LANES:(2 * cblk + 2) * LANES] = second.astype(BF16)

    k_ref[...] = rope(p[:, Q_DIM:Q_DIM + KV_DIM]).astype(BF16)
    v_ref[...] = p[:, Q_DIM + KV_DIM:Q_DIM + 2 * KV_DIM].astype(BF16)
    u0 = Q_DIM + 2 * KV_DIM
    u_ref[...] = p[:, u0:u0 + SG_WIDTH]
    z_ref[...] = p[:, u0 + SG_WIDTH:u0 + 2 * SG_WIDTH]


def _even_in(x2d, mods, layer, mod_row_fn, g, w_bf, tabs, tab_blocks, tm):
    n, d = x2d.shape
    ein = w_bf.shape[1]
    cos, sa, sb = tabs
    row = lambda i: layer * MOD_ROWS + mod_row_fn(i)
    tab_spec = pl.BlockSpec((tm, LANES), lambda i: (i % tab_blocks, 0))
    qx_dim = N_Q_HEADS * LANES
    return pl.pallas_call(
        _even_in_kernel,
        grid=(n // tm,),
        in_specs=[
            pl.BlockSpec((tm, d), lambda i: (i, 0)),
            pl.BlockSpec((None, 1, d), lambda i: (row(i), 0, 0)),
            pl.BlockSpec((None, 1, d), lambda i: (row(i), 0, 1)),
            pl.BlockSpec((1, d), lambda i: (0, 0)),
            pl.BlockSpec((d, ein), lambda i: (0, 0)),
            tab_spec, tab_spec, tab_spec,
        ],
        out_specs=[
            pl.BlockSpec((tm, qx_dim), lambda i: (i, 0)),
            pl.BlockSpec((tm, KV_DIM), lambda i: (i, 0)),
            pl.BlockSpec((tm, KV_DIM), lambda i: (i, 0)),
            pl.BlockSpec((tm, SG_WIDTH), lambda i: (i, 0)),
            pl.BlockSpec((tm, SG_WIDTH), lambda i: (i, 0)),
        ],
        out_shape=[
            jax.ShapeDtypeStruct((n, qx_dim), BF16),
            jax.ShapeDtypeStruct((n, KV_DIM), BF16),
            jax.ShapeDtypeStruct((n, KV_DIM), BF16),
            jax.ShapeDtypeStruct((n, SG_WIDTH), F32),
            jax.ShapeDtypeStruct((n, SG_WIDTH), F32),
        ],
        compiler_params=_params(("parallel",)),
        name="even_in",
    )(x2d, mods, mods, g, w_bf, cos, sa, sb)


def _gelu(x):
    return 0.5 * x * (1.0 + lax.erf(x * (2.0 ** -0.5)))


def _even_mix_kernel(sink_ref, lat_ref, qx_ref, km_ref, kp_ref, kn_ref, vm_ref, vp_ref, vn_ref,
                     kc_ref, vc_ref, u_ref, z_ref, gsgu_ref, wsp_ref, bsp_ref, wout_ref, gate_ref,
                     o_ref, kband, vband, mixin, *, tiles_per_seq):
    i = pl.program_id(0)
    tq = qx_ref.shape[0]
    blk = ATT_BLOCK
    nsub = tq // blk
    n_ctx = kc_ref.shape[0]
    first = (i % tiles_per_seq) == 0
    last = (i % tiles_per_seq) == tiles_per_seq - 1

    kband[0:blk] = kp_ref[...]
    kband[blk:blk + tq] = km_ref[...]
    kband[blk + tq:] = kn_ref[...]
    vband[0:blk] = vp_ref[...]
    vband[blk:blk + tq] = vm_ref[...]
    vband[blk + tq:] = vn_ref[...]

    rows = N_Q_HEADS * blk
    tok = lax.broadcasted_iota(jnp.int32, (rows, blk), 0) & (blk - 1)
    col = lax.broadcasted_iota(jnp.int32, (rows, blk), 1)
    tri_prev = col >= tok
    tri_next = col <= tok
    head = lax.broadcasted_iota(jnp.int32, (rows, 1), 0) // blk
    sink_col = jnp.zeros((rows, 1), F32)
    for hd in range(N_Q_HEADS):
        sink_col = jnp.where(head == hd, sink_ref[hd], sink_col)
    lane_low = lax.broadcasted_iota(jnp.int32, (blk, LANES), 1) < HEAD_DIM
    ones = jnp.ones((n_ctx + 3 * blk, LANES), BF16)
    nt = (((1,), (1,)), ((), ()))

    def sub_block(j, carry):
        r0 = pl.multiple_of(j * blk, blk)
        ok_prev = jnp.logical_not(jnp.logical_and(first, j == 0))
        ok_next = jnp.logical_not(jnp.logical_and(last, j == nsub - 1))
        qs = jnp.concatenate([qx_ref[pl.ds(r0, blk), hd * LANES:(hd + 1) * LANES]
                              for hd in range(N_Q_HEADS)], axis=0)
        kall = jnp.concatenate([kc_ref[...], kband[pl.ds(r0, 3 * blk), :]], axis=0)
        vall = jnp.concatenate([vc_ref[...], vband[pl.ds(r0, 3 * blk), :]], axis=0)
        s = lax.dot_general(qs, kall, nt, preferred_element_type=F32)
        c0 = n_ctx
        s = jnp.concatenate([
            s[:, :c0],
            jnp.where(jnp.logical_and(tri_prev, ok_prev), s[:, c0:c0 + blk], NEG_INF),
            s[:, c0 + blk:c0 + 2 * blk],
            jnp.where(jnp.logical_and(tri_next, ok_next), s[:, c0 + 2 * blk:], NEG_INF),
        ], axis=1)
        m = jnp.maximum(jnp.max(s, axis=-1, keepdims=True), sink_col)
        p = jnp.exp(s - m).astype(BF16)
        o = jnp.dot(p, jnp.concatenate([vall, ones], axis=1), preferred_element_type=F32)
        att = o[:, :LANES] / (o[:, LANES:] + jnp.exp(sink_col - m))
        half = N_Q_HEADS // 2
        for hd in range(half):
            pair = jnp.where(lane_low, att[hd * blk:(hd + 1) * blk], att[(hd + half) * blk:(hd + half + 1) * blk])
            mixin[pl.ds(r0, blk), hd * LANES:(hd + 1) * LANES] = pair.astype(BF16)

        ug = _gelu(u_ref[pl.ds(r0, blk), :])
        zg = _gelu(z_ref[pl.ds(r0, blk), :])
        mu = jnp.mean(zg, axis=-1, keepdims=True)
        zc = zg - mu
        zn = zc * lax.rsqrt(jnp.mean(zc * zc, axis=-1, keepdims=True) + EPS) * gsgu_ref[...]
        for pair in range(SG_GROUPS // 2):
            zp = zn[:, pair * LANES:(pair + 1) * LANES]
            zero = jnp.zeros_like(zp)
            lo = jnp.where(lane_low, zp, zero).astype(BF16)
            hi = jnp.where(lane_low, zero, zp).astype(BF16)
            sg = (jnp.dot(wsp_ref[2 * pair], lo, preferred_element_type=F32)
                  + jnp.dot(wsp_ref[2 * pair + 1], hi, preferred_element_type=F32)
                  + bsp_ref[:, pair * LANES:(pair + 1) * LANES])
            mixin[pl.ds(r0, blk), Q_DIM + pair * LANES:Q_DIM + (pair + 1) * LANES] = (
                ug[:, pair * LANES:(pair + 1) * LANES] * sg).astype(BF16)
        return carry

    lax.fori_loop(0, nsub, sub_block, 0, unroll=True)
    mix = jnp.dot(mixin[...], wout_ref[...], preferred_element_type=F32)
    o_ref[...] = lat_ref[...] + gate_ref[...] * mix


def _even_mix(lat, qx, k, v, kc, vc, u, z, sink, g_sgu, wsp_bf, bsp_full, wout_bf, mods, seq, ctx_len):
    n, d = lat.shape
    tq = ATT_TQ
    tiles_per_seq = seq // tq
    sub = tq // ATT_BLOCK
    nblk = n // ATT_BLOCK
    main = lambda w: pl.BlockSpec((tq, w), lambda i: (i, 0))
    prev = pl.BlockSpec((ATT_BLOCK, KV_DIM), lambda i: (jnp.maximum(i * sub - 1, 0), 0))
    nxt = pl.BlockSpec((ATT_BLOCK, KV_DIM), lambda i: (jnp.minimum((i + 1) * sub, nblk - 1), 0))
    ctxs = pl.BlockSpec((ctx_len, KV_DIM), lambda i: (i // tiles_per_seq, 0))
    const = lambda shape: pl.BlockSpec(shape, lambda i: (0,) * len(shape), pipeline_mode=pl.Buffered(1))
    return pl.pallas_call(
        functools.partial(_even_mix_kernel, tiles_per_seq=tiles_per_seq),
        grid=(n // tq,),
        in_specs=[
            pl.BlockSpec(memory_space=pltpu.SMEM),
            main(d), main(qx.shape[1]),
            main(KV_DIM), prev, nxt,
            main(KV_DIM), prev, nxt,
            ctxs, ctxs,
            main(SG_WIDTH), main(SG_WIDTH),
            const((1, SG_WIDTH)), const(wsp_bf.shape), const(bsp_full.shape), const(wout_bf.shape),
            pl.BlockSpec((None, 1, d), lambda i: (i // tiles_per_seq, 0, 2)),
        ],
        out_specs=pl.BlockSpec((tq, d), lambda i: (i, 0)),
        out_shape=jax.ShapeDtypeStruct((n, d), F32),
        scratch_shapes=[
            pltpu.VMEM((tq + 2 * ATT_BLOCK, KV_DIM), BF16),
            pltpu.VMEM((tq + 2 * ATT_BLOCK, KV_DIM), BF16),
            pltpu.VMEM((tq, Q_DIM + SG_WIDTH), BF16),
        ],
        compiler_params=_params(("parallel",)),
        name="even_mix",
    )(sink, lat, qx, k, k, k, v, v, v, kc, vc, u, z, g_sgu, wsp_bf, bsp_full, wout_bf, mods)


def _odd_mix_kernel(x_ref, xp_ref, sh_ref, sc_ref, gate_ref, g_ref, win_ref, cw_ref, wout_ref,
                    o_ref, y_s, bg_s, tail_s, *, tiles_per_seq):
    i = pl.program_id(0)
    n_tiles = pl.num_programs(0) - 1
    tm, d = x_ref.shape
    cur = i % 2
    prv = 1 - cur

    tail_s[...] = y_s[cur, tm - 8:tm, :]

    @pl.when(i < n_tiles)
    def _():
        h = _rms_mod(x_ref[...], g_ref[...], sh_ref[...], sc_ref[...])
        p = jnp.dot(h.astype(BF16), win_ref[...], preferred_element_type=F32)
        bg_s[cur] = p[:, 0:d]
        y_s[cur] = p[:, d:2 * d] * p[:, 2 * d:3 * d]

    @pl.when(i >= 1)
    def _():
        t_prev = i - 1
        first = (t_prev % tiles_per_seq) == 0
        last = (t_prev % tiles_per_seq) == tiles_per_seq - 1
        y = y_s[prv]
        left = jnp.where(first, 0.0, tail_s[7:8, :])
        right = jnp.where(last, 0.0, y_s[cur, 0:1, :])
        ridx = lax.broadcasted_iota(jnp.int32, (tm, d), 0)
        y_dn = jnp.where(ridx == 0, left, pltpu.roll(y, 1, 0))
        y_up = jnp.where(ridx == tm - 1, right, pltpu.roll(y, tm - 1, 0))
        conv = y_dn * cw_ref[0:1, :] + y * cw_ref[1:2, :] + y_up * cw_ref[2:3, :]
        mix = jnp.dot((bg_s[prv] * conv).astype(BF16), wout_ref[...], preferred_element_type=F32)
        o_ref[...] = xp_ref[...] + gate_ref[...] * mix


def _odd_mix(lat, mods, layer, g, win_bf, conv_w8, wout_bf, seq):
    n, d = lat.shape
    tm = ODD_TM
    nt = n // tm
    tiles_per_seq = seq // tm
    cur = lambda i: jnp.minimum(i, nt - 1)
    prv = lambda i: jnp.maximum(i - 1, 0)
    row = lambda t: layer * MOD_ROWS + t // tiles_per_seq
    const = lambda shape: pl.BlockSpec(shape, lambda i: (0,) * len(shape), pipeline_mode=pl.Buffered(1))
    return pl.pallas_call(
        functools.partial(_odd_mix_kernel, tiles_per_seq=tiles_per_seq),
        grid=(nt + 1,),
        in_specs=[
            pl.BlockSpec((tm, d), lambda i: (cur(i), 0)),
            pl.BlockSpec((tm, d), lambda i: (prv(i), 0)),
            pl.BlockSpec((None, 1, d), lambda i: (row(cur(i)), 0, 0)),
            pl.BlockSpec((None, 1, d), lambda i: (row(cur(i)), 0, 1)),
            pl.BlockSpec((None, 1, d), lambda i: (row(prv(i)), 0, 2)),
            const((1, d)), const(win_bf.shape), const(conv_w8.shape), const(wout_bf.shape),
        ],
        out_specs=pl.BlockSpec((tm, d), lambda i: (prv(i), 0)),
        out_shape=jax.ShapeDtypeStruct((n, d), F32),
        scratch_shapes=[
            pltpu.VMEM((2, tm, d), F32),
            pltpu.VMEM((2, tm, d), F32),
            pltpu.VMEM((8, d), F32),
        ],
        compiler_params=_params(("arbitrary",)),
        name="odd_mix",
    )(lat, lat, mods, mods, mods, g, win_bf, conv_w8, wout_bf)


def _router_kernel(x_ref, sh_ref, sc_ref, g_ref, wr_ref, br_ref,
                   h_ref, mi_ref, wc_ref, cnt_ref, carry, upper):
    i = pl.program_id(0)
    tm = x_ref.shape[0]
    epg = EXPERTS_PER_GROUP

    @pl.when(i == 0)
    def _():
        carry[...] = jnp.zeros_like(carry)
        r_i = lax.broadcasted_iota(jnp.int32, (tm, tm), 0)
        c_i = lax.broadcasted_iota(jnp.int32, (tm, tm), 1)
        upper[...] = jnp.where(r_i < c_i, 1.0, 0.0).astype(BF16)

    h = _rms_mod(x_ref[...], g_ref[...], sh_ref[...], sc_ref[...])
    h_ref[...] = _pack_rows(h)
    h_hi = h.astype(BF16)
    h_lo = (h - h_hi.astype(F32)).astype(BF16)
    w = wr_ref[...]
    w1 = w.astype(BF16).astype(F32)
    r1 = w - w1
    w2 = r1.astype(BF16).astype(F32)
    w3 = r1 - w2
    nt = (((1,), (1,)), ((), ()))
    nr = w.shape[0]
    w123 = jnp.concatenate([w1, w2, w3, jnp.zeros((8, w.shape[1]), F32)], axis=0).astype(BF16)
    w12 = jnp.concatenate([w1, w2], axis=0).astype(BF16)
    p_hi = lax.dot_general(w123, h_hi, nt, preferred_element_type=F32)
    p_lo = lax.dot_general(w12, h_lo, nt, preferred_element_type=F32)
    lg = ((p_hi[2 * nr:3 * nr] + p_lo[nr:2 * nr]) + (p_hi[nr:2 * nr] + p_lo[0:nr])) + p_hi[0:nr] + br_ref[...]
    io8 = lax.broadcasted_iota(jnp.int32, (epg, tm), 0)
    gl = lg[0:epg]
    gmax = jnp.max(gl, axis=0, keepdims=True)
    g_idx = jnp.min(jnp.where(gl == gmax, io8, epg), axis=0, keepdims=True)
    g_w = 1.0 / jnp.sum(jnp.exp(gl - gmax), axis=0, keepdims=True)
    e_sel = lg[epg:2 * epg]
    for gi in range(1, N_GROUPS):
        e_sel = jnp.where(g_idx == gi, lg[(gi + 1) * epg:(gi + 2) * epg], e_sel)
    v0 = jnp.max(e_sel, axis=0, keepdims=True)
    i0 = jnp.min(jnp.where(e_sel == v0, io8, epg), axis=0, keepdims=True)
    rest = jnp.where(io8 == i0, -jnp.inf, e_sel)
    v1 = jnp.max(rest, axis=0, keepdims=True)
    i1 = jnp.min(jnp.where(rest == v1, io8, epg), axis=0, keepdims=True)
    t = jnp.exp(v1 - v0)
    w0 = g_w / (1.0 + t)
    w1 = g_w * t / (1.0 + t)
    e0 = g_idx * epg + i0
    e1 = g_idx * epg + i1

    io32 = lax.broadcasted_iota(jnp.int32, (N_EXPERTS, tm), 0)
    hit0 = io32 == e0
    hit1 = io32 == e1
    onehot = jnp.where(hit0 | hit1, 1.0, 0.0)
    cum = jnp.dot(onehot.astype(BF16), upper[...], preferred_element_type=F32) + carry[...]
    rank0 = jnp.sum(jnp.where(hit0, cum, 0.0), axis=0, keepdims=True).astype(jnp.int32)
    rank1 = jnp.sum(jnp.where(hit1, cum, 0.0), axis=0, keepdims=True).astype(jnp.int32)
    carry[...] = carry[...] + jnp.sum(onehot, axis=1, keepdims=True)
    cnt_ref[...] = jnp.broadcast_to(carry[...], cnt_ref.shape)

    mi_ref[...] = jnp.where(io8 == 0, e0, jnp.where(io8 == 1, e1, jnp.where(io8 == 2, rank0,
                            jnp.where(io8 == 3, rank1, 0))))
    io128 = lax.broadcasted_iota(jnp.int32, (LANES, tm), 0)
    wrow = jnp.where(io128 == 0, w0, jnp.where(io128 == 1, w1, 0.0))
    wc_ref[...] = wrow.T


def _router(lat, mods, layer, g, wr_t, br_t, seq, tok0, n):
    d = lat.shape[1]
    tm = ROUTE_TM
    tiles_per_seq = seq // tm
    t0 = tok0 // tm
    row = lambda i: layer * MOD_ROWS + (t0 + i) // tiles_per_seq
    const = lambda shape: pl.BlockSpec(shape, lambda i: (0,) * len(shape), pipeline_mode=pl.Buffered(1))
    return pl.pallas_call(
        _router_kernel,
        grid=(n // tm,),
        in_specs=[
            pl.BlockSpec((tm, d), lambda i: (t0 + i, 0)),
            pl.BlockSpec((None, 1, d), lambda i: (row(i), 0, 3)),
            pl.BlockSpec((None, 1, d), lambda i: (row(i), 0, 4)),
            const((1, d)), const(wr_t.shape), const(br_t.shape),
        ],
        out_specs=[
            pl.BlockSpec((tm, d // 2), lambda i: (i, 0)),
            pl.BlockSpec((8, tm), lambda i: (0, i)),
            pl.BlockSpec((tm, LANES), lambda i: (i, 0)),
            pl.BlockSpec((N_EXPERTS, LANES), lambda i: (0, 0)),
        ],
        out_shape=[
            jax.ShapeDtypeStruct((n, d // 2), jnp.uint32),
            jax.ShapeDtypeStruct((8, n), jnp.int32),
            jax.ShapeDtypeStruct((n, LANES), F32),
            jax.ShapeDtypeStruct((N_EXPERTS, LANES), F32),
        ],
        scratch_shapes=[pltpu.VMEM((N_EXPERTS, 1), F32), pltpu.VMEM((tm, tm), BF16)],
        compiler_params=_params(("arbitrary",)),
        name="router",
    )(lat, mods, mods, g, wr_t, br_t)


def _plan_kernel(cnt_ref, mi_ref, dest_ref, be_ref, runs_ref, nv_ref, nu_ref, ps_ref, *, n_blocks):
    bm = MOE_BM

    def per_expert(e, carry):
        blk0, n_runs = carry
        cnt = cnt_ref[e]
        nb = (cnt + bm - 1) // bm
        ps_ref[e] = blk0 * bm

        def fill(b, c):
            be_ref[b] = e
            nv_ref[b] = jnp.minimum(cnt - (b - blk0) * bm, bm)
            return c

        lax.fori_loop(blk0, blk0 + nb, fill, 0)

        @pl.when(nb > 0)
        def _():
            runs_ref[n_runs] = e

        return blk0 + nb, n_runs + jnp.where(nb > 0, 1, 0)

    n_used, n_runs = lax.fori_loop(0, N_EXPERTS, per_expert, (0, 0))
    nu_ref[0] = n_used
    nu_ref[1] = n_runs
    last_e = be_ref[jnp.maximum(n_used - 1, 0)]

    def fill_tail(b, c):
        be_ref[b] = last_e
        nv_ref[b] = 0
        return c

    lax.fori_loop(n_used, n_blocks, fill_tail, 0)

    def fill_runs(k, c):
        runs_ref[k] = last_e
        return c

    lax.fori_loop(n_runs, N_EXPERTS, fill_runs, 0)

    e01 = mi_ref[0:2, :]
    dest = mi_ref[2:4, :]
    for e in range(N_EXPERTS):
        dest = dest + jnp.where(e01 == e, ps_ref[e], 0)
    dest_ref[...] = dest


def _plan(counts, meta_i, n_blocks):
    n = meta_i.shape[1]
    return pl.pallas_call(
        functools.partial(_plan_kernel, n_blocks=n_blocks),
        in_specs=[pl.BlockSpec(memory_space=pltpu.SMEM), pl.BlockSpec(memory_space=pltpu.VMEM)],
        out_specs=[pl.BlockSpec(memory_space=pltpu.VMEM)] + [pl.BlockSpec(memory_space=pltpu.SMEM)] * 4,
        out_shape=[
            jax.ShapeDtypeStruct((2, n), jnp.int32),
            jax.ShapeDtypeStruct((n_blocks,), jnp.int32),
            jax.ShapeDtypeStruct((N_EXPERTS,), jnp.int32),
            jax.ShapeDtypeStruct((n_blocks,), jnp.int32),
            jax.ShapeDtypeStruct((2,), jnp.int32),
        ],
        scratch_shapes=[pltpu.SMEM((N_EXPERTS,), jnp.int32)],
        compiler_params=pltpu.CompilerParams(vmem_limit_bytes=VMEM_LIMIT),
        name="plan",
    )(counts, meta_i)


def _sc_mesh():
    return plsc.VectorSubcoreMesh(core_axis_name="c", subcore_axis_name="s",
                                  num_cores=SC_CORES, num_subcores=SC_SUBCORES)


def _sc_worker():
    return lax.axis_index("s") * SC_CORES + lax.axis_index("c")


def _sc_dispatch(h2, dest, n_rows):
    n, d = h2.shape
    c = SC_CHUNK
    per_w = n // SC_WORKERS
    nchunk = per_w // c
    idx = dest.reshape(2, SC_WORKERS, nchunk, c)

    @functools.partial(
        pl.kernel, mesh=_sc_mesh(), out_type=jax.ShapeDtypeStruct((n_rows, d), h2.dtype),
        scratch_types=[pltpu.VMEM((nchunk, c), jnp.int32), pltpu.VMEM((nchunk, c), jnp.int32),
                       pltpu.VMEM((2, c, d), h2.dtype),
                       pltpu.SemaphoreType.DMA((2,)), pltpu.SemaphoreType.DMA((2,))])
    def k(h_hbm, idx_hbm, xb_hbm, idx0_v, idx1_v, rows_v, gsem, ssem):
        wid = _sc_worker()
        base = wid * per_w
        idx_v = (idx0_v, idx1_v)
        for kk in range(2):
            pltpu.sync_copy(idx_hbm.at[kk, wid], idx_v[kk])

        def get(j, slot):
            return pltpu.make_async_copy(h_hbm.at[pl.ds(base + j * c, c)], rows_v.at[slot], gsem.at[slot])

        def put(j, slot, kk):
            return pltpu.make_async_copy(rows_v.at[slot], xb_hbm.at[idx_v[kk].at[j]], ssem.at[slot])

        get(0, 0).start()

        @pl.loop(0, nchunk, step=2)
        def _(j):
            for slot in range(2):
                jj = j + slot
                get(jj, slot).wait()

                @pl.when(jj >= 1)
                def _():
                    for kk in range(2):
                        put(jj - 1, 1 - slot, kk).wait()

                @pl.when(jj + 1 < nchunk)
                def _():
                    get(jj + 1, 1 - slot).start()

                for kk in range(2):
                    put(jj, slot, kk).start()

        for kk in range(2):
            put(nchunk - 1, (nchunk - 1) % 2, kk).wait()

    return k(h2, idx)


def _sc_gather(y, dest):
    d = y.shape[1]
    total = dest.shape[0] * dest.shape[1]
    c = SC_CHUNK
    per_w = total // SC_WORKERS
    nchunk = per_w // c
    idx = dest.reshape(SC_WORKERS, nchunk, c)

    @functools.partial(
        pl.kernel, mesh=_sc_mesh(), out_type=jax.ShapeDtypeStruct((total, d), y.dtype),
        scratch_types=[pltpu.VMEM((nchunk, c), jnp.int32), pltpu.VMEM((2, c, d), y.dtype),
                       pltpu.SemaphoreType.DMA((2,)), pltpu.SemaphoreType.DMA((2,))])
    def k(y_hbm, idx_hbm, out_hbm, idx_v, rows_v, gsem, ssem):
        wid = _sc_worker()
        base = wid * per_w
        pltpu.sync_copy(idx_hbm.at[wid], idx_v)

        def get(j, slot):
            return pltpu.make_async_copy(y_hbm.at[idx_v.at[j]], rows_v.at[slot], gsem.at[slot])

        def put(j, slot):
            return pltpu.make_async_copy(rows_v.at[slot], out_hbm.at[pl.ds(base + j * c, c)], ssem.at[slot])

        get(0, 0).start()

        @pl.loop(0, nchunk, step=2)
        def _(j):
            for slot in range(2):
                jj = j + slot
                get(jj, slot).wait()

                @pl.when(jj >= 1)
                def _():
                    put(jj - 1, 1 - slot).wait()

                @pl.when(jj + 1 < nchunk)
                def _():
                    get(jj + 1, 1 - slot).start()

                put(jj, slot).start()

        put(nchunk - 1, (nchunk - 1) % 2).wait()

    return k(y, idx)


def _expert_kernel(be_ref, runs_ref, nv_ref, nu_ref, x_ref, wg_hbm, wu_hbm, wd_hbm, y_ref,
                   wgu_s, wd_s, stg_g, stg_u, stg_d, run_s, sems, *, layer):
    b = pl.program_id(0)
    hid = stg_g.shape[2]
    e = be_ref[b]
    n_runs = nu_ref[1]
    changed = jnp.logical_or(b == 0, e != be_ref[jnp.maximum(b - 1, 0)])

    def fetch(run):
        expert = runs_ref[run]
        slot = run % WEIGHT_SLOTS
        return (pltpu.make_async_copy(wg_hbm.at[layer, expert], stg_g.at[slot], sems.at[slot]),
                pltpu.make_async_copy(wu_hbm.at[layer, expert], stg_u.at[slot], sems.at[slot]),
                pltpu.make_async_copy(wd_hbm.at[layer, expert], stg_d.at[slot], sems.at[slot]))

    @pl.when(b == 0)
    def _():
        for r in range(WEIGHT_SLOTS - 1):
            @pl.when(r < n_runs)
            def _():
                for cp in fetch(r):
                    cp.start()

    @pl.when(changed)
    def _():
        run = jnp.where(b == 0, 0, run_s[0] + 1)
        run_s[0] = run
        for cp in fetch(run):
            cp.wait()

        ahead = run + WEIGHT_SLOTS - 1

        @pl.when(ahead < n_runs)
        def _():
            for cp in fetch(ahead):
                cp.start()

        slot = run % WEIGHT_SLOTS
        wgu_s[:, 0:hid] = stg_g[slot].astype(BF16)
        wgu_s[:, hid:2 * hid] = stg_u[slot].astype(BF16)
        wd_s[...] = stg_d[slot].astype(BF16)

    bm, dp = x_ref.shape
    nv = nv_ref[b]
    in_use = b < nu_ref[0]

    def run(rows):
        live = lax.broadcasted_iota(jnp.int32, (rows, dp), 0) < nv
        x = _unpack_rows(jnp.where(live, x_ref[0:rows, :], jnp.uint32(0)))
        gu = jnp.dot(x.astype(BF16), wgu_s[...], preferred_element_type=F32)
        gate = gu[:, 0:hid]
        act = gate * (1.0 / (1.0 + jnp.exp(-gate))) * gu[:, hid:2 * hid]
        y_ref[0:rows, :] = _pack_rows(jnp.dot(act.astype(BF16), wd_s[...], preferred_element_type=F32))

    @pl.when(jnp.logical_and(in_use, nv > bm // 2))
    def _():
        run(bm)

    @pl.when(jnp.logical_and(in_use, nv <= bm // 2))
    def _():
        run(bm // 2)
        y_ref[bm // 2:bm, :] = jnp.zeros((bm - bm // 2, dp), y_ref.dtype)

    @pl.when(jnp.logical_not(in_use))
    def _():
        y_ref[...] = jnp.zeros_like(y_ref)


def _experts(block_e, runs, n_valid, n_used, xb, w_gate, w_up, w_down, layer):
    n_rows, dp = xb.shape
    d, hid = w_gate.shape[2], w_gate.shape[3]
    bm = MOE_BM
    n_blocks = n_rows // bm
    hbm = pl.BlockSpec(memory_space=pl.ANY)
    return pl.pallas_call(
        functools.partial(_expert_kernel, layer=layer),
        grid_spec=pltpu.PrefetchScalarGridSpec(
            num_scalar_prefetch=4,
            grid=(n_blocks,),
            in_specs=[
                pl.BlockSpec((bm, dp), lambda b, be, nx, nv, nu: (jnp.minimum(b, nu[0] - 1), 0)),
                hbm, hbm, hbm,
            ],
            out_specs=pl.BlockSpec((bm, dp), lambda b, be, nx, nv, nu: (b, 0)),
            scratch_shapes=[
                pltpu.VMEM((d, 2 * hid), BF16), pltpu.VMEM((hid, d), BF16),
                pltpu.VMEM((WEIGHT_SLOTS, d, hid), F32), pltpu.VMEM((WEIGHT_SLOTS, d, hid), F32),
                pltpu.VMEM((WEIGHT_SLOTS, hid, d), F32),
                pltpu.SMEM((1,), jnp.int32), pltpu.SemaphoreType.DMA((WEIGHT_SLOTS,)),
            ],
        ),
        out_shape=jax.ShapeDtypeStruct((n_rows, dp), jnp.uint32),
        compiler_params=_params(("arbitrary",)),
        name="experts",
    )(block_e, runs, n_valid, n_used, xb, w_gate, w_up, w_down)


def _combine_kernel(lat_ref, y0_ref, y1_ref, wc_ref, gate_ref, gf_ref, *rest, final):
    o_ref = rest[-1]
    wc = wc_ref[...]
    moe = wc[:, 0:1] * _unpack_rows(y0_ref[...]) + wc[:, 1:2] * _unpack_rows(y1_ref[...])
    out = lat_ref[...] + gate_ref[...] * moe
    if final:
        ms = jnp.mean(out * out, axis=-1, keepdims=True)
        out = out * lax.rsqrt(ms + EPS) * gf_ref[...]
    o_ref[...] = out


def _combine(lat, yg, wcol, mods, layer, g_final, seq, final, tok0, prev_out):
    n, d = lat.shape
    tm = COMBINE_TM
    nt = wcol.shape[0] // tm
    t0 = tok0 // tm
    tiles_per_seq = seq // tm
    row = lambda i: layer * MOD_ROWS + (t0 + i) // tiles_per_seq
    in_specs = [
        pl.BlockSpec((tm, d), lambda i: (t0 + i, 0)),
        pl.BlockSpec((tm, d // 2), lambda i: (i, 0)),
        pl.BlockSpec((tm, d // 2), lambda i: (nt + i, 0)),
        pl.BlockSpec((tm, LANES), lambda i: (i, 0)),
        pl.BlockSpec((None, 1, d), lambda i: (row(i), 0, 5)),
        pl.BlockSpec((1, d), lambda i: (0, 0)),
    ]
    args = [lat, yg, yg, wcol, mods, g_final]
    aliases = {}
    if prev_out is not None:
        in_specs.append(pl.BlockSpec(memory_space=pl.ANY))
        args.append(prev_out)
        aliases = {len(args) - 1: 0}
    return pl.pallas_call(
        functools.partial(_combine_kernel, final=final),
        grid=(nt,),
        in_specs=in_specs,
        out_specs=pl.BlockSpec((tm, d), lambda i: (t0 + i, 0)),
        out_shape=jax.ShapeDtypeStruct((n, d), F32),
        input_output_aliases=aliases,
        compiler_params=_params(("parallel",)),
        name="combine",
    )(*args)


def _moe(lat, mods, layer, g2, wr_t, br_t, w_gate, w_up, w_down, g_final, seq, final):
    n, d = lat.shape
    part = n // MOE_PARTS
    n_blocks = (2 * part) // MOE_BM + N_EXPERTS
    out = None
    for p in range(MOE_PARTS):
        tok0 = p * part
        h2, meta_i, wcol, counts = _router(lat, mods, layer, g2, wr_t, br_t, seq, tok0, part)
        dest, block_e, runs, n_valid, n_used = _plan(counts[:, 0].astype(jnp.int32), meta_i, n_blocks)
        xb = _sc_dispatch(h2, dest, n_blocks * MOE_BM)
        yb = _experts(block_e, runs, n_valid, n_used, xb, w_gate, w_up, w_down, layer)
        yg = _sc_gather(yb, dest)
        out = _combine(lat, yg, wcol, mods, layer, g_final, seq, final, tok0, out)
    return out


def _rope_tables(seq):
    quarter = HEAD_DIM // 4
    pos = jnp.arange(seq, dtype=F32)
    row_ids = jnp.floor(pos / GRID_W)
    col_ids = pos - row_ids * GRID_W
    inv = ROPE_BASE ** (-jnp.arange(quarter, dtype=F32) / quarter)
    ang_r = row_ids[:, None] * inv
    ang_c = col_ids[:, None] * inv
    zero = jnp.zeros_like(ang_r)
    cos = jnp.concatenate([jnp.cos(ang_r), jnp.cos(ang_r), jnp.cos(ang_c), jnp.cos(ang_c)], axis=-1)
    sa = jnp.concatenate([-jnp.sin(ang_r), zero, -jnp.sin(ang_c), zero], axis=-1)
    sb = jnp.concatenate([zero, jnp.sin(ang_r), zero, jnp.sin(ang_c)], axis=-1)
    rep = LANES // HEAD_DIM
    return tuple(jnp.tile(t, (1, rep)) for t in (cos, sa, sb))


def _router_weights(w_rg, b_rg, w_re, b_re):
    d = w_rg.shape[0]
    pad = EXPERTS_PER_GROUP - N_GROUPS
    wr_t = jnp.concatenate([w_rg.T, jnp.zeros((pad, d), F32), w_re.T], axis=0)
    br_t = jnp.concatenate([b_rg, jnp.full((pad,), NEG_INF, F32), b_re])[:, None]
    return wr_t, br_t


def kernel(x, c, ctx, c_ctx, w_ada, b_ada, g_norm1, g_norm2, g_final, w_in_even, attn_sink, g_sgu,
           w_spatial, b_spatial, w_out_even, w_in_odd, conv_w, w_out_odd, w_router_group,
           b_router_group, w_router_expert, b_router_expert, w_gate, w_up, w_down):
    b, s, d = x.shape
    n = b * s
    n_ctx = ctx.shape[1]
    depth = w_ada.shape[0]
    assert depth == 2 and b + 1 <= MOD_ROWS

    cond = jnp.concatenate([c, c_ctx[None, :], jnp.zeros((MOD_ROWS - b - 1, d), F32)], axis=0)
    mods = _ada(cond, w_ada, b_ada).reshape(depth * MOD_ROWS, 1, 6 * d)
    gf = g_final[None, :]

    lat = x.reshape(n, d)
    w_in_bf = w_in_even[0].astype(BF16)
    tabs = _rope_tables(s)
    qx, k, v, u, z = _even_in(lat, mods, 0, lambda i: i // (s // EVEN_TM), g_norm1[0][None, :], w_in_bf,
                              tabs, s // EVEN_TM, EVEN_TM)
    ones = jnp.ones((n_ctx, LANES), F32)
    zeros = jnp.zeros((n_ctx, LANES), F32)
    _, kc, vc, _, _ = _even_in(ctx.reshape(b * n_ctx, d), mods, 0, lambda i: b, g_norm1[0][None, :],
                               w_in_bf, (ones, zeros, zeros), 1, n_ctx)
    bsp_full = jnp.repeat(b_spatial[0].T, HEAD_DIM, axis=1)
    half = N_Q_HEADS // 2
    w_att = w_out_even[0][:Q_DIM].reshape(2, half, HEAD_DIM, d).transpose(1, 0, 2, 3).reshape(Q_DIM, d)
    w_out_bf = jnp.concatenate([w_att, w_out_even[0][Q_DIM:]], axis=0).astype(BF16)
    lat = _even_mix(lat, qx, k, v, kc, vc, u, z, attn_sink[0], g_sgu[0][None, :],
                    w_spatial[0].astype(BF16), bsp_full, w_out_bf, mods, s, n_ctx)
    wr_t, br_t = _router_weights(w_router_group[0], b_router_group[0], w_router_expert[0], b_router_expert[0])
    lat = _moe(lat, mods, 0, g_norm2[0][None, :], wr_t, br_t, w_gate, w_up, w_down, gf, s, False)

    conv_w8 = jnp.concatenate([conv_w[0], jnp.zeros((8 - conv_w.shape[1], d), F32)], axis=0)
    lat = _odd_mix(lat, mods, 1, g_norm1[1][None, :], w_in_odd[0].astype(BF16), conv_w8,
                   w_out_odd[0].astype(BF16), s)
    wr_t, br_t = _router_weights(w_router_group[1], b_router_group[1], w_router_expert[1], b_router_expert[1])
    out = _moe(lat, mods, 1, g_norm2[1][None, :], wr_t, br_t, w_gate, w_up, w_down, gf, s, True)
    return out.reshape(b, s, d)
```

```python
import functools

import jax
import jax.numpy as jnp
from jax import lax
from jax.experimental import pallas as pl
from jax.experimental.pallas import tpu as pltpu
from jax.experimental.pallas import tpu_sc as plsc

F32 = jnp.float32
BF16 = jnp.bfloat16
HIGHEST = lax.Precision.HIGHEST

GRID_W = 64
N_Q_HEADS = 8
N_KV_HEADS = 2
HEAD_DIM = 64
ATT_BLOCK = 128
ROPE_BASE = 10000.0
Q_DIM = N_Q_HEADS * HEAD_DIM
KV_DIM = N_KV_HEADS * HEAD_DIM
SG_GROUPS = 8
SG_WIDTH = SG_GROUPS * HEAD_DIM
N_GROUPS = 4
EXPERTS_PER_GROUP = 8
N_EXPERTS = N_GROUPS * EXPERTS_PER_GROUP
EPS = 1e-6
NEG_INF = -1e30

LANES = 128
SC_CORES = 2
SC_SUBCORES = 16
SC_WORKERS = SC_CORES * SC_SUBCORES
SC_CHUNK = 32
MOD_ROWS = 8
VMEM_LIMIT = 56 * 1024 * 1024

ADA_TN = 1536
EVEN_TM = 512
ATT_TQ = 512
ODD_TM = 512
ROUTE_TM = 512
MOE_BM = 512
MOE_PARTS = 2
WEIGHT_SLOTS = 3
COMBINE_TM = 512


def _params(sem):
    return pltpu.CompilerParams(dimension_semantics=sem, vmem_limit_bytes=VMEM_LIMIT)


def _rms_mod(x, g, shift, scale):
    ms = jnp.mean(x * x, axis=-1, keepdims=True)
    return (x * lax.rsqrt(ms + EPS) * g) * (1.0 + scale) + shift


def _pack_rows(a):
    w = a.shape[1] // 2
    hi = pltpu.bitcast(a[:, :w].astype(BF16).astype(F32), jnp.uint32)
    lo = pltpu.bitcast(a[:, w:].astype(BF16).astype(F32), jnp.uint32)
    return hi | (lo >> 16)


def _unpack_rows(p):
    hi = pltpu.bitcast(p & jnp.uint32(0xFFFF0000), F32)
    lo = pltpu.bitcast(p << 16, F32)
    return jnp.concatenate([hi, lo], axis=1)


def _ada_kernel(a_ref, w_ref, b_ref, o_ref):
    a = a_ref[...]
    s = a * (1.0 / (1.0 + jnp.exp(-a)))
    o_ref[0] = jnp.dot(s, w_ref[0], preferred_element_type=F32, precision=HIGHEST) + b_ref[0]


def _ada(cond, w_ada, b_ada):
    depth, d, six_d = w_ada.shape
    return pl.pallas_call(
        _ada_kernel,
        grid=(depth, six_d // ADA_TN),
        in_specs=[
            pl.BlockSpec((MOD_ROWS, d), lambda l, j: (0, 0)),
            pl.BlockSpec((1, d, ADA_TN), lambda l, j: (l, 0, j)),
            pl.BlockSpec((1, 1, ADA_TN), lambda l, j: (l, 0, j)),
        ],
        out_specs=pl.BlockSpec((1, MOD_ROWS, ADA_TN), lambda l, j: (l, 0, j)),
        out_shape=jax.ShapeDtypeStruct((depth, MOD_ROWS, six_d), F32),
        compiler_params=_params(("arbitrary", "arbitrary")),
        name="ada",
    )(cond, w_ada, b_ada.reshape(depth, 1, six_d))


def _even_in_kernel(x_ref, sh_ref, sc_ref, g_ref, w_ref, cos_ref, sa_ref, sb_ref,
                    qx_ref, k_ref, v_ref, u_ref, z_ref):
    h = _rms_mod(x_ref[...], g_ref[...], sh_ref[...], sc_ref[...])
    p = jnp.dot(h.astype(BF16), w_ref[...], preferred_element_type=F32)
    cos, sa, sb = cos_ref[...], sa_ref[...], sb_ref[...]

    def rope(t):
        return t * cos + pltpu.roll(t, LANES - 16, 1) * sa + pltpu.roll(t, 16, 1) * sb

    scale = HEAD_DIM ** -0.5
    low = lax.broadcasted_iota(jnp.int32, (x_ref.shape[0], LANES), 1) < HEAD_DIM
    heads_per_kv = N_Q_HEADS // N_KV_HEADS
    for cblk in range(Q_DIM // LANES):
        t = rope(p[:, cblk * LANES:(cblk + 1) * LANES]) * scale
        sw = pltpu.roll(t, HEAD_DIM, 1)
        zero = jnp.zeros_like(t)
        if (2 * cblk) // heads_per_kv == 0:
            first, second = jnp.where(low, t, zero), jnp.where(low, sw, zero)
        else:
            first, second = jnp.where(low, zero, sw), jnp.where(low, zero, t)
        qx_ref[:, (2 * cblk) * LANES:(2 * cblk + 1) * LANES] = first.astype(BF16)
        qx_ref[:, (2 * cblk + 1) * LANES:(2 * cblk + 2) * LANES] = second.astype(BF16)

    k_ref[...] = rope(p[:, Q_DIM:Q_DIM + KV_DIM]).astype(BF16)
    v_ref[...] = p[:, Q_DIM + KV_DIM:Q_DIM + 2 * KV_DIM].astype(BF16)
    u0 = Q_DIM + 2 * KV_DIM
    u_ref[...] = p[:, u0:u0 + SG_WIDTH]
    z_ref[...] = p[:, u0 + SG_WIDTH:u0 + 2 * SG_WIDTH]


def _even_in(x2d, mods, layer, mod_row_fn, g, w_bf, tabs, tab_blocks, tm):
    n, d = x2d.shape
    ein = w_bf.shape[1]
    cos, sa, sb = tabs
    row = lambda i: layer * MOD_ROWS + mod_row_fn(i)
    tab_spec = pl.BlockSpec((tm, LANES), lambda i: (i % tab_blocks, 0))
    qx_dim = N_Q_HEADS * LANES
    return pl.pallas_call(
        _even_in_kernel,
        grid=(n // tm,),
        in_specs=[
            pl.BlockSpec((tm, d), lambda i: (i, 0)),
            pl.BlockSpec((None, 1, d), lambda i: (row(i), 0, 0)),
            pl.BlockSpec((None, 1, d), lambda i: (row(i), 0, 1)),
            pl.BlockSpec((1, d), lambda i: (0, 0)),
            pl.BlockSpec((d, ein), lambda i: (0, 0)),
            tab_spec, tab_spec, tab_spec,
        ],
        out_specs=[
            pl.BlockSpec((tm, qx_dim), lambda i: (i, 0)),
            pl.BlockSpec((tm, KV_DIM), lambda i: (i, 0)),
            pl.BlockSpec((tm, KV_DIM), lambda i: (i, 0)),
            pl.BlockSpec((tm, SG_WIDTH), lambda i: (i, 0)),
            pl.BlockSpec((tm, SG_WIDTH), lambda i: (i, 0)),
        ],
        out_shape=[
            jax.ShapeDtypeStruct((n, qx_dim), BF16),
            jax.ShapeDtypeStruct((n, KV_DIM), BF16),
            jax.ShapeDtypeStruct((n, KV_DIM), BF16),
            jax.ShapeDtypeStruct((n, SG_WIDTH), F32),
            jax.ShapeDtypeStruct((n, SG_WIDTH), F32),
        ],
        compiler_params=_params(("parallel",)),
        name="even_in",
    )(x2d, mods, mods, g, w_bf, cos, sa, sb)


def _gelu(x):
    return 0.5 * x * (1.0 + lax.erf(x * (2.0 ** -0.5)))


def _even_mix_kernel(sink_ref, lat_ref, qx_ref, km_ref, kp_ref, kn_ref, vm_ref, vp_ref, vn_ref,
                     kc_ref, vc_ref, u_ref, z_ref, gsgu_ref, wsp_ref, bsp_ref, wout_ref, gate_ref,
                     o_ref, kband, vband, mixin, *, tiles_per_seq):
    i = pl.program_id(0)
    tq = qx_ref.shape[0]
    blk = ATT_BLOCK
    nsub = tq // blk
    n_ctx = kc_ref.shape[0]
    first = (i % tiles_per_seq) == 0
    last = (i % tiles_per_seq) == tiles_per_seq - 1

    kband[0:blk] = kp_ref[...]
    kband[blk:blk + tq] = km_ref[...]
    kband[blk + tq:] = kn_ref[...]
    vband[0:blk] = vp_ref[...]
    vband[blk:blk + tq] = vm_ref[...]
    vband[blk + tq:] = vn_ref[...]

    rows = N_Q_HEADS * blk
    tok = lax.broadcasted_iota(jnp.int32, (rows, blk), 0) & (blk - 1)
    col = lax.broadcasted_iota(jnp.int32, (rows, blk), 1)
    tri_prev = col >= tok
    tri_next = col <= tok
    head = lax.broadcasted_iota(jnp.int32, (rows, 1), 0) // blk
    sink_col = jnp.zeros((rows, 1), F32)
    for hd in range(N_Q_HEADS):
        sink_col = jnp.where(head == hd, sink_ref[hd], sink_col)
    lane_low = lax.broadcasted_iota(jnp.int32, (blk, LANES), 1) < HEAD_DIM
    ones = jnp.ones((n_ctx + 3 * blk, LANES), BF16)
    nt = (((1,), (1,)), ((), ()))

    def sub_block(j, carry):
        r0 = pl.multiple_of(j * blk, blk)
        ok_prev = jnp.logical_not(jnp.logical_and(first, j == 0))
        ok_next = jnp.logical_not(jnp.logical_and(last, j == nsub - 1))
        qs = jnp.concatenate([qx_ref[pl.ds(r0, blk), hd * LANES:(hd + 1) * LANES]
                              for hd in range(N_Q_HEADS)], axis=0)
        kall = jnp.concatenate([kc_ref[...], kband[pl.ds(r0, 3 * blk), :]], axis=0)
        vall = jnp.concatenate([vc_ref[...], vband[pl.ds(r0, 3 * blk), :]], axis=0)
        s = lax.dot_general(qs, kall, nt, preferred_element_type=F32)
        c0 = n_ctx
        s = jnp.concatenate([
            s[:, :c0],
            jnp.where(jnp.logical_and(tri_prev, ok_prev), s[:, c0:c0 + blk], NEG_INF),
            s[:, c0 + blk:c0 + 2 * blk],
            jnp.where(jnp.logical_and(tri_next, ok_next), s[:, c0 + 2 * blk:], NEG_INF),
        ], axis=1)
        m = jnp.maximum(jnp.max(s, axis=-1, keepdims=True), sink_col)
        p = jnp.exp(s - m).astype(BF16)
        o = jnp.dot(p, jnp.concatenate([vall, ones], axis=1), preferred_element_type=F32)
        att = o[:, :LANES] / (o[:, LANES:] + jnp.exp(sink_col - m))
        half = N_Q_HEADS // 2
        for hd in range(half):
            pair = jnp.where(lane_low, att[hd * blk:(hd + 1) * blk], att[(hd + half) * blk:(hd + half + 1) * blk])
            mixin[pl.ds(r0, blk), hd * LANES:(hd + 1) * LANES] = pair.astype(BF16)

        ug = _gelu(u_ref[pl.ds(r0, blk), :])
        zg = _gelu(z_ref[pl.ds(r0, blk), :])
        mu = jnp.mean(zg, axis=-1, keepdims=True)
        zc = zg - mu
        zn = zc * lax.rsqrt(jnp.mean(zc * zc, axis=-1, keepdims=True) + EPS) * gsgu_ref[...]
        for pair in range(SG_GROUPS // 2):
            zp = zn[:, pair * LANES:(pair + 1) * LANES]
            zero = jnp.zeros_like(zp)
            lo = jnp.where(lane_low, zp, zero).astype(BF16)
            hi = jnp.where(lane_low, zero, zp).astype(BF16)
            sg = (jnp.dot(wsp_ref[2 * pair], lo, preferred_element_type=F32)
                  + jnp.dot(wsp_ref[2 * pair + 1], hi, preferred_element_type=F32)
                  + bsp_ref[:, pair * LANES:(pair + 1) * LANES])
            mixin[pl.ds(r0, blk), Q_DIM + pair * LANES:Q_DIM + (pair + 1) * LANES] = (
                ug[:, pair * LANES:(pair + 1) * LANES] * sg).astype(BF16)
        return carry

    lax.fori_loop(0, nsub, sub_block, 0, unroll=True)
    mix = jnp.dot(mixin[...], wout_ref[...], preferred_element_type=F32)
    o_ref[...] = lat_ref[...] + gate_ref[...] * mix


def _even_mix(lat, qx, k, v, kc, vc, u, z, sink, g_sgu, wsp_bf, bsp_full, wout_bf, mods, seq, ctx_len):
    n, d = lat.shape
    tq = ATT_TQ
    tiles_per_seq = seq // tq
    sub = tq // ATT_BLOCK
    nblk = n // ATT_BLOCK
    main = lambda w: pl.BlockSpec((tq, w), lambda i: (i, 0))
    prev = pl.BlockSpec((ATT_BLOCK, KV_DIM), lambda i: (jnp.maximum(i * sub - 1, 0), 0))
    nxt = pl.BlockSpec((ATT_BLOCK, KV_DIM), lambda i: (jnp.minimum((i + 1) * sub, nblk - 1), 0))
    ctxs = pl.BlockSpec((ctx_len, KV_DIM), lambda i: (i // tiles_per_seq, 0))
    const = lambda shape: pl.BlockSpec(shape, lambda i: (0,) * len(shape), pipeline_mode=pl.Buffered(1))
    return pl.pallas_call(
        functools.partial(_even_mix_kernel, tiles_per_seq=tiles_per_seq),
        grid=(n // tq,),
        in_specs=[
            pl.BlockSpec(memory_space=pltpu.SMEM),
            main(d), main(qx.shape[1]),
            main(KV_DIM), prev, nxt,
            main(KV_DIM), prev, nxt,
            ctxs, ctxs,
            main(SG_WIDTH), main(SG_WIDTH),
            const((1, SG_WIDTH)), const(wsp_bf.shape), const(bsp_full.shape), const(wout_bf.shape),
            pl.BlockSpec((None, 1, d), lambda i: (i // tiles_per_seq, 0, 2)),
        ],
        out_specs=pl.BlockSpec((tq, d), lambda i: (i, 0)),
        out_shape=jax.ShapeDtypeStruct((n, d), F32),
        scratch_shapes=[
            pltpu.VMEM((tq + 2 * ATT_BLOCK, KV_DIM), BF16),
            pltpu.VMEM((tq + 2 * ATT_BLOCK, KV_DIM), BF16),
            pltpu.VMEM((tq, Q_DIM + SG_WIDTH), BF16),
        ],
        compiler_params=_params(("parallel",)),
        name="even_mix",
    )(sink, lat, qx, k, k, k, v, v, v, kc, vc, u, z, g_sgu, wsp_bf, bsp_full, wout_bf, mods)


def _odd_mix_kernel(x_ref, xp_ref, sh_ref, sc_ref, gate_ref, g_ref, win_ref, cw_ref, wout_ref,
                    o_ref, y_s, bg_s, tail_s, *, tiles_per_seq):
    i = pl.program_id(0)
    n_tiles = pl.num_programs(0) - 1
    tm, d = x_ref.shape
    cur = i % 2
    prv = 1 - cur

    tail_s[...] = y_s[cur, tm - 8:tm, :]

    @pl.when(i < n_tiles)
    def _():
        h = _rms_mod(x_ref[...], g_ref[...], sh_ref[...], sc_ref[...])
        p = jnp.dot(h.astype(BF16), win_ref[...], preferred_element_type=F32)
        bg_s[cur] = p[:, 0:d]
        y_s[cur] = p[:, d:2 * d] * p[:, 2 * d:3 * d]

    @pl.when(i >= 1)
    def _():
        t_prev = i - 1
        first = (t_prev % tiles_per_seq) == 0
        last = (t_prev % tiles_per_seq) == tiles_per_seq - 1
        y = y_s[prv]
        left = jnp.where(first, 0.0, tail_s[7:8, :])
        right = jnp.where(last, 0.0, y_s[cur, 0:1, :])
        ridx = lax.broadcasted_iota(jnp.int32, (tm, d), 0)
        y_dn = jnp.where(ridx == 0, left, pltpu.roll(y, 1, 0))
        y_up = jnp.where(ridx == tm - 1, right, pltpu.roll(y, tm - 1, 0))
        conv = y_dn * cw_ref[0:1, :] + y * cw_ref[1:2, :] + y_up * cw_ref[2:3, :]
        mix = jnp.dot((bg_s[prv] * conv).astype(BF16), wout_ref[...], preferred_element_type=F32)
        o_ref[...] = xp_ref[...] + gate_ref[...] * mix


def _odd_mix(lat, mods, layer, g, win_bf, conv_w8, wout_bf, seq):
    n, d = lat.shape
    tm = ODD_TM
    nt = n // tm
    tiles_per_seq = seq // tm
    cur = lambda i: jnp.minimum(i, nt - 1)
    prv = lambda i: jnp.maximum(i - 1, 0)
    row = lambda t: layer * MOD_ROWS + t // tiles_per_seq
    const = lambda shape: pl.BlockSpec(shape, lambda i: (0,) * len(shape), pipeline_mode=pl.Buffered(1))
    return pl.pallas_call(
        functools.partial(_odd_mix_kernel, tiles_per_seq=tiles_per_seq),
        grid=(nt + 1,),
        in_specs=[
            pl.BlockSpec((tm, d), lambda i: (cur(i), 0)),
            pl.BlockSpec((tm, d), lambda i: (prv(i), 0)),
            pl.BlockSpec((None, 1, d), lambda i: (row(cur(i)), 0, 0)),
            pl.BlockSpec((None, 1, d), lambda i: (row(cur(i)), 0, 1)),
            pl.BlockSpec((None, 1, d), lambda i: (row(prv(i)), 0, 2)),
            const((1, d)), const(win_bf.shape), const(conv_w8.shape), const(wout_bf.shape),
        ],
        out_specs=pl.BlockSpec((tm, d), lambda i: (prv(i), 0)),
        out_shape=jax.ShapeDtypeStruct((n, d), F32),
        scratch_shapes=[
            pltpu.VMEM((2, tm, d), F32),
            pltpu.VMEM((2, tm, d), F32),
            pltpu.VMEM((8, d), F32),
        ],
        compiler_params=_params(("arbitrary",)),
        name="odd_mix",
    )(lat, lat, mods, mods, mods, g, win_bf, conv_w8, wout_bf)


def _router_kernel(x_ref, sh_ref, sc_ref, g_ref, wr_ref, br_ref,
                   h_ref, mi_ref, wc_ref, cnt_ref, carry, upper):
    i = pl.program_id(0)
    tm = x_ref.shape[0]
    epg = EXPERTS_PER_GROUP

    @pl.when(i == 0)
    def _():
        carry[...] = jnp.zeros_like(carry)
        r_i = lax.broadcasted_iota(jnp.int32, (tm, tm), 0)
        c_i = lax.broadcasted_iota(jnp.int32, (tm, tm), 1)
        upper[...] = jnp.where(r_i < c_i, 1.0, 0.0).astype(BF16)

    h = _rms_mod(x_ref[...], g_ref[...], sh_ref[...], sc_ref[...])
    h_ref[...] = _pack_rows(h)
    h_hi = h.astype(BF16)
    h_lo = (h - h_hi.astype(F32)).astype(BF16)
    w = wr_ref[...]
    w1 = w.astype(BF16).astype(F32)
    r1 = w - w1
    w2 = r1.astype(BF16).astype(F32)
    w3 = r1 - w2
    nt = (((1,), (1,)), ((), ()))
    nr = w.shape[0]
    w123 = jnp.concatenate([w1, w2, w3, jnp.zeros((8, w.shape[1]), F32)], axis=0).astype(BF16)
    w12 = jnp.concatenate([w1, w2], axis=0).astype(BF16)
    p_hi = lax.dot_general(w123, h_hi, nt, preferred_element_type=F32)
    p_lo = lax.dot_general(w12, h_lo, nt, preferred_element_type=F32)
    lg = ((p_hi[2 * nr:3 * nr] + p_lo[nr:2 * nr]) + (p_hi[nr:2 * nr] + p_lo[0:nr])) + p_hi[0:nr] + br_ref[...]
    io8 = lax.broadcasted_iota(jnp.int32, (epg, tm), 0)
    gl = lg[0:epg]
    gmax = jnp.max(gl, axis=0, keepdims=True)
    g_idx = jnp.min(jnp.where(gl == gmax, io8, epg), axis=0, keepdims=True)
    g_w = 1.0 / jnp.sum(jnp.exp(gl - gmax), axis=0, keepdims=True)
    e_sel = lg[epg:2 * epg]
    for gi in range(1, N_GROUPS):
        e_sel = jnp.where(g_idx == gi, lg[(gi + 1) * epg:(gi + 2) * epg], e_sel)
    v0 = jnp.max(e_sel, axis=0, keepdims=True)
    i0 = jnp.min(jnp.where(e_sel == v0, io8, epg), axis=0, keepdims=True)
    rest = jnp.where(io8 == i0, -jnp.inf, e_sel)
    v1 = jnp.max(rest, axis=0, keepdims=True)
    i1 = jnp.min(jnp.where(rest == v1, io8, epg), axis=0, keepdims=True)
    t = jnp.exp(v1 - v0)
    w0 = g_w / (1.0 + t)
    w1 = g_w * t / (1.0 + t)
    e0 = g_idx * epg + i0
    e1 = g_idx * epg + i1

    io32 = lax.broadcasted_iota(jnp.int32, (N_EXPERTS, tm), 0)
    hit0 = io32 == e0
    hit1 = io32 == e1
    onehot = jnp.where(hit0 | hit1, 1.0, 0.0)
    cum = jnp.dot(onehot.astype(BF16), upper[...], preferred_element_type=F32) + carry[...]
    rank0 = jnp.sum(jnp.where(hit0, cum, 0.0), axis=0, keepdims=True).astype(jnp.int32)
    rank1 = jnp.sum(jnp.where(hit1, cum, 0.0), axis=0, keepdims=True).astype(jnp.int32)
    carry[...] = carry[...] + jnp.sum(onehot, axis=1, keepdims=True)
    cnt_ref[...] = jnp.broadcast_to(carry[...], cnt_ref.shape)

    mi_ref[...] = jnp.where(io8 == 0, e0, jnp.where(io8 == 1, e1, jnp.where(io8 == 2, rank0,
                            jnp.where(io8 == 3, rank1, 0))))
    io128 = lax.broadcasted_iota(jnp.int32, (LANES, tm), 0)
    wrow = jnp.where(io128 == 0, w0, jnp.where(io128 == 1, w1, 0.0))
    wc_ref[...] = wrow.T


def _router(lat, mods, layer, g, wr_t, br_t, seq, tok0, n):
    d = lat.shape[1]
    tm = ROUTE_TM
    tiles_per_seq = seq // tm
    t0 = tok0 // tm
    row = lambda i: layer * MOD_ROWS + (t0 + i) // tiles_per_seq
    const = lambda shape: pl.BlockSpec(shape, lambda i: (0,) * len(shape), pipeline_mode=pl.Buffered(1))
    return pl.pallas_call(
        _router_kernel,
        grid=(n // tm,),
        in_specs=[
            pl.BlockSpec((tm, d), lambda i: (t0 + i, 0)),
            pl.BlockSpec((None, 1, d), lambda i: (row(i), 0, 3)),
            pl.BlockSpec((None, 1, d), lambda i: (row(i), 0, 4)),
            const((1, d)), const(wr_t.shape), const(br_t.shape),
        ],
        out_specs=[
            pl.BlockSpec((tm, d // 2), lambda i: (i, 0)),
            pl.BlockSpec((8, tm), lambda i: (0, i)),
            pl.BlockSpec((tm, LANES), lambda i: (i, 0)),
            pl.BlockSpec((N_EXPERTS, LANES), lambda i: (0, 0)),
        ],
        out_shape=[
            jax.ShapeDtypeStruct((n, d // 2), jnp.uint32),
            jax.ShapeDtypeStruct((8, n), jnp.int32),
            jax.ShapeDtypeStruct((n, LANES), F32),
            jax.ShapeDtypeStruct((N_EXPERTS, LANES), F32),
        ],
        scratch_shapes=[pltpu.VMEM((N_EXPERTS, 1), F32), pltpu.VMEM((tm, tm), BF16)],
        compiler_params=_params(("arbitrary",)),
        name="router",
    )(lat, mods, mods, g, wr_t, br_t)


def _plan_kernel(cnt_ref, mi_ref, dest_ref, be_ref, runs_ref, nv_ref, nu_ref, ps_ref, *, n_blocks):
    bm = MOE_BM

    def per_expert(e, carry):
        blk0, n_runs = carry
        cnt = cnt_ref[e]
        nb = (cnt + bm - 1) // bm
        ps_ref[e] = blk0 * bm

        def fill(b, c):
            be_ref[b] = e
            nv_ref[b] = jnp.minimum(cnt - (b - blk0) * bm, bm)
            return c

        lax.fori_loop(blk0, blk0 + nb, fill, 0)

        @pl.when(nb > 0)
        def _():
            runs_ref[n_runs] = e

        return blk0 + nb, n_runs + jnp.where(nb > 0, 1, 0)

    n_used, n_runs = lax.fori_loop(0, N_EXPERTS, per_expert, (0, 0))
    nu_ref[0] = n_used
    nu_ref[1] = n_runs
    last_e = be_ref[jnp.maximum(n_used - 1, 0)]

    def fill_tail(b, c):
        be_ref[b] = last_e
        nv_ref[b] = 0
        return c

    lax.fori_loop(n_used, n_blocks, fill_tail, 0)

    def fill_runs(k, c):
        runs_ref[k] = last_e
        return c

    lax.fori_loop(n_runs, N_EXPERTS, fill_runs, 0)

    e01 = mi_ref[0:2, :]
    dest = mi_ref[2:4, :]
    for e in range(N_EXPERTS):
        dest = dest + jnp.where(e01 == e, ps_ref[e], 0)
    dest_ref[...] = dest


def _plan(counts, meta_i, n_blocks):
    n = meta_i.shape[1]
    return pl.pallas_call(
        functools.partial(_plan_kernel, n_blocks=n_blocks),
        in_specs=[pl.BlockSpec(memory_space=pltpu.SMEM), pl.BlockSpec(memory_space=pltpu.VMEM)],
        out_specs=[pl.BlockSpec(memory_space=pltpu.VMEM)] + [pl.BlockSpec(memory_space=pltpu.SMEM)] * 4,
        out_shape=[
            jax.ShapeDtypeStruct((2, n), jnp.int32),
            jax.ShapeDtypeStruct((n_blocks,), jnp.int32),
            jax.ShapeDtypeStruct((N_EXPERTS,), jnp.int32),
            jax.ShapeDtypeStruct((n_blocks,), jnp.int32),
            jax.ShapeDtypeStruct((2,), jnp.int32),
        ],
        scratch_shapes=[pltpu.SMEM((N_EXPERTS,), jnp.int32)],
        compiler_params=pltpu.CompilerParams(vmem_limit_bytes=VMEM_LIMIT),
        name="plan",
    )(counts, meta_i)


def _sc_mesh():
    return plsc.VectorSubcoreMesh(core_axis_name="c", subcore_axis_name="s",
                                  num_cores=SC_CORES, num_subcores=SC_SUBCORES)


def _sc_worker():
    return lax.axis_index("s") * SC_CORES + lax.axis_index("c")


def _sc_dispatch(h2, dest, n_rows):
    n, d = h2.shape
    c = SC_CHUNK
    per_w = n // SC_WORKERS
    nchunk = per_w // c
    idx = dest.reshape(2, SC_WORKERS, nchunk, c)

    @functools.partial(
        pl.kernel, mesh=_sc_mesh(), out_type=jax.ShapeDtypeStruct((n_rows, d), h2.dtype),
        scratch_types=[pltpu.VMEM((nchunk, c), jnp.int32), pltpu.VMEM((nchunk, c), jnp.int32),
                       pltpu.VMEM((2, c, d), h2.dtype),
                       pltpu.SemaphoreType.DMA((2,)), pltpu.SemaphoreType.DMA((2,))])
    def k(h_hbm, idx_hbm, xb_hbm, idx0_v, idx1_v, rows_v, gsem, ssem):
        wid = _sc_worker()
        base = wid * per_w
        idx_v = (idx0_v, idx1_v)
        for kk in range(2):
            pltpu.sync_copy(idx_hbm.at[kk, wid], idx_v[kk])

        def get(j, slot):
            return pltpu.make_async_copy(h_hbm.at[pl.ds(base + j * c, c)], rows_v.at[slot], gsem.at[slot])

        def put(j, slot, kk):
            return pltpu.make_async_copy(rows_v.at[slot], xb_hbm.at[idx_v[kk].at[j]], ssem.at[slot])

        get(0, 0).start()

        @pl.loop(0, nchunk, step=2)
        def _(j):
            for slot in range(2):
                jj = j + slot
                get(jj, slot).wait()

                @pl.when(jj >= 1)
                def _():
                    for kk in range(2):
                        put(jj - 1, 1 - slot, kk).wait()

                @pl.when(jj + 1 < nchunk)
                def _():
                    get(jj + 1, 1 - slot).start()

                for kk in range(2):
                    put(jj, slot, kk).start()

        for kk in range(2):
            put(nchunk - 1, (nchunk - 1) % 2, kk).wait()

    return k(h2, idx)


def _sc_gather(y, dest):
    d = y.shape[1]
    total = dest.shape[0] * dest.shape[1]
    c = SC_CHUNK
    per_w = total // SC_WORKERS
    nchunk = per_w // c
    idx = dest.reshape(SC_WORKERS, nchunk, c)

    @functools.partial(
        pl.kernel, mesh=_sc_mesh(), out_type=jax.ShapeDtypeStruct((total, d), y.dtype),
        scratch_types=[pltpu.VMEM((nchunk, c), jnp.int32), pltpu.VMEM((2, c, d), y.dtype),
                       pltpu.SemaphoreType.DMA((2,)), pltpu.SemaphoreType.DMA((2,))])
    def k(y_hbm, idx_hbm, out_hbm, idx_v, rows_v, gsem, ssem):
        wid = _sc_worker()
        base = wid * per_w
        pltpu.sync_copy(idx_hbm.at[wid], idx_v)

        def get(j, slot):
            return pltpu.make_async_copy(y_hbm.at[idx_v.at[j]], rows_v.at[slot], gsem.at[slot])

        def put(j, slot):
            return pltpu.make_async_copy(rows_v.at[slot], out_hbm.at[pl.ds(base + j * c, c)], ssem.at[slot])

        get(0, 0).start()

        @pl.loop(0, nchunk, step=2)
        def _(j):
            for slot in range(2):
                jj = j + slot
                get(jj, slot).wait()

                @pl.when(jj >= 1)
                def _():
                    put(jj - 1, 1 - slot).wait()

                @pl.when(jj + 1 < nchunk)
                def _():
                    get(jj + 1, 1 - slot).start()

                put(jj, slot).start()

        put(nchunk - 1, (nchunk - 1) % 2).wait()

    return k(y, idx)


def _expert_kernel(be_ref, runs_ref, nv_ref, nu_ref, x_ref, wgu_hbm, wd_hbm, y_ref,
                   wgu_s, wd_s, run_s, sems, *, layer):
    b = pl.program_id(0)
    hid = wd_s.shape[1]
    e = be_ref[b]
    n_runs = nu_ref[1]
    changed = jnp.logical_or(b == 0, e != be_ref[jnp.maximum(b - 1, 0)])

    def fetch(run):
        expert = runs_ref[run]
        slot = run % WEIGHT_SLOTS
        return (pltpu.make_async_copy(wgu_hbm.at[layer, expert], wgu_s.at[slot], sems.at[slot]),
                pltpu.make_async_copy(wd_hbm.at[layer, expert], wd_s.at[slot], sems.at[slot]))

    @pl.when(b == 0)
    def _():
        for r in range(WEIGHT_SLOTS - 1):
            @pl.when(r < n_runs)
            def _():
                for cp in fetch(r):
                    cp.start()

    @pl.when(changed)
    def _():
        run = jnp.where(b == 0, 0, run_s[0] + 1)
        run_s[0] = run
        for cp in fetch(run):
            cp.wait()

        ahead = run + WEIGHT_SLOTS - 1

        @pl.when(ahead < n_runs)
        def _():
            for cp in fetch(ahead):
                cp.start()

    bm, dp = x_ref.shape
    nv = nv_ref[b]
    in_use = b < nu_ref[0]
    slot = run_s[0] % WEIGHT_SLOTS

    def run(rows):
        live = lax.broadcasted_iota(jnp.int32, (rows, dp), 0) < nv
        x = _unpack_rows(jnp.where(live, x_ref[0:rows, :], jnp.uint32(0)))
        gu = jnp.dot(x.astype(BF16), wgu_s[slot], preferred_element_type=F32)
        gate = gu[:, 0:hid]
        act = gate * (1.0 / (1.0 + jnp.exp(-gate))) * gu[:, hid:2 * hid]
        y_ref[0:rows, :] = _pack_rows(jnp.dot(act.astype(BF16), wd_s[slot], preferred_element_type=F32))

    @pl.when(jnp.logical_and(in_use, nv > bm // 2))
    def _():
        run(bm)

    @pl.when(jnp.logical_and(in_use, nv <= bm // 2))
    def _():
        run(bm // 2)
        y_ref[bm // 2:bm, :] = jnp.zeros((bm - bm // 2, dp), y_ref.dtype)

    @pl.when(jnp.logical_not(in_use))
    def _():
        y_ref[...] = jnp.zeros_like(y_ref)


def _experts(block_e, runs, n_valid, n_used, xb, w_gu, w_down, layer):
    n_rows, dp = xb.shape
    hid, d = w_down.shape[2], w_down.shape[3]
    bm = MOE_BM
    n_blocks = n_rows // bm
    hbm = pl.BlockSpec(memory_space=pl.ANY)
    return pl.pallas_call(
        functools.partial(_expert_kernel, layer=layer),
        grid_spec=pltpu.PrefetchScalarGridSpec(
            num_scalar_prefetch=4,
            grid=(n_blocks,),
            in_specs=[
                pl.BlockSpec((bm, dp), lambda b, be, nx, nv, nu: (jnp.minimum(b, nu[0] - 1), 0)),
                hbm, hbm,
            ],
            out_specs=pl.BlockSpec((bm, dp), lambda b, be, nx, nv, nu: (b, 0)),
            scratch_shapes=[
                pltpu.VMEM((WEIGHT_SLOTS, d, 2 * hid), BF16), pltpu.VMEM((WEIGHT_SLOTS, hid, d), BF16),
                pltpu.SMEM((1,), jnp.int32), pltpu.SemaphoreType.DMA((WEIGHT_SLOTS,)),
            ],
        ),
        out_shape=jax.ShapeDtypeStruct((n_rows, dp), jnp.uint32),
        compiler_params=_params(("arbitrary",)),
        name="experts",
    )(block_e, runs, n_valid, n_used, xb, w_gu, w_down)


def _combine_kernel(lat_ref, y0_ref, y1_ref, wc_ref, gate_ref, gf_ref, *rest, final):
    o_ref = rest[-1]
    wc = wc_ref[...]
    moe = wc[:, 0:1] * _unpack_rows(y0_ref[...]) + wc[:, 1:2] * _unpack_rows(y1_ref[...])
    out = lat_ref[...] + gate_ref[...] * moe
    if final:
        ms = jnp.mean(out * out, axis=-1, keepdims=True)
        out = out * lax.rsqrt(ms + EPS) * gf_ref[...]
    o_ref[...] = out


def _combine(lat, yg, wcol, mods, layer, g_final, seq, final, tok0, prev_out):
    n, d = lat.shape
    tm = COMBINE_TM
    nt = wcol.shape[0] // tm
    t0 = tok0 // tm
    tiles_per_seq = seq // tm
    row = lambda i: layer * MOD_ROWS + (t0 + i) // tiles_per_seq
    in_specs = [
        pl.BlockSpec((tm, d), lambda i: (t0 + i, 0)),
        pl.BlockSpec((tm, d // 2), lambda i: (i, 0)),
        pl.BlockSpec((tm, d // 2), lambda i: (nt + i, 0)),
        pl.BlockSpec((tm, LANES), lambda i: (i, 0)),
        pl.BlockSpec((None, 1, d), lambda i: (row(i), 0, 5)),
        pl.BlockSpec((1, d), lambda i: (0, 0)),
    ]
    args = [lat, yg, yg, wcol, mods, g_final]
    aliases = {}
    if prev_out is not None:
        in_specs.append(pl.BlockSpec(memory_space=pl.ANY))
        args.append(prev_out)
        aliases = {len(args) - 1: 0}
    return pl.pallas_call(
        functools.partial(_combine_kernel, final=final),
        grid=(nt,),
        in_specs=in_specs,
        out_specs=pl.BlockSpec((tm, d), lambda i: (t0 + i, 0)),
        out_shape=jax.ShapeDtypeStruct((n, d), F32),
        input_output_aliases=aliases,
        compiler_params=_params(("parallel",)),
        name="combine",
    )(*args)


def _moe(lat, mods, layer, g2, wr_t, br_t, w_gu, w_down, g_final, seq, final):
    n, d = lat.shape
    part = n // MOE_PARTS
    n_blocks = (2 * part) // MOE_BM + N_EXPERTS
    out = None
    for p in range(MOE_PARTS):
        tok0 = p * part
        h2, meta_i, wcol, counts = _router(lat, mods, layer, g2, wr_t, br_t, seq, tok0, part)
        dest, block_e, runs, n_valid, n_used = _plan(counts[:, 0].astype(jnp.int32), meta_i, n_blocks)
        xb = _sc_dispatch(h2, dest, n_blocks * MOE_BM)
        yb = _experts(block_e, runs, n_valid, n_used, xb, w_gu, w_down, layer)
        yg = _sc_gather(yb, dest)
        out = _combine(lat, yg, wcol, mods, layer, g_final, seq, final, tok0, out)
    return out


def _rope_tables(seq):
    quarter = HEAD_DIM // 4
    pos = jnp.arange(seq, dtype=F32)
    row_ids = jnp.floor(pos / GRID_W)
    col_ids = pos - row_ids * GRID_W
    inv = ROPE_BASE ** (-jnp.arange(quarter, dtype=F32) / quarter)
    ang_r = row_ids[:, None] * inv
    ang_c = col_ids[:, None] * inv
    zero = jnp.zeros_like(ang_r)
    cos = jnp.concatenate([jnp.cos(ang_r), jnp.cos(ang_r), jnp.cos(ang_c), jnp.cos(ang_c)], axis=-1)
    sa = jnp.concatenate([-jnp.sin(ang_r), zero, -jnp.sin(ang_c), zero], axis=-1)
    sb = jnp.concatenate([zero, jnp.sin(ang_r), zero, jnp.sin(ang_c)], axis=-1)
    rep = LANES // HEAD_DIM
    return tuple(jnp.tile(t, (1, rep)) for t in (cos, sa, sb))


def _router_weights(w_rg, b_rg, w_re, b_re):
    d = w_rg.shape[0]
    pad = EXPERTS_PER_GROUP - N_GROUPS
    wr_t = jnp.concatenate([w_rg.T, jnp.zeros((pad, d), F32), w_re.T], axis=0)
    br_t = jnp.concatenate([b_rg, jnp.full((pad,), NEG_INF, F32), b_re])[:, None]
    return wr_t, br_t


def kernel(x, c, ctx, c_ctx, w_ada, b_ada, g_norm1, g_norm2, g_final, w_in_even, attn_sink, g_sgu,
           w_spatial, b_spatial, w_out_even, w_in_odd, conv_w, w_out_odd, w_router_group,
           b_router_group, w_router_expert, b_router_expert, w_gate, w_up, w_down):
    b, s, d = x.shape
    n = b * s
    n_ctx = ctx.shape[1]
    depth = w_ada.shape[0]
    assert depth == 2 and b + 1 <= MOD_ROWS

    cond = jnp.concatenate([c, c_ctx[None, :], jnp.zeros((MOD_ROWS - b - 1, d), F32)], axis=0)
    mods = _ada(cond, w_ada, b_ada).reshape(depth * MOD_ROWS, 1, 6 * d)
    gf = g_final[None, :]
    w_gu_bf = jnp.concatenate([w_gate, w_up], axis=-1).astype(BF16)
    w_down_bf = w_down.astype(BF16)

    lat = x.reshape(n, d)
    w_in_bf = w_in_even[0].astype(BF16)
    tabs = _rope_tables(s)
    qx, k, v, u, z = _even_in(lat, mods, 0, lambda i: i // (s // EVEN_TM), g_norm1[0][None, :], w_in_bf,
                              tabs, s // EVEN_TM, EVEN_TM)
    ones = jnp.ones((n_ctx, LANES), F32)
    zeros = jnp.zeros((n_ctx, LANES), F32)
    _, kc, vc, _, _ = _even_in(ctx.reshape(b * n_ctx, d), mods, 0, lambda i: b, g_norm1[0][None, :],
                               w_in_bf, (ones, zeros, zeros), 1, n_ctx)
    bsp_full = jnp.repeat(b_spatial[0].T, HEAD_DIM, axis=1)
    half = N_Q_HEADS // 2
    w_att = w_out_even[0][:Q_DIM].reshape(2, half, HEAD_DIM, d).transpose(1, 0, 2, 3).reshape(Q_DIM, d)
    w_out_bf = jnp.concatenate([w_att, w_out_even[0][Q_DIM:]], axis=0).astype(BF16)
    lat = _even_mix(lat, qx, k, v, kc, vc, u, z, attn_sink[0], g_sgu[0][None, :],
                    w_spatial[0].astype(BF16), bsp_full, w_out_bf, mods, s, n_ctx)
    wr_t, br_t = _router_weights(w_router_group[0], b_router_group[0], w_router_expert[0], b_router_expert[0])
    lat = _moe(lat, mods, 0, g_norm2[0][None, :], wr_t, br_t, w_gu_bf, w_down_bf, gf, s, False)

    conv_w8 = jnp.concatenate([conv_w[0], jnp.zeros((8 - conv_w.shape[1], d), F32)], axis=0)
    lat = _odd_mix(lat, mods, 1, g_norm1[1][None, :], w_in_odd[0].astype(BF16), conv_w8,
                   w_out_odd[0].astype(BF16), s)
    wr_t, br_t = _router_weights(w_router_group[1], b_router_group[1], w_router_expert[1], b_router_expert[1])
    out = _moe(lat, mods, 1, g_norm2[1][None, :], wr_t, br_t, w_gu_bf, w_down_bf, gf, s, True)
    return out.reshape(b, s, d)
```

```python
import functools

import jax
import jax.numpy as jnp
from jax import lax
from jax.experimental import pallas as pl
from jax.experimental.pallas import tpu as pltpu
from jax.experimental.pallas import tpu_sc as plsc

F32 = jnp.float32
BF16 = jnp.bfloat16
HIGHEST = lax.Precision.HIGHEST

GRID_W = 64
N_Q_HEADS = 8
N_KV_HEADS = 2
HEAD_DIM = 64
ATT_BLOCK = 128
ROPE_BASE = 10000.0
Q_DIM = N_Q_HEADS * HEAD_DIM
KV_DIM = N_KV_HEADS * HEAD_DIM
SG_GROUPS = 8
SG_WIDTH = SG_GROUPS * HEAD_DIM
N_GROUPS = 4
EXPERTS_PER_GROUP = 8
N_EXPERTS = N_GROUPS * EXPERTS_PER_GROUP
EPS = 1e-6
NEG_INF = -1e30

LANES = 128
SC_CORES = 2
SC_SUBCORES = 16
SC_WORKERS = SC_CORES * SC_SUBCORES
SC_CHUNK = 32
MOD_ROWS = 8
VMEM_LIMIT = 56 * 1024 * 1024

ADA_TN = 1536
EVEN_TM = 512
ATT_TQ = 512
ODD_TM = 512
ROUTE_TM = 512
MOE_BM = 1024
MOE_QUANTUM = 256
MOE_PARTS = 2
WEIGHT_SLOTS = 3
COMBINE_TM = 512


def _params(sem):
    return pltpu.CompilerParams(dimension_semantics=sem, vmem_limit_bytes=VMEM_LIMIT)


def _rms_mod(x, g, shift, scale):
    ms = jnp.mean(x * x, axis=-1, keepdims=True)
    return (x * lax.rsqrt(ms + EPS) * g) * (1.0 + scale) + shift


def _pack_rows(a):
    w = a.shape[1] // 2
    hi = pltpu.bitcast(a[:, :w].astype(BF16).astype(F32), jnp.uint32)
    lo = pltpu.bitcast(a[:, w:].astype(BF16).astype(F32), jnp.uint32)
    return hi | (lo >> 16)


def _unpack_rows(p):
    hi = pltpu.bitcast(p & jnp.uint32(0xFFFF0000), F32)
    lo = pltpu.bitcast(p << 16, F32)
    return jnp.concatenate([hi, lo], axis=1)


def _ada_kernel(a_ref, w_ref, b_ref, o_ref):
    a = a_ref[...]
    s = a * (1.0 / (1.0 + jnp.exp(-a)))
    o_ref[0] = jnp.dot(s, w_ref[0], preferred_element_type=F32, precision=HIGHEST) + b_ref[0]


def _ada(cond, w_ada, b_ada):
    depth, d, six_d = w_ada.shape
    return pl.pallas_call(
        _ada_kernel,
        grid=(depth, six_d // ADA_TN),
        in_specs=[
            pl.BlockSpec((MOD_ROWS, d), lambda l, j: (0, 0)),
            pl.BlockSpec((1, d, ADA_TN), lambda l, j: (l, 0, j)),
            pl.BlockSpec((1, 1, ADA_TN), lambda l, j: (l, 0, j)),
        ],
        out_specs=pl.BlockSpec((1, MOD_ROWS, ADA_TN), lambda l, j: (l, 0, j)),
        out_shape=jax.ShapeDtypeStruct((depth, MOD_ROWS, six_d), F32),
        compiler_params=_params(("arbitrary", "arbitrary")),
        name="ada",
    )(cond, w_ada, b_ada.reshape(depth, 1, six_d))


def _even_in_kernel(x_ref, sh_ref, sc_ref, g_ref, w_ref, cos_ref, sa_ref, sb_ref,
                    qx_ref, k_ref, v_ref, u_ref, z_ref):
    h = _rms_mod(x_ref[...], g_ref[...], sh_ref[...], sc_ref[...])
    p = jnp.dot(h.astype(BF16), w_ref[...], preferred_element_type=F32)
    cos, sa, sb = cos_ref[...], sa_ref[...], sb_ref[...]

    def rope(t):
        return t * cos + pltpu.roll(t, LANES - 16, 1) * sa + pltpu.roll(t, 16, 1) * sb

    scale = HEAD_DIM ** -0.5
    low = lax.broadcasted_iota(jnp.int32, (x_ref.shape[0], LANES), 1) < HEAD_DIM
    heads_per_kv = N_Q_HEADS // N_KV_HEADS
    for cblk in range(Q_DIM // LANES):
        t = rope(p[:, cblk * LANES:(cblk + 1) * LANES]) * scale
        sw = pltpu.roll(t, HEAD_DIM, 1)
        zero = jnp.zeros_like(t)
        if (2 * cblk) // heads_per_kv == 0:
            first, second = jnp.where(low, t, zero), jnp.where(low, sw, zero)
        else:
            first, second = jnp.where(low, zero, sw), jnp.where(low, zero, t)
        qx_ref[:, (2 * cblk) * LANES:(2 * cblk + 1) * LANES] = first.astype(BF16)
        qx_ref[:, (2 * cblk + 1) * LANES:(2 * cblk + 2) * LANES] = second.astype(BF16)

    k_ref[...] = rope(p[:, Q_DIM:Q_DIM + KV_DIM]).astype(BF16)
    v_ref[...] = p[:, Q_DIM + KV_DIM:Q_DIM + 2 * KV_DIM].astype(BF16)
    u0 = Q_DIM + 2 * KV_DIM
    u_ref[...] = p[:, u0:u0 + SG_WIDTH]
    z_ref[...] = p[:, u0 + SG_WIDTH:u0 + 2 * SG_WIDTH]


def _even_in(x2d, mods, layer, mod_row_fn, g, w_bf, tabs, tab_blocks, tm):
    n, d = x2d.shape
    ein = w_bf.shape[1]
    cos, sa, sb = tabs
    row = lambda i: layer * MOD_ROWS + mod_row_fn(i)
    tab_spec = pl.BlockSpec((tm, LANES), lambda i: (i % tab_blocks, 0))
    qx_dim = N_Q_HEADS * LANES
    return pl.pallas_call(
        _even_in_kernel,
        grid=(n // tm,),
        in_specs=[
            pl.BlockSpec((tm, d), lambda i: (i, 0)),
            pl.BlockSpec((None, 1, d), lambda i: (row(i), 0, 0)),
            pl.BlockSpec((None, 1, d), lambda i: (row(i), 0, 1)),
            pl.BlockSpec((1, d), lambda i: (0, 0)),
            pl.BlockSpec((d, ein), lambda i: (0, 0)),
            tab_spec, tab_spec, tab_spec,
        ],
        out_specs=[
            pl.BlockSpec((tm, qx_dim), lambda i: (i, 0)),
            pl.BlockSpec((tm, KV_DIM), lambda i: (i, 0)),
            pl.BlockSpec((tm, KV_DIM), lambda i: (i, 0)),
            pl.BlockSpec((tm, SG_WIDTH), lambda i: (i, 0)),
            pl.BlockSpec((tm, SG_WIDTH), lambda i: (i, 0)),
        ],
        out_shape=[
            jax.ShapeDtypeStruct((n, qx_dim), BF16),
            jax.ShapeDtypeStruct((n, KV_DIM), BF16),
            jax.ShapeDtypeStruct((n, KV_DIM), BF16),
            jax.ShapeDtypeStruct((n, SG_WIDTH), F32),
            jax.ShapeDtypeStruct((n, SG_WIDTH), F32),
        ],
        compiler_params=_params(("parallel",)),
        name="even_in",
    )(x2d, mods, mods, g, w_bf, cos, sa, sb)


def _gelu(x):
    return 0.5 * x * (1.0 + lax.erf(x * (2.0 ** -0.5)))


def _even_mix_kernel(sink_ref, lat_ref, qx_ref, km_ref, kp_ref, kn_ref, vm_ref, vp_ref, vn_ref,
                     kc_ref, vc_ref, u_ref, z_ref, gsgu_ref, wsp_ref, bsp_ref, wout_ref, gate_ref,
                     o_ref, kband, vband, mixin, *, tiles_per_seq):
    i = pl.program_id(0)
    tq = qx_ref.shape[0]
    blk = ATT_BLOCK
    nsub = tq // blk
    n_ctx = kc_ref.shape[0]
    first = (i % tiles_per_seq) == 0
    last = (i % tiles_per_seq) == tiles_per_seq - 1

    kband[0:blk] = kp_ref[...]
    kband[blk:blk + tq] = km_ref[...]
    kband[blk + tq:] = kn_ref[...]
    vband[0:blk] = vp_ref[...]
    vband[blk:blk + tq] = vm_ref[...]
    vband[blk + tq:] = vn_ref[...]

    rows = N_Q_HEADS * blk
    tok = lax.broadcasted_iota(jnp.int32, (rows, blk), 0) & (blk - 1)
    col = lax.broadcasted_iota(jnp.int32, (rows, blk), 1)
    tri_prev = col >= tok
    tri_next = col <= tok
    head = lax.broadcasted_iota(jnp.int32, (rows, 1), 0) // blk
    sink_col = jnp.zeros((rows, 1), F32)
    for hd in range(N_Q_HEADS):
        sink_col = jnp.where(head == hd, sink_ref[hd], sink_col)
    lane_low = lax.broadcasted_iota(jnp.int32, (blk, LANES), 1) < HEAD_DIM
    ones = jnp.ones((n_ctx + 3 * blk, LANES), BF16)
    nt = (((1,), (1,)), ((), ()))

    def sub_block(j, carry):
        r0 = pl.multiple_of(j * blk, blk)
        ok_prev = jnp.logical_not(jnp.logical_and(first, j == 0))
        ok_next = jnp.logical_not(jnp.logical_and(last, j == nsub - 1))
        qs = jnp.concatenate([qx_ref[pl.ds(r0, blk), hd * LANES:(hd + 1) * LANES]
                              for hd in range(N_Q_HEADS)], axis=0)
        kall = jnp.concatenate([kc_ref[...], kband[pl.ds(r0, 3 * blk), :]], axis=0)
        vall = jnp.concatenate([vc_ref[...], vband[pl.ds(r0, 3 * blk), :]], axis=0)
        s = lax.dot_general(qs, kall, nt, preferred_element_type=F32)
        c0 = n_ctx
        s = jnp.concatenate([
            s[:, :c0],
            jnp.where(jnp.logical_and(tri_prev, ok_prev), s[:, c0:c0 + blk], NEG_INF),
            s[:, c0 + blk:c0 + 2 * blk],
            jnp.where(jnp.logical_and(tri_next, ok_next), s[:, c0 + 2 * blk:], NEG_INF),
        ], axis=1)
        m = jnp.maximum(jnp.max(s, axis=-1, keepdims=True), sink_col)
        p = jnp.exp(s - m).astype(BF16)
        o = jnp.dot(p, jnp.concatenate([vall, ones], axis=1), preferred_element_type=F32)
        att = o[:, :LANES] / (o[:, LANES:] + jnp.exp(sink_col - m))
        half = N_Q_HEADS // 2
        for hd in range(half):
            pair = jnp.where(lane_low, att[hd * blk:(hd + 1) * blk], att[(hd + half) * blk:(hd + half + 1) * blk])
            mixin[pl.ds(r0, blk), hd * LANES:(hd + 1) * LANES] = pair.astype(BF16)

        ug = _gelu(u_ref[pl.ds(r0, blk), :])
        zg = _gelu(z_ref[pl.ds(r0, blk), :])
        mu = jnp.mean(zg, axis=-1, keepdims=True)
        zc = zg - mu
        zn = zc * lax.rsqrt(jnp.mean(zc * zc, axis=-1, keepdims=True) + EPS) * gsgu_ref[...]
        for pair in range(SG_GROUPS // 2):
            zp = zn[:, pair * LANES:(pair + 1) * LANES]
            zero = jnp.zeros_like(zp)
            lo = jnp.where(lane_low, zp, zero).astype(BF16)
            hi = jnp.where(lane_low, zero, zp).astype(BF16)
            sg = (jnp.dot(wsp_ref[2 * pair], lo, preferred_element_type=F32)
                  + jnp.dot(wsp_ref[2 * pair + 1], hi, preferred_element_type=F32)
                  + bsp_ref[:, pair * LANES:(pair + 1) * LANES])
            mixin[pl.ds(r0, blk), Q_DIM + pair * LANES:Q_DIM + (pair + 1) * LANES] = (
                ug[:, pair * LANES:(pair + 1) * LANES] * sg).astype(BF16)
        return carry

    lax.fori_loop(0, nsub, sub_block, 0, unroll=True)
    mix = jnp.dot(mixin[...], wout_ref[...], preferred_element_type=F32)
    o_ref[...] = lat_ref[...] + gate_ref[...] * mix


def _even_mix(lat, qx, k, v, kc, vc, u, z, sink, g_sgu, wsp_bf, bsp_full, wout_bf, mods, seq, ctx_len):
    n, d = lat.shape
    tq = ATT_TQ
    tiles_per_seq = seq // tq
    sub = tq // ATT_BLOCK
    nblk = n // ATT_BLOCK
    main = lambda w: pl.BlockSpec((tq, w), lambda i: (i, 0))
    prev = pl.BlockSpec((ATT_BLOCK, KV_DIM), lambda i: (jnp.maximum(i * sub - 1, 0), 0))
    nxt = pl.BlockSpec((ATT_BLOCK, KV_DIM), lambda i: (jnp.minimum((i + 1) * sub, nblk - 1), 0))
    ctxs = pl.BlockSpec((ctx_len, KV_DIM), lambda i: (i // tiles_per_seq, 0))
    const = lambda shape: pl.BlockSpec(shape, lambda i: (0,) * len(shape), pipeline_mode=pl.Buffered(1))
    return pl.pallas_call(
        functools.partial(_even_mix_kernel, tiles_per_seq=tiles_per_seq),
        grid=(n // tq,),
        in_specs=[
            pl.BlockSpec(memory_space=pltpu.SMEM),
            main(d), main(qx.shape[1]),
            main(KV_DIM), prev, nxt,
            main(KV_DIM), prev, nxt,
            ctxs, ctxs,
            main(SG_WIDTH), main(SG_WIDTH),
            const((1, SG_WIDTH)), const(wsp_bf.shape), const(bsp_full.shape), const(wout_bf.shape),
            pl.BlockSpec((None, 1, d), lambda i: (i // tiles_per_seq, 0, 2)),
        ],
        out_specs=pl.BlockSpec((tq, d), lambda i: (i, 0)),
        out_shape=jax.ShapeDtypeStruct((n, d), F32),
        scratch_shapes=[
            pltpu.VMEM((tq + 2 * ATT_BLOCK, KV_DIM), BF16),
            pltpu.VMEM((tq + 2 * ATT_BLOCK, KV_DIM), BF16),
            pltpu.VMEM((tq, Q_DIM + SG_WIDTH), BF16),
        ],
        compiler_params=_params(("parallel",)),
        name="even_mix",
    )(sink, lat, qx, k, k, k, v, v, v, kc, vc, u, z, g_sgu, wsp_bf, bsp_full, wout_bf, mods)


def _odd_mix_kernel(x_ref, xp_ref, sh_ref, sc_ref, gate_ref, g_ref, win_ref, cw_ref, wout_ref,
                    o_ref, y_s, bg_s, tail_s, *, tiles_per_seq):
    i = pl.program_id(0)
    n_tiles = pl.num_programs(0) - 1
    tm, d = x_ref.shape
    cur = i % 2
    prv = 1 - cur

    tail_s[...] = y_s[cur, tm - 8:tm, :]

    @pl.when(i < n_tiles)
    def _():
        h = _rms_mod(x_ref[...], g_ref[...], sh_ref[...], sc_ref[...])
        p = jnp.dot(h.astype(BF16), win_ref[...], preferred_element_type=F32)
        bg_s[cur] = p[:, 0:d]
        y_s[cur] = p[:, d:2 * d] * p[:, 2 * d:3 * d]

    @pl.when(i >= 1)
    def _():
        t_prev = i - 1
        first = (t_prev % tiles_per_seq) == 0
        last = (t_prev % tiles_per_seq) == tiles_per_seq - 1
        y = y_s[prv]
        left = jnp.where(first, 0.0, tail_s[7:8, :])
        right = jnp.where(last, 0.0, y_s[cur, 0:1, :])
        ridx = lax.broadcasted_iota(jnp.int32, (tm, d), 0)
        y_dn = jnp.where(ridx == 0, left, pltpu.roll(y, 1, 0))
        y_up = jnp.where(ridx == tm - 1, right, pltpu.roll(y, tm - 1, 0))
        conv = y_dn * cw_ref[0:1, :] + y * cw_ref[1:2, :] + y_up * cw_ref[2:3, :]
        mix = jnp.dot((bg_s[prv] * conv).astype(BF16), wout_ref[...], preferred_element_type=F32)
        o_ref[...] = xp_ref[...] + gate_ref[...] * mix


def _odd_mix(lat, mods, layer, g, win_bf, conv_w8, wout_bf, seq):
    n, d = lat.shape
    tm = ODD_TM
    nt = n // tm
    tiles_per_seq = seq // tm
    cur = lambda i: jnp.minimum(i, nt - 1)
    prv = lambda i: jnp.maximum(i - 1, 0)
    row = lambda t: layer * MOD_ROWS + t // tiles_per_seq
    const = lambda shape: pl.BlockSpec(shape, lambda i: (0,) * len(shape), pipeline_mode=pl.Buffered(1))
    return pl.pallas_call(
        functools.partial(_odd_mix_kernel, tiles_per_seq=tiles_per_seq),
        grid=(nt + 1,),
        in_specs=[
            pl.BlockSpec((tm, d), lambda i: (cur(i), 0)),
            pl.BlockSpec((tm, d), lambda i: (prv(i), 0)),
            pl.BlockSpec((None, 1, d), lambda i: (row(cur(i)), 0, 0)),
            pl.BlockSpec((None, 1, d), lambda i: (row(cur(i)), 0, 1)),
            pl.BlockSpec((None, 1, d), lambda i: (row(prv(i)), 0, 2)),
            const((1, d)), const(win_bf.shape), const(conv_w8.shape), const(wout_bf.shape),
        ],
        out_specs=pl.BlockSpec((tm, d), lambda i: (prv(i), 0)),
        out_shape=jax.ShapeDtypeStruct((n, d), F32),
        scratch_shapes=[
            pltpu.VMEM((2, tm, d), F32),
            pltpu.VMEM((2, tm, d), F32),
            pltpu.VMEM((8, d), F32),
        ],
        compiler_params=_params(("arbitrary",)),
        name="odd_mix",
    )(lat, lat, mods, mods, mods, g, win_bf, conv_w8, wout_bf)


def _router_kernel(x_ref, sh_ref, sc_ref, g_ref, wr_ref, br_ref,
                   h_ref, mi_ref, wc_ref, cnt_ref, carry, upper):
    i = pl.program_id(0)
    tm = x_ref.shape[0]
    epg = EXPERTS_PER_GROUP

    @pl.when(i == 0)
    def _():
        carry[...] = jnp.zeros_like(carry)
        r_i = lax.broadcasted_iota(jnp.int32, (tm, tm), 0)
        c_i = lax.broadcasted_iota(jnp.int32, (tm, tm), 1)
        upper[...] = jnp.where(r_i < c_i, 1.0, 0.0).astype(BF16)

    h = _rms_mod(x_ref[...], g_ref[...], sh_ref[...], sc_ref[...])
    h_ref[...] = _pack_rows(h)
    h_hi = h.astype(BF16)
    h_lo = (h - h_hi.astype(F32)).astype(BF16)
    w = wr_ref[...]
    w1 = w.astype(BF16).astype(F32)
    r1 = w - w1
    w2 = r1.astype(BF16).astype(F32)
    w3 = r1 - w2
    nt = (((1,), (1,)), ((), ()))
    nr = w.shape[0]
    w123 = jnp.concatenate([w1, w2, w3, jnp.zeros((8, w.shape[1]), F32)], axis=0).astype(BF16)
    w12 = jnp.concatenate([w1, w2], axis=0).astype(BF16)
    p_hi = lax.dot_general(w123, h_hi, nt, preferred_element_type=F32)
    p_lo = lax.dot_general(w12, h_lo, nt, preferred_element_type=F32)
    lg = ((p_hi[2 * nr:3 * nr] + p_lo[nr:2 * nr]) + (p_hi[nr:2 * nr] + p_lo[0:nr])) + p_hi[0:nr] + br_ref[...]
    io8 = lax.broadcasted_iota(jnp.int32, (epg, tm), 0)
    gl = lg[0:epg]
    gmax = jnp.max(gl, axis=0, keepdims=True)
    g_idx = jnp.min(jnp.where(gl == gmax, io8, epg), axis=0, keepdims=True)
    g_w = 1.0 / jnp.sum(jnp.exp(gl - gmax), axis=0, keepdims=True)
    e_sel = lg[epg:2 * epg]
    for gi in range(1, N_GROUPS):
        e_sel = jnp.where(g_idx == gi, lg[(gi + 1) * epg:(gi + 2) * epg], e_sel)
    v0 = jnp.max(e_sel, axis=0, keepdims=True)
    i0 = jnp.min(jnp.where(e_sel == v0, io8, epg), axis=0, keepdims=True)
    rest = jnp.where(io8 == i0, -jnp.inf, e_sel)
    v1 = jnp.max(rest, axis=0, keepdims=True)
    i1 = jnp.min(jnp.where(rest == v1, io8, epg), axis=0, keepdims=True)
    t = jnp.exp(v1 - v0)
    w0 = g_w / (1.0 + t)
    w1 = g_w * t / (1.0 + t)
    e0 = g_idx * epg + i0
    e1 = g_idx * epg + i1

    io32 = lax.broadcasted_iota(jnp.int32, (N_EXPERTS, tm), 0)
    hit0 = io32 == e0
    hit1 = io32 == e1
    onehot = jnp.where(hit0 | hit1, 1.0, 0.0)
    cum = jnp.dot(onehot.astype(BF16), upper[...], preferred_element_type=F32) + carry[...]
    rank0 = jnp.sum(jnp.where(hit0, cum, 0.0), axis=0, keepdims=True).astype(jnp.int32)
    rank1 = jnp.sum(jnp.where(hit1, cum, 0.0), axis=0, keepdims=True).astype(jnp.int32)
    carry[...] = carry[...] + jnp.sum(onehot, axis=1, keepdims=True)
    cnt_ref[...] = jnp.broadcast_to(carry[...], cnt_ref.shape)

    mi_ref[...] = jnp.where(io8 == 0, e0, jnp.where(io8 == 1, e1, jnp.where(io8 == 2, rank0,
                            jnp.where(io8 == 3, rank1, 0))))
    io128 = lax.broadcasted_iota(jnp.int32, (LANES, tm), 0)
    wrow = jnp.where(io128 == 0, w0, jnp.where(io128 == 1, w1, 0.0))
    wc_ref[...] = wrow.T


def _router(lat, mods, layer, g, wr_t, br_t, seq, tok0, n):
    d = lat.shape[1]
    tm = ROUTE_TM
    tiles_per_seq = seq // tm
    t0 = tok0 // tm
    row = lambda i: layer * MOD_ROWS + (t0 + i) // tiles_per_seq
    const = lambda shape: pl.BlockSpec(shape, lambda i: (0,) * len(shape), pipeline_mode=pl.Buffered(1))
    return pl.pallas_call(
        _router_kernel,
        grid=(n // tm,),
        in_specs=[
            pl.BlockSpec((tm, d), lambda i: (t0 + i, 0)),
            pl.BlockSpec((None, 1, d), lambda i: (row(i), 0, 3)),
            pl.BlockSpec((None, 1, d), lambda i: (row(i), 0, 4)),
            const((1, d)), const(wr_t.shape), const(br_t.shape),
        ],
        out_specs=[
            pl.BlockSpec((tm, d // 2), lambda i: (i, 0)),
            pl.BlockSpec((8, tm), lambda i: (0, i)),
            pl.BlockSpec((tm, LANES), lambda i: (i, 0)),
            pl.BlockSpec((N_EXPERTS, LANES), lambda i: (0, 0)),
        ],
        out_shape=[
            jax.ShapeDtypeStruct((n, d // 2), jnp.uint32),
            jax.ShapeDtypeStruct((8, n), jnp.int32),
            jax.ShapeDtypeStruct((n, LANES), F32),
            jax.ShapeDtypeStruct((N_EXPERTS, LANES), F32),
        ],
        scratch_shapes=[pltpu.VMEM((N_EXPERTS, 1), F32), pltpu.VMEM((tm, tm), BF16)],
        compiler_params=_params(("arbitrary",)),
        name="router",
    )(lat, mods, mods, g, wr_t, br_t)


def _plan_kernel(cnt_ref, mi_ref, dest_ref, be_ref, runs_ref, nv_ref, nu_ref, ps_ref, *, n_blocks):
    bm = MOE_BM

    def per_expert(e, carry):
        blk0, n_runs = carry
        cnt = cnt_ref[e]
        nb = (cnt + bm - 1) // bm
        ps_ref[e] = blk0 * bm

        def fill(b, c):
            be_ref[b] = e
            nv_ref[b] = jnp.minimum(cnt - (b - blk0) * bm, bm)
            return c

        lax.fori_loop(blk0, blk0 + nb, fill, 0)

        @pl.when(nb > 0)
        def _():
            runs_ref[n_runs] = e

        return blk0 + nb, n_runs + jnp.where(nb > 0, 1, 0)

    n_used, n_runs = lax.fori_loop(0, N_EXPERTS, per_expert, (0, 0))
    nu_ref[0] = n_used
    nu_ref[1] = n_runs
    last_e = be_ref[jnp.maximum(n_used - 1, 0)]

    def fill_tail(b, c):
        be_ref[b] = last_e
        nv_ref[b] = 0
        return c

    lax.fori_loop(n_used, n_blocks, fill_tail, 0)

    def fill_runs(k, c):
        runs_ref[k] = last_e
        return c

    lax.fori_loop(n_runs, N_EXPERTS, fill_runs, 0)

    e01 = mi_ref[0:2, :]
    dest = mi_ref[2:4, :]
    for e in range(N_EXPERTS):
        dest = dest + jnp.where(e01 == e, ps_ref[e], 0)
    dest_ref[...] = dest


def _plan(counts, meta_i, n_blocks):
    n = meta_i.shape[1]
    return pl.pallas_call(
        functools.partial(_plan_kernel, n_blocks=n_blocks),
        in_specs=[pl.BlockSpec(memory_space=pltpu.SMEM), pl.BlockSpec(memory_space=pltpu.VMEM)],
        out_specs=[pl.BlockSpec(memory_space=pltpu.VMEM)] + [pl.BlockSpec(memory_space=pltpu.SMEM)] * 4,
        out_shape=[
            jax.ShapeDtypeStruct((2, n), jnp.int32),
            jax.ShapeDtypeStruct((n_blocks,), jnp.int32),
            jax.ShapeDtypeStruct((N_EXPERTS,), jnp.int32),
            jax.ShapeDtypeStruct((n_blocks,), jnp.int32),
            jax.ShapeDtypeStruct((2,), jnp.int32),
        ],
        scratch_shapes=[pltpu.SMEM((N_EXPERTS,), jnp.int32)],
        compiler_params=pltpu.CompilerParams(vmem_limit_bytes=VMEM_LIMIT),
        name="plan",
    )(counts, meta_i)


def _sc_mesh():
    return plsc.VectorSubcoreMesh(core_axis_name="c", subcore_axis_name="s",
                                  num_cores=SC_CORES, num_subcores=SC_SUBCORES)


def _sc_worker():
    return lax.axis_index("s") * SC_CORES + lax.axis_index("c")


def _sc_dispatch(h2, dest, n_rows):
    n, d = h2.shape
    c = SC_CHUNK
    per_w = n // SC_WORKERS
    nchunk = per_w // c
    idx = dest.reshape(2, SC_WORKERS, nchunk, c)

    @functools.partial(
        pl.kernel, mesh=_sc_mesh(), out_type=jax.ShapeDtypeStruct((n_rows, d), h2.dtype),
        scratch_types=[pltpu.VMEM((nchunk, c), jnp.int32), pltpu.VMEM((nchunk, c), jnp.int32),
                       pltpu.VMEM((2, c, d), h2.dtype),
                       pltpu.SemaphoreType.DMA((2,)), pltpu.SemaphoreType.DMA((2,))])
    def k(h_hbm, idx_hbm, xb_hbm, idx0_v, idx1_v, rows_v, gsem, ssem):
        wid = _sc_worker()
        base = wid * per_w
        idx_v = (idx0_v, idx1_v)
        for kk in range(2):
            pltpu.sync_copy(idx_hbm.at[kk, wid], idx_v[kk])

        def get(j, slot):
            return pltpu.make_async_copy(h_hbm.at[pl.ds(base + j * c, c)], rows_v.at[slot], gsem.at[slot])

        def put(j, slot, kk):
            return pltpu.make_async_copy(rows_v.at[slot], xb_hbm.at[idx_v[kk].at[j]], ssem.at[slot])

        get(0, 0).start()

        @pl.loop(0, nchunk, step=2)
        def _(j):
            for slot in range(2):
                jj = j + slot
                get(jj, slot).wait()

                @pl.when(jj >= 1)
                def _():
                    for kk in range(2):
                        put(jj - 1, 1 - slot, kk).wait()

                @pl.when(jj + 1 < nchunk)
                def _():
                    get(jj + 1, 1 - slot).start()

                for kk in range(2):
                    put(jj, slot, kk).start()

        for kk in range(2):
            put(nchunk - 1, (nchunk - 1) % 2, kk).wait()

    return k(h2, idx)


def _sc_gather(y, dest):
    d = y.shape[1]
    total = dest.shape[0] * dest.shape[1]
    c = SC_CHUNK
    per_w = total // SC_WORKERS
    nchunk = per_w // c
    idx = dest.reshape(SC_WORKERS, nchunk, c)

    @functools.partial(
        pl.kernel, mesh=_sc_mesh(), out_type=jax.ShapeDtypeStruct((total, d), y.dtype),
        scratch_types=[pltpu.VMEM((nchunk, c), jnp.int32), pltpu.VMEM((2, c, d), y.dtype),
                       pltpu.SemaphoreType.DMA((2,)), pltpu.SemaphoreType.DMA((2,))])
    def k(y_hbm, idx_hbm, out_hbm, idx_v, rows_v, gsem, ssem):
        wid = _sc_worker()
        base = wid * per_w
        pltpu.sync_copy(idx_hbm.at[wid], idx_v)

        def get(j, slot):
            return pltpu.make_async_copy(y_hbm.at[idx_v.at[j]], rows_v.at[slot], gsem.at[slot])

        def put(j, slot):
            return pltpu.make_async_copy(rows_v.at[slot], out_hbm.at[pl.ds(base + j * c, c)], ssem.at[slot])

        get(0, 0).start()

        @pl.loop(0, nchunk, step=2)
        def _(j):
            for slot in range(2):
                jj = j + slot
                get(jj, slot).wait()

                @pl.when(jj >= 1)
                def _():
                    put(jj - 1, 1 - slot).wait()

                @pl.when(jj + 1 < nchunk)
                def _():
                    get(jj + 1, 1 - slot).start()

                put(jj, slot).start()

        put(nchunk - 1, (nchunk - 1) % 2).wait()

    return k(y, idx)


def _expert_kernel(be_ref, runs_ref, nv_ref, nu_ref, x_ref, wg_hbm, wu_hbm, wd_hbm, y_ref,
                   wgu_s, wd_s, stg_g, stg_u, stg_d, run_s, sems, *, layer):
    b = pl.program_id(0)
    hid = stg_g.shape[2]
    e = be_ref[b]
    n_runs = nu_ref[1]
    changed = jnp.logical_or(b == 0, e != be_ref[jnp.maximum(b - 1, 0)])

    def fetch(run):
        expert = runs_ref[run]
        slot = run % WEIGHT_SLOTS
        return (pltpu.make_async_copy(wg_hbm.at[layer, expert], stg_g.at[slot], sems.at[slot]),
                pltpu.make_async_copy(wu_hbm.at[layer, expert], stg_u.at[slot], sems.at[slot]),
                pltpu.make_async_copy(wd_hbm.at[layer, expert], stg_d.at[slot], sems.at[slot]))

    @pl.when(b == 0)
    def _():
        for r in range(WEIGHT_SLOTS - 1):
            @pl.when(r < n_runs)
            def _():
                for cp in fetch(r):
                    cp.start()

    @pl.when(changed)
    def _():
        run = jnp.where(b == 0, 0, run_s[0] + 1)
        run_s[0] = run
        for cp in fetch(run):
            cp.wait()

        ahead = run + WEIGHT_SLOTS - 1

        @pl.when(ahead < n_runs)
        def _():
            for cp in fetch(ahead):
                cp.start()

        slot = run % WEIGHT_SLOTS
        wgu_s[:, 0:hid] = stg_g[slot].astype(BF16)
        wgu_s[:, hid:2 * hid] = stg_u[slot].astype(BF16)
        wd_s[...] = stg_d[slot].astype(BF16)

    bm, dp = x_ref.shape
    nv = nv_ref[b]
    in_use = b < nu_ref[0]

    def run(rows):
        live = lax.broadcasted_iota(jnp.int32, (rows, dp), 0) < nv
        x = _unpack_rows(jnp.where(live, x_ref[0:rows, :], jnp.uint32(0)))
        gu = jnp.dot(x.astype(BF16), wgu_s[...], preferred_element_type=F32)
        gate = gu[:, 0:hid]
        act = gate * (1.0 / (1.0 + jnp.exp(-gate))) * gu[:, hid:2 * hid]
        y_ref[0:rows, :] = _pack_rows(jnp.dot(act.astype(BF16), wd_s[...], preferred_element_type=F32))

    n_quanta = bm // MOE_QUANTUM
    for q in range(1, n_quanta + 1):
        rows = q * MOE_QUANTUM

        @pl.when(jnp.logical_and(in_use, jnp.logical_and(nv > rows - MOE_QUANTUM, nv <= rows)))
        def _(rows=rows):
            run(rows)
            if rows < bm:
                y_ref[rows:bm, :] = jnp.zeros((bm - rows, dp), y_ref.dtype)

    @pl.when(jnp.logical_not(in_use))
    def _():
        y_ref[...] = jnp.zeros_like(y_ref)


def _experts(block_e, runs, n_valid, n_used, xb, w_gate, w_up, w_down, layer):
    n_rows, dp = xb.shape
    d, hid = w_gate.shape[2], w_gate.shape[3]
    bm = MOE_BM
    n_blocks = n_rows // bm
    hbm = pl.BlockSpec(memory_space=pl.ANY)
    return pl.pallas_call(
        functools.partial(_expert_kernel, layer=layer),
        grid_spec=pltpu.PrefetchScalarGridSpec(
            num_scalar_prefetch=4,
            grid=(n_blocks,),
            in_specs=[
                pl.BlockSpec((bm, dp), lambda b, be, nx, nv, nu: (jnp.minimum(b, nu[0] - 1), 0)),
                hbm, hbm, hbm,
            ],
            out_specs=pl.BlockSpec((bm, dp), lambda b, be, nx, nv, nu: (b, 0)),
            scratch_shapes=[
                pltpu.VMEM((d, 2 * hid), BF16), pltpu.VMEM((hid, d), BF16),
                pltpu.VMEM((WEIGHT_SLOTS, d, hid), F32), pltpu.VMEM((WEIGHT_SLOTS, d, hid), F32),
                pltpu.VMEM((WEIGHT_SLOTS, hid, d), F32),
                pltpu.SMEM((1,), jnp.int32), pltpu.SemaphoreType.DMA((WEIGHT_SLOTS,)),
            ],
        ),
        out_shape=jax.ShapeDtypeStruct((n_rows, dp), jnp.uint32),
        compiler_params=_params(("arbitrary",)),
        name="experts",
    )(block_e, runs, n_valid, n_used, xb, w_gate, w_up, w_down)


def _combine_kernel(lat_ref, y0_ref, y1_ref, wc_ref, gate_ref, gf_ref, *rest, final):
    o_ref = rest[-1]
    wc = wc_ref[...]
    moe = wc[:, 0:1] * _unpack_rows(y0_ref[...]) + wc[:, 1:2] * _unpack_rows(y1_ref[...])
    out = lat_ref[...] + gate_ref[...] * moe
    if final:
        ms = jnp.mean(out * out, axis=-1, keepdims=True)
        out = out * lax.rsqrt(ms + EPS) * gf_ref[...]
    o_ref[...] = out


def _combine(lat, yg, wcol, mods, layer, g_final, seq, final, tok0, prev_out):
    n, d = lat.shape
    tm = COMBINE_TM
    nt = wcol.shape[0] // tm
    t0 = tok0 // tm
    tiles_per_seq = seq // tm
    row = lambda i: layer * MOD_ROWS + (t0 + i) // tiles_per_seq
    in_specs = [
        pl.BlockSpec((tm, d), lambda i: (t0 + i, 0)),
        pl.BlockSpec((tm, d // 2), lambda i: (i, 0)),
        pl.BlockSpec((tm, d // 2), lambda i: (nt + i, 0)),
        pl.BlockSpec((tm, LANES), lambda i: (i, 0)),
        pl.BlockSpec((None, 1, d), lambda i: (row(i), 0, 5)),
        pl.BlockSpec((1, d), lambda i: (0, 0)),
    ]
    args = [lat, yg, yg, wcol, mods, g_final]
    aliases = {}
    if prev_out is not None:
        in_specs.append(pl.BlockSpec(memory_space=pl.ANY))
        args.append(prev_out)
        aliases = {len(args) - 1: 0}
    return pl.pallas_call(
        functools.partial(_combine_kernel, final=final),
        grid=(nt,),
        in_specs=in_specs,
        out_specs=pl.BlockSpec((tm, d), lambda i: (t0 + i, 0)),
        out_shape=jax.ShapeDtypeStruct((n, d), F32),
        input_output_aliases=aliases,
        compiler_params=_params(("parallel",)),
        name="combine",
    )(*args)


def _moe(lat, mods, layer, g2, wr_t, br_t, w_gate, w_up, w_down, g_final, seq, final):
    n, d = lat.shape
    part = n // MOE_PARTS
    n_blocks = (2 * part) // MOE_BM + N_EXPERTS
    out = None
    for p in range(MOE_PARTS):
        tok0 = p * part
        h2, meta_i, wcol, counts = _router(lat, mods, layer, g2, wr_t, br_t, seq, tok0, part)
        dest, block_e, runs, n_valid, n_used = _plan(counts[:, 0].astype(jnp.int32), meta_i, n_blocks)
        xb = _sc_dispatch(h2, dest, n_blocks * MOE_BM)
        yb = _experts(block_e, runs, n_valid, n_used, xb, w_gate, w_up, w_down, layer)
        yg = _sc_gather(yb, dest)
        out = _combine(lat, yg, wcol, mods, layer, g_final, seq, final, tok0, out)
    return out


def _rope_tables(seq):
    quarter = HEAD_DIM // 4
    pos = jnp.arange(seq, dtype=F32)
    row_ids = jnp.floor(pos / GRID_W)
    col_ids = pos - row_ids * GRID_W
    inv = ROPE_BASE ** (-jnp.arange(quarter, dtype=F32) / quarter)
    ang_r = row_ids[:, None] * inv
    ang_c = col_ids[:, None] * inv
    zero = jnp.zeros_like(ang_r)
    cos = jnp.concatenate([jnp.cos(ang_r), jnp.cos(ang_r), jnp.cos(ang_c), jnp.cos(ang_c)], axis=-1)
    sa = jnp.concatenate([-jnp.sin(ang_r), zero, -jnp.sin(ang_c), zero], axis=-1)
    sb = jnp.concatenate([zero, jnp.sin(ang_r), zero, jnp.sin(ang_c)], axis=-1)
    rep = LANES // HEAD_DIM
    return tuple(jnp.tile(t, (1, rep)) for t in (cos, sa, sb))


def _router_weights(w_rg, b_rg, w_re, b_re):
    d = w_rg.shape[0]
    pad = EXPERTS_PER_GROUP - N_GROUPS
    wr_t = jnp.concatenate([w_rg.T, jnp.zeros((pad, d), F32), w_re.T], axis=0)
    br_t = jnp.concatenate([b_rg, jnp.full((pad,), NEG_INF, F32), b_re])[:, None]
    return wr_t, br_t


def kernel(x, c, ctx, c_ctx, w_ada, b_ada, g_norm1, g_norm2, g_final, w_in_even, attn_sink, g_sgu,
           w_spatial, b_spatial, w_out_even, w_in_odd, conv_w, w_out_odd, w_router_group,
           b_router_group, w_router_expert, b_router_expert, w_gate, w_up, w_down):
    b, s, d = x.shape
    n = b * s
    n_ctx = ctx.shape[1]
    depth = w_ada.shape[0]
    assert depth == 2 and b + 1 <= MOD_ROWS

    cond = jnp.concatenate([c, c_ctx[None, :], jnp.zeros((MOD_ROWS - b - 1, d), F32)], axis=0)
    mods = _ada(cond, w_ada, b_ada).reshape(depth * MOD_ROWS, 1, 6 * d)
    gf = g_final[None, :]

    lat = x.reshape(n, d)
    w_in_bf = w_in_even[0].astype(BF16)
    tabs = _rope_tables(s)
    qx, k, v, u, z = _even_in(lat, mods, 0, lambda i: i // (s // EVEN_TM), g_norm1[0][None, :], w_in_bf,
                              tabs, s // EVEN_TM, EVEN_TM)
    ones = jnp.ones((n_ctx, LANES), F32)
    zeros = jnp.zeros((n_ctx, LANES), F32)
    _, kc, vc, _, _ = _even_in(ctx.reshape(b * n_ctx, d), mods, 0, lambda i: b, g_norm1[0][None, :],
                               w_in_bf, (ones, zeros, zeros), 1, n_ctx)
    bsp_full = jnp.repeat(b_spatial[0].T, HEAD_DIM, axis=1)
    half = N_Q_HEADS // 2
    w_att = w_out_even[0][:Q_DIM].reshape(2, half, HEAD_DIM, d).transpose(1, 0, 2, 3).reshape(Q_DIM, d)
    w_out_bf = jnp.concatenate([w_att, w_out_even[0][Q_DIM:]], axis=0).astype(BF16)
    lat = _even_mix(lat, qx, k, v, kc, vc, u, z, attn_sink[0], g_sgu[0][None, :],
                    w_spatial[0].astype(BF16), bsp_full, w_out_bf, mods, s, n_ctx)
    wr_t, br_t = _router_weights(w_router_group[0], b_router_group[0], w_router_expert[0], b_router_expert[0])
    lat = _moe(lat, mods, 0, g_norm2[0][None, :], wr_t, br_t, w_gate, w_up, w_down, gf, s, False)

    conv_w8 = jnp.concatenate([conv_w[0], jnp.zeros((8 - conv_w.shape[1], d), F32)], axis=0)
    lat = _odd_mix(lat, mods, 1, g_norm1[1][None, :], w_in_odd[0].astype(BF16), conv_w8,
                   w_out_odd[0].astype(BF16), s)
    wr_t, br_t = _router_weights(w_router_group[1], b_router_group[1], w_router_expert[1], b_router_expert[1])
    out = _moe(lat, mods, 1, g_norm2[1][None, :], wr_t, br_t, w_gate, w_up, w_down, gf, s, True)
    return out.reshape(b, s, d)
```

```python
import functools

import jax
import jax.numpy as jnp
from jax import lax
from jax.experimental import pallas as pl
from jax.experimental.pallas import tpu as pltpu
from jax.experimental.pallas import tpu_sc as plsc

F32 = jnp.float32
BF16 = jnp.bfloat16
HIGHEST = lax.Precision.HIGHEST

GRID_W = 64
N_Q_HEADS = 8
N_KV_HEADS = 2
HEAD_DIM = 64
ATT_BLOCK = 128
ROPE_BASE = 10000.0
Q_DIM = N_Q_HEADS * HEAD_DIM
KV_DIM = N_KV_HEADS * HEAD_DIM
SG_GROUPS = 8
SG_WIDTH = SG_GROUPS * HEAD_DIM
N_GROUPS = 4
EXPERTS_PER_GROUP = 8
N_EXPERTS = N_GROUPS * EXPERTS_PER_GROUP
EPS = 1e-6
NEG_INF = -1e30

LANES = 128
SC_CORES = 2
SC_SUBCORES = 16
SC_WORKERS = SC_CORES * SC_SUBCORES
SC_CHUNK = 32
MOD_ROWS = 8
ROUTER_ROWS = EXPERTS_PER_GROUP + N_EXPERTS
VMEM_LIMIT = 56 * 1024 * 1024

ADA_TN = 1536
EVEN_TM = 512
ATT_TQ = 512
ODD_TM = 512
MOE_BM = 1024
MOE_QUANTUM = 128
MOE_PARTS = 2
WEIGHT_SLOTS = 3
COMBINE_TM = 512


def _params(sem):
    return pltpu.CompilerParams(dimension_semantics=sem, vmem_limit_bytes=VMEM_LIMIT)


def _rms_mod(x, g, shift, scale):
    ms = jnp.mean(x * x, axis=-1, keepdims=True)
    return (x * lax.rsqrt(ms + EPS)) * (g * (1.0 + scale)) + shift


def _pack_rounded(a):
    w = a.shape[1] // 2
    hi = pltpu.bitcast(a[:, :w], jnp.uint32)
    lo = pltpu.bitcast(a[:, w:], jnp.uint32)
    return hi | (lo >> 16)


def _pack_rows(a):
    return _pack_rounded(a.astype(BF16).astype(F32))


def _unpack_rows(p):
    hi = pltpu.bitcast(p & jnp.uint32(0xFFFF0000), F32)
    lo = pltpu.bitcast(p << 16, F32)
    return jnp.concatenate([hi, lo], axis=1)


def _ada_kernel(a_ref, w_ref, b_ref, o_ref):
    a = a_ref[...]
    s = a * (1.0 / (1.0 + jnp.exp(-a)))
    o_ref[0] = jnp.dot(s, w_ref[0], preferred_element_type=F32, precision=HIGHEST) + b_ref[0]


def _ada(cond, w_ada, b_ada):
    depth, d, six_d = w_ada.shape
    return pl.pallas_call(
        _ada_kernel,
        grid=(depth, six_d // ADA_TN),
        in_specs=[
            pl.BlockSpec((MOD_ROWS, d), lambda l, j: (0, 0)),
            pl.BlockSpec((1, d, ADA_TN), lambda l, j: (l, 0, j)),
            pl.BlockSpec((1, 1, ADA_TN), lambda l, j: (l, 0, j)),
        ],
        out_specs=pl.BlockSpec((1, MOD_ROWS, ADA_TN), lambda l, j: (l, 0, j)),
        out_shape=jax.ShapeDtypeStruct((depth, MOD_ROWS, six_d), F32),
        compiler_params=_params(("arbitrary", "arbitrary")),
        name="ada",
    )(cond, w_ada, b_ada.reshape(depth, 1, six_d))


def _even_in_kernel(x_ref, sh_ref, sc_ref, g_ref, w_ref, cos_ref, sa_ref, sb_ref,
                    qx_ref, k_ref, v_ref, u_ref, z_ref):
    h = _rms_mod(x_ref[...], g_ref[...], sh_ref[...], sc_ref[...])
    p = jnp.dot(h.astype(BF16), w_ref[...], preferred_element_type=F32)
    cos, sa, sb = cos_ref[...], sa_ref[...], sb_ref[...]

    def rope(t):
        return t * cos + pltpu.roll(t, LANES - 16, 1) * sa + pltpu.roll(t, 16, 1) * sb

    scale = HEAD_DIM ** -0.5
    low = lax.broadcasted_iota(jnp.int32, (x_ref.shape[0], LANES), 1) < HEAD_DIM
    heads_per_kv = N_Q_HEADS // N_KV_HEADS
    for cblk in range(Q_DIM // LANES):
        t = rope(p[:, cblk * LANES:(cblk + 1) * LANES]) * scale
        sw = pltpu.roll(t, HEAD_DIM, 1)
        zero = jnp.zeros_like(t)
        if (2 * cblk) // heads_per_kv == 0:
            first, second = jnp.where(low, t, zero), jnp.where(low, sw, zero)
        else:
            first, second = jnp.where(low, zero, sw), jnp.where(low, zero, t)
        qx_ref[:, (2 * cblk) * LANES:(2 * cblk + 1) * LANES] = first.astype(BF16)
        qx_ref[:, (2 * cblk + 1) * LANES:(2 * cblk + 2) * LANES] = second.astype(BF16)

    k_ref[...] = rope(p[:, Q_DIM:Q_DIM + KV_DIM]).astype(BF16)
    v_ref[...] = p[:, Q_DIM + KV_DIM:Q_DIM + 2 * KV_DIM].astype(BF16)
    u0 = Q_DIM + 2 * KV_DIM
    u_ref[...] = p[:, u0:u0 + SG_WIDTH]
    z_ref[...] = p[:, u0 + SG_WIDTH:u0 + 2 * SG_WIDTH]


def _even_in(x2d, mods, layer, mod_row_fn, g, w_bf, tabs, tab_blocks, tm):
    n, d = x2d.shape
    ein = w_bf.shape[1]
    cos, sa, sb = tabs
    row = lambda i: layer * MOD_ROWS + mod_row_fn(i)
    tab_spec = pl.BlockSpec((tm, LANES), lambda i: (i % tab_blocks, 0))
    qx_dim = N_Q_HEADS * LANES
    return pl.pallas_call(
        _even_in_kernel,
        grid=(n // tm,),
        in_specs=[
            pl.BlockSpec((tm, d), lambda i: (i, 0)),
            pl.BlockSpec((None, 1, d), lambda i: (row(i), 0, 0)),
            pl.BlockSpec((None, 1, d), lambda i: (row(i), 0, 1)),
            pl.BlockSpec((1, d), lambda i: (0, 0)),
            pl.BlockSpec((d, ein), lambda i: (0, 0)),
            tab_spec, tab_spec, tab_spec,
        ],
        out_specs=[
            pl.BlockSpec((tm, qx_dim), lambda i: (i, 0)),
            pl.BlockSpec((tm, KV_DIM), lambda i: (i, 0)),
            pl.BlockSpec((tm, KV_DIM), lambda i: (i, 0)),
            pl.BlockSpec((tm, SG_WIDTH), lambda i: (i, 0)),
            pl.BlockSpec((tm, SG_WIDTH), lambda i: (i, 0)),
        ],
        out_shape=[
            jax.ShapeDtypeStruct((n, qx_dim), BF16),
            jax.ShapeDtypeStruct((n, KV_DIM), BF16),
            jax.ShapeDtypeStruct((n, KV_DIM), BF16),
            jax.ShapeDtypeStruct((n, SG_WIDTH), F32),
            jax.ShapeDtypeStruct((n, SG_WIDTH), F32),
        ],
        compiler_params=_params(("parallel",)),
        name="even_in",
    )(x2d, mods, mods, g, w_bf, cos, sa, sb)


def _gelu(x):
    return 0.5 * x * (1.0 + lax.erf(x * (2.0 ** -0.5)))


def _even_mix_kernel(sink_ref, lat_ref, qx_ref, km_ref, kp_ref, kn_ref, vm_ref, vp_ref, vn_ref,
                     kc_ref, vc_ref, u_ref, z_ref, gsgu_ref, wsp_ref, bsp_ref, wout_ref, gate_ref,
                     sh2_ref, sc2_ref, g2_ref, wr_ref, br_ref,
                     o_ref, h_ref, mi_ref, wc_ref, cnt_ref,
                     kband, vband, mixin, carry, upper, *, tiles_per_seq, tiles_per_part):
    i = pl.program_id(0)
    tq = qx_ref.shape[0]

    @pl.when(i == 0)
    def _():
        _route_init(carry, upper)

    @pl.when(i % tiles_per_part == 0)
    def _():
        carry[...] = jnp.zeros_like(carry)

    blk = ATT_BLOCK
    nsub = tq // blk
    n_ctx = kc_ref.shape[0]
    first = (i % tiles_per_seq) == 0
    last = (i % tiles_per_seq) == tiles_per_seq - 1

    kband[0:blk] = kp_ref[...]
    kband[blk:blk + tq] = km_ref[...]
    kband[blk + tq:] = kn_ref[...]
    vband[0:blk] = vp_ref[...]
    vband[blk:blk + tq] = vm_ref[...]
    vband[blk + tq:] = vn_ref[...]

    rows = N_Q_HEADS * blk
    tok = lax.broadcasted_iota(jnp.int32, (rows, blk), 0) & (blk - 1)
    col = lax.broadcasted_iota(jnp.int32, (rows, blk), 1)
    tri_prev = col >= tok
    tri_next = col <= tok
    head = lax.broadcasted_iota(jnp.int32, (rows, 1), 0) // blk
    sink_col = jnp.zeros((rows, 1), F32)
    for hd in range(N_Q_HEADS):
        sink_col = jnp.where(head == hd, sink_ref[hd], sink_col)
    lane_low = lax.broadcasted_iota(jnp.int32, (blk, LANES), 1) < HEAD_DIM
    ones = jnp.ones((n_ctx + 3 * blk, LANES), BF16)
    nt = (((1,), (1,)), ((), ()))

    def sub_block(j, c):
        r0 = pl.multiple_of(j * blk, blk)
        ok_prev = jnp.logical_not(jnp.logical_and(first, j == 0))
        ok_next = jnp.logical_not(jnp.logical_and(last, j == nsub - 1))
        qs = jnp.concatenate([qx_ref[pl.ds(r0, blk), hd * LANES:(hd + 1) * LANES]
                              for hd in range(N_Q_HEADS)], axis=0)
        kall = jnp.concatenate([kc_ref[...], kband[pl.ds(r0, 3 * blk), :]], axis=0)
        vall = jnp.concatenate([vc_ref[...], vband[pl.ds(r0, 3 * blk), :]], axis=0)
        s = lax.dot_general(qs, kall, nt, preferred_element_type=F32)
        c0 = n_ctx
        s = jnp.concatenate([
            s[:, :c0],
            jnp.where(jnp.logical_and(tri_prev, ok_prev), s[:, c0:c0 + blk], NEG_INF),
            s[:, c0 + blk:c0 + 2 * blk],
            jnp.where(jnp.logical_and(tri_next, ok_next), s[:, c0 + 2 * blk:], NEG_INF),
        ], axis=1)
        m = jnp.maximum(jnp.max(s, axis=-1, keepdims=True), sink_col)
        p = jnp.exp(s - m).astype(BF16)
        o = jnp.dot(p, jnp.concatenate([vall, ones], axis=1), preferred_element_type=F32)
        att = o[:, :LANES] / (o[:, LANES:] + jnp.exp(sink_col - m))
        half = N_Q_HEADS // 2
        for hd in range(half):
            pair = jnp.where(lane_low, att[hd * blk:(hd + 1) * blk], att[(hd + half) * blk:(hd + half + 1) * blk])
            mixin[pl.ds(r0, blk), hd * LANES:(hd + 1) * LANES] = pair.astype(BF16)

        ug = _gelu(u_ref[pl.ds(r0, blk), :])
        zg = _gelu(z_ref[pl.ds(r0, blk), :])
        mu = jnp.mean(zg, axis=-1, keepdims=True)
        zc = zg - mu
        zn = zc * lax.rsqrt(jnp.mean(zc * zc, axis=-1, keepdims=True) + EPS) * gsgu_ref[...]
        for pair in range(SG_GROUPS // 2):
            zp = zn[:, pair * LANES:(pair + 1) * LANES]
            zero = jnp.zeros_like(zp)
            lo = jnp.where(lane_low, zp, zero).astype(BF16)
            hi = jnp.where(lane_low, zero, zp).astype(BF16)
            sg = (jnp.dot(wsp_ref[2 * pair], lo, preferred_element_type=F32)
                  + jnp.dot(wsp_ref[2 * pair + 1], hi, preferred_element_type=F32)
                  + bsp_ref[:, pair * LANES:(pair + 1) * LANES])
            mixin[pl.ds(r0, blk), Q_DIM + pair * LANES:Q_DIM + (pair + 1) * LANES] = (
                ug[:, pair * LANES:(pair + 1) * LANES] * sg).astype(BF16)
        return c

    lax.fori_loop(0, nsub, sub_block, 0, unroll=True)
    mix = jnp.dot(mixin[...], wout_ref[...], preferred_element_type=F32)
    lat = lat_ref[...] + gate_ref[...] * mix
    o_ref[...] = lat
    _route_tile(lat, sh2_ref[...], sc2_ref[...], g2_ref[...], wr_ref, br_ref, carry, upper,
                h_ref, mi_ref, wc_ref, cnt_ref)


def _even_mix(lat, qx, k, v, kc, vc, u, z, sink, g_sgu, wsp_bf, bsp_full, wout_bf, mods, seq, ctx_len,
              g2, wr_t, br_t):
    n, d = lat.shape
    tq = ATT_TQ
    tiles_per_seq = seq // tq
    sub = tq // ATT_BLOCK
    nblk = n // ATT_BLOCK
    main = lambda w: pl.BlockSpec((tq, w), lambda i: (i, 0))
    prev = pl.BlockSpec((ATT_BLOCK, KV_DIM), lambda i: (jnp.maximum(i * sub - 1, 0), 0))
    nxt = pl.BlockSpec((ATT_BLOCK, KV_DIM), lambda i: (jnp.minimum((i + 1) * sub, nblk - 1), 0))
    ctxs = pl.BlockSpec((ctx_len, KV_DIM), lambda i: (i // tiles_per_seq, 0))
    const = lambda shape: pl.BlockSpec(shape, lambda i: (0,) * len(shape), pipeline_mode=pl.Buffered(1))
    r_in, r_out, r_shapes, r_scratch = _route_specs(n, d, tq, lambda i: i, lambda t: t // tiles_per_seq)
    return pl.pallas_call(
        functools.partial(_even_mix_kernel, tiles_per_seq=tiles_per_seq,
                          tiles_per_part=(n // MOE_PARTS) // tq),
        grid=(n // tq,),
        in_specs=[
            pl.BlockSpec(memory_space=pltpu.SMEM),
            main(d), main(qx.shape[1]),
            main(KV_DIM), prev, nxt,
            main(KV_DIM), prev, nxt,
            ctxs, ctxs,
            main(SG_WIDTH), main(SG_WIDTH),
            const((1, SG_WIDTH)), const(wsp_bf.shape), const(bsp_full.shape), const(wout_bf.shape),
            pl.BlockSpec((None, 1, d), lambda i: (i // tiles_per_seq, 0, 2)),
        ] + r_in,
        out_specs=[pl.BlockSpec((tq, d), lambda i: (i, 0))] + r_out,
        out_shape=[jax.ShapeDtypeStruct((n, d), F32)] + r_shapes,
        scratch_shapes=[
            pltpu.VMEM((tq + 2 * ATT_BLOCK, KV_DIM), BF16),
            pltpu.VMEM((tq + 2 * ATT_BLOCK, KV_DIM), BF16),
            pltpu.VMEM((tq, Q_DIM + SG_WIDTH), BF16),
        ] + r_scratch,
        compiler_params=_params(("arbitrary",)),
        name="even_mix",
    )(sink, lat, qx, k, k, k, v, v, v, kc, vc, u, z, g_sgu, wsp_bf, bsp_full, wout_bf, mods,
      mods, mods, g2, wr_t, br_t)


def _odd_mix_kernel(x_ref, xp_ref, sh_ref, sc_ref, gate_ref, g_ref, win_ref, cw_ref, wout_ref,
                    sh2_ref, sc2_ref, g2_ref, wr_ref, br_ref,
                    o_ref, h_ref, mi_ref, wc_ref, cnt_ref,
                    y_s, bg_s, tail_s, carry, upper, *, tiles_per_seq, tiles_per_part):
    i = pl.program_id(0)
    n_tiles = pl.num_programs(0) - 1
    tm, d = x_ref.shape
    cur = i % 2
    prv = 1 - cur

    @pl.when(i == 0)
    def _():
        _route_init(carry, upper)

    tail_s[...] = y_s[cur, tm - 8:tm, :]

    @pl.when(i < n_tiles)
    def _():
        h = _rms_mod(x_ref[...], g_ref[...], sh_ref[...], sc_ref[...])
        p = jnp.dot(h.astype(BF16), win_ref[...], preferred_element_type=F32)
        bg_s[cur] = p[:, 0:d]
        y_s[cur] = p[:, d:2 * d] * p[:, 2 * d:3 * d]

    @pl.when(i >= 1)
    def _():
        t_prev = i - 1
        first = (t_prev % tiles_per_seq) == 0
        last = (t_prev % tiles_per_seq) == tiles_per_seq - 1
        y = y_s[prv]
        left = jnp.where(first, 0.0, tail_s[7:8, :])
        right = jnp.where(last, 0.0, y_s[cur, 0:1, :])
        ridx = lax.broadcasted_iota(jnp.int32, (tm, d), 0)
        y_dn = jnp.where(ridx == 0, left, pltpu.roll(y, 1, 0))
        y_up = jnp.where(ridx == tm - 1, right, pltpu.roll(y, tm - 1, 0))
        conv = y_dn * cw_ref[0:1, :] + y * cw_ref[1:2, :] + y_up * cw_ref[2:3, :]
        mix = jnp.dot((bg_s[prv] * conv).astype(BF16), wout_ref[...], preferred_element_type=F32)
        lat = xp_ref[...] + gate_ref[...] * mix
        o_ref[...] = lat

        @pl.when(t_prev % tiles_per_part == 0)
        def _():
            carry[...] = jnp.zeros_like(carry)

        _route_tile(lat, sh2_ref[...], sc2_ref[...], g2_ref[...], wr_ref, br_ref, carry, upper,
                    h_ref, mi_ref, wc_ref, cnt_ref)


def _odd_mix(lat, mods, layer, g, win_bf, conv_w8, wout_bf, seq, g2, wr_t, br_t):
    n, d = lat.shape
    tm = ODD_TM
    nt = n // tm
    tiles_per_seq = seq // tm
    cur = lambda i: jnp.minimum(i, nt - 1)
    prv = lambda i: jnp.maximum(i - 1, 0)
    row = lambda t: layer * MOD_ROWS + t // tiles_per_seq
    const = lambda shape: pl.BlockSpec(shape, lambda i: (0,) * len(shape), pipeline_mode=pl.Buffered(1))
    r_in, r_out, r_shapes, r_scratch = _route_specs(n, d, tm, prv, row)
    return pl.pallas_call(
        functools.partial(_odd_mix_kernel, tiles_per_seq=tiles_per_seq,
                          tiles_per_part=(n // MOE_PARTS) // tm),
        grid=(nt + 1,),
        in_specs=[
            pl.BlockSpec((tm, d), lambda i: (cur(i), 0)),
            pl.BlockSpec((tm, d), lambda i: (prv(i), 0)),
            pl.BlockSpec((None, 1, d), lambda i: (row(cur(i)), 0, 0)),
            pl.BlockSpec((None, 1, d), lambda i: (row(cur(i)), 0, 1)),
            pl.BlockSpec((None, 1, d), lambda i: (row(prv(i)), 0, 2)),
            const((1, d)), const(win_bf.shape), const(conv_w8.shape), const(wout_bf.shape),
        ] + r_in,
        out_specs=[pl.BlockSpec((tm, d), lambda i: (prv(i), 0))] + r_out,
        out_shape=[jax.ShapeDtypeStruct((n, d), F32)] + r_shapes,
        scratch_shapes=[
            pltpu.VMEM((2, tm, d), F32),
            pltpu.VMEM((2, tm, d), F32),
            pltpu.VMEM((8, d), F32),
        ] + r_scratch,
        compiler_params=_params(("arbitrary",)),
        name="odd_mix",
    )(lat, lat, mods, mods, mods, g, win_bf, conv_w8, wout_bf, mods, mods, g2, wr_t, br_t)


def _route_init(carry, upper):
    tm = upper.shape[0]
    carry[...] = jnp.zeros_like(carry)
    r_i = lax.broadcasted_iota(jnp.int32, (tm, tm), 0)
    c_i = lax.broadcasted_iota(jnp.int32, (tm, tm), 1)
    upper[...] = jnp.where(r_i < c_i, 1.0, 0.0).astype(BF16)


def _route_tile(lat, sh, sc, g, wr_ref, br_ref, carry, upper, h_ref, mi_ref, wc_ref, cnt_ref):
    tm = lat.shape[0]
    epg = EXPERTS_PER_GROUP
    h = _rms_mod(lat, g, sh, sc)
    h_hi = h.astype(BF16)
    h_hi_f = h_hi.astype(F32)
    h_ref[...] = _pack_rounded(h_hi_f)
    h_lo = (h - h_hi_f).astype(BF16)
    w = wr_ref[...]
    w1 = w.astype(BF16).astype(F32)
    r1 = w - w1
    w2 = r1.astype(BF16).astype(F32)
    w3 = r1 - w2
    nt = (((1,), (1,)), ((), ()))
    nr = w.shape[0]
    w123 = jnp.concatenate([w1, w2, w3, jnp.zeros((8, w.shape[1]), F32)], axis=0).astype(BF16)
    w12 = jnp.concatenate([w1, w2], axis=0).astype(BF16)
    p_hi = lax.dot_general(w123, h_hi, nt, preferred_element_type=F32)
    p_lo = lax.dot_general(w12, h_lo, nt, preferred_element_type=F32)
    lg = ((p_hi[2 * nr:3 * nr] + p_lo[nr:2 * nr]) + (p_hi[nr:2 * nr] + p_lo[0:nr])) + p_hi[0:nr] + br_ref[...]
    io8 = lax.broadcasted_iota(jnp.int32, (epg, tm), 0)
    gl = lg[0:epg]
    gmax = jnp.max(gl, axis=0, keepdims=True)
    g_idx = jnp.min(jnp.where(gl == gmax, io8, epg), axis=0, keepdims=True)
    g_w = 1.0 / jnp.sum(jnp.exp(gl - gmax), axis=0, keepdims=True)
    e_sel = lg[epg:2 * epg]
    for gi in range(1, N_GROUPS):
        e_sel = jnp.where(g_idx == gi, lg[(gi + 1) * epg:(gi + 2) * epg], e_sel)
    v0 = jnp.max(e_sel, axis=0, keepdims=True)
    i0 = jnp.min(jnp.where(e_sel == v0, io8, epg), axis=0, keepdims=True)
    rest = jnp.where(io8 == i0, -jnp.inf, e_sel)
    v1 = jnp.max(rest, axis=0, keepdims=True)
    i1 = jnp.min(jnp.where(rest == v1, io8, epg), axis=0, keepdims=True)
    t = jnp.exp(v1 - v0)
    w0 = g_w / (1.0 + t)
    w1 = g_w * t / (1.0 + t)
    e0 = g_idx * epg + i0
    e1 = g_idx * epg + i1

    io32 = lax.broadcasted_iota(jnp.int32, (N_EXPERTS, tm), 0)
    hit0 = io32 == e0
    hit1 = io32 == e1
    onehot = jnp.where(hit0 | hit1, 1.0, 0.0)
    cum = jnp.dot(onehot.astype(BF16), upper[...], preferred_element_type=F32) + carry[...]
    rank0 = jnp.sum(jnp.where(hit0, cum, 0.0), axis=0, keepdims=True).astype(jnp.int32)
    rank1 = jnp.sum(jnp.where(hit1, cum, 0.0), axis=0, keepdims=True).astype(jnp.int32)
    carry[...] = carry[...] + jnp.sum(onehot, axis=1, keepdims=True)
    cnt_ref[...] = jnp.broadcast_to(carry[...], cnt_ref.shape)

    mi_ref[...] = jnp.where(io8 == 0, e0, jnp.where(io8 == 1, e1, jnp.where(io8 == 2, rank0,
                            jnp.where(io8 == 3, rank1, 0))))
    io128 = lax.broadcasted_iota(jnp.int32, (LANES, tm), 0)
    wrow = jnp.where(io128 == 0, w0, jnp.where(io128 == 1, w1, 0.0))
    wc_ref[...] = wrow.T


def _route_specs(n, d, tm, tile_of, mods_row):
    tiles_per_part = (n // MOE_PARTS) // tm
    in_specs = [
        pl.BlockSpec((None, 1, d), lambda i: (mods_row(tile_of(i)), 0, 3)),
        pl.BlockSpec((None, 1, d), lambda i: (mods_row(tile_of(i)), 0, 4)),
        pl.BlockSpec((1, d), lambda i: (0, 0), pipeline_mode=pl.Buffered(1)),
        pl.BlockSpec((ROUTER_ROWS, d), lambda i: (0, 0), pipeline_mode=pl.Buffered(1)),
        pl.BlockSpec((ROUTER_ROWS, 1), lambda i: (0, 0), pipeline_mode=pl.Buffered(1)),
    ]
    out_specs = [
        pl.BlockSpec((tm, d // 2), lambda i: (tile_of(i), 0)),
        pl.BlockSpec((8, tm), lambda i: (0, tile_of(i))),
        pl.BlockSpec((tm, LANES), lambda i: (tile_of(i), 0)),
        pl.BlockSpec((N_EXPERTS, LANES), lambda i: (tile_of(i) // tiles_per_part, 0)),
    ]
    out_shapes = [
        jax.ShapeDtypeStruct((n, d // 2), jnp.uint32),
        jax.ShapeDtypeStruct((8, n), jnp.int32),
        jax.ShapeDtypeStruct((n, LANES), F32),
        jax.ShapeDtypeStruct((MOE_PARTS * N_EXPERTS, LANES), F32),
    ]
    scratch = [pltpu.VMEM((N_EXPERTS, 1), F32), pltpu.VMEM((tm, tm), BF16)]
    return in_specs, out_specs, out_shapes, scratch


def _plan_kernel(cnt_ref, mi_ref, dest_ref, be_ref, runs_ref, nv_ref, nu_ref, ps_ref, *, n_blocks, tok0):
    bm = MOE_BM

    def per_expert(e, carry):
        blk0, n_runs = carry
        cnt = cnt_ref[e]
        nb = (cnt + bm - 1) // bm
        ps_ref[e] = blk0 * bm

        def fill(b, c):
            be_ref[b] = e
            nv_ref[b] = jnp.minimum(cnt - (b - blk0) * bm, bm)
            return c

        lax.fori_loop(blk0, blk0 + nb, fill, 0)

        @pl.when(nb > 0)
        def _():
            runs_ref[n_runs] = e

        return blk0 + nb, n_runs + jnp.where(nb > 0, 1, 0)

    n_used, n_runs = lax.fori_loop(0, N_EXPERTS, per_expert, (0, 0))
    nu_ref[0] = n_used
    nu_ref[1] = n_runs
    last_e = be_ref[jnp.maximum(n_used - 1, 0)]

    def fill_tail(b, c):
        be_ref[b] = last_e
        nv_ref[b] = 0
        return c

    lax.fori_loop(n_used, n_blocks, fill_tail, 0)

    def fill_runs(k, c):
        runs_ref[k] = last_e
        return c

    lax.fori_loop(n_runs, N_EXPERTS, fill_runs, 0)

    part = dest_ref.shape[1]
    e01 = mi_ref[0:2, tok0:tok0 + part]
    dest = mi_ref[2:4, tok0:tok0 + part]
    for e in range(N_EXPERTS):
        dest = dest + jnp.where(e01 == e, ps_ref[e], 0)
    dest_ref[...] = dest


def _plan(counts, meta_i, n_blocks, tok0, n):
    return pl.pallas_call(
        functools.partial(_plan_kernel, n_blocks=n_blocks, tok0=tok0),
        in_specs=[pl.BlockSpec(memory_space=pltpu.SMEM), pl.BlockSpec(memory_space=pltpu.VMEM)],
        out_specs=[pl.BlockSpec(memory_space=pltpu.VMEM)] + [pl.BlockSpec(memory_space=pltpu.SMEM)] * 4,
        out_shape=[
            jax.ShapeDtypeStruct((2, n), jnp.int32),
            jax.ShapeDtypeStruct((n_blocks,), jnp.int32),
            jax.ShapeDtypeStruct((N_EXPERTS,), jnp.int32),
            jax.ShapeDtypeStruct((n_blocks,), jnp.int32),
            jax.ShapeDtypeStruct((2,), jnp.int32),
        ],
        scratch_shapes=[pltpu.SMEM((N_EXPERTS,), jnp.int32)],
        compiler_params=pltpu.CompilerParams(vmem_limit_bytes=VMEM_LIMIT),
        name="plan",
    )(counts, meta_i)


def _sc_mesh():
    return plsc.VectorSubcoreMesh(core_axis_name="c", subcore_axis_name="s",
                                  num_cores=SC_CORES, num_subcores=SC_SUBCORES)


def _sc_worker():
    return lax.axis_index("s") * SC_CORES + lax.axis_index("c")


def _sc_dispatch(h2, dest, n_rows, tok0):
    n, d = dest.shape[1], h2.shape[1]
    c = SC_CHUNK
    per_w = n // SC_WORKERS
    nchunk = per_w // c
    idx = dest.reshape(2, SC_WORKERS, nchunk, c)

    @functools.partial(
        pl.kernel, mesh=_sc_mesh(), out_type=jax.ShapeDtypeStruct((n_rows, d), h2.dtype),
        scratch_types=[pltpu.VMEM((nchunk, c), jnp.int32), pltpu.VMEM((nchunk, c), jnp.int32),
                       pltpu.VMEM((2, c, d), h2.dtype),
                       pltpu.SemaphoreType.DMA((2,)), pltpu.SemaphoreType.DMA((2,))])
    def k(h_hbm, idx_hbm, xb_hbm, idx0_v, idx1_v, rows_v, gsem, ssem):
        wid = _sc_worker()
        base = tok0 + wid * per_w
        idx_v = (idx0_v, idx1_v)
        for kk in range(2):
            pltpu.sync_copy(idx_hbm.at[kk, wid], idx_v[kk])

        def get(j, slot):
            return pltpu.make_async_copy(h_hbm.at[pl.ds(base + j * c, c)], rows_v.at[slot], gsem.at[slot])

        def put(j, slot, kk):
            return pltpu.make_async_copy(rows_v.at[slot], xb_hbm.at[idx_v[kk].at[j]], ssem.at[slot])

        get(0, 0).start()

        @pl.loop(0, nchunk, step=2)
        def _(j):
            for slot in range(2):
                jj = j + slot
                get(jj, slot).wait()

                @pl.when(jj >= 1)
                def _():
                    for kk in range(2):
                        put(jj - 1, 1 - slot, kk).wait()

                @pl.when(jj + 1 < nchunk)
                def _():
                    get(jj + 1, 1 - slot).start()

                for kk in range(2):
                    put(jj, slot, kk).start()

        for kk in range(2):
            put(nchunk - 1, (nchunk - 1) % 2, kk).wait()

    return k(h2, idx)


def _sc_gather(y, dest):
    d = y.shape[1]
    total = dest.shape[0] * dest.shape[1]
    c = SC_CHUNK
    per_w = total // SC_WORKERS
    nchunk = per_w // c
    idx = dest.reshape(SC_WORKERS, nchunk, c)

    @functools.partial(
        pl.kernel, mesh=_sc_mesh(), out_type=jax.ShapeDtypeStruct((total, d), y.dtype),
        scratch_types=[pltpu.VMEM((nchunk, c), jnp.int32), pltpu.VMEM((2, c, d), y.dtype),
                       pltpu.SemaphoreType.DMA((2,)), pltpu.SemaphoreType.DMA((2,))])
    def k(y_hbm, idx_hbm, out_hbm, idx_v, rows_v, gsem, ssem):
        wid = _sc_worker()
        base = wid * per_w
        pltpu.sync_copy(idx_hbm.at[wid], idx_v)

        def get(j, slot):
            return pltpu.make_async_copy(y_hbm.at[idx_v.at[j]], rows_v.at[slot], gsem.at[slot])

        def put(j, slot):
            return pltpu.make_async_copy(rows_v.at[slot], out_hbm.at[pl.ds(base + j * c, c)], ssem.at[slot])

        get(0, 0).start()

        @pl.loop(0, nchunk, step=2)
        def _(j):
            for slot in range(2):
                jj = j + slot
                get(jj, slot).wait()

                @pl.when(jj >= 1)
                def _():
                    put(jj - 1, 1 - slot).wait()

                @pl.when(jj + 1 < nchunk)
                def _():
                    get(jj + 1, 1 - slot).start()

                put(jj, slot).start()

        put(nchunk - 1, (nchunk - 1) % 2).wait()

    return k(y, idx)


def _expert_kernel(be_ref, runs_ref, nv_ref, nu_ref, x_ref, wg_hbm, wu_hbm, wd_hbm, y_ref,
                   wgu_s, wd_s, stg_g, stg_u, stg_d, run_s, sems, *, layer):
    b = pl.program_id(0)
    hid = stg_g.shape[2]
    e = be_ref[b]
    n_runs = nu_ref[1]
    changed = jnp.logical_or(b == 0, e != be_ref[jnp.maximum(b - 1, 0)])

    def fetch(run):
        expert = runs_ref[run]
        slot = run % WEIGHT_SLOTS
        return (pltpu.make_async_copy(wg_hbm.at[layer, expert], stg_g.at[slot], sems.at[slot]),
                pltpu.make_async_copy(wu_hbm.at[layer, expert], stg_u.at[slot], sems.at[slot]),
                pltpu.make_async_copy(wd_hbm.at[layer, expert], stg_d.at[slot], sems.at[slot]))

    @pl.when(b == 0)
    def _():
        for r in range(WEIGHT_SLOTS - 1):
            @pl.when(r < n_runs)
            def _():
                for cp in fetch(r):
                    cp.start()

    @pl.when(changed)
    def _():
        run = jnp.where(b == 0, 0, run_s[0] + 1)
        run_s[0] = run
        for cp in fetch(run):
            cp.wait()

        ahead = run + WEIGHT_SLOTS - 1

        @pl.when(ahead < n_runs)
        def _():
            for cp in fetch(ahead):
                cp.start()

        slot = run % WEIGHT_SLOTS
        wgu_s[:, 0:hid] = stg_g[slot].astype(BF16)
        wgu_s[:, hid:2 * hid] = stg_u[slot].astype(BF16)
        wd_s[...] = stg_d[slot].astype(BF16)

    bm, dp = x_ref.shape
    nv = nv_ref[b]
    in_use = b < nu_ref[0]

    def run(rows):
        live = lax.broadcasted_iota(jnp.int32, (rows, dp), 0) < nv
        x = _unpack_rows(jnp.where(live, x_ref[0:rows, :], jnp.uint32(0)))
        gu = jnp.dot(x.astype(BF16), wgu_s[...], preferred_element_type=F32)
        gate = gu[:, 0:hid]
        act = gate * (1.0 / (1.0 + jnp.exp(-gate))) * gu[:, hid:2 * hid]
        y_ref[0:rows, :] = _pack_rows(jnp.dot(act.astype(BF16), wd_s[...], preferred_element_type=F32))

    n_quanta = bm // MOE_QUANTUM
    for q in range(1, n_quanta + 1):
        rows = q * MOE_QUANTUM

        @pl.when(jnp.logical_and(in_use, jnp.logical_and(nv > rows - MOE_QUANTUM, nv <= rows)))
        def _(rows=rows):
            run(rows)
            if rows < bm:
                y_ref[rows:bm, :] = jnp.zeros((bm - rows, dp), y_ref.dtype)

    @pl.when(jnp.logical_not(in_use))
    def _():
        y_ref[...] = jnp.zeros_like(y_ref)


def _experts(block_e, runs, n_valid, n_used, xb, w_gate, w_up, w_down, layer):
    n_rows, dp = xb.shape
    d, hid = w_gate.shape[2], w_gate.shape[3]
    bm = MOE_BM
    n_blocks = n_rows // bm
    hbm = pl.BlockSpec(memory_space=pl.ANY)
    return pl.pallas_call(
        functools.partial(_expert_kernel, layer=layer),
        grid_spec=pltpu.PrefetchScalarGridSpec(
            num_scalar_prefetch=4,
            grid=(n_blocks,),
            in_specs=[
                pl.BlockSpec((bm, dp), lambda b, be, nx, nv, nu: (jnp.minimum(b, nu[0] - 1), 0)),
                hbm, hbm, hbm,
            ],
            out_specs=pl.BlockSpec((bm, dp), lambda b, be, nx, nv, nu: (b, 0)),
            scratch_shapes=[
                pltpu.VMEM((d, 2 * hid), BF16), pltpu.VMEM((hid, d), BF16),
                pltpu.VMEM((WEIGHT_SLOTS, d, hid), F32), pltpu.VMEM((WEIGHT_SLOTS, d, hid), F32),
                pltpu.VMEM((WEIGHT_SLOTS, hid, d), F32),
                pltpu.SMEM((1,), jnp.int32), pltpu.SemaphoreType.DMA((WEIGHT_SLOTS,)),
            ],
        ),
        out_shape=jax.ShapeDtypeStruct((n_rows, dp), jnp.uint32),
        compiler_params=_params(("arbitrary",)),
        name="experts",
    )(block_e, runs, n_valid, n_used, xb, w_gate, w_up, w_down)


def _combine_kernel(lat_ref, y0_ref, y1_ref, wc_ref, gate_ref, gf_ref, *rest, final):
    o_ref = rest[-1]
    wc = wc_ref[...]
    moe = wc[:, 0:1] * _unpack_rows(y0_ref[...]) + wc[:, 1:2] * _unpack_rows(y1_ref[...])
    out = lat_ref[...] + gate_ref[...] * moe
    if final:
        ms = jnp.mean(out * out, axis=-1, keepdims=True)
        out = out * lax.rsqrt(ms + EPS) * gf_ref[...]
    o_ref[...] = out


def _combine(lat, yg, wcol, mods, layer, g_final, seq, final, tok0, prev_out):
    n, d = lat.shape
    tm = COMBINE_TM
    nt = yg.shape[0] // (2 * tm)
    t0 = tok0 // tm
    tiles_per_seq = seq // tm
    row = lambda i: layer * MOD_ROWS + (t0 + i) // tiles_per_seq
    in_specs = [
        pl.BlockSpec((tm, d), lambda i: (t0 + i, 0)),
        pl.BlockSpec((tm, d // 2), lambda i: (i, 0)),
        pl.BlockSpec((tm, d // 2), lambda i: (nt + i, 0)),
        pl.BlockSpec((tm, LANES), lambda i: (t0 + i, 0)),
        pl.BlockSpec((None, 1, d), lambda i: (row(i), 0, 5)),
        pl.BlockSpec((1, d), lambda i: (0, 0)),
    ]
    args = [lat, yg, yg, wcol, mods, g_final]
    aliases = {}
    if prev_out is not None:
        in_specs.append(pl.BlockSpec(memory_space=pl.ANY))
        args.append(prev_out)
        aliases = {len(args) - 1: 0}
    return pl.pallas_call(
        functools.partial(_combine_kernel, final=final),
        grid=(nt,),
        in_specs=in_specs,
        out_specs=pl.BlockSpec((tm, d), lambda i: (t0 + i, 0)),
        out_shape=jax.ShapeDtypeStruct((n, d), F32),
        input_output_aliases=aliases,
        compiler_params=_params(("parallel",)),
        name="combine",
    )(*args)


def _moe(lat, routed, mods, layer, w_gate, w_up, w_down, g_final, seq, final):
    h2, meta_i, wcol, counts = routed
    n, d = lat.shape
    part = n // MOE_PARTS
    n_blocks = (2 * part) // MOE_BM + N_EXPERTS
    out = None
    for p in range(MOE_PARTS):
        tok0 = p * part
        cnt = counts[p * N_EXPERTS:(p + 1) * N_EXPERTS, 0].astype(jnp.int32)
        dest, block_e, runs, n_valid, n_used = _plan(cnt, meta_i, n_blocks, tok0, part)
        xb = _sc_dispatch(h2, dest, n_blocks * MOE_BM, tok0)
        yb = _experts(block_e, runs, n_valid, n_used, xb, w_gate, w_up, w_down, layer)
        yg = _sc_gather(yb, dest)
        out = _combine(lat, yg, wcol, mods, layer, g_final, seq, final, tok0, out)
    return out


def _rope_tables(seq):
    quarter = HEAD_DIM // 4
    pos = jnp.arange(seq, dtype=F32)
    row_ids = jnp.floor(pos / GRID_W)
    col_ids = pos - row_ids * GRID_W
    inv = ROPE_BASE ** (-jnp.arange(quarter, dtype=F32) / quarter)
    ang_r = row_ids[:, None] * inv
    ang_c = col_ids[:, None] * inv
    zero = jnp.zeros_like(ang_r)
    cos = jnp.concatenate([jnp.cos(ang_r), jnp.cos(ang_r), jnp.cos(ang_c), jnp.cos(ang_c)], axis=-1)
    sa = jnp.concatenate([-jnp.sin(ang_r), zero, -jnp.sin(ang_c), zero], axis=-1)
    sb = jnp.concatenate([zero, jnp.sin(ang_r), zero, jnp.sin(ang_c)], axis=-1)
    rep = LANES // HEAD_DIM
    return tuple(jnp.tile(t, (1, rep)) for t in (cos, sa, sb))


def _router_weights(w_rg, b_rg, w_re, b_re):
    d = w_rg.shape[0]
    pad = EXPERTS_PER_GROUP - N_GROUPS
    wr_t = jnp.concatenate([w_rg.T, jnp.zeros((pad, d), F32), w_re.T], axis=0)
    br_t = jnp.concatenate([b_rg, jnp.full((pad,), NEG_INF, F32), b_re])[:, None]
    return wr_t, br_t


def kernel(x, c, ctx, c_ctx, w_ada, b_ada, g_norm1, g_norm2, g_final, w_in_even, attn_sink, g_sgu,
           w_spatial, b_spatial, w_out_even, w_in_odd, conv_w, w_out_odd, w_router_group,
           b_router_group, w_router_expert, b_router_expert, w_gate, w_up, w_down):
    b, s, d = x.shape
    n = b * s
    n_ctx = ctx.shape[1]
    depth = w_ada.shape[0]
    assert depth == 2 and b + 1 <= MOD_ROWS

    cond = jnp.concatenate([c, c_ctx[None, :], jnp.zeros((MOD_ROWS - b - 1, d), F32)], axis=0)
    mods = _ada(cond, w_ada, b_ada).reshape(depth * MOD_ROWS, 1, 6 * d)
    gf = g_final[None, :]

    lat = x.reshape(n, d)
    w_in_bf = w_in_even[0].astype(BF16)
    tabs = _rope_tables(s)
    qx, k, v, u, z = _even_in(lat, mods, 0, lambda i: i // (s // EVEN_TM), g_norm1[0][None, :], w_in_bf,
                              tabs, s // EVEN_TM, EVEN_TM)
    ones = jnp.ones((n_ctx, LANES), F32)
    zeros = jnp.zeros((n_ctx, LANES), F32)
    _, kc, vc, _, _ = _even_in(ctx.reshape(b * n_ctx, d), mods, 0, lambda i: b, g_norm1[0][None, :],
                               w_in_bf, (ones, zeros, zeros), 1, n_ctx)
    bsp_full = jnp.repeat(b_spatial[0].T, HEAD_DIM, axis=1)
    half = N_Q_HEADS // 2
    w_att = w_out_even[0][:Q_DIM].reshape(2, half, HEAD_DIM, d).transpose(1, 0, 2, 3).reshape(Q_DIM, d)
    w_out_bf = jnp.concatenate([w_att, w_out_even[0][Q_DIM:]], axis=0).astype(BF16)
    wr_t, br_t = _router_weights(w_router_group[0], b_router_group[0], w_router_expert[0], b_router_expert[0])
    lat, *routed = _even_mix(lat, qx, k, v, kc, vc, u, z, attn_sink[0], g_sgu[0][None, :],
                             w_spatial[0].astype(BF16), bsp_full, w_out_bf, mods, s, n_ctx,
                             g_norm2[0][None, :], wr_t, br_t)
    lat = _moe(lat, routed, mods, 0, w_gate, w_up, w_down, gf, s, False)

    conv_w8 = jnp.concatenate([conv_w[0], jnp.zeros((8 - conv_w.shape[1], d), F32)], axis=0)
    wr_t, br_t = _router_weights(w_router_group[1], b_router_group[1], w_router_expert[1], b_router_expert[1])
    lat, *routed = _odd_mix(lat, mods, 1, g_norm1[1][None, :], w_in_odd[0].astype(BF16), conv_w8,
                            w_out_odd[0].astype(BF16), s, g_norm2[1][None, :], wr_t, br_t)
    out = _moe(lat, routed, mods, 1, w_gate, w_up, w_down, gf, s, True)
    return out.reshape(b, s, d)
```

```python
import functools

import jax
import jax.numpy as jnp
from jax import lax
from jax.experimental import pallas as pl
from jax.experimental.pallas import tpu as pltpu
from jax.experimental.pallas import tpu_sc as plsc

F32 = jnp.float32
BF16 = jnp.bfloat16
HIGHEST = lax.Precision.HIGHEST

GRID_W = 64
N_Q_HEADS = 8
N_KV_HEADS = 2
HEAD_DIM = 64
ATT_BLOCK = 128
ROPE_BASE = 10000.0
Q_DIM = N_Q_HEADS * HEAD_DIM
KV_DIM = N_KV_HEADS * HEAD_DIM
SG_GROUPS = 8
SG_WIDTH = SG_GROUPS * HEAD_DIM
N_GROUPS = 4
EXPERTS_PER_GROUP = 8
N_EXPERTS = N_GROUPS * EXPERTS_PER_GROUP
EPS = 1e-6
NEG_INF = -1e30

LANES = 128
SC_CORES = 2
SC_SUBCORES = 16
SC_WORKERS = SC_CORES * SC_SUBCORES
SC_CHUNK = 32
MOD_ROWS = 8
ROUTER_ROWS = EXPERTS_PER_GROUP + N_EXPERTS
VMEM_LIMIT = 56 * 1024 * 1024

ADA_TN = 1536
EVEN_TM = 1024
EVEN_SUB = 512
ATT_TQ = 512
ODD_TM = 512
MOE_BM = 1024
MOE_QUANTUM = 128
MOE_PARTS = 2
WEIGHT_SLOTS = 3
COMBINE_TM = 512


def _params(sem):
    return pltpu.CompilerParams(dimension_semantics=sem, vmem_limit_bytes=VMEM_LIMIT)


def _rms_mod(x, g, shift, scale):
    ms = jnp.mean(x * x, axis=-1, keepdims=True)
    return (x * lax.rsqrt(ms + EPS)) * (g * (1.0 + scale)) + shift


def _pack_rounded(a):
    w = a.shape[1] // 2
    hi = pltpu.bitcast(a[:, :w], jnp.uint32)
    lo = pltpu.bitcast(a[:, w:], jnp.uint32)
    return hi | (lo >> 16)


def _pack_rows(a):
    return _pack_rounded(a.astype(BF16).astype(F32))


def _unpack_rows(p):
    hi = pltpu.bitcast(p & jnp.uint32(0xFFFF0000), F32)
    lo = pltpu.bitcast(p << 16, F32)
    return jnp.concatenate([hi, lo], axis=1)


def _ada_kernel(a_ref, w_ref, b_ref, o_ref):
    a = a_ref[...]
    s = a * (1.0 / (1.0 + jnp.exp(-a)))
    o_ref[0] = jnp.dot(s, w_ref[0], preferred_element_type=F32, precision=HIGHEST) + b_ref[0]


def _ada(cond, w_ada, b_ada):
    depth, d, six_d = w_ada.shape
    return pl.pallas_call(
        _ada_kernel,
        grid=(depth, six_d // ADA_TN),
        in_specs=[
            pl.BlockSpec((MOD_ROWS, d), lambda l, j: (0, 0)),
            pl.BlockSpec((1, d, ADA_TN), lambda l, j: (l, 0, j)),
            pl.BlockSpec((1, 1, ADA_TN), lambda l, j: (l, 0, j)),
        ],
        out_specs=pl.BlockSpec((1, MOD_ROWS, ADA_TN), lambda l, j: (l, 0, j)),
        out_shape=jax.ShapeDtypeStruct((depth, MOD_ROWS, six_d), F32),
        compiler_params=_params(("arbitrary", "arbitrary")),
        name="ada",
    )(cond, w_ada, b_ada.reshape(depth, 1, six_d))


def _even_in_kernel(x_ref, sh_ref, sc_ref, g_ref, w_ref, cos_ref, sa_ref, sb_ref,
                    qx_ref, k_ref, v_ref, u_ref, z_ref):
    tm = x_ref.shape[0]
    sub = min(tm, EVEN_SUB)
    scale = HEAD_DIM ** -0.5
    low = lax.broadcasted_iota(jnp.int32, (sub, LANES), 1) < HEAD_DIM
    heads_per_kv = N_Q_HEADS // N_KV_HEADS
    u0 = Q_DIM + 2 * KV_DIM

    for r0 in range(0, tm, sub):
        rows = slice(r0, r0 + sub)
        h = _rms_mod(x_ref[rows, :], g_ref[...], sh_ref[...], sc_ref[...])
        p = jnp.dot(h.astype(BF16), w_ref[...], preferred_element_type=F32)
        cos, sa, sb = cos_ref[rows, :], sa_ref[rows, :], sb_ref[rows, :]

        def rope(t):
            return t * cos + pltpu.roll(t, LANES - 16, 1) * sa + pltpu.roll(t, 16, 1) * sb

        for cblk in range(Q_DIM // LANES):
            t = rope(p[:, cblk * LANES:(cblk + 1) * LANES]) * scale
            sw = pltpu.roll(t, HEAD_DIM, 1)
            zero = jnp.zeros_like(t)
            if (2 * cblk) // heads_per_kv == 0:
                first, second = jnp.where(low, t, zero), jnp.where(low, sw, zero)
            else:
                first, second = jnp.where(low, zero, sw), jnp.where(low, zero, t)
            qx_ref[rows, (2 * cblk) * LANES:(2 * cblk + 1) * LANES] = first.astype(BF16)
            qx_ref[rows, (2 * cblk + 1) * LANES:(2 * cblk + 2) * LANES] = second.astype(BF16)

        k_ref[rows, :] = rope(p[:, Q_DIM:Q_DIM + KV_DIM]).astype(BF16)
        v_ref[rows, :] = p[:, Q_DIM + KV_DIM:Q_DIM + 2 * KV_DIM].astype(BF16)
        u_ref[rows, :] = p[:, u0:u0 + SG_WIDTH]
        z_ref[rows, :] = p[:, u0 + SG_WIDTH:u0 + 2 * SG_WIDTH]


def _even_in(x2d, mods, layer, mod_row_fn, g, w_bf, tabs, tab_blocks, tm):
    n, d = x2d.shape
    ein = w_bf.shape[1]
    cos, sa, sb = tabs
    row = lambda i: layer * MOD_ROWS + mod_row_fn(i)
    tab_spec = pl.BlockSpec((tm, LANES), lambda i: (i % tab_blocks, 0))
    qx_dim = N_Q_HEADS * LANES
    return pl.pallas_call(
        _even_in_kernel,
        grid=(n // tm,),
        in_specs=[
            pl.BlockSpec((tm, d), lambda i: (i, 0)),
            pl.BlockSpec((None, 1, d), lambda i: (row(i), 0, 0)),
            pl.BlockSpec((None, 1, d), lambda i: (row(i), 0, 1)),
            pl.BlockSpec((1, d), lambda i: (0, 0)),
            pl.BlockSpec((d, ein), lambda i: (0, 0)),
            tab_spec, tab_spec, tab_spec,
        ],
        out_specs=[
            pl.BlockSpec((tm, qx_dim), lambda i: (i, 0)),
            pl.BlockSpec((tm, KV_DIM), lambda i: (i, 0)),
            pl.BlockSpec((tm, KV_DIM), lambda i: (i, 0)),
            pl.BlockSpec((tm, SG_WIDTH), lambda i: (i, 0)),
            pl.BlockSpec((tm, SG_WIDTH), lambda i: (i, 0)),
        ],
        out_shape=[
            jax.ShapeDtypeStruct((n, qx_dim), BF16),
            jax.ShapeDtypeStruct((n, KV_DIM), BF16),
            jax.ShapeDtypeStruct((n, KV_DIM), BF16),
            jax.ShapeDtypeStruct((n, SG_WIDTH), F32),
            jax.ShapeDtypeStruct((n, SG_WIDTH), F32),
        ],
        compiler_params=_params(("parallel",)),
        name="even_in",
    )(x2d, mods, mods, g, w_bf, cos, sa, sb)


def _gelu(x):
    return 0.5 * x * (1.0 + lax.erf(x * (2.0 ** -0.5)))


def _even_mix_kernel(sink_ref, lat_ref, qx_ref, km_ref, kp_ref, kn_ref, vm_ref, vp_ref, vn_ref,
                     kc_ref, vc_ref, u_ref, z_ref, gsgu_ref, wsp_ref, bsp_ref, wout_ref, gate_ref,
                     sh2_ref, sc2_ref, g2_ref, wr_ref, br_ref,
                     o_ref, h_ref, mi_ref, wc_ref, cnt_ref,
                     kband, vband, mixin, carry, upper, *, tiles_per_seq, tiles_per_part):
    i = pl.program_id(0)
    tq = qx_ref.shape[0]

    @pl.when(i == 0)
    def _():
        _route_init(carry, upper)

    @pl.when(i % tiles_per_part == 0)
    def _():
        carry[...] = jnp.zeros_like(carry)

    blk = ATT_BLOCK
    nsub = tq // blk
    n_ctx = kc_ref.shape[0]
    first = (i % tiles_per_seq) == 0
    last = (i % tiles_per_seq) == tiles_per_seq - 1

    kband[0:blk] = kp_ref[...]
    kband[blk:blk + tq] = km_ref[...]
    kband[blk + tq:] = kn_ref[...]
    vband[0:blk] = vp_ref[...]
    vband[blk:blk + tq] = vm_ref[...]
    vband[blk + tq:] = vn_ref[...]

    rows = N_Q_HEADS * blk
    tok = lax.broadcasted_iota(jnp.int32, (rows, blk), 0) & (blk - 1)
    col = lax.broadcasted_iota(jnp.int32, (rows, blk), 1)
    tri_prev = col >= tok
    tri_next = col <= tok
    head = lax.broadcasted_iota(jnp.int32, (rows, 1), 0) // blk
    sink_col = jnp.zeros((rows, 1), F32)
    for hd in range(N_Q_HEADS):
        sink_col = jnp.where(head == hd, sink_ref[hd], sink_col)
    lane_low = lax.broadcasted_iota(jnp.int32, (blk, LANES), 1) < HEAD_DIM
    ones = jnp.ones((n_ctx + 3 * blk, LANES), BF16)
    nt = (((1,), (1,)), ((), ()))

    def sub_block(j, c):
        r0 = pl.multiple_of(j * blk, blk)
        ok_prev = jnp.logical_not(jnp.logical_and(first, j == 0))
        ok_next = jnp.logical_not(jnp.logical_and(last, j == nsub - 1))
        qs = jnp.concatenate([qx_ref[pl.ds(r0, blk), hd * LANES:(hd + 1) * LANES]
                              for hd in range(N_Q_HEADS)], axis=0)
        kall = jnp.concatenate([kc_ref[...], kband[pl.ds(r0, 3 * blk), :]], axis=0)
        vall = jnp.concatenate([vc_ref[...], vband[pl.ds(r0, 3 * blk), :]], axis=0)
        s = lax.dot_general(qs, kall, nt, preferred_element_type=F32)
        c0 = n_ctx
        s = jnp.concatenate([
            s[:, :c0],
            jnp.where(jnp.logical_and(tri_prev, ok_prev), s[:, c0:c0 + blk], NEG_INF),
            s[:, c0 + blk:c0 + 2 * blk],
            jnp.where(jnp.logical_and(tri_next, ok_next), s[:, c0 + 2 * blk:], NEG_INF),
        ], axis=1)
        m = jnp.maximum(jnp.max(s, axis=-1, keepdims=True), sink_col)
        p = jnp.exp(s - m).astype(BF16)
        o = jnp.dot(p, jnp.concatenate([vall, ones], axis=1), preferred_element_type=F32)
        att = o[:, :LANES] / (o[:, LANES:] + jnp.exp(sink_col - m))
        half = N_Q_HEADS // 2
        for hd in range(half):
            pair = jnp.where(lane_low, att[hd * blk:(hd + 1) * blk], att[(hd + half) * blk:(hd + half + 1) * blk])
            mixin[pl.ds(r0, blk), hd * LANES:(hd + 1) * LANES] = pair.astype(BF16)

        ug = _gelu(u_ref[pl.ds(r0, blk), :])
        zg = _gelu(z_ref[pl.ds(r0, blk), :])
        mu = jnp.mean(zg, axis=-1, keepdims=True)
        zc = zg - mu
        zn = zc * lax.rsqrt(jnp.mean(zc * zc, axis=-1, keepdims=True) + EPS) * gsgu_ref[...]
        for pair in range(SG_GROUPS // 2):
            zp = zn[:, pair * LANES:(pair + 1) * LANES]
            zero = jnp.zeros_like(zp)
            lo = jnp.where(lane_low, zp, zero).astype(BF16)
            hi = jnp.where(lane_low, zero, zp).astype(BF16)
            sg = (jnp.dot(wsp_ref[2 * pair], lo, preferred_element_type=F32)
                  + jnp.dot(wsp_ref[2 * pair + 1], hi, preferred_element_type=F32)
                  + bsp_ref[:, pair * LANES:(pair + 1) * LANES])
            mixin[pl.ds(r0, blk), Q_DIM + pair * LANES:Q_DIM + (pair + 1) * LANES] = (
                ug[:, pair * LANES:(pair + 1) * LANES] * sg).astype(BF16)
        return c

    lax.fori_loop(0, nsub, sub_block, 0, unroll=True)
    mix = jnp.dot(mixin[...], wout_ref[...], preferred_element_type=F32)
    lat = lat_ref[...] + gate_ref[...] * mix
    o_ref[...] = lat
    _route_tile(lat, sh2_ref[...], sc2_ref[...], g2_ref[...], wr_ref, br_ref, carry, upper,
                h_ref, mi_ref, wc_ref, cnt_ref)


def _even_mix(lat, qx, k, v, kc, vc, u, z, sink, g_sgu, wsp_bf, bsp_full, wout_bf, mods, seq, ctx_len,
              g2, wr_t, br_t):
    n, d = lat.shape
    tq = ATT_TQ
    tiles_per_seq = seq // tq
    sub = tq // ATT_BLOCK
    nblk = n // ATT_BLOCK
    main = lambda w: pl.BlockSpec((tq, w), lambda i: (i, 0))
    prev = pl.BlockSpec((ATT_BLOCK, KV_DIM), lambda i: (jnp.maximum(i * sub - 1, 0), 0))
    nxt = pl.BlockSpec((ATT_BLOCK, KV_DIM), lambda i: (jnp.minimum((i + 1) * sub, nblk - 1), 0))
    ctxs = pl.BlockSpec((ctx_len, KV_DIM), lambda i: (i // tiles_per_seq, 0))
    const = lambda shape: pl.BlockSpec(shape, lambda i: (0,) * len(shape), pipeline_mode=pl.Buffered(1))
    r_in, r_out, r_shapes, r_scratch = _route_specs(n, d, tq, lambda i: i, lambda t: t // tiles_per_seq)
    return pl.pallas_call(
        functools.partial(_even_mix_kernel, tiles_per_seq=tiles_per_seq,
                          tiles_per_part=(n // MOE_PARTS) // tq),
        grid=(n // tq,),
        in_specs=[
            pl.BlockSpec(memory_space=pltpu.SMEM),
            main(d), main(qx.shape[1]),
            main(KV_DIM), prev, nxt,
            main(KV_DIM), prev, nxt,
            ctxs, ctxs,
            main(SG_WIDTH), main(SG_WIDTH),
            const((1, SG_WIDTH)), const(wsp_bf.shape), const(bsp_full.shape), const(wout_bf.shape),
            pl.BlockSpec((None, 1, d), lambda i: (i // tiles_per_seq, 0, 2)),
        ] + r_in,
        out_specs=[pl.BlockSpec((tq, d), lambda i: (i, 0))] + r_out,
        out_shape=[jax.ShapeDtypeStruct((n, d), F32)] + r_shapes,
        scratch_shapes=[
            pltpu.VMEM((tq + 2 * ATT_BLOCK, KV_DIM), BF16),
            pltpu.VMEM((tq + 2 * ATT_BLOCK, KV_DIM), BF16),
            pltpu.VMEM((tq, Q_DIM + SG_WIDTH), BF16),
        ] + r_scratch,
        compiler_params=_params(("arbitrary",)),
        name="even_mix",
    )(sink, lat, qx, k, k, k, v, v, v, kc, vc, u, z, g_sgu, wsp_bf, bsp_full, wout_bf, mods,
      mods, mods, g2, wr_t, br_t)


def _odd_mix_kernel(x_ref, xp_ref, sh_ref, sc_ref, gate_ref, g_ref, win_ref, cw_ref, wout_ref,
                    sh2_ref, sc2_ref, g2_ref, wr_ref, br_ref,
                    o_ref, h_ref, mi_ref, wc_ref, cnt_ref,
                    y_s, bg_s, tail_s, carry, upper, *, tiles_per_seq, tiles_per_part):
    i = pl.program_id(0)
    n_tiles = pl.num_programs(0) - 1
    tm, d = x_ref.shape
    cur = i % 2
    prv = 1 - cur

    @pl.when(i == 0)
    def _():
        _route_init(carry, upper)

    tail_s[...] = y_s[cur, tm - 8:tm, :]

    @pl.when(i < n_tiles)
    def _():
        h = _rms_mod(x_ref[...], g_ref[...], sh_ref[...], sc_ref[...])
        p = jnp.dot(h.astype(BF16), win_ref[...], preferred_element_type=F32)
        bg_s[cur] = p[:, 0:d]
        y_s[cur] = p[:, d:2 * d] * p[:, 2 * d:3 * d]

    @pl.when(i >= 1)
    def _():
        t_prev = i - 1
        first = (t_prev % tiles_per_seq) == 0
        last = (t_prev % tiles_per_seq) == tiles_per_seq - 1
        y = y_s[prv]
        left = jnp.where(first, 0.0, tail_s[7:8, :])
        right = jnp.where(last, 0.0, y_s[cur, 0:1, :])
        ridx = lax.broadcasted_iota(jnp.int32, (tm, d), 0)
        y_dn = jnp.where(ridx == 0, left, pltpu.roll(y, 1, 0))
        y_up = jnp.where(ridx == tm - 1, right, pltpu.roll(y, tm - 1, 0))
        conv = y_dn * cw_ref[0:1, :] + y * cw_ref[1:2, :] + y_up * cw_ref[2:3, :]
        mix = jnp.dot((bg_s[prv] * conv).astype(BF16), wout_ref[...], preferred_element_type=F32)
        lat = xp_ref[...] + gate_ref[...] * mix
        o_ref[...] = lat

        @pl.when(t_prev % tiles_per_part == 0)
        def _():
            carry[...] = jnp.zeros_like(carry)

        _route_tile(lat, sh2_ref[...], sc2_ref[...], g2_ref[...], wr_ref, br_ref, carry, upper,
                    h_ref, mi_ref, wc_ref, cnt_ref)


def _odd_mix(lat, mods, layer, g, win_bf, conv_w8, wout_bf, seq, g2, wr_t, br_t):
    n, d = lat.shape
    tm = ODD_TM
    nt = n // tm
    tiles_per_seq = seq // tm
    cur = lambda i: jnp.minimum(i, nt - 1)
    prv = lambda i: jnp.maximum(i - 1, 0)
    row = lambda t: layer * MOD_ROWS + t // tiles_per_seq
    const = lambda shape: pl.BlockSpec(shape, lambda i: (0,) * len(shape), pipeline_mode=pl.Buffered(1))
    r_in, r_out, r_shapes, r_scratch = _route_specs(n, d, tm, prv, row)
    return pl.pallas_call(
        functools.partial(_odd_mix_kernel, tiles_per_seq=tiles_per_seq,
                          tiles_per_part=(n // MOE_PARTS) // tm),
        grid=(nt + 1,),
        in_specs=[
            pl.BlockSpec((tm, d), lambda i: (cur(i), 0)),
            pl.BlockSpec((tm, d), lambda i: (prv(i), 0)),
            pl.BlockSpec((None, 1, d), lambda i: (row(cur(i)), 0, 0)),
            pl.BlockSpec((None, 1, d), lambda i: (row(cur(i)), 0, 1)),
            pl.BlockSpec((None, 1, d), lambda i: (row(prv(i)), 0, 2)),
            const((1, d)), const(win_bf.shape), const(conv_w8.shape), const(wout_bf.shape),
        ] + r_in,
        out_specs=[pl.BlockSpec((tm, d), lambda i: (prv(i), 0))] + r_out,
        out_shape=[jax.ShapeDtypeStruct((n, d), F32)] + r_shapes,
        scratch_shapes=[
            pltpu.VMEM((2, tm, d), F32),
            pltpu.VMEM((2, tm, d), F32),
            pltpu.VMEM((8, d), F32),
        ] + r_scratch,
        compiler_params=_params(("arbitrary",)),
        name="odd_mix",
    )(lat, lat, mods, mods, mods, g, win_bf, conv_w8, wout_bf, mods, mods, g2, wr_t, br_t)


def _route_init(carry, upper):
    tm = upper.shape[0]
    carry[...] = jnp.zeros_like(carry)
    r_i = lax.broadcasted_iota(jnp.int32, (tm, tm), 0)
    c_i = lax.broadcasted_iota(jnp.int32, (tm, tm), 1)
    upper[...] = jnp.where(r_i < c_i, 1.0, 0.0).astype(BF16)


def _route_tile(lat, sh, sc, g, wr_ref, br_ref, carry, upper, h_ref, mi_ref, wc_ref, cnt_ref):
    tm = lat.shape[0]
    epg = EXPERTS_PER_GROUP
    h = _rms_mod(lat, g, sh, sc)
    h_hi = h.astype(BF16)
    h_hi_f = h_hi.astype(F32)
    h_ref[...] = _pack_rounded(h_hi_f)
    h_lo = (h - h_hi_f).astype(BF16)
    w = wr_ref[...]
    w1 = w.astype(BF16).astype(F32)
    r1 = w - w1
    w2 = r1.astype(BF16).astype(F32)
    w3 = r1 - w2
    nt = (((1,), (1,)), ((), ()))
    nr = w.shape[0]
    w123 = jnp.concatenate([w1, w2, w3, jnp.zeros((8, w.shape[1]), F32)], axis=0).astype(BF16)
    w12 = jnp.concatenate([w1, w2], axis=0).astype(BF16)
    p_hi = lax.dot_general(w123, h_hi, nt, preferred_element_type=F32)
    p_lo = lax.dot_general(w12, h_lo, nt, preferred_element_type=F32)
    lg = ((p_hi[2 * nr:3 * nr] + p_lo[nr:2 * nr]) + (p_hi[nr:2 * nr] + p_lo[0:nr])) + p_hi[0:nr] + br_ref[...]
    io8 = lax.broadcasted_iota(jnp.int32, (epg, tm), 0)
    gl = lg[0:epg]
    gmax = jnp.max(gl, axis=0, keepdims=True)
    g_idx = jnp.min(jnp.where(gl == gmax, io8, epg), axis=0, keepdims=True)
    g_w = 1.0 / jnp.sum(jnp.exp(gl - gmax), axis=0, keepdims=True)
    e_sel = lg[epg:2 * epg]
    for gi in range(1, N_GROUPS):
        e_sel = jnp.where(g_idx == gi, lg[(gi + 1) * epg:(gi + 2) * epg], e_sel)
    v0 = jnp.max(e_sel, axis=0, keepdims=True)
    i0 = jnp.min(jnp.where(e_sel == v0, io8, epg), axis=0, keepdims=True)
    rest = jnp.where(io8 == i0, -jnp.inf, e_sel)
    v1 = jnp.max(rest, axis=0, keepdims=True)
    i1 = jnp.min(jnp.where(rest == v1, io8, epg), axis=0, keepdims=True)
    t = jnp.exp(v1 - v0)
    w0 = g_w / (1.0 + t)
    w1 = g_w * t / (1.0 + t)
    e0 = g_idx * epg + i0
    e1 = g_idx * epg + i1

    io32 = lax.broadcasted_iota(jnp.int32, (N_EXPERTS, tm), 0)
    hit0 = io32 == e0
    hit1 = io32 == e1
    onehot = jnp.where(hit0 | hit1, 1.0, 0.0)
    cum = jnp.dot(onehot.astype(BF16), upper[...], preferred_element_type=F32) + carry[...]
    rank0 = jnp.sum(jnp.where(hit0, cum, 0.0), axis=0, keepdims=True).astype(jnp.int32)
    rank1 = jnp.sum(jnp.where(hit1, cum, 0.0), axis=0, keepdims=True).astype(jnp.int32)
    carry[...] = carry[...] + jnp.sum(onehot, axis=1, keepdims=True)
    cnt_ref[...] = jnp.broadcast_to(carry[...], cnt_ref.shape)

    mi_ref[...] = jnp.where(io8 == 0, e0, jnp.where(io8 == 1, e1, jnp.where(io8 == 2, rank0,
                            jnp.where(io8 == 3, rank1, 0))))
    io128 = lax.broadcasted_iota(jnp.int32, (LANES, tm), 0)
    wrow = jnp.where(io128 == 0, w0, jnp.where(io128 == 1, w1, 0.0))
    wc_ref[...] = wrow.T


def _route_specs(n, d, tm, tile_of, mods_row):
    tiles_per_part = (n // MOE_PARTS) // tm
    in_specs = [
        pl.BlockSpec((None, 1, d), lambda i: (mods_row(tile_of(i)), 0, 3)),
        pl.BlockSpec((None, 1, d), lambda i: (mods_row(tile_of(i)), 0, 4)),
        pl.BlockSpec((1, d), lambda i: (0, 0), pipeline_mode=pl.Buffered(1)),
        pl.BlockSpec((ROUTER_ROWS, d), lambda i: (0, 0), pipeline_mode=pl.Buffered(1)),
        pl.BlockSpec((ROUTER_ROWS, 1), lambda i: (0, 0), pipeline_mode=pl.Buffered(1)),
    ]
    out_specs = [
        pl.BlockSpec((tm, d // 2), lambda i: (tile_of(i), 0)),
        pl.BlockSpec((8, tm), lambda i: (0, tile_of(i))),
        pl.BlockSpec((tm, LANES), lambda i: (tile_of(i), 0)),
        pl.BlockSpec((N_EXPERTS, LANES), lambda i: (tile_of(i) // tiles_per_part, 0)),
    ]
    out_shapes = [
        jax.ShapeDtypeStruct((n, d // 2), jnp.uint32),
        jax.ShapeDtypeStruct((8, n), jnp.int32),
        jax.ShapeDtypeStruct((n, LANES), F32),
        jax.ShapeDtypeStruct((MOE_PARTS * N_EXPERTS, LANES), F32),
    ]
    scratch = [pltpu.VMEM((N_EXPERTS, 1), F32), pltpu.VMEM((tm, tm), BF16)]
    return in_specs, out_specs, out_shapes, scratch


def _plan_kernel(cnt_ref, mi_ref, dest_ref, be_ref, runs_ref, nv_ref, nu_ref, ps_ref, *, n_blocks, tok0):
    bm = MOE_BM

    def per_expert(e, carry):
        blk0, n_runs = carry
        cnt = cnt_ref[e]
        nb = (cnt + bm - 1) // bm
        ps_ref[e] = blk0 * bm

        def fill(b, c):
            be_ref[b] = e
            nv_ref[b] = jnp.minimum(cnt - (b - blk0) * bm, bm)
            return c

        lax.fori_loop(blk0, blk0 + nb, fill, 0)

        @pl.when(nb > 0)
        def _():
            runs_ref[n_runs] = e

        return blk0 + nb, n_runs + jnp.where(nb > 0, 1, 0)

    n_used, n_runs = lax.fori_loop(0, N_EXPERTS, per_expert, (0, 0))
    nu_ref[0] = n_used
    nu_ref[1] = n_runs
    last_e = be_ref[jnp.maximum(n_used - 1, 0)]

    def fill_tail(b, c):
        be_ref[b] = last_e
        nv_ref[b] = 0
        return c

    lax.fori_loop(n_used, n_blocks, fill_tail, 0)

    def fill_runs(k, c):
        runs_ref[k] = last_e
        return c

    lax.fori_loop(n_runs, N_EXPERTS, fill_runs, 0)

    part = dest_ref.shape[1]
    e01 = mi_ref[0:2, tok0:tok0 + part]
    dest = mi_ref[2:4, tok0:tok0 + part]
    for e in range(N_EXPERTS):
        dest = dest + jnp.where(e01 == e, ps_ref[e], 0)
    dest_ref[...] = dest


def _plan(counts, meta_i, n_blocks, tok0, n):
    return pl.pallas_call(
        functools.partial(_plan_kernel, n_blocks=n_blocks, tok0=tok0),
        in_specs=[pl.BlockSpec(memory_space=pltpu.SMEM), pl.BlockSpec(memory_space=pltpu.VMEM)],
        out_specs=[pl.BlockSpec(memory_space=pltpu.VMEM)] + [pl.BlockSpec(memory_space=pltpu.SMEM)] * 4,
        out_shape=[
            jax.ShapeDtypeStruct((2, n), jnp.int32),
            jax.ShapeDtypeStruct((n_blocks,), jnp.int32),
            jax.ShapeDtypeStruct((N_EXPERTS,), jnp.int32),
            jax.ShapeDtypeStruct((n_blocks,), jnp.int32),
            jax.ShapeDtypeStruct((2,), jnp.int32),
        ],
        scratch_shapes=[pltpu.SMEM((N_EXPERTS,), jnp.int32)],
        compiler_params=pltpu.CompilerParams(vmem_limit_bytes=VMEM_LIMIT),
        name="plan",
    )(counts, meta_i)


def _sc_mesh():
    return plsc.VectorSubcoreMesh(core_axis_name="c", subcore_axis_name="s",
                                  num_cores=SC_CORES, num_subcores=SC_SUBCORES)


def _sc_worker():
    return lax.axis_index("s") * SC_CORES + lax.axis_index("c")


def _sc_dispatch(h2, dest, n_rows, tok0):
    n, d = dest.shape[1], h2.shape[1]
    c = SC_CHUNK
    per_w = n // SC_WORKERS
    nchunk = per_w // c
    idx = dest.reshape(2, SC_WORKERS, nchunk, c)

    @functools.partial(
        pl.kernel, mesh=_sc_mesh(), out_type=jax.ShapeDtypeStruct((n_rows, d), h2.dtype),
        scratch_types=[pltpu.VMEM((nchunk, c), jnp.int32), pltpu.VMEM((nchunk, c), jnp.int32),
                       pltpu.VMEM((2, c, d), h2.dtype),
                       pltpu.SemaphoreType.DMA((2,)), pltpu.SemaphoreType.DMA((2,))])
    def k(h_hbm, idx_hbm, xb_hbm, idx0_v, idx1_v, rows_v, gsem, ssem):
        wid = _sc_worker()
        base = tok0 + wid * per_w
        idx_v = (idx0_v, idx1_v)
        for kk in range(2):
            pltpu.sync_copy(idx_hbm.at[kk, wid], idx_v[kk])

        def get(j, slot):
            return pltpu.make_async_copy(h_hbm.at[pl.ds(base + j * c, c)], rows_v.at[slot], gsem.at[slot])

        def put(j, slot, kk):
            return pltpu.make_async_copy(rows_v.at[slot], xb_hbm.at[idx_v[kk].at[j]], ssem.at[slot])

        get(0, 0).start()

        @pl.loop(0, nchunk, step=2)
        def _(j):
            for slot in range(2):
                jj = j + slot
                get(jj, slot).wait()

                @pl.when(jj >= 1)
                def _():
                    for kk in range(2):
                        put(jj - 1, 1 - slot, kk).wait()

                @pl.when(jj + 1 < nchunk)
                def _():
                    get(jj + 1, 1 - slot).start()

                for kk in range(2):
                    put(jj, slot, kk).start()

        for kk in range(2):
            put(nchunk - 1, (nchunk - 1) % 2, kk).wait()

    return k(h2, idx)


def _sc_gather(y, dest):
    d = y.shape[1]
    total = dest.shape[0] * dest.shape[1]
    c = SC_CHUNK
    per_w = total // SC_WORKERS
    nchunk = per_w // c
    idx = dest.reshape(SC_WORKERS, nchunk, c)

    @functools.partial(
        pl.kernel, mesh=_sc_mesh(), out_type=jax.ShapeDtypeStruct((total, d), y.dtype),
        scratch_types=[pltpu.VMEM((nchunk, c), jnp.int32), pltpu.VMEM((2, c, d), y.dtype),
                       pltpu.SemaphoreType.DMA((2,)), pltpu.SemaphoreType.DMA((2,))])
    def k(y_hbm, idx_hbm, out_hbm, idx_v, rows_v, gsem, ssem):
        wid = _sc_worker()
        base = wid * per_w
        pltpu.sync_copy(idx_hbm.at[wid], idx_v)

        def get(j, slot):
            return pltpu.make_async_copy(y_hbm.at[idx_v.at[j]], rows_v.at[slot], gsem.at[slot])

        def put(j, slot):
            return pltpu.make_async_copy(rows_v.at[slot], out_hbm.at[pl.ds(base + j * c, c)], ssem.at[slot])

        get(0, 0).start()

        @pl.loop(0, nchunk, step=2)
        def _(j):
            for slot in range(2):
                jj = j + slot
                get(jj, slot).wait()

                @pl.when(jj >= 1)
                def _():
                    put(jj - 1, 1 - slot).wait()

                @pl.when(jj + 1 < nchunk)
                def _():
                    get(jj + 1, 1 - slot).start()

                put(jj, slot).start()

        put(nchunk - 1, (nchunk - 1) % 2).wait()

    return k(y, idx)


def _expert_kernel(be_ref, runs_ref, nv_ref, nu_ref, x_ref, wg_hbm, wu_hbm, wd_hbm, y_ref,
                   wgu_s, wd_s, stg_g, stg_u, stg_d, run_s, sems, *, layer):
    b = pl.program_id(0)
    hid = stg_g.shape[2]
    e = be_ref[b]
    n_runs = nu_ref[1]
    changed = jnp.logical_or(b == 0, e != be_ref[jnp.maximum(b - 1, 0)])

    def fetch(run):
        expert = runs_ref[run]
        slot = run % WEIGHT_SLOTS
        return (pltpu.make_async_copy(wg_hbm.at[layer, expert], stg_g.at[slot], sems.at[slot]),
                pltpu.make_async_copy(wu_hbm.at[layer, expert], stg_u.at[slot], sems.at[slot]),
                pltpu.make_async_copy(wd_hbm.at[layer, expert], stg_d.at[slot], sems.at[slot]))

    @pl.when(b == 0)
    def _():
        for r in range(WEIGHT_SLOTS - 1):
            @pl.when(r < n_runs)
            def _():
                for cp in fetch(r):
                    cp.start()

    @pl.when(changed)
    def _():
        run = jnp.where(b == 0, 0, run_s[0] + 1)
        run_s[0] = run
        for cp in fetch(run):
            cp.wait()

        ahead = run + WEIGHT_SLOTS - 1

        @pl.when(ahead < n_runs)
        def _():
            for cp in fetch(ahead):
                cp.start()

        slot = run % WEIGHT_SLOTS
        wgu_s[:, 0:hid] = stg_g[slot].astype(BF16)
        wgu_s[:, hid:2 * hid] = stg_u[slot].astype(BF16)
        wd_s[...] = stg_d[slot].astype(BF16)

    bm, dp = x_ref.shape
    nv = nv_ref[b]
    in_use = b < nu_ref[0]

    def run(rows):
        live = lax.broadcasted_iota(jnp.int32, (rows, dp), 0) < nv
        x = _unpack_rows(jnp.where(live, x_ref[0:rows, :], jnp.uint32(0)))
        gu = jnp.dot(x.astype(BF16), wgu_s[...], preferred_element_type=F32)
        gate = gu[:, 0:hid]
        act = gate * (1.0 / (1.0 + jnp.exp(-gate))) * gu[:, hid:2 * hid]
        y_ref[0:rows, :] = _pack_rows(jnp.dot(act.astype(BF16), wd_s[...], preferred_element_type=F32))

    n_quanta = bm // MOE_QUANTUM
    for q in range(1, n_quanta + 1):
        rows = q * MOE_QUANTUM

        @pl.when(jnp.logical_and(in_use, jnp.logical_and(nv > rows - MOE_QUANTUM, nv <= rows)))
        def _(rows=rows):
            run(rows)
            if rows < bm:
                y_ref[rows:bm, :] = jnp.zeros((bm - rows, dp), y_ref.dtype)

    @pl.when(jnp.logical_not(in_use))
    def _():
        y_ref[...] = jnp.zeros_like(y_ref)


def _experts(block_e, runs, n_valid, n_used, xb, w_gate, w_up, w_down, layer):
    n_rows, dp = xb.shape
    d, hid = w_gate.shape[2], w_gate.shape[3]
    bm = MOE_BM
    n_blocks = n_rows // bm
    hbm = pl.BlockSpec(memory_space=pl.ANY)
    return pl.pallas_call(
        functools.partial(_expert_kernel, layer=layer),
        grid_spec=pltpu.PrefetchScalarGridSpec(
            num_scalar_prefetch=4,
            grid=(n_blocks,),
            in_specs=[
                pl.BlockSpec((bm, dp), lambda b, be, nx, nv, nu: (jnp.minimum(b, nu[0] - 1), 0)),
                hbm, hbm, hbm,
            ],
            out_specs=pl.BlockSpec((bm, dp), lambda b, be, nx, nv, nu: (b, 0)),
            scratch_shapes=[
                pltpu.VMEM((d, 2 * hid), BF16), pltpu.VMEM((hid, d), BF16),
                pltpu.VMEM((WEIGHT_SLOTS, d, hid), F32), pltpu.VMEM((WEIGHT_SLOTS, d, hid), F32),
                pltpu.VMEM((WEIGHT_SLOTS, hid, d), F32),
                pltpu.SMEM((1,), jnp.int32), pltpu.SemaphoreType.DMA((WEIGHT_SLOTS,)),
            ],
        ),
        out_shape=jax.ShapeDtypeStruct((n_rows, dp), jnp.uint32),
        compiler_params=_params(("arbitrary",)),
        name="experts",
    )(block_e, runs, n_valid, n_used, xb, w_gate, w_up, w_down)


def _combine_kernel(lat_ref, y0_ref, y1_ref, wc_ref, gate_ref, gf_ref, *rest, final):
    o_ref = rest[-1]
    wc = wc_ref[...]
    moe = wc[:, 0:1] * _unpack_rows(y0_ref[...]) + wc[:, 1:2] * _unpack_rows(y1_ref[...])
    out = lat_ref[...] + gate_ref[...] * moe
    if final:
        ms = jnp.mean(out * out, axis=-1, keepdims=True)
        out = out * lax.rsqrt(ms + EPS) * gf_ref[...]
    o_ref[...] = out


def _combine(lat, yg, wcol, mods, layer, g_final, seq, final, tok0, prev_out):
    n, d = lat.shape
    tm = COMBINE_TM
    nt = yg.shape[0] // (2 * tm)
    t0 = tok0 // tm
    tiles_per_seq = seq // tm
    row = lambda i: layer * MOD_ROWS + (t0 + i) // tiles_per_seq
    in_specs = [
        pl.BlockSpec((tm, d), lambda i: (t0 + i, 0)),
        pl.BlockSpec((tm, d // 2), lambda i: (i, 0)),
        pl.BlockSpec((tm, d // 2), lambda i: (nt + i, 0)),
        pl.BlockSpec((tm, LANES), lambda i: (t0 + i, 0)),
        pl.BlockSpec((None, 1, d), lambda i: (row(i), 0, 5)),
        pl.BlockSpec((1, d), lambda i: (0, 0)),
    ]
    args = [lat, yg, yg, wcol, mods, g_final]
    aliases = {}
    if prev_out is not None:
        in_specs.append(pl.BlockSpec(memory_space=pl.ANY))
        args.append(prev_out)
        aliases = {len(args) - 1: 0}
    return pl.pallas_call(
        functools.partial(_combine_kernel, final=final),
        grid=(nt,),
        in_specs=in_specs,
        out_specs=pl.BlockSpec((tm, d), lambda i: (t0 + i, 0)),
        out_shape=jax.ShapeDtypeStruct((n, d), F32),
        input_output_aliases=aliases,
        compiler_params=_params(("parallel",)),
        name="combine",
    )(*args)


def _moe(lat, routed, mods, layer, w_gate, w_up, w_down, g_final, seq, final):
    h2, meta_i, wcol, counts = routed
    n, d = lat.shape
    part = n // MOE_PARTS
    n_blocks = (2 * part) // MOE_BM + N_EXPERTS
    out = None
    for p in range(MOE_PARTS):
        tok0 = p * part
        cnt = counts[p * N_EXPERTS:(p + 1) * N_EXPERTS, 0].astype(jnp.int32)
        dest, block_e, runs, n_valid, n_used = _plan(cnt, meta_i, n_blocks, tok0, part)
        xb = _sc_dispatch(h2, dest, n_blocks * MOE_BM, tok0)
        yb = _experts(block_e, runs, n_valid, n_used, xb, w_gate, w_up, w_down, layer)
        yg = _sc_gather(yb, dest)
        out = _combine(lat, yg, wcol, mods, layer, g_final, seq, final, tok0, out)
    return out


def _rope_tables(seq):
    quarter = HEAD_DIM // 4
    pos = jnp.arange(seq, dtype=F32)
    row_ids = jnp.floor(pos / GRID_W)
    col_ids = pos - row_ids * GRID_W
    inv = ROPE_BASE ** (-jnp.arange(quarter, dtype=F32) / quarter)
    ang_r = row_ids[:, None] * inv
    ang_c = col_ids[:, None] * inv
    zero = jnp.zeros_like(ang_r)
    cos = jnp.concatenate([jnp.cos(ang_r), jnp.cos(ang_r), jnp.cos(ang_c), jnp.cos(ang_c)], axis=-1)
    sa = jnp.concatenate([-jnp.sin(ang_r), zero, -jnp.sin(ang_c), zero], axis=-1)
    sb = jnp.concatenate([zero, jnp.sin(ang_r), zero, jnp.sin(ang_c)], axis=-1)
    rep = LANES // HEAD_DIM
    return tuple(jnp.tile(t, (1, rep)) for t in (cos, sa, sb))


def _router_weights(w_rg, b_rg, w_re, b_re):
    d = w_rg.shape[0]
    pad = EXPERTS_PER_GROUP - N_GROUPS
    wr_t = jnp.concatenate([w_rg.T, jnp.zeros((pad, d), F32), w_re.T], axis=0)
    br_t = jnp.concatenate([b_rg, jnp.full((pad,), NEG_INF, F32), b_re])[:, None]
    return wr_t, br_t


def kernel(x, c, ctx, c_ctx, w_ada, b_ada, g_norm1, g_norm2, g_final, w_in_even, attn_sink, g_sgu,
           w_spatial, b_spatial, w_out_even, w_in_odd, conv_w, w_out_odd, w_router_group,
           b_router_group, w_router_expert, b_router_expert, w_gate, w_up, w_down):
    b, s, d = x.shape
    n = b * s
    n_ctx = ctx.shape[1]
    depth = w_ada.shape[0]
    assert depth == 2 and b + 1 <= MOD_ROWS

    cond = jnp.concatenate([c, c_ctx[None, :], jnp.zeros((MOD_ROWS - b - 1, d), F32)], axis=0)
    mods = _ada(cond, w_ada, b_ada).reshape(depth * MOD_ROWS, 1, 6 * d)
    gf = g_final[None, :]

    lat = x.reshape(n, d)
    w_in_bf = w_in_even[0].astype(BF16)
    tabs = _rope_tables(s)
    qx, k, v, u, z = _even_in(lat, mods, 0, lambda i: i // (s // EVEN_TM), g_norm1[0][None, :], w_in_bf,
                              tabs, s // EVEN_TM, EVEN_TM)
    ones = jnp.ones((n_ctx, LANES), F32)
    zeros = jnp.zeros((n_ctx, LANES), F32)
    _, kc, vc, _, _ = _even_in(ctx.reshape(b * n_ctx, d), mods, 0, lambda i: b, g_norm1[0][None, :],
                               w_in_bf, (ones, zeros, zeros), 1, n_ctx)
    bsp_full = jnp.repeat(b_spatial[0].T, HEAD_DIM, axis=1)
    half = N_Q_HEADS // 2
    w_att = w_out_even[0][:Q_DIM].reshape(2, half, HEAD_DIM, d).transpose(1, 0, 2, 3).reshape(Q_DIM, d)
    w_out_bf = jnp.concatenate([w_att, w_out_even[0][Q_DIM:]], axis=0).astype(BF16)
    wr_t, br_t = _router_weights(w_router_group[0], b_router_group[0], w_router_expert[0], b_router_expert[0])
    lat, *routed = _even_mix(lat, qx, k, v, kc, vc, u, z, attn_sink[0], g_sgu[0][None, :],
                             w_spatial[0].astype(BF16), bsp_full, w_out_bf, mods, s, n_ctx,
                             g_norm2[0][None, :], wr_t, br_t)
    lat = _moe(lat, routed, mods, 0, w_gate, w_up, w_down, gf, s, False)

    conv_w8 = jnp.concatenate([conv_w[0], jnp.zeros((8 - conv_w.shape[1], d), F32)], axis=0)
    wr_t, br_t = _router_weights(w_router_group[1], b_router_group[1], w_router_expert[1], b_router_expert[1])
    lat, *routed = _odd_mix(lat, mods, 1, g_norm1[1][None, :], w_in_odd[0].astype(BF16), conv_w8,
                            w_out_odd[0].astype(BF16), s, g_norm2[1][None, :], wr_t, br_t)
    out = _moe(lat, routed, mods, 1, w_gate, w_up, w_down, gf, s, True)
    return out.reshape(b, s, d)
```

```python
import functools

import jax
import jax.numpy as jnp
from jax import lax
from jax.experimental import pallas as pl
from jax.experimental.pallas import tpu as pltpu
from jax.experimental.pallas import tpu_sc as plsc

F32 = jnp.float32
BF16 = jnp.bfloat16
HIGHEST = lax.Precision.HIGHEST

GRID_W = 64
N_Q_HEADS = 8
N_KV_HEADS = 2
HEAD_DIM = 64
ATT_BLOCK = 128
ROPE_BASE = 10000.0
Q_DIM = N_Q_HEADS * HEAD_DIM
KV_DIM = N_KV_HEADS * HEAD_DIM
SG_GROUPS = 8
SG_WIDTH = SG_GROUPS * HEAD_DIM
N_GROUPS = 4
EXPERTS_PER_GROUP = 8
N_EXPERTS = N_GROUPS * EXPERTS_PER_GROUP
EPS = 1e-6
NEG_INF = -1e30

LANES = 128
SC_CORES = 2
SC_SUBCORES = 16
SC_WORKERS = SC_CORES * SC_SUBCORES
SC_CHUNK = 32
MOD_ROWS = 8
ROUTER_ROWS = EXPERTS_PER_GROUP + N_EXPERTS
VMEM_LIMIT = 56 * 1024 * 1024

ADA_TN = 1536
EVEN_TM = 1024
EVEN_SUB = 512
ATT_TQ = 512
ODD_TM = 512
MOE_BM = 1024
MOE_QUANTUM = 128
MOE_PARTS = 2
WEIGHT_SLOTS = 3
COMBINE_TM = 512


def _params(sem):
    return pltpu.CompilerParams(dimension_semantics=sem, vmem_limit_bytes=VMEM_LIMIT)


def _rms_mod(x, g, shift, scale):
    ms = jnp.mean(x * x, axis=-1, keepdims=True)
    return (x * lax.rsqrt(ms + EPS)) * (g * (1.0 + scale)) + shift


def _pack_rounded(a):
    w = a.shape[1] // 2
    hi = pltpu.bitcast(a[:, :w], jnp.uint32)
    lo = pltpu.bitcast(a[:, w:], jnp.uint32)
    return hi | (lo >> 16)


def _pack_rows(a):
    return _pack_rounded(a.astype(BF16).astype(F32))


def _unpack_rows(p):
    hi = pltpu.bitcast(p & jnp.uint32(0xFFFF0000), F32)
    lo = pltpu.bitcast(p << 16, F32)
    return jnp.concatenate([hi, lo], axis=1)


def _ada_kernel(a_ref, w_ref, b_ref, o_ref):
    a = a_ref[...]
    s = a * (1.0 / (1.0 + jnp.exp(-a)))
    o_ref[0] = jnp.dot(s, w_ref[0], preferred_element_type=F32, precision=HIGHEST) + b_ref[0]


def _ada(cond, w_ada, b_ada):
    depth, d, six_d = w_ada.shape
    return pl.pallas_call(
        _ada_kernel,
        grid=(depth, six_d // ADA_TN),
        in_specs=[
            pl.BlockSpec((MOD_ROWS, d), lambda l, j: (0, 0)),
            pl.BlockSpec((1, d, ADA_TN), lambda l, j: (l, 0, j)),
            pl.BlockSpec((1, 1, ADA_TN), lambda l, j: (l, 0, j)),
        ],
        out_specs=pl.BlockSpec((1, MOD_ROWS, ADA_TN), lambda l, j: (l, 0, j)),
        out_shape=jax.ShapeDtypeStruct((depth, MOD_ROWS, six_d), F32),
        compiler_params=_params(("arbitrary", "arbitrary")),
        name="ada",
    )(cond, w_ada, b_ada.reshape(depth, 1, six_d))


def _even_in_kernel(x_ref, sh_ref, sc_ref, g_ref, w_ref, cos_ref, sa_ref, sb_ref,
                    qx_ref, k_ref, v_ref, u_ref, z_ref):
    tm = x_ref.shape[0]
    sub = min(tm, EVEN_SUB)
    scale = HEAD_DIM ** -0.5
    low = lax.broadcasted_iota(jnp.int32, (sub, LANES), 1) < HEAD_DIM
    heads_per_kv = N_Q_HEADS // N_KV_HEADS
    u0 = Q_DIM + 2 * KV_DIM

    for r0 in range(0, tm, sub):
        rows = slice(r0, r0 + sub)
        h = _rms_mod(x_ref[rows, :], g_ref[...], sh_ref[...], sc_ref[...])
        p = jnp.dot(h.astype(BF16), w_ref[...], preferred_element_type=F32)
        cos, sa, sb = cos_ref[rows, :], sa_ref[rows, :], sb_ref[rows, :]

        def rope(t):
            return t * cos + pltpu.roll(t, LANES - 16, 1) * sa + pltpu.roll(t, 16, 1) * sb

        for cblk in range(Q_DIM // LANES):
            t = rope(p[:, cblk * LANES:(cblk + 1) * LANES]) * scale
            sw = pltpu.roll(t, HEAD_DIM, 1)
            zero = jnp.zeros_like(t)
            if (2 * cblk) // heads_per_kv == 0:
                first, second = jnp.where(low, t, zero), jnp.where(low, sw, zero)
            else:
                first, second = jnp.where(low, zero, sw), jnp.where(low, zero, t)
            qx_ref[rows, (2 * cblk) * LANES:(2 * cblk + 1) * LANES] = first.astype(BF16)
            qx_ref[rows, (2 * cblk + 1) * LANES:(2 * cblk + 2) * LANES] = second.astype(BF16)

        k_ref[rows, :] = rope(p[:, Q_DIM:Q_DIM + KV_DIM]).astype(BF16)
        v_ref[rows, :] = p[:, Q_DIM + KV_DIM:Q_DIM + 2 * KV_DIM].astype(BF16)
        u_ref[rows, :] = p[:, u0:u0 + SG_WIDTH]
        z_ref[rows, :] = p[:, u0 + SG_WIDTH:u0 + 2 * SG_WIDTH]


def _even_in(x2d, mods, layer, mod_row_fn, g, w_bf, tabs, tab_blocks, tm):
    n, d = x2d.shape
    ein = w_bf.shape[1]
    cos, sa, sb = tabs
    row = lambda i: layer * MOD_ROWS + mod_row_fn(i)
    tab_spec = pl.BlockSpec((tm, LANES), lambda i: (i % tab_blocks, 0))
    qx_dim = N_Q_HEADS * LANES
    return pl.pallas_call(
        _even_in_kernel,
        grid=(n // tm,),
        in_specs=[
            pl.BlockSpec((tm, d), lambda i: (i, 0)),
            pl.BlockSpec((None, 1, d), lambda i: (row(i), 0, 0)),
            pl.BlockSpec((None, 1, d), lambda i: (row(i), 0, 1)),
            pl.BlockSpec((1, d), lambda i: (0, 0)),
            pl.BlockSpec((d, ein), lambda i: (0, 0)),
            tab_spec, tab_spec, tab_spec,
        ],
        out_specs=[
            pl.BlockSpec((tm, qx_dim), lambda i: (i, 0)),
            pl.BlockSpec((tm, KV_DIM), lambda i: (i, 0)),
            pl.BlockSpec((tm, KV_DIM), lambda i: (i, 0)),
            pl.BlockSpec((tm, SG_WIDTH), lambda i: (i, 0)),
            pl.BlockSpec((tm, SG_WIDTH), lambda i: (i, 0)),
        ],
        out_shape=[
            jax.ShapeDtypeStruct((n, qx_dim), BF16),
            jax.ShapeDtypeStruct((n, KV_DIM), BF16),
            jax.ShapeDtypeStruct((n, KV_DIM), BF16),
            jax.ShapeDtypeStruct((n, SG_WIDTH), F32),
            jax.ShapeDtypeStruct((n, SG_WIDTH), F32),
        ],
        compiler_params=_params(("parallel",)),
        name="even_in",
    )(x2d, mods, mods, g, w_bf, cos, sa, sb)


def _gelu(x):
    return 0.5 * x * (1.0 + lax.erf(x * (2.0 ** -0.5)))


def _even_mix_kernel(sink_ref, lat_ref, qx_ref, km_ref, kp_ref, kn_ref, vm_ref, vp_ref, vn_ref,
                     kc_ref, vc_ref, u_ref, z_ref, gsgu_ref, wsp_ref, bsp_ref, wout_ref, gate_ref,
                     sh2_ref, sc2_ref, g2_ref, wr_ref, br_ref,
                     o_ref, h_ref, mi_ref, wc_ref, cnt_ref,
                     kband, vband, mixin, carry, upper, *, tiles_per_seq, tiles_per_part):
    i = pl.program_id(0)
    tq = qx_ref.shape[0]

    @pl.when(i == 0)
    def _():
        _route_init(carry, upper)

    @pl.when(i % tiles_per_part == 0)
    def _():
        carry[...] = jnp.zeros_like(carry)

    blk = ATT_BLOCK
    nsub = tq // blk
    n_ctx = kc_ref.shape[0]
    first = (i % tiles_per_seq) == 0
    last = (i % tiles_per_seq) == tiles_per_seq - 1

    kband[0:blk] = kp_ref[...]
    kband[blk:blk + tq] = km_ref[...]
    kband[blk + tq:] = kn_ref[...]
    vband[0:blk] = vp_ref[...]
    vband[blk:blk + tq] = vm_ref[...]
    vband[blk + tq:] = vn_ref[...]

    rows = N_Q_HEADS * blk
    tok = lax.broadcasted_iota(jnp.int32, (rows, blk), 0) & (blk - 1)
    col = lax.broadcasted_iota(jnp.int32, (rows, blk), 1)
    tri_prev = col >= tok
    tri_next = col <= tok
    head = lax.broadcasted_iota(jnp.int32, (rows, 1), 0) // blk
    sink_col = jnp.zeros((rows, 1), F32)
    for hd in range(N_Q_HEADS):
        sink_col = jnp.where(head == hd, sink_ref[hd], sink_col)
    lane_low = lax.broadcasted_iota(jnp.int32, (blk, LANES), 1) < HEAD_DIM
    ones = jnp.ones((n_ctx + 3 * blk, LANES), BF16)
    nt = (((1,), (1,)), ((), ()))

    def sub_block(j, c):
        r0 = pl.multiple_of(j * blk, blk)
        ok_prev = jnp.logical_not(jnp.logical_and(first, j == 0))
        ok_next = jnp.logical_not(jnp.logical_and(last, j == nsub - 1))
        qs = jnp.concatenate([qx_ref[pl.ds(r0, blk), hd * LANES:(hd + 1) * LANES]
                              for hd in range(N_Q_HEADS)], axis=0)
        kall = jnp.concatenate([kc_ref[...], kband[pl.ds(r0, 3 * blk), :]], axis=0)
        vall = jnp.concatenate([vc_ref[...], vband[pl.ds(r0, 3 * blk), :]], axis=0)
        s = lax.dot_general(qs, kall, nt, preferred_element_type=F32)
        c0 = n_ctx
        s = jnp.concatenate([
            s[:, :c0],
            jnp.where(jnp.logical_and(tri_prev, ok_prev), s[:, c0:c0 + blk], NEG_INF),
            s[:, c0 + blk:c0 + 2 * blk],
            jnp.where(jnp.logical_and(tri_next, ok_next), s[:, c0 + 2 * blk:], NEG_INF),
        ], axis=1)
        m = jnp.maximum(jnp.max(s, axis=-1, keepdims=True), sink_col)
        p = jnp.exp(s - m).astype(BF16)
        o = jnp.dot(p, jnp.concatenate([vall, ones], axis=1), preferred_element_type=F32)
        att = o[:, :LANES] / (o[:, LANES:] + jnp.exp(sink_col - m))
        half = N_Q_HEADS // 2
        for hd in range(half):
            pair = jnp.where(lane_low, att[hd * blk:(hd + 1) * blk], att[(hd + half) * blk:(hd + half + 1) * blk])
            mixin[pl.ds(r0, blk), hd * LANES:(hd + 1) * LANES] = pair.astype(BF16)

        ug = _gelu(u_ref[pl.ds(r0, blk), :])
        zg = _gelu(z_ref[pl.ds(r0, blk), :])
        mu = jnp.mean(zg, axis=-1, keepdims=True)
        zc = zg - mu
        zn = zc * lax.rsqrt(jnp.mean(zc * zc, axis=-1, keepdims=True) + EPS) * gsgu_ref[...]
        for pair in range(SG_GROUPS // 2):
            zp = zn[:, pair * LANES:(pair + 1) * LANES]
            zero = jnp.zeros_like(zp)
            lo = jnp.where(lane_low, zp, zero).astype(BF16)
            hi = jnp.where(lane_low, zero, zp).astype(BF16)
            sg = (jnp.dot(wsp_ref[2 * pair], lo, preferred_element_type=F32)
                  + jnp.dot(wsp_ref[2 * pair + 1], hi, preferred_element_type=F32)
                  + bsp_ref[:, pair * LANES:(pair + 1) * LANES])
            mixin[pl.ds(r0, blk), Q_DIM + pair * LANES:Q_DIM + (pair + 1) * LANES] = (
                ug[:, pair * LANES:(pair + 1) * LANES] * sg).astype(BF16)
        return c

    lax.fori_loop(0, nsub, sub_block, 0, unroll=True)
    mix = jnp.dot(mixin[...], wout_ref[...], preferred_element_type=F32)
    lat = lat_ref[...] + gate_ref[...] * mix
    o_ref[...] = lat
    _route_tile(lat, sh2_ref[...], sc2_ref[...], g2_ref[...], wr_ref, br_ref, carry, upper,
                h_ref, mi_ref, wc_ref, cnt_ref)


def _even_mix(lat, qx, k, v, kc, vc, u, z, sink, g_sgu, wsp_bf, bsp_full, wout_bf, mods, seq, ctx_len,
              g2, wr_t, br_t, tok0, n):
    d = lat.shape[1]
    tq = ATT_TQ
    tiles_per_seq = seq // tq
    sub = tq // ATT_BLOCK
    nblk = lat.shape[0] // ATT_BLOCK
    t0 = tok0 // tq
    main = lambda w: pl.BlockSpec((tq, w), lambda i: (t0 + i, 0))
    prev = pl.BlockSpec((ATT_BLOCK, KV_DIM), lambda i: (jnp.maximum((t0 + i) * sub - 1, 0), 0))
    nxt = pl.BlockSpec((ATT_BLOCK, KV_DIM), lambda i: (jnp.minimum((t0 + i + 1) * sub, nblk - 1), 0))
    batch = lambda t: (t0 + t) // tiles_per_seq
    ctxs = pl.BlockSpec((ctx_len, KV_DIM), lambda i: (batch(i), 0))
    const = lambda shape: pl.BlockSpec(shape, lambda i: (0,) * len(shape), pipeline_mode=pl.Buffered(1))
    r_in, r_out, r_shapes, r_scratch = _route_specs(n, d, tq, lambda i: i, batch)
    return pl.pallas_call(
        functools.partial(_even_mix_kernel, tiles_per_seq=tiles_per_seq, tiles_per_part=n // tq),
        grid=(n // tq,),
        in_specs=[
            pl.BlockSpec(memory_space=pltpu.SMEM),
            main(d), main(qx.shape[1]),
            main(KV_DIM), prev, nxt,
            main(KV_DIM), prev, nxt,
            ctxs, ctxs,
            main(SG_WIDTH), main(SG_WIDTH),
            const((1, SG_WIDTH)), const(wsp_bf.shape), const(bsp_full.shape), const(wout_bf.shape),
            pl.BlockSpec((None, 1, d), lambda i: (batch(i), 0, 2)),
        ] + r_in,
        out_specs=[pl.BlockSpec((tq, d), lambda i: (i, 0))] + r_out,
        out_shape=[jax.ShapeDtypeStruct((n, d), F32)] + r_shapes,
        scratch_shapes=[
            pltpu.VMEM((tq + 2 * ATT_BLOCK, KV_DIM), BF16),
            pltpu.VMEM((tq + 2 * ATT_BLOCK, KV_DIM), BF16),
            pltpu.VMEM((tq, Q_DIM + SG_WIDTH), BF16),
        ] + r_scratch,
        compiler_params=_params(("arbitrary",)),
        name="even_mix",
    )(sink, lat, qx, k, k, k, v, v, v, kc, vc, u, z, g_sgu, wsp_bf, bsp_full, wout_bf, mods,
      mods, mods, g2, wr_t, br_t)


def _odd_mix_kernel(x_ref, xp_ref, sh_ref, sc_ref, gate_ref, g_ref, win_ref, cw_ref, wout_ref,
                    sh2_ref, sc2_ref, g2_ref, wr_ref, br_ref,
                    o_ref, h_ref, mi_ref, wc_ref, cnt_ref,
                    y_s, bg_s, tail_s, carry, upper, *, tiles_per_seq, tiles_per_part):
    i = pl.program_id(0)
    n_tiles = pl.num_programs(0) - 1
    tm, d = x_ref.shape
    cur = i % 2
    prv = 1 - cur

    @pl.when(i == 0)
    def _():
        _route_init(carry, upper)

    tail_s[...] = y_s[cur, tm - 8:tm, :]

    @pl.when(i < n_tiles)
    def _():
        h = _rms_mod(x_ref[...], g_ref[...], sh_ref[...], sc_ref[...])
        p = jnp.dot(h.astype(BF16), win_ref[...], preferred_element_type=F32)
        bg_s[cur] = p[:, 0:d]
        y_s[cur] = p[:, d:2 * d] * p[:, 2 * d:3 * d]

    @pl.when(i >= 1)
    def _():
        t_prev = i - 1
        first = (t_prev % tiles_per_seq) == 0
        last = (t_prev % tiles_per_seq) == tiles_per_seq - 1
        y = y_s[prv]
        left = jnp.where(first, 0.0, tail_s[7:8, :])
        right = jnp.where(last, 0.0, y_s[cur, 0:1, :])
        ridx = lax.broadcasted_iota(jnp.int32, (tm, d), 0)
        y_dn = jnp.where(ridx == 0, left, pltpu.roll(y, 1, 0))
        y_up = jnp.where(ridx == tm - 1, right, pltpu.roll(y, tm - 1, 0))
        conv = y_dn * cw_ref[0:1, :] + y * cw_ref[1:2, :] + y_up * cw_ref[2:3, :]
        mix = jnp.dot((bg_s[prv] * conv).astype(BF16), wout_ref[...], preferred_element_type=F32)
        lat = xp_ref[...] + gate_ref[...] * mix
        o_ref[...] = lat

        @pl.when(t_prev % tiles_per_part == 0)
        def _():
            carry[...] = jnp.zeros_like(carry)

        _route_tile(lat, sh2_ref[...], sc2_ref[...], g2_ref[...], wr_ref, br_ref, carry, upper,
                    h_ref, mi_ref, wc_ref, cnt_ref)


def _odd_mix(lat, mods, layer, g, win_bf, conv_w8, wout_bf, seq, g2, wr_t, br_t, batch0):
    n, d = lat.shape
    tm = ODD_TM
    nt = n // tm
    tiles_per_seq = seq // tm
    cur = lambda i: jnp.minimum(i, nt - 1)
    prv = lambda i: jnp.maximum(i - 1, 0)
    row = lambda t: layer * MOD_ROWS + batch0 + t // tiles_per_seq
    const = lambda shape: pl.BlockSpec(shape, lambda i: (0,) * len(shape), pipeline_mode=pl.Buffered(1))
    r_in, r_out, r_shapes, r_scratch = _route_specs(n, d, tm, prv, row)
    return pl.pallas_call(
        functools.partial(_odd_mix_kernel, tiles_per_seq=tiles_per_seq, tiles_per_part=nt),
        grid=(nt + 1,),
        in_specs=[
            pl.BlockSpec((tm, d), lambda i: (cur(i), 0)),
            pl.BlockSpec((tm, d), lambda i: (prv(i), 0)),
            pl.BlockSpec((None, 1, d), lambda i: (row(cur(i)), 0, 0)),
            pl.BlockSpec((None, 1, d), lambda i: (row(cur(i)), 0, 1)),
            pl.BlockSpec((None, 1, d), lambda i: (row(prv(i)), 0, 2)),
            const((1, d)), const(win_bf.shape), const(conv_w8.shape), const(wout_bf.shape),
        ] + r_in,
        out_specs=[pl.BlockSpec((tm, d), lambda i: (prv(i), 0))] + r_out,
        out_shape=[jax.ShapeDtypeStruct((n, d), F32)] + r_shapes,
        scratch_shapes=[
            pltpu.VMEM((2, tm, d), F32),
            pltpu.VMEM((2, tm, d), F32),
            pltpu.VMEM((8, d), F32),
        ] + r_scratch,
        compiler_params=_params(("arbitrary",)),
        name="odd_mix",
    )(lat, lat, mods, mods, mods, g, win_bf, conv_w8, wout_bf, mods, mods, g2, wr_t, br_t)


def _route_init(carry, upper):
    tm = upper.shape[0]
    carry[...] = jnp.zeros_like(carry)
    r_i = lax.broadcasted_iota(jnp.int32, (tm, tm), 0)
    c_i = lax.broadcasted_iota(jnp.int32, (tm, tm), 1)
    upper[...] = jnp.where(r_i < c_i, 1.0, 0.0).astype(BF16)


def _route_tile(lat, sh, sc, g, wr_ref, br_ref, carry, upper, h_ref, mi_ref, wc_ref, cnt_ref):
    tm = lat.shape[0]
    epg = EXPERTS_PER_GROUP
    h = _rms_mod(lat, g, sh, sc)
    h_hi = h.astype(BF16)
    h_hi_f = h_hi.astype(F32)
    h_ref[...] = _pack_rounded(h_hi_f)
    h_lo = (h - h_hi_f).astype(BF16)
    w = wr_ref[...]
    w1 = w.astype(BF16).astype(F32)
    r1 = w - w1
    w2 = r1.astype(BF16).astype(F32)
    w3 = r1 - w2
    nt = (((1,), (1,)), ((), ()))
    nr = w.shape[0]
    w123 = jnp.concatenate([w1, w2, w3, jnp.zeros((8, w.shape[1]), F32)], axis=0).astype(BF16)
    w12 = jnp.concatenate([w1, w2], axis=0).astype(BF16)
    p_hi = lax.dot_general(w123, h_hi, nt, preferred_element_type=F32)
    p_lo = lax.dot_general(w12, h_lo, nt, preferred_element_type=F32)
    lg = ((p_hi[2 * nr:3 * nr] + p_lo[nr:2 * nr]) + (p_hi[nr:2 * nr] + p_lo[0:nr])) + p_hi[0:nr] + br_ref[...]
    io8 = lax.broadcasted_iota(jnp.int32, (epg, tm), 0)
    gl = lg[0:epg]
    gmax = jnp.max(gl, axis=0, keepdims=True)
    g_idx = jnp.min(jnp.where(gl == gmax, io8, epg), axis=0, keepdims=True)
    g_w = 1.0 / jnp.sum(jnp.exp(gl - gmax), axis=0, keepdims=True)
    e_sel = lg[epg:2 * epg]
    for gi in range(1, N_GROUPS):
        e_sel = jnp.where(g_idx == gi, lg[(gi + 1) * epg:(gi + 2) * epg], e_sel)
    v0 = jnp.max(e_sel, axis=0, keepdims=True)
    i0 = jnp.min(jnp.where(e_sel == v0, io8, epg), axis=0, keepdims=True)
    rest = jnp.where(io8 == i0, -jnp.inf, e_sel)
    v1 = jnp.max(rest, axis=0, keepdims=True)
    i1 = jnp.min(jnp.where(rest == v1, io8, epg), axis=0, keepdims=True)
    t = jnp.exp(v1 - v0)
    w0 = g_w / (1.0 + t)
    w1 = g_w * t / (1.0 + t)
    e0 = g_idx * epg + i0
    e1 = g_idx * epg + i1

    io32 = lax.broadcasted_iota(jnp.int32, (N_EXPERTS, tm), 0)
    hit0 = io32 == e0
    hit1 = io32 == e1
    onehot = jnp.where(hit0 | hit1, 1.0, 0.0)
    cum = jnp.dot(onehot.astype(BF16), upper[...], preferred_element_type=F32) + carry[...]
    rank0 = jnp.sum(jnp.where(hit0, cum, 0.0), axis=0, keepdims=True).astype(jnp.int32)
    rank1 = jnp.sum(jnp.where(hit1, cum, 0.0), axis=0, keepdims=True).astype(jnp.int32)
    carry[...] = carry[...] + jnp.sum(onehot, axis=1, keepdims=True)
    cnt_ref[...] = jnp.broadcast_to(carry[...], cnt_ref.shape)

    mi_ref[...] = jnp.where(io8 == 0, e0, jnp.where(io8 == 1, e1, jnp.where(io8 == 2, rank0,
                            jnp.where(io8 == 3, rank1, 0))))
    io128 = lax.broadcasted_iota(jnp.int32, (LANES, tm), 0)
    wrow = jnp.where(io128 == 0, w0, jnp.where(io128 == 1, w1, 0.0))
    wc_ref[...] = wrow.T


def _route_specs(n, d, tm, tile_of, mods_row):
    in_specs = [
        pl.BlockSpec((None, 1, d), lambda i: (mods_row(tile_of(i)), 0, 3)),
        pl.BlockSpec((None, 1, d), lambda i: (mods_row(tile_of(i)), 0, 4)),
        pl.BlockSpec((1, d), lambda i: (0, 0), pipeline_mode=pl.Buffered(1)),
        pl.BlockSpec((ROUTER_ROWS, d), lambda i: (0, 0), pipeline_mode=pl.Buffered(1)),
        pl.BlockSpec((ROUTER_ROWS, 1), lambda i: (0, 0), pipeline_mode=pl.Buffered(1)),
    ]
    out_specs = [
        pl.BlockSpec((tm, d // 2), lambda i: (tile_of(i), 0)),
        pl.BlockSpec((8, tm), lambda i: (0, tile_of(i))),
        pl.BlockSpec((tm, LANES), lambda i: (tile_of(i), 0)),
        pl.BlockSpec((N_EXPERTS, LANES), lambda i: (0, 0)),
    ]
    out_shapes = [
        jax.ShapeDtypeStruct((n, d // 2), jnp.uint32),
        jax.ShapeDtypeStruct((8, n), jnp.int32),
        jax.ShapeDtypeStruct((n, LANES), F32),
        jax.ShapeDtypeStruct((N_EXPERTS, LANES), F32),
    ]
    scratch = [pltpu.VMEM((N_EXPERTS, 1), F32), pltpu.VMEM((tm, tm), BF16)]
    return in_specs, out_specs, out_shapes, scratch


def _plan_kernel(cnt_ref, mi_ref, dest_ref, be_ref, runs_ref, nv_ref, nu_ref, ps_ref, *, n_blocks):
    bm = MOE_BM

    def per_expert(e, carry):
        blk0, n_runs = carry
        cnt = cnt_ref[e]
        nb = (cnt + bm - 1) // bm
        ps_ref[e] = blk0 * bm

        def fill(b, c):
            be_ref[b] = e
            nv_ref[b] = jnp.minimum(cnt - (b - blk0) * bm, bm)
            return c

        lax.fori_loop(blk0, blk0 + nb, fill, 0)

        @pl.when(nb > 0)
        def _():
            runs_ref[n_runs] = e

        return blk0 + nb, n_runs + jnp.where(nb > 0, 1, 0)

    n_used, n_runs = lax.fori_loop(0, N_EXPERTS, per_expert, (0, 0))
    nu_ref[0] = n_used
    nu_ref[1] = n_runs
    last_e = be_ref[jnp.maximum(n_used - 1, 0)]

    def fill_tail(b, c):
        be_ref[b] = last_e
        nv_ref[b] = 0
        return c

    lax.fori_loop(n_used, n_blocks, fill_tail, 0)

    def fill_runs(k, c):
        runs_ref[k] = last_e
        return c

    lax.fori_loop(n_runs, N_EXPERTS, fill_runs, 0)

    e01 = mi_ref[0:2, :]
    dest = mi_ref[2:4, :]
    for e in range(N_EXPERTS):
        dest = dest + jnp.where(e01 == e, ps_ref[e], 0)
    dest_ref[...] = dest


def _plan(counts, meta_i, n_blocks):
    n = meta_i.shape[1]
    return pl.pallas_call(
        functools.partial(_plan_kernel, n_blocks=n_blocks),
        in_specs=[pl.BlockSpec(memory_space=pltpu.SMEM), pl.BlockSpec(memory_space=pltpu.VMEM)],
        out_specs=[pl.BlockSpec(memory_space=pltpu.VMEM)] + [pl.BlockSpec(memory_space=pltpu.SMEM)] * 4,
        out_shape=[
            jax.ShapeDtypeStruct((2, n), jnp.int32),
            jax.ShapeDtypeStruct((n_blocks,), jnp.int32),
            jax.ShapeDtypeStruct((N_EXPERTS,), jnp.int32),
            jax.ShapeDtypeStruct((n_blocks,), jnp.int32),
            jax.ShapeDtypeStruct((2,), jnp.int32),
        ],
        scratch_shapes=[pltpu.SMEM((N_EXPERTS,), jnp.int32)],
        compiler_params=pltpu.CompilerParams(vmem_limit_bytes=VMEM_LIMIT),
        name="plan",
    )(counts, meta_i)


def _sc_mesh():
    return plsc.VectorSubcoreMesh(core_axis_name="c", subcore_axis_name="s",
                                  num_cores=SC_CORES, num_subcores=SC_SUBCORES)


def _sc_worker():
    return lax.axis_index("s") * SC_CORES + lax.axis_index("c")


def _sc_dispatch(h2, dest, n_rows):
    n, d = h2.shape
    c = SC_CHUNK
    per_w = n // SC_WORKERS
    nchunk = per_w // c
    idx = dest.reshape(2, SC_WORKERS, nchunk, c)

    @functools.partial(
        pl.kernel, mesh=_sc_mesh(), out_type=jax.ShapeDtypeStruct((n_rows, d), h2.dtype),
        scratch_types=[pltpu.VMEM((nchunk, c), jnp.int32), pltpu.VMEM((nchunk, c), jnp.int32),
                       pltpu.VMEM((2, c, d), h2.dtype),
                       pltpu.SemaphoreType.DMA((2,)), pltpu.SemaphoreType.DMA((2,))])
    def k(h_hbm, idx_hbm, xb_hbm, idx0_v, idx1_v, rows_v, gsem, ssem):
        wid = _sc_worker()
        base = wid * per_w
        idx_v = (idx0_v, idx1_v)
        for kk in range(2):
            pltpu.sync_copy(idx_hbm.at[kk, wid], idx_v[kk])

        def get(j, slot):
            return pltpu.make_async_copy(h_hbm.at[pl.ds(base + j * c, c)], rows_v.at[slot], gsem.at[slot])

        def put(j, slot, kk):
            return pltpu.make_async_copy(rows_v.at[slot], xb_hbm.at[idx_v[kk].at[j]], ssem.at[slot])

        get(0, 0).start()

        @pl.loop(0, nchunk, step=2)
        def _(j):
            for slot in range(2):
                jj = j + slot
                get(jj, slot).wait()

                @pl.when(jj >= 1)
                def _():
                    for kk in range(2):
                        put(jj - 1, 1 - slot, kk).wait()

                @pl.when(jj + 1 < nchunk)
                def _():
                    get(jj + 1, 1 - slot).start()

                for kk in range(2):
                    put(jj, slot, kk).start()

        for kk in range(2):
            put(nchunk - 1, (nchunk - 1) % 2, kk).wait()

    return k(h2, idx)


def _sc_gather(y, dest):
    d = y.shape[1]
    total = dest.shape[0] * dest.shape[1]
    c = SC_CHUNK
    per_w = total // SC_WORKERS
    nchunk = per_w // c
    idx = dest.reshape(SC_WORKERS, nchunk, c)

    @functools.partial(
        pl.kernel, mesh=_sc_mesh(), out_type=jax.ShapeDtypeStruct((total, d), y.dtype),
        scratch_types=[pltpu.VMEM((nchunk, c), jnp.int32), pltpu.VMEM((2, c, d), y.dtype),
                       pltpu.SemaphoreType.DMA((2,)), pltpu.SemaphoreType.DMA((2,))])
    def k(y_hbm, idx_hbm, out_hbm, idx_v, rows_v, gsem, ssem):
        wid = _sc_worker()
        base = wid * per_w
        pltpu.sync_copy(idx_hbm.at[wid], idx_v)

        def get(j, slot):
            return pltpu.make_async_copy(y_hbm.at[idx_v.at[j]], rows_v.at[slot], gsem.at[slot])

        def put(j, slot):
            return pltpu.make_async_copy(rows_v.at[slot], out_hbm.at[pl.ds(base + j * c, c)], ssem.at[slot])

        get(0, 0).start()

        @pl.loop(0, nchunk, step=2)
        def _(j):
            for slot in range(2):
                jj = j + slot
                get(jj, slot).wait()

                @pl.when(jj >= 1)
                def _():
                    put(jj - 1, 1 - slot).wait()

                @pl.when(jj + 1 < nchunk)
                def _():
                    get(jj + 1, 1 - slot).start()

                put(jj, slot).start()

        put(nchunk - 1, (nchunk - 1) % 2).wait()

    return k(y, idx)


def _expert_kernel(be_ref, runs_ref, nv_ref, nu_ref, x_ref, wg_hbm, wu_hbm, wd_hbm, y_ref,
                   wgu_s, wd_s, stg_g, stg_u, stg_d, run_s, sems, *, layer):
    b = pl.program_id(0)
    hid = stg_g.shape[2]
    e = be_ref[b]
    n_runs = nu_ref[1]
    changed = jnp.logical_or(b == 0, e != be_ref[jnp.maximum(b - 1, 0)])

    def fetch(run):
        expert = runs_ref[run]
        slot = run % WEIGHT_SLOTS
        return (pltpu.make_async_copy(wg_hbm.at[layer, expert], stg_g.at[slot], sems.at[slot]),
                pltpu.make_async_copy(wu_hbm.at[layer, expert], stg_u.at[slot], sems.at[slot]),
                pltpu.make_async_copy(wd_hbm.at[layer, expert], stg_d.at[slot], sems.at[slot]))

    @pl.when(b == 0)
    def _():
        for r in range(WEIGHT_SLOTS - 1):
            @pl.when(r < n_runs)
            def _():
                for cp in fetch(r):
                    cp.start()

    @pl.when(changed)
    def _():
        run = jnp.where(b == 0, 0, run_s[0] + 1)
        run_s[0] = run
        for cp in fetch(run):
            cp.wait()

        ahead = run + WEIGHT_SLOTS - 1

        @pl.when(ahead < n_runs)
        def _():
            for cp in fetch(ahead):
                cp.start()

        slot = run % WEIGHT_SLOTS
        wgu_s[:, 0:hid] = stg_g[slot].astype(BF16)
        wgu_s[:, hid:2 * hid] = stg_u[slot].astype(BF16)
        wd_s[...] = stg_d[slot].astype(BF16)

    bm, dp = x_ref.shape
    nv = nv_ref[b]
    in_use = b < nu_ref[0]

    def run(rows):
        live = lax.broadcasted_iota(jnp.int32, (rows, dp), 0) < nv
        x = _unpack_rows(jnp.where(live, x_ref[0:rows, :], jnp.uint32(0)))
        gu = jnp.dot(x.astype(BF16), wgu_s[...], preferred_element_type=F32)
        gate = gu[:, 0:hid]
        act = gate * (1.0 / (1.0 + jnp.exp(-gate))) * gu[:, hid:2 * hid]
        y_ref[0:rows, :] = _pack_rows(jnp.dot(act.astype(BF16), wd_s[...], preferred_element_type=F32))

    n_quanta = bm // MOE_QUANTUM
    for q in range(1, n_quanta + 1):
        rows = q * MOE_QUANTUM

        @pl.when(jnp.logical_and(in_use, jnp.logical_and(nv > rows - MOE_QUANTUM, nv <= rows)))
        def _(rows=rows):
            run(rows)
            if rows < bm:
                y_ref[rows:bm, :] = jnp.zeros((bm - rows, dp), y_ref.dtype)

    @pl.when(jnp.logical_not(in_use))
    def _():
        y_ref[...] = jnp.zeros_like(y_ref)


def _experts(block_e, runs, n_valid, n_used, xb, w_gate, w_up, w_down, layer):
    n_rows, dp = xb.shape
    d, hid = w_gate.shape[2], w_gate.shape[3]
    bm = MOE_BM
    n_blocks = n_rows // bm
    hbm = pl.BlockSpec(memory_space=pl.ANY)
    return pl.pallas_call(
        functools.partial(_expert_kernel, layer=layer),
        grid_spec=pltpu.PrefetchScalarGridSpec(
            num_scalar_prefetch=4,
            grid=(n_blocks,),
            in_specs=[
                pl.BlockSpec((bm, dp), lambda b, be, nx, nv, nu: (jnp.minimum(b, nu[0] - 1), 0)),
                hbm, hbm, hbm,
            ],
            out_specs=pl.BlockSpec((bm, dp), lambda b, be, nx, nv, nu: (b, 0)),
            scratch_shapes=[
                pltpu.VMEM((d, 2 * hid), BF16), pltpu.VMEM((hid, d), BF16),
                pltpu.VMEM((WEIGHT_SLOTS, d, hid), F32), pltpu.VMEM((WEIGHT_SLOTS, d, hid), F32),
                pltpu.VMEM((WEIGHT_SLOTS, hid, d), F32),
                pltpu.SMEM((1,), jnp.int32), pltpu.SemaphoreType.DMA((WEIGHT_SLOTS,)),
            ],
        ),
        out_shape=jax.ShapeDtypeStruct((n_rows, dp), jnp.uint32),
        compiler_params=_params(("arbitrary",)),
        name="experts",
    )(block_e, runs, n_valid, n_used, xb, w_gate, w_up, w_down)


def _combine_kernel(lat_ref, y0_ref, y1_ref, wc_ref, gate_ref, gf_ref, *rest, final):
    o_ref = rest[-1]
    wc = wc_ref[...]
    moe = wc[:, 0:1] * _unpack_rows(y0_ref[...]) + wc[:, 1:2] * _unpack_rows(y1_ref[...])
    out = lat_ref[...] + gate_ref[...] * moe
    if final:
        ms = jnp.mean(out * out, axis=-1, keepdims=True)
        out = out * lax.rsqrt(ms + EPS) * gf_ref[...]
    o_ref[...] = out


def _combine(lat, yg, wcol, mods, layer, batch0, g_final, seq, final, out_rows, tok0, prev_out):
    n, d = lat.shape
    tm = COMBINE_TM
    nt = n // tm
    t0 = tok0 // tm
    tiles_per_seq = seq // tm
    row = lambda i: layer * MOD_ROWS + batch0 + i // tiles_per_seq
    in_specs = [
        pl.BlockSpec((tm, d), lambda i: (i, 0)),
        pl.BlockSpec((tm, d // 2), lambda i: (i, 0)),
        pl.BlockSpec((tm, d // 2), lambda i: (nt + i, 0)),
        pl.BlockSpec((tm, LANES), lambda i: (i, 0)),
        pl.BlockSpec((None, 1, d), lambda i: (row(i), 0, 5)),
        pl.BlockSpec((1, d), lambda i: (0, 0)),
    ]
    args = [lat, yg, yg, wcol, mods, g_final]
    aliases = {}
    if prev_out is not None:
        in_specs.append(pl.BlockSpec(memory_space=pl.ANY))
        args.append(prev_out)
        aliases = {len(args) - 1: 0}
    return pl.pallas_call(
        functools.partial(_combine_kernel, final=final),
        grid=(nt,),
        in_specs=in_specs,
        out_specs=pl.BlockSpec((tm, d), lambda i: (t0 + i, 0)),
        out_shape=jax.ShapeDtypeStruct((out_rows, d), F32),
        input_output_aliases=aliases,
        compiler_params=_params(("parallel",)),
        name="combine",
    )(*args)


def _moe(lat, routed, mods, layer, batch0, w_gate, w_up, w_down, g_final, seq, final,
         out_rows, tok0, prev_out):
    h2, meta_i, wcol, counts = routed
    n = lat.shape[0]
    n_blocks = (2 * n) // MOE_BM + N_EXPERTS
    dest, block_e, runs, n_valid, n_used = _plan(counts[:, 0].astype(jnp.int32), meta_i, n_blocks)
    xb = _sc_dispatch(h2, dest, n_blocks * MOE_BM)
    yb = _experts(block_e, runs, n_valid, n_used, xb, w_gate, w_up, w_down, layer)
    yg = _sc_gather(yb, dest)
    return _combine(lat, yg, wcol, mods, layer, batch0, g_final, seq, final, out_rows, tok0, prev_out)


def _rope_tables(seq):
    quarter = HEAD_DIM // 4
    pos = jnp.arange(seq, dtype=F32)
    row_ids = jnp.floor(pos / GRID_W)
    col_ids = pos - row_ids * GRID_W
    inv = ROPE_BASE ** (-jnp.arange(quarter, dtype=F32) / quarter)
    ang_r = row_ids[:, None] * inv
    ang_c = col_ids[:, None] * inv
    zero = jnp.zeros_like(ang_r)
    cos = jnp.concatenate([jnp.cos(ang_r), jnp.cos(ang_r), jnp.cos(ang_c), jnp.cos(ang_c)], axis=-1)
    sa = jnp.concatenate([-jnp.sin(ang_r), zero, -jnp.sin(ang_c), zero], axis=-1)
    sb = jnp.concatenate([zero, jnp.sin(ang_r), zero, jnp.sin(ang_c)], axis=-1)
    rep = LANES // HEAD_DIM
    return tuple(jnp.tile(t, (1, rep)) for t in (cos, sa, sb))


def _router_weights(w_rg, b_rg, w_re, b_re):
    d = w_rg.shape[0]
    pad = EXPERTS_PER_GROUP - N_GROUPS
    wr_t = jnp.concatenate([w_rg.T, jnp.zeros((pad, d), F32), w_re.T], axis=0)
    br_t = jnp.concatenate([b_rg, jnp.full((pad,), NEG_INF, F32), b_re])[:, None]
    return wr_t, br_t


def kernel(x, c, ctx, c_ctx, w_ada, b_ada, g_norm1, g_norm2, g_final, w_in_even, attn_sink, g_sgu,
           w_spatial, b_spatial, w_out_even, w_in_odd, conv_w, w_out_odd, w_router_group,
           b_router_group, w_router_expert, b_router_expert, w_gate, w_up, w_down):
    b, s, d = x.shape
    n = b * s
    n_ctx = ctx.shape[1]
    depth = w_ada.shape[0]
    assert depth == 2 and b + 1 <= MOD_ROWS

    cond = jnp.concatenate([c, c_ctx[None, :], jnp.zeros((MOD_ROWS - b - 1, d), F32)], axis=0)
    mods = _ada(cond, w_ada, b_ada).reshape(depth * MOD_ROWS, 1, 6 * d)
    gf = g_final[None, :]

    lat = x.reshape(n, d)
    w_in_bf = w_in_even[0].astype(BF16)
    tabs = _rope_tables(s)
    qx, k, v, u, z = _even_in(lat, mods, 0, lambda i: i // (s // EVEN_TM), g_norm1[0][None, :], w_in_bf,
                              tabs, s // EVEN_TM, EVEN_TM)
    ones = jnp.ones((n_ctx, LANES), F32)
    zeros = jnp.zeros((n_ctx, LANES), F32)
    _, kc, vc, _, _ = _even_in(ctx.reshape(b * n_ctx, d), mods, 0, lambda i: b, g_norm1[0][None, :],
                               w_in_bf, (ones, zeros, zeros), 1, n_ctx)
    bsp_full = jnp.repeat(b_spatial[0].T, HEAD_DIM, axis=1)
    half = N_Q_HEADS // 2
    w_att = w_out_even[0][:Q_DIM].reshape(2, half, HEAD_DIM, d).transpose(1, 0, 2, 3).reshape(Q_DIM, d)
    w_out_bf = jnp.concatenate([w_att, w_out_even[0][Q_DIM:]], axis=0).astype(BF16)
    conv_w8 = jnp.concatenate([conv_w[0], jnp.zeros((8 - conv_w.shape[1], d), F32)], axis=0)
    w_in_odd_bf = w_in_odd[0].astype(BF16)
    w_out_odd_bf = w_out_odd[0].astype(BF16)
    wsp_bf = w_spatial[0].astype(BF16)
    wr0, br0 = _router_weights(w_router_group[0], b_router_group[0], w_router_expert[0], b_router_expert[0])
    wr1, br1 = _router_weights(w_router_group[1], b_router_group[1], w_router_expert[1], b_router_expert[1])

    part = n // MOE_PARTS
    out = None
    for p in range(MOE_PARTS):
        tok0 = p * part
        batch0 = tok0 // s
        lat_p, *routed = _even_mix(lat, qx, k, v, kc, vc, u, z, attn_sink[0], g_sgu[0][None, :],
                                   wsp_bf, bsp_full, w_out_bf, mods, s, n_ctx,
                                   g_norm2[0][None, :], wr0, br0, tok0, part)
        lat_p = _moe(lat_p, routed, mods, 0, batch0, w_gate, w_up, w_down, gf, s, False, part, 0, None)
        lat_p, *routed = _odd_mix(lat_p, mods, 1, g_norm1[1][None, :], w_in_odd_bf, conv_w8,
                                  w_out_odd_bf, s, g_norm2[1][None, :], wr1, br1, batch0)
        out = _moe(lat_p, routed, mods, 1, batch0, w_gate, w_up, w_down, gf, s, True, n, tok0, out)
    return out.reshape(b, s, d)
```

```python
import functools

import jax
import jax.numpy as jnp
from jax import lax
from jax.experimental import pallas as pl
from jax.experimental.pallas import tpu as pltpu
from jax.experimental.pallas import tpu_sc as plsc

F32 = jnp.float32
BF16 = jnp.bfloat16
HIGHEST = lax.Precision.HIGHEST

GRID_W = 64
N_Q_HEADS = 8
N_KV_HEADS = 2
HEAD_DIM = 64
ATT_BLOCK = 128
ROPE_BASE = 10000.0
Q_DIM = N_Q_HEADS * HEAD_DIM
KV_DIM = N_KV_HEADS * HEAD_DIM
SG_GROUPS = 8
SG_WIDTH = SG_GROUPS * HEAD_DIM
N_GROUPS = 4
EXPERTS_PER_GROUP = 8
N_EXPERTS = N_GROUPS * EXPERTS_PER_GROUP
EPS = 1e-6
NEG_INF = -1e30

LANES = 128
SC_CORES = 2
SC_SUBCORES = 16
SC_WORKERS = SC_CORES * SC_SUBCORES
SC_CHUNK = 32
MOD_ROWS = 8
ROUTER_ROWS = EXPERTS_PER_GROUP + N_EXPERTS
VMEM_LIMIT = 56 * 1024 * 1024

ADA_TN = 768
EVEN_TM = 1024
EVEN_SUB = 512
ATT_TQ = 512
ODD_TM = 512
MOE_BM = 1024
MOE_QUANTUM = 128
MOE_PARTS = 2
WEIGHT_SLOTS = 3
COMBINE_TM = 1024


def _params(sem):
    return pltpu.CompilerParams(dimension_semantics=sem, vmem_limit_bytes=VMEM_LIMIT)


def _rms_mod(x, g, shift, scale):
    ms = jnp.mean(x * x, axis=-1, keepdims=True)
    return (x * lax.rsqrt(ms + EPS)) * (g * (1.0 + scale)) + shift


def _pack_rounded(a):
    w = a.shape[1] // 2
    hi = pltpu.bitcast(a[:, :w], jnp.uint32)
    lo = pltpu.bitcast(a[:, w:], jnp.uint32)
    return hi | (lo >> 16)


def _pack_rows(a):
    return _pack_rounded(a.astype(BF16).astype(F32))


def _unpack_rows(p):
    hi = pltpu.bitcast(p & jnp.uint32(0xFFFF0000), F32)
    lo = pltpu.bitcast(p << 16, F32)
    return jnp.concatenate([hi, lo], axis=1)


def _ada_kernel(a_ref, w_ref, b_ref, o_ref):
    a = a_ref[...]
    s = a * (1.0 / (1.0 + jnp.exp(-a)))
    o_ref[0] = jnp.dot(s, w_ref[0], preferred_element_type=F32, precision=HIGHEST) + b_ref[0]


def _ada(cond, w_ada, b_ada):
    depth, d, six_d = w_ada.shape
    return pl.pallas_call(
        _ada_kernel,
        grid=(depth, six_d // ADA_TN),
        in_specs=[
            pl.BlockSpec((MOD_ROWS, d), lambda l, j: (0, 0)),
            pl.BlockSpec((1, d, ADA_TN), lambda l, j: (l, 0, j)),
            pl.BlockSpec((1, 1, ADA_TN), lambda l, j: (l, 0, j)),
        ],
        out_specs=pl.BlockSpec((1, MOD_ROWS, ADA_TN), lambda l, j: (l, 0, j)),
        out_shape=jax.ShapeDtypeStruct((depth, MOD_ROWS, six_d), F32),
        compiler_params=_params(("arbitrary", "arbitrary")),
        name="ada",
    )(cond, w_ada, b_ada.reshape(depth, 1, six_d))


def _even_in_kernel(x_ref, sh_ref, sc_ref, g_ref, w_ref, cos_ref, sa_ref, sb_ref,
                    qx_ref, k_ref, v_ref, u_ref, z_ref):
    tm = x_ref.shape[0]
    sub = min(tm, EVEN_SUB)
    scale = HEAD_DIM ** -0.5
    low = lax.broadcasted_iota(jnp.int32, (sub, LANES), 1) < HEAD_DIM
    heads_per_kv = N_Q_HEADS // N_KV_HEADS
    u0 = Q_DIM + 2 * KV_DIM

    for r0 in range(0, tm, sub):
        rows = slice(r0, r0 + sub)
        h = _rms_mod(x_ref[rows, :], g_ref[...], sh_ref[...], sc_ref[...])
        p = jnp.dot(h.astype(BF16), w_ref[...], preferred_element_type=F32)
        cos, sa, sb = cos_ref[rows, :], sa_ref[rows, :], sb_ref[rows, :]

        def rope(t):
            return t * cos + pltpu.roll(t, LANES - 16, 1) * sa + pltpu.roll(t, 16, 1) * sb

        for cblk in range(Q_DIM // LANES):
            t = rope(p[:, cblk * LANES:(cblk + 1) * LANES]) * scale
            sw = pltpu.roll(t, HEAD_DIM, 1)
            zero = jnp.zeros_like(t)
            if (2 * cblk) // heads_per_kv == 0:
                first, second = jnp.where(low, t, zero), jnp.where(low, sw, zero)
            else:
                first, second = jnp.where(low, zero, sw), jnp.where(low, zero, t)
            qx_ref[rows, (2 * cblk) * LANES:(2 * cblk + 1) * LANES] = first.astype(BF16)
            qx_ref[rows, (2 * cblk + 1) * LANES:(2 * cblk + 2) * LANES] = second.astype(BF16)

        k_ref[rows, :] = rope(p[:, Q_DIM:Q_DIM + KV_DIM]).astype(BF16)
        v_ref[rows, :] = p[:, Q_DIM + KV_DIM:Q_DIM + 2 * KV_DIM].astype(BF16)
        u_ref[rows, :] = p[:, u0:u0 + SG_WIDTH]
        z_ref[rows, :] = p[:, u0 + SG_WIDTH:u0 + 2 * SG_WIDTH]


def _even_in(x2d, mods, layer, mod_row_fn, g, w_bf, tabs, tab_blocks, tm):
    n, d = x2d.shape
    ein = w_bf.shape[1]
    cos, sa, sb = tabs
    row = lambda i: layer * MOD_ROWS + mod_row_fn(i)
    tab_spec = pl.BlockSpec((tm, LANES), lambda i: (i % tab_blocks, 0))
    qx_dim = N_Q_HEADS * LANES
    return pl.pallas_call(
        _even_in_kernel,
        grid=(n // tm,),
        in_specs=[
            pl.BlockSpec((tm, d), lambda i: (i, 0)),
            pl.BlockSpec((None, 1, d), lambda i: (row(i), 0, 0)),
            pl.BlockSpec((None, 1, d), lambda i: (row(i), 0, 1)),
            pl.BlockSpec((1, d), lambda i: (0, 0)),
            pl.BlockSpec((d, ein), lambda i: (0, 0)),
            tab_spec, tab_spec, tab_spec,
        ],
        out_specs=[
            pl.BlockSpec((tm, qx_dim), lambda i: (i, 0)),
            pl.BlockSpec((tm, KV_DIM), lambda i: (i, 0)),
            pl.BlockSpec((tm, KV_DIM), lambda i: (i, 0)),
            pl.BlockSpec((tm, SG_WIDTH), lambda i: (i, 0)),
            pl.BlockSpec((tm, SG_WIDTH), lambda i: (i, 0)),
        ],
        out_shape=[
            jax.ShapeDtypeStruct((n, qx_dim), BF16),
            jax.ShapeDtypeStruct((n, KV_DIM), BF16),
            jax.ShapeDtypeStruct((n, KV_DIM), BF16),
            jax.ShapeDtypeStruct((n, SG_WIDTH), F32),
            jax.ShapeDtypeStruct((n, SG_WIDTH), F32),
        ],
        compiler_params=_params(("parallel",)),
        name="even_in",
    )(x2d, mods, mods, g, w_bf, cos, sa, sb)


def _gelu(x):
    return 0.5 * x * (1.0 + lax.erf(x * (2.0 ** -0.5)))


def _even_mix_kernel(sink_ref, lat_ref, qx_ref, km_ref, kp_ref, kn_ref, vm_ref, vp_ref, vn_ref,
                     kc_ref, vc_ref, u_ref, z_ref, gsgu_ref, wsp_ref, bsp_ref, wout_ref, gate_ref,
                     sh2_ref, sc2_ref, g2_ref, wr_ref, br_ref,
                     o_ref, h_ref, mi_ref, wc_ref, cnt_ref,
                     kband, vband, mixin, carry, upper, *, tiles_per_seq, tiles_per_part):
    i = pl.program_id(0)
    tq = qx_ref.shape[0]

    @pl.when(i == 0)
    def _():
        _route_init(carry, upper)

    @pl.when(i % tiles_per_part == 0)
    def _():
        carry[...] = jnp.zeros_like(carry)

    blk = ATT_BLOCK
    nsub = tq // blk
    n_ctx = kc_ref.shape[0]
    first = (i % tiles_per_seq) == 0
    last = (i % tiles_per_seq) == tiles_per_seq - 1

    kband[0:blk] = kp_ref[...]
    kband[blk:blk + tq] = km_ref[...]
    kband[blk + tq:] = kn_ref[...]
    vband[0:blk] = vp_ref[...]
    vband[blk:blk + tq] = vm_ref[...]
    vband[blk + tq:] = vn_ref[...]

    rows = N_Q_HEADS * blk
    tok = lax.broadcasted_iota(jnp.int32, (rows, blk), 0) & (blk - 1)
    col = lax.broadcasted_iota(jnp.int32, (rows, blk), 1)
    tri_prev = col >= tok
    tri_next = col <= tok
    head = lax.broadcasted_iota(jnp.int32, (rows, 1), 0) // blk
    sink_col = jnp.zeros((rows, 1), F32)
    for hd in range(N_Q_HEADS):
        sink_col = jnp.where(head == hd, sink_ref[hd], sink_col)
    lane_low = lax.broadcasted_iota(jnp.int32, (blk, LANES), 1) < HEAD_DIM
    ones = jnp.ones((n_ctx + 3 * blk, LANES), BF16)
    nt = (((1,), (1,)), ((), ()))

    def sub_block(j, c):
        r0 = pl.multiple_of(j * blk, blk)
        ok_prev = jnp.logical_not(jnp.logical_and(first, j == 0))
        ok_next = jnp.logical_not(jnp.logical_and(last, j == nsub - 1))
        qs = jnp.concatenate([qx_ref[pl.ds(r0, blk), hd * LANES:(hd + 1) * LANES]
                              for hd in range(N_Q_HEADS)], axis=0)
        kall = jnp.concatenate([kc_ref[...], kband[pl.ds(r0, 3 * blk), :]], axis=0)
        vall = jnp.concatenate([vc_ref[...], vband[pl.ds(r0, 3 * blk), :]], axis=0)
        s = lax.dot_general(qs, kall, nt, preferred_element_type=F32)
        c0 = n_ctx
        s = jnp.concatenate([
            s[:, :c0],
            jnp.where(jnp.logical_and(tri_prev, ok_prev), s[:, c0:c0 + blk], NEG_INF),
            s[:, c0 + blk:c0 + 2 * blk],
            jnp.where(jnp.logical_and(tri_next, ok_next), s[:, c0 + 2 * blk:], NEG_INF),
        ], axis=1)
        m = jnp.maximum(jnp.max(s, axis=-1, keepdims=True), sink_col)
        p = jnp.exp(s - m).astype(BF16)
        o = jnp.dot(p, jnp.concatenate([vall, ones], axis=1), preferred_element_type=F32)
        att = o[:, :LANES] / (o[:, LANES:] + jnp.exp(sink_col - m))
        half = N_Q_HEADS // 2
        for hd in range(half):
            pair = jnp.where(lane_low, att[hd * blk:(hd + 1) * blk], att[(hd + half) * blk:(hd + half + 1) * blk])
            mixin[pl.ds(r0, blk), hd * LANES:(hd + 1) * LANES] = pair.astype(BF16)

        ug = _gelu(u_ref[pl.ds(r0, blk), :])
        zg = _gelu(z_ref[pl.ds(r0, blk), :])
        mu = jnp.mean(zg, axis=-1, keepdims=True)
        zc = zg - mu
        zn = zc * lax.rsqrt(jnp.mean(zc * zc, axis=-1, keepdims=True) + EPS) * gsgu_ref[...]
        for pair in range(SG_GROUPS // 2):
            zp = zn[:, pair * LANES:(pair + 1) * LANES]
            zero = jnp.zeros_like(zp)
            lo = jnp.where(lane_low, zp, zero).astype(BF16)
            hi = jnp.where(lane_low, zero, zp).astype(BF16)
            sg = (jnp.dot(wsp_ref[2 * pair], lo, preferred_element_type=F32)
                  + jnp.dot(wsp_ref[2 * pair + 1], hi, preferred_element_type=F32)
                  + bsp_ref[:, pair * LANES:(pair + 1) * LANES])
            mixin[pl.ds(r0, blk), Q_DIM + pair * LANES:Q_DIM + (pair + 1) * LANES] = (
                ug[:, pair * LANES:(pair + 1) * LANES] * sg).astype(BF16)
        return c

    lax.fori_loop(0, nsub, sub_block, 0, unroll=True)
    mix = jnp.dot(mixin[...], wout_ref[...], preferred_element_type=F32)
    lat = lat_ref[...] + gate_ref[...] * mix
    o_ref[...] = lat
    _route_tile(lat, sh2_ref[...], sc2_ref[...], g2_ref[...], wr_ref, br_ref, carry, upper,
                h_ref, mi_ref, wc_ref, cnt_ref)


def _even_mix(lat, qx, k, v, kc, vc, u, z, sink, g_sgu, wsp_bf, bsp_full, wout_bf, mods, seq, ctx_len,
              g2, wr_t, br_t, tok0, n):
    d = lat.shape[1]
    tq = ATT_TQ
    tiles_per_seq = seq // tq
    sub = tq // ATT_BLOCK
    nblk = lat.shape[0] // ATT_BLOCK
    t0 = tok0 // tq
    main = lambda w: pl.BlockSpec((tq, w), lambda i: (t0 + i, 0))
    prev = pl.BlockSpec((ATT_BLOCK, KV_DIM), lambda i: (jnp.maximum((t0 + i) * sub - 1, 0), 0))
    nxt = pl.BlockSpec((ATT_BLOCK, KV_DIM), lambda i: (jnp.minimum((t0 + i + 1) * sub, nblk - 1), 0))
    batch = lambda t: (t0 + t) // tiles_per_seq
    ctxs = pl.BlockSpec((ctx_len, KV_DIM), lambda i: (batch(i), 0))
    const = lambda shape: pl.BlockSpec(shape, lambda i: (0,) * len(shape), pipeline_mode=pl.Buffered(1))
    r_in, r_out, r_shapes, r_scratch = _route_specs(n, d, tq, lambda i: i, batch)
    return pl.pallas_call(
        functools.partial(_even_mix_kernel, tiles_per_seq=tiles_per_seq, tiles_per_part=n // tq),
        grid=(n // tq,),
        in_specs=[
            pl.BlockSpec(memory_space=pltpu.SMEM),
            main(d), main(qx.shape[1]),
            main(KV_DIM), prev, nxt,
            main(KV_DIM), prev, nxt,
            ctxs, ctxs,
            main(SG_WIDTH), main(SG_WIDTH),
            const((1, SG_WIDTH)), const(wsp_bf.shape), const(bsp_full.shape), const(wout_bf.shape),
            pl.BlockSpec((None, 1, d), lambda i: (batch(i), 0, 2)),
        ] + r_in,
        out_specs=[pl.BlockSpec((tq, d), lambda i: (i, 0))] + r_out,
        out_shape=[jax.ShapeDtypeStruct((n, d), F32)] + r_shapes,
        scratch_shapes=[
            pltpu.VMEM((tq + 2 * ATT_BLOCK, KV_DIM), BF16),
            pltpu.VMEM((tq + 2 * ATT_BLOCK, KV_DIM), BF16),
            pltpu.VMEM((tq, Q_DIM + SG_WIDTH), BF16),
        ] + r_scratch,
        compiler_params=_params(("arbitrary",)),
        name="even_mix",
    )(sink, lat, qx, k, k, k, v, v, v, kc, vc, u, z, g_sgu, wsp_bf, bsp_full, wout_bf, mods,
      mods, mods, g2, wr_t, br_t)


def _odd_mix_kernel(x_ref, xp_ref, sh_ref, sc_ref, gate_ref, g_ref, win_ref, cw_ref, wout_ref,
                    sh2_ref, sc2_ref, g2_ref, wr_ref, br_ref,
                    o_ref, h_ref, mi_ref, wc_ref, cnt_ref,
                    y_s, bg_s, tail_s, carry, upper, *, tiles_per_seq, tiles_per_part):
    i = pl.program_id(0)
    n_tiles = pl.num_programs(0) - 1
    tm, d = x_ref.shape
    cur = i % 2
    prv = 1 - cur

    @pl.when(i == 0)
    def _():
        _route_init(carry, upper)

    tail_s[...] = y_s[cur, tm - 8:tm, :]

    @pl.when(i < n_tiles)
    def _():
        h = _rms_mod(x_ref[...], g_ref[...], sh_ref[...], sc_ref[...])
        p = jnp.dot(h.astype(BF16), win_ref[...], preferred_element_type=F32)
        bg_s[cur] = p[:, 0:d]
        y_s[cur] = p[:, d:2 * d] * p[:, 2 * d:3 * d]

    @pl.when(i >= 1)
    def _():
        t_prev = i - 1
        first = (t_prev % tiles_per_seq) == 0
        last = (t_prev % tiles_per_seq) == tiles_per_seq - 1
        y = y_s[prv]
        left = jnp.where(first, 0.0, tail_s[7:8, :])
        right = jnp.where(last, 0.0, y_s[cur, 0:1, :])
        ridx = lax.broadcasted_iota(jnp.int32, (tm, d), 0)
        y_dn = jnp.where(ridx == 0, left, pltpu.roll(y, 1, 0))
        y_up = jnp.where(ridx == tm - 1, right, pltpu.roll(y, tm - 1, 0))
        conv = y_dn * cw_ref[0:1, :] + y * cw_ref[1:2, :] + y_up * cw_ref[2:3, :]
        mix = jnp.dot((bg_s[prv] * conv).astype(BF16), wout_ref[...], preferred_element_type=F32)
        lat = xp_ref[...] + gate_ref[...] * mix
        o_ref[...] = lat

        @pl.when(t_prev % tiles_per_part == 0)
        def _():
            carry[...] = jnp.zeros_like(carry)

        _route_tile(lat, sh2_ref[...], sc2_ref[...], g2_ref[...], wr_ref, br_ref, carry, upper,
                    h_ref, mi_ref, wc_ref, cnt_ref)


def _odd_mix(lat, mods, layer, g, win_bf, conv_w8, wout_bf, seq, g2, wr_t, br_t, batch0):
    n, d = lat.shape
    tm = ODD_TM
    nt = n // tm
    tiles_per_seq = seq // tm
    cur = lambda i: jnp.minimum(i, nt - 1)
    prv = lambda i: jnp.maximum(i - 1, 0)
    row = lambda t: layer * MOD_ROWS + batch0 + t // tiles_per_seq
    const = lambda shape: pl.BlockSpec(shape, lambda i: (0,) * len(shape), pipeline_mode=pl.Buffered(1))
    r_in, r_out, r_shapes, r_scratch = _route_specs(n, d, tm, prv, row)
    return pl.pallas_call(
        functools.partial(_odd_mix_kernel, tiles_per_seq=tiles_per_seq, tiles_per_part=nt),
        grid=(nt + 1,),
        in_specs=[
            pl.BlockSpec((tm, d), lambda i: (cur(i), 0)),
            pl.BlockSpec((tm, d), lambda i: (prv(i), 0)),
            pl.BlockSpec((None, 1, d), lambda i: (row(cur(i)), 0, 0)),
            pl.BlockSpec((None, 1, d), lambda i: (row(cur(i)), 0, 1)),
            pl.BlockSpec((None, 1, d), lambda i: (row(prv(i)), 0, 2)),
            const((1, d)), const(win_bf.shape), const(conv_w8.shape), const(wout_bf.shape),
        ] + r_in,
        out_specs=[pl.BlockSpec((tm, d), lambda i: (prv(i), 0))] + r_out,
        out_shape=[jax.ShapeDtypeStruct((n, d), F32)] + r_shapes,
        scratch_shapes=[
            pltpu.VMEM((2, tm, d), F32),
            pltpu.VMEM((2, tm, d), F32),
            pltpu.VMEM((8, d), F32),
        ] + r_scratch,
        compiler_params=_params(("arbitrary",)),
        name="odd_mix",
    )(lat, lat, mods, mods, mods, g, win_bf, conv_w8, wout_bf, mods, mods, g2, wr_t, br_t)


def _route_init(carry, upper):
    tm = upper.shape[0]
    carry[...] = jnp.zeros_like(carry)
    r_i = lax.broadcasted_iota(jnp.int32, (tm, tm), 0)
    c_i = lax.broadcasted_iota(jnp.int32, (tm, tm), 1)
    upper[...] = jnp.where(r_i < c_i, 1.0, 0.0).astype(BF16)


def _route_tile(lat, sh, sc, g, wr_ref, br_ref, carry, upper, h_ref, mi_ref, wc_ref, cnt_ref):
    tm = lat.shape[0]
    epg = EXPERTS_PER_GROUP
    h = _rms_mod(lat, g, sh, sc)
    h_hi = h.astype(BF16)
    h_hi_f = h_hi.astype(F32)
    h_ref[...] = _pack_rounded(h_hi_f)
    h_lo = (h - h_hi_f).astype(BF16)
    w = wr_ref[...]
    w1 = w.astype(BF16).astype(F32)
    r1 = w - w1
    w2 = r1.astype(BF16).astype(F32)
    w3 = r1 - w2
    nt = (((1,), (1,)), ((), ()))
    nr = w.shape[0]
    w123 = jnp.concatenate([w1, w2, w3, jnp.zeros((8, w.shape[1]), F32)], axis=0).astype(BF16)
    w12 = jnp.concatenate([w1, w2], axis=0).astype(BF16)
    p_hi = lax.dot_general(w123, h_hi, nt, preferred_element_type=F32)
    p_lo = lax.dot_general(w12, h_lo, nt, preferred_element_type=F32)
    lg = ((p_hi[2 * nr:3 * nr] + p_lo[nr:2 * nr]) + (p_hi[nr:2 * nr] + p_lo[0:nr])) + p_hi[0:nr] + br_ref[...]
    io8 = lax.broadcasted_iota(jnp.int32, (epg, tm), 0)
    gl = lg[0:epg]
    gmax = jnp.max(gl, axis=0, keepdims=True)
    g_idx = jnp.min(jnp.where(gl == gmax, io8, epg), axis=0, keepdims=True)
    g_w = 1.0 / jnp.sum(jnp.exp(gl - gmax), axis=0, keepdims=True)
    e_sel = lg[epg:2 * epg]
    for gi in range(1, N_GROUPS):
        e_sel = jnp.where(g_idx == gi, lg[(gi + 1) * epg:(gi + 2) * epg], e_sel)
    v0 = jnp.max(e_sel, axis=0, keepdims=True)
    i0 = jnp.min(jnp.where(e_sel == v0, io8, epg), axis=0, keepdims=True)
    rest = jnp.where(io8 == i0, -jnp.inf, e_sel)
    v1 = jnp.max(rest, axis=0, keepdims=True)
    i1 = jnp.min(jnp.where(rest == v1, io8, epg), axis=0, keepdims=True)
    t = jnp.exp(v1 - v0)
    w0 = g_w / (1.0 + t)
    w1 = g_w * t / (1.0 + t)
    e0 = g_idx * epg + i0
    e1 = g_idx * epg + i1

    io32 = lax.broadcasted_iota(jnp.int32, (N_EXPERTS, tm), 0)
    hit0 = io32 == e0
    hit1 = io32 == e1
    onehot = jnp.where(hit0 | hit1, 1.0, 0.0)
    cum = jnp.dot(onehot.astype(BF16), upper[...], preferred_element_type=F32) + carry[...]
    rank0 = jnp.sum(jnp.where(hit0, cum, 0.0), axis=0, keepdims=True).astype(jnp.int32)
    rank1 = jnp.sum(jnp.where(hit1, cum, 0.0), axis=0, keepdims=True).astype(jnp.int32)
    carry[...] = carry[...] + jnp.sum(onehot, axis=1, keepdims=True)
    cnt_ref[...] = jnp.broadcast_to(carry[...], cnt_ref.shape)

    mi_ref[...] = jnp.where(io8 == 0, e0, jnp.where(io8 == 1, e1, jnp.where(io8 == 2, rank0,
                            jnp.where(io8 == 3, rank1, 0))))
    io128 = lax.broadcasted_iota(jnp.int32, (LANES, tm), 0)
    wrow = jnp.where(io128 == 0, w0, jnp.where(io128 == 1, w1, 0.0))
    wc_ref[...] = wrow.T


def _route_specs(n, d, tm, tile_of, mods_row):
    in_specs = [
        pl.BlockSpec((None, 1, d), lambda i: (mods_row(tile_of(i)), 0, 3)),
        pl.BlockSpec((None, 1, d), lambda i: (mods_row(tile_of(i)), 0, 4)),
        pl.BlockSpec((1, d), lambda i: (0, 0), pipeline_mode=pl.Buffered(1)),
        pl.BlockSpec((ROUTER_ROWS, d), lambda i: (0, 0), pipeline_mode=pl.Buffered(1)),
        pl.BlockSpec((ROUTER_ROWS, 1), lambda i: (0, 0), pipeline_mode=pl.Buffered(1)),
    ]
    out_specs = [
        pl.BlockSpec((tm, d // 2), lambda i: (tile_of(i), 0)),
        pl.BlockSpec((8, tm), lambda i: (0, tile_of(i))),
        pl.BlockSpec((tm, LANES), lambda i: (tile_of(i), 0)),
        pl.BlockSpec((N_EXPERTS, LANES), lambda i: (0, 0)),
    ]
    out_shapes = [
        jax.ShapeDtypeStruct((n, d // 2), jnp.uint32),
        jax.ShapeDtypeStruct((8, n), jnp.int32),
        jax.ShapeDtypeStruct((n, LANES), F32),
        jax.ShapeDtypeStruct((N_EXPERTS, LANES), F32),
    ]
    scratch = [pltpu.VMEM((N_EXPERTS, 1), F32), pltpu.VMEM((tm, tm), BF16)]
    return in_specs, out_specs, out_shapes, scratch


def _plan_kernel(cnt_ref, mi_ref, dest_ref, be_ref, runs_ref, nv_ref, nu_ref, ps_ref, *, n_blocks):
    bm = MOE_BM

    def per_expert(e, carry):
        blk0, n_runs = carry
        cnt = cnt_ref[e]
        nb = (cnt + bm - 1) // bm
        ps_ref[e] = blk0 * bm

        def fill(b, c):
            be_ref[b] = e
            nv_ref[b] = jnp.minimum(cnt - (b - blk0) * bm, bm)
            return c

        lax.fori_loop(blk0, blk0 + nb, fill, 0)

        @pl.when(nb > 0)
        def _():
            runs_ref[n_runs] = e

        return blk0 + nb, n_runs + jnp.where(nb > 0, 1, 0)

    n_used, n_runs = lax.fori_loop(0, N_EXPERTS, per_expert, (0, 0))
    nu_ref[0] = n_used
    nu_ref[1] = n_runs
    last_e = be_ref[jnp.maximum(n_used - 1, 0)]

    def fill_tail(b, c):
        be_ref[b] = last_e
        nv_ref[b] = 0
        return c

    lax.fori_loop(n_used, n_blocks, fill_tail, 0)

    def fill_runs(k, c):
        runs_ref[k] = last_e
        return c

    lax.fori_loop(n_runs, N_EXPERTS, fill_runs, 0)

    e01 = mi_ref[0:2, :]
    dest = mi_ref[2:4, :]
    for e in range(N_EXPERTS):
        dest = dest + jnp.where(e01 == e, ps_ref[e], 0)
    dest_ref[...] = dest


def _plan(counts, meta_i, n_blocks):
    n = meta_i.shape[1]
    return pl.pallas_call(
        functools.partial(_plan_kernel, n_blocks=n_blocks),
        in_specs=[pl.BlockSpec(memory_space=pltpu.SMEM), pl.BlockSpec(memory_space=pltpu.VMEM)],
        out_specs=[pl.BlockSpec(memory_space=pltpu.VMEM)] + [pl.BlockSpec(memory_space=pltpu.SMEM)] * 4,
        out_shape=[
            jax.ShapeDtypeStruct((2, n), jnp.int32),
            jax.ShapeDtypeStruct((n_blocks,), jnp.int32),
            jax.ShapeDtypeStruct((N_EXPERTS,), jnp.int32),
            jax.ShapeDtypeStruct((n_blocks,), jnp.int32),
            jax.ShapeDtypeStruct((2,), jnp.int32),
        ],
        scratch_shapes=[pltpu.SMEM((N_EXPERTS,), jnp.int32)],
        compiler_params=pltpu.CompilerParams(vmem_limit_bytes=VMEM_LIMIT),
        name="plan",
    )(counts, meta_i)


def _sc_mesh():
    return plsc.VectorSubcoreMesh(core_axis_name="c", subcore_axis_name="s",
                                  num_cores=SC_CORES, num_subcores=SC_SUBCORES)


def _sc_worker():
    return lax.axis_index("s") * SC_CORES + lax.axis_index("c")


def _sc_dispatch(h2, dest, n_rows):
    n, d = h2.shape
    c = SC_CHUNK
    per_w = n // SC_WORKERS
    nchunk = per_w // c
    idx = dest.reshape(2, SC_WORKERS, nchunk, c)

    @functools.partial(
        pl.kernel, mesh=_sc_mesh(), out_type=jax.ShapeDtypeStruct((n_rows, d), h2.dtype),
        scratch_types=[pltpu.VMEM((nchunk, c), jnp.int32), pltpu.VMEM((nchunk, c), jnp.int32),
                       pltpu.VMEM((2, c, d), h2.dtype),
                       pltpu.SemaphoreType.DMA((2,)), pltpu.SemaphoreType.DMA((2,))])
    def k(h_hbm, idx_hbm, xb_hbm, idx0_v, idx1_v, rows_v, gsem, ssem):
        wid = _sc_worker()
        base = wid * per_w
        idx_v = (idx0_v, idx1_v)
        for kk in range(2):
            pltpu.sync_copy(idx_hbm.at[kk, wid], idx_v[kk])

        def get(j, slot):
            return pltpu.make_async_copy(h_hbm.at[pl.ds(base + j * c, c)], rows_v.at[slot], gsem.at[slot])

        def put(j, slot, kk):
            return pltpu.make_async_copy(rows_v.at[slot], xb_hbm.at[idx_v[kk].at[j]], ssem.at[slot])

        get(0, 0).start()

        @pl.loop(0, nchunk, step=2)
        def _(j):
            for slot in range(2):
                jj = j + slot
                get(jj, slot).wait()

                @pl.when(jj >= 1)
                def _():
                    for kk in range(2):
                        put(jj - 1, 1 - slot, kk).wait()

                @pl.when(jj + 1 < nchunk)
                def _():
                    get(jj + 1, 1 - slot).start()

                for kk in range(2):
                    put(jj, slot, kk).start()

        for kk in range(2):
            put(nchunk - 1, (nchunk - 1) % 2, kk).wait()

    return k(h2, idx)


def _sc_gather(y, dest):
    d = y.shape[1]
    total = dest.shape[0] * dest.shape[1]
    c = SC_CHUNK
    per_w = total // SC_WORKERS
    nchunk = per_w // c
    idx = dest.reshape(SC_WORKERS, nchunk, c)

    @functools.partial(
        pl.kernel, mesh=_sc_mesh(), out_type=jax.ShapeDtypeStruct((total, d), y.dtype),
        scratch_types=[pltpu.VMEM((nchunk, c), jnp.int32), pltpu.VMEM((2, c, d), y.dtype),
                       pltpu.SemaphoreType.DMA((2,)), pltpu.SemaphoreType.DMA((2,))])
    def k(y_hbm, idx_hbm, out_hbm, idx_v, rows_v, gsem, ssem):
        wid = _sc_worker()
        base = wid * per_w
        pltpu.sync_copy(idx_hbm.at[wid], idx_v)

        def get(j, slot):
            return pltpu.make_async_copy(y_hbm.at[idx_v.at[j]], rows_v.at[slot], gsem.at[slot])

        def put(j, slot):
            return pltpu.make_async_copy(rows_v.at[slot], out_hbm.at[pl.ds(base + j * c, c)], ssem.at[slot])

        get(0, 0).start()

        @pl.loop(0, nchunk, step=2)
        def _(j):
            for slot in range(2):
                jj = j + slot
                get(jj, slot).wait()

                @pl.when(jj >= 1)
                def _():
                    put(jj - 1, 1 - slot).wait()

                @pl.when(jj + 1 < nchunk)
                def _():
                    get(jj + 1, 1 - slot).start()

                put(jj, slot).start()

        put(nchunk - 1, (nchunk - 1) % 2).wait()

    return k(y, idx)


def _expert_kernel(be_ref, runs_ref, nv_ref, nu_ref, x_ref, wg_hbm, wu_hbm, wd_hbm, y_ref,
                   wgu_s, wd_s, stg_g, stg_u, stg_d, run_s, sems, *, layer):
    b = pl.program_id(0)
    hid = stg_g.shape[2]
    e = be_ref[b]
    n_runs = nu_ref[1]
    changed = jnp.logical_or(b == 0, e != be_ref[jnp.maximum(b - 1, 0)])

    def fetch(run):
        expert = runs_ref[run]
        slot = run % WEIGHT_SLOTS
        return (pltpu.make_async_copy(wg_hbm.at[layer, expert], stg_g.at[slot], sems.at[slot]),
                pltpu.make_async_copy(wu_hbm.at[layer, expert], stg_u.at[slot], sems.at[slot]),
                pltpu.make_async_copy(wd_hbm.at[layer, expert], stg_d.at[slot], sems.at[slot]))

    @pl.when(b == 0)
    def _():
        for r in range(WEIGHT_SLOTS - 1):
            @pl.when(r < n_runs)
            def _():
                for cp in fetch(r):
                    cp.start()

    @pl.when(changed)
    def _():
        run = jnp.where(b == 0, 0, run_s[0] + 1)
        run_s[0] = run
        for cp in fetch(run):
            cp.wait()

        ahead = run + WEIGHT_SLOTS - 1

        @pl.when(ahead < n_runs)
        def _():
            for cp in fetch(ahead):
                cp.start()

        slot = run % WEIGHT_SLOTS
        wgu_s[:, 0:hid] = stg_g[slot].astype(BF16)
        wgu_s[:, hid:2 * hid] = stg_u[slot].astype(BF16)
        wd_s[...] = stg_d[slot].astype(BF16)

    bm, dp = x_ref.shape
    nv = nv_ref[b]
    in_use = b < nu_ref[0]

    def run(rows):
        live = lax.broadcasted_iota(jnp.int32, (rows, dp), 0) < nv
        x = _unpack_rows(jnp.where(live, x_ref[0:rows, :], jnp.uint32(0)))
        gu = jnp.dot(x.astype(BF16), wgu_s[...], preferred_element_type=F32)
        gate = gu[:, 0:hid]
        act = gate * (1.0 / (1.0 + jnp.exp(-gate))) * gu[:, hid:2 * hid]
        y_ref[0:rows, :] = _pack_rows(jnp.dot(act.astype(BF16), wd_s[...], preferred_element_type=F32))

    n_quanta = bm // MOE_QUANTUM
    for q in range(1, n_quanta + 1):
        rows = q * MOE_QUANTUM

        @pl.when(jnp.logical_and(in_use, jnp.logical_and(nv > rows - MOE_QUANTUM, nv <= rows)))
        def _(rows=rows):
            run(rows)
            if rows < bm:
                y_ref[rows:bm, :] = jnp.zeros((bm - rows, dp), y_ref.dtype)

    @pl.when(jnp.logical_not(in_use))
    def _():
        y_ref[...] = jnp.zeros_like(y_ref)


def _experts(block_e, runs, n_valid, n_used, xb, w_gate, w_up, w_down, layer):
    n_rows, dp = xb.shape
    d, hid = w_gate.shape[2], w_gate.shape[3]
    bm = MOE_BM
    n_blocks = n_rows // bm
    hbm = pl.BlockSpec(memory_space=pl.ANY)
    return pl.pallas_call(
        functools.partial(_expert_kernel, layer=layer),
        grid_spec=pltpu.PrefetchScalarGridSpec(
            num_scalar_prefetch=4,
            grid=(n_blocks,),
            in_specs=[
                pl.BlockSpec((bm, dp), lambda b, be, nx, nv, nu: (jnp.minimum(b, nu[0] - 1), 0)),
                hbm, hbm, hbm,
            ],
            out_specs=pl.BlockSpec((bm, dp), lambda b, be, nx, nv, nu: (b, 0)),
            scratch_shapes=[
                pltpu.VMEM((d, 2 * hid), BF16), pltpu.VMEM((hid, d), BF16),
                pltpu.VMEM((WEIGHT_SLOTS, d, hid), F32), pltpu.VMEM((WEIGHT_SLOTS, d, hid), F32),
                pltpu.VMEM((WEIGHT_SLOTS, hid, d), F32),
                pltpu.SMEM((1,), jnp.int32), pltpu.SemaphoreType.DMA((WEIGHT_SLOTS,)),
            ],
        ),
        out_shape=jax.ShapeDtypeStruct((n_rows, dp), jnp.uint32),
        compiler_params=_params(("arbitrary",)),
        name="experts",
    )(block_e, runs, n_valid, n_used, xb, w_gate, w_up, w_down)


def _combine_kernel(lat_ref, y0_ref, y1_ref, wc_ref, gate_ref, gf_ref, *rest, final):
    o_ref = rest[-1]
    wc = wc_ref[...]
    moe = wc[:, 0:1] * _unpack_rows(y0_ref[...]) + wc[:, 1:2] * _unpack_rows(y1_ref[...])
    out = lat_ref[...] + gate_ref[...] * moe
    if final:
        ms = jnp.mean(out * out, axis=-1, keepdims=True)
        out = out * lax.rsqrt(ms + EPS) * gf_ref[...]
    o_ref[...] = out


def _combine(lat, yg, wcol, mods, layer, batch0, g_final, seq, final, out_rows, tok0, prev_out):
    n, d = lat.shape
    tm = COMBINE_TM
    nt = n // tm
    t0 = tok0 // tm
    tiles_per_seq = seq // tm
    row = lambda i: layer * MOD_ROWS + batch0 + i // tiles_per_seq
    in_specs = [
        pl.BlockSpec((tm, d), lambda i: (i, 0)),
        pl.BlockSpec((tm, d // 2), lambda i: (i, 0)),
        pl.BlockSpec((tm, d // 2), lambda i: (nt + i, 0)),
        pl.BlockSpec((tm, LANES), lambda i: (i, 0)),
        pl.BlockSpec((None, 1, d), lambda i: (row(i), 0, 5)),
        pl.BlockSpec((1, d), lambda i: (0, 0)),
    ]
    args = [lat, yg, yg, wcol, mods, g_final]
    aliases = {}
    if prev_out is not None:
        in_specs.append(pl.BlockSpec(memory_space=pl.ANY))
        args.append(prev_out)
        aliases = {len(args) - 1: 0}
    return pl.pallas_call(
        functools.partial(_combine_kernel, final=final),
        grid=(nt,),
        in_specs=in_specs,
        out_specs=pl.BlockSpec((tm, d), lambda i: (t0 + i, 0)),
        out_shape=jax.ShapeDtypeStruct((out_rows, d), F32),
        input_output_aliases=aliases,
        compiler_params=_params(("parallel",)),
        name="combine",
    )(*args)


def _moe(lat, routed, mods, layer, batch0, w_gate, w_up, w_down, g_final, seq, final,
         out_rows, tok0, prev_out):
    h2, meta_i, wcol, counts = routed
    n = lat.shape[0]
    n_blocks = (2 * n) // MOE_BM + N_EXPERTS
    dest, block_e, runs, n_valid, n_used = _plan(counts[:, 0].astype(jnp.int32), meta_i, n_blocks)
    xb = _sc_dispatch(h2, dest, n_blocks * MOE_BM)
    yb = _experts(block_e, runs, n_valid, n_used, xb, w_gate, w_up, w_down, layer)
    yg = _sc_gather(yb, dest)
    return _combine(lat, yg, wcol, mods, layer, batch0, g_final, seq, final, out_rows, tok0, prev_out)


def _rope_tables(seq):
    quarter = HEAD_DIM // 4
    pos = jnp.arange(seq, dtype=F32)
    row_ids = jnp.floor(pos / GRID_W)
    col_ids = pos - row_ids * GRID_W
    inv = ROPE_BASE ** (-jnp.arange(quarter, dtype=F32) / quarter)
    ang_r = row_ids[:, None] * inv
    ang_c = col_ids[:, None] * inv
    zero = jnp.zeros_like(ang_r)
    cos = jnp.concatenate([jnp.cos(ang_r), jnp.cos(ang_r), jnp.cos(ang_c), jnp.cos(ang_c)], axis=-1)
    sa = jnp.concatenate([-jnp.sin(ang_r), zero, -jnp.sin(ang_c), zero], axis=-1)
    sb = jnp.concatenate([zero, jnp.sin(ang_r), zero, jnp.sin(ang_c)], axis=-1)
    rep = LANES // HEAD_DIM
    return tuple(jnp.tile(t, (1, rep)) for t in (cos, sa, sb))


def _router_weights(w_rg, b_rg, w_re, b_re):
    d = w_rg.shape[0]
    pad = EXPERTS_PER_GROUP - N_GROUPS
    wr_t = jnp.concatenate([w_rg.T, jnp.zeros((pad, d), F32), w_re.T], axis=0)
    br_t = jnp.concatenate([b_rg, jnp.full((pad,), NEG_INF, F32), b_re])[:, None]
    return wr_t, br_t


def kernel(x, c, ctx, c_ctx, w_ada, b_ada, g_norm1, g_norm2, g_final, w_in_even, attn_sink, g_sgu,
           w_spatial, b_spatial, w_out_even, w_in_odd, conv_w, w_out_odd, w_router_group,
           b_router_group, w_router_expert, b_router_expert, w_gate, w_up, w_down):
    b, s, d = x.shape
    n = b * s
    n_ctx = ctx.shape[1]
    depth = w_ada.shape[0]
    assert depth == 2 and b + 1 <= MOD_ROWS

    cond = jnp.concatenate([c, c_ctx[None, :], jnp.zeros((MOD_ROWS - b - 1, d), F32)], axis=0)
    mods = _ada(cond, w_ada, b_ada).reshape(depth * MOD_ROWS, 1, 6 * d)
    gf = g_final[None, :]

    lat = x.reshape(n, d)
    w_in_bf = w_in_even[0].astype(BF16)
    tabs = _rope_tables(s)
    qx, k, v, u, z = _even_in(lat, mods, 0, lambda i: i // (s // EVEN_TM), g_norm1[0][None, :], w_in_bf,
                              tabs, s // EVEN_TM, EVEN_TM)
    ones = jnp.ones((n_ctx, LANES), F32)
    zeros = jnp.zeros((n_ctx, LANES), F32)
    _, kc, vc, _, _ = _even_in(ctx.reshape(b * n_ctx, d), mods, 0, lambda i: b, g_norm1[0][None, :],
                               w_in_bf, (ones, zeros, zeros), 1, n_ctx)
    bsp_full = jnp.repeat(b_spatial[0].T, HEAD_DIM, axis=1)
    half = N_Q_HEADS // 2
    w_att = w_out_even[0][:Q_DIM].reshape(2, half, HEAD_DIM, d).transpose(1, 0, 2, 3).reshape(Q_DIM, d)
    w_out_bf = jnp.concatenate([w_att, w_out_even[0][Q_DIM:]], axis=0).astype(BF16)
    conv_w8 = jnp.concatenate([conv_w[0], jnp.zeros((8 - conv_w.shape[1], d), F32)], axis=0)
    w_in_odd_bf = w_in_odd[0].astype(BF16)
    w_out_odd_bf = w_out_odd[0].astype(BF16)
    wsp_bf = w_spatial[0].astype(BF16)
    wr0, br0 = _router_weights(w_router_group[0], b_router_group[0], w_router_expert[0], b_router_expert[0])
    wr1, br1 = _router_weights(w_router_group[1], b_router_group[1], w_router_expert[1], b_router_expert[1])

    part = n // MOE_PARTS
    out = None
    for p in range(MOE_PARTS):
        tok0 = p * part
        batch0 = tok0 // s
        lat_p, *routed = _even_mix(lat, qx, k, v, kc, vc, u, z, attn_sink[0], g_sgu[0][None, :],
                                   wsp_bf, bsp_full, w_out_bf, mods, s, n_ctx,
                                   g_norm2[0][None, :], wr0, br0, tok0, part)
        lat_p = _moe(lat_p, routed, mods, 0, batch0, w_gate, w_up, w_down, gf, s, False, part, 0, None)
        lat_p, *routed = _odd_mix(lat_p, mods, 1, g_norm1[1][None, :], w_in_odd_bf, conv_w8,
                                  w_out_odd_bf, s, g_norm2[1][None, :], wr1, br1, batch0)
        out = _moe(lat_p, routed, mods, 1, batch0, w_gate, w_up, w_down, gf, s, True, n, tok0, out)
    return out.reshape(b, s, d)
```

```python
import functools

import jax
import jax.numpy as jnp
from jax import lax
from jax.experimental import pallas as pl
from jax.experimental.pallas import tpu as pltpu
from jax.experimental.pallas import tpu_sc as plsc

F32 = jnp.float32
BF16 = jnp.bfloat16
HIGHEST = lax.Precision.HIGHEST

GRID_W = 64
N_Q_HEADS = 8
N_KV_HEADS = 2
HEAD_DIM = 64
ATT_BLOCK = 128
ROPE_BASE = 10000.0
Q_DIM = N_Q_HEADS * HEAD_DIM
KV_DIM = N_KV_HEADS * HEAD_DIM
SG_GROUPS = 8
SG_WIDTH = SG_GROUPS * HEAD_DIM
N_GROUPS = 4
EXPERTS_PER_GROUP = 8
N_EXPERTS = N_GROUPS * EXPERTS_PER_GROUP
EPS = 1e-6
NEG_INF = -1e30

LANES = 128
SC_CORES = 2
SC_SUBCORES = 16
SC_WORKERS = SC_CORES * SC_SUBCORES
SC_CHUNK = 32
MOD_ROWS = 8
ROUTER_ROWS = EXPERTS_PER_GROUP + N_EXPERTS
VMEM_LIMIT = 56 * 1024 * 1024

ADA_TN = 1536
EVEN_TM = 1024
EVEN_SUB = 512
ATT_TQ = 512
ODD_TM = 512
MOE_BM = 1024
MOE_QUANTUM = 128
MOE_PARTS = 2
WEIGHT_SLOTS = 3
COMBINE_TM = 512


def _params(sem):
    return pltpu.CompilerParams(dimension_semantics=sem, vmem_limit_bytes=VMEM_LIMIT)


def _rms_mod(x, g, shift, scale):
    ms = jnp.mean(x * x, axis=-1, keepdims=True)
    return (x * lax.rsqrt(ms + EPS)) * (g * (1.0 + scale)) + shift


def _pack_rounded(a):
    w = a.shape[1] // 2
    hi = pltpu.bitcast(a[:, :w], jnp.uint32)
    lo = pltpu.bitcast(a[:, w:], jnp.uint32)
    return hi | (lo >> 16)


def _pack_rows(a):
    return _pack_rounded(a.astype(BF16).astype(F32))


def _unpack_rows(p):
    hi = pltpu.bitcast(p & jnp.uint32(0xFFFF0000), F32)
    lo = pltpu.bitcast(p << 16, F32)
    return jnp.concatenate([hi, lo], axis=1)


def _ada_kernel(a_ref, w_ref, b_ref, o_ref):
    a = a_ref[...]
    s = a * (1.0 / (1.0 + jnp.exp(-a)))
    o_ref[0] = jnp.dot(s, w_ref[0], preferred_element_type=F32, precision=HIGHEST) + b_ref[0]


def _ada(cond, w_ada, b_ada):
    depth, d, six_d = w_ada.shape
    return pl.pallas_call(
        _ada_kernel,
        grid=(depth, six_d // ADA_TN),
        in_specs=[
            pl.BlockSpec((MOD_ROWS, d), lambda l, j: (0, 0)),
            pl.BlockSpec((1, d, ADA_TN), lambda l, j: (l, 0, j)),
            pl.BlockSpec((1, 1, ADA_TN), lambda l, j: (l, 0, j)),
        ],
        out_specs=pl.BlockSpec((1, MOD_ROWS, ADA_TN), lambda l, j: (l, 0, j)),
        out_shape=jax.ShapeDtypeStruct((depth, MOD_ROWS, six_d), F32),
        compiler_params=_params(("arbitrary", "arbitrary")),
        name="ada",
    )(cond, w_ada, b_ada.reshape(depth, 1, six_d))


def _even_in_kernel(x_ref, sh_ref, sc_ref, g_ref, w_ref, cos_ref, sa_ref, sb_ref,
                    qx_ref, k_ref, v_ref, u_ref, z_ref):
    tm = x_ref.shape[0]
    sub = min(tm, EVEN_SUB)
    scale = HEAD_DIM ** -0.5
    low = lax.broadcasted_iota(jnp.int32, (sub, LANES), 1) < HEAD_DIM
    heads_per_kv = N_Q_HEADS // N_KV_HEADS
    u0 = Q_DIM + 2 * KV_DIM

    for r0 in range(0, tm, sub):
        rows = slice(r0, r0 + sub)
        h = _rms_mod(x_ref[rows, :], g_ref[...], sh_ref[...], sc_ref[...])
        p = jnp.dot(h.astype(BF16), w_ref[...], preferred_element_type=F32)
        cos, sa, sb = cos_ref[rows, :], sa_ref[rows, :], sb_ref[rows, :]

        def rope(t):
            return t * cos + pltpu.roll(t, LANES - 16, 1) * sa + pltpu.roll(t, 16, 1) * sb

        for cblk in range(Q_DIM // LANES):
            t = rope(p[:, cblk * LANES:(cblk + 1) * LANES]) * scale
            sw = pltpu.roll(t, HEAD_DIM, 1)
            zero = jnp.zeros_like(t)
            if (2 * cblk) // heads_per_kv == 0:
                first, second = jnp.where(low, t, zero), jnp.where(low, sw, zero)
            else:
                first, second = jnp.where(low, zero, sw), jnp.where(low, zero, t)
            qx_ref[rows, (2 * cblk) * LANES:(2 * cblk + 1) * LANES] = first.astype(BF16)
            qx_ref[rows, (2 * cblk + 1) * LANES:(2 * cblk + 2) * LANES] = second.astype(BF16)

        k_ref[rows, :] = rope(p[:, Q_DIM:Q_DIM + KV_DIM]).astype(BF16)
        v_ref[rows, :] = p[:, Q_DIM + KV_DIM:Q_DIM + 2 * KV_DIM].astype(BF16)
        u_ref[rows, :] = p[:, u0:u0 + SG_WIDTH]
        z_ref[rows, :] = p[:, u0 + SG_WIDTH:u0 + 2 * SG_WIDTH]


def _even_in(x2d, mods, layer, mod_row_fn, g, w_bf, tabs, tab_blocks, tm):
    n, d = x2d.shape
    ein = w_bf.shape[1]
    cos, sa, sb = tabs
    row = lambda i: layer * MOD_ROWS + mod_row_fn(i)
    tab_spec = pl.BlockSpec((tm, LANES), lambda i: (i % tab_blocks, 0))
    qx_dim = N_Q_HEADS * LANES
    return pl.pallas_call(
        _even_in_kernel,
        grid=(n // tm,),
        in_specs=[
            pl.BlockSpec((tm, d), lambda i: (i, 0)),
            pl.BlockSpec((None, 1, d), lambda i: (row(i), 0, 0)),
            pl.BlockSpec((None, 1, d), lambda i: (row(i), 0, 1)),
            pl.BlockSpec((1, d), lambda i: (0, 0)),
            pl.BlockSpec((d, ein), lambda i: (0, 0)),
            tab_spec, tab_spec, tab_spec,
        ],
        out_specs=[
            pl.BlockSpec((tm, qx_dim), lambda i: (i, 0)),
            pl.BlockSpec((tm, KV_DIM), lambda i: (i, 0)),
            pl.BlockSpec((tm, KV_DIM), lambda i: (i, 0)),
            pl.BlockSpec((tm, SG_WIDTH), lambda i: (i, 0)),
            pl.BlockSpec((tm, SG_WIDTH), lambda i: (i, 0)),
        ],
        out_shape=[
            jax.ShapeDtypeStruct((n, qx_dim), BF16),
            jax.ShapeDtypeStruct((n, KV_DIM), BF16),
            jax.ShapeDtypeStruct((n, KV_DIM), BF16),
            jax.ShapeDtypeStruct((n, SG_WIDTH), F32),
            jax.ShapeDtypeStruct((n, SG_WIDTH), F32),
        ],
        compiler_params=_params(("parallel",)),
        name="even_in",
    )(x2d, mods, mods, g, w_bf, cos, sa, sb)


def _gelu(x):
    return 0.5 * x * (1.0 + lax.erf(x * (2.0 ** -0.5)))


def _even_mix_kernel(sink_ref, lat_ref, qx_ref, km_ref, kp_ref, kn_ref, vm_ref, vp_ref, vn_ref,
                     kc_ref, vc_ref, u_ref, z_ref, gsgu_ref, wsp_ref, bsp_ref, wout_ref, gate_ref,
                     sh2_ref, sc2_ref, g2_ref, wr_ref, br_ref,
                     o_ref, h_ref, mi_ref, wc_ref, cnt_ref,
                     kband, vband, mixin, carry, upper, *, tiles_per_seq, tiles_per_part):
    i = pl.program_id(0)
    tq = qx_ref.shape[0]

    @pl.when(i == 0)
    def _():
        _route_init(carry, upper)

    @pl.when(i % tiles_per_part == 0)
    def _():
        carry[...] = jnp.zeros_like(carry)

    blk = ATT_BLOCK
    nsub = tq // blk
    n_ctx = kc_ref.shape[0]
    first = (i % tiles_per_seq) == 0
    last = (i % tiles_per_seq) == tiles_per_seq - 1

    kband[0:blk] = kp_ref[...]
    kband[blk:blk + tq] = km_ref[...]
    kband[blk + tq:] = kn_ref[...]
    vband[0:blk] = vp_ref[...]
    vband[blk:blk + tq] = vm_ref[...]
    vband[blk + tq:] = vn_ref[...]

    rows = N_Q_HEADS * blk
    tok = lax.broadcasted_iota(jnp.int32, (rows, blk), 0) & (blk - 1)
    col = lax.broadcasted_iota(jnp.int32, (rows, blk), 1)
    tri_prev = col >= tok
    tri_next = col <= tok
    head = lax.broadcasted_iota(jnp.int32, (rows, 1), 0) // blk
    sink_col = jnp.zeros((rows, 1), F32)
    for hd in range(N_Q_HEADS):
        sink_col = jnp.where(head == hd, sink_ref[hd], sink_col)
    lane_low = lax.broadcasted_iota(jnp.int32, (blk, LANES), 1) < HEAD_DIM
    ones = jnp.ones((n_ctx + 3 * blk, LANES), BF16)
    nt = (((1,), (1,)), ((), ()))

    def sub_block(j, c):
        r0 = pl.multiple_of(j * blk, blk)
        ok_prev = jnp.logical_not(jnp.logical_and(first, j == 0))
        ok_next = jnp.logical_not(jnp.logical_and(last, j == nsub - 1))
        qs = jnp.concatenate([qx_ref[pl.ds(r0, blk), hd * LANES:(hd + 1) * LANES]
                              for hd in range(N_Q_HEADS)], axis=0)
        kall = jnp.concatenate([kc_ref[...], kband[pl.ds(r0, 3 * blk), :]], axis=0)
        vall = jnp.concatenate([vc_ref[...], vband[pl.ds(r0, 3 * blk), :]], axis=0)
        s = lax.dot_general(qs, kall, nt, preferred_element_type=F32)
        c0 = n_ctx
        s = jnp.concatenate([
            s[:, :c0],
            jnp.where(jnp.logical_and(tri_prev, ok_prev), s[:, c0:c0 + blk], NEG_INF),
            s[:, c0 + blk:c0 + 2 * blk],
            jnp.where(jnp.logical_and(tri_next, ok_next), s[:, c0 + 2 * blk:], NEG_INF),
        ], axis=1)
        m = jnp.maximum(jnp.max(s, axis=-1, keepdims=True), sink_col)
        p = jnp.exp(s - m).astype(BF16)
        o = jnp.dot(p, jnp.concatenate([vall, ones], axis=1), preferred_element_type=F32)
        att = o[:, :LANES] / (o[:, LANES:] + jnp.exp(sink_col - m))
        half = N_Q_HEADS // 2
        for hd in range(half):
            pair = jnp.where(lane_low, att[hd * blk:(hd + 1) * blk], att[(hd + half) * blk:(hd + half + 1) * blk])
            mixin[pl.ds(r0, blk), hd * LANES:(hd + 1) * LANES] = pair.astype(BF16)

        ug = _gelu(u_ref[pl.ds(r0, blk), :])
        zg = _gelu(z_ref[pl.ds(r0, blk), :])
        mu = jnp.mean(zg, axis=-1, keepdims=True)
        zc = zg - mu
        zn = zc * lax.rsqrt(jnp.mean(zc * zc, axis=-1, keepdims=True) + EPS) * gsgu_ref[...]
        for pair in range(SG_GROUPS // 2):
            zp = zn[:, pair * LANES:(pair + 1) * LANES]
            zero = jnp.zeros_like(zp)
            lo = jnp.where(lane_low, zp, zero).astype(BF16)
            hi = jnp.where(lane_low, zero, zp).astype(BF16)
            sg = (jnp.dot(wsp_ref[2 * pair], lo, preferred_element_type=F32)
                  + jnp.dot(wsp_ref[2 * pair + 1], hi, preferred_element_type=F32)
                  + bsp_ref[:, pair * LANES:(pair + 1) * LANES])
            mixin[pl.ds(r0, blk), Q_DIM + pair * LANES:Q_DIM + (pair + 1) * LANES] = (
                ug[:, pair * LANES:(pair + 1) * LANES] * sg).astype(BF16)
        return c

    lax.fori_loop(0, nsub, sub_block, 0, unroll=True)
    mix = jnp.dot(mixin[...], wout_ref[...], preferred_element_type=F32)
    lat = lat_ref[...] + gate_ref[...] * mix
    o_ref[...] = lat
    _route_tile(lat, sh2_ref[...], sc2_ref[...], g2_ref[...], wr_ref, br_ref, carry, upper,
                h_ref, mi_ref, wc_ref, cnt_ref)


def _even_mix(lat, qx, k, v, kc, vc, u, z, sink, g_sgu, wsp_bf, bsp_full, wout_bf, mods, seq, ctx_len,
              g2, wr_t, br_t, tok0, n):
    d = lat.shape[1]
    tq = ATT_TQ
    tiles_per_seq = seq // tq
    sub = tq // ATT_BLOCK
    nblk = lat.shape[0] // ATT_BLOCK
    t0 = tok0 // tq
    main = lambda w: pl.BlockSpec((tq, w), lambda i: (t0 + i, 0))
    prev = pl.BlockSpec((ATT_BLOCK, KV_DIM), lambda i: (jnp.maximum((t0 + i) * sub - 1, 0), 0))
    nxt = pl.BlockSpec((ATT_BLOCK, KV_DIM), lambda i: (jnp.minimum((t0 + i + 1) * sub, nblk - 1), 0))
    batch = lambda t: (t0 + t) // tiles_per_seq
    ctxs = pl.BlockSpec((ctx_len, KV_DIM), lambda i: (batch(i), 0))
    const = lambda shape: pl.BlockSpec(shape, lambda i: (0,) * len(shape), pipeline_mode=pl.Buffered(1))
    r_in, r_out, r_shapes, r_scratch = _route_specs(n, d, tq, lambda i: i, batch)
    return pl.pallas_call(
        functools.partial(_even_mix_kernel, tiles_per_seq=tiles_per_seq, tiles_per_part=n // tq),
        grid=(n // tq,),
        in_specs=[
            pl.BlockSpec(memory_space=pltpu.SMEM),
            main(d), main(qx.shape[1]),
            main(KV_DIM), prev, nxt,
            main(KV_DIM), prev, nxt,
            ctxs, ctxs,
            main(SG_WIDTH), main(SG_WIDTH),
            const((1, SG_WIDTH)), const(wsp_bf.shape), const(bsp_full.shape), const(wout_bf.shape),
            pl.BlockSpec((None, 1, d), lambda i: (batch(i), 0, 2)),
        ] + r_in,
        out_specs=[pl.BlockSpec((tq, d), lambda i: (i, 0))] + r_out,
        out_shape=[jax.ShapeDtypeStruct((n, d), F32)] + r_shapes,
        scratch_shapes=[
            pltpu.VMEM((tq + 2 * ATT_BLOCK, KV_DIM), BF16),
            pltpu.VMEM((tq + 2 * ATT_BLOCK, KV_DIM), BF16),
            pltpu.VMEM((tq, Q_DIM + SG_WIDTH), BF16),
        ] + r_scratch,
        compiler_params=_params(("arbitrary",)),
        name="even_mix",
    )(sink, lat, qx, k, k, k, v, v, v, kc, vc, u, z, g_sgu, wsp_bf, bsp_full, wout_bf, mods,
      mods, mods, g2, wr_t, br_t)


def _odd_mix_kernel(x_ref, e0_ref, e1_ref, ew_ref, egate_ref, sh_ref, sc_ref, gate_ref, g_ref,
                    win_ref, cw_ref, wout_ref, sh2_ref, sc2_ref, g2_ref, wr_ref, br_ref,
                    o_ref, h_ref, mi_ref, wc_ref, cnt_ref,
                    y_s, bg_s, tail_s, x_s, carry, upper, *, tiles_per_seq, tiles_per_part):
    i = pl.program_id(0)
    n_tiles = pl.num_programs(0) - 1
    tm, d = x_ref.shape
    cur = i % 2
    prv = 1 - cur

    @pl.when(i == 0)
    def _():
        _route_init(carry, upper)

    tail_s[...] = y_s[cur, tm - 8:tm, :]

    @pl.when(i < n_tiles)
    def _():
        ew = ew_ref[...]
        moe = ew[:, 0:1] * _unpack_rows(e0_ref[...]) + ew[:, 1:2] * _unpack_rows(e1_ref[...])
        x = x_ref[...] + egate_ref[...] * moe
        x_s[cur] = x
        h = _rms_mod(x, g_ref[...], sh_ref[...], sc_ref[...])
        p = jnp.dot(h.astype(BF16), win_ref[...], preferred_element_type=F32)
        bg_s[cur] = p[:, 0:d]
        y_s[cur] = p[:, d:2 * d] * p[:, 2 * d:3 * d]

    @pl.when(i >= 1)
    def _():
        t_prev = i - 1
        first = (t_prev % tiles_per_seq) == 0
        last = (t_prev % tiles_per_seq) == tiles_per_seq - 1
        y = y_s[prv]
        left = jnp.where(first, 0.0, tail_s[7:8, :])
        right = jnp.where(last, 0.0, y_s[cur, 0:1, :])
        ridx = lax.broadcasted_iota(jnp.int32, (tm, d), 0)
        y_dn = jnp.where(ridx == 0, left, pltpu.roll(y, 1, 0))
        y_up = jnp.where(ridx == tm - 1, right, pltpu.roll(y, tm - 1, 0))
        conv = y_dn * cw_ref[0:1, :] + y * cw_ref[1:2, :] + y_up * cw_ref[2:3, :]
        mix = jnp.dot((bg_s[prv] * conv).astype(BF16), wout_ref[...], preferred_element_type=F32)
        lat = x_s[prv] + gate_ref[...] * mix
        o_ref[...] = lat

        @pl.when(t_prev % tiles_per_part == 0)
        def _():
            carry[...] = jnp.zeros_like(carry)

        _route_tile(lat, sh2_ref[...], sc2_ref[...], g2_ref[...], wr_ref, br_ref, carry, upper,
                    h_ref, mi_ref, wc_ref, cnt_ref)


def _odd_mix(lat, moe_rows, mods, layer, g, win_bf, conv_w8, wout_bf, seq, g2, wr_t, br_t, batch0):
    yg, ewcol = moe_rows
    n, d = lat.shape
    tm = ODD_TM
    nt = n // tm
    tiles_per_seq = seq // tm
    cur = lambda i: jnp.minimum(i, nt - 1)
    prv = lambda i: jnp.maximum(i - 1, 0)
    row = lambda t: layer * MOD_ROWS + batch0 + t // tiles_per_seq
    const = lambda shape: pl.BlockSpec(shape, lambda i: (0,) * len(shape), pipeline_mode=pl.Buffered(1))
    r_in, r_out, r_shapes, r_scratch = _route_specs(n, d, tm, prv, row)
    return pl.pallas_call(
        functools.partial(_odd_mix_kernel, tiles_per_seq=tiles_per_seq, tiles_per_part=nt),
        grid=(nt + 1,),
        in_specs=[
            pl.BlockSpec((tm, d), lambda i: (cur(i), 0)),
            pl.BlockSpec((tm, d // 2), lambda i: (cur(i), 0)),
            pl.BlockSpec((tm, d // 2), lambda i: (nt + cur(i), 0)),
            pl.BlockSpec((tm, LANES), lambda i: (cur(i), 0)),
            pl.BlockSpec((None, 1, d), lambda i: (row(cur(i)) - MOD_ROWS, 0, 5)),
            pl.BlockSpec((None, 1, d), lambda i: (row(cur(i)), 0, 0)),
            pl.BlockSpec((None, 1, d), lambda i: (row(cur(i)), 0, 1)),
            pl.BlockSpec((None, 1, d), lambda i: (row(prv(i)), 0, 2)),
            const((1, d)), const(win_bf.shape), const(conv_w8.shape), const(wout_bf.shape),
        ] + r_in,
        out_specs=[pl.BlockSpec((tm, d), lambda i: (prv(i), 0))] + r_out,
        out_shape=[jax.ShapeDtypeStruct((n, d), F32)] + r_shapes,
        scratch_shapes=[
            pltpu.VMEM((2, tm, d), F32),
            pltpu.VMEM((2, tm, d), F32),
            pltpu.VMEM((8, d), F32),
            pltpu.VMEM((2, tm, d), F32),
        ] + r_scratch,
        compiler_params=_params(("arbitrary",)),
        name="odd_mix",
    )(lat, yg, yg, ewcol, mods, mods, mods, mods, g, win_bf, conv_w8, wout_bf, mods, mods, g2, wr_t, br_t)


def _route_init(carry, upper):
    tm = upper.shape[0]
    carry[...] = jnp.zeros_like(carry)
    r_i = lax.broadcasted_iota(jnp.int32, (tm, tm), 0)
    c_i = lax.broadcasted_iota(jnp.int32, (tm, tm), 1)
    upper[...] = jnp.where(r_i < c_i, 1.0, 0.0).astype(BF16)


def _route_tile(lat, sh, sc, g, wr_ref, br_ref, carry, upper, h_ref, mi_ref, wc_ref, cnt_ref):
    tm = lat.shape[0]
    epg = EXPERTS_PER_GROUP
    h = _rms_mod(lat, g, sh, sc)
    h_hi = h.astype(BF16)
    h_hi_f = h_hi.astype(F32)
    h_ref[...] = _pack_rounded(h_hi_f)
    h_lo = (h - h_hi_f).astype(BF16)
    w = wr_ref[...]
    w1 = w.astype(BF16).astype(F32)
    r1 = w - w1
    w2 = r1.astype(BF16).astype(F32)
    w3 = r1 - w2
    nt = (((1,), (1,)), ((), ()))
    nr = w.shape[0]
    w123 = jnp.concatenate([w1, w2, w3, jnp.zeros((8, w.shape[1]), F32)], axis=0).astype(BF16)
    w12 = jnp.concatenate([w1, w2], axis=0).astype(BF16)
    p_hi = lax.dot_general(w123, h_hi, nt, preferred_element_type=F32)
    p_lo = lax.dot_general(w12, h_lo, nt, preferred_element_type=F32)
    lg = ((p_hi[2 * nr:3 * nr] + p_lo[nr:2 * nr]) + (p_hi[nr:2 * nr] + p_lo[0:nr])) + p_hi[0:nr] + br_ref[...]
    io8 = lax.broadcasted_iota(jnp.int32, (epg, tm), 0)
    gl = lg[0:epg]
    gmax = jnp.max(gl, axis=0, keepdims=True)
    g_idx = jnp.min(jnp.where(gl == gmax, io8, epg), axis=0, keepdims=True)
    g_w = 1.0 / jnp.sum(jnp.exp(gl - gmax), axis=0, keepdims=True)
    e_sel = lg[epg:2 * epg]
    for gi in range(1, N_GROUPS):
        e_sel = jnp.where(g_idx == gi, lg[(gi + 1) * epg:(gi + 2) * epg], e_sel)
    v0 = jnp.max(e_sel, axis=0, keepdims=True)
    i0 = jnp.min(jnp.where(e_sel == v0, io8, epg), axis=0, keepdims=True)
    rest = jnp.where(io8 == i0, -jnp.inf, e_sel)
    v1 = jnp.max(rest, axis=0, keepdims=True)
    i1 = jnp.min(jnp.where(rest == v1, io8, epg), axis=0, keepdims=True)
    t = jnp.exp(v1 - v0)
    w0 = g_w / (1.0 + t)
    w1 = g_w * t / (1.0 + t)
    e0 = g_idx * epg + i0
    e1 = g_idx * epg + i1

    io32 = lax.broadcasted_iota(jnp.int32, (N_EXPERTS, tm), 0)
    hit0 = io32 == e0
    hit1 = io32 == e1
    onehot = jnp.where(hit0 | hit1, 1.0, 0.0)
    cum = jnp.dot(onehot.astype(BF16), upper[...], preferred_element_type=F32) + carry[...]
    rank0 = jnp.sum(jnp.where(hit0, cum, 0.0), axis=0, keepdims=True).astype(jnp.int32)
    rank1 = jnp.sum(jnp.where(hit1, cum, 0.0), axis=0, keepdims=True).astype(jnp.int32)
    carry[...] = carry[...] + jnp.sum(onehot, axis=1, keepdims=True)
    cnt_ref[...] = jnp.broadcast_to(carry[...], cnt_ref.shape)

    mi_ref[...] = jnp.where(io8 == 0, e0, jnp.where(io8 == 1, e1, jnp.where(io8 == 2, rank0,
                            jnp.where(io8 == 3, rank1, 0))))
    io128 = lax.broadcasted_iota(jnp.int32, (LANES, tm), 0)
    wrow = jnp.where(io128 == 0, w0, jnp.where(io128 == 1, w1, 0.0))
    wc_ref[...] = wrow.T


def _route_specs(n, d, tm, tile_of, mods_row):
    in_specs = [
        pl.BlockSpec((None, 1, d), lambda i: (mods_row(tile_of(i)), 0, 3)),
        pl.BlockSpec((None, 1, d), lambda i: (mods_row(tile_of(i)), 0, 4)),
        pl.BlockSpec((1, d), lambda i: (0, 0), pipeline_mode=pl.Buffered(1)),
        pl.BlockSpec((ROUTER_ROWS, d), lambda i: (0, 0), pipeline_mode=pl.Buffered(1)),
        pl.BlockSpec((ROUTER_ROWS, 1), lambda i: (0, 0), pipeline_mode=pl.Buffered(1)),
    ]
    out_specs = [
        pl.BlockSpec((tm, d // 2), lambda i: (tile_of(i), 0)),
        pl.BlockSpec((8, tm), lambda i: (0, tile_of(i))),
        pl.BlockSpec((tm, LANES), lambda i: (tile_of(i), 0)),
        pl.BlockSpec((N_EXPERTS, LANES), lambda i: (0, 0)),
    ]
    out_shapes = [
        jax.ShapeDtypeStruct((n, d // 2), jnp.uint32),
        jax.ShapeDtypeStruct((8, n), jnp.int32),
        jax.ShapeDtypeStruct((n, LANES), F32),
        jax.ShapeDtypeStruct((N_EXPERTS, LANES), F32),
    ]
    scratch = [pltpu.VMEM((N_EXPERTS, 1), F32), pltpu.VMEM((tm, tm), BF16)]
    return in_specs, out_specs, out_shapes, scratch


def _plan_kernel(cnt_ref, mi_ref, dest_ref, be_ref, runs_ref, nv_ref, nu_ref, ps_ref, *, n_blocks):
    bm = MOE_BM

    def per_expert(e, carry):
        blk0, n_runs = carry
        cnt = cnt_ref[e]
        nb = (cnt + bm - 1) // bm
        ps_ref[e] = blk0 * bm

        def fill(b, c):
            be_ref[b] = e
            nv_ref[b] = jnp.minimum(cnt - (b - blk0) * bm, bm)
            return c

        lax.fori_loop(blk0, blk0 + nb, fill, 0)

        @pl.when(nb > 0)
        def _():
            runs_ref[n_runs] = e

        return blk0 + nb, n_runs + jnp.where(nb > 0, 1, 0)

    n_used, n_runs = lax.fori_loop(0, N_EXPERTS, per_expert, (0, 0))
    nu_ref[0] = n_used
    nu_ref[1] = n_runs
    last_e = be_ref[jnp.maximum(n_used - 1, 0)]

    def fill_tail(b, c):
        be_ref[b] = last_e
        nv_ref[b] = 0
        return c

    lax.fori_loop(n_used, n_blocks, fill_tail, 0)

    def fill_runs(k, c):
        runs_ref[k] = last_e
        return c

    lax.fori_loop(n_runs, N_EXPERTS, fill_runs, 0)

    e01 = mi_ref[0:2, :]
    dest = mi_ref[2:4, :]
    for e in range(N_EXPERTS):
        dest = dest + jnp.where(e01 == e, ps_ref[e], 0)
    dest_ref[...] = dest


def _plan(counts, meta_i, n_blocks):
    n = meta_i.shape[1]
    return pl.pallas_call(
        functools.partial(_plan_kernel, n_blocks=n_blocks),
        in_specs=[pl.BlockSpec(memory_space=pltpu.SMEM), pl.BlockSpec(memory_space=pltpu.VMEM)],
        out_specs=[pl.BlockSpec(memory_space=pltpu.VMEM)] + [pl.BlockSpec(memory_space=pltpu.SMEM)] * 4,
        out_shape=[
            jax.ShapeDtypeStruct((2, n), jnp.int32),
            jax.ShapeDtypeStruct((n_blocks,), jnp.int32),
            jax.ShapeDtypeStruct((N_EXPERTS,), jnp.int32),
            jax.ShapeDtypeStruct((n_blocks,), jnp.int32),
            jax.ShapeDtypeStruct((2,), jnp.int32),
        ],
        scratch_shapes=[pltpu.SMEM((N_EXPERTS,), jnp.int32)],
        compiler_params=pltpu.CompilerParams(vmem_limit_bytes=VMEM_LIMIT),
        name="plan",
    )(counts, meta_i)


def _sc_mesh():
    return plsc.VectorSubcoreMesh(core_axis_name="c", subcore_axis_name="s",
                                  num_cores=SC_CORES, num_subcores=SC_SUBCORES)


def _sc_worker():
    return lax.axis_index("s") * SC_CORES + lax.axis_index("c")


def _sc_dispatch(h2, dest, n_rows):
    n, d = h2.shape
    c = SC_CHUNK
    per_w = n // SC_WORKERS
    nchunk = per_w // c
    idx = dest.reshape(2, SC_WORKERS, nchunk, c)

    @functools.partial(
        pl.kernel, mesh=_sc_mesh(), out_type=jax.ShapeDtypeStruct((n_rows, d), h2.dtype),
        scratch_types=[pltpu.VMEM((nchunk, c), jnp.int32), pltpu.VMEM((nchunk, c), jnp.int32),
                       pltpu.VMEM((2, c, d), h2.dtype),
                       pltpu.SemaphoreType.DMA((2,)), pltpu.SemaphoreType.DMA((2,))])
    def k(h_hbm, idx_hbm, xb_hbm, idx0_v, idx1_v, rows_v, gsem, ssem):
        wid = _sc_worker()
        base = wid * per_w
        idx_v = (idx0_v, idx1_v)
        for kk in range(2):
            pltpu.sync_copy(idx_hbm.at[kk, wid], idx_v[kk])

        def get(j, slot):
            return pltpu.make_async_copy(h_hbm.at[pl.ds(base + j * c, c)], rows_v.at[slot], gsem.at[slot])

        def put(j, slot, kk):
            return pltpu.make_async_copy(rows_v.at[slot], xb_hbm.at[idx_v[kk].at[j]], ssem.at[slot])

        get(0, 0).start()

        @pl.loop(0, nchunk, step=2)
        def _(j):
            for slot in range(2):
                jj = j + slot
                get(jj, slot).wait()

                @pl.when(jj >= 1)
                def _():
                    for kk in range(2):
                        put(jj - 1, 1 - slot, kk).wait()

                @pl.when(jj + 1 < nchunk)
                def _():
                    get(jj + 1, 1 - slot).start()

                for kk in range(2):
                    put(jj, slot, kk).start()

        for kk in range(2):
            put(nchunk - 1, (nchunk - 1) % 2, kk).wait()

    return k(h2, idx)


def _sc_gather(y, dest):
    d = y.shape[1]
    total = dest.shape[0] * dest.shape[1]
    c = SC_CHUNK
    per_w = total // SC_WORKERS
    nchunk = per_w // c
    idx = dest.reshape(SC_WORKERS, nchunk, c)

    @functools.partial(
        pl.kernel, mesh=_sc_mesh(), out_type=jax.ShapeDtypeStruct((total, d), y.dtype),
        scratch_types=[pltpu.VMEM((nchunk, c), jnp.int32), pltpu.VMEM((2, c, d), y.dtype),
                       pltpu.SemaphoreType.DMA((2,)), pltpu.SemaphoreType.DMA((2,))])
    def k(y_hbm, idx_hbm, out_hbm, idx_v, rows_v, gsem, ssem):
        wid = _sc_worker()
        base = wid * per_w
        pltpu.sync_copy(idx_hbm.at[wid], idx_v)

        def get(j, slot):
            return pltpu.make_async_copy(y_hbm.at[idx_v.at[j]], rows_v.at[slot], gsem.at[slot])

        def put(j, slot):
            return pltpu.make_async_copy(rows_v.at[slot], out_hbm.at[pl.ds(base + j * c, c)], ssem.at[slot])

        get(0, 0).start()

        @pl.loop(0, nchunk, step=2)
        def _(j):
            for slot in range(2):
                jj = j + slot
                get(jj, slot).wait()

                @pl.when(jj >= 1)
                def _():
                    put(jj - 1, 1 - slot).wait()

                @pl.when(jj + 1 < nchunk)
                def _():
                    get(jj + 1, 1 - slot).start()

                put(jj, slot).start()

        put(nchunk - 1, (nchunk - 1) % 2).wait()

    return k(y, idx)


def _expert_kernel(be_ref, runs_ref, nv_ref, nu_ref, x_ref, wg_hbm, wu_hbm, wd_hbm, y_ref,
                   wgu_s, wd_s, stg_g, stg_u, stg_d, run_s, sems, *, layer):
    b = pl.program_id(0)
    hid = stg_g.shape[2]
    e = be_ref[b]
    n_runs = nu_ref[1]
    changed = jnp.logical_or(b == 0, e != be_ref[jnp.maximum(b - 1, 0)])

    def fetch(run):
        expert = runs_ref[run]
        slot = run % WEIGHT_SLOTS
        return (pltpu.make_async_copy(wg_hbm.at[layer, expert], stg_g.at[slot], sems.at[slot]),
                pltpu.make_async_copy(wu_hbm.at[layer, expert], stg_u.at[slot], sems.at[slot]),
                pltpu.make_async_copy(wd_hbm.at[layer, expert], stg_d.at[slot], sems.at[slot]))

    @pl.when(b == 0)
    def _():
        for r in range(WEIGHT_SLOTS - 1):
            @pl.when(r < n_runs)
            def _():
                for cp in fetch(r):
                    cp.start()

    @pl.when(changed)
    def _():
        run = jnp.where(b == 0, 0, run_s[0] + 1)
        run_s[0] = run
        for cp in fetch(run):
            cp.wait()

        ahead = run + WEIGHT_SLOTS - 1

        @pl.when(ahead < n_runs)
        def _():
            for cp in fetch(ahead):
                cp.start()

        slot = run % WEIGHT_SLOTS
        wgu_s[:, 0:hid] = stg_g[slot].astype(BF16)
        wgu_s[:, hid:2 * hid] = stg_u[slot].astype(BF16)
        wd_s[...] = stg_d[slot].astype(BF16)

    bm, dp = x_ref.shape
    nv = nv_ref[b]
    in_use = b < nu_ref[0]

    def run(rows):
        live = lax.broadcasted_iota(jnp.int32, (rows, dp), 0) < nv
        x = _unpack_rows(jnp.where(live, x_ref[0:rows, :], jnp.uint32(0)))
        gu = jnp.dot(x.astype(BF16), wgu_s[...], preferred_element_type=F32)
        gate = gu[:, 0:hid]
        act = gate * (1.0 / (1.0 + jnp.exp(-gate))) * gu[:, hid:2 * hid]
        y_ref[0:rows, :] = _pack_rows(jnp.dot(act.astype(BF16), wd_s[...], preferred_element_type=F32))

    n_quanta = bm // MOE_QUANTUM
    for q in range(1, n_quanta + 1):
        rows = q * MOE_QUANTUM

        @pl.when(jnp.logical_and(in_use, jnp.logical_and(nv > rows - MOE_QUANTUM, nv <= rows)))
        def _(rows=rows):
            run(rows)
            if rows < bm:
                y_ref[rows:bm, :] = jnp.zeros((bm - rows, dp), y_ref.dtype)

    @pl.when(jnp.logical_not(in_use))
    def _():
        y_ref[...] = jnp.zeros_like(y_ref)


def _experts(block_e, runs, n_valid, n_used, xb, w_gate, w_up, w_down, layer):
    n_rows, dp = xb.shape
    d, hid = w_gate.shape[2], w_gate.shape[3]
    bm = MOE_BM
    n_blocks = n_rows // bm
    hbm = pl.BlockSpec(memory_space=pl.ANY)
    return pl.pallas_call(
        functools.partial(_expert_kernel, layer=layer),
        grid_spec=pltpu.PrefetchScalarGridSpec(
            num_scalar_prefetch=4,
            grid=(n_blocks,),
            in_specs=[
                pl.BlockSpec((bm, dp), lambda b, be, nx, nv, nu: (jnp.minimum(b, nu[0] - 1), 0)),
                hbm, hbm, hbm,
            ],
            out_specs=pl.BlockSpec((bm, dp), lambda b, be, nx, nv, nu: (b, 0)),
            scratch_shapes=[
                pltpu.VMEM((d, 2 * hid), BF16), pltpu.VMEM((hid, d), BF16),
                pltpu.VMEM((WEIGHT_SLOTS, d, hid), F32), pltpu.VMEM((WEIGHT_SLOTS, d, hid), F32),
                pltpu.VMEM((WEIGHT_SLOTS, hid, d), F32),
                pltpu.SMEM((1,), jnp.int32), pltpu.SemaphoreType.DMA((WEIGHT_SLOTS,)),
            ],
        ),
        out_shape=jax.ShapeDtypeStruct((n_rows, dp), jnp.uint32),
        compiler_params=_params(("arbitrary",)),
        name="experts",
    )(block_e, runs, n_valid, n_used, xb, w_gate, w_up, w_down)


def _combine_kernel(lat_ref, y0_ref, y1_ref, wc_ref, gate_ref, gf_ref, *rest, final):
    o_ref = rest[-1]
    wc = wc_ref[...]
    moe = wc[:, 0:1] * _unpack_rows(y0_ref[...]) + wc[:, 1:2] * _unpack_rows(y1_ref[...])
    out = lat_ref[...] + gate_ref[...] * moe
    if final:
        ms = jnp.mean(out * out, axis=-1, keepdims=True)
        out = out * lax.rsqrt(ms + EPS) * gf_ref[...]
    o_ref[...] = out


def _combine(lat, yg, wcol, mods, layer, batch0, g_final, seq, final, out_rows, tok0, prev_out):
    n, d = lat.shape
    tm = COMBINE_TM
    nt = n // tm
    t0 = tok0 // tm
    tiles_per_seq = seq // tm
    row = lambda i: layer * MOD_ROWS + batch0 + i // tiles_per_seq
    in_specs = [
        pl.BlockSpec((tm, d), lambda i: (i, 0)),
        pl.BlockSpec((tm, d // 2), lambda i: (i, 0)),
        pl.BlockSpec((tm, d // 2), lambda i: (nt + i, 0)),
        pl.BlockSpec((tm, LANES), lambda i: (i, 0)),
        pl.BlockSpec((None, 1, d), lambda i: (row(i), 0, 5)),
        pl.BlockSpec((1, d), lambda i: (0, 0)),
    ]
    args = [lat, yg, yg, wcol, mods, g_final]
    aliases = {}
    if prev_out is not None:
        in_specs.append(pl.BlockSpec(memory_space=pl.ANY))
        args.append(prev_out)
        aliases = {len(args) - 1: 0}
    return pl.pallas_call(
        functools.partial(_combine_kernel, final=final),
        grid=(nt,),
        in_specs=in_specs,
        out_specs=pl.BlockSpec((tm, d), lambda i: (t0 + i, 0)),
        out_shape=jax.ShapeDtypeStruct((out_rows, d), F32),
        input_output_aliases=aliases,
        compiler_params=_params(("parallel",)),
        name="combine",
    )(*args)


def _moe_rows(routed, layer, w_gate, w_up, w_down):
    h2, meta_i, wcol, counts = routed
    n = h2.shape[0]
    n_blocks = (2 * n) // MOE_BM + N_EXPERTS
    dest, block_e, runs, n_valid, n_used = _plan(counts[:, 0].astype(jnp.int32), meta_i, n_blocks)
    xb = _sc_dispatch(h2, dest, n_blocks * MOE_BM)
    yb = _experts(block_e, runs, n_valid, n_used, xb, w_gate, w_up, w_down, layer)
    return _sc_gather(yb, dest), wcol


def _rope_tables(seq):
    quarter = HEAD_DIM // 4
    pos = jnp.arange(seq, dtype=F32)
    row_ids = jnp.floor(pos / GRID_W)
    col_ids = pos - row_ids * GRID_W
    inv = ROPE_BASE ** (-jnp.arange(quarter, dtype=F32) / quarter)
    ang_r = row_ids[:, None] * inv
    ang_c = col_ids[:, None] * inv
    zero = jnp.zeros_like(ang_r)
    cos = jnp.concatenate([jnp.cos(ang_r), jnp.cos(ang_r), jnp.cos(ang_c), jnp.cos(ang_c)], axis=-1)
    sa = jnp.concatenate([-jnp.sin(ang_r), zero, -jnp.sin(ang_c), zero], axis=-1)
    sb = jnp.concatenate([zero, jnp.sin(ang_r), zero, jnp.sin(ang_c)], axis=-1)
    rep = LANES // HEAD_DIM
    return tuple(jnp.tile(t, (1, rep)) for t in (cos, sa, sb))


def _router_weights(w_rg, b_rg, w_re, b_re):
    d = w_rg.shape[0]
    pad = EXPERTS_PER_GROUP - N_GROUPS
    wr_t = jnp.concatenate([w_rg.T, jnp.zeros((pad, d), F32), w_re.T], axis=0)
    br_t = jnp.concatenate([b_rg, jnp.full((pad,), NEG_INF, F32), b_re])[:, None]
    return wr_t, br_t


def kernel(x, c, ctx, c_ctx, w_ada, b_ada, g_norm1, g_norm2, g_final, w_in_even, attn_sink, g_sgu,
           w_spatial, b_spatial, w_out_even, w_in_odd, conv_w, w_out_odd, w_router_group,
           b_router_group, w_router_expert, b_router_expert, w_gate, w_up, w_down):
    b, s, d = x.shape
    n = b * s
    n_ctx = ctx.shape[1]
    depth = w_ada.shape[0]
    assert depth == 2 and b + 1 <= MOD_ROWS

    cond = jnp.concatenate([c, c_ctx[None, :], jnp.zeros((MOD_ROWS - b - 1, d), F32)], axis=0)
    mods = _ada(cond, w_ada, b_ada).reshape(depth * MOD_ROWS, 1, 6 * d)
    gf = g_final[None, :]

    lat = x.reshape(n, d)
    w_in_bf = w_in_even[0].astype(BF16)
    tabs = _rope_tables(s)
    qx, k, v, u, z = _even_in(lat, mods, 0, lambda i: i // (s // EVEN_TM), g_norm1[0][None, :], w_in_bf,
                              tabs, s // EVEN_TM, EVEN_TM)
    ones = jnp.ones((n_ctx, LANES), F32)
    zeros = jnp.zeros((n_ctx, LANES), F32)
    _, kc, vc, _, _ = _even_in(ctx.reshape(b * n_ctx, d), mods, 0, lambda i: b, g_norm1[0][None, :],
                               w_in_bf, (ones, zeros, zeros), 1, n_ctx)
    bsp_full = jnp.repeat(b_spatial[0].T, HEAD_DIM, axis=1)
    half = N_Q_HEADS // 2
    w_att = w_out_even[0][:Q_DIM].reshape(2, half, HEAD_DIM, d).transpose(1, 0, 2, 3).reshape(Q_DIM, d)
    w_out_bf = jnp.concatenate([w_att, w_out_even[0][Q_DIM:]], axis=0).astype(BF16)
    conv_w8 = jnp.concatenate([conv_w[0], jnp.zeros((8 - conv_w.shape[1], d), F32)], axis=0)
    w_in_odd_bf = w_in_odd[0].astype(BF16)
    w_out_odd_bf = w_out_odd[0].astype(BF16)
    wsp_bf = w_spatial[0].astype(BF16)
    wr0, br0 = _router_weights(w_router_group[0], b_router_group[0], w_router_expert[0], b_router_expert[0])
    wr1, br1 = _router_weights(w_router_group[1], b_router_group[1], w_router_expert[1], b_router_expert[1])

    part = n // MOE_PARTS
    out = None
    for p in range(MOE_PARTS):
        tok0 = p * part
        batch0 = tok0 // s
        lat_p, *routed = _even_mix(lat, qx, k, v, kc, vc, u, z, attn_sink[0], g_sgu[0][None, :],
                                   wsp_bf, bsp_full, w_out_bf, mods, s, n_ctx,
                                   g_norm2[0][None, :], wr0, br0, tok0, part)
        moe_rows = _moe_rows(routed, 0, w_gate, w_up, w_down)
        lat_p, *routed = _odd_mix(lat_p, moe_rows, mods, 1, g_norm1[1][None, :], w_in_odd_bf, conv_w8,
                                  w_out_odd_bf, s, g_norm2[1][None, :], wr1, br1, batch0)
        yg, wcol = _moe_rows(routed, 1, w_gate, w_up, w_down)
        out = _combine(lat_p, yg, wcol, mods, 1, batch0, gf, s, True, n, tok0, out)
    return out.reshape(b, s, d)
```

```python
import functools

import jax
import jax.numpy as jnp
from jax import lax
from jax.experimental import pallas as pl
from jax.experimental.pallas import tpu as pltpu
from jax.experimental.pallas import tpu_sc as plsc

F32 = jnp.float32
BF16 = jnp.bfloat16
HIGHEST = lax.Precision.HIGHEST

GRID_W = 64
N_Q_HEADS = 8
N_KV_HEADS = 2
HEAD_DIM = 64
ATT_BLOCK = 128
ROPE_BASE = 10000.0
Q_DIM = N_Q_HEADS * HEAD_DIM
KV_DIM = N_KV_HEADS * HEAD_DIM
SG_GROUPS = 8
SG_WIDTH = SG_GROUPS * HEAD_DIM
N_GROUPS = 4
EXPERTS_PER_GROUP = 8
N_EXPERTS = N_GROUPS * EXPERTS_PER_GROUP
EPS = 1e-6
NEG_INF = -1e30

LANES = 128
SC_CORES = 2
SC_SUBCORES = 16
SC_WORKERS = SC_CORES * SC_SUBCORES
SC_CHUNK = 32
MOD_ROWS = 8
ROUTER_ROWS = EXPERTS_PER_GROUP + N_EXPERTS
VMEM_LIMIT = 56 * 1024 * 1024

ADA_TN = 1536
EVEN_TM = 1024
EVEN_SUB = 512
ATT_TQ = 512
ODD_TM = 512
ODD_SUB = 256
MOE_BM = 1024
MOE_QUANTUM = 128
MOE_PARTS = 2
WEIGHT_SLOTS = 3
COMBINE_TM = 512


def _params(sem):
    return pltpu.CompilerParams(dimension_semantics=sem, vmem_limit_bytes=VMEM_LIMIT)


def _rms_mod(x, g, shift, scale):
    ms = jnp.mean(x * x, axis=-1, keepdims=True)
    return (x * lax.rsqrt(ms + EPS)) * (g * (1.0 + scale)) + shift


def _pack_rounded(a):
    w = a.shape[1] // 2
    hi = pltpu.bitcast(a[:, :w], jnp.uint32)
    lo = pltpu.bitcast(a[:, w:], jnp.uint32)
    return hi | (lo >> 16)


def _pack_rows(a):
    return _pack_rounded(a.astype(BF16).astype(F32))


def _unpack_rows(p):
    hi = pltpu.bitcast(p & jnp.uint32(0xFFFF0000), F32)
    lo = pltpu.bitcast(p << 16, F32)
    return jnp.concatenate([hi, lo], axis=1)


def _ada_kernel(a_ref, w_ref, b_ref, o_ref):
    a = a_ref[...]
    s = a * (1.0 / (1.0 + jnp.exp(-a)))
    o_ref[0] = jnp.dot(s, w_ref[0], preferred_element_type=F32, precision=HIGHEST) + b_ref[0]


def _ada(cond, w_ada, b_ada):
    depth, d, six_d = w_ada.shape
    return pl.pallas_call(
        _ada_kernel,
        grid=(depth, six_d // ADA_TN),
        in_specs=[
            pl.BlockSpec((MOD_ROWS, d), lambda l, j: (0, 0)),
            pl.BlockSpec((1, d, ADA_TN), lambda l, j: (l, 0, j)),
            pl.BlockSpec((1, 1, ADA_TN), lambda l, j: (l, 0, j)),
        ],
        out_specs=pl.BlockSpec((1, MOD_ROWS, ADA_TN), lambda l, j: (l, 0, j)),
        out_shape=jax.ShapeDtypeStruct((depth, MOD_ROWS, six_d), F32),
        compiler_params=_params(("arbitrary", "arbitrary")),
        name="ada",
    )(cond, w_ada, b_ada.reshape(depth, 1, six_d))


def _even_in_kernel(x_ref, sh_ref, sc_ref, g_ref, w_ref, cos_ref, sa_ref, sb_ref,
                    qx_ref, k_ref, v_ref, u_ref, z_ref):
    tm = x_ref.shape[0]
    sub = min(tm, EVEN_SUB)
    scale = HEAD_DIM ** -0.5
    low = lax.broadcasted_iota(jnp.int32, (sub, LANES), 1) < HEAD_DIM
    heads_per_kv = N_Q_HEADS // N_KV_HEADS
    u0 = Q_DIM + 2 * KV_DIM

    for r0 in range(0, tm, sub):
        rows = slice(r0, r0 + sub)
        h = _rms_mod(x_ref[rows, :], g_ref[...], sh_ref[...], sc_ref[...])
        p = jnp.dot(h.astype(BF16), w_ref[...], preferred_element_type=F32)
        cos, sa, sb = cos_ref[rows, :], sa_ref[rows, :], sb_ref[rows, :]

        def rope(t):
            return t * cos + pltpu.roll(t, LANES - 16, 1) * sa + pltpu.roll(t, 16, 1) * sb

        for cblk in range(Q_DIM // LANES):
            t = rope(p[:, cblk * LANES:(cblk + 1) * LANES]) * scale
            sw = pltpu.roll(t, HEAD_DIM, 1)
            zero = jnp.zeros_like(t)
            if (2 * cblk) // heads_per_kv == 0:
                first, second = jnp.where(low, t, zero), jnp.where(low, sw, zero)
            else:
                first, second = jnp.where(low, zero, sw), jnp.where(low, zero, t)
            qx_ref[rows, (2 * cblk) * LANES:(2 * cblk + 1) * LANES] = first.astype(BF16)
            qx_ref[rows, (2 * cblk + 1) * LANES:(2 * cblk + 2) * LANES] = second.astype(BF16)

        k_ref[rows, :] = rope(p[:, Q_DIM:Q_DIM + KV_DIM]).astype(BF16)
        v_ref[rows, :] = p[:, Q_DIM + KV_DIM:Q_DIM + 2 * KV_DIM].astype(BF16)
        u_ref[rows, :] = p[:, u0:u0 + SG_WIDTH]
        z_ref[rows, :] = p[:, u0 + SG_WIDTH:u0 + 2 * SG_WIDTH]


def _even_in(x2d, mods, layer, mod_row_fn, g, w_bf, tabs, tab_blocks, tm):
    n, d = x2d.shape
    ein = w_bf.shape[1]
    cos, sa, sb = tabs
    row = lambda i: layer * MOD_ROWS + mod_row_fn(i)
    tab_spec = pl.BlockSpec((tm, LANES), lambda i: (i % tab_blocks, 0))
    qx_dim = N_Q_HEADS * LANES
    return pl.pallas_call(
        _even_in_kernel,
        grid=(n // tm,),
        in_specs=[
            pl.BlockSpec((tm, d), lambda i: (i, 0)),
            pl.BlockSpec((None, 1, d), lambda i: (row(i), 0, 0)),
            pl.BlockSpec((None, 1, d), lambda i: (row(i), 0, 1)),
            pl.BlockSpec((1, d), lambda i: (0, 0)),
            pl.BlockSpec((d, ein), lambda i: (0, 0)),
            tab_spec, tab_spec, tab_spec,
        ],
        out_specs=[
            pl.BlockSpec((tm, qx_dim), lambda i: (i, 0)),
            pl.BlockSpec((tm, KV_DIM), lambda i: (i, 0)),
            pl.BlockSpec((tm, KV_DIM), lambda i: (i, 0)),
            pl.BlockSpec((tm, SG_WIDTH), lambda i: (i, 0)),
            pl.BlockSpec((tm, SG_WIDTH), lambda i: (i, 0)),
        ],
        out_shape=[
            jax.ShapeDtypeStruct((n, qx_dim), BF16),
            jax.ShapeDtypeStruct((n, KV_DIM), BF16),
            jax.ShapeDtypeStruct((n, KV_DIM), BF16),
            jax.ShapeDtypeStruct((n, SG_WIDTH), F32),
            jax.ShapeDtypeStruct((n, SG_WIDTH), F32),
        ],
        compiler_params=_params(("parallel",)),
        name="even_in",
    )(x2d, mods, mods, g, w_bf, cos, sa, sb)


def _gelu(x):
    return 0.5 * x * (1.0 + lax.erf(x * (2.0 ** -0.5)))


def _even_mix_kernel(sink_ref, lat_ref, qx_ref, km_ref, kp_ref, kn_ref, vm_ref, vp_ref, vn_ref,
                     kc_ref, vc_ref, u_ref, z_ref, gsgu_ref, wsp_ref, bsp_ref, wout_ref, gate_ref,
                     sh2_ref, sc2_ref, g2_ref, wr_ref, br_ref,
                     o_ref, h_ref, mi_ref, wc_ref, cnt_ref,
                     kband, vband, mixin, carry, upper, *, tiles_per_seq, tiles_per_part):
    i = pl.program_id(0)
    tq = qx_ref.shape[0]

    @pl.when(i == 0)
    def _():
        _route_init(carry, upper)

    @pl.when(i % tiles_per_part == 0)
    def _():
        carry[...] = jnp.zeros_like(carry)

    blk = ATT_BLOCK
    nsub = tq // blk
    n_ctx = kc_ref.shape[0]
    first = (i % tiles_per_seq) == 0
    last = (i % tiles_per_seq) == tiles_per_seq - 1

    kband[0:blk] = kp_ref[...]
    kband[blk:blk + tq] = km_ref[...]
    kband[blk + tq:] = kn_ref[...]
    vband[0:blk] = vp_ref[...]
    vband[blk:blk + tq] = vm_ref[...]
    vband[blk + tq:] = vn_ref[...]

    rows = N_Q_HEADS * blk
    tok = lax.broadcasted_iota(jnp.int32, (rows, blk), 0) & (blk - 1)
    col = lax.broadcasted_iota(jnp.int32, (rows, blk), 1)
    tri_prev = col >= tok
    tri_next = col <= tok
    head = lax.broadcasted_iota(jnp.int32, (rows, 1), 0) // blk
    sink_col = jnp.zeros((rows, 1), F32)
    for hd in range(N_Q_HEADS):
        sink_col = jnp.where(head == hd, sink_ref[hd], sink_col)
    lane_low = lax.broadcasted_iota(jnp.int32, (blk, LANES), 1) < HEAD_DIM
    ones = jnp.ones((n_ctx + 3 * blk, LANES), BF16)
    nt = (((1,), (1,)), ((), ()))

    def sub_block(j, c):
        r0 = pl.multiple_of(j * blk, blk)
        ok_prev = jnp.logical_not(jnp.logical_and(first, j == 0))
        ok_next = jnp.logical_not(jnp.logical_and(last, j == nsub - 1))
        qs = jnp.concatenate([qx_ref[pl.ds(r0, blk), hd * LANES:(hd + 1) * LANES]
                              for hd in range(N_Q_HEADS)], axis=0)
        kall = jnp.concatenate([kc_ref[...], kband[pl.ds(r0, 3 * blk), :]], axis=0)
        vall = jnp.concatenate([vc_ref[...], vband[pl.ds(r0, 3 * blk), :]], axis=0)
        s = lax.dot_general(qs, kall, nt, preferred_element_type=F32)
        c0 = n_ctx
        s = jnp.concatenate([
            s[:, :c0],
            jnp.where(jnp.logical_and(tri_prev, ok_prev), s[:, c0:c0 + blk], NEG_INF),
            s[:, c0 + blk:c0 + 2 * blk],
            jnp.where(jnp.logical_and(tri_next, ok_next), s[:, c0 + 2 * blk:], NEG_INF),
        ], axis=1)
        m = jnp.maximum(jnp.max(s, axis=-1, keepdims=True), sink_col)
        p = jnp.exp(s - m).astype(BF16)
        o = jnp.dot(p, jnp.concatenate([vall, ones], axis=1), preferred_element_type=F32)
        att = o[:, :LANES] / (o[:, LANES:] + jnp.exp(sink_col - m))
        half = N_Q_HEADS // 2
        for hd in range(half):
            pair = jnp.where(lane_low, att[hd * blk:(hd + 1) * blk], att[(hd + half) * blk:(hd + half + 1) * blk])
            mixin[pl.ds(r0, blk), hd * LANES:(hd + 1) * LANES] = pair.astype(BF16)

        ug = _gelu(u_ref[pl.ds(r0, blk), :])
        zg = _gelu(z_ref[pl.ds(r0, blk), :])
        mu = jnp.mean(zg, axis=-1, keepdims=True)
        zc = zg - mu
        zn = zc * lax.rsqrt(jnp.mean(zc * zc, axis=-1, keepdims=True) + EPS) * gsgu_ref[...]
        for pair in range(SG_GROUPS // 2):
            zp = zn[:, pair * LANES:(pair + 1) * LANES]
            zero = jnp.zeros_like(zp)
            lo = jnp.where(lane_low, zp, zero).astype(BF16)
            hi = jnp.where(lane_low, zero, zp).astype(BF16)
            sg = (jnp.dot(wsp_ref[2 * pair], lo, preferred_element_type=F32)
                  + jnp.dot(wsp_ref[2 * pair + 1], hi, preferred_element_type=F32)
                  + bsp_ref[:, pair * LANES:(pair + 1) * LANES])
            mixin[pl.ds(r0, blk), Q_DIM + pair * LANES:Q_DIM + (pair + 1) * LANES] = (
                ug[:, pair * LANES:(pair + 1) * LANES] * sg).astype(BF16)
        return c

    lax.fori_loop(0, nsub, sub_block, 0, unroll=True)
    mix = jnp.dot(mixin[...], wout_ref[...], preferred_element_type=F32)
    lat = lat_ref[...] + gate_ref[...] * mix
    o_ref[...] = lat
    _route_tile(lat, sh2_ref[...], sc2_ref[...], g2_ref[...], wr_ref, br_ref, carry, upper,
                h_ref, mi_ref, wc_ref, cnt_ref)


def _even_mix(lat, qx, k, v, kc, vc, u, z, sink, g_sgu, wsp_bf, bsp_full, wout_bf, mods, seq, ctx_len,
              g2, wr_t, br_t, tok0, n):
    d = lat.shape[1]
    tq = ATT_TQ
    tiles_per_seq = seq // tq
    sub = tq // ATT_BLOCK
    nblk = lat.shape[0] // ATT_BLOCK
    t0 = tok0 // tq
    main = lambda w: pl.BlockSpec((tq, w), lambda i: (t0 + i, 0))
    prev = pl.BlockSpec((ATT_BLOCK, KV_DIM), lambda i: (jnp.maximum((t0 + i) * sub - 1, 0), 0))
    nxt = pl.BlockSpec((ATT_BLOCK, KV_DIM), lambda i: (jnp.minimum((t0 + i + 1) * sub, nblk - 1), 0))
    batch = lambda t: (t0 + t) // tiles_per_seq
    ctxs = pl.BlockSpec((ctx_len, KV_DIM), lambda i: (batch(i), 0))
    const = lambda shape: pl.BlockSpec(shape, lambda i: (0,) * len(shape), pipeline_mode=pl.Buffered(1))
    r_in, r_out, r_shapes, r_scratch = _route_specs(n, d, tq, lambda i: i, batch)
    return pl.pallas_call(
        functools.partial(_even_mix_kernel, tiles_per_seq=tiles_per_seq, tiles_per_part=n // tq),
        grid=(n // tq,),
        in_specs=[
            pl.BlockSpec(memory_space=pltpu.SMEM),
            main(d), main(qx.shape[1]),
            main(KV_DIM), prev, nxt,
            main(KV_DIM), prev, nxt,
            ctxs, ctxs,
            main(SG_WIDTH), main(SG_WIDTH),
            const((1, SG_WIDTH)), const(wsp_bf.shape), const(bsp_full.shape), const(wout_bf.shape),
            pl.BlockSpec((None, 1, d), lambda i: (batch(i), 0, 2)),
        ] + r_in,
        out_specs=[pl.BlockSpec((tq, d), lambda i: (i, 0))] + r_out,
        out_shape=[jax.ShapeDtypeStruct((n, d), F32)] + r_shapes,
        scratch_shapes=[
            pltpu.VMEM((tq + 2 * ATT_BLOCK, KV_DIM), BF16),
            pltpu.VMEM((tq + 2 * ATT_BLOCK, KV_DIM), BF16),
            pltpu.VMEM((tq, Q_DIM + SG_WIDTH), BF16),
        ] + r_scratch,
        compiler_params=_params(("arbitrary",)),
        name="even_mix",
    )(sink, lat, qx, k, k, k, v, v, v, kc, vc, u, z, g_sgu, wsp_bf, bsp_full, wout_bf, mods,
      mods, mods, g2, wr_t, br_t)


def _odd_mix_kernel(x_ref, e0_ref, e1_ref, ew_ref, egate_ref, sh_ref, sc_ref, gate_ref, g_ref,
                    win_ref, cw_ref, wout_ref, sh2_ref, sc2_ref, g2_ref, wr_ref, br_ref,
                    o_ref, h_ref, mi_ref, wc_ref, cnt_ref,
                    y_s, bg_s, tail_s, x_s, carry, upper, *, tiles_per_seq, tiles_per_part):
    i = pl.program_id(0)
    n_tiles = pl.num_programs(0) - 1
    tm, d = x_ref.shape
    cur = i % 2
    prv = 1 - cur

    @pl.when(i == 0)
    def _():
        _route_init(carry, upper)

    tail_s[...] = y_s[cur, tm - 8:tm, :]

    @pl.when(i < n_tiles)
    def _():
        for r0 in range(0, tm, ODD_SUB):
            rows = slice(r0, r0 + ODD_SUB)
            ew = ew_ref[rows, :]
            moe = ew[:, 0:1] * _unpack_rows(e0_ref[rows, :]) + ew[:, 1:2] * _unpack_rows(e1_ref[rows, :])
            x = x_ref[rows, :] + egate_ref[...] * moe
            x_s[cur, rows, :] = x
            h = _rms_mod(x, g_ref[...], sh_ref[...], sc_ref[...])
            p = jnp.dot(h.astype(BF16), win_ref[...], preferred_element_type=F32)
            bg_s[cur, rows, :] = p[:, 0:d]
            y_s[cur, rows, :] = p[:, d:2 * d] * p[:, 2 * d:3 * d]

    @pl.when(i >= 1)
    def _():
        t_prev = i - 1
        first = (t_prev % tiles_per_seq) == 0
        last = (t_prev % tiles_per_seq) == tiles_per_seq - 1
        y = y_s[prv]
        left = jnp.where(first, 0.0, tail_s[7:8, :])
        right = jnp.where(last, 0.0, y_s[cur, 0:1, :])
        ridx = lax.broadcasted_iota(jnp.int32, (tm, d), 0)
        y_dn = jnp.where(ridx == 0, left, pltpu.roll(y, 1, 0))
        y_up = jnp.where(ridx == tm - 1, right, pltpu.roll(y, tm - 1, 0))
        conv = y_dn * cw_ref[0:1, :] + y * cw_ref[1:2, :] + y_up * cw_ref[2:3, :]
        mix = jnp.dot((bg_s[prv] * conv).astype(BF16), wout_ref[...], preferred_element_type=F32)
        lat = x_s[prv] + gate_ref[...] * mix
        o_ref[...] = lat

        @pl.when(t_prev % tiles_per_part == 0)
        def _():
            carry[...] = jnp.zeros_like(carry)

        _route_tile(lat, sh2_ref[...], sc2_ref[...], g2_ref[...], wr_ref, br_ref, carry, upper,
                    h_ref, mi_ref, wc_ref, cnt_ref)


def _odd_mix(lat, moe_rows, mods, layer, g, win_bf, conv_w8, wout_bf, seq, g2, wr_t, br_t, batch0):
    yg, ewcol = moe_rows
    n, d = lat.shape
    tm = ODD_TM
    nt = n // tm
    tiles_per_seq = seq // tm
    cur = lambda i: jnp.minimum(i, nt - 1)
    prv = lambda i: jnp.maximum(i - 1, 0)
    row = lambda t: layer * MOD_ROWS + batch0 + t // tiles_per_seq
    const = lambda shape: pl.BlockSpec(shape, lambda i: (0,) * len(shape), pipeline_mode=pl.Buffered(1))
    r_in, r_out, r_shapes, r_scratch = _route_specs(n, d, tm, prv, row)
    return pl.pallas_call(
        functools.partial(_odd_mix_kernel, tiles_per_seq=tiles_per_seq, tiles_per_part=nt),
        grid=(nt + 1,),
        in_specs=[
            pl.BlockSpec((tm, d), lambda i: (cur(i), 0)),
            pl.BlockSpec((tm, d // 2), lambda i: (cur(i), 0)),
            pl.BlockSpec((tm, d // 2), lambda i: (nt + cur(i), 0)),
            pl.BlockSpec((tm, LANES), lambda i: (cur(i), 0)),
            pl.BlockSpec((None, 1, d), lambda i: (row(cur(i)) - MOD_ROWS, 0, 5)),
            pl.BlockSpec((None, 1, d), lambda i: (row(cur(i)), 0, 0)),
            pl.BlockSpec((None, 1, d), lambda i: (row(cur(i)), 0, 1)),
            pl.BlockSpec((None, 1, d), lambda i: (row(prv(i)), 0, 2)),
            const((1, d)), const(win_bf.shape), const(conv_w8.shape), const(wout_bf.shape),
        ] + r_in,
        out_specs=[pl.BlockSpec((tm, d), lambda i: (prv(i), 0))] + r_out,
        out_shape=[jax.ShapeDtypeStruct((n, d), F32)] + r_shapes,
        scratch_shapes=[
            pltpu.VMEM((2, tm, d), F32),
            pltpu.VMEM((2, tm, d), F32),
            pltpu.VMEM((8, d), F32),
            pltpu.VMEM((2, tm, d), F32),
        ] + r_scratch,
        compiler_params=_params(("arbitrary",)),
        name="odd_mix",
    )(lat, yg, yg, ewcol, mods, mods, mods, mods, g, win_bf, conv_w8, wout_bf, mods, mods, g2, wr_t, br_t)


def _route_init(carry, upper):
    tm = upper.shape[0]
    carry[...] = jnp.zeros_like(carry)
    r_i = lax.broadcasted_iota(jnp.int32, (tm, tm), 0)
    c_i = lax.broadcasted_iota(jnp.int32, (tm, tm), 1)
    upper[...] = jnp.where(r_i < c_i, 1.0, 0.0).astype(BF16)


def _route_tile(lat, sh, sc, g, wr_ref, br_ref, carry, upper, h_ref, mi_ref, wc_ref, cnt_ref):
    tm = lat.shape[0]
    epg = EXPERTS_PER_GROUP
    h = _rms_mod(lat, g, sh, sc)
    h_hi = h.astype(BF16)
    h_hi_f = h_hi.astype(F32)
    h_ref[...] = _pack_rounded(h_hi_f)
    h_lo = (h - h_hi_f).astype(BF16)
    w = wr_ref[...]
    w1 = w.astype(BF16).astype(F32)
    r1 = w - w1
    w2 = r1.astype(BF16).astype(F32)
    w3 = r1 - w2
    nt = (((1,), (1,)), ((), ()))
    nr = w.shape[0]
    w123 = jnp.concatenate([w1, w2, w3, jnp.zeros((8, w.shape[1]), F32)], axis=0).astype(BF16)
    w12 = jnp.concatenate([w1, w2], axis=0).astype(BF16)
    p_hi = lax.dot_general(w123, h_hi, nt, preferred_element_type=F32)
    p_lo = lax.dot_general(w12, h_lo, nt, preferred_element_type=F32)
    lg = ((p_hi[2 * nr:3 * nr] + p_lo[nr:2 * nr]) + (p_hi[nr:2 * nr] + p_lo[0:nr])) + p_hi[0:nr] + br_ref[...]
    io8 = lax.broadcasted_iota(jnp.int32, (epg, tm), 0)
    gl = lg[0:epg]
    gmax = jnp.max(gl, axis=0, keepdims=True)
    g_idx = jnp.min(jnp.where(gl == gmax, io8, epg), axis=0, keepdims=True)
    g_w = 1.0 / jnp.sum(jnp.exp(gl - gmax), axis=0, keepdims=True)
    e_sel = lg[epg:2 * epg]
    for gi in range(1, N_GROUPS):
        e_sel = jnp.where(g_idx == gi, lg[(gi + 1) * epg:(gi + 2) * epg], e_sel)
    v0 = jnp.max(e_sel, axis=0, keepdims=True)
    i0 = jnp.min(jnp.where(e_sel == v0, io8, epg), axis=0, keepdims=True)
    rest = jnp.where(io8 == i0, -jnp.inf, e_sel)
    v1 = jnp.max(rest, axis=0, keepdims=True)
    i1 = jnp.min(jnp.where(rest == v1, io8, epg), axis=0, keepdims=True)
    t = jnp.exp(v1 - v0)
    w0 = g_w / (1.0 + t)
    w1 = g_w * t / (1.0 + t)
    e0 = g_idx * epg + i0
    e1 = g_idx * epg + i1

    io32 = lax.broadcasted_iota(jnp.int32, (N_EXPERTS, tm), 0)
    hit0 = io32 == e0
    hit1 = io32 == e1
    onehot = jnp.where(hit0 | hit1, 1.0, 0.0)
    cum = jnp.dot(onehot.astype(BF16), upper[...], preferred_element_type=F32) + carry[...]
    rank0 = jnp.sum(jnp.where(hit0, cum, 0.0), axis=0, keepdims=True).astype(jnp.int32)
    rank1 = jnp.sum(jnp.where(hit1, cum, 0.0), axis=0, keepdims=True).astype(jnp.int32)
    carry[...] = carry[...] + jnp.sum(onehot, axis=1, keepdims=True)
    cnt_ref[...] = jnp.broadcast_to(carry[...], cnt_ref.shape)

    mi_ref[...] = jnp.where(io8 == 0, e0, jnp.where(io8 == 1, e1, jnp.where(io8 == 2, rank0,
                            jnp.where(io8 == 3, rank1, 0))))
    io128 = lax.broadcasted_iota(jnp.int32, (LANES, tm), 0)
    wrow = jnp.where(io128 == 0, w0, jnp.where(io128 == 1, w1, 0.0))
    wc_ref[...] = wrow.T


def _route_specs(n, d, tm, tile_of, mods_row):
    in_specs = [
        pl.BlockSpec((None, 1, d), lambda i: (mods_row(tile_of(i)), 0, 3)),
        pl.BlockSpec((None, 1, d), lambda i: (mods_row(tile_of(i)), 0, 4)),
        pl.BlockSpec((1, d), lambda i: (0, 0), pipeline_mode=pl.Buffered(1)),
        pl.BlockSpec((ROUTER_ROWS, d), lambda i: (0, 0), pipeline_mode=pl.Buffered(1)),
        pl.BlockSpec((ROUTER_ROWS, 1), lambda i: (0, 0), pipeline_mode=pl.Buffered(1)),
    ]
    out_specs = [
        pl.BlockSpec((tm, d // 2), lambda i: (tile_of(i), 0)),
        pl.BlockSpec((8, tm), lambda i: (0, tile_of(i))),
        pl.BlockSpec((tm, LANES), lambda i: (tile_of(i), 0)),
        pl.BlockSpec((N_EXPERTS, LANES), lambda i: (0, 0)),
    ]
    out_shapes = [
        jax.ShapeDtypeStruct((n, d // 2), jnp.uint32),
        jax.ShapeDtypeStruct((8, n), jnp.int32),
        jax.ShapeDtypeStruct((n, LANES), F32),
        jax.ShapeDtypeStruct((N_EXPERTS, LANES), F32),
    ]
    scratch = [pltpu.VMEM((N_EXPERTS, 1), F32), pltpu.VMEM((tm, tm), BF16)]
    return in_specs, out_specs, out_shapes, scratch


def _plan_kernel(cnt_ref, mi_ref, dest_ref, be_ref, runs_ref, nv_ref, nu_ref, ps_ref, *, n_blocks):
    bm = MOE_BM

    def per_expert(e, carry):
        blk0, n_runs = carry
        cnt = cnt_ref[e]
        nb = (cnt + bm - 1) // bm
        ps_ref[e] = blk0 * bm

        def fill(b, c):
            be_ref[b] = e
            nv_ref[b] = jnp.minimum(cnt - (b - blk0) * bm, bm)
            return c

        lax.fori_loop(blk0, blk0 + nb, fill, 0)

        @pl.when(nb > 0)
        def _():
            runs_ref[n_runs] = e

        return blk0 + nb, n_runs + jnp.where(nb > 0, 1, 0)

    n_used, n_runs = lax.fori_loop(0, N_EXPERTS, per_expert, (0, 0))
    nu_ref[0] = n_used
    nu_ref[1] = n_runs
    last_e = be_ref[jnp.maximum(n_used - 1, 0)]

    def fill_tail(b, c):
        be_ref[b] = last_e
        nv_ref[b] = 0
        return c

    lax.fori_loop(n_used, n_blocks, fill_tail, 0)

    def fill_runs(k, c):
        runs_ref[k] = last_e
        return c

    lax.fori_loop(n_runs, N_EXPERTS, fill_runs, 0)

    e01 = mi_ref[0:2, :]
    dest = mi_ref[2:4, :]
    for e in range(N_EXPERTS):
        dest = dest + jnp.where(e01 == e, ps_ref[e], 0)
    dest_ref[...] = dest


def _plan(counts, meta_i, n_blocks):
    n = meta_i.shape[1]
    return pl.pallas_call(
        functools.partial(_plan_kernel, n_blocks=n_blocks),
        in_specs=[pl.BlockSpec(memory_space=pltpu.SMEM), pl.BlockSpec(memory_space=pltpu.VMEM)],
        out_specs=[pl.BlockSpec(memory_space=pltpu.VMEM)] + [pl.BlockSpec(memory_space=pltpu.SMEM)] * 4,
        out_shape=[
            jax.ShapeDtypeStruct((2, n), jnp.int32),
            jax.ShapeDtypeStruct((n_blocks,), jnp.int32),
            jax.ShapeDtypeStruct((N_EXPERTS,), jnp.int32),
            jax.ShapeDtypeStruct((n_blocks,), jnp.int32),
            jax.ShapeDtypeStruct((2,), jnp.int32),
        ],
        scratch_shapes=[pltpu.SMEM((N_EXPERTS,), jnp.int32)],
        compiler_params=pltpu.CompilerParams(vmem_limit_bytes=VMEM_LIMIT),
        name="plan",
    )(counts, meta_i)


def _sc_mesh():
    return plsc.VectorSubcoreMesh(core_axis_name="c", subcore_axis_name="s",
                                  num_cores=SC_CORES, num_subcores=SC_SUBCORES)


def _sc_worker():
    return lax.axis_index("s") * SC_CORES + lax.axis_index("c")


def _sc_dispatch(h2, dest, n_rows):
    n, d = h2.shape
    c = SC_CHUNK
    per_w = n // SC_WORKERS
    nchunk = per_w // c
    idx = dest.reshape(2, SC_WORKERS, nchunk, c)

    @functools.partial(
        pl.kernel, mesh=_sc_mesh(), out_type=jax.ShapeDtypeStruct((n_rows, d), h2.dtype),
        scratch_types=[pltpu.VMEM((nchunk, c), jnp.int32), pltpu.VMEM((nchunk, c), jnp.int32),
                       pltpu.VMEM((2, c, d), h2.dtype),
                       pltpu.SemaphoreType.DMA((2,)), pltpu.SemaphoreType.DMA((2,))])
    def k(h_hbm, idx_hbm, xb_hbm, idx0_v, idx1_v, rows_v, gsem, ssem):
        wid = _sc_worker()
        base = wid * per_w
        idx_v = (idx0_v, idx1_v)
        for kk in range(2):
            pltpu.sync_copy(idx_hbm.at[kk, wid], idx_v[kk])

        def get(j, slot):
            return pltpu.make_async_copy(h_hbm.at[pl.ds(base + j * c, c)], rows_v.at[slot], gsem.at[slot])

        def put(j, slot, kk):
            return pltpu.make_async_copy(rows_v.at[slot], xb_hbm.at[idx_v[kk].at[j]], ssem.at[slot])

        get(0, 0).start()

        @pl.loop(0, nchunk, step=2)
        def _(j):
            for slot in range(2):
                jj = j + slot
                get(jj, slot).wait()

                @pl.when(jj >= 1)
                def _():
                    for kk in range(2):
                        put(jj - 1, 1 - slot, kk).wait()

                @pl.when(jj + 1 < nchunk)
                def _():
                    get(jj + 1, 1 - slot).start()

                for kk in range(2):
                    put(jj, slot, kk).start()

        for kk in range(2):
            put(nchunk - 1, (nchunk - 1) % 2, kk).wait()

    return k(h2, idx)


def _sc_gather(y, dest):
    d = y.shape[1]
    total = dest.shape[0] * dest.shape[1]
    c = SC_CHUNK
    per_w = total // SC_WORKERS
    nchunk = per_w // c
    idx = dest.reshape(SC_WORKERS, nchunk, c)

    @functools.partial(
        pl.kernel, mesh=_sc_mesh(), out_type=jax.ShapeDtypeStruct((total, d), y.dtype),
        scratch_types=[pltpu.VMEM((nchunk, c), jnp.int32), pltpu.VMEM((2, c, d), y.dtype),
                       pltpu.SemaphoreType.DMA((2,)), pltpu.SemaphoreType.DMA((2,))])
    def k(y_hbm, idx_hbm, out_hbm, idx_v, rows_v, gsem, ssem):
        wid = _sc_worker()
        base = wid * per_w
        pltpu.sync_copy(idx_hbm.at[wid], idx_v)

        def get(j, slot):
            return pltpu.make_async_copy(y_hbm.at[idx_v.at[j]], rows_v.at[slot], gsem.at[slot])

        def put(j, slot):
            return pltpu.make_async_copy(rows_v.at[slot], out_hbm.at[pl.ds(base + j * c, c)], ssem.at[slot])

        get(0, 0).start()

        @pl.loop(0, nchunk, step=2)
        def _(j):
            for slot in range(2):
                jj = j + slot
                get(jj, slot).wait()

                @pl.when(jj >= 1)
                def _():
                    put(jj - 1, 1 - slot).wait()

                @pl.when(jj + 1 < nchunk)
                def _():
                    get(jj + 1, 1 - slot).start()

                put(jj, slot).start()

        put(nchunk - 1, (nchunk - 1) % 2).wait()

    return k(y, idx)


def _expert_kernel(be_ref, runs_ref, nv_ref, nu_ref, x_ref, wg_hbm, wu_hbm, wd_hbm, y_ref,
                   wgu_s, wd_s, stg_g, stg_u, stg_d, run_s, sems, *, layer):
    b = pl.program_id(0)
    hid = stg_g.shape[2]
    e = be_ref[b]
    n_runs = nu_ref[1]
    changed = jnp.logical_or(b == 0, e != be_ref[jnp.maximum(b - 1, 0)])

    def fetch(run):
        expert = runs_ref[run]
        slot = run % WEIGHT_SLOTS
        return (pltpu.make_async_copy(wg_hbm.at[layer, expert], stg_g.at[slot], sems.at[slot]),
                pltpu.make_async_copy(wu_hbm.at[layer, expert], stg_u.at[slot], sems.at[slot]),
                pltpu.make_async_copy(wd_hbm.at[layer, expert], stg_d.at[slot], sems.at[slot]))

    @pl.when(b == 0)
    def _():
        for r in range(WEIGHT_SLOTS - 1):
            @pl.when(r < n_runs)
            def _():
                for cp in fetch(r):
                    cp.start()

    @pl.when(changed)
    def _():
        run = jnp.where(b == 0, 0, run_s[0] + 1)
        run_s[0] = run
        for cp in fetch(run):
            cp.wait()

        ahead = run + WEIGHT_SLOTS - 1

        @pl.when(ahead < n_runs)
        def _():
            for cp in fetch(ahead):
                cp.start()

        slot = run % WEIGHT_SLOTS
        wgu_s[:, 0:hid] = stg_g[slot].astype(BF16)
        wgu_s[:, hid:2 * hid] = stg_u[slot].astype(BF16)
        wd_s[...] = stg_d[slot].astype(BF16)

    bm, dp = x_ref.shape
    nv = nv_ref[b]
    in_use = b < nu_ref[0]

    def run(rows):
        live = lax.broadcasted_iota(jnp.int32, (rows, dp), 0) < nv
        x = _unpack_rows(jnp.where(live, x_ref[0:rows, :], jnp.uint32(0)))
        gu = jnp.dot(x.astype(BF16), wgu_s[...], preferred_element_type=F32)
        gate = gu[:, 0:hid]
        act = gate * (1.0 / (1.0 + jnp.exp(-gate))) * gu[:, hid:2 * hid]
        y_ref[0:rows, :] = _pack_rows(jnp.dot(act.astype(BF16), wd_s[...], preferred_element_type=F32))

    n_quanta = bm // MOE_QUANTUM
    for q in range(1, n_quanta + 1):
        rows = q * MOE_QUANTUM

        @pl.when(jnp.logical_and(in_use, jnp.logical_and(nv > rows - MOE_QUANTUM, nv <= rows)))
        def _(rows=rows):
            run(rows)
            if rows < bm:
                y_ref[rows:bm, :] = jnp.zeros((bm - rows, dp), y_ref.dtype)

    @pl.when(jnp.logical_not(in_use))
    def _():
        y_ref[...] = jnp.zeros_like(y_ref)


def _experts(block_e, runs, n_valid, n_used, xb, w_gate, w_up, w_down, layer):
    n_rows, dp = xb.shape
    d, hid = w_gate.shape[2], w_gate.shape[3]
    bm = MOE_BM
    n_blocks = n_rows // bm
    hbm = pl.BlockSpec(memory_space=pl.ANY)
    return pl.pallas_call(
        functools.partial(_expert_kernel, layer=layer),
        grid_spec=pltpu.PrefetchScalarGridSpec(
            num_scalar_prefetch=4,
            grid=(n_blocks,),
            in_specs=[
                pl.BlockSpec((bm, dp), lambda b, be, nx, nv, nu: (jnp.minimum(b, nu[0] - 1), 0)),
                hbm, hbm, hbm,
            ],
            out_specs=pl.BlockSpec((bm, dp), lambda b, be, nx, nv, nu: (b, 0)),
            scratch_shapes=[
                pltpu.VMEM((d, 2 * hid), BF16), pltpu.VMEM((hid, d), BF16),
                pltpu.VMEM((WEIGHT_SLOTS, d, hid), F32), pltpu.VMEM((WEIGHT_SLOTS, d, hid), F32),
                pltpu.VMEM((WEIGHT_SLOTS, hid, d), F32),
                pltpu.SMEM((1,), jnp.int32), pltpu.SemaphoreType.DMA((WEIGHT_SLOTS,)),
            ],
        ),
        out_shape=jax.ShapeDtypeStruct((n_rows, dp), jnp.uint32),
        compiler_params=_params(("arbitrary",)),
        name="experts",
    )(block_e, runs, n_valid, n_used, xb, w_gate, w_up, w_down)


def _combine_kernel(lat_ref, y0_ref, y1_ref, wc_ref, gate_ref, gf_ref, *rest, final):
    o_ref = rest[-1]
    wc = wc_ref[...]
    moe = wc[:, 0:1] * _unpack_rows(y0_ref[...]) + wc[:, 1:2] * _unpack_rows(y1_ref[...])
    out = lat_ref[...] + gate_ref[...] * moe
    if final:
        ms = jnp.mean(out * out, axis=-1, keepdims=True)
        out = out * lax.rsqrt(ms + EPS) * gf_ref[...]
    o_ref[...] = out


def _combine(lat, yg, wcol, mods, layer, batch0, g_final, seq, final, out_rows, tok0, prev_out):
    n, d = lat.shape
    tm = COMBINE_TM
    nt = n // tm
    t0 = tok0 // tm
    tiles_per_seq = seq // tm
    row = lambda i: layer * MOD_ROWS + batch0 + i // tiles_per_seq
    in_specs = [
        pl.BlockSpec((tm, d), lambda i: (i, 0)),
        pl.BlockSpec((tm, d // 2), lambda i: (i, 0)),
        pl.BlockSpec((tm, d // 2), lambda i: (nt + i, 0)),
        pl.BlockSpec((tm, LANES), lambda i: (i, 0)),
        pl.BlockSpec((None, 1, d), lambda i: (row(i), 0, 5)),
        pl.BlockSpec((1, d), lambda i: (0, 0)),
    ]
    args = [lat, yg, yg, wcol, mods, g_final]
    aliases = {}
    if prev_out is not None:
        in_specs.append(pl.BlockSpec(memory_space=pl.ANY))
        args.append(prev_out)
        aliases = {len(args) - 1: 0}
    return pl.pallas_call(
        functools.partial(_combine_kernel, final=final),
        grid=(nt,),
        in_specs=in_specs,
        out_specs=pl.BlockSpec((tm, d), lambda i: (t0 + i, 0)),
        out_shape=jax.ShapeDtypeStruct((out_rows, d), F32),
        input_output_aliases=aliases,
        compiler_params=_params(("parallel",)),
        name="combine",
    )(*args)


def _moe_rows(routed, layer, w_gate, w_up, w_down):
    h2, meta_i, wcol, counts = routed
    n = h2.shape[0]
    n_blocks = (2 * n) // MOE_BM + N_EXPERTS
    dest, block_e, runs, n_valid, n_used = _plan(counts[:, 0].astype(jnp.int32), meta_i, n_blocks)
    xb = _sc_dispatch(h2, dest, n_blocks * MOE_BM)
    yb = _experts(block_e, runs, n_valid, n_used, xb, w_gate, w_up, w_down, layer)
    return _sc_gather(yb, dest), wcol


def _rope_tables(seq):
    quarter = HEAD_DIM // 4
    pos = jnp.arange(seq, dtype=F32)
    row_ids = jnp.floor(pos / GRID_W)
    col_ids = pos - row_ids * GRID_W
    inv = ROPE_BASE ** (-jnp.arange(quarter, dtype=F32) / quarter)
    ang_r = row_ids[:, None] * inv
    ang_c = col_ids[:, None] * inv
    zero = jnp.zeros_like(ang_r)
    cos = jnp.concatenate([jnp.cos(ang_r), jnp.cos(ang_r), jnp.cos(ang_c), jnp.cos(ang_c)], axis=-1)
    sa = jnp.concatenate([-jnp.sin(ang_r), zero, -jnp.sin(ang_c), zero], axis=-1)
    sb = jnp.concatenate([zero, jnp.sin(ang_r), zero, jnp.sin(ang_c)], axis=-1)
    rep = LANES // HEAD_DIM
    return tuple(jnp.tile(t, (1, rep)) for t in (cos, sa, sb))


def _router_weights(w_rg, b_rg, w_re, b_re):
    d = w_rg.shape[0]
    pad = EXPERTS_PER_GROUP - N_GROUPS
    wr_t = jnp.concatenate([w_rg.T, jnp.zeros((pad, d), F32), w_re.T], axis=0)
    br_t = jnp.concatenate([b_rg, jnp.full((pad,), NEG_INF, F32), b_re])[:, None]
    return wr_t, br_t


def kernel(x, c, ctx, c_ctx, w_ada, b_ada, g_norm1, g_norm2, g_final, w_in_even, attn_sink, g_sgu,
           w_spatial, b_spatial, w_out_even, w_in_odd, conv_w, w_out_odd, w_router_group,
           b_router_group, w_router_expert, b_router_expert, w_gate, w_up, w_down):
    b, s, d = x.shape
    n = b * s
    n_ctx = ctx.shape[1]
    depth = w_ada.shape[0]
    assert depth == 2 and b + 1 <= MOD_ROWS

    cond = jnp.concatenate([c, c_ctx[None, :], jnp.zeros((MOD_ROWS - b - 1, d), F32)], axis=0)
    mods = _ada(cond, w_ada, b_ada).reshape(depth * MOD_ROWS, 1, 6 * d)
    gf = g_final[None, :]

    lat = x.reshape(n, d)
    w_in_bf = w_in_even[0].astype(BF16)
    tabs = _rope_tables(s)
    qx, k, v, u, z = _even_in(lat, mods, 0, lambda i: i // (s // EVEN_TM), g_norm1[0][None, :], w_in_bf,
                              tabs, s // EVEN_TM, EVEN_TM)
    ones = jnp.ones((n_ctx, LANES), F32)
    zeros = jnp.zeros((n_ctx, LANES), F32)
    _, kc, vc, _, _ = _even_in(ctx.reshape(b * n_ctx, d), mods, 0, lambda i: b, g_norm1[0][None, :],
                               w_in_bf, (ones, zeros, zeros), 1, n_ctx)
    bsp_full = jnp.repeat(b_spatial[0].T, HEAD_DIM, axis=1)
    half = N_Q_HEADS // 2
    w_att = w_out_even[0][:Q_DIM].reshape(2, half, HEAD_DIM, d).transpose(1, 0, 2, 3).reshape(Q_DIM, d)
    w_out_bf = jnp.concatenate([w_att, w_out_even[0][Q_DIM:]], axis=0).astype(BF16)
    conv_w8 = jnp.concatenate([conv_w[0], jnp.zeros((8 - conv_w.shape[1], d), F32)], axis=0)
    w_in_odd_bf = w_in_odd[0].astype(BF16)
    w_out_odd_bf = w_out_odd[0].astype(BF16)
    wsp_bf = w_spatial[0].astype(BF16)
    wr0, br0 = _router_weights(w_router_group[0], b_router_group[0], w_router_expert[0], b_router_expert[0])
    wr1, br1 = _router_weights(w_router_group[1], b_router_group[1], w_router_expert[1], b_router_expert[1])

    part = n // MOE_PARTS
    out = None
    for p in range(MOE_PARTS):
        tok0 = p * part
        batch0 = tok0 // s
        lat_p, *routed = _even_mix(lat, qx, k, v, kc, vc, u, z, attn_sink[0], g_sgu[0][None, :],
                                   wsp_bf, bsp_full, w_out_bf, mods, s, n_ctx,
                                   g_norm2[0][None, :], wr0, br0, tok0, part)
        moe_rows = _moe_rows(routed, 0, w_gate, w_up, w_down)
        lat_p, *routed = _odd_mix(lat_p, moe_rows, mods, 1, g_norm1[1][None, :], w_in_odd_bf, conv_w8,
                                  w_out_odd_bf, s, g_norm2[1][None, :], wr1, br1, batch0)
        yg, wcol = _moe_rows(routed, 1, w_gate, w_up, w_down)
        out = _combine(lat_p, yg, wcol, mods, 1, batch0, gf, s, True, n, tok0, out)
    return out.reshape(b, s, d)
```

```python
import functools

import jax
import jax.numpy as jnp
from jax import lax
from jax.experimental import pallas as pl
from jax.experimental.pallas import tpu as pltpu
from jax.experimental.pallas import tpu_sc as plsc

F32 = jnp.float32
BF16 = jnp.bfloat16
HIGHEST = lax.Precision.HIGHEST

GRID_W = 64
N_Q_HEADS = 8
N_KV_HEADS = 2
HEAD_DIM = 64
ATT_BLOCK = 128
ROPE_BASE = 10000.0
Q_DIM = N_Q_HEADS * HEAD_DIM
KV_DIM = N_KV_HEADS * HEAD_DIM
SG_GROUPS = 8
SG_WIDTH = SG_GROUPS * HEAD_DIM
N_GROUPS = 4
EXPERTS_PER_GROUP = 8
N_EXPERTS = N_GROUPS * EXPERTS_PER_GROUP
EPS = 1e-6
NEG_INF = -1e30

LANES = 128
SC_CORES = 2
SC_SUBCORES = 16
SC_WORKERS = SC_CORES * SC_SUBCORES
SC_CHUNK = 32
MOD_ROWS = 8
ROUTER_ROWS = EXPERTS_PER_GROUP + N_EXPERTS
VMEM_LIMIT = 56 * 1024 * 1024

ADA_TN = 1536
EVEN_TM = 1024
EVEN_SUB = 512
ATT_TQ = 512
ODD_TM = 512
ODD_SUB = 256
ROUTE_SUB = 256
MOE_BM = 1024
MOE_QUANTUM = 128
MOE_PARTS = 2
WEIGHT_SLOTS = 3
COMBINE_TM = 512


def _params(sem):
    return pltpu.CompilerParams(dimension_semantics=sem, vmem_limit_bytes=VMEM_LIMIT)


def _rms_mod(x, g, shift, scale):
    ms = jnp.mean(x * x, axis=-1, keepdims=True)
    return (x * lax.rsqrt(ms + EPS)) * (g * (1.0 + scale)) + shift


def _pack_rounded(a):
    w = a.shape[1] // 2
    hi = pltpu.bitcast(a[:, :w], jnp.uint32)
    lo = pltpu.bitcast(a[:, w:], jnp.uint32)
    return hi | (lo >> 16)


def _pack_rows(a):
    return _pack_rounded(a.astype(BF16).astype(F32))


def _unpack_rows(p):
    hi = pltpu.bitcast(p & jnp.uint32(0xFFFF0000), F32)
    lo = pltpu.bitcast(p << 16, F32)
    return jnp.concatenate([hi, lo], axis=1)


def _ada_kernel(a_ref, w_ref, b_ref, o_ref):
    a = a_ref[...]
    s = a * (1.0 / (1.0 + jnp.exp(-a)))
    o_ref[0] = jnp.dot(s, w_ref[0], preferred_element_type=F32, precision=HIGHEST) + b_ref[0]


def _ada(cond, w_ada, b_ada):
    depth, d, six_d = w_ada.shape
    return pl.pallas_call(
        _ada_kernel,
        grid=(depth, six_d // ADA_TN),
        in_specs=[
            pl.BlockSpec((MOD_ROWS, d), lambda l, j: (0, 0)),
            pl.BlockSpec((1, d, ADA_TN), lambda l, j: (l, 0, j)),
            pl.BlockSpec((1, 1, ADA_TN), lambda l, j: (l, 0, j)),
        ],
        out_specs=pl.BlockSpec((1, MOD_ROWS, ADA_TN), lambda l, j: (l, 0, j)),
        out_shape=jax.ShapeDtypeStruct((depth, MOD_ROWS, six_d), F32),
        compiler_params=_params(("arbitrary", "arbitrary")),
        name="ada",
    )(cond, w_ada, b_ada.reshape(depth, 1, six_d))


def _even_in_kernel(x_ref, sh_ref, sc_ref, g_ref, w_ref, cos_ref, sa_ref, sb_ref,
                    qx_ref, k_ref, v_ref, u_ref, z_ref):
    tm = x_ref.shape[0]
    sub = min(tm, EVEN_SUB)
    scale = HEAD_DIM ** -0.5
    low = lax.broadcasted_iota(jnp.int32, (sub, LANES), 1) < HEAD_DIM
    heads_per_kv = N_Q_HEADS // N_KV_HEADS
    u0 = Q_DIM + 2 * KV_DIM

    for r0 in range(0, tm, sub):
        rows = slice(r0, r0 + sub)
        h = _rms_mod(x_ref[rows, :], g_ref[...], sh_ref[...], sc_ref[...])
        p = jnp.dot(h.astype(BF16), w_ref[...], preferred_element_type=F32)
        cos, sa, sb = cos_ref[rows, :], sa_ref[rows, :], sb_ref[rows, :]

        def rope(t):
            return t * cos + pltpu.roll(t, LANES - 16, 1) * sa + pltpu.roll(t, 16, 1) * sb

        for cblk in range(Q_DIM // LANES):
            t = rope(p[:, cblk * LANES:(cblk + 1) * LANES]) * scale
            sw = pltpu.roll(t, HEAD_DIM, 1)
            zero = jnp.zeros_like(t)
            if (2 * cblk) // heads_per_kv == 0:
                first, second = jnp.where(low, t, zero), jnp.where(low, sw, zero)
            else:
                first, second = jnp.where(low, zero, sw), jnp.where(low, zero, t)
            qx_ref[rows, (2 * cblk) * LANES:(2 * cblk + 1) * LANES] = first.astype(BF16)
            qx_ref[rows, (2 * cblk + 1) * LANES:(2 * cblk + 2) * LANES] = second.astype(BF16)

        k_ref[rows, :] = rope(p[:, Q_DIM:Q_DIM + KV_DIM]).astype(BF16)
        v_ref[rows, :] = p[:, Q_DIM + KV_DIM:Q_DIM + 2 * KV_DIM].astype(BF16)
        u_ref[rows, :] = p[:, u0:u0 + SG_WIDTH]
        z_ref[rows, :] = p[:, u0 + SG_WIDTH:u0 + 2 * SG_WIDTH]


def _even_in(x2d, mods, layer, mod_row_fn, g, w_bf, tabs, tab_blocks, tm):
    n, d = x2d.shape
    ein = w_bf.shape[1]
    cos, sa, sb = tabs
    row = lambda i: layer * MOD_ROWS + mod_row_fn(i)
    tab_spec = pl.BlockSpec((tm, LANES), lambda i: (i % tab_blocks, 0))
    qx_dim = N_Q_HEADS * LANES
    return pl.pallas_call(
        _even_in_kernel,
        grid=(n // tm,),
        in_specs=[
            pl.BlockSpec((tm, d), lambda i: (i, 0)),
            pl.BlockSpec((None, 1, d), lambda i: (row(i), 0, 0)),
            pl.BlockSpec((None, 1, d), lambda i: (row(i), 0, 1)),
            pl.BlockSpec((1, d), lambda i: (0, 0)),
            pl.BlockSpec((d, ein), lambda i: (0, 0)),
            tab_spec, tab_spec, tab_spec,
        ],
        out_specs=[
            pl.BlockSpec((tm, qx_dim), lambda i: (i, 0)),
            pl.BlockSpec((tm, KV_DIM), lambda i: (i, 0)),
            pl.BlockSpec((tm, KV_DIM), lambda i: (i, 0)),
            pl.BlockSpec((tm, SG_WIDTH), lambda i: (i, 0)),
            pl.BlockSpec((tm, SG_WIDTH), lambda i: (i, 0)),
        ],
        out_shape=[
            jax.ShapeDtypeStruct((n, qx_dim), BF16),
            jax.ShapeDtypeStruct((n, KV_DIM), BF16),
            jax.ShapeDtypeStruct((n, KV_DIM), BF16),
            jax.ShapeDtypeStruct((n, SG_WIDTH), F32),
            jax.ShapeDtypeStruct((n, SG_WIDTH), F32),
        ],
        compiler_params=_params(("parallel",)),
        name="even_in",
    )(x2d, mods, mods, g, w_bf, cos, sa, sb)


def _gelu(x):
    return 0.5 * x * (1.0 + lax.erf(x * (2.0 ** -0.5)))


def _even_mix_kernel(sink_ref, lat_ref, qx_ref, km_ref, kp_ref, kn_ref, vm_ref, vp_ref, vn_ref,
                     kc_ref, vc_ref, u_ref, z_ref, gsgu_ref, wsp_ref, bsp_ref, wout_ref, gate_ref,
                     sh2_ref, sc2_ref, g2_ref, wr_ref, br_ref,
                     o_ref, h_ref, mi_ref, wc_ref, cnt_ref,
                     kband, vband, mixin, carry, upper, *, tiles_per_seq, tiles_per_part):
    i = pl.program_id(0)
    tq = qx_ref.shape[0]

    @pl.when(i == 0)
    def _():
        _route_init(carry, upper)

    @pl.when(i % tiles_per_part == 0)
    def _():
        carry[...] = jnp.zeros_like(carry)

    blk = ATT_BLOCK
    nsub = tq // blk
    n_ctx = kc_ref.shape[0]
    first = (i % tiles_per_seq) == 0
    last = (i % tiles_per_seq) == tiles_per_seq - 1

    kband[0:blk] = kp_ref[...]
    kband[blk:blk + tq] = km_ref[...]
    kband[blk + tq:] = kn_ref[...]
    vband[0:blk] = vp_ref[...]
    vband[blk:blk + tq] = vm_ref[...]
    vband[blk + tq:] = vn_ref[...]

    rows = N_Q_HEADS * blk
    tok = lax.broadcasted_iota(jnp.int32, (rows, blk), 0) & (blk - 1)
    col = lax.broadcasted_iota(jnp.int32, (rows, blk), 1)
    tri_prev = col >= tok
    tri_next = col <= tok
    head = lax.broadcasted_iota(jnp.int32, (rows, 1), 0) // blk
    sink_col = jnp.zeros((rows, 1), F32)
    for hd in range(N_Q_HEADS):
        sink_col = jnp.where(head == hd, sink_ref[hd], sink_col)
    lane_low = lax.broadcasted_iota(jnp.int32, (blk, LANES), 1) < HEAD_DIM
    ones = jnp.ones((n_ctx + 3 * blk, LANES), BF16)
    nt = (((1,), (1,)), ((), ()))

    def sub_block(j, c):
        r0 = pl.multiple_of(j * blk, blk)
        ok_prev = jnp.logical_not(jnp.logical_and(first, j == 0))
        ok_next = jnp.logical_not(jnp.logical_and(last, j == nsub - 1))
        qs = jnp.concatenate([qx_ref[pl.ds(r0, blk), hd * LANES:(hd + 1) * LANES]
                              for hd in range(N_Q_HEADS)], axis=0)
        kall = jnp.concatenate([kc_ref[...], kband[pl.ds(r0, 3 * blk), :]], axis=0)
        vall = jnp.concatenate([vc_ref[...], vband[pl.ds(r0, 3 * blk), :]], axis=0)
        s = lax.dot_general(qs, kall, nt, preferred_element_type=F32)
        c0 = n_ctx
        s = jnp.concatenate([
            s[:, :c0],
            jnp.where(jnp.logical_and(tri_prev, ok_prev), s[:, c0:c0 + blk], NEG_INF),
            s[:, c0 + blk:c0 + 2 * blk],
            jnp.where(jnp.logical_and(tri_next, ok_next), s[:, c0 + 2 * blk:], NEG_INF),
        ], axis=1)
        m = jnp.maximum(jnp.max(s, axis=-1, keepdims=True), sink_col)
        p = jnp.exp(s - m).astype(BF16)
        o = jnp.dot(p, jnp.concatenate([vall, ones], axis=1), preferred_element_type=F32)
        att = o[:, :LANES] / (o[:, LANES:] + jnp.exp(sink_col - m))
        half = N_Q_HEADS // 2
        for hd in range(half):
            pair = jnp.where(lane_low, att[hd * blk:(hd + 1) * blk], att[(hd + half) * blk:(hd + half + 1) * blk])
            mixin[pl.ds(r0, blk), hd * LANES:(hd + 1) * LANES] = pair.astype(BF16)

        ug = _gelu(u_ref[pl.ds(r0, blk), :])
        zg = _gelu(z_ref[pl.ds(r0, blk), :])
        mu = jnp.mean(zg, axis=-1, keepdims=True)
        zc = zg - mu
        zn = zc * lax.rsqrt(jnp.mean(zc * zc, axis=-1, keepdims=True) + EPS) * gsgu_ref[...]
        for pair in range(SG_GROUPS // 2):
            zp = zn[:, pair * LANES:(pair + 1) * LANES]
            zero = jnp.zeros_like(zp)
            lo = jnp.where(lane_low, zp, zero).astype(BF16)
            hi = jnp.where(lane_low, zero, zp).astype(BF16)
            sg = (jnp.dot(wsp_ref[2 * pair], lo, preferred_element_type=F32)
                  + jnp.dot(wsp_ref[2 * pair + 1], hi, preferred_element_type=F32)
                  + bsp_ref[:, pair * LANES:(pair + 1) * LANES])
            mixin[pl.ds(r0, blk), Q_DIM + pair * LANES:Q_DIM + (pair + 1) * LANES] = (
                ug[:, pair * LANES:(pair + 1) * LANES] * sg).astype(BF16)
        return c

    lax.fori_loop(0, nsub, sub_block, 0, unroll=True)
    for r0 in range(0, tq, ROUTE_SUB):
        rows = slice(r0, r0 + ROUTE_SUB)
        mix = jnp.dot(mixin[rows, :], wout_ref[...], preferred_element_type=F32)
        lat = lat_ref[rows, :] + gate_ref[...] * mix
        o_ref[rows, :] = lat
        _route_tile(lat, sh2_ref[...], sc2_ref[...], g2_ref[...], wr_ref, br_ref, carry, upper,
                    h_ref.at[rows, :], mi_ref.at[:, rows], wc_ref.at[rows, :], cnt_ref)


def _even_mix(lat, qx, k, v, kc, vc, u, z, sink, g_sgu, wsp_bf, bsp_full, wout_bf, mods, seq, ctx_len,
              g2, wr_t, br_t, tok0, n):
    d = lat.shape[1]
    tq = ATT_TQ
    tiles_per_seq = seq // tq
    sub = tq // ATT_BLOCK
    nblk = lat.shape[0] // ATT_BLOCK
    t0 = tok0 // tq
    main = lambda w: pl.BlockSpec((tq, w), lambda i: (t0 + i, 0))
    prev = pl.BlockSpec((ATT_BLOCK, KV_DIM), lambda i: (jnp.maximum((t0 + i) * sub - 1, 0), 0))
    nxt = pl.BlockSpec((ATT_BLOCK, KV_DIM), lambda i: (jnp.minimum((t0 + i + 1) * sub, nblk - 1), 0))
    batch = lambda t: (t0 + t) // tiles_per_seq
    ctxs = pl.BlockSpec((ctx_len, KV_DIM), lambda i: (batch(i), 0))
    const = lambda shape: pl.BlockSpec(shape, lambda i: (0,) * len(shape), pipeline_mode=pl.Buffered(1))
    r_in, r_out, r_shapes, r_scratch = _route_specs(n, d, tq, lambda i: i, batch, ROUTE_SUB)
    return pl.pallas_call(
        functools.partial(_even_mix_kernel, tiles_per_seq=tiles_per_seq, tiles_per_part=n // tq),
        grid=(n // tq,),
        in_specs=[
            pl.BlockSpec(memory_space=pltpu.SMEM),
            main(d), main(qx.shape[1]),
            main(KV_DIM), prev, nxt,
            main(KV_DIM), prev, nxt,
            ctxs, ctxs,
            main(SG_WIDTH), main(SG_WIDTH),
            const((1, SG_WIDTH)), const(wsp_bf.shape), const(bsp_full.shape), const(wout_bf.shape),
            pl.BlockSpec((None, 1, d), lambda i: (batch(i), 0, 2)),
        ] + r_in,
        out_specs=[pl.BlockSpec((tq, d), lambda i: (i, 0))] + r_out,
        out_shape=[jax.ShapeDtypeStruct((n, d), F32)] + r_shapes,
        scratch_shapes=[
            pltpu.VMEM((tq + 2 * ATT_BLOCK, KV_DIM), BF16),
            pltpu.VMEM((tq + 2 * ATT_BLOCK, KV_DIM), BF16),
            pltpu.VMEM((tq, Q_DIM + SG_WIDTH), BF16),
        ] + r_scratch,
        compiler_params=_params(("arbitrary",)),
        name="even_mix",
    )(sink, lat, qx, k, k, k, v, v, v, kc, vc, u, z, g_sgu, wsp_bf, bsp_full, wout_bf, mods,
      mods, mods, g2, wr_t, br_t)


def _odd_mix_kernel(x_ref, e0_ref, e1_ref, ew_ref, egate_ref, sh_ref, sc_ref, gate_ref, g_ref,
                    win_ref, cw_ref, wout_ref, sh2_ref, sc2_ref, g2_ref, wr_ref, br_ref,
                    o_ref, h_ref, mi_ref, wc_ref, cnt_ref,
                    y_s, bg_s, tail_s, x_s, carry, upper, *, tiles_per_seq, tiles_per_part):
    i = pl.program_id(0)
    n_tiles = pl.num_programs(0) - 1
    tm, d = x_ref.shape
    cur = i % 2
    prv = 1 - cur

    @pl.when(i == 0)
    def _():
        _route_init(carry, upper)

    tail_s[...] = y_s[cur, tm - 8:tm, :]

    @pl.when(i < n_tiles)
    def _():
        for r0 in range(0, tm, ODD_SUB):
            rows = slice(r0, r0 + ODD_SUB)
            ew = ew_ref[rows, :]
            moe = ew[:, 0:1] * _unpack_rows(e0_ref[rows, :]) + ew[:, 1:2] * _unpack_rows(e1_ref[rows, :])
            x = x_ref[rows, :] + egate_ref[...] * moe
            x_s[cur, rows, :] = x
            h = _rms_mod(x, g_ref[...], sh_ref[...], sc_ref[...])
            p = jnp.dot(h.astype(BF16), win_ref[...], preferred_element_type=F32)
            bg_s[cur, rows, :] = p[:, 0:d]
            y_s[cur, rows, :] = p[:, d:2 * d] * p[:, 2 * d:3 * d]

    @pl.when(i >= 1)
    def _():
        t_prev = i - 1
        first = (t_prev % tiles_per_seq) == 0
        last = (t_prev % tiles_per_seq) == tiles_per_seq - 1
        y = y_s[prv]
        left = jnp.where(first, 0.0, tail_s[7:8, :])
        right = jnp.where(last, 0.0, y_s[cur, 0:1, :])
        ridx = lax.broadcasted_iota(jnp.int32, (tm, d), 0)
        y_dn = jnp.where(ridx == 0, left, pltpu.roll(y, 1, 0))
        y_up = jnp.where(ridx == tm - 1, right, pltpu.roll(y, tm - 1, 0))
        conv = y_dn * cw_ref[0:1, :] + y * cw_ref[1:2, :] + y_up * cw_ref[2:3, :]
        mix = jnp.dot((bg_s[prv] * conv).astype(BF16), wout_ref[...], preferred_element_type=F32)
        lat = x_s[prv] + gate_ref[...] * mix
        o_ref[...] = lat

        @pl.when(t_prev % tiles_per_part == 0)
        def _():
            carry[...] = jnp.zeros_like(carry)

        _route_tile(lat, sh2_ref[...], sc2_ref[...], g2_ref[...], wr_ref, br_ref, carry, upper,
                    h_ref, mi_ref, wc_ref, cnt_ref)


def _odd_mix(lat, moe_rows, mods, layer, g, win_bf, conv_w8, wout_bf, seq, g2, wr_t, br_t, batch0):
    yg, ewcol = moe_rows
    n, d = lat.shape
    tm = ODD_TM
    nt = n // tm
    tiles_per_seq = seq // tm
    cur = lambda i: jnp.minimum(i, nt - 1)
    prv = lambda i: jnp.maximum(i - 1, 0)
    row = lambda t: layer * MOD_ROWS + batch0 + t // tiles_per_seq
    const = lambda shape: pl.BlockSpec(shape, lambda i: (0,) * len(shape), pipeline_mode=pl.Buffered(1))
    r_in, r_out, r_shapes, r_scratch = _route_specs(n, d, tm, prv, row, tm)
    return pl.pallas_call(
        functools.partial(_odd_mix_kernel, tiles_per_seq=tiles_per_seq, tiles_per_part=nt),
        grid=(nt + 1,),
        in_specs=[
            pl.BlockSpec((tm, d), lambda i: (cur(i), 0)),
            pl.BlockSpec((tm, d // 2), lambda i: (cur(i), 0)),
            pl.BlockSpec((tm, d // 2), lambda i: (nt + cur(i), 0)),
            pl.BlockSpec((tm, LANES), lambda i: (cur(i), 0)),
            pl.BlockSpec((None, 1, d), lambda i: (row(cur(i)) - MOD_ROWS, 0, 5)),
            pl.BlockSpec((None, 1, d), lambda i: (row(cur(i)), 0, 0)),
            pl.BlockSpec((None, 1, d), lambda i: (row(cur(i)), 0, 1)),
            pl.BlockSpec((None, 1, d), lambda i: (row(prv(i)), 0, 2)),
            const((1, d)), const(win_bf.shape), const(conv_w8.shape), const(wout_bf.shape),
        ] + r_in,
        out_specs=[pl.BlockSpec((tm, d), lambda i: (prv(i), 0))] + r_out,
        out_shape=[jax.ShapeDtypeStruct((n, d), F32)] + r_shapes,
        scratch_shapes=[
            pltpu.VMEM((2, tm, d), F32),
            pltpu.VMEM((2, tm, d), F32),
            pltpu.VMEM((8, d), F32),
            pltpu.VMEM((2, tm, d), F32),
        ] + r_scratch,
        compiler_params=_params(("arbitrary",)),
        name="odd_mix",
    )(lat, yg, yg, ewcol, mods, mods, mods, mods, g, win_bf, conv_w8, wout_bf, mods, mods, g2, wr_t, br_t)


def _route_init(carry, upper):
    tm = upper.shape[0]
    carry[...] = jnp.zeros_like(carry)
    r_i = lax.broadcasted_iota(jnp.int32, (tm, tm), 0)
    c_i = lax.broadcasted_iota(jnp.int32, (tm, tm), 1)
    upper[...] = jnp.where(r_i < c_i, 1.0, 0.0).astype(BF16)


def _route_tile(lat, sh, sc, g, wr_ref, br_ref, carry, upper, h_ref, mi_ref, wc_ref, cnt_ref):
    tm = lat.shape[0]
    epg = EXPERTS_PER_GROUP
    h = _rms_mod(lat, g, sh, sc)
    h_hi = h.astype(BF16)
    h_hi_f = h_hi.astype(F32)
    h_ref[...] = _pack_rounded(h_hi_f)
    h_lo = (h - h_hi_f).astype(BF16)
    w = wr_ref[...]
    w1 = w.astype(BF16).astype(F32)
    r1 = w - w1
    w2 = r1.astype(BF16).astype(F32)
    w3 = r1 - w2
    nt = (((1,), (1,)), ((), ()))
    nr = w.shape[0]
    w123 = jnp.concatenate([w1, w2, w3, jnp.zeros((8, w.shape[1]), F32)], axis=0).astype(BF16)
    w12 = jnp.concatenate([w1, w2], axis=0).astype(BF16)
    p_hi = lax.dot_general(w123, h_hi, nt, preferred_element_type=F32)
    p_lo = lax.dot_general(w12, h_lo, nt, preferred_element_type=F32)
    lg = ((p_hi[2 * nr:3 * nr] + p_lo[nr:2 * nr]) + (p_hi[nr:2 * nr] + p_lo[0:nr])) + p_hi[0:nr] + br_ref[...]
    io8 = lax.broadcasted_iota(jnp.int32, (epg, tm), 0)
    gl = lg[0:epg]
    gmax = jnp.max(gl, axis=0, keepdims=True)
    g_idx = jnp.min(jnp.where(gl == gmax, io8, epg), axis=0, keepdims=True)
    g_w = 1.0 / jnp.sum(jnp.exp(gl - gmax), axis=0, keepdims=True)
    e_sel = lg[epg:2 * epg]
    for gi in range(1, N_GROUPS):
        e_sel = jnp.where(g_idx == gi, lg[(gi + 1) * epg:(gi + 2) * epg], e_sel)
    v0 = jnp.max(e_sel, axis=0, keepdims=True)
    i0 = jnp.min(jnp.where(e_sel == v0, io8, epg), axis=0, keepdims=True)
    rest = jnp.where(io8 == i0, -jnp.inf, e_sel)
    v1 = jnp.max(rest, axis=0, keepdims=True)
    i1 = jnp.min(jnp.where(rest == v1, io8, epg), axis=0, keepdims=True)
    t = jnp.exp(v1 - v0)
    w0 = g_w / (1.0 + t)
    w1 = g_w * t / (1.0 + t)
    e0 = g_idx * epg + i0
    e1 = g_idx * epg + i1

    io32 = lax.broadcasted_iota(jnp.int32, (N_EXPERTS, tm), 0)
    hit0 = io32 == e0
    hit1 = io32 == e1
    onehot = jnp.where(hit0 | hit1, 1.0, 0.0)
    cum = jnp.dot(onehot.astype(BF16), upper[...], preferred_element_type=F32) + carry[...]
    rank0 = jnp.sum(jnp.where(hit0, cum, 0.0), axis=0, keepdims=True).astype(jnp.int32)
    rank1 = jnp.sum(jnp.where(hit1, cum, 0.0), axis=0, keepdims=True).astype(jnp.int32)
    carry[...] = carry[...] + jnp.sum(onehot, axis=1, keepdims=True)
    cnt_ref[...] = jnp.broadcast_to(carry[...], cnt_ref.shape)

    mi_ref[...] = jnp.where(io8 == 0, e0, jnp.where(io8 == 1, e1, jnp.where(io8 == 2, rank0,
                            jnp.where(io8 == 3, rank1, 0))))
    io128 = lax.broadcasted_iota(jnp.int32, (LANES, tm), 0)
    wrow = jnp.where(io128 == 0, w0, jnp.where(io128 == 1, w1, 0.0))
    wc_ref[...] = wrow.T


def _route_specs(n, d, tm, tile_of, mods_row, route_rows):
    in_specs = [
        pl.BlockSpec((None, 1, d), lambda i: (mods_row(tile_of(i)), 0, 3)),
        pl.BlockSpec((None, 1, d), lambda i: (mods_row(tile_of(i)), 0, 4)),
        pl.BlockSpec((1, d), lambda i: (0, 0), pipeline_mode=pl.Buffered(1)),
        pl.BlockSpec((ROUTER_ROWS, d), lambda i: (0, 0), pipeline_mode=pl.Buffered(1)),
        pl.BlockSpec((ROUTER_ROWS, 1), lambda i: (0, 0), pipeline_mode=pl.Buffered(1)),
    ]
    out_specs = [
        pl.BlockSpec((tm, d // 2), lambda i: (tile_of(i), 0)),
        pl.BlockSpec((8, tm), lambda i: (0, tile_of(i))),
        pl.BlockSpec((tm, LANES), lambda i: (tile_of(i), 0)),
        pl.BlockSpec((N_EXPERTS, LANES), lambda i: (0, 0)),
    ]
    out_shapes = [
        jax.ShapeDtypeStruct((n, d // 2), jnp.uint32),
        jax.ShapeDtypeStruct((8, n), jnp.int32),
        jax.ShapeDtypeStruct((n, LANES), F32),
        jax.ShapeDtypeStruct((N_EXPERTS, LANES), F32),
    ]
    scratch = [pltpu.VMEM((N_EXPERTS, 1), F32), pltpu.VMEM((route_rows, route_rows), BF16)]
    return in_specs, out_specs, out_shapes, scratch


def _plan_kernel(cnt_ref, mi_ref, dest_ref, be_ref, runs_ref, nv_ref, nu_ref, ps_ref, *, n_blocks):
    bm = MOE_BM

    def per_expert(e, carry):
        blk0, n_runs = carry
        cnt = cnt_ref[e]
        nb = (cnt + bm - 1) // bm
        ps_ref[e] = blk0 * bm

        def fill(b, c):
            be_ref[b] = e
            nv_ref[b] = jnp.minimum(cnt - (b - blk0) * bm, bm)
            return c

        lax.fori_loop(blk0, blk0 + nb, fill, 0)

        @pl.when(nb > 0)
        def _():
            runs_ref[n_runs] = e

        return blk0 + nb, n_runs + jnp.where(nb > 0, 1, 0)

    n_used, n_runs = lax.fori_loop(0, N_EXPERTS, per_expert, (0, 0))
    nu_ref[0] = n_used
    nu_ref[1] = n_runs
    last_e = be_ref[jnp.maximum(n_used - 1, 0)]

    def fill_tail(b, c):
        be_ref[b] = last_e
        nv_ref[b] = 0
        return c

    lax.fori_loop(n_used, n_blocks, fill_tail, 0)

    def fill_runs(k, c):
        runs_ref[k] = last_e
        return c

    lax.fori_loop(n_runs, N_EXPERTS, fill_runs, 0)

    e01 = mi_ref[0:2, :]
    dest = mi_ref[2:4, :]
    for e in range(N_EXPERTS):
        dest = dest + jnp.where(e01 == e, ps_ref[e], 0)
    dest_ref[...] = dest


def _plan(counts, meta_i, n_blocks):
    n = meta_i.shape[1]
    return pl.pallas_call(
        functools.partial(_plan_kernel, n_blocks=n_blocks),
        in_specs=[pl.BlockSpec(memory_space=pltpu.SMEM), pl.BlockSpec(memory_space=pltpu.VMEM)],
        out_specs=[pl.BlockSpec(memory_space=pltpu.VMEM)] + [pl.BlockSpec(memory_space=pltpu.SMEM)] * 4,
        out_shape=[
            jax.ShapeDtypeStruct((2, n), jnp.int32),
            jax.ShapeDtypeStruct((n_blocks,), jnp.int32),
            jax.ShapeDtypeStruct((N_EXPERTS,), jnp.int32),
            jax.ShapeDtypeStruct((n_blocks,), jnp.int32),
            jax.ShapeDtypeStruct((2,), jnp.int32),
        ],
        scratch_shapes=[pltpu.SMEM((N_EXPERTS,), jnp.int32)],
        compiler_params=pltpu.CompilerParams(vmem_limit_bytes=VMEM_LIMIT),
        name="plan",
    )(counts, meta_i)


def _sc_mesh():
    return plsc.VectorSubcoreMesh(core_axis_name="c", subcore_axis_name="s",
                                  num_cores=SC_CORES, num_subcores=SC_SUBCORES)


def _sc_worker():
    return lax.axis_index("s") * SC_CORES + lax.axis_index("c")


def _sc_dispatch(h2, dest, n_rows):
    n, d = h2.shape
    c = SC_CHUNK
    per_w = n // SC_WORKERS
    nchunk = per_w // c
    idx = dest.reshape(2, SC_WORKERS, nchunk, c)

    @functools.partial(
        pl.kernel, mesh=_sc_mesh(), out_type=jax.ShapeDtypeStruct((n_rows, d), h2.dtype),
        scratch_types=[pltpu.VMEM((nchunk, c), jnp.int32), pltpu.VMEM((nchunk, c), jnp.int32),
                       pltpu.VMEM((2, c, d), h2.dtype),
                       pltpu.SemaphoreType.DMA((2,)), pltpu.SemaphoreType.DMA((2,))])
    def k(h_hbm, idx_hbm, xb_hbm, idx0_v, idx1_v, rows_v, gsem, ssem):
        wid = _sc_worker()
        base = wid * per_w
        idx_v = (idx0_v, idx1_v)
        for kk in range(2):
            pltpu.sync_copy(idx_hbm.at[kk, wid], idx_v[kk])

        def get(j, slot):
            return pltpu.make_async_copy(h_hbm.at[pl.ds(base + j * c, c)], rows_v.at[slot], gsem.at[slot])

        def put(j, slot, kk):
            return pltpu.make_async_copy(rows_v.at[slot], xb_hbm.at[idx_v[kk].at[j]], ssem.at[slot])

        get(0, 0).start()

        @pl.loop(0, nchunk, step=2)
        def _(j):
            for slot in range(2):
                jj = j + slot
                get(jj, slot).wait()

                @pl.when(jj >= 1)
                def _():
                    for kk in range(2):
                        put(jj - 1, 1 - slot, kk).wait()

                @pl.when(jj + 1 < nchunk)
                def _():
                    get(jj + 1, 1 - slot).start()

                for kk in range(2):
                    put(jj, slot, kk).start()

        for kk in range(2):
            put(nchunk - 1, (nchunk - 1) % 2, kk).wait()

    return k(h2, idx)


def _sc_gather(y, dest):
    d = y.shape[1]
    total = dest.shape[0] * dest.shape[1]
    c = SC_CHUNK
    per_w = total // SC_WORKERS
    nchunk = per_w // c
    idx = dest.reshape(SC_WORKERS, nchunk, c)

    @functools.partial(
        pl.kernel, mesh=_sc_mesh(), out_type=jax.ShapeDtypeStruct((total, d), y.dtype),
        scratch_types=[pltpu.VMEM((nchunk, c), jnp.int32), pltpu.VMEM((2, c, d), y.dtype),
                       pltpu.SemaphoreType.DMA((2,)), pltpu.SemaphoreType.DMA((2,))])
    def k(y_hbm, idx_hbm, out_hbm, idx_v, rows_v, gsem, ssem):
        wid = _sc_worker()
        base = wid * per_w
        pltpu.sync_copy(idx_hbm.at[wid], idx_v)

        def get(j, slot):
            return pltpu.make_async_copy(y_hbm.at[idx_v.at[j]], rows_v.at[slot], gsem.at[slot])

        def put(j, slot):
            return pltpu.make_async_copy(rows_v.at[slot], out_hbm.at[pl.ds(base + j * c, c)], ssem.at[slot])

        get(0, 0).start()

        @pl.loop(0, nchunk, step=2)
        def _(j):
            for slot in range(2):
                jj = j + slot
                get(jj, slot).wait()

                @pl.when(jj >= 1)
                def _():
                    put(jj - 1, 1 - slot).wait()

                @pl.when(jj + 1 < nchunk)
                def _():
                    get(jj + 1, 1 - slot).start()

                put(jj, slot).start()

        put(nchunk - 1, (nchunk - 1) % 2).wait()

    return k(y, idx)


def _expert_kernel(be_ref, runs_ref, nv_ref, nu_ref, x_ref, wg_hbm, wu_hbm, wd_hbm, y_ref,
                   wgu_s, wd_s, stg_g, stg_u, stg_d, run_s, sems, *, layer):
    b = pl.program_id(0)
    hid = stg_g.shape[2]
    e = be_ref[b]
    n_runs = nu_ref[1]
    changed = jnp.logical_or(b == 0, e != be_ref[jnp.maximum(b - 1, 0)])

    def fetch(run):
        expert = runs_ref[run]
        slot = run % WEIGHT_SLOTS
        return (pltpu.make_async_copy(wg_hbm.at[layer, expert], stg_g.at[slot], sems.at[slot]),
                pltpu.make_async_copy(wu_hbm.at[layer, expert], stg_u.at[slot], sems.at[slot]),
                pltpu.make_async_copy(wd_hbm.at[layer, expert], stg_d.at[slot], sems.at[slot]))

    @pl.when(b == 0)
    def _():
        for r in range(WEIGHT_SLOTS - 1):
            @pl.when(r < n_runs)
            def _():
                for cp in fetch(r):
                    cp.start()

    @pl.when(changed)
    def _():
        run = jnp.where(b == 0, 0, run_s[0] + 1)
        run_s[0] = run
        for cp in fetch(run):
            cp.wait()

        ahead = run + WEIGHT_SLOTS - 1

        @pl.when(ahead < n_runs)
        def _():
            for cp in fetch(ahead):
                cp.start()

        slot = run % WEIGHT_SLOTS
        wgu_s[:, 0:hid] = stg_g[slot].astype(BF16)
        wgu_s[:, hid:2 * hid] = stg_u[slot].astype(BF16)
        wd_s[...] = stg_d[slot].astype(BF16)

    bm, dp = x_ref.shape
    nv = nv_ref[b]
    in_use = b < nu_ref[0]

    def run(rows):
        live = lax.broadcasted_iota(jnp.int32, (rows, dp), 0) < nv
        x = _unpack_rows(jnp.where(live, x_ref[0:rows, :], jnp.uint32(0)))
        gu = jnp.dot(x.astype(BF16), wgu_s[...], preferred_element_type=F32)
        gate = gu[:, 0:hid]
        act = gate * (1.0 / (1.0 + jnp.exp(-gate))) * gu[:, hid:2 * hid]
        y_ref[0:rows, :] = _pack_rows(jnp.dot(act.astype(BF16), wd_s[...], preferred_element_type=F32))

    n_quanta = bm // MOE_QUANTUM
    for q in range(1, n_quanta + 1):
        rows = q * MOE_QUANTUM

        @pl.when(jnp.logical_and(in_use, jnp.logical_and(nv > rows - MOE_QUANTUM, nv <= rows)))
        def _(rows=rows):
            run(rows)
            if rows < bm:
                y_ref[rows:bm, :] = jnp.zeros((bm - rows, dp), y_ref.dtype)

    @pl.when(jnp.logical_not(in_use))
    def _():
        y_ref[...] = jnp.zeros_like(y_ref)


def _experts(block_e, runs, n_valid, n_used, xb, w_gate, w_up, w_down, layer):
    n_rows, dp = xb.shape
    d, hid = w_gate.shape[2], w_gate.shape[3]
    bm = MOE_BM
    n_blocks = n_rows // bm
    hbm = pl.BlockSpec(memory_space=pl.ANY)
    return pl.pallas_call(
        functools.partial(_expert_kernel, layer=layer),
        grid_spec=pltpu.PrefetchScalarGridSpec(
            num_scalar_prefetch=4,
            grid=(n_blocks,),
            in_specs=[
                pl.BlockSpec((bm, dp), lambda b, be, nx, nv, nu: (jnp.minimum(b, nu[0] - 1), 0)),
                hbm, hbm, hbm,
            ],
            out_specs=pl.BlockSpec((bm, dp), lambda b, be, nx, nv, nu: (b, 0)),
            scratch_shapes=[
                pltpu.VMEM((d, 2 * hid), BF16), pltpu.VMEM((hid, d), BF16),
                pltpu.VMEM((WEIGHT_SLOTS, d, hid), F32), pltpu.VMEM((WEIGHT_SLOTS, d, hid), F32),
                pltpu.VMEM((WEIGHT_SLOTS, hid, d), F32),
                pltpu.SMEM((1,), jnp.int32), pltpu.SemaphoreType.DMA((WEIGHT_SLOTS,)),
            ],
        ),
        out_shape=jax.ShapeDtypeStruct((n_rows, dp), jnp.uint32),
        compiler_params=_params(("arbitrary",)),
        name="experts",
    )(block_e, runs, n_valid, n_used, xb, w_gate, w_up, w_down)


def _combine_kernel(lat_ref, y0_ref, y1_ref, wc_ref, gate_ref, gf_ref, *rest, final):
    o_ref = rest[-1]
    wc = wc_ref[...]
    moe = wc[:, 0:1] * _unpack_rows(y0_ref[...]) + wc[:, 1:2] * _unpack_rows(y1_ref[...])
    out = lat_ref[...] + gate_ref[...] * moe
    if final:
        ms = jnp.mean(out * out, axis=-1, keepdims=True)
        out = out * lax.rsqrt(ms + EPS) * gf_ref[...]
    o_ref[...] = out


def _combine(lat, yg, wcol, mods, layer, batch0, g_final, seq, final, out_rows, tok0, prev_out):
    n, d = lat.shape
    tm = COMBINE_TM
    nt = n // tm
    t0 = tok0 // tm
    tiles_per_seq = seq // tm
    row = lambda i: layer * MOD_ROWS + batch0 + i // tiles_per_seq
    in_specs = [
        pl.BlockSpec((tm, d), lambda i: (i, 0)),
        pl.BlockSpec((tm, d // 2), lambda i: (i, 0)),
        pl.BlockSpec((tm, d // 2), lambda i: (nt + i, 0)),
        pl.BlockSpec((tm, LANES), lambda i: (i, 0)),
        pl.BlockSpec((None, 1, d), lambda i: (row(i), 0, 5)),
        pl.BlockSpec((1, d), lambda i: (0, 0)),
    ]
    args = [lat, yg, yg, wcol, mods, g_final]
    aliases = {}
    if prev_out is not None:
        in_specs.append(pl.BlockSpec(memory_space=pl.ANY))
        args.append(prev_out)
        aliases = {len(args) - 1: 0}
    return pl.pallas_call(
        functools.partial(_combine_kernel, final=final),
        grid=(nt,),
        in_specs=in_specs,
        out_specs=pl.BlockSpec((tm, d), lambda i: (t0 + i, 0)),
        out_shape=jax.ShapeDtypeStruct((out_rows, d), F32),
        input_output_aliases=aliases,
        compiler_params=_params(("parallel",)),
        name="combine",
    )(*args)


def _moe_rows(routed, layer, w_gate, w_up, w_down):
    h2, meta_i, wcol, counts = routed
    n = h2.shape[0]
    n_blocks = (2 * n) // MOE_BM + N_EXPERTS
    dest, block_e, runs, n_valid, n_used = _plan(counts[:, 0].astype(jnp.int32), meta_i, n_blocks)
    xb = _sc_dispatch(h2, dest, n_blocks * MOE_BM)
    yb = _experts(block_e, runs, n_valid, n_used, xb, w_gate, w_up, w_down, layer)
    return _sc_gather(yb, dest), wcol


def _rope_tables(seq):
    quarter = HEAD_DIM // 4
    pos = jnp.arange(seq, dtype=F32)
    row_ids = jnp.floor(pos / GRID_W)
    col_ids = pos - row_ids * GRID_W
    inv = ROPE_BASE ** (-jnp.arange(quarter, dtype=F32) / quarter)
    ang_r = row_ids[:, None] * inv
    ang_c = col_ids[:, None] * inv
    zero = jnp.zeros_like(ang_r)
    cos = jnp.concatenate([jnp.cos(ang_r), jnp.cos(ang_r), jnp.cos(ang_c), jnp.cos(ang_c)], axis=-1)
    sa = jnp.concatenate([-jnp.sin(ang_r), zero, -jnp.sin(ang_c), zero], axis=-1)
    sb = jnp.concatenate([zero, jnp.sin(ang_r), zero, jnp.sin(ang_c)], axis=-1)
    rep = LANES // HEAD_DIM
    return tuple(jnp.tile(t, (1, rep)) for t in (cos, sa, sb))


def _router_weights(w_rg, b_rg, w_re, b_re):
    d = w_rg.shape[0]
    pad = EXPERTS_PER_GROUP - N_GROUPS
    wr_t = jnp.concatenate([w_rg.T, jnp.zeros((pad, d), F32), w_re.T], axis=0)
    br_t = jnp.concatenate([b_rg, jnp.full((pad,), NEG_INF, F32), b_re])[:, None]
    return wr_t, br_t


def kernel(x, c, ctx, c_ctx, w_ada, b_ada, g_norm1, g_norm2, g_final, w_in_even, attn_sink, g_sgu,
           w_spatial, b_spatial, w_out_even, w_in_odd, conv_w, w_out_odd, w_router_group,
           b_router_group, w_router_expert, b_router_expert, w_gate, w_up, w_down):
    b, s, d = x.shape
    n = b * s
    n_ctx = ctx.shape[1]
    depth = w_ada.shape[0]
    assert depth == 2 and b + 1 <= MOD_ROWS

    cond = jnp.concatenate([c, c_ctx[None, :], jnp.zeros((MOD_ROWS - b - 1, d), F32)], axis=0)
    mods = _ada(cond, w_ada, b_ada).reshape(depth * MOD_ROWS, 1, 6 * d)
    gf = g_final[None, :]

    lat = x.reshape(n, d)
    w_in_bf = w_in_even[0].astype(BF16)
    tabs = _rope_tables(s)
    qx, k, v, u, z = _even_in(lat, mods, 0, lambda i: i // (s // EVEN_TM), g_norm1[0][None, :], w_in_bf,
                              tabs, s // EVEN_TM, EVEN_TM)
    ones = jnp.ones((n_ctx, LANES), F32)
    zeros = jnp.zeros((n_ctx, LANES), F32)
    _, kc, vc, _, _ = _even_in(ctx.reshape(b * n_ctx, d), mods, 0, lambda i: b, g_norm1[0][None, :],
                               w_in_bf, (ones, zeros, zeros), 1, n_ctx)
    bsp_full = jnp.repeat(b_spatial[0].T, HEAD_DIM, axis=1)
    half = N_Q_HEADS // 2
    w_att = w_out_even[0][:Q_DIM].reshape(2, half, HEAD_DIM, d).transpose(1, 0, 2, 3).reshape(Q_DIM, d)
    w_out_bf = jnp.concatenate([w_att, w_out_even[0][Q_DIM:]], axis=0).astype(BF16)
    conv_w8 = jnp.concatenate([conv_w[0], jnp.zeros((8 - conv_w.shape[1], d), F32)], axis=0)
    w_in_odd_bf = w_in_odd[0].astype(BF16)
    w_out_odd_bf = w_out_odd[0].astype(BF16)
    wsp_bf = w_spatial[0].astype(BF16)
    wr0, br0 = _router_weights(w_router_group[0], b_router_group[0], w_router_expert[0], b_router_expert[0])
    wr1, br1 = _router_weights(w_router_group[1], b_router_group[1], w_router_expert[1], b_router_expert[1])

    part = n // MOE_PARTS
    out = None
    for p in range(MOE_PARTS):
        tok0 = p * part
        batch0 = tok0 // s
        lat_p, *routed = _even_mix(lat, qx, k, v, kc, vc, u, z, attn_sink[0], g_sgu[0][None, :],
                                   wsp_bf, bsp_full, w_out_bf, mods, s, n_ctx,
                                   g_norm2[0][None, :], wr0, br0, tok0, part)
        moe_rows = _moe_rows(routed, 0, w_gate, w_up, w_down)
        lat_p, *routed = _odd_mix(lat_p, moe_rows, mods, 1, g_norm1[1][None, :], w_in_odd_bf, conv_w8,
                                  w_out_odd_bf, s, g_norm2[1][None, :], wr1, br1, batch0)
        yg, wcol = _moe_rows(routed, 1, w_gate, w_up, w_down)
        out = _combine(lat_p, yg, wcol, mods, 1, batch0, gf, s, True, n, tok0, out)
    return out.reshape(b, s, d)
```

```python
import functools

import jax
import jax.numpy as jnp
from jax import lax
from jax.experimental import pallas as pl
from jax.experimental.pallas import tpu as pltpu
from jax.experimental.pallas import tpu_sc as plsc

F32 = jnp.float32
BF16 = jnp.bfloat16
HIGHEST = lax.Precision.HIGHEST

GRID_W = 64
N_Q_HEADS = 8
N_KV_HEADS = 2
HEAD_DIM = 64
ATT_BLOCK = 128
ROPE_BASE = 10000.0
Q_DIM = N_Q_HEADS * HEAD_DIM
KV_DIM = N_KV_HEADS * HEAD_DIM
SG_GROUPS = 8
SG_WIDTH = SG_GROUPS * HEAD_DIM
N_GROUPS = 4
EXPERTS_PER_GROUP = 8
N_EXPERTS = N_GROUPS * EXPERTS_PER_GROUP
EPS = 1e-6
NEG_INF = -1e30

LANES = 128
SC_CORES = 2
SC_SUBCORES = 16
SC_WORKERS = SC_CORES * SC_SUBCORES
SC_CHUNK = 32
MOD_ROWS = 8
ROUTER_ROWS = EXPERTS_PER_GROUP + N_EXPERTS
VMEM_LIMIT = 56 * 1024 * 1024

ADA_TN = 1536
EVEN_TM = 1024
EVEN_SUB = 512
ATT_TQ = 512
ODD_TM = 512
ODD_SUB = 256
ROUTE_SUB = 256
MOE_BM = 1024
MOE_QUANTUM = 128
MOE_PARTS = 2
WEIGHT_SLOTS = 3
COMBINE_TM = 512


def _params(sem):
    return pltpu.CompilerParams(dimension_semantics=sem, vmem_limit_bytes=VMEM_LIMIT)


def _rms_mod(x, g, shift, scale):
    ms = jnp.mean(x * x, axis=-1, keepdims=True)
    return (x * lax.rsqrt(ms + EPS)) * (g * (1.0 + scale)) + shift


def _pack_rounded(a):
    w = a.shape[1] // 2
    hi = pltpu.bitcast(a[:, :w], jnp.uint32)
    lo = pltpu.bitcast(a[:, w:], jnp.uint32)
    return hi | (lo >> 16)


def _pack_rows(a):
    return _pack_rounded(a.astype(BF16).astype(F32))


def _unpack_rows(p):
    hi = pltpu.bitcast(p & jnp.uint32(0xFFFF0000), F32)
    lo = pltpu.bitcast(p << 16, F32)
    return jnp.concatenate([hi, lo], axis=1)


def _ada_kernel(a_ref, w_ref, b_ref, o_ref):
    a = a_ref[...]
    s = a * (1.0 / (1.0 + jnp.exp(-a)))
    o_ref[0] = jnp.dot(s, w_ref[0], preferred_element_type=F32, precision=HIGHEST) + b_ref[0]


def _ada(cond, w_ada, b_ada):
    depth, d, six_d = w_ada.shape
    return pl.pallas_call(
        _ada_kernel,
        grid=(depth, six_d // ADA_TN),
        in_specs=[
            pl.BlockSpec((MOD_ROWS, d), lambda l, j: (0, 0)),
            pl.BlockSpec((1, d, ADA_TN), lambda l, j: (l, 0, j)),
            pl.BlockSpec((1, 1, ADA_TN), lambda l, j: (l, 0, j)),
        ],
        out_specs=pl.BlockSpec((1, MOD_ROWS, ADA_TN), lambda l, j: (l, 0, j)),
        out_shape=jax.ShapeDtypeStruct((depth, MOD_ROWS, six_d), F32),
        compiler_params=_params(("arbitrary", "arbitrary")),
        name="ada",
    )(cond, w_ada, b_ada.reshape(depth, 1, six_d))


def _even_in_kernel(x_ref, sh_ref, sc_ref, g_ref, w_ref, cos_ref, sa_ref, sb_ref,
                    qx_ref, k_ref, v_ref, u_ref, z_ref):
    tm = x_ref.shape[0]
    sub = min(tm, EVEN_SUB)
    scale = HEAD_DIM ** -0.5
    low = lax.broadcasted_iota(jnp.int32, (sub, LANES), 1) < HEAD_DIM
    heads_per_kv = N_Q_HEADS // N_KV_HEADS
    u0 = Q_DIM + 2 * KV_DIM

    for r0 in range(0, tm, sub):
        rows = slice(r0, r0 + sub)
        h = _rms_mod(x_ref[rows, :], g_ref[...], sh_ref[...], sc_ref[...])
        p = jnp.dot(h.astype(BF16), w_ref[...], preferred_element_type=F32)
        cos, sa, sb = cos_ref[rows, :], sa_ref[rows, :], sb_ref[rows, :]

        def rope(t):
            return t * cos + pltpu.roll(t, LANES - 16, 1) * sa + pltpu.roll(t, 16, 1) * sb

        for cblk in range(Q_DIM // LANES):
            t = rope(p[:, cblk * LANES:(cblk + 1) * LANES]) * scale
            sw = pltpu.roll(t, HEAD_DIM, 1)
            zero = jnp.zeros_like(t)
            if (2 * cblk) // heads_per_kv == 0:
                first, second = jnp.where(low, t, zero), jnp.where(low, sw, zero)
            else:
                first, second = jnp.where(low, zero, sw), jnp.where(low, zero, t)
            qx_ref[rows, (2 * cblk) * LANES:(2 * cblk + 1) * LANES] = first.astype(BF16)
            qx_ref[rows, (2 * cblk + 1) * LANES:(2 * cblk + 2) * LANES] = second.astype(BF16)

        k_ref[rows, :] = rope(p[:, Q_DIM:Q_DIM + KV_DIM]).astype(BF16)
        v_ref[rows, :] = p[:, Q_DIM + KV_DIM:Q_DIM + 2 * KV_DIM].astype(BF16)
        u_ref[rows, :] = p[:, u0:u0 + SG_WIDTH]
        z_ref[rows, :] = p[:, u0 + SG_WIDTH:u0 + 2 * SG_WIDTH]


def _even_in(x2d, mods, layer, mod_row_fn, g, w_bf, tabs, tab_blocks, tm):
    n, d = x2d.shape
    ein = w_bf.shape[1]
    cos, sa, sb = tabs
    row = lambda i: layer * MOD_ROWS + mod_row_fn(i)
    tab_spec = pl.BlockSpec((tm, LANES), lambda i: (i % tab_blocks, 0))
    qx_dim = N_Q_HEADS * LANES
    return pl.pallas_call(
        _even_in_kernel,
        grid=(n // tm,),
        in_specs=[
            pl.BlockSpec((tm, d), lambda i: (i, 0)),
            pl.BlockSpec((None, 1, d), lambda i: (row(i), 0, 0)),
            pl.BlockSpec((None, 1, d), lambda i: (row(i), 0, 1)),
            pl.BlockSpec((1, d), lambda i: (0, 0)),
            pl.BlockSpec((d, ein), lambda i: (0, 0)),
            tab_spec, tab_spec, tab_spec,
        ],
        out_specs=[
            pl.BlockSpec((tm, qx_dim), lambda i: (i, 0)),
            pl.BlockSpec((tm, KV_DIM), lambda i: (i, 0)),
            pl.BlockSpec((tm, KV_DIM), lambda i: (i, 0)),
            pl.BlockSpec((tm, SG_WIDTH), lambda i: (i, 0)),
            pl.BlockSpec((tm, SG_WIDTH), lambda i: (i, 0)),
        ],
        out_shape=[
            jax.ShapeDtypeStruct((n, qx_dim), BF16),
            jax.ShapeDtypeStruct((n, KV_DIM), BF16),
            jax.ShapeDtypeStruct((n, KV_DIM), BF16),
            jax.ShapeDtypeStruct((n, SG_WIDTH), F32),
            jax.ShapeDtypeStruct((n, SG_WIDTH), F32),
        ],
        compiler_params=_params(("parallel",)),
        name="even_in",
    )(x2d, mods, mods, g, w_bf, cos, sa, sb)


def _gelu(x):
    return 0.5 * x * (1.0 + lax.erf(x * (2.0 ** -0.5)))


def _even_mix_kernel(sink_ref, lat_ref, qx_ref, km_ref, kp_ref, kn_ref, vm_ref, vp_ref, vn_ref,
                     kc_ref, vc_ref, u_ref, z_ref, gsgu_ref, wsp_ref, bsp_ref, wout_ref, gate_ref,
                     sh2_ref, sc2_ref, g2_ref, wr_ref, br_ref,
                     o_ref, h_ref, mi_ref, wc_ref, cnt_ref,
                     kband, vband, mixin, carry, upper, *, tiles_per_seq):
    i = pl.program_id(0)
    tq = qx_ref.shape[0]

    @pl.when(i == 0)
    def _():
        _route_init(carry, upper)

    blk = ATT_BLOCK
    nsub = tq // blk
    n_ctx = kc_ref.shape[0]
    first = (i % tiles_per_seq) == 0
    last = (i % tiles_per_seq) == tiles_per_seq - 1

    kband[0:blk] = kp_ref[...]
    kband[blk:blk + tq] = km_ref[...]
    kband[blk + tq:] = kn_ref[...]
    vband[0:blk] = vp_ref[...]
    vband[blk:blk + tq] = vm_ref[...]
    vband[blk + tq:] = vn_ref[...]

    rows = N_Q_HEADS * blk
    tok = lax.broadcasted_iota(jnp.int32, (rows, blk), 0) & (blk - 1)
    col = lax.broadcasted_iota(jnp.int32, (rows, blk), 1)
    tri_prev = col >= tok
    tri_next = col <= tok
    head = lax.broadcasted_iota(jnp.int32, (rows, 1), 0) // blk
    sink_col = jnp.zeros((rows, 1), F32)
    for hd in range(N_Q_HEADS):
        sink_col = jnp.where(head == hd, sink_ref[hd], sink_col)
    lane_low = lax.broadcasted_iota(jnp.int32, (blk, LANES), 1) < HEAD_DIM
    ones = jnp.ones((n_ctx + 3 * blk, LANES), BF16)
    nt = (((1,), (1,)), ((), ()))

    def sub_block(j, c):
        r0 = pl.multiple_of(j * blk, blk)
        ok_prev = jnp.logical_not(jnp.logical_and(first, j == 0))
        ok_next = jnp.logical_not(jnp.logical_and(last, j == nsub - 1))
        qs = jnp.concatenate([qx_ref[pl.ds(r0, blk), hd * LANES:(hd + 1) * LANES]
                              for hd in range(N_Q_HEADS)], axis=0)
        kall = jnp.concatenate([kc_ref[...], kband[pl.ds(r0, 3 * blk), :]], axis=0)
        vall = jnp.concatenate([vc_ref[...], vband[pl.ds(r0, 3 * blk), :]], axis=0)
        s = lax.dot_general(qs, kall, nt, preferred_element_type=F32)
        c0 = n_ctx
        s = jnp.concatenate([
            s[:, :c0],
            jnp.where(jnp.logical_and(tri_prev, ok_prev), s[:, c0:c0 + blk], NEG_INF),
            s[:, c0 + blk:c0 + 2 * blk],
            jnp.where(jnp.logical_and(tri_next, ok_next), s[:, c0 + 2 * blk:], NEG_INF),
        ], axis=1)
        m = jnp.maximum(jnp.max(s, axis=-1, keepdims=True), sink_col)
        p = jnp.exp(s - m).astype(BF16)
        o = jnp.dot(p, jnp.concatenate([vall, ones], axis=1), preferred_element_type=F32)
        att = o[:, :LANES] / (o[:, LANES:] + jnp.exp(sink_col - m))
        half = N_Q_HEADS // 2
        for hd in range(half):
            pair = jnp.where(lane_low, att[hd * blk:(hd + 1) * blk], att[(hd + half) * blk:(hd + half + 1) * blk])
            mixin[pl.ds(r0, blk), hd * LANES:(hd + 1) * LANES] = pair.astype(BF16)

        ug = _gelu(u_ref[pl.ds(r0, blk), :])
        zg = _gelu(z_ref[pl.ds(r0, blk), :])
        mu = jnp.mean(zg, axis=-1, keepdims=True)
        zc = zg - mu
        zn = zc * lax.rsqrt(jnp.mean(zc * zc, axis=-1, keepdims=True) + EPS) * gsgu_ref[...]
        for pair in range(SG_GROUPS // 2):
            zp = zn[:, pair * LANES:(pair + 1) * LANES]
            zero = jnp.zeros_like(zp)
            lo = jnp.where(lane_low, zp, zero).astype(BF16)
            hi = jnp.where(lane_low, zero, zp).astype(BF16)
            sg = (jnp.dot(wsp_ref[2 * pair], lo, preferred_element_type=F32)
                  + jnp.dot(wsp_ref[2 * pair + 1], hi, preferred_element_type=F32)
                  + bsp_ref[:, pair * LANES:(pair + 1) * LANES])
            mixin[pl.ds(r0, blk), Q_DIM + pair * LANES:Q_DIM + (pair + 1) * LANES] = (
                ug[:, pair * LANES:(pair + 1) * LANES] * sg).astype(BF16)
        return c

    lax.fori_loop(0, nsub, sub_block, 0, unroll=True)
    for r0 in range(0, tq, ROUTE_SUB):
        rows = slice(r0, r0 + ROUTE_SUB)
        mix = jnp.dot(mixin[rows, :], wout_ref[...], preferred_element_type=F32)
        lat = lat_ref[rows, :] + gate_ref[...] * mix
        o_ref[rows, :] = lat
        _route_tile(lat, sh2_ref[...], sc2_ref[...], g2_ref[...], wr_ref, br_ref, carry, upper,
                    h_ref.at[rows, :], mi_ref.at[:, rows], wc_ref.at[rows, :], cnt_ref)


def _even_mix(lat, qx, k, v, kc, vc, u, z, sink, g_sgu, wsp_bf, bsp_full, wout_bf, mods, seq, ctx_len,
              g2, wr_t, br_t, tok0, n):
    d = lat.shape[1]
    tq = ATT_TQ
    tiles_per_seq = seq // tq
    sub = tq // ATT_BLOCK
    nblk = lat.shape[0] // ATT_BLOCK
    t0 = tok0 // tq
    main = lambda w: pl.BlockSpec((tq, w), lambda i: (t0 + i, 0))
    prev = pl.BlockSpec((ATT_BLOCK, KV_DIM), lambda i: (jnp.maximum((t0 + i) * sub - 1, 0), 0))
    nxt = pl.BlockSpec((ATT_BLOCK, KV_DIM), lambda i: (jnp.minimum((t0 + i + 1) * sub, nblk - 1), 0))
    batch = lambda t: (t0 + t) // tiles_per_seq
    ctxs = pl.BlockSpec((ctx_len, KV_DIM), lambda i: (batch(i), 0))
    const = lambda shape: pl.BlockSpec(shape, lambda i: (0,) * len(shape), pipeline_mode=pl.Buffered(1))
    r_in, r_out, r_shapes, r_scratch = _route_specs(n, d, tq, lambda i: i, batch, ROUTE_SUB)
    return pl.pallas_call(
        functools.partial(_even_mix_kernel, tiles_per_seq=tiles_per_seq),
        grid=(n // tq,),
        in_specs=[
            pl.BlockSpec(memory_space=pltpu.SMEM),
            main(d), main(qx.shape[1]),
            main(KV_DIM), prev, nxt,
            main(KV_DIM), prev, nxt,
            ctxs, ctxs,
            main(SG_WIDTH), main(SG_WIDTH),
            const((1, SG_WIDTH)), const(wsp_bf.shape), const(bsp_full.shape), const(wout_bf.shape),
            pl.BlockSpec((None, 1, d), lambda i: (batch(i), 0, 2)),
        ] + r_in,
        out_specs=[pl.BlockSpec((tq, d), lambda i: (i, 0))] + r_out,
        out_shape=[jax.ShapeDtypeStruct((n, d), F32)] + r_shapes,
        scratch_shapes=[
            pltpu.VMEM((tq + 2 * ATT_BLOCK, KV_DIM), BF16),
            pltpu.VMEM((tq + 2 * ATT_BLOCK, KV_DIM), BF16),
            pltpu.VMEM((tq, Q_DIM + SG_WIDTH), BF16),
        ] + r_scratch,
        compiler_params=_params(("arbitrary",)),
        name="even_mix",
    )(sink, lat, qx, k, k, k, v, v, v, kc, vc, u, z, g_sgu, wsp_bf, bsp_full, wout_bf, mods,
      mods, mods, g2, wr_t, br_t)


def _odd_mix_kernel(x_ref, e0_ref, e1_ref, ew_ref, egate_ref, sh_ref, sc_ref, gate_ref, g_ref,
                    win_ref, cw_ref, wout_ref, sh2_ref, sc2_ref, g2_ref, wr_ref, br_ref,
                    o_ref, h_ref, mi_ref, wc_ref, cnt_ref,
                    y_s, bg_s, tail_s, x_s, carry, upper, *, tiles_per_seq):
    i = pl.program_id(0)
    n_tiles = pl.num_programs(0) - 1
    tm, d = x_ref.shape
    cur = i % 2
    prv = 1 - cur

    @pl.when(i == 0)
    def _():
        _route_init(carry, upper)

    tail_s[...] = y_s[cur, tm - 8:tm, :]

    @pl.when(i < n_tiles)
    def _():
        for r0 in range(0, tm, ODD_SUB):
            rows = slice(r0, r0 + ODD_SUB)
            ew = ew_ref[rows, :]
            moe = ew[:, 0:1] * _unpack_rows(e0_ref[rows, :]) + ew[:, 1:2] * _unpack_rows(e1_ref[rows, :])
            x = x_ref[rows, :] + egate_ref[...] * moe
            x_s[cur, rows, :] = x
            h = _rms_mod(x, g_ref[...], sh_ref[...], sc_ref[...])
            p = jnp.dot(h.astype(BF16), win_ref[...], preferred_element_type=F32)
            bg_s[cur, rows, :] = p[:, 0:d]
            y_s[cur, rows, :] = p[:, d:2 * d] * p[:, 2 * d:3 * d]

    @pl.when(i >= 1)
    def _():
        t_prev = i - 1
        first = (t_prev % tiles_per_seq) == 0
        last = (t_prev % tiles_per_seq) == tiles_per_seq - 1
        y = y_s[prv]
        left = jnp.where(first, 0.0, tail_s[7:8, :])
        right = jnp.where(last, 0.0, y_s[cur, 0:1, :])
        ridx = lax.broadcasted_iota(jnp.int32, (tm, d), 0)
        y_dn = jnp.where(ridx == 0, left, pltpu.roll(y, 1, 0))
        y_up = jnp.where(ridx == tm - 1, right, pltpu.roll(y, tm - 1, 0))
        conv = y_dn * cw_ref[0:1, :] + y * cw_ref[1:2, :] + y_up * cw_ref[2:3, :]
        mix = jnp.dot((bg_s[prv] * conv).astype(BF16), wout_ref[...], preferred_element_type=F32)
        lat = x_s[prv] + gate_ref[...] * mix
        o_ref[...] = lat
        _route_tile(lat, sh2_ref[...], sc2_ref[...], g2_ref[...], wr_ref, br_ref, carry, upper,
                    h_ref, mi_ref, wc_ref, cnt_ref)


def _odd_mix(lat, moe_rows, mods, layer, g, win_bf, conv_w8, wout_bf, seq, g2, wr_t, br_t, batch0):
    yg, ewcol = moe_rows
    n, d = lat.shape
    tm = ODD_TM
    nt = n // tm
    tiles_per_seq = seq // tm
    cur = lambda i: jnp.minimum(i, nt - 1)
    prv = lambda i: jnp.maximum(i - 1, 0)
    row = lambda t: layer * MOD_ROWS + batch0 + t // tiles_per_seq
    const = lambda shape: pl.BlockSpec(shape, lambda i: (0,) * len(shape), pipeline_mode=pl.Buffered(1))
    r_in, r_out, r_shapes, r_scratch = _route_specs(n, d, tm, prv, row, tm)
    return pl.pallas_call(
        functools.partial(_odd_mix_kernel, tiles_per_seq=tiles_per_seq),
        grid=(nt + 1,),
        in_specs=[
            pl.BlockSpec((tm, d), lambda i: (cur(i), 0)),
            pl.BlockSpec((tm, d // 2), lambda i: (cur(i), 0)),
            pl.BlockSpec((tm, d // 2), lambda i: (nt + cur(i), 0)),
            pl.BlockSpec((tm, LANES), lambda i: (cur(i), 0)),
            pl.BlockSpec((None, 1, d), lambda i: (row(cur(i)) - MOD_ROWS, 0, 5)),
            pl.BlockSpec((None, 1, d), lambda i: (row(cur(i)), 0, 0)),
            pl.BlockSpec((None, 1, d), lambda i: (row(cur(i)), 0, 1)),
            pl.BlockSpec((None, 1, d), lambda i: (row(prv(i)), 0, 2)),
            const((1, d)), const(win_bf.shape), const(conv_w8.shape), const(wout_bf.shape),
        ] + r_in,
        out_specs=[pl.BlockSpec((tm, d), lambda i: (prv(i), 0))] + r_out,
        out_shape=[jax.ShapeDtypeStruct((n, d), F32)] + r_shapes,
        scratch_shapes=[
            pltpu.VMEM((2, tm, d), F32),
            pltpu.VMEM((2, tm, d), F32),
            pltpu.VMEM((8, d), F32),
            pltpu.VMEM((2, tm, d), F32),
        ] + r_scratch,
        compiler_params=_params(("arbitrary",)),
        name="odd_mix",
    )(lat, yg, yg, ewcol, mods, mods, mods, mods, g, win_bf, conv_w8, wout_bf, mods, mods, g2, wr_t, br_t)


def _route_init(carry, upper):
    tm = upper.shape[0]
    carry[...] = jnp.zeros_like(carry)
    r_i = lax.broadcasted_iota(jnp.int32, (tm, tm), 0)
    c_i = lax.broadcasted_iota(jnp.int32, (tm, tm), 1)
    upper[...] = jnp.where(r_i < c_i, 1.0, 0.0).astype(BF16)


def _route_tile(lat, sh, sc, g, wr_ref, br_ref, carry, upper, h_ref, mi_ref, wc_ref, cnt_ref):
    tm = lat.shape[0]
    epg = EXPERTS_PER_GROUP
    h = _rms_mod(lat, g, sh, sc)
    h_hi = h.astype(BF16)
    h_hi_f = h_hi.astype(F32)
    h_ref[...] = _pack_rounded(h_hi_f)
    h_lo = (h - h_hi_f).astype(BF16)
    w = wr_ref[...]
    w1 = w.astype(BF16).astype(F32)
    r1 = w - w1
    w2 = r1.astype(BF16).astype(F32)
    w3 = r1 - w2
    nt = (((1,), (1,)), ((), ()))
    nr = w.shape[0]
    w123 = jnp.concatenate([w1, w2, w3, jnp.zeros((8, w.shape[1]), F32)], axis=0).astype(BF16)
    w12 = jnp.concatenate([w1, w2], axis=0).astype(BF16)
    p_hi = lax.dot_general(w123, h_hi, nt, preferred_element_type=F32)
    p_lo = lax.dot_general(w12, h_lo, nt, preferred_element_type=F32)
    lg = ((p_hi[2 * nr:3 * nr] + p_lo[nr:2 * nr]) + (p_hi[nr:2 * nr] + p_lo[0:nr])) + p_hi[0:nr] + br_ref[...]
    io8 = lax.broadcasted_iota(jnp.int32, (epg, tm), 0)
    gl = lg[0:epg]
    gmax = jnp.max(gl, axis=0, keepdims=True)
    g_idx = jnp.min(jnp.where(gl == gmax, io8, epg), axis=0, keepdims=True)
    g_w = 1.0 / jnp.sum(jnp.exp(gl - gmax), axis=0, keepdims=True)
    e_sel = lg[epg:2 * epg]
    for gi in range(1, N_GROUPS):
        e_sel = jnp.where(g_idx == gi, lg[(gi + 1) * epg:(gi + 2) * epg], e_sel)
    v0 = jnp.max(e_sel, axis=0, keepdims=True)
    i0 = jnp.min(jnp.where(e_sel == v0, io8, epg), axis=0, keepdims=True)
    rest = jnp.where(io8 == i0, -jnp.inf, e_sel)
    v1 = jnp.max(rest, axis=0, keepdims=True)
    i1 = jnp.min(jnp.where(rest == v1, io8, epg), axis=0, keepdims=True)
    t = jnp.exp(v1 - v0)
    w0 = g_w / (1.0 + t)
    w1 = g_w * t / (1.0 + t)
    e0 = g_idx * epg + i0
    e1 = g_idx * epg + i1

    io32 = lax.broadcasted_iota(jnp.int32, (N_EXPERTS, tm), 0)
    hit0 = io32 == e0
    hit1 = io32 == e1
    onehot = jnp.where(hit0 | hit1, 1.0, 0.0)
    cum = jnp.dot(onehot.astype(BF16), upper[...], preferred_element_type=F32) + carry[...]
    rank0 = jnp.sum(jnp.where(hit0, cum, 0.0), axis=0, keepdims=True).astype(jnp.int32)
    rank1 = jnp.sum(jnp.where(hit1, cum, 0.0), axis=0, keepdims=True).astype(jnp.int32)
    carry[...] = carry[...] + jnp.sum(onehot, axis=1, keepdims=True)
    cnt_ref[...] = jnp.broadcast_to(carry[...], cnt_ref.shape)

    mi_ref[...] = jnp.where(io8 == 0, e0, jnp.where(io8 == 1, e1, jnp.where(io8 == 2, rank0,
                            jnp.where(io8 == 3, rank1, 0))))
    io128 = lax.broadcasted_iota(jnp.int32, (LANES, tm), 0)
    wrow = jnp.where(io128 == 0, w0, jnp.where(io128 == 1, w1, 0.0))
    wc_ref[...] = wrow.T


def _route_specs(n, d, tm, tile_of, mods_row, route_rows):
    in_specs = [
        pl.BlockSpec((None, 1, d), lambda i: (mods_row(tile_of(i)), 0, 3)),
        pl.BlockSpec((None, 1, d), lambda i: (mods_row(tile_of(i)), 0, 4)),
        pl.BlockSpec((1, d), lambda i: (0, 0), pipeline_mode=pl.Buffered(1)),
        pl.BlockSpec((ROUTER_ROWS, d), lambda i: (0, 0), pipeline_mode=pl.Buffered(1)),
        pl.BlockSpec((ROUTER_ROWS, 1), lambda i: (0, 0), pipeline_mode=pl.Buffered(1)),
    ]
    out_specs = [
        pl.BlockSpec((tm, d // 2), lambda i: (tile_of(i), 0)),
        pl.BlockSpec((8, tm), lambda i: (0, tile_of(i))),
        pl.BlockSpec((tm, LANES), lambda i: (tile_of(i), 0)),
        pl.BlockSpec((N_EXPERTS, LANES), lambda i: (0, 0)),
    ]
    out_shapes = [
        jax.ShapeDtypeStruct((n, d // 2), jnp.uint32),
        jax.ShapeDtypeStruct((8, n), jnp.int32),
        jax.ShapeDtypeStruct((n, LANES), F32),
        jax.ShapeDtypeStruct((N_EXPERTS, LANES), F32),
    ]
    scratch = [pltpu.VMEM((N_EXPERTS, 1), F32), pltpu.VMEM((route_rows, route_rows), BF16)]
    return in_specs, out_specs, out_shapes, scratch


def _plan_kernel(cnt_ref, mi_ref, dest_ref, be_ref, runs_ref, nv_ref, nu_ref, ps_ref, *, n_blocks):
    bm = MOE_BM

    def per_expert(e, carry):
        blk0, n_runs = carry
        cnt = cnt_ref[e]
        nb = (cnt + bm - 1) // bm
        ps_ref[e] = blk0 * bm

        def fill(b, c):
            be_ref[b] = e
            nv_ref[b] = jnp.minimum(cnt - (b - blk0) * bm, bm)
            return c

        lax.fori_loop(blk0, blk0 + nb, fill, 0)

        @pl.when(nb > 0)
        def _():
            runs_ref[n_runs] = e

        return blk0 + nb, n_runs + jnp.where(nb > 0, 1, 0)

    n_used, n_runs = lax.fori_loop(0, N_EXPERTS, per_expert, (0, 0))
    nu_ref[0] = n_used
    nu_ref[1] = n_runs
    last_e = be_ref[jnp.maximum(n_used - 1, 0)]

    def fill_tail(b, c):
        be_ref[b] = last_e
        nv_ref[b] = 0
        return c

    lax.fori_loop(n_used, n_blocks, fill_tail, 0)

    def fill_runs(k, c):
        runs_ref[k] = last_e
        return c

    lax.fori_loop(n_runs, N_EXPERTS, fill_runs, 0)

    e01 = mi_ref[0:2, :]
    dest = mi_ref[2:4, :]
    for e in range(N_EXPERTS):
        dest = dest + jnp.where(e01 == e, ps_ref[e], 0)
    dest_ref[...] = dest


def _plan(counts, meta_i, n_blocks):
    n = meta_i.shape[1]
    return pl.pallas_call(
        functools.partial(_plan_kernel, n_blocks=n_blocks),
        in_specs=[pl.BlockSpec(memory_space=pltpu.SMEM), pl.BlockSpec(memory_space=pltpu.VMEM)],
        out_specs=[pl.BlockSpec(memory_space=pltpu.VMEM)] + [pl.BlockSpec(memory_space=pltpu.SMEM)] * 4,
        out_shape=[
            jax.ShapeDtypeStruct((2, n), jnp.int32),
            jax.ShapeDtypeStruct((n_blocks,), jnp.int32),
            jax.ShapeDtypeStruct((N_EXPERTS,), jnp.int32),
            jax.ShapeDtypeStruct((n_blocks,), jnp.int32),
            jax.ShapeDtypeStruct((2,), jnp.int32),
        ],
        scratch_shapes=[pltpu.SMEM((N_EXPERTS,), jnp.int32)],
        compiler_params=pltpu.CompilerParams(vmem_limit_bytes=VMEM_LIMIT),
        name="plan",
    )(counts, meta_i)


def _sc_mesh():
    return plsc.VectorSubcoreMesh(core_axis_name="c", subcore_axis_name="s",
                                  num_cores=SC_CORES, num_subcores=SC_SUBCORES)


def _sc_worker():
    return lax.axis_index("s") * SC_CORES + lax.axis_index("c")


def _sc_dispatch(h2, dest, n_rows):
    n, d = h2.shape
    c = SC_CHUNK
    per_w = n // SC_WORKERS
    nchunk = per_w // c
    idx = dest.reshape(2, SC_WORKERS, nchunk, c)

    @functools.partial(
        pl.kernel, mesh=_sc_mesh(), out_type=jax.ShapeDtypeStruct((n_rows, d), h2.dtype),
        scratch_types=[pltpu.VMEM((nchunk, c), jnp.int32), pltpu.VMEM((nchunk, c), jnp.int32),
                       pltpu.VMEM((2, c, d), h2.dtype),
                       pltpu.SemaphoreType.DMA((2,)), pltpu.SemaphoreType.DMA((2,))])
    def k(h_hbm, idx_hbm, xb_hbm, idx0_v, idx1_v, rows_v, gsem, ssem):
        wid = _sc_worker()
        base = wid * per_w
        idx_v = (idx0_v, idx1_v)
        for kk in range(2):
            pltpu.sync_copy(idx_hbm.at[kk, wid], idx_v[kk])

        def get(j, slot):
            return pltpu.make_async_copy(h_hbm.at[pl.ds(base + j * c, c)], rows_v.at[slot], gsem.at[slot])

        def put(j, slot, kk):
            return pltpu.make_async_copy(rows_v.at[slot], xb_hbm.at[idx_v[kk].at[j]], ssem.at[slot])

        get(0, 0).start()

        @pl.loop(0, nchunk, step=2)
        def _(j):
            for slot in range(2):
                jj = j + slot
                get(jj, slot).wait()

                @pl.when(jj >= 1)
                def _():
                    for kk in range(2):
                        put(jj - 1, 1 - slot, kk).wait()

                @pl.when(jj + 1 < nchunk)
                def _():
                    get(jj + 1, 1 - slot).start()

                for kk in range(2):
                    put(jj, slot, kk).start()

        for kk in range(2):
            put(nchunk - 1, (nchunk - 1) % 2, kk).wait()

    return k(h2, idx)


def _sc_gather(y, dest):
    d = y.shape[1]
    total = dest.shape[0] * dest.shape[1]
    c = SC_CHUNK
    per_w = total // SC_WORKERS
    nchunk = per_w // c
    idx = dest.reshape(SC_WORKERS, nchunk, c)

    @functools.partial(
        pl.kernel, mesh=_sc_mesh(), out_type=jax.ShapeDtypeStruct((total, d), y.dtype),
        scratch_types=[pltpu.VMEM((nchunk, c), jnp.int32), pltpu.VMEM((2, c, d), y.dtype),
                       pltpu.SemaphoreType.DMA((2,)), pltpu.SemaphoreType.DMA((2,))])
    def k(y_hbm, idx_hbm, out_hbm, idx_v, rows_v, gsem, ssem):
        wid = _sc_worker()
        base = wid * per_w
        pltpu.sync_copy(idx_hbm.at[wid], idx_v)

        def get(j, slot):
            return pltpu.make_async_copy(y_hbm.at[idx_v.at[j]], rows_v.at[slot], gsem.at[slot])

        def put(j, slot):
            return pltpu.make_async_copy(rows_v.at[slot], out_hbm.at[pl.ds(base + j * c, c)], ssem.at[slot])

        get(0, 0).start()

        @pl.loop(0, nchunk, step=2)
        def _(j):
            for slot in range(2):
                jj = j + slot
                get(jj, slot).wait()

                @pl.when(jj >= 1)
                def _():
                    put(jj - 1, 1 - slot).wait()

                @pl.when(jj + 1 < nchunk)
                def _():
                    get(jj + 1, 1 - slot).start()

                put(jj, slot).start()

        put(nchunk - 1, (nchunk - 1) % 2).wait()

    return k(y, idx)


def _expert_kernel(be_ref, runs_ref, nv_ref, nu_ref, x_ref, wg_hbm, wu_hbm, wd_hbm, y_ref,
                   wgu_s, wd_s, stg_g, stg_u, stg_d, run_s, sems, *, layer):
    b = pl.program_id(0)
    hid = stg_g.shape[2]
    e = be_ref[b]
    n_runs = nu_ref[1]
    changed = jnp.logical_or(b == 0, e != be_ref[jnp.maximum(b - 1, 0)])

    def fetch(run):
        expert = runs_ref[run]
        slot = run % WEIGHT_SLOTS
        return (pltpu.make_async_copy(wg_hbm.at[layer, expert], stg_g.at[slot], sems.at[slot]),
                pltpu.make_async_copy(wu_hbm.at[layer, expert], stg_u.at[slot], sems.at[slot]),
                pltpu.make_async_copy(wd_hbm.at[layer, expert], stg_d.at[slot], sems.at[slot]))

    @pl.when(b == 0)
    def _():
        for r in range(WEIGHT_SLOTS - 1):
            @pl.when(r < n_runs)
            def _():
                for cp in fetch(r):
                    cp.start()

    @pl.when(changed)
    def _():
        run = jnp.where(b == 0, 0, run_s[0] + 1)
        run_s[0] = run
        for cp in fetch(run):
            cp.wait()

        ahead = run + WEIGHT_SLOTS - 1

        @pl.when(ahead < n_runs)
        def _():
            for cp in fetch(ahead):
                cp.start()

        slot = run % WEIGHT_SLOTS
        wgu_s[:, 0:hid] = stg_g[slot].astype(BF16)
        wgu_s[:, hid:2 * hid] = stg_u[slot].astype(BF16)
        wd_s[...] = stg_d[slot].astype(BF16)

    bm, dp = x_ref.shape
    nv = nv_ref[b]
    in_use = b < nu_ref[0]

    def run(rows):
        live = lax.broadcasted_iota(jnp.int32, (rows, dp), 0) < nv
        x = _unpack_rows(jnp.where(live, x_ref[0:rows, :], jnp.uint32(0)))
        gu = jnp.dot(x.astype(BF16), wgu_s[...], preferred_element_type=F32)
        gate = gu[:, 0:hid]
        act = gate * (1.0 / (1.0 + jnp.exp(-gate))) * gu[:, hid:2 * hid]
        y_ref[0:rows, :] = _pack_rows(jnp.dot(act.astype(BF16), wd_s[...], preferred_element_type=F32))

    n_quanta = bm // MOE_QUANTUM
    for q in range(1, n_quanta + 1):
        rows = q * MOE_QUANTUM

        @pl.when(jnp.logical_and(in_use, jnp.logical_and(nv > rows - MOE_QUANTUM, nv <= rows)))
        def _(rows=rows):
            run(rows)
            if rows < bm:
                y_ref[rows:bm, :] = jnp.zeros((bm - rows, dp), y_ref.dtype)

    @pl.when(jnp.logical_not(in_use))
    def _():
        y_ref[...] = jnp.zeros_like(y_ref)


def _experts(block_e, runs, n_valid, n_used, xb, w_gate, w_up, w_down, layer):
    n_rows, dp = xb.shape
    d, hid = w_gate.shape[2], w_gate.shape[3]
    bm = MOE_BM
    n_blocks = n_rows // bm
    hbm = pl.BlockSpec(memory_space=pl.ANY)
    return pl.pallas_call(
        functools.partial(_expert_kernel, layer=layer),
        grid_spec=pltpu.PrefetchScalarGridSpec(
            num_scalar_prefetch=4,
            grid=(n_blocks,),
            in_specs=[
                pl.BlockSpec((bm, dp), lambda b, be, nx, nv, nu: (jnp.minimum(b, nu[0] - 1), 0)),
                hbm, hbm, hbm,
            ],
            out_specs=pl.BlockSpec((bm, dp), lambda b, be, nx, nv, nu: (b, 0)),
            scratch_shapes=[
                pltpu.VMEM((d, 2 * hid), BF16), pltpu.VMEM((hid, d), BF16),
                pltpu.VMEM((WEIGHT_SLOTS, d, hid), F32), pltpu.VMEM((WEIGHT_SLOTS, d, hid), F32),
                pltpu.VMEM((WEIGHT_SLOTS, hid, d), F32),
                pltpu.SMEM((1,), jnp.int32), pltpu.SemaphoreType.DMA((WEIGHT_SLOTS,)),
            ],
        ),
        out_shape=jax.ShapeDtypeStruct((n_rows, dp), jnp.uint32),
        compiler_params=_params(("arbitrary",)),
        name="experts",
    )(block_e, runs, n_valid, n_used, xb, w_gate, w_up, w_down)


def _combine_kernel(lat_ref, y0_ref, y1_ref, wc_ref, gate_ref, gf_ref, *rest):
    o_ref = rest[-1]
    wc = wc_ref[...]
    moe = wc[:, 0:1] * _unpack_rows(y0_ref[...]) + wc[:, 1:2] * _unpack_rows(y1_ref[...])
    out = lat_ref[...] + gate_ref[...] * moe
    ms = jnp.mean(out * out, axis=-1, keepdims=True)
    o_ref[...] = out * lax.rsqrt(ms + EPS) * gf_ref[...]


def _combine(lat, yg, wcol, mods, layer, batch0, g_final, seq, out_rows, tok0, prev_out):
    n, d = lat.shape
    tm = COMBINE_TM
    nt = n // tm
    t0 = tok0 // tm
    tiles_per_seq = seq // tm
    row = lambda i: layer * MOD_ROWS + batch0 + i // tiles_per_seq
    in_specs = [
        pl.BlockSpec((tm, d), lambda i: (i, 0)),
        pl.BlockSpec((tm, d // 2), lambda i: (i, 0)),
        pl.BlockSpec((tm, d // 2), lambda i: (nt + i, 0)),
        pl.BlockSpec((tm, LANES), lambda i: (i, 0)),
        pl.BlockSpec((None, 1, d), lambda i: (row(i), 0, 5)),
        pl.BlockSpec((1, d), lambda i: (0, 0)),
    ]
    args = [lat, yg, yg, wcol, mods, g_final]
    aliases = {}
    if prev_out is not None:
        in_specs.append(pl.BlockSpec(memory_space=pl.ANY))
        args.append(prev_out)
        aliases = {len(args) - 1: 0}
    return pl.pallas_call(
        _combine_kernel,
        grid=(nt,),
        in_specs=in_specs,
        out_specs=pl.BlockSpec((tm, d), lambda i: (t0 + i, 0)),
        out_shape=jax.ShapeDtypeStruct((out_rows, d), F32),
        input_output_aliases=aliases,
        compiler_params=_params(("parallel",)),
        name="combine",
    )(*args)


def _moe_rows(routed, layer, w_gate, w_up, w_down):
    h2, meta_i, wcol, counts = routed
    n = h2.shape[0]
    n_blocks = (2 * n) // MOE_BM + N_EXPERTS
    dest, block_e, runs, n_valid, n_used = _plan(counts[:, 0].astype(jnp.int32), meta_i, n_blocks)
    xb = _sc_dispatch(h2, dest, n_blocks * MOE_BM)
    yb = _experts(block_e, runs, n_valid, n_used, xb, w_gate, w_up, w_down, layer)
    return _sc_gather(yb, dest), wcol


def _rope_tables(seq):
    quarter = HEAD_DIM // 4
    pos = jnp.arange(seq, dtype=F32)
    row_ids = jnp.floor(pos / GRID_W)
    col_ids = pos - row_ids * GRID_W
    inv = ROPE_BASE ** (-jnp.arange(quarter, dtype=F32) / quarter)
    ang_r = row_ids[:, None] * inv
    ang_c = col_ids[:, None] * inv
    zero = jnp.zeros_like(ang_r)
    cos = jnp.concatenate([jnp.cos(ang_r), jnp.cos(ang_r), jnp.cos(ang_c), jnp.cos(ang_c)], axis=-1)
    sa = jnp.concatenate([-jnp.sin(ang_r), zero, -jnp.sin(ang_c), zero], axis=-1)
    sb = jnp.concatenate([zero, jnp.sin(ang_r), zero, jnp.sin(ang_c)], axis=-1)
    rep = LANES // HEAD_DIM
    return tuple(jnp.tile(t, (1, rep)) for t in (cos, sa, sb))


def _router_weights(w_rg, b_rg, w_re, b_re):
    d = w_rg.shape[0]
    pad = EXPERTS_PER_GROUP - N_GROUPS
    wr_t = jnp.concatenate([w_rg.T, jnp.zeros((pad, d), F32), w_re.T], axis=0)
    br_t = jnp.concatenate([b_rg, jnp.full((pad,), NEG_INF, F32), b_re])[:, None]
    return wr_t, br_t


def kernel(x, c, ctx, c_ctx, w_ada, b_ada, g_norm1, g_norm2, g_final, w_in_even, attn_sink, g_sgu,
           w_spatial, b_spatial, w_out_even, w_in_odd, conv_w, w_out_odd, w_router_group,
           b_router_group, w_router_expert, b_router_expert, w_gate, w_up, w_down):
    b, s, d = x.shape
    n = b * s
    n_ctx = ctx.shape[1]
    depth = w_ada.shape[0]
    assert depth == 2 and b + 1 <= MOD_ROWS

    cond = jnp.concatenate([c, c_ctx[None, :], jnp.zeros((MOD_ROWS - b - 1, d), F32)], axis=0)
    mods = _ada(cond, w_ada, b_ada).reshape(depth * MOD_ROWS, 1, 6 * d)
    gf = g_final[None, :]

    lat = x.reshape(n, d)
    w_in_bf = w_in_even[0].astype(BF16)
    tabs = _rope_tables(s)
    qx, k, v, u, z = _even_in(lat, mods, 0, lambda i: i // (s // EVEN_TM), g_norm1[0][None, :], w_in_bf,
                              tabs, s // EVEN_TM, EVEN_TM)
    ones = jnp.ones((n_ctx, LANES), F32)
    zeros = jnp.zeros((n_ctx, LANES), F32)
    _, kc, vc, _, _ = _even_in(ctx.reshape(b * n_ctx, d), mods, 0, lambda i: b, g_norm1[0][None, :],
                               w_in_bf, (ones, zeros, zeros), 1, n_ctx)
    bsp_full = jnp.repeat(b_spatial[0].T, HEAD_DIM, axis=1)
    half = N_Q_HEADS // 2
    w_att = w_out_even[0][:Q_DIM].reshape(2, half, HEAD_DIM, d).transpose(1, 0, 2, 3).reshape(Q_DIM, d)
    w_out_bf = jnp.concatenate([w_att, w_out_even[0][Q_DIM:]], axis=0).astype(BF16)
    conv_w8 = jnp.concatenate([conv_w[0], jnp.zeros((8 - conv_w.shape[1], d), F32)], axis=0)
    w_in_odd_bf = w_in_odd[0].astype(BF16)
    w_out_odd_bf = w_out_odd[0].astype(BF16)
    wsp_bf = w_spatial[0].astype(BF16)
    wr0, br0 = _router_weights(w_router_group[0], b_router_group[0], w_router_expert[0], b_router_expert[0])
    wr1, br1 = _router_weights(w_router_group[1], b_router_group[1], w_router_expert[1], b_router_expert[1])

    part = n // MOE_PARTS
    out = None
    for p in range(MOE_PARTS):
        tok0 = p * part
        batch0 = tok0 // s
        lat_p, *routed = _even_mix(lat, qx, k, v, kc, vc, u, z, attn_sink[0], g_sgu[0][None, :],
                                   wsp_bf, bsp_full, w_out_bf, mods, s, n_ctx,
                                   g_norm2[0][None, :], wr0, br0, tok0, part)
        moe_rows = _moe_rows(routed, 0, w_gate, w_up, w_down)
        lat_p, *routed = _odd_mix(lat_p, moe_rows, mods, 1, g_norm1[1][None, :], w_in_odd_bf, conv_w8,
                                  w_out_odd_bf, s, g_norm2[1][None, :], wr1, br1, batch0)
        yg, wcol = _moe_rows(routed, 1, w_gate, w_up, w_down)
        out = _combine(lat_p, yg, wcol, mods, 1, batch0, gf, s, n, tok0, out)
    return out.reshape(b, s, d)
```

```python
import functools

import jax
import jax.numpy as jnp
from jax import lax
from jax.experimental import pallas as pl
from jax.experimental.pallas import tpu as pltpu
from jax.experimental.pallas import tpu_sc as plsc

F32 = jnp.float32
BF16 = jnp.bfloat16
HIGHEST = lax.Precision.HIGHEST

GRID_W = 64
N_Q_HEADS = 8
N_KV_HEADS = 2
HEAD_DIM = 64
ATT_BLOCK = 128
ROPE_BASE = 10000.0
Q_DIM = N_Q_HEADS * HEAD_DIM
KV_DIM = N_KV_HEADS * HEAD_DIM
SG_GROUPS = 8
SG_WIDTH = SG_GROUPS * HEAD_DIM
N_GROUPS = 4
EXPERTS_PER_GROUP = 8
N_EXPERTS = N_GROUPS * EXPERTS_PER_GROUP
EPS = 1e-6
NEG_INF = -1e30

LANES = 128
SC_CORES = 2
SC_SUBCORES = 16
SC_WORKERS = SC_CORES * SC_SUBCORES
SC_CHUNK = 32
MOD_ROWS = 8
ROUTER_ROWS = EXPERTS_PER_GROUP + N_EXPERTS
VMEM_LIMIT = 56 * 1024 * 1024

ADA_TN = 1536
EVEN_TM = 1024
EVEN_SUB = 512
ATT_TQ = 1024
ODD_TM = 512
ODD_SUB = 256
ROUTE_SUB = 256
MOE_BM = 1024
MOE_QUANTUM = 128
MOE_PARTS = 2
WEIGHT_SLOTS = 3
COMBINE_TM = 512


def _params(sem):
    return pltpu.CompilerParams(dimension_semantics=sem, vmem_limit_bytes=VMEM_LIMIT)


def _rms_mod(x, g, shift, scale):
    ms = jnp.mean(x * x, axis=-1, keepdims=True)
    return (x * lax.rsqrt(ms + EPS)) * (g * (1.0 + scale)) + shift


def _pack_rounded(a):
    w = a.shape[1] // 2
    hi = pltpu.bitcast(a[:, :w], jnp.uint32)
    lo = pltpu.bitcast(a[:, w:], jnp.uint32)
    return hi | (lo >> 16)


def _pack_rows(a):
    return _pack_rounded(a.astype(BF16).astype(F32))


def _unpack_rows(p):
    hi = pltpu.bitcast(p & jnp.uint32(0xFFFF0000), F32)
    lo = pltpu.bitcast(p << 16, F32)
    return jnp.concatenate([hi, lo], axis=1)


def _ada_kernel(a_ref, w_ref, b_ref, o_ref):
    a = a_ref[...]
    s = a * (1.0 / (1.0 + jnp.exp(-a)))
    o_ref[0] = jnp.dot(s, w_ref[0], preferred_element_type=F32, precision=HIGHEST) + b_ref[0]


def _ada(cond, w_ada, b_ada):
    depth, d, six_d = w_ada.shape
    return pl.pallas_call(
        _ada_kernel,
        grid=(depth, six_d // ADA_TN),
        in_specs=[
            pl.BlockSpec((MOD_ROWS, d), lambda l, j: (0, 0)),
            pl.BlockSpec((1, d, ADA_TN), lambda l, j: (l, 0, j)),
            pl.BlockSpec((1, 1, ADA_TN), lambda l, j: (l, 0, j)),
        ],
        out_specs=pl.BlockSpec((1, MOD_ROWS, ADA_TN), lambda l, j: (l, 0, j)),
        out_shape=jax.ShapeDtypeStruct((depth, MOD_ROWS, six_d), F32),
        compiler_params=_params(("arbitrary", "arbitrary")),
        name="ada",
    )(cond, w_ada, b_ada.reshape(depth, 1, six_d))


def _even_in_kernel(x_ref, sh_ref, sc_ref, g_ref, w_ref, cos_ref, sa_ref, sb_ref,
                    qx_ref, k_ref, v_ref, u_ref, z_ref):
    tm = x_ref.shape[0]
    sub = min(tm, EVEN_SUB)
    scale = HEAD_DIM ** -0.5
    low = lax.broadcasted_iota(jnp.int32, (sub, LANES), 1) < HEAD_DIM
    heads_per_kv = N_Q_HEADS // N_KV_HEADS
    u0 = Q_DIM + 2 * KV_DIM

    for r0 in range(0, tm, sub):
        rows = slice(r0, r0 + sub)
        h = _rms_mod(x_ref[rows, :], g_ref[...], sh_ref[...], sc_ref[...])
        p = jnp.dot(h.astype(BF16), w_ref[...], preferred_element_type=F32)
        cos, sa, sb = cos_ref[rows, :], sa_ref[rows, :], sb_ref[rows, :]

        def rope(t):
            return t * cos + pltpu.roll(t, LANES - 16, 1) * sa + pltpu.roll(t, 16, 1) * sb

        for cblk in range(Q_DIM // LANES):
            t = rope(p[:, cblk * LANES:(cblk + 1) * LANES]) * scale
            sw = pltpu.roll(t, HEAD_DIM, 1)
            zero = jnp.zeros_like(t)
            if (2 * cblk) // heads_per_kv == 0:
                first, second = jnp.where(low, t, zero), jnp.where(low, sw, zero)
            else:
                first, second = jnp.where(low, zero, sw), jnp.where(low, zero, t)
            qx_ref[rows, (2 * cblk) * LANES:(2 * cblk + 1) * LANES] = first.astype(BF16)
            qx_ref[rows, (2 * cblk + 1) * LANES:(2 * cblk + 2) * LANES] = second.astype(BF16)

        k_ref[rows, :] = rope(p[:, Q_DIM:Q_DIM + KV_DIM]).astype(BF16)
        v_ref[rows, :] = p[:, Q_DIM + KV_DIM:Q_DIM + 2 * KV_DIM].astype(BF16)
        u_ref[rows, :] = p[:, u0:u0 + SG_WIDTH]
        z_ref[rows, :] = p[:, u0 + SG_WIDTH:u0 + 2 * SG_WIDTH]


def _even_in(x2d, mods, layer, mod_row_fn, g, w_bf, tabs, tab_blocks, tm):
    n, d = x2d.shape
    ein = w_bf.shape[1]
    cos, sa, sb = tabs
    row = lambda i: layer * MOD_ROWS + mod_row_fn(i)
    tab_spec = pl.BlockSpec((tm, LANES), lambda i: (i % tab_blocks, 0))
    qx_dim = N_Q_HEADS * LANES
    return pl.pallas_call(
        _even_in_kernel,
        grid=(n // tm,),
        in_specs=[
            pl.BlockSpec((tm, d), lambda i: (i, 0)),
            pl.BlockSpec((None, 1, d), lambda i: (row(i), 0, 0)),
            pl.BlockSpec((None, 1, d), lambda i: (row(i), 0, 1)),
            pl.BlockSpec((1, d), lambda i: (0, 0)),
            pl.BlockSpec((d, ein), lambda i: (0, 0)),
            tab_spec, tab_spec, tab_spec,
        ],
        out_specs=[
            pl.BlockSpec((tm, qx_dim), lambda i: (i, 0)),
            pl.BlockSpec((tm, KV_DIM), lambda i: (i, 0)),
            pl.BlockSpec((tm, KV_DIM), lambda i: (i, 0)),
            pl.BlockSpec((tm, SG_WIDTH), lambda i: (i, 0)),
            pl.BlockSpec((tm, SG_WIDTH), lambda i: (i, 0)),
        ],
        out_shape=[
            jax.ShapeDtypeStruct((n, qx_dim), BF16),
            jax.ShapeDtypeStruct((n, KV_DIM), BF16),
            jax.ShapeDtypeStruct((n, KV_DIM), BF16),
            jax.ShapeDtypeStruct((n, SG_WIDTH), F32),
            jax.ShapeDtypeStruct((n, SG_WIDTH), F32),
        ],
        compiler_params=_params(("parallel",)),
        name="even_in",
    )(x2d, mods, mods, g, w_bf, cos, sa, sb)


def _gelu(x):
    return 0.5 * x * (1.0 + lax.erf(x * (2.0 ** -0.5)))


def _even_mix_kernel(sink_ref, lat_ref, qx_ref, km_ref, kp_ref, kn_ref, vm_ref, vp_ref, vn_ref,
                     kc_ref, vc_ref, u_ref, z_ref, gsgu_ref, wsp_ref, bsp_ref, wout_ref, gate_ref,
                     sh2_ref, sc2_ref, g2_ref, wr_ref, br_ref,
                     o_ref, h_ref, mi_ref, wc_ref, cnt_ref,
                     kband, vband, mixin, carry, upper, *, tiles_per_seq):
    i = pl.program_id(0)
    tq = qx_ref.shape[0]

    @pl.when(i == 0)
    def _():
        _route_init(carry, upper)

    blk = ATT_BLOCK
    nsub = tq // blk
    n_ctx = kc_ref.shape[0]
    first = (i % tiles_per_seq) == 0
    last = (i % tiles_per_seq) == tiles_per_seq - 1

    kband[0:blk] = kp_ref[...]
    kband[blk:blk + tq] = km_ref[...]
    kband[blk + tq:] = kn_ref[...]
    vband[0:blk] = vp_ref[...]
    vband[blk:blk + tq] = vm_ref[...]
    vband[blk + tq:] = vn_ref[...]

    rows = N_Q_HEADS * blk
    tok = lax.broadcasted_iota(jnp.int32, (rows, blk), 0) & (blk - 1)
    col = lax.broadcasted_iota(jnp.int32, (rows, blk), 1)
    tri_prev = col >= tok
    tri_next = col <= tok
    head = lax.broadcasted_iota(jnp.int32, (rows, 1), 0) // blk
    sink_col = jnp.zeros((rows, 1), F32)
    for hd in range(N_Q_HEADS):
        sink_col = jnp.where(head == hd, sink_ref[hd], sink_col)
    lane_low = lax.broadcasted_iota(jnp.int32, (blk, LANES), 1) < HEAD_DIM
    ones = jnp.ones((n_ctx + 3 * blk, LANES), BF16)
    nt = (((1,), (1,)), ((), ()))

    def sub_block(j, c):
        r0 = pl.multiple_of(j * blk, blk)
        ok_prev = jnp.logical_not(jnp.logical_and(first, j == 0))
        ok_next = jnp.logical_not(jnp.logical_and(last, j == nsub - 1))
        qs = jnp.concatenate([qx_ref[pl.ds(r0, blk), hd * LANES:(hd + 1) * LANES]
                              for hd in range(N_Q_HEADS)], axis=0)
        kall = jnp.concatenate([kc_ref[...], kband[pl.ds(r0, 3 * blk), :]], axis=0)
        vall = jnp.concatenate([vc_ref[...], vband[pl.ds(r0, 3 * blk), :]], axis=0)
        s = lax.dot_general(qs, kall, nt, preferred_element_type=F32)
        c0 = n_ctx
        s = jnp.concatenate([
            s[:, :c0],
            jnp.where(jnp.logical_and(tri_prev, ok_prev), s[:, c0:c0 + blk], NEG_INF),
            s[:, c0 + blk:c0 + 2 * blk],
            jnp.where(jnp.logical_and(tri_next, ok_next), s[:, c0 + 2 * blk:], NEG_INF),
        ], axis=1)
        m = jnp.maximum(jnp.max(s, axis=-1, keepdims=True), sink_col)
        p = jnp.exp(s - m).astype(BF16)
        o = jnp.dot(p, jnp.concatenate([vall, ones], axis=1), preferred_element_type=F32)
        att = o[:, :LANES] / (o[:, LANES:] + jnp.exp(sink_col - m))
        half = N_Q_HEADS // 2
        for hd in range(half):
            pair = jnp.where(lane_low, att[hd * blk:(hd + 1) * blk], att[(hd + half) * blk:(hd + half + 1) * blk])
            mixin[pl.ds(r0, blk), hd * LANES:(hd + 1) * LANES] = pair.astype(BF16)

        ug = _gelu(u_ref[pl.ds(r0, blk), :])
        zg = _gelu(z_ref[pl.ds(r0, blk), :])
        mu = jnp.mean(zg, axis=-1, keepdims=True)
        zc = zg - mu
        zn = zc * lax.rsqrt(jnp.mean(zc * zc, axis=-1, keepdims=True) + EPS) * gsgu_ref[...]
        for pair in range(SG_GROUPS // 2):
            zp = zn[:, pair * LANES:(pair + 1) * LANES]
            zero = jnp.zeros_like(zp)
            lo = jnp.where(lane_low, zp, zero).astype(BF16)
            hi = jnp.where(lane_low, zero, zp).astype(BF16)
            sg = (jnp.dot(wsp_ref[2 * pair], lo, preferred_element_type=F32)
                  + jnp.dot(wsp_ref[2 * pair + 1], hi, preferred_element_type=F32)
                  + bsp_ref[:, pair * LANES:(pair + 1) * LANES])
            mixin[pl.ds(r0, blk), Q_DIM + pair * LANES:Q_DIM + (pair + 1) * LANES] = (
                ug[:, pair * LANES:(pair + 1) * LANES] * sg).astype(BF16)
        return c

    lax.fori_loop(0, nsub, sub_block, 0, unroll=True)
    for r0 in range(0, tq, ROUTE_SUB):
        rows = slice(r0, r0 + ROUTE_SUB)
        mix = jnp.dot(mixin[rows, :], wout_ref[...], preferred_element_type=F32)
        lat = lat_ref[rows, :] + gate_ref[...] * mix
        o_ref[rows, :] = lat
        _route_tile(lat, sh2_ref[...], sc2_ref[...], g2_ref[...], wr_ref, br_ref, carry, upper,
                    h_ref.at[rows, :], mi_ref.at[:, rows], wc_ref.at[rows, :], cnt_ref)


def _even_mix(lat, qx, k, v, kc, vc, u, z, sink, g_sgu, wsp_bf, bsp_full, wout_bf, mods, seq, ctx_len,
              g2, wr_t, br_t, tok0, n):
    d = lat.shape[1]
    tq = ATT_TQ
    tiles_per_seq = seq // tq
    sub = tq // ATT_BLOCK
    nblk = lat.shape[0] // ATT_BLOCK
    t0 = tok0 // tq
    main = lambda w: pl.BlockSpec((tq, w), lambda i: (t0 + i, 0))
    prev = pl.BlockSpec((ATT_BLOCK, KV_DIM), lambda i: (jnp.maximum((t0 + i) * sub - 1, 0), 0))
    nxt = pl.BlockSpec((ATT_BLOCK, KV_DIM), lambda i: (jnp.minimum((t0 + i + 1) * sub, nblk - 1), 0))
    batch = lambda t: (t0 + t) // tiles_per_seq
    ctxs = pl.BlockSpec((ctx_len, KV_DIM), lambda i: (batch(i), 0))
    const = lambda shape: pl.BlockSpec(shape, lambda i: (0,) * len(shape), pipeline_mode=pl.Buffered(1))
    r_in, r_out, r_shapes, r_scratch = _route_specs(n, d, tq, lambda i: i, batch, ROUTE_SUB)
    return pl.pallas_call(
        functools.partial(_even_mix_kernel, tiles_per_seq=tiles_per_seq),
        grid=(n // tq,),
        in_specs=[
            pl.BlockSpec(memory_space=pltpu.SMEM),
            main(d), main(qx.shape[1]),
            main(KV_DIM), prev, nxt,
            main(KV_DIM), prev, nxt,
            ctxs, ctxs,
            main(SG_WIDTH), main(SG_WIDTH),
            const((1, SG_WIDTH)), const(wsp_bf.shape), const(bsp_full.shape), const(wout_bf.shape),
            pl.BlockSpec((None, 1, d), lambda i: (batch(i), 0, 2)),
        ] + r_in,
        out_specs=[pl.BlockSpec((tq, d), lambda i: (i, 0))] + r_out,
        out_shape=[jax.ShapeDtypeStruct((n, d), F32)] + r_shapes,
        scratch_shapes=[
            pltpu.VMEM((tq + 2 * ATT_BLOCK, KV_DIM), BF16),
            pltpu.VMEM((tq + 2 * ATT_BLOCK, KV_DIM), BF16),
            pltpu.VMEM((tq, Q_DIM + SG_WIDTH), BF16),
        ] + r_scratch,
        compiler_params=_params(("arbitrary",)),
        name="even_mix",
    )(sink, lat, qx, k, k, k, v, v, v, kc, vc, u, z, g_sgu, wsp_bf, bsp_full, wout_bf, mods,
      mods, mods, g2, wr_t, br_t)


def _odd_mix_kernel(x_ref, e0_ref, e1_ref, ew_ref, egate_ref, sh_ref, sc_ref, gate_ref, g_ref,
                    win_ref, cw_ref, wout_ref, sh2_ref, sc2_ref, g2_ref, wr_ref, br_ref,
                    o_ref, h_ref, mi_ref, wc_ref, cnt_ref,
                    y_s, bg_s, tail_s, x_s, carry, upper, *, tiles_per_seq):
    i = pl.program_id(0)
    n_tiles = pl.num_programs(0) - 1
    tm, d = x_ref.shape
    cur = i % 2
    prv = 1 - cur

    @pl.when(i == 0)
    def _():
        _route_init(carry, upper)

    tail_s[...] = y_s[cur, tm - 8:tm, :]

    @pl.when(i < n_tiles)
    def _():
        for r0 in range(0, tm, ODD_SUB):
            rows = slice(r0, r0 + ODD_SUB)
            ew = ew_ref[rows, :]
            moe = ew[:, 0:1] * _unpack_rows(e0_ref[rows, :]) + ew[:, 1:2] * _unpack_rows(e1_ref[rows, :])
            x = x_ref[rows, :] + egate_ref[...] * moe
            x_s[cur, rows, :] = x
            h = _rms_mod(x, g_ref[...], sh_ref[...], sc_ref[...])
            p = jnp.dot(h.astype(BF16), win_ref[...], preferred_element_type=F32)
            bg_s[cur, rows, :] = p[:, 0:d]
            y_s[cur, rows, :] = p[:, d:2 * d] * p[:, 2 * d:3 * d]

    @pl.when(i >= 1)
    def _():
        t_prev = i - 1
        first = (t_prev % tiles_per_seq) == 0
        last = (t_prev % tiles_per_seq) == tiles_per_seq - 1
        y = y_s[prv]
        left = jnp.where(first, 0.0, tail_s[7:8, :])
        right = jnp.where(last, 0.0, y_s[cur, 0:1, :])
        ridx = lax.broadcasted_iota(jnp.int32, (tm, d), 0)
        y_dn = jnp.where(ridx == 0, left, pltpu.roll(y, 1, 0))
        y_up = jnp.where(ridx == tm - 1, right, pltpu.roll(y, tm - 1, 0))
        conv = y_dn * cw_ref[0:1, :] + y * cw_ref[1:2, :] + y_up * cw_ref[2:3, :]
        mix = jnp.dot((bg_s[prv] * conv).astype(BF16), wout_ref[...], preferred_element_type=F32)
        lat = x_s[prv] + gate_ref[...] * mix
        o_ref[...] = lat
        _route_tile(lat, sh2_ref[...], sc2_ref[...], g2_ref[...], wr_ref, br_ref, carry, upper,
                    h_ref, mi_ref, wc_ref, cnt_ref)


def _odd_mix(lat, moe_rows, mods, layer, g, win_bf, conv_w8, wout_bf, seq, g2, wr_t, br_t, batch0):
    yg, ewcol = moe_rows
    n, d = lat.shape
    tm = ODD_TM
    nt = n // tm
    tiles_per_seq = seq // tm
    cur = lambda i: jnp.minimum(i, nt - 1)
    prv = lambda i: jnp.maximum(i - 1, 0)
    row = lambda t: layer * MOD_ROWS + batch0 + t // tiles_per_seq
    const = lambda shape: pl.BlockSpec(shape, lambda i: (0,) * len(shape), pipeline_mode=pl.Buffered(1))
    r_in, r_out, r_shapes, r_scratch = _route_specs(n, d, tm, prv, row, tm)
    return pl.pallas_call(
        functools.partial(_odd_mix_kernel, tiles_per_seq=tiles_per_seq),
        grid=(nt + 1,),
        in_specs=[
            pl.BlockSpec((tm, d), lambda i: (cur(i), 0)),
            pl.BlockSpec((tm, d // 2), lambda i: (cur(i), 0)),
            pl.BlockSpec((tm, d // 2), lambda i: (nt + cur(i), 0)),
            pl.BlockSpec((tm, LANES), lambda i: (cur(i), 0)),
            pl.BlockSpec((None, 1, d), lambda i: (row(cur(i)) - MOD_ROWS, 0, 5)),
            pl.BlockSpec((None, 1, d), lambda i: (row(cur(i)), 0, 0)),
            pl.BlockSpec((None, 1, d), lambda i: (row(cur(i)), 0, 1)),
            pl.BlockSpec((None, 1, d), lambda i: (row(prv(i)), 0, 2)),
            const((1, d)), const(win_bf.shape), const(conv_w8.shape), const(wout_bf.shape),
        ] + r_in,
        out_specs=[pl.BlockSpec((tm, d), lambda i: (prv(i), 0))] + r_out,
        out_shape=[jax.ShapeDtypeStruct((n, d), F32)] + r_shapes,
        scratch_shapes=[
            pltpu.VMEM((2, tm, d), F32),
            pltpu.VMEM((2, tm, d), F32),
            pltpu.VMEM((8, d), F32),
            pltpu.VMEM((2, tm, d), F32),
        ] + r_scratch,
        compiler_params=_params(("arbitrary",)),
        name="odd_mix",
    )(lat, yg, yg, ewcol, mods, mods, mods, mods, g, win_bf, conv_w8, wout_bf, mods, mods, g2, wr_t, br_t)


def _route_init(carry, upper):
    tm = upper.shape[0]
    carry[...] = jnp.zeros_like(carry)
    r_i = lax.broadcasted_iota(jnp.int32, (tm, tm), 0)
    c_i = lax.broadcasted_iota(jnp.int32, (tm, tm), 1)
    upper[...] = jnp.where(r_i < c_i, 1.0, 0.0).astype(BF16)


def _route_tile(lat, sh, sc, g, wr_ref, br_ref, carry, upper, h_ref, mi_ref, wc_ref, cnt_ref):
    tm = lat.shape[0]
    epg = EXPERTS_PER_GROUP
    h = _rms_mod(lat, g, sh, sc)
    h_hi = h.astype(BF16)
    h_hi_f = h_hi.astype(F32)
    h_ref[...] = _pack_rounded(h_hi_f)
    h_lo = (h - h_hi_f).astype(BF16)
    w = wr_ref[...]
    w1 = w.astype(BF16).astype(F32)
    r1 = w - w1
    w2 = r1.astype(BF16).astype(F32)
    w3 = r1 - w2
    nt = (((1,), (1,)), ((), ()))
    nr = w.shape[0]
    w123 = jnp.concatenate([w1, w2, w3, jnp.zeros((8, w.shape[1]), F32)], axis=0).astype(BF16)
    w12 = jnp.concatenate([w1, w2], axis=0).astype(BF16)
    p_hi = lax.dot_general(w123, h_hi, nt, preferred_element_type=F32)
    p_lo = lax.dot_general(w12, h_lo, nt, preferred_element_type=F32)
    lg = ((p_hi[2 * nr:3 * nr] + p_lo[nr:2 * nr]) + (p_hi[nr:2 * nr] + p_lo[0:nr])) + p_hi[0:nr] + br_ref[...]
    io8 = lax.broadcasted_iota(jnp.int32, (epg, tm), 0)
    gl = lg[0:epg]
    gmax = jnp.max(gl, axis=0, keepdims=True)
    g_idx = jnp.min(jnp.where(gl == gmax, io8, epg), axis=0, keepdims=True)
    g_w = 1.0 / jnp.sum(jnp.exp(gl - gmax), axis=0, keepdims=True)
    e_sel = lg[epg:2 * epg]
    for gi in range(1, N_GROUPS):
        e_sel = jnp.where(g_idx == gi, lg[(gi + 1) * epg:(gi + 2) * epg], e_sel)
    v0 = jnp.max(e_sel, axis=0, keepdims=True)
    i0 = jnp.min(jnp.where(e_sel == v0, io8, epg), axis=0, keepdims=True)
    rest = jnp.where(io8 == i0, -jnp.inf, e_sel)
    v1 = jnp.max(rest, axis=0, keepdims=True)
    i1 = jnp.min(jnp.where(rest == v1, io8, epg), axis=0, keepdims=True)
    t = jnp.exp(v1 - v0)
    w0 = g_w / (1.0 + t)
    w1 = g_w * t / (1.0 + t)
    e0 = g_idx * epg + i0
    e1 = g_idx * epg + i1

    io32 = lax.broadcasted_iota(jnp.int32, (N_EXPERTS, tm), 0)
    hit0 = io32 == e0
    hit1 = io32 == e1
    onehot = jnp.where(hit0 | hit1, 1.0, 0.0)
    cum = jnp.dot(onehot.astype(BF16), upper[...], preferred_element_type=F32) + carry[...]
    rank0 = jnp.sum(jnp.where(hit0, cum, 0.0), axis=0, keepdims=True).astype(jnp.int32)
    rank1 = jnp.sum(jnp.where(hit1, cum, 0.0), axis=0, keepdims=True).astype(jnp.int32)
    carry[...] = carry[...] + jnp.sum(onehot, axis=1, keepdims=True)
    cnt_ref[...] = jnp.broadcast_to(carry[...], cnt_ref.shape)

    mi_ref[...] = jnp.where(io8 == 0, e0, jnp.where(io8 == 1, e1, jnp.where(io8 == 2, rank0,
                            jnp.where(io8 == 3, rank1, 0))))
    io128 = lax.broadcasted_iota(jnp.int32, (LANES, tm), 0)
    wrow = jnp.where(io128 == 0, w0, jnp.where(io128 == 1, w1, 0.0))
    wc_ref[...] = wrow.T


def _route_specs(n, d, tm, tile_of, mods_row, route_rows):
    in_specs = [
        pl.BlockSpec((None, 1, d), lambda i: (mods_row(tile_of(i)), 0, 3)),
        pl.BlockSpec((None, 1, d), lambda i: (mods_row(tile_of(i)), 0, 4)),
        pl.BlockSpec((1, d), lambda i: (0, 0), pipeline_mode=pl.Buffered(1)),
        pl.BlockSpec((ROUTER_ROWS, d), lambda i: (0, 0), pipeline_mode=pl.Buffered(1)),
        pl.BlockSpec((ROUTER_ROWS, 1), lambda i: (0, 0), pipeline_mode=pl.Buffered(1)),
    ]
    out_specs = [
        pl.BlockSpec((tm, d // 2), lambda i: (tile_of(i), 0)),
        pl.BlockSpec((8, tm), lambda i: (0, tile_of(i))),
        pl.BlockSpec((tm, LANES), lambda i: (tile_of(i), 0)),
        pl.BlockSpec((N_EXPERTS, LANES), lambda i: (0, 0)),
    ]
    out_shapes = [
        jax.ShapeDtypeStruct((n, d // 2), jnp.uint32),
        jax.ShapeDtypeStruct((8, n), jnp.int32),
        jax.ShapeDtypeStruct((n, LANES), F32),
        jax.ShapeDtypeStruct((N_EXPERTS, LANES), F32),
    ]
    scratch = [pltpu.VMEM((N_EXPERTS, 1), F32), pltpu.VMEM((route_rows, route_rows), BF16)]
    return in_specs, out_specs, out_shapes, scratch


def _plan_kernel(cnt_ref, mi_ref, dest_ref, be_ref, runs_ref, nv_ref, nu_ref, ps_ref, *, n_blocks):
    bm = MOE_BM

    def per_expert(e, carry):
        blk0, n_runs = carry
        cnt = cnt_ref[e]
        nb = (cnt + bm - 1) // bm
        ps_ref[e] = blk0 * bm

        def fill(b, c):
            be_ref[b] = e
            nv_ref[b] = jnp.minimum(cnt - (b - blk0) * bm, bm)
            return c

        lax.fori_loop(blk0, blk0 + nb, fill, 0)

        @pl.when(nb > 0)
        def _():
            runs_ref[n_runs] = e

        return blk0 + nb, n_runs + jnp.where(nb > 0, 1, 0)

    n_used, n_runs = lax.fori_loop(0, N_EXPERTS, per_expert, (0, 0))
    nu_ref[0] = n_used
    nu_ref[1] = n_runs
    last_e = be_ref[jnp.maximum(n_used - 1, 0)]

    def fill_tail(b, c):
        be_ref[b] = last_e
        nv_ref[b] = 0
        return c

    lax.fori_loop(n_used, n_blocks, fill_tail, 0)

    def fill_runs(k, c):
        runs_ref[k] = last_e
        return c

    lax.fori_loop(n_runs, N_EXPERTS, fill_runs, 0)

    e01 = mi_ref[0:2, :]
    dest = mi_ref[2:4, :]
    for e in range(N_EXPERTS):
        dest = dest + jnp.where(e01 == e, ps_ref[e], 0)
    dest_ref[...] = dest


def _plan(counts, meta_i, n_blocks):
    n = meta_i.shape[1]
    return pl.pallas_call(
        functools.partial(_plan_kernel, n_blocks=n_blocks),
        in_specs=[pl.BlockSpec(memory_space=pltpu.SMEM), pl.BlockSpec(memory_space=pltpu.VMEM)],
        out_specs=[pl.BlockSpec(memory_space=pltpu.VMEM)] + [pl.BlockSpec(memory_space=pltpu.SMEM)] * 4,
        out_shape=[
            jax.ShapeDtypeStruct((2, n), jnp.int32),
            jax.ShapeDtypeStruct((n_blocks,), jnp.int32),
            jax.ShapeDtypeStruct((N_EXPERTS,), jnp.int32),
            jax.ShapeDtypeStruct((n_blocks,), jnp.int32),
            jax.ShapeDtypeStruct((2,), jnp.int32),
        ],
        scratch_shapes=[pltpu.SMEM((N_EXPERTS,), jnp.int32)],
        compiler_params=pltpu.CompilerParams(vmem_limit_bytes=VMEM_LIMIT),
        name="plan",
    )(counts, meta_i)


def _sc_mesh():
    return plsc.VectorSubcoreMesh(core_axis_name="c", subcore_axis_name="s",
                                  num_cores=SC_CORES, num_subcores=SC_SUBCORES)


def _sc_worker():
    return lax.axis_index("s") * SC_CORES + lax.axis_index("c")


def _sc_dispatch(h2, dest, n_rows):
    n, d = h2.shape
    c = SC_CHUNK
    per_w = n // SC_WORKERS
    nchunk = per_w // c
    idx = dest.reshape(2, SC_WORKERS, nchunk, c)

    @functools.partial(
        pl.kernel, mesh=_sc_mesh(), out_type=jax.ShapeDtypeStruct((n_rows, d), h2.dtype),
        scratch_types=[pltpu.VMEM((nchunk, c), jnp.int32), pltpu.VMEM((nchunk, c), jnp.int32),
                       pltpu.VMEM((2, c, d), h2.dtype),
                       pltpu.SemaphoreType.DMA((2,)), pltpu.SemaphoreType.DMA((2,))])
    def k(h_hbm, idx_hbm, xb_hbm, idx0_v, idx1_v, rows_v, gsem, ssem):
        wid = _sc_worker()
        base = wid * per_w
        idx_v = (idx0_v, idx1_v)
        for kk in range(2):
            pltpu.sync_copy(idx_hbm.at[kk, wid], idx_v[kk])

        def get(j, slot):
            return pltpu.make_async_copy(h_hbm.at[pl.ds(base + j * c, c)], rows_v.at[slot], gsem.at[slot])

        def put(j, slot, kk):
            return pltpu.make_async_copy(rows_v.at[slot], xb_hbm.at[idx_v[kk].at[j]], ssem.at[slot])

        get(0, 0).start()

        @pl.loop(0, nchunk, step=2)
        def _(j):
            for slot in range(2):
                jj = j + slot
                get(jj, slot).wait()

                @pl.when(jj >= 1)
                def _():
                    for kk in range(2):
                        put(jj - 1, 1 - slot, kk).wait()

                @pl.when(jj + 1 < nchunk)
                def _():
                    get(jj + 1, 1 - slot).start()

                for kk in range(2):
                    put(jj, slot, kk).start()

        for kk in range(2):
            put(nchunk - 1, (nchunk - 1) % 2, kk).wait()

    return k(h2, idx)


def _sc_gather(y, dest):
    d = y.shape[1]
    total = dest.shape[0] * dest.shape[1]
    c = SC_CHUNK
    per_w = total // SC_WORKERS
    nchunk = per_w // c
    idx = dest.reshape(SC_WORKERS, nchunk, c)

    @functools.partial(
        pl.kernel, mesh=_sc_mesh(), out_type=jax.ShapeDtypeStruct((total, d), y.dtype),
        scratch_types=[pltpu.VMEM((nchunk, c), jnp.int32), pltpu.VMEM((2, c, d), y.dtype),
                       pltpu.SemaphoreType.DMA((2,)), pltpu.SemaphoreType.DMA((2,))])
    def k(y_hbm, idx_hbm, out_hbm, idx_v, rows_v, gsem, ssem):
        wid = _sc_worker()
        base = wid * per_w
        pltpu.sync_copy(idx_hbm.at[wid], idx_v)

        def get(j, slot):
            return pltpu.make_async_copy(y_hbm.at[idx_v.at[j]], rows_v.at[slot], gsem.at[slot])

        def put(j, slot):
            return pltpu.make_async_copy(rows_v.at[slot], out_hbm.at[pl.ds(base + j * c, c)], ssem.at[slot])

        get(0, 0).start()

        @pl.loop(0, nchunk, step=2)
        def _(j):
            for slot in range(2):
                jj = j + slot
                get(jj, slot).wait()

                @pl.when(jj >= 1)
                def _():
                    put(jj - 1, 1 - slot).wait()

                @pl.when(jj + 1 < nchunk)
                def _():
                    get(jj + 1, 1 - slot).start()

                put(jj, slot).start()

        put(nchunk - 1, (nchunk - 1) % 2).wait()

    return k(y, idx)


def _expert_kernel(be_ref, runs_ref, nv_ref, nu_ref, x_ref, wg_hbm, wu_hbm, wd_hbm, y_ref,
                   wgu_s, wd_s, stg_g, stg_u, stg_d, run_s, sems, *, layer):
    b = pl.program_id(0)
    hid = stg_g.shape[2]
    e = be_ref[b]
    n_runs = nu_ref[1]
    changed = jnp.logical_or(b == 0, e != be_ref[jnp.maximum(b - 1, 0)])

    def fetch(run):
        expert = runs_ref[run]
        slot = run % WEIGHT_SLOTS
        return (pltpu.make_async_copy(wg_hbm.at[layer, expert], stg_g.at[slot], sems.at[slot]),
                pltpu.make_async_copy(wu_hbm.at[layer, expert], stg_u.at[slot], sems.at[slot]),
                pltpu.make_async_copy(wd_hbm.at[layer, expert], stg_d.at[slot], sems.at[slot]))

    @pl.when(b == 0)
    def _():
        for r in range(WEIGHT_SLOTS - 1):
            @pl.when(r < n_runs)
            def _():
                for cp in fetch(r):
                    cp.start()

    @pl.when(changed)
    def _():
        run = jnp.where(b == 0, 0, run_s[0] + 1)
        run_s[0] = run
        for cp in fetch(run):
            cp.wait()

        ahead = run + WEIGHT_SLOTS - 1

        @pl.when(ahead < n_runs)
        def _():
            for cp in fetch(ahead):
                cp.start()

        slot = run % WEIGHT_SLOTS
        wgu_s[:, 0:hid] = stg_g[slot].astype(BF16)
        wgu_s[:, hid:2 * hid] = stg_u[slot].astype(BF16)
        wd_s[...] = stg_d[slot].astype(BF16)

    bm, dp = x_ref.shape
    nv = nv_ref[b]
    in_use = b < nu_ref[0]

    def run(rows):
        live = lax.broadcasted_iota(jnp.int32, (rows, dp), 0) < nv
        x = _unpack_rows(jnp.where(live, x_ref[0:rows, :], jnp.uint32(0)))
        gu = jnp.dot(x.astype(BF16), wgu_s[...], preferred_element_type=F32)
        gate = gu[:, 0:hid]
        act = gate * (1.0 / (1.0 + jnp.exp(-gate))) * gu[:, hid:2 * hid]
        y_ref[0:rows, :] = _pack_rows(jnp.dot(act.astype(BF16), wd_s[...], preferred_element_type=F32))

    n_quanta = bm // MOE_QUANTUM
    for q in range(1, n_quanta + 1):
        rows = q * MOE_QUANTUM

        @pl.when(jnp.logical_and(in_use, jnp.logical_and(nv > rows - MOE_QUANTUM, nv <= rows)))
        def _(rows=rows):
            run(rows)
            if rows < bm:
                y_ref[rows:bm, :] = jnp.zeros((bm - rows, dp), y_ref.dtype)

    @pl.when(jnp.logical_not(in_use))
    def _():
        y_ref[...] = jnp.zeros_like(y_ref)


def _experts(block_e, runs, n_valid, n_used, xb, w_gate, w_up, w_down, layer):
    n_rows, dp = xb.shape
    d, hid = w_gate.shape[2], w_gate.shape[3]
    bm = MOE_BM
    n_blocks = n_rows // bm
    hbm = pl.BlockSpec(memory_space=pl.ANY)
    return pl.pallas_call(
        functools.partial(_expert_kernel, layer=layer),
        grid_spec=pltpu.PrefetchScalarGridSpec(
            num_scalar_prefetch=4,
            grid=(n_blocks,),
            in_specs=[
                pl.BlockSpec((bm, dp), lambda b, be, nx, nv, nu: (jnp.minimum(b, nu[0] - 1), 0)),
                hbm, hbm, hbm,
            ],
            out_specs=pl.BlockSpec((bm, dp), lambda b, be, nx, nv, nu: (b, 0)),
            scratch_shapes=[
                pltpu.VMEM((d, 2 * hid), BF16), pltpu.VMEM((hid, d), BF16),
                pltpu.VMEM((WEIGHT_SLOTS, d, hid), F32), pltpu.VMEM((WEIGHT_SLOTS, d, hid), F32),
                pltpu.VMEM((WEIGHT_SLOTS, hid, d), F32),
                pltpu.SMEM((1,), jnp.int32), pltpu.SemaphoreType.DMA((WEIGHT_SLOTS,)),
            ],
        ),
        out_shape=jax.ShapeDtypeStruct((n_rows, dp), jnp.uint32),
        compiler_params=_params(("arbitrary",)),
        name="experts",
    )(block_e, runs, n_valid, n_used, xb, w_gate, w_up, w_down)


def _combine_kernel(lat_ref, y0_ref, y1_ref, wc_ref, gate_ref, gf_ref, *rest):
    o_ref = rest[-1]
    wc = wc_ref[...]
    moe = wc[:, 0:1] * _unpack_rows(y0_ref[...]) + wc[:, 1:2] * _unpack_rows(y1_ref[...])
    out = lat_ref[...] + gate_ref[...] * moe
    ms = jnp.mean(out * out, axis=-1, keepdims=True)
    o_ref[...] = out * lax.rsqrt(ms + EPS) * gf_ref[...]


def _combine(lat, yg, wcol, mods, layer, batch0, g_final, seq, out_rows, tok0, prev_out):
    n, d = lat.shape
    tm = COMBINE_TM
    nt = n // tm
    t0 = tok0 // tm
    tiles_per_seq = seq // tm
    row = lambda i: layer * MOD_ROWS + batch0 + i // tiles_per_seq
    in_specs = [
        pl.BlockSpec((tm, d), lambda i: (i, 0)),
        pl.BlockSpec((tm, d // 2), lambda i: (i, 0)),
        pl.BlockSpec((tm, d // 2), lambda i: (nt + i, 0)),
        pl.BlockSpec((tm, LANES), lambda i: (i, 0)),
        pl.BlockSpec((None, 1, d), lambda i: (row(i), 0, 5)),
        pl.BlockSpec((1, d), lambda i: (0, 0)),
    ]
    args = [lat, yg, yg, wcol, mods, g_final]
    aliases = {}
    if prev_out is not None:
        in_specs.append(pl.BlockSpec(memory_space=pl.ANY))
        args.append(prev_out)
        aliases = {len(args) - 1: 0}
    return pl.pallas_call(
        _combine_kernel,
        grid=(nt,),
        in_specs=in_specs,
        out_specs=pl.BlockSpec((tm, d), lambda i: (t0 + i, 0)),
        out_shape=jax.ShapeDtypeStruct((out_rows, d), F32),
        input_output_aliases=aliases,
        compiler_params=_params(("parallel",)),
        name="combine",
    )(*args)


def _moe_rows(routed, layer, w_gate, w_up, w_down):
    h2, meta_i, wcol, counts = routed
    n = h2.shape[0]
    n_blocks = (2 * n) // MOE_BM + N_EXPERTS
    dest, block_e, runs, n_valid, n_used = _plan(counts[:, 0].astype(jnp.int32), meta_i, n_blocks)
    xb = _sc_dispatch(h2, dest, n_blocks * MOE_BM)
    yb = _experts(block_e, runs, n_valid, n_used, xb, w_gate, w_up, w_down, layer)
    return _sc_gather(yb, dest), wcol


def _rope_tables(seq):
    quarter = HEAD_DIM // 4
    pos = jnp.arange(seq, dtype=F32)
    row_ids = jnp.floor(pos / GRID_W)
    col_ids = pos - row_ids * GRID_W
    inv = ROPE_BASE ** (-jnp.arange(quarter, dtype=F32) / quarter)
    ang_r = row_ids[:, None] * inv
    ang_c = col_ids[:, None] * inv
    zero = jnp.zeros_like(ang_r)
    cos = jnp.concatenate([jnp.cos(ang_r), jnp.cos(ang_r), jnp.cos(ang_c), jnp.cos(ang_c)], axis=-1)
    sa = jnp.concatenate([-jnp.sin(ang_r), zero, -jnp.sin(ang_c), zero], axis=-1)
    sb = jnp.concatenate([zero, jnp.sin(ang_r), zero, jnp.sin(ang_c)], axis=-1)
    rep = LANES // HEAD_DIM
    return tuple(jnp.tile(t, (1, rep)) for t in (cos, sa, sb))


def _router_weights(w_rg, b_rg, w_re, b_re):
    d = w_rg.shape[0]
    pad = EXPERTS_PER_GROUP - N_GROUPS
    wr_t = jnp.concatenate([w_rg.T, jnp.zeros((pad, d), F32), w_re.T], axis=0)
    br_t = jnp.concatenate([b_rg, jnp.full((pad,), NEG_INF, F32), b_re])[:, None]
    return wr_t, br_t


def kernel(x, c, ctx, c_ctx, w_ada, b_ada, g_norm1, g_norm2, g_final, w_in_even, attn_sink, g_sgu,
           w_spatial, b_spatial, w_out_even, w_in_odd, conv_w, w_out_odd, w_router_group,
           b_router_group, w_router_expert, b_router_expert, w_gate, w_up, w_down):
    b, s, d = x.shape
    n = b * s
    n_ctx = ctx.shape[1]
    depth = w_ada.shape[0]
    assert depth == 2 and b + 1 <= MOD_ROWS

    cond = jnp.concatenate([c, c_ctx[None, :], jnp.zeros((MOD_ROWS - b - 1, d), F32)], axis=0)
    mods = _ada(cond, w_ada, b_ada).reshape(depth * MOD_ROWS, 1, 6 * d)
    gf = g_final[None, :]

    lat = x.reshape(n, d)
    w_in_bf = w_in_even[0].astype(BF16)
    tabs = _rope_tables(s)
    qx, k, v, u, z = _even_in(lat, mods, 0, lambda i: i // (s // EVEN_TM), g_norm1[0][None, :], w_in_bf,
                              tabs, s // EVEN_TM, EVEN_TM)
    ones = jnp.ones((n_ctx, LANES), F32)
    zeros = jnp.zeros((n_ctx, LANES), F32)
    _, kc, vc, _, _ = _even_in(ctx.reshape(b * n_ctx, d), mods, 0, lambda i: b, g_norm1[0][None, :],
                               w_in_bf, (ones, zeros, zeros), 1, n_ctx)
    bsp_full = jnp.repeat(b_spatial[0].T, HEAD_DIM, axis=1)
    half = N_Q_HEADS // 2
    w_att = w_out_even[0][:Q_DIM].reshape(2, half, HEAD_DIM, d).transpose(1, 0, 2, 3).reshape(Q_DIM, d)
    w_out_bf = jnp.concatenate([w_att, w_out_even[0][Q_DIM:]], axis=0).astype(BF16)
    conv_w8 = jnp.concatenate([conv_w[0], jnp.zeros((8 - conv_w.shape[1], d), F32)], axis=0)
    w_in_odd_bf = w_in_odd[0].astype(BF16)
    w_out_odd_bf = w_out_odd[0].astype(BF16)
    wsp_bf = w_spatial[0].astype(BF16)
    wr0, br0 = _router_weights(w_router_group[0], b_router_group[0], w_router_expert[0], b_router_expert[0])
    wr1, br1 = _router_weights(w_router_group[1], b_router_group[1], w_router_expert[1], b_router_expert[1])

    part = n // MOE_PARTS
    out = None
    for p in range(MOE_PARTS):
        tok0 = p * part
        batch0 = tok0 // s
        lat_p, *routed = _even_mix(lat, qx, k, v, kc, vc, u, z, attn_sink[0], g_sgu[0][None, :],
                                   wsp_bf, bsp_full, w_out_bf, mods, s, n_ctx,
                                   g_norm2[0][None, :], wr0, br0, tok0, part)
        moe_rows = _moe_rows(routed, 0, w_gate, w_up, w_down)
        lat_p, *routed = _odd_mix(lat_p, moe_rows, mods, 1, g_norm1[1][None, :], w_in_odd_bf, conv_w8,
                                  w_out_odd_bf, s, g_norm2[1][None, :], wr1, br1, batch0)
        yg, wcol = _moe_rows(routed, 1, w_gate, w_up, w_down)
        out = _combine(lat_p, yg, wcol, mods, 1, batch0, gf, s, n, tok0, out)
    return out.reshape(b, s, d)
```

```python
import functools

import jax
import jax.numpy as jnp
from jax import lax
from jax.experimental import pallas as pl
from jax.experimental.pallas import tpu as pltpu
from jax.experimental.pallas import tpu_sc as plsc

F32 = jnp.float32
BF16 = jnp.bfloat16
HIGHEST = lax.Precision.HIGHEST

GRID_W = 64
N_Q_HEADS = 8
N_KV_HEADS = 2
HEAD_DIM = 64
ATT_BLOCK = 128
ROPE_BASE = 10000.0
Q_DIM = N_Q_HEADS * HEAD_DIM
KV_DIM = N_KV_HEADS * HEAD_DIM
SG_GROUPS = 8
SG_WIDTH = SG_GROUPS * HEAD_DIM
N_GROUPS = 4
EXPERTS_PER_GROUP = 8
N_EXPERTS = N_GROUPS * EXPERTS_PER_GROUP
EPS = 1e-6
NEG_INF = -1e30

LANES = 128
SC_CORES = 2
SC_SUBCORES = 16
SC_WORKERS = SC_CORES * SC_SUBCORES
SC_CHUNK = 32
MOD_ROWS = 8
ROUTER_ROWS = EXPERTS_PER_GROUP + N_EXPERTS
VMEM_LIMIT = 56 * 1024 * 1024

ADA_TN = 1536
EVEN_TM = 1024
EVEN_SUB = 512
ATT_TQ = 1024
ODD_TM = 512
ODD_SUB = 256
ROUTE_SUB = 256
MOE_BM = 1024
MOE_QUANTUM = 128
MOE_PARTS = 2
WEIGHT_SLOTS = 3
COMBINE_TM = 512


def _params(sem):
    return pltpu.CompilerParams(dimension_semantics=sem, vmem_limit_bytes=VMEM_LIMIT)


def _rms_mod(x, g, shift, scale):
    ms = jnp.mean(x * x, axis=-1, keepdims=True)
    return (x * lax.rsqrt(ms + EPS)) * (g * (1.0 + scale)) + shift


def _pack_rounded(a):
    w = a.shape[1] // 2
    hi = pltpu.bitcast(a[:, :w], jnp.uint32)
    lo = pltpu.bitcast(a[:, w:], jnp.uint32)
    return hi | (lo >> 16)


def _pack_rows(a):
    return _pack_rounded(a.astype(BF16).astype(F32))


def _unpack_rows(p):
    hi = pltpu.bitcast(p & jnp.uint32(0xFFFF0000), F32)
    lo = pltpu.bitcast(p << 16, F32)
    return jnp.concatenate([hi, lo], axis=1)


def _ada_kernel(a_ref, w_ref, b_ref, o_ref):
    a = a_ref[...]
    s = a * (1.0 / (1.0 + jnp.exp(-a)))
    o_ref[0] = jnp.dot(s, w_ref[0], preferred_element_type=F32, precision=HIGHEST) + b_ref[0]


def _ada(cond, w_ada, b_ada):
    depth, d, six_d = w_ada.shape
    return pl.pallas_call(
        _ada_kernel,
        grid=(depth, six_d // ADA_TN),
        in_specs=[
            pl.BlockSpec((MOD_ROWS, d), lambda l, j: (0, 0)),
            pl.BlockSpec((1, d, ADA_TN), lambda l, j: (l, 0, j)),
            pl.BlockSpec((1, 1, ADA_TN), lambda l, j: (l, 0, j)),
        ],
        out_specs=pl.BlockSpec((1, MOD_ROWS, ADA_TN), lambda l, j: (l, 0, j)),
        out_shape=jax.ShapeDtypeStruct((depth, MOD_ROWS, six_d), F32),
        compiler_params=_params(("arbitrary", "arbitrary")),
        name="ada",
    )(cond, w_ada, b_ada.reshape(depth, 1, six_d))


def _even_in_kernel(x_ref, sh_ref, sc_ref, g_ref, w_ref, cos_ref, sa_ref, sb_ref,
                    qx_ref, k_ref, v_ref, u_ref, z_ref):
    tm = x_ref.shape[0]
    sub = min(tm, EVEN_SUB)
    scale = HEAD_DIM ** -0.5
    low = lax.broadcasted_iota(jnp.int32, (sub, LANES), 1) < HEAD_DIM
    heads_per_kv = N_Q_HEADS // N_KV_HEADS
    u0 = Q_DIM + 2 * KV_DIM

    for r0 in range(0, tm, sub):
        rows = slice(r0, r0 + sub)
        h = _rms_mod(x_ref[rows, :], g_ref[...], sh_ref[...], sc_ref[...])
        p = jnp.dot(h.astype(BF16), w_ref[...], preferred_element_type=F32)
        cos, sa, sb = cos_ref[rows, :], sa_ref[rows, :], sb_ref[rows, :]

        def rope(t):
            return t * cos + pltpu.roll(t, LANES - 16, 1) * sa + pltpu.roll(t, 16, 1) * sb

        for cblk in range(Q_DIM // LANES):
            t = rope(p[:, cblk * LANES:(cblk + 1) * LANES]) * scale
            sw = pltpu.roll(t, HEAD_DIM, 1)
            zero = jnp.zeros_like(t)
            if (2 * cblk) // heads_per_kv == 0:
                first, second = jnp.where(low, t, zero), jnp.where(low, sw, zero)
            else:
                first, second = jnp.where(low, zero, sw), jnp.where(low, zero, t)
            qx_ref[rows, (2 * cblk) * LANES:(2 * cblk + 1) * LANES] = first.astype(BF16)
            qx_ref[rows, (2 * cblk + 1) * LANES:(2 * cblk + 2) * LANES] = second.astype(BF16)

        k_ref[rows, :] = rope(p[:, Q_DIM:Q_DIM + KV_DIM]).astype(BF16)
        v_ref[rows, :] = p[:, Q_DIM + KV_DIM:Q_DIM + 2 * KV_DIM].astype(BF16)
        u_ref[rows, :] = p[:, u0:u0 + SG_WIDTH]
        z_ref[rows, :] = p[:, u0 + SG_WIDTH:u0 + 2 * SG_WIDTH]


def _even_in(x2d, mods, layer, mod_row_fn, g, w_bf, tabs, tab_blocks, tm):
    n, d = x2d.shape
    ein = w_bf.shape[1]
    cos, sa, sb = tabs
    row = lambda i: layer * MOD_ROWS + mod_row_fn(i)
    tab_spec = pl.BlockSpec((tm, LANES), lambda i: (i % tab_blocks, 0))
    qx_dim = N_Q_HEADS * LANES
    return pl.pallas_call(
        _even_in_kernel,
        grid=(n // tm,),
        in_specs=[
            pl.BlockSpec((tm, d), lambda i: (i, 0)),
            pl.BlockSpec((None, 1, d), lambda i: (row(i), 0, 0)),
            pl.BlockSpec((None, 1, d), lambda i: (row(i), 0, 1)),
            pl.BlockSpec((1, d), lambda i: (0, 0)),
            pl.BlockSpec((d, ein), lambda i: (0, 0)),
            tab_spec, tab_spec, tab_spec,
        ],
        out_specs=[
            pl.BlockSpec((tm, qx_dim), lambda i: (i, 0)),
            pl.BlockSpec((tm, KV_DIM), lambda i: (i, 0)),
            pl.BlockSpec((tm, KV_DIM), lambda i: (i, 0)),
            pl.BlockSpec((tm, SG_WIDTH), lambda i: (i, 0)),
            pl.BlockSpec((tm, SG_WIDTH), lambda i: (i, 0)),
        ],
        out_shape=[
            jax.ShapeDtypeStruct((n, qx_dim), BF16),
            jax.ShapeDtypeStruct((n, KV_DIM), BF16),
            jax.ShapeDtypeStruct((n, KV_DIM), BF16),
            jax.ShapeDtypeStruct((n, SG_WIDTH), F32),
            jax.ShapeDtypeStruct((n, SG_WIDTH), F32),
        ],
        compiler_params=_params(("parallel",)),
        name="even_in",
    )(x2d, mods, mods, g, w_bf, cos, sa, sb)


def _gelu(x):
    return 0.5 * x * (1.0 + lax.erf(x * (2.0 ** -0.5)))


def _even_mix_kernel(sink_ref, lat_ref, qx_ref, km_ref, kp_ref, kn_ref, vm_ref, vp_ref, vn_ref,
                     kc_ref, vc_ref, u_ref, z_ref, gsgu_ref, wsp_ref, bsp_ref, wout_ref, gate_ref,
                     sh2_ref, sc2_ref, g2_ref, wr_ref, br_ref,
                     o_ref, h_ref, mi_ref, wc_ref, cnt_ref,
                     kband, vband, mixin, carry, upper, *, tiles_per_seq):
    i = pl.program_id(0)
    tq = qx_ref.shape[0]

    @pl.when(i == 0)
    def _():
        _route_init(carry, upper)

    blk = ATT_BLOCK
    nsub = tq // blk
    n_ctx = kc_ref.shape[0]
    first = (i % tiles_per_seq) == 0
    last = (i % tiles_per_seq) == tiles_per_seq - 1

    kband[0:blk] = kp_ref[...]
    kband[blk:blk + tq] = km_ref[...]
    kband[blk + tq:] = kn_ref[...]
    vband[0:blk] = vp_ref[...]
    vband[blk:blk + tq] = vm_ref[...]
    vband[blk + tq:] = vn_ref[...]

    rows = N_Q_HEADS * blk
    tok = lax.broadcasted_iota(jnp.int32, (rows, blk), 0) & (blk - 1)
    col = lax.broadcasted_iota(jnp.int32, (rows, blk), 1)
    tri_prev = col >= tok
    tri_next = col <= tok
    head = lax.broadcasted_iota(jnp.int32, (rows, 1), 0) // blk
    sink_col = jnp.zeros((rows, 1), F32)
    for hd in range(N_Q_HEADS):
        sink_col = jnp.where(head == hd, sink_ref[hd], sink_col)
    lane_low = lax.broadcasted_iota(jnp.int32, (blk, LANES), 1) < HEAD_DIM
    ones = jnp.ones((n_ctx + 3 * blk, LANES), BF16)
    nt = (((1,), (1,)), ((), ()))

    def sub_block(j, c):
        r0 = pl.multiple_of(j * blk, blk)
        ok_prev = jnp.logical_not(jnp.logical_and(first, j == 0))
        ok_next = jnp.logical_not(jnp.logical_and(last, j == nsub - 1))
        qs = jnp.concatenate([qx_ref[pl.ds(r0, blk), hd * LANES:(hd + 1) * LANES]
                              for hd in range(N_Q_HEADS)], axis=0)
        kall = jnp.concatenate([kc_ref[...], kband[pl.ds(r0, 3 * blk), :]], axis=0)
        vall = jnp.concatenate([vc_ref[...], vband[pl.ds(r0, 3 * blk), :]], axis=0)
        s = lax.dot_general(qs, kall, nt, preferred_element_type=F32)
        c0 = n_ctx
        s = jnp.concatenate([
            s[:, :c0],
            jnp.where(jnp.logical_and(tri_prev, ok_prev), s[:, c0:c0 + blk], NEG_INF),
            s[:, c0 + blk:c0 + 2 * blk],
            jnp.where(jnp.logical_and(tri_next, ok_next), s[:, c0 + 2 * blk:], NEG_INF),
        ], axis=1)
        m = jnp.maximum(jnp.max(s, axis=-1, keepdims=True), sink_col)
        p = jnp.exp(s - m).astype(BF16)
        o = jnp.dot(p, jnp.concatenate([vall, ones], axis=1), preferred_element_type=F32)
        att = o[:, :LANES] / (o[:, LANES:] + jnp.exp(sink_col - m))
        half = N_Q_HEADS // 2
        for hd in range(half):
            pair = jnp.where(lane_low, att[hd * blk:(hd + 1) * blk], att[(hd + half) * blk:(hd + half + 1) * blk])
            mixin[pl.ds(r0, blk), hd * LANES:(hd + 1) * LANES] = pair.astype(BF16)

        ug = _gelu(u_ref[pl.ds(r0, blk), :])
        zg = _gelu(z_ref[pl.ds(r0, blk), :])
        mu = jnp.mean(zg, axis=-1, keepdims=True)
        zc = zg - mu
        zn = zc * lax.rsqrt(jnp.mean(zc * zc, axis=-1, keepdims=True) + EPS) * gsgu_ref[...]
        for pair in range(SG_GROUPS // 2):
            zp = zn[:, pair * LANES:(pair + 1) * LANES]
            zero = jnp.zeros_like(zp)
            lo = jnp.where(lane_low, zp, zero).astype(BF16)
            hi = jnp.where(lane_low, zero, zp).astype(BF16)
            sg = (jnp.dot(wsp_ref[pair], jnp.concatenate([lo, hi], axis=0), preferred_element_type=F32)
                  + bsp_ref[:, pair * LANES:(pair + 1) * LANES])
            mixin[pl.ds(r0, blk), Q_DIM + pair * LANES:Q_DIM + (pair + 1) * LANES] = (
                ug[:, pair * LANES:(pair + 1) * LANES] * sg).astype(BF16)
        return c

    lax.fori_loop(0, nsub, sub_block, 0, unroll=True)
    for r0 in range(0, tq, ROUTE_SUB):
        rows = slice(r0, r0 + ROUTE_SUB)
        mix = jnp.dot(mixin[rows, :], wout_ref[...], preferred_element_type=F32)
        lat = lat_ref[rows, :] + gate_ref[...] * mix
        o_ref[rows, :] = lat
        _route_tile(lat, sh2_ref[...], sc2_ref[...], g2_ref[...], wr_ref, br_ref, carry, upper,
                    h_ref.at[rows, :], mi_ref.at[:, rows], wc_ref.at[rows, :], cnt_ref)


def _even_mix(lat, qx, k, v, kc, vc, u, z, sink, g_sgu, wsp_bf, bsp_full, wout_bf, mods, seq, ctx_len,
              g2, wr_t, br_t, tok0, n):
    d = lat.shape[1]
    tq = ATT_TQ
    tiles_per_seq = seq // tq
    sub = tq // ATT_BLOCK
    nblk = lat.shape[0] // ATT_BLOCK
    t0 = tok0 // tq
    main = lambda w: pl.BlockSpec((tq, w), lambda i: (t0 + i, 0))
    prev = pl.BlockSpec((ATT_BLOCK, KV_DIM), lambda i: (jnp.maximum((t0 + i) * sub - 1, 0), 0))
    nxt = pl.BlockSpec((ATT_BLOCK, KV_DIM), lambda i: (jnp.minimum((t0 + i + 1) * sub, nblk - 1), 0))
    batch = lambda t: (t0 + t) // tiles_per_seq
    ctxs = pl.BlockSpec((ctx_len, KV_DIM), lambda i: (batch(i), 0))
    const = lambda shape: pl.BlockSpec(shape, lambda i: (0,) * len(shape), pipeline_mode=pl.Buffered(1))
    r_in, r_out, r_shapes, r_scratch = _route_specs(n, d, tq, lambda i: i, batch, ROUTE_SUB)
    return pl.pallas_call(
        functools.partial(_even_mix_kernel, tiles_per_seq=tiles_per_seq),
        grid=(n // tq,),
        in_specs=[
            pl.BlockSpec(memory_space=pltpu.SMEM),
            main(d), main(qx.shape[1]),
            main(KV_DIM), prev, nxt,
            main(KV_DIM), prev, nxt,
            ctxs, ctxs,
            main(SG_WIDTH), main(SG_WIDTH),
            const((1, SG_WIDTH)), const(wsp_bf.shape), const(bsp_full.shape), const(wout_bf.shape),
            pl.BlockSpec((None, 1, d), lambda i: (batch(i), 0, 2)),
        ] + r_in,
        out_specs=[pl.BlockSpec((tq, d), lambda i: (i, 0))] + r_out,
        out_shape=[jax.ShapeDtypeStruct((n, d), F32)] + r_shapes,
        scratch_shapes=[
            pltpu.VMEM((tq + 2 * ATT_BLOCK, KV_DIM), BF16),
            pltpu.VMEM((tq + 2 * ATT_BLOCK, KV_DIM), BF16),
            pltpu.VMEM((tq, Q_DIM + SG_WIDTH), BF16),
        ] + r_scratch,
        compiler_params=_params(("arbitrary",)),
        name="even_mix",
    )(sink, lat, qx, k, k, k, v, v, v, kc, vc, u, z, g_sgu, wsp_bf, bsp_full, wout_bf, mods,
      mods, mods, g2, wr_t, br_t)


def _odd_mix_kernel(x_ref, e0_ref, e1_ref, ew_ref, egate_ref, sh_ref, sc_ref, gate_ref, g_ref,
                    win_ref, cw_ref, wout_ref, sh2_ref, sc2_ref, g2_ref, wr_ref, br_ref,
                    o_ref, h_ref, mi_ref, wc_ref, cnt_ref,
                    y_s, bg_s, tail_s, x_s, carry, upper, *, tiles_per_seq):
    i = pl.program_id(0)
    n_tiles = pl.num_programs(0) - 1
    tm, d = x_ref.shape
    cur = i % 2
    prv = 1 - cur

    @pl.when(i == 0)
    def _():
        _route_init(carry, upper)

    tail_s[...] = y_s[cur, tm - 8:tm, :]

    @pl.when(i < n_tiles)
    def _():
        for r0 in range(0, tm, ODD_SUB):
            rows = slice(r0, r0 + ODD_SUB)
            ew = ew_ref[rows, :]
            moe = ew[:, 0:1] * _unpack_rows(e0_ref[rows, :]) + ew[:, 1:2] * _unpack_rows(e1_ref[rows, :])
            x = x_ref[rows, :] + egate_ref[...] * moe
            x_s[cur, rows, :] = x
            h = _rms_mod(x, g_ref[...], sh_ref[...], sc_ref[...])
            p = jnp.dot(h.astype(BF16), win_ref[...], preferred_element_type=F32)
            bg_s[cur, rows, :] = p[:, 0:d]
            y_s[cur, rows, :] = p[:, d:2 * d] * p[:, 2 * d:3 * d]

    @pl.when(i >= 1)
    def _():
        t_prev = i - 1
        first = (t_prev % tiles_per_seq) == 0
        last = (t_prev % tiles_per_seq) == tiles_per_seq - 1
        y = y_s[prv]
        left = jnp.where(first, 0.0, tail_s[7:8, :])
        right = jnp.where(last, 0.0, y_s[cur, 0:1, :])
        ridx = lax.broadcasted_iota(jnp.int32, (tm, d), 0)
        y_dn = jnp.where(ridx == 0, left, pltpu.roll(y, 1, 0))
        y_up = jnp.where(ridx == tm - 1, right, pltpu.roll(y, tm - 1, 0))
        conv = y_dn * cw_ref[0:1, :] + y * cw_ref[1:2, :] + y_up * cw_ref[2:3, :]
        mix = jnp.dot((bg_s[prv] * conv).astype(BF16), wout_ref[...], preferred_element_type=F32)
        lat = x_s[prv] + gate_ref[...] * mix
        o_ref[...] = lat
        _route_tile(lat, sh2_ref[...], sc2_ref[...], g2_ref[...], wr_ref, br_ref, carry, upper,
                    h_ref, mi_ref, wc_ref, cnt_ref)


def _odd_mix(lat, moe_rows, mods, layer, g, win_bf, conv_w8, wout_bf, seq, g2, wr_t, br_t, batch0):
    yg, ewcol = moe_rows
    n, d = lat.shape
    tm = ODD_TM
    nt = n // tm
    tiles_per_seq = seq // tm
    cur = lambda i: jnp.minimum(i, nt - 1)
    prv = lambda i: jnp.maximum(i - 1, 0)
    row = lambda t: layer * MOD_ROWS + batch0 + t // tiles_per_seq
    const = lambda shape: pl.BlockSpec(shape, lambda i: (0,) * len(shape), pipeline_mode=pl.Buffered(1))
    r_in, r_out, r_shapes, r_scratch = _route_specs(n, d, tm, prv, row, tm)
    return pl.pallas_call(
        functools.partial(_odd_mix_kernel, tiles_per_seq=tiles_per_seq),
        grid=(nt + 1,),
        in_specs=[
            pl.BlockSpec((tm, d), lambda i: (cur(i), 0)),
            pl.BlockSpec((tm, d // 2), lambda i: (cur(i), 0)),
            pl.BlockSpec((tm, d // 2), lambda i: (nt + cur(i), 0)),
            pl.BlockSpec((tm, LANES), lambda i: (cur(i), 0)),
            pl.BlockSpec((None, 1, d), lambda i: (row(cur(i)) - MOD_ROWS, 0, 5)),
            pl.BlockSpec((None, 1, d), lambda i: (row(cur(i)), 0, 0)),
            pl.BlockSpec((None, 1, d), lambda i: (row(cur(i)), 0, 1)),
            pl.BlockSpec((None, 1, d), lambda i: (row(prv(i)), 0, 2)),
            const((1, d)), const(win_bf.shape), const(conv_w8.shape), const(wout_bf.shape),
        ] + r_in,
        out_specs=[pl.BlockSpec((tm, d), lambda i: (prv(i), 0))] + r_out,
        out_shape=[jax.ShapeDtypeStruct((n, d), F32)] + r_shapes,
        scratch_shapes=[
            pltpu.VMEM((2, tm, d), F32),
            pltpu.VMEM((2, tm, d), F32),
            pltpu.VMEM((8, d), F32),
            pltpu.VMEM((2, tm, d), F32),
        ] + r_scratch,
        compiler_params=_params(("arbitrary",)),
        name="odd_mix",
    )(lat, yg, yg, ewcol, mods, mods, mods, mods, g, win_bf, conv_w8, wout_bf, mods, mods, g2, wr_t, br_t)


def _route_init(carry, upper):
    tm = upper.shape[0]
    carry[...] = jnp.zeros_like(carry)
    r_i = lax.broadcasted_iota(jnp.int32, (tm, tm), 0)
    c_i = lax.broadcasted_iota(jnp.int32, (tm, tm), 1)
    upper[...] = jnp.where(r_i < c_i, 1.0, 0.0).astype(BF16)


def _route_tile(lat, sh, sc, g, wr_ref, br_ref, carry, upper, h_ref, mi_ref, wc_ref, cnt_ref):
    tm = lat.shape[0]
    epg = EXPERTS_PER_GROUP
    h = _rms_mod(lat, g, sh, sc)
    h_hi = h.astype(BF16)
    h_hi_f = h_hi.astype(F32)
    h_ref[...] = _pack_rounded(h_hi_f)
    h_lo = (h - h_hi_f).astype(BF16)
    w = wr_ref[...]
    w1 = w.astype(BF16).astype(F32)
    r1 = w - w1
    w2 = r1.astype(BF16).astype(F32)
    w3 = r1 - w2
    nt = (((1,), (1,)), ((), ()))
    nr = w.shape[0]
    w123 = jnp.concatenate([w1, w2, w3, jnp.zeros((8, w.shape[1]), F32)], axis=0).astype(BF16)
    w12 = jnp.concatenate([w1, w2], axis=0).astype(BF16)
    p_hi = lax.dot_general(w123, h_hi, nt, preferred_element_type=F32)
    p_lo = lax.dot_general(w12, h_lo, nt, preferred_element_type=F32)
    lg = ((p_hi[2 * nr:3 * nr] + p_lo[nr:2 * nr]) + (p_hi[nr:2 * nr] + p_lo[0:nr])) + p_hi[0:nr] + br_ref[...]
    io8 = lax.broadcasted_iota(jnp.int32, (epg, tm), 0)
    gl = lg[0:epg]
    gmax = jnp.max(gl, axis=0, keepdims=True)
    g_idx = jnp.min(jnp.where(gl == gmax, io8, epg), axis=0, keepdims=True)
    g_w = 1.0 / jnp.sum(jnp.exp(gl - gmax), axis=0, keepdims=True)
    e_sel = lg[epg:2 * epg]
    for gi in range(1, N_GROUPS):
        e_sel = jnp.where(g_idx == gi, lg[(gi + 1) * epg:(gi + 2) * epg], e_sel)
    v0 = jnp.max(e_sel, axis=0, keepdims=True)
    i0 = jnp.min(jnp.where(e_sel == v0, io8, epg), axis=0, keepdims=True)
    rest = jnp.where(io8 == i0, -jnp.inf, e_sel)
    v1 = jnp.max(rest, axis=0, keepdims=True)
    i1 = jnp.min(jnp.where(rest == v1, io8, epg), axis=0, keepdims=True)
    t = jnp.exp(v1 - v0)
    w0 = g_w / (1.0 + t)
    w1 = g_w * t / (1.0 + t)
    e0 = g_idx * epg + i0
    e1 = g_idx * epg + i1

    io32 = lax.broadcasted_iota(jnp.int32, (N_EXPERTS, tm), 0)
    hit0 = io32 == e0
    hit1 = io32 == e1
    onehot = jnp.where(hit0 | hit1, 1.0, 0.0)
    cum = jnp.dot(onehot.astype(BF16), upper[...], preferred_element_type=F32) + carry[...]
    rank0 = jnp.sum(jnp.where(hit0, cum, 0.0), axis=0, keepdims=True).astype(jnp.int32)
    rank1 = jnp.sum(jnp.where(hit1, cum, 0.0), axis=0, keepdims=True).astype(jnp.int32)
    carry[...] = carry[...] + jnp.sum(onehot, axis=1, keepdims=True)
    cnt_ref[...] = jnp.broadcast_to(carry[...], cnt_ref.shape)

    mi_ref[...] = jnp.where(io8 == 0, e0, jnp.where(io8 == 1, e1, jnp.where(io8 == 2, rank0,
                            jnp.where(io8 == 3, rank1, 0))))
    io128 = lax.broadcasted_iota(jnp.int32, (LANES, tm), 0)
    wrow = jnp.where(io128 == 0, w0, jnp.where(io128 == 1, w1, 0.0))
    wc_ref[...] = wrow.T


def _route_specs(n, d, tm, tile_of, mods_row, route_rows):
    in_specs = [
        pl.BlockSpec((None, 1, d), lambda i: (mods_row(tile_of(i)), 0, 3)),
        pl.BlockSpec((None, 1, d), lambda i: (mods_row(tile_of(i)), 0, 4)),
        pl.BlockSpec((1, d), lambda i: (0, 0), pipeline_mode=pl.Buffered(1)),
        pl.BlockSpec((ROUTER_ROWS, d), lambda i: (0, 0), pipeline_mode=pl.Buffered(1)),
        pl.BlockSpec((ROUTER_ROWS, 1), lambda i: (0, 0), pipeline_mode=pl.Buffered(1)),
    ]
    out_specs = [
        pl.BlockSpec((tm, d // 2), lambda i: (tile_of(i), 0)),
        pl.BlockSpec((8, tm), lambda i: (0, tile_of(i))),
        pl.BlockSpec((tm, LANES), lambda i: (tile_of(i), 0)),
        pl.BlockSpec((N_EXPERTS, LANES), lambda i: (0, 0)),
    ]
    out_shapes = [
        jax.ShapeDtypeStruct((n, d // 2), jnp.uint32),
        jax.ShapeDtypeStruct((8, n), jnp.int32),
        jax.ShapeDtypeStruct((n, LANES), F32),
        jax.ShapeDtypeStruct((N_EXPERTS, LANES), F32),
    ]
    scratch = [pltpu.VMEM((N_EXPERTS, 1), F32), pltpu.VMEM((route_rows, route_rows), BF16)]
    return in_specs, out_specs, out_shapes, scratch


def _plan_kernel(cnt_ref, mi_ref, dest_ref, be_ref, runs_ref, nv_ref, nu_ref, ps_ref, *, n_blocks):
    bm = MOE_BM

    def per_expert(e, carry):
        blk0, n_runs = carry
        cnt = cnt_ref[e]
        nb = (cnt + bm - 1) // bm
        ps_ref[e] = blk0 * bm

        def fill(b, c):
            be_ref[b] = e
            nv_ref[b] = jnp.minimum(cnt - (b - blk0) * bm, bm)
            return c

        lax.fori_loop(blk0, blk0 + nb, fill, 0)

        @pl.when(nb > 0)
        def _():
            runs_ref[n_runs] = e

        return blk0 + nb, n_runs + jnp.where(nb > 0, 1, 0)

    n_used, n_runs = lax.fori_loop(0, N_EXPERTS, per_expert, (0, 0))
    nu_ref[0] = n_used
    nu_ref[1] = n_runs
    last_e = be_ref[jnp.maximum(n_used - 1, 0)]

    def fill_tail(b, c):
        be_ref[b] = last_e
        nv_ref[b] = 0
        return c

    lax.fori_loop(n_used, n_blocks, fill_tail, 0)

    def fill_runs(k, c):
        runs_ref[k] = last_e
        return c

    lax.fori_loop(n_runs, N_EXPERTS, fill_runs, 0)

    e01 = mi_ref[0:2, :]
    dest = mi_ref[2:4, :]
    for e in range(N_EXPERTS):
        dest = dest + jnp.where(e01 == e, ps_ref[e], 0)
    dest_ref[...] = dest


def _plan(counts, meta_i, n_blocks):
    n = meta_i.shape[1]
    return pl.pallas_call(
        functools.partial(_plan_kernel, n_blocks=n_blocks),
        in_specs=[pl.BlockSpec(memory_space=pltpu.SMEM), pl.BlockSpec(memory_space=pltpu.VMEM)],
        out_specs=[pl.BlockSpec(memory_space=pltpu.VMEM)] + [pl.BlockSpec(memory_space=pltpu.SMEM)] * 4,
        out_shape=[
            jax.ShapeDtypeStruct((2, n), jnp.int32),
            jax.ShapeDtypeStruct((n_blocks,), jnp.int32),
            jax.ShapeDtypeStruct((N_EXPERTS,), jnp.int32),
            jax.ShapeDtypeStruct((n_blocks,), jnp.int32),
            jax.ShapeDtypeStruct((2,), jnp.int32),
        ],
        scratch_shapes=[pltpu.SMEM((N_EXPERTS,), jnp.int32)],
        compiler_params=pltpu.CompilerParams(vmem_limit_bytes=VMEM_LIMIT),
        name="plan",
    )(counts, meta_i)


def _sc_mesh():
    return plsc.VectorSubcoreMesh(core_axis_name="c", subcore_axis_name="s",
                                  num_cores=SC_CORES, num_subcores=SC_SUBCORES)


def _sc_worker():
    return lax.axis_index("s") * SC_CORES + lax.axis_index("c")


def _sc_dispatch(h2, dest, n_rows):
    n, d = h2.shape
    c = SC_CHUNK
    per_w = n // SC_WORKERS
    nchunk = per_w // c
    idx = dest.reshape(2, SC_WORKERS, nchunk, c)

    @functools.partial(
        pl.kernel, mesh=_sc_mesh(), out_type=jax.ShapeDtypeStruct((n_rows, d), h2.dtype),
        scratch_types=[pltpu.VMEM((nchunk, c), jnp.int32), pltpu.VMEM((nchunk, c), jnp.int32),
                       pltpu.VMEM((2, c, d), h2.dtype),
                       pltpu.SemaphoreType.DMA((2,)), pltpu.SemaphoreType.DMA((2,))])
    def k(h_hbm, idx_hbm, xb_hbm, idx0_v, idx1_v, rows_v, gsem, ssem):
        wid = _sc_worker()
        base = wid * per_w
        idx_v = (idx0_v, idx1_v)
        for kk in range(2):
            pltpu.sync_copy(idx_hbm.at[kk, wid], idx_v[kk])

        def get(j, slot):
            return pltpu.make_async_copy(h_hbm.at[pl.ds(base + j * c, c)], rows_v.at[slot], gsem.at[slot])

        def put(j, slot, kk):
            return pltpu.make_async_copy(rows_v.at[slot], xb_hbm.at[idx_v[kk].at[j]], ssem.at[slot])

        get(0, 0).start()

        @pl.loop(0, nchunk, step=2)
        def _(j):
            for slot in range(2):
                jj = j + slot
                get(jj, slot).wait()

                @pl.when(jj >= 1)
                def _():
                    for kk in range(2):
                        put(jj - 1, 1 - slot, kk).wait()

                @pl.when(jj + 1 < nchunk)
                def _():
                    get(jj + 1, 1 - slot).start()

                for kk in range(2):
                    put(jj, slot, kk).start()

        for kk in range(2):
            put(nchunk - 1, (nchunk - 1) % 2, kk).wait()

    return k(h2, idx)


def _sc_gather(y, dest):
    d = y.shape[1]
    total = dest.shape[0] * dest.shape[1]
    c = SC_CHUNK
    per_w = total // SC_WORKERS
    nchunk = per_w // c
    idx = dest.reshape(SC_WORKERS, nchunk, c)

    @functools.partial(
        pl.kernel, mesh=_sc_mesh(), out_type=jax.ShapeDtypeStruct((total, d), y.dtype),
        scratch_types=[pltpu.VMEM((nchunk, c), jnp.int32), pltpu.VMEM((2, c, d), y.dtype),
                       pltpu.SemaphoreType.DMA((2,)), pltpu.SemaphoreType.DMA((2,))])
    def k(y_hbm, idx_hbm, out_hbm, idx_v, rows_v, gsem, ssem):
        wid = _sc_worker()
        base = wid * per_w
        pltpu.sync_copy(idx_hbm.at[wid], idx_v)

        def get(j, slot):
            return pltpu.make_async_copy(y_hbm.at[idx_v.at[j]], rows_v.at[slot], gsem.at[slot])

        def put(j, slot):
            return pltpu.make_async_copy(rows_v.at[slot], out_hbm.at[pl.ds(base + j * c, c)], ssem.at[slot])

        get(0, 0).start()

        @pl.loop(0, nchunk, step=2)
        def _(j):
            for slot in range(2):
                jj = j + slot
                get(jj, slot).wait()

                @pl.when(jj >= 1)
                def _():
                    put(jj - 1, 1 - slot).wait()

                @pl.when(jj + 1 < nchunk)
                def _():
                    get(jj + 1, 1 - slot).start()

                put(jj, slot).start()

        put(nchunk - 1, (nchunk - 1) % 2).wait()

    return k(y, idx)


def _expert_kernel(be_ref, runs_ref, nv_ref, nu_ref, x_ref, wg_hbm, wu_hbm, wd_hbm, y_ref,
                   wgu_s, wd_s, stg_g, stg_u, stg_d, run_s, sems, *, layer):
    b = pl.program_id(0)
    hid = stg_g.shape[2]
    e = be_ref[b]
    n_runs = nu_ref[1]
    changed = jnp.logical_or(b == 0, e != be_ref[jnp.maximum(b - 1, 0)])

    def fetch(run):
        expert = runs_ref[run]
        slot = run % WEIGHT_SLOTS
        return (pltpu.make_async_copy(wg_hbm.at[layer, expert], stg_g.at[slot], sems.at[slot]),
                pltpu.make_async_copy(wu_hbm.at[layer, expert], stg_u.at[slot], sems.at[slot]),
                pltpu.make_async_copy(wd_hbm.at[layer, expert], stg_d.at[slot], sems.at[slot]))

    @pl.when(b == 0)
    def _():
        for r in range(WEIGHT_SLOTS - 1):
            @pl.when(r < n_runs)
            def _():
                for cp in fetch(r):
                    cp.start()

    @pl.when(changed)
    def _():
        run = jnp.where(b == 0, 0, run_s[0] + 1)
        run_s[0] = run
        for cp in fetch(run):
            cp.wait()

        ahead = run + WEIGHT_SLOTS - 1

        @pl.when(ahead < n_runs)
        def _():
            for cp in fetch(ahead):
                cp.start()

        slot = run % WEIGHT_SLOTS
        wgu_s[:, 0:hid] = stg_g[slot].astype(BF16)
        wgu_s[:, hid:2 * hid] = stg_u[slot].astype(BF16)
        wd_s[...] = stg_d[slot].astype(BF16)

    bm, dp = x_ref.shape
    nv = nv_ref[b]
    in_use = b < nu_ref[0]

    def run(rows):
        live = lax.broadcasted_iota(jnp.int32, (rows, dp), 0) < nv
        x = _unpack_rows(jnp.where(live, x_ref[0:rows, :], jnp.uint32(0)))
        gu = jnp.dot(x.astype(BF16), wgu_s[...], preferred_element_type=F32)
        gate = gu[:, 0:hid]
        act = gate * (1.0 / (1.0 + jnp.exp(-gate))) * gu[:, hid:2 * hid]
        y_ref[0:rows, :] = _pack_rows(jnp.dot(act.astype(BF16), wd_s[...], preferred_element_type=F32))

    n_quanta = bm // MOE_QUANTUM
    for q in range(1, n_quanta + 1):
        rows = q * MOE_QUANTUM

        @pl.when(jnp.logical_and(in_use, jnp.logical_and(nv > rows - MOE_QUANTUM, nv <= rows)))
        def _(rows=rows):
            run(rows)
            if rows < bm:
                y_ref[rows:bm, :] = jnp.zeros((bm - rows, dp), y_ref.dtype)

    @pl.when(jnp.logical_not(in_use))
    def _():
        y_ref[...] = jnp.zeros_like(y_ref)


def _experts(block_e, runs, n_valid, n_used, xb, w_gate, w_up, w_down, layer):
    n_rows, dp = xb.shape
    d, hid = w_gate.shape[2], w_gate.shape[3]
    bm = MOE_BM
    n_blocks = n_rows // bm
    hbm = pl.BlockSpec(memory_space=pl.ANY)
    return pl.pallas_call(
        functools.partial(_expert_kernel, layer=layer),
        grid_spec=pltpu.PrefetchScalarGridSpec(
            num_scalar_prefetch=4,
            grid=(n_blocks,),
            in_specs=[
                pl.BlockSpec((bm, dp), lambda b, be, nx, nv, nu: (jnp.minimum(b, nu[0] - 1), 0)),
                hbm, hbm, hbm,
            ],
            out_specs=pl.BlockSpec((bm, dp), lambda b, be, nx, nv, nu: (b, 0)),
            scratch_shapes=[
                pltpu.VMEM((d, 2 * hid), BF16), pltpu.VMEM((hid, d), BF16),
                pltpu.VMEM((WEIGHT_SLOTS, d, hid), F32), pltpu.VMEM((WEIGHT_SLOTS, d, hid), F32),
                pltpu.VMEM((WEIGHT_SLOTS, hid, d), F32),
                pltpu.SMEM((1,), jnp.int32), pltpu.SemaphoreType.DMA((WEIGHT_SLOTS,)),
            ],
        ),
        out_shape=jax.ShapeDtypeStruct((n_rows, dp), jnp.uint32),
        compiler_params=_params(("arbitrary",)),
        name="experts",
    )(block_e, runs, n_valid, n_used, xb, w_gate, w_up, w_down)


def _combine_kernel(lat_ref, y0_ref, y1_ref, wc_ref, gate_ref, gf_ref, *rest):
    o_ref = rest[-1]
    wc = wc_ref[...]
    moe = wc[:, 0:1] * _unpack_rows(y0_ref[...]) + wc[:, 1:2] * _unpack_rows(y1_ref[...])
    out = lat_ref[...] + gate_ref[...] * moe
    ms = jnp.mean(out * out, axis=-1, keepdims=True)
    o_ref[...] = out * lax.rsqrt(ms + EPS) * gf_ref[...]


def _combine(lat, yg, wcol, mods, layer, batch0, g_final, seq, out_rows, tok0, prev_out):
    n, d = lat.shape
    tm = COMBINE_TM
    nt = n // tm
    t0 = tok0 // tm
    tiles_per_seq = seq // tm
    row = lambda i: layer * MOD_ROWS + batch0 + i // tiles_per_seq
    in_specs = [
        pl.BlockSpec((tm, d), lambda i: (i, 0)),
        pl.BlockSpec((tm, d // 2), lambda i: (i, 0)),
        pl.BlockSpec((tm, d // 2), lambda i: (nt + i, 0)),
        pl.BlockSpec((tm, LANES), lambda i: (i, 0)),
        pl.BlockSpec((None, 1, d), lambda i: (row(i), 0, 5)),
        pl.BlockSpec((1, d), lambda i: (0, 0)),
    ]
    args = [lat, yg, yg, wcol, mods, g_final]
    aliases = {}
    if prev_out is not None:
        in_specs.append(pl.BlockSpec(memory_space=pl.ANY))
        args.append(prev_out)
        aliases = {len(args) - 1: 0}
    return pl.pallas_call(
        _combine_kernel,
        grid=(nt,),
        in_specs=in_specs,
        out_specs=pl.BlockSpec((tm, d), lambda i: (t0 + i, 0)),
        out_shape=jax.ShapeDtypeStruct((out_rows, d), F32),
        input_output_aliases=aliases,
        compiler_params=_params(("parallel",)),
        name="combine",
    )(*args)


def _moe_rows(routed, layer, w_gate, w_up, w_down):
    h2, meta_i, wcol, counts = routed
    n = h2.shape[0]
    n_blocks = (2 * n) // MOE_BM + N_EXPERTS
    dest, block_e, runs, n_valid, n_used = _plan(counts[:, 0].astype(jnp.int32), meta_i, n_blocks)
    xb = _sc_dispatch(h2, dest, n_blocks * MOE_BM)
    yb = _experts(block_e, runs, n_valid, n_used, xb, w_gate, w_up, w_down, layer)
    return _sc_gather(yb, dest), wcol


def _rope_tables(seq):
    quarter = HEAD_DIM // 4
    pos = jnp.arange(seq, dtype=F32)
    row_ids = jnp.floor(pos / GRID_W)
    col_ids = pos - row_ids * GRID_W
    inv = ROPE_BASE ** (-jnp.arange(quarter, dtype=F32) / quarter)
    ang_r = row_ids[:, None] * inv
    ang_c = col_ids[:, None] * inv
    zero = jnp.zeros_like(ang_r)
    cos = jnp.concatenate([jnp.cos(ang_r), jnp.cos(ang_r), jnp.cos(ang_c), jnp.cos(ang_c)], axis=-1)
    sa = jnp.concatenate([-jnp.sin(ang_r), zero, -jnp.sin(ang_c), zero], axis=-1)
    sb = jnp.concatenate([zero, jnp.sin(ang_r), zero, jnp.sin(ang_c)], axis=-1)
    rep = LANES // HEAD_DIM
    return tuple(jnp.tile(t, (1, rep)) for t in (cos, sa, sb))


def _router_weights(w_rg, b_rg, w_re, b_re):
    d = w_rg.shape[0]
    pad = EXPERTS_PER_GROUP - N_GROUPS
    wr_t = jnp.concatenate([w_rg.T, jnp.zeros((pad, d), F32), w_re.T], axis=0)
    br_t = jnp.concatenate([b_rg, jnp.full((pad,), NEG_INF, F32), b_re])[:, None]
    return wr_t, br_t


def kernel(x, c, ctx, c_ctx, w_ada, b_ada, g_norm1, g_norm2, g_final, w_in_even, attn_sink, g_sgu,
           w_spatial, b_spatial, w_out_even, w_in_odd, conv_w, w_out_odd, w_router_group,
           b_router_group, w_router_expert, b_router_expert, w_gate, w_up, w_down):
    b, s, d = x.shape
    n = b * s
    n_ctx = ctx.shape[1]
    depth = w_ada.shape[0]
    assert depth == 2 and b + 1 <= MOD_ROWS

    cond = jnp.concatenate([c, c_ctx[None, :], jnp.zeros((MOD_ROWS - b - 1, d), F32)], axis=0)
    mods = _ada(cond, w_ada, b_ada).reshape(depth * MOD_ROWS, 1, 6 * d)
    gf = g_final[None, :]

    lat = x.reshape(n, d)
    w_in_bf = w_in_even[0].astype(BF16)
    tabs = _rope_tables(s)
    qx, k, v, u, z = _even_in(lat, mods, 0, lambda i: i // (s // EVEN_TM), g_norm1[0][None, :], w_in_bf,
                              tabs, s // EVEN_TM, EVEN_TM)
    ones = jnp.ones((n_ctx, LANES), F32)
    zeros = jnp.zeros((n_ctx, LANES), F32)
    _, kc, vc, _, _ = _even_in(ctx.reshape(b * n_ctx, d), mods, 0, lambda i: b, g_norm1[0][None, :],
                               w_in_bf, (ones, zeros, zeros), 1, n_ctx)
    bsp_full = jnp.repeat(b_spatial[0].T, HEAD_DIM, axis=1)
    half = N_Q_HEADS // 2
    w_att = w_out_even[0][:Q_DIM].reshape(2, half, HEAD_DIM, d).transpose(1, 0, 2, 3).reshape(Q_DIM, d)
    w_out_bf = jnp.concatenate([w_att, w_out_even[0][Q_DIM:]], axis=0).astype(BF16)
    conv_w8 = jnp.concatenate([conv_w[0], jnp.zeros((8 - conv_w.shape[1], d), F32)], axis=0)
    w_in_odd_bf = w_in_odd[0].astype(BF16)
    w_out_odd_bf = w_out_odd[0].astype(BF16)
    wsp_bf = jnp.concatenate([w_spatial[0][0::2], w_spatial[0][1::2]], axis=-1).astype(BF16)
    wr0, br0 = _router_weights(w_router_group[0], b_router_group[0], w_router_expert[0], b_router_expert[0])
    wr1, br1 = _router_weights(w_router_group[1], b_router_group[1], w_router_expert[1], b_router_expert[1])

    part = n // MOE_PARTS
    out = None
    for p in range(MOE_PARTS):
        tok0 = p * part
        batch0 = tok0 // s
        lat_p, *routed = _even_mix(lat, qx, k, v, kc, vc, u, z, attn_sink[0], g_sgu[0][None, :],
                                   wsp_bf, bsp_full, w_out_bf, mods, s, n_ctx,
                                   g_norm2[0][None, :], wr0, br0, tok0, part)
        moe_rows = _moe_rows(routed, 0, w_gate, w_up, w_down)
        lat_p, *routed = _odd_mix(lat_p, moe_rows, mods, 1, g_norm1[1][None, :], w_in_odd_bf, conv_w8,
                                  w_out_odd_bf, s, g_norm2[1][None, :], wr1, br1, batch0)
        yg, wcol = _moe_rows(routed, 1, w_gate, w_up, w_down)
        out = _combine(lat_p, yg, wcol, mods, 1, batch0, gf, s, n, tok0, out)
    return out.reshape(b, s, d)
```

```python
import functools

import jax
import jax.numpy as jnp
from jax import lax
from jax.experimental import pallas as pl
from jax.experimental.pallas import tpu as pltpu
from jax.experimental.pallas import tpu_sc as plsc

F32 = jnp.float32
BF16 = jnp.bfloat16
HIGHEST = lax.Precision.HIGHEST

GRID_W = 64
N_Q_HEADS = 8
N_KV_HEADS = 2
HEAD_DIM = 64
ATT_BLOCK = 128
ROPE_BASE = 10000.0
Q_DIM = N_Q_HEADS * HEAD_DIM
KV_DIM = N_KV_HEADS * HEAD_DIM
SG_GROUPS = 8
SG_WIDTH = SG_GROUPS * HEAD_DIM
N_GROUPS = 4
EXPERTS_PER_GROUP = 8
N_EXPERTS = N_GROUPS * EXPERTS_PER_GROUP
EPS = 1e-6
NEG_INF = -1e30

LANES = 128
SC_CORES = 2
SC_SUBCORES = 16
SC_WORKERS = SC_CORES * SC_SUBCORES
SC_CHUNK = 32
MOD_ROWS = 8
ROUTER_ROWS = EXPERTS_PER_GROUP + N_EXPERTS
VMEM_LIMIT = 56 * 1024 * 1024

ADA_TN = 1536
EVEN_TM = 1024
EVEN_SUB = 512
ATT_TQ = 1024
ODD_TM = 512
ODD_SUB = 256
ROUTE_SUB = 256
MOE_BM = 1024
MOE_QUANTUM = 128
MOE_PARTS = 2
WEIGHT_SLOTS = 3
WEIGHT_DMA_PRIORITY = 1
COMBINE_TM = 512


def _params(sem):
    return pltpu.CompilerParams(dimension_semantics=sem, vmem_limit_bytes=VMEM_LIMIT)


def _rms_mod(x, g, shift, scale):
    ms = jnp.mean(x * x, axis=-1, keepdims=True)
    return (x * lax.rsqrt(ms + EPS)) * (g * (1.0 + scale)) + shift


def _pack_rounded(a):
    w = a.shape[1] // 2
    hi = pltpu.bitcast(a[:, :w], jnp.uint32)
    lo = pltpu.bitcast(a[:, w:], jnp.uint32)
    return hi | (lo >> 16)


def _pack_rows(a):
    return _pack_rounded(a.astype(BF16).astype(F32))


def _unpack_rows(p):
    hi = pltpu.bitcast(p & jnp.uint32(0xFFFF0000), F32)
    lo = pltpu.bitcast(p << 16, F32)
    return jnp.concatenate([hi, lo], axis=1)


def _ada_kernel(a_ref, w_ref, b_ref, o_ref):
    a = a_ref[...]
    s = a * (1.0 / (1.0 + jnp.exp(-a)))
    o_ref[0] = jnp.dot(s, w_ref[0], preferred_element_type=F32, precision=HIGHEST) + b_ref[0]


def _ada(cond, w_ada, b_ada):
    depth, d, six_d = w_ada.shape
    return pl.pallas_call(
        _ada_kernel,
        grid=(depth, six_d // ADA_TN),
        in_specs=[
            pl.BlockSpec((MOD_ROWS, d), lambda l, j: (0, 0)),
            pl.BlockSpec((1, d, ADA_TN), lambda l, j: (l, 0, j)),
            pl.BlockSpec((1, 1, ADA_TN), lambda l, j: (l, 0, j)),
        ],
        out_specs=pl.BlockSpec((1, MOD_ROWS, ADA_TN), lambda l, j: (l, 0, j)),
        out_shape=jax.ShapeDtypeStruct((depth, MOD_ROWS, six_d), F32),
        compiler_params=_params(("arbitrary", "arbitrary")),
        name="ada",
    )(cond, w_ada, b_ada.reshape(depth, 1, six_d))


def _even_in_kernel(x_ref, sh_ref, sc_ref, g_ref, w_ref, cos_ref, sa_ref, sb_ref,
                    qx_ref, k_ref, v_ref, u_ref, z_ref):
    tm = x_ref.shape[0]
    sub = min(tm, EVEN_SUB)
    scale = HEAD_DIM ** -0.5
    low = lax.broadcasted_iota(jnp.int32, (sub, LANES), 1) < HEAD_DIM
    heads_per_kv = N_Q_HEADS // N_KV_HEADS
    u0 = Q_DIM + 2 * KV_DIM

    for r0 in range(0, tm, sub):
        rows = slice(r0, r0 + sub)
        h = _rms_mod(x_ref[rows, :], g_ref[...], sh_ref[...], sc_ref[...])
        p = jnp.dot(h.astype(BF16), w_ref[...], preferred_element_type=F32)
        cos, sa, sb = cos_ref[rows, :], sa_ref[rows, :], sb_ref[rows, :]

        def rope(t):
            return t * cos + pltpu.roll(t, LANES - 16, 1) * sa + pltpu.roll(t, 16, 1) * sb

        for cblk in range(Q_DIM // LANES):
            t = rope(p[:, cblk * LANES:(cblk + 1) * LANES]) * scale
            sw = pltpu.roll(t, HEAD_DIM, 1)
            zero = jnp.zeros_like(t)
            if (2 * cblk) // heads_per_kv == 0:
                first, second = jnp.where(low, t, zero), jnp.where(low, sw, zero)
            else:
                first, second = jnp.where(low, zero, sw), jnp.where(low, zero, t)
            qx_ref[rows, (2 * cblk) * LANES:(2 * cblk + 1) * LANES] = first.astype(BF16)
            qx_ref[rows, (2 * cblk + 1) * LANES:(2 * cblk + 2) * LANES] = second.astype(BF16)

        k_ref[rows, :] = rope(p[:, Q_DIM:Q_DIM + KV_DIM]).astype(BF16)
        v_ref[rows, :] = p[:, Q_DIM + KV_DIM:Q_DIM + 2 * KV_DIM].astype(BF16)
        u_ref[rows, :] = p[:, u0:u0 + SG_WIDTH]
        z_ref[rows, :] = p[:, u0 + SG_WIDTH:u0 + 2 * SG_WIDTH]


def _even_in(x2d, mods, layer, mod_row_fn, g, w_bf, tabs, tab_blocks, tm):
    n, d = x2d.shape
    ein = w_bf.shape[1]
    cos, sa, sb = tabs
    row = lambda i: layer * MOD_ROWS + mod_row_fn(i)
    tab_spec = pl.BlockSpec((tm, LANES), lambda i: (i % tab_blocks, 0))
    qx_dim = N_Q_HEADS * LANES
    return pl.pallas_call(
        _even_in_kernel,
        grid=(n // tm,),
        in_specs=[
            pl.BlockSpec((tm, d), lambda i: (i, 0)),
            pl.BlockSpec((None, 1, d), lambda i: (row(i), 0, 0)),
            pl.BlockSpec((None, 1, d), lambda i: (row(i), 0, 1)),
            pl.BlockSpec((1, d), lambda i: (0, 0)),
            pl.BlockSpec((d, ein), lambda i: (0, 0)),
            tab_spec, tab_spec, tab_spec,
        ],
        out_specs=[
            pl.BlockSpec((tm, qx_dim), lambda i: (i, 0)),
            pl.BlockSpec((tm, KV_DIM), lambda i: (i, 0)),
            pl.BlockSpec((tm, KV_DIM), lambda i: (i, 0)),
            pl.BlockSpec((tm, SG_WIDTH), lambda i: (i, 0)),
            pl.BlockSpec((tm, SG_WIDTH), lambda i: (i, 0)),
        ],
        out_shape=[
            jax.ShapeDtypeStruct((n, qx_dim), BF16),
            jax.ShapeDtypeStruct((n, KV_DIM), BF16),
            jax.ShapeDtypeStruct((n, KV_DIM), BF16),
            jax.ShapeDtypeStruct((n, SG_WIDTH), F32),
            jax.ShapeDtypeStruct((n, SG_WIDTH), F32),
        ],
        compiler_params=_params(("parallel",)),
        name="even_in",
    )(x2d, mods, mods, g, w_bf, cos, sa, sb)


def _gelu(x):
    return 0.5 * x * (1.0 + lax.erf(x * (2.0 ** -0.5)))


def _even_mix_kernel(sink_ref, lat_ref, qx_ref, km_ref, kp_ref, kn_ref, vm_ref, vp_ref, vn_ref,
                     kc_ref, vc_ref, u_ref, z_ref, gsgu_ref, wsp_ref, bsp_ref, wout_ref, gate_ref,
                     sh2_ref, sc2_ref, g2_ref, wr_ref, br_ref,
                     o_ref, h_ref, mi_ref, wc_ref, cnt_ref,
                     kband, vband, mixin, carry, upper, *, tiles_per_seq):
    i = pl.program_id(0)
    tq = qx_ref.shape[0]

    @pl.when(i == 0)
    def _():
        _route_init(carry, upper)

    blk = ATT_BLOCK
    nsub = tq // blk
    n_ctx = kc_ref.shape[0]
    first = (i % tiles_per_seq) == 0
    last = (i % tiles_per_seq) == tiles_per_seq - 1

    kband[0:blk] = kp_ref[...]
    kband[blk:blk + tq] = km_ref[...]
    kband[blk + tq:] = kn_ref[...]
    vband[0:blk] = vp_ref[...]
    vband[blk:blk + tq] = vm_ref[...]
    vband[blk + tq:] = vn_ref[...]

    rows = N_Q_HEADS * blk
    tok = lax.broadcasted_iota(jnp.int32, (rows, blk), 0) & (blk - 1)
    col = lax.broadcasted_iota(jnp.int32, (rows, blk), 1)
    tri_prev = col >= tok
    tri_next = col <= tok
    head = lax.broadcasted_iota(jnp.int32, (rows, 1), 0) // blk
    sink_col = jnp.zeros((rows, 1), F32)
    for hd in range(N_Q_HEADS):
        sink_col = jnp.where(head == hd, sink_ref[hd], sink_col)
    lane_low = lax.broadcasted_iota(jnp.int32, (blk, LANES), 1) < HEAD_DIM
    ones = jnp.ones((n_ctx + 3 * blk, LANES), BF16)
    nt = (((1,), (1,)), ((), ()))

    def sub_block(j, c):
        r0 = pl.multiple_of(j * blk, blk)
        ok_prev = jnp.logical_not(jnp.logical_and(first, j == 0))
        ok_next = jnp.logical_not(jnp.logical_and(last, j == nsub - 1))
        qs = jnp.concatenate([qx_ref[pl.ds(r0, blk), hd * LANES:(hd + 1) * LANES]
                              for hd in range(N_Q_HEADS)], axis=0)
        kall = jnp.concatenate([kc_ref[...], kband[pl.ds(r0, 3 * blk), :]], axis=0)
        vall = jnp.concatenate([vc_ref[...], vband[pl.ds(r0, 3 * blk), :]], axis=0)
        s = lax.dot_general(qs, kall, nt, preferred_element_type=F32)
        c0 = n_ctx
        s = jnp.concatenate([
            s[:, :c0],
            jnp.where(jnp.logical_and(tri_prev, ok_prev), s[:, c0:c0 + blk], NEG_INF),
            s[:, c0 + blk:c0 + 2 * blk],
            jnp.where(jnp.logical_and(tri_next, ok_next), s[:, c0 + 2 * blk:], NEG_INF),
        ], axis=1)
        m = jnp.maximum(jnp.max(s, axis=-1, keepdims=True), sink_col)
        p = jnp.exp(s - m).astype(BF16)
        o = jnp.dot(p, jnp.concatenate([vall, ones], axis=1), preferred_element_type=F32)
        att = o[:, :LANES] / (o[:, LANES:] + jnp.exp(sink_col - m))
        half = N_Q_HEADS // 2
        for hd in range(half):
            pair = jnp.where(lane_low, att[hd * blk:(hd + 1) * blk], att[(hd + half) * blk:(hd + half + 1) * blk])
            mixin[pl.ds(r0, blk), hd * LANES:(hd + 1) * LANES] = pair.astype(BF16)

        ug = _gelu(u_ref[pl.ds(r0, blk), :])
        zg = _gelu(z_ref[pl.ds(r0, blk), :])
        mu = jnp.mean(zg, axis=-1, keepdims=True)
        zc = zg - mu
        zn = zc * lax.rsqrt(jnp.mean(zc * zc, axis=-1, keepdims=True) + EPS) * gsgu_ref[...]
        for pair in range(SG_GROUPS // 2):
            zp = zn[:, pair * LANES:(pair + 1) * LANES]
            zero = jnp.zeros_like(zp)
            lo = jnp.where(lane_low, zp, zero).astype(BF16)
            hi = jnp.where(lane_low, zero, zp).astype(BF16)
            sg = (jnp.dot(wsp_ref[pair], jnp.concatenate([lo, hi], axis=0), preferred_element_type=F32)
                  + bsp_ref[:, pair * LANES:(pair + 1) * LANES])
            mixin[pl.ds(r0, blk), Q_DIM + pair * LANES:Q_DIM + (pair + 1) * LANES] = (
                ug[:, pair * LANES:(pair + 1) * LANES] * sg).astype(BF16)
        return c

    lax.fori_loop(0, nsub, sub_block, 0, unroll=True)
    for r0 in range(0, tq, ROUTE_SUB):
        rows = slice(r0, r0 + ROUTE_SUB)
        mix = jnp.dot(mixin[rows, :], wout_ref[...], preferred_element_type=F32)
        lat = lat_ref[rows, :] + gate_ref[...] * mix
        o_ref[rows, :] = lat
        _route_tile(lat, sh2_ref[...], sc2_ref[...], g2_ref[...], wr_ref, br_ref, carry, upper,
                    h_ref.at[rows, :], mi_ref.at[:, rows], wc_ref.at[rows, :], cnt_ref)


def _even_mix(lat, qx, k, v, kc, vc, u, z, sink, g_sgu, wsp_bf, bsp_full, wout_bf, mods, seq, ctx_len,
              g2, wr_t, br_t, tok0, n):
    d = lat.shape[1]
    tq = ATT_TQ
    tiles_per_seq = seq // tq
    sub = tq // ATT_BLOCK
    nblk = lat.shape[0] // ATT_BLOCK
    t0 = tok0 // tq
    main = lambda w: pl.BlockSpec((tq, w), lambda i: (t0 + i, 0))
    prev = pl.BlockSpec((ATT_BLOCK, KV_DIM), lambda i: (jnp.maximum((t0 + i) * sub - 1, 0), 0))
    nxt = pl.BlockSpec((ATT_BLOCK, KV_DIM), lambda i: (jnp.minimum((t0 + i + 1) * sub, nblk - 1), 0))
    batch = lambda t: (t0 + t) // tiles_per_seq
    ctxs = pl.BlockSpec((ctx_len, KV_DIM), lambda i: (batch(i), 0))
    const = lambda shape: pl.BlockSpec(shape, lambda i: (0,) * len(shape), pipeline_mode=pl.Buffered(1))
    r_in, r_out, r_shapes, r_scratch = _route_specs(n, d, tq, lambda i: i, batch, ROUTE_SUB)
    return pl.pallas_call(
        functools.partial(_even_mix_kernel, tiles_per_seq=tiles_per_seq),
        grid=(n // tq,),
        in_specs=[
            pl.BlockSpec(memory_space=pltpu.SMEM),
            main(d), main(qx.shape[1]),
            main(KV_DIM), prev, nxt,
            main(KV_DIM), prev, nxt,
            ctxs, ctxs,
            main(SG_WIDTH), main(SG_WIDTH),
            const((1, SG_WIDTH)), const(wsp_bf.shape), const(bsp_full.shape), const(wout_bf.shape),
            pl.BlockSpec((None, 1, d), lambda i: (batch(i), 0, 2)),
        ] + r_in,
        out_specs=[pl.BlockSpec((tq, d), lambda i: (i, 0))] + r_out,
        out_shape=[jax.ShapeDtypeStruct((n, d), F32)] + r_shapes,
        scratch_shapes=[
            pltpu.VMEM((tq + 2 * ATT_BLOCK, KV_DIM), BF16),
            pltpu.VMEM((tq + 2 * ATT_BLOCK, KV_DIM), BF16),
            pltpu.VMEM((tq, Q_DIM + SG_WIDTH), BF16),
        ] + r_scratch,
        compiler_params=_params(("arbitrary",)),
        name="even_mix",
    )(sink, lat, qx, k, k, k, v, v, v, kc, vc, u, z, g_sgu, wsp_bf, bsp_full, wout_bf, mods,
      mods, mods, g2, wr_t, br_t)


def _odd_mix_kernel(x_ref, e0_ref, e1_ref, ew_ref, egate_ref, sh_ref, sc_ref, gate_ref, g_ref,
                    win_ref, cw_ref, wout_ref, sh2_ref, sc2_ref, g2_ref, wr_ref, br_ref,
                    o_ref, h_ref, mi_ref, wc_ref, cnt_ref,
                    y_s, bg_s, tail_s, x_s, carry, upper, *, tiles_per_seq):
    i = pl.program_id(0)
    n_tiles = pl.num_programs(0) - 1
    tm, d = x_ref.shape
    cur = i % 2
    prv = 1 - cur

    @pl.when(i == 0)
    def _():
        _route_init(carry, upper)

    tail_s[...] = y_s[cur, tm - 8:tm, :]

    @pl.when(i < n_tiles)
    def _():
        for r0 in range(0, tm, ODD_SUB):
            rows = slice(r0, r0 + ODD_SUB)
            ew = ew_ref[rows, :]
            moe = ew[:, 0:1] * _unpack_rows(e0_ref[rows, :]) + ew[:, 1:2] * _unpack_rows(e1_ref[rows, :])
            x = x_ref[rows, :] + egate_ref[...] * moe
            x_s[cur, rows, :] = x
            h = _rms_mod(x, g_ref[...], sh_ref[...], sc_ref[...])
            p = jnp.dot(h.astype(BF16), win_ref[...], preferred_element_type=F32)
            bg_s[cur, rows, :] = p[:, 0:d]
            y_s[cur, rows, :] = p[:, d:2 * d] * p[:, 2 * d:3 * d]

    @pl.when(i >= 1)
    def _():
        t_prev = i - 1
        first = (t_prev % tiles_per_seq) == 0
        last = (t_prev % tiles_per_seq) == tiles_per_seq - 1
        y = y_s[prv]
        left = jnp.where(first, 0.0, tail_s[7:8, :])
        right = jnp.where(last, 0.0, y_s[cur, 0:1, :])
        ridx = lax.broadcasted_iota(jnp.int32, (tm, d), 0)
        y_dn = jnp.where(ridx == 0, left, pltpu.roll(y, 1, 0))
        y_up = jnp.where(ridx == tm - 1, right, pltpu.roll(y, tm - 1, 0))
        conv = y_dn * cw_ref[0:1, :] + y * cw_ref[1:2, :] + y_up * cw_ref[2:3, :]
        mix = jnp.dot((bg_s[prv] * conv).astype(BF16), wout_ref[...], preferred_element_type=F32)
        lat = x_s[prv] + gate_ref[...] * mix
        o_ref[...] = lat
        _route_tile(lat, sh2_ref[...], sc2_ref[...], g2_ref[...], wr_ref, br_ref, carry, upper,
                    h_ref, mi_ref, wc_ref, cnt_ref)


def _odd_mix(lat, moe_rows, mods, layer, g, win_bf, conv_w8, wout_bf, seq, g2, wr_t, br_t, batch0):
    yg, ewcol = moe_rows
    n, d = lat.shape
    tm = ODD_TM
    nt = n // tm
    tiles_per_seq = seq // tm
    cur = lambda i: jnp.minimum(i, nt - 1)
    prv = lambda i: jnp.maximum(i - 1, 0)
    row = lambda t: layer * MOD_ROWS + batch0 + t // tiles_per_seq
    const = lambda shape: pl.BlockSpec(shape, lambda i: (0,) * len(shape), pipeline_mode=pl.Buffered(1))
    r_in, r_out, r_shapes, r_scratch = _route_specs(n, d, tm, prv, row, tm)
    return pl.pallas_call(
        functools.partial(_odd_mix_kernel, tiles_per_seq=tiles_per_seq),
        grid=(nt + 1,),
        in_specs=[
            pl.BlockSpec((tm, d), lambda i: (cur(i), 0)),
            pl.BlockSpec((tm, d // 2), lambda i: (cur(i), 0)),
            pl.BlockSpec((tm, d // 2), lambda i: (nt + cur(i), 0)),
            pl.BlockSpec((tm, LANES), lambda i: (cur(i), 0)),
            pl.BlockSpec((None, 1, d), lambda i: (row(cur(i)) - MOD_ROWS, 0, 5)),
            pl.BlockSpec((None, 1, d), lambda i: (row(cur(i)), 0, 0)),
            pl.BlockSpec((None, 1, d), lambda i: (row(cur(i)), 0, 1)),
            pl.BlockSpec((None, 1, d), lambda i: (row(prv(i)), 0, 2)),
            const((1, d)), const(win_bf.shape), const(conv_w8.shape), const(wout_bf.shape),
        ] + r_in,
        out_specs=[pl.BlockSpec((tm, d), lambda i: (prv(i), 0))] + r_out,
        out_shape=[jax.ShapeDtypeStruct((n, d), F32)] + r_shapes,
        scratch_shapes=[
            pltpu.VMEM((2, tm, d), F32),
            pltpu.VMEM((2, tm, d), F32),
            pltpu.VMEM((8, d), F32),
            pltpu.VMEM((2, tm, d), F32),
        ] + r_scratch,
        compiler_params=_params(("arbitrary",)),
        name="odd_mix",
    )(lat, yg, yg, ewcol, mods, mods, mods, mods, g, win_bf, conv_w8, wout_bf, mods, mods, g2, wr_t, br_t)


def _route_init(carry, upper):
    tm = upper.shape[0]
    carry[...] = jnp.zeros_like(carry)
    r_i = lax.broadcasted_iota(jnp.int32, (tm, tm), 0)
    c_i = lax.broadcasted_iota(jnp.int32, (tm, tm), 1)
    upper[...] = jnp.where(r_i < c_i, 1.0, 0.0).astype(BF16)


def _route_tile(lat, sh, sc, g, wr_ref, br_ref, carry, upper, h_ref, mi_ref, wc_ref, cnt_ref):
    tm = lat.shape[0]
    epg = EXPERTS_PER_GROUP
    h = _rms_mod(lat, g, sh, sc)
    h_hi = h.astype(BF16)
    h_hi_f = h_hi.astype(F32)
    h_ref[...] = _pack_rounded(h_hi_f)
    h_lo = (h - h_hi_f).astype(BF16)
    w = wr_ref[...]
    w1 = w.astype(BF16).astype(F32)
    r1 = w - w1
    w2 = r1.astype(BF16).astype(F32)
    w3 = r1 - w2
    nt = (((1,), (1,)), ((), ()))
    nr = w.shape[0]
    w123 = jnp.concatenate([w1, w2, w3, jnp.zeros((8, w.shape[1]), F32)], axis=0).astype(BF16)
    w12 = jnp.concatenate([w1, w2], axis=0).astype(BF16)
    p_hi = lax.dot_general(w123, h_hi, nt, preferred_element_type=F32)
    p_lo = lax.dot_general(w12, h_lo, nt, preferred_element_type=F32)
    lg = ((p_hi[2 * nr:3 * nr] + p_lo[nr:2 * nr]) + (p_hi[nr:2 * nr] + p_lo[0:nr])) + p_hi[0:nr] + br_ref[...]
    io8 = lax.broadcasted_iota(jnp.int32, (epg, tm), 0)
    gl = lg[0:epg]
    gmax = jnp.max(gl, axis=0, keepdims=True)
    g_idx = jnp.min(jnp.where(gl == gmax, io8, epg), axis=0, keepdims=True)
    g_w = 1.0 / jnp.sum(jnp.exp(gl - gmax), axis=0, keepdims=True)
    e_sel = lg[epg:2 * epg]
    for gi in range(1, N_GROUPS):
        e_sel = jnp.where(g_idx == gi, lg[(gi + 1) * epg:(gi + 2) * epg], e_sel)
    v0 = jnp.max(e_sel, axis=0, keepdims=True)
    i0 = jnp.min(jnp.where(e_sel == v0, io8, epg), axis=0, keepdims=True)
    rest = jnp.where(io8 == i0, -jnp.inf, e_sel)
    v1 = jnp.max(rest, axis=0, keepdims=True)
    i1 = jnp.min(jnp.where(rest == v1, io8, epg), axis=0, keepdims=True)
    t = jnp.exp(v1 - v0)
    w0 = g_w / (1.0 + t)
    w1 = g_w * t / (1.0 + t)
    e0 = g_idx * epg + i0
    e1 = g_idx * epg + i1

    io32 = lax.broadcasted_iota(jnp.int32, (N_EXPERTS, tm), 0)
    hit0 = io32 == e0
    hit1 = io32 == e1
    onehot = jnp.where(hit0 | hit1, 1.0, 0.0)
    cum = jnp.dot(onehot.astype(BF16), upper[...], preferred_element_type=F32) + carry[...]
    rank0 = jnp.sum(jnp.where(hit0, cum, 0.0), axis=0, keepdims=True).astype(jnp.int32)
    rank1 = jnp.sum(jnp.where(hit1, cum, 0.0), axis=0, keepdims=True).astype(jnp.int32)
    carry[...] = carry[...] + jnp.sum(onehot, axis=1, keepdims=True)
    cnt_ref[...] = jnp.broadcast_to(carry[...], cnt_ref.shape)

    mi_ref[...] = jnp.where(io8 == 0, e0, jnp.where(io8 == 1, e1, jnp.where(io8 == 2, rank0,
                            jnp.where(io8 == 3, rank1, 0))))
    io128 = lax.broadcasted_iota(jnp.int32, (LANES, tm), 0)
    wrow = jnp.where(io128 == 0, w0, jnp.where(io128 == 1, w1, 0.0))
    wc_ref[...] = wrow.T


def _route_specs(n, d, tm, tile_of, mods_row, route_rows):
    in_specs = [
        pl.BlockSpec((None, 1, d), lambda i: (mods_row(tile_of(i)), 0, 3)),
        pl.BlockSpec((None, 1, d), lambda i: (mods_row(tile_of(i)), 0, 4)),
        pl.BlockSpec((1, d), lambda i: (0, 0), pipeline_mode=pl.Buffered(1)),
        pl.BlockSpec((ROUTER_ROWS, d), lambda i: (0, 0), pipeline_mode=pl.Buffered(1)),
        pl.BlockSpec((ROUTER_ROWS, 1), lambda i: (0, 0), pipeline_mode=pl.Buffered(1)),
    ]
    out_specs = [
        pl.BlockSpec((tm, d // 2), lambda i: (tile_of(i), 0)),
        pl.BlockSpec((8, tm), lambda i: (0, tile_of(i))),
        pl.BlockSpec((tm, LANES), lambda i: (tile_of(i), 0)),
        pl.BlockSpec((N_EXPERTS, LANES), lambda i: (0, 0)),
    ]
    out_shapes = [
        jax.ShapeDtypeStruct((n, d // 2), jnp.uint32),
        jax.ShapeDtypeStruct((8, n), jnp.int32),
        jax.ShapeDtypeStruct((n, LANES), F32),
        jax.ShapeDtypeStruct((N_EXPERTS, LANES), F32),
    ]
    scratch = [pltpu.VMEM((N_EXPERTS, 1), F32), pltpu.VMEM((route_rows, route_rows), BF16)]
    return in_specs, out_specs, out_shapes, scratch


def _plan_kernel(cnt_ref, mi_ref, dest_ref, be_ref, runs_ref, nv_ref, nu_ref, ps_ref, *, n_blocks):
    bm = MOE_BM

    def per_expert(e, carry):
        blk0, n_runs = carry
        cnt = cnt_ref[e]
        nb = (cnt + bm - 1) // bm
        ps_ref[e] = blk0 * bm

        def fill(b, c):
            be_ref[b] = e
            nv_ref[b] = jnp.minimum(cnt - (b - blk0) * bm, bm)
            return c

        lax.fori_loop(blk0, blk0 + nb, fill, 0)

        @pl.when(nb > 0)
        def _():
            runs_ref[n_runs] = e

        return blk0 + nb, n_runs + jnp.where(nb > 0, 1, 0)

    n_used, n_runs = lax.fori_loop(0, N_EXPERTS, per_expert, (0, 0))
    nu_ref[0] = n_used
    nu_ref[1] = n_runs
    last_e = be_ref[jnp.maximum(n_used - 1, 0)]

    def fill_tail(b, c):
        be_ref[b] = last_e
        nv_ref[b] = 0
        return c

    lax.fori_loop(n_used, n_blocks, fill_tail, 0)

    def fill_runs(k, c):
        runs_ref[k] = last_e
        return c

    lax.fori_loop(n_runs, N_EXPERTS, fill_runs, 0)

    e01 = mi_ref[0:2, :]
    dest = mi_ref[2:4, :]
    for e in range(N_EXPERTS):
        dest = dest + jnp.where(e01 == e, ps_ref[e], 0)
    dest_ref[...] = dest


def _plan(counts, meta_i, n_blocks):
    n = meta_i.shape[1]
    return pl.pallas_call(
        functools.partial(_plan_kernel, n_blocks=n_blocks),
        in_specs=[pl.BlockSpec(memory_space=pltpu.SMEM), pl.BlockSpec(memory_space=pltpu.VMEM)],
        out_specs=[pl.BlockSpec(memory_space=pltpu.VMEM)] + [pl.BlockSpec(memory_space=pltpu.SMEM)] * 4,
        out_shape=[
            jax.ShapeDtypeStruct((2, n), jnp.int32),
            jax.ShapeDtypeStruct((n_blocks,), jnp.int32),
            jax.ShapeDtypeStruct((N_EXPERTS,), jnp.int32),
            jax.ShapeDtypeStruct((n_blocks,), jnp.int32),
            jax.ShapeDtypeStruct((2,), jnp.int32),
        ],
        scratch_shapes=[pltpu.SMEM((N_EXPERTS,), jnp.int32)],
        compiler_params=pltpu.CompilerParams(vmem_limit_bytes=VMEM_LIMIT),
        name="plan",
    )(counts, meta_i)


def _sc_mesh():
    return plsc.VectorSubcoreMesh(core_axis_name="c", subcore_axis_name="s",
                                  num_cores=SC_CORES, num_subcores=SC_SUBCORES)


def _sc_worker():
    return lax.axis_index("s") * SC_CORES + lax.axis_index("c")


def _sc_dispatch(h2, dest, n_rows):
    n, d = h2.shape
    c = SC_CHUNK
    per_w = n // SC_WORKERS
    nchunk = per_w // c
    idx = dest.reshape(2, SC_WORKERS, nchunk, c)

    @functools.partial(
        pl.kernel, mesh=_sc_mesh(), out_type=jax.ShapeDtypeStruct((n_rows, d), h2.dtype),
        scratch_types=[pltpu.VMEM((nchunk, c), jnp.int32), pltpu.VMEM((nchunk, c), jnp.int32),
                       pltpu.VMEM((2, c, d), h2.dtype),
                       pltpu.SemaphoreType.DMA((2,)), pltpu.SemaphoreType.DMA((2,))])
    def k(h_hbm, idx_hbm, xb_hbm, idx0_v, idx1_v, rows_v, gsem, ssem):
        wid = _sc_worker()
        base = wid * per_w
        idx_v = (idx0_v, idx1_v)
        for kk in range(2):
            pltpu.sync_copy(idx_hbm.at[kk, wid], idx_v[kk])

        def get(j, slot):
            return pltpu.make_async_copy(h_hbm.at[pl.ds(base + j * c, c)], rows_v.at[slot], gsem.at[slot])

        def put(j, slot, kk):
            return pltpu.make_async_copy(rows_v.at[slot], xb_hbm.at[idx_v[kk].at[j]], ssem.at[slot])

        get(0, 0).start()

        @pl.loop(0, nchunk, step=2)
        def _(j):
            for slot in range(2):
                jj = j + slot
                get(jj, slot).wait()

                @pl.when(jj >= 1)
                def _():
                    for kk in range(2):
                        put(jj - 1, 1 - slot, kk).wait()

                @pl.when(jj + 1 < nchunk)
                def _():
                    get(jj + 1, 1 - slot).start()

                for kk in range(2):
                    put(jj, slot, kk).start()

        for kk in range(2):
            put(nchunk - 1, (nchunk - 1) % 2, kk).wait()

    return k(h2, idx)


def _sc_gather(y, dest):
    d = y.shape[1]
    total = dest.shape[0] * dest.shape[1]
    c = SC_CHUNK
    per_w = total // SC_WORKERS
    nchunk = per_w // c
    idx = dest.reshape(SC_WORKERS, nchunk, c)

    @functools.partial(
        pl.kernel, mesh=_sc_mesh(), out_type=jax.ShapeDtypeStruct((total, d), y.dtype),
        scratch_types=[pltpu.VMEM((nchunk, c), jnp.int32), pltpu.VMEM((2, c, d), y.dtype),
                       pltpu.SemaphoreType.DMA((2,)), pltpu.SemaphoreType.DMA((2,))])
    def k(y_hbm, idx_hbm, out_hbm, idx_v, rows_v, gsem, ssem):
        wid = _sc_worker()
        base = wid * per_w
        pltpu.sync_copy(idx_hbm.at[wid], idx_v)

        def get(j, slot):
            return pltpu.make_async_copy(y_hbm.at[idx_v.at[j]], rows_v.at[slot], gsem.at[slot])

        def put(j, slot):
            return pltpu.make_async_copy(rows_v.at[slot], out_hbm.at[pl.ds(base + j * c, c)], ssem.at[slot])

        get(0, 0).start()

        @pl.loop(0, nchunk, step=2)
        def _(j):
            for slot in range(2):
                jj = j + slot
                get(jj, slot).wait()

                @pl.when(jj >= 1)
                def _():
                    put(jj - 1, 1 - slot).wait()

                @pl.when(jj + 1 < nchunk)
                def _():
                    get(jj + 1, 1 - slot).start()

                put(jj, slot).start()

        put(nchunk - 1, (nchunk - 1) % 2).wait()

    return k(y, idx)


def _expert_kernel(be_ref, runs_ref, nv_ref, nu_ref, x_ref, wg_hbm, wu_hbm, wd_hbm, y_ref,
                   wgu_s, wd_s, stg_g, stg_u, stg_d, run_s, sems, *, layer):
    b = pl.program_id(0)
    hid = stg_g.shape[2]
    e = be_ref[b]
    n_runs = nu_ref[1]
    changed = jnp.logical_or(b == 0, e != be_ref[jnp.maximum(b - 1, 0)])

    def fetch(run):
        expert = runs_ref[run]
        slot = run % WEIGHT_SLOTS
        return (pltpu.make_async_copy(wg_hbm.at[layer, expert], stg_g.at[slot], sems.at[slot]),
                pltpu.make_async_copy(wu_hbm.at[layer, expert], stg_u.at[slot], sems.at[slot]),
                pltpu.make_async_copy(wd_hbm.at[layer, expert], stg_d.at[slot], sems.at[slot]))

    @pl.when(b == 0)
    def _():
        for r in range(WEIGHT_SLOTS - 1):
            @pl.when(r < n_runs)
            def _():
                for cp in fetch(r):
                    cp.start(priority=WEIGHT_DMA_PRIORITY)

    @pl.when(changed)
    def _():
        run = jnp.where(b == 0, 0, run_s[0] + 1)
        run_s[0] = run
        for cp in fetch(run):
            cp.wait()

        ahead = run + WEIGHT_SLOTS - 1

        @pl.when(ahead < n_runs)
        def _():
            for cp in fetch(ahead):
                cp.start(priority=WEIGHT_DMA_PRIORITY)

        slot = run % WEIGHT_SLOTS
        wgu_s[:, 0:hid] = stg_g[slot].astype(BF16)
        wgu_s[:, hid:2 * hid] = stg_u[slot].astype(BF16)
        wd_s[...] = stg_d[slot].astype(BF16)

    bm, dp = x_ref.shape
    nv = nv_ref[b]
    in_use = b < nu_ref[0]

    def run(rows):
        live = lax.broadcasted_iota(jnp.int32, (rows, dp), 0) < nv
        x = _unpack_rows(jnp.where(live, x_ref[0:rows, :], jnp.uint32(0)))
        gu = jnp.dot(x.astype(BF16), wgu_s[...], preferred_element_type=F32)
        gate = gu[:, 0:hid]
        act = gate * (1.0 / (1.0 + jnp.exp(-gate))) * gu[:, hid:2 * hid]
        y_ref[0:rows, :] = _pack_rows(jnp.dot(act.astype(BF16), wd_s[...], preferred_element_type=F32))

    n_quanta = bm // MOE_QUANTUM
    for q in range(1, n_quanta + 1):
        rows = q * MOE_QUANTUM

        @pl.when(jnp.logical_and(in_use, jnp.logical_and(nv > rows - MOE_QUANTUM, nv <= rows)))
        def _(rows=rows):
            run(rows)
            if rows < bm:
                y_ref[rows:bm, :] = jnp.zeros((bm - rows, dp), y_ref.dtype)

    @pl.when(jnp.logical_not(in_use))
    def _():
        y_ref[...] = jnp.zeros_like(y_ref)


def _experts(block_e, runs, n_valid, n_used, xb, w_gate, w_up, w_down, layer):
    n_rows, dp = xb.shape
    d, hid = w_gate.shape[2], w_gate.shape[3]
    bm = MOE_BM
    n_blocks = n_rows // bm
    hbm = pl.BlockSpec(memory_space=pl.ANY)
    return pl.pallas_call(
        functools.partial(_expert_kernel, layer=layer),
        grid_spec=pltpu.PrefetchScalarGridSpec(
            num_scalar_prefetch=4,
            grid=(n_blocks,),
            in_specs=[
                pl.BlockSpec((bm, dp), lambda b, be, nx, nv, nu: (jnp.minimum(b, nu[0] - 1), 0)),
                hbm, hbm, hbm,
            ],
            out_specs=pl.BlockSpec((bm, dp), lambda b, be, nx, nv, nu: (b, 0)),
            scratch_shapes=[
                pltpu.VMEM((d, 2 * hid), BF16), pltpu.VMEM((hid, d), BF16),
                pltpu.VMEM((WEIGHT_SLOTS, d, hid), F32), pltpu.VMEM((WEIGHT_SLOTS, d, hid), F32),
                pltpu.VMEM((WEIGHT_SLOTS, hid, d), F32),
                pltpu.SMEM((1,), jnp.int32), pltpu.SemaphoreType.DMA((WEIGHT_SLOTS,)),
            ],
        ),
        out_shape=jax.ShapeDtypeStruct((n_rows, dp), jnp.uint32),
        compiler_params=_params(("arbitrary",)),
        name="experts",
    )(block_e, runs, n_valid, n_used, xb, w_gate, w_up, w_down)


def _combine_kernel(lat_ref, y0_ref, y1_ref, wc_ref, gate_ref, gf_ref, *rest):
    o_ref = rest[-1]
    wc = wc_ref[...]
    moe = wc[:, 0:1] * _unpack_rows(y0_ref[...]) + wc[:, 1:2] * _unpack_rows(y1_ref[...])
    out = lat_ref[...] + gate_ref[...] * moe
    ms = jnp.mean(out * out, axis=-1, keepdims=True)
    o_ref[...] = out * lax.rsqrt(ms + EPS) * gf_ref[...]


def _combine(lat, yg, wcol, mods, layer, batch0, g_final, seq, out_rows, tok0, prev_out):
    n, d = lat.shape
    tm = COMBINE_TM
    nt = n // tm
    t0 = tok0 // tm
    tiles_per_seq = seq // tm
    row = lambda i: layer * MOD_ROWS + batch0 + i // tiles_per_seq
    in_specs = [
        pl.BlockSpec((tm, d), lambda i: (i, 0)),
        pl.BlockSpec((tm, d // 2), lambda i: (i, 0)),
        pl.BlockSpec((tm, d // 2), lambda i: (nt + i, 0)),
        pl.BlockSpec((tm, LANES), lambda i: (i, 0)),
        pl.BlockSpec((None, 1, d), lambda i: (row(i), 0, 5)),
        pl.BlockSpec((1, d), lambda i: (0, 0)),
    ]
    args = [lat, yg, yg, wcol, mods, g_final]
    aliases = {}
    if prev_out is not None:
        in_specs.append(pl.BlockSpec(memory_space=pl.ANY))
        args.append(prev_out)
        aliases = {len(args) - 1: 0}
    return pl.pallas_call(
        _combine_kernel,
        grid=(nt,),
        in_specs=in_specs,
        out_specs=pl.BlockSpec((tm, d), lambda i: (t0 + i, 0)),
        out_shape=jax.ShapeDtypeStruct((out_rows, d), F32),
        input_output_aliases=aliases,
        compiler_params=_params(("parallel",)),
        name="combine",
    )(*args)


def _moe_rows(routed, layer, w_gate, w_up, w_down):
    h2, meta_i, wcol, counts = routed
    n = h2.shape[0]
    n_blocks = (2 * n) // MOE_BM + N_EXPERTS
    dest, block_e, runs, n_valid, n_used = _plan(counts[:, 0].astype(jnp.int32), meta_i, n_blocks)
    xb = _sc_dispatch(h2, dest, n_blocks * MOE_BM)
    yb = _experts(block_e, runs, n_valid, n_used, xb, w_gate, w_up, w_down, layer)
    return _sc_gather(yb, dest), wcol


def _rope_tables(seq):
    quarter = HEAD_DIM // 4
    pos = jnp.arange(seq, dtype=F32)
    row_ids = jnp.floor(pos / GRID_W)
    col_ids = pos - row_ids * GRID_W
    inv = ROPE_BASE ** (-jnp.arange(quarter, dtype=F32) / quarter)
    ang_r = row_ids[:, None] * inv
    ang_c = col_ids[:, None] * inv
    zero = jnp.zeros_like(ang_r)
    cos = jnp.concatenate([jnp.cos(ang_r), jnp.cos(ang_r), jnp.cos(ang_c), jnp.cos(ang_c)], axis=-1)
    sa = jnp.concatenate([-jnp.sin(ang_r), zero, -jnp.sin(ang_c), zero], axis=-1)
    sb = jnp.concatenate([zero, jnp.sin(ang_r), zero, jnp.sin(ang_c)], axis=-1)
    rep = LANES // HEAD_DIM
    return tuple(jnp.tile(t, (1, rep)) for t in (cos, sa, sb))


def _router_weights(w_rg, b_rg, w_re, b_re):
    d = w_rg.shape[0]
    pad = EXPERTS_PER_GROUP - N_GROUPS
    wr_t = jnp.concatenate([w_rg.T, jnp.zeros((pad, d), F32), w_re.T], axis=0)
    br_t = jnp.concatenate([b_rg, jnp.full((pad,), NEG_INF, F32), b_re])[:, None]
    return wr_t, br_t


def kernel(x, c, ctx, c_ctx, w_ada, b_ada, g_norm1, g_norm2, g_final, w_in_even, attn_sink, g_sgu,
           w_spatial, b_spatial, w_out_even, w_in_odd, conv_w, w_out_odd, w_router_group,
           b_router_group, w_router_expert, b_router_expert, w_gate, w_up, w_down):
    b, s, d = x.shape
    n = b * s
    n_ctx = ctx.shape[1]
    depth = w_ada.shape[0]
    assert depth == 2 and b + 1 <= MOD_ROWS

    cond = jnp.concatenate([c, c_ctx[None, :], jnp.zeros((MOD_ROWS - b - 1, d), F32)], axis=0)
    mods = _ada(cond, w_ada, b_ada).reshape(depth * MOD_ROWS, 1, 6 * d)
    gf = g_final[None, :]

    lat = x.reshape(n, d)
    w_in_bf = w_in_even[0].astype(BF16)
    tabs = _rope_tables(s)
    qx, k, v, u, z = _even_in(lat, mods, 0, lambda i: i // (s // EVEN_TM), g_norm1[0][None, :], w_in_bf,
                              tabs, s // EVEN_TM, EVEN_TM)
    ones = jnp.ones((n_ctx, LANES), F32)
    zeros = jnp.zeros((n_ctx, LANES), F32)
    _, kc, vc, _, _ = _even_in(ctx.reshape(b * n_ctx, d), mods, 0, lambda i: b, g_norm1[0][None, :],
                               w_in_bf, (ones, zeros, zeros), 1, n_ctx)
    bsp_full = jnp.repeat(b_spatial[0].T, HEAD_DIM, axis=1)
    half = N_Q_HEADS // 2
    w_att = w_out_even[0][:Q_DIM].reshape(2, half, HEAD_DIM, d).transpose(1, 0, 2, 3).reshape(Q_DIM, d)
    w_out_bf = jnp.concatenate([w_att, w_out_even[0][Q_DIM:]], axis=0).astype(BF16)
    conv_w8 = jnp.concatenate([conv_w[0], jnp.zeros((8 - conv_w.shape[1], d), F32)], axis=0)
    w_in_odd_bf = w_in_odd[0].astype(BF16)
    w_out_odd_bf = w_out_odd[0].astype(BF16)
    wsp_bf = jnp.concatenate([w_spatial[0][0::2], w_spatial[0][1::2]], axis=-1).astype(BF16)
    wr0, br0 = _router_weights(w_router_group[0], b_router_group[0], w_router_expert[0], b_router_expert[0])
    wr1, br1 = _router_weights(w_router_group[1], b_router_group[1], w_router_expert[1], b_router_expert[1])

    part = n // MOE_PARTS
    out = None
    for p in range(MOE_PARTS):
        tok0 = p * part
        batch0 = tok0 // s
        lat_p, *routed = _even_mix(lat, qx, k, v, kc, vc, u, z, attn_sink[0], g_sgu[0][None, :],
                                   wsp_bf, bsp_full, w_out_bf, mods, s, n_ctx,
                                   g_norm2[0][None, :], wr0, br0, tok0, part)
        moe_rows = _moe_rows(routed, 0, w_gate, w_up, w_down)
        lat_p, *routed = _odd_mix(lat_p, moe_rows, mods, 1, g_norm1[1][None, :], w_in_odd_bf, conv_w8,
                                  w_out_odd_bf, s, g_norm2[1][None, :], wr1, br1, batch0)
        yg, wcol = _moe_rows(routed, 1, w_gate, w_up, w_down)
        out = _combine(lat_p, yg, wcol, mods, 1, batch0, gf, s, n, tok0, out)
    return out.reshape(b, s, d)
```

```python
import functools

import jax
import jax.numpy as jnp
from jax import lax
from jax.experimental import pallas as pl
from jax.experimental.pallas import tpu as pltpu
from jax.experimental.pallas import tpu_sc as plsc

F32 = jnp.float32
BF16 = jnp.bfloat16
HIGHEST = lax.Precision.HIGHEST

GRID_W = 64
N_Q_HEADS = 8
N_KV_HEADS = 2
HEAD_DIM = 64
ATT_BLOCK = 128
ROPE_BASE = 10000.0
Q_DIM = N_Q_HEADS * HEAD_DIM
KV_DIM = N_KV_HEADS * HEAD_DIM
SG_GROUPS = 8
SG_WIDTH = SG_GROUPS * HEAD_DIM
N_GROUPS = 4
EXPERTS_PER_GROUP = 8
N_EXPERTS = N_GROUPS * EXPERTS_PER_GROUP
EPS = 1e-6
NEG_INF = -1e30

LANES = 128
SC_CORES = 2
SC_SUBCORES = 16
SC_WORKERS = SC_CORES * SC_SUBCORES
SC_CHUNK = 32
MOD_ROWS = 8
ROUTER_ROWS = EXPERTS_PER_GROUP + N_EXPERTS
VMEM_LIMIT = 56 * 1024 * 1024

ADA_TN = 1536
EVEN_TM = 1024
EVEN_SUB = 512
ATT_TQ = 1024
ODD_TM = 512
ODD_SUB = 256
ROUTE_SUB = 256
MOE_BM = 1024
MOE_QUANTUM = 128
MOE_PARTS = 2
WEIGHT_SLOTS = 3
WEIGHT_DMA_PRIORITY = 1
COMBINE_TM = 512


def _params(sem):
    return pltpu.CompilerParams(dimension_semantics=sem, vmem_limit_bytes=VMEM_LIMIT)


def _rms_mod(x, g, shift, scale):
    ms = jnp.mean(x * x, axis=-1, keepdims=True)
    return (x * lax.rsqrt(ms + EPS)) * (g * (1.0 + scale)) + shift


def _pack_rounded(a):
    w = a.shape[1] // 2
    hi = pltpu.bitcast(a[:, :w], jnp.uint32)
    lo = pltpu.bitcast(a[:, w:], jnp.uint32)
    return hi | (lo >> 16)


def _pack_rows(a):
    return _pack_rounded(a.astype(BF16).astype(F32))


def _unpack_rows(p):
    hi = pltpu.bitcast(p & jnp.uint32(0xFFFF0000), F32)
    lo = pltpu.bitcast(p << 16, F32)
    return jnp.concatenate([hi, lo], axis=1)


def _ada_kernel(a_ref, w_ref, b_ref, o_ref):
    a = a_ref[...]
    s = a * (1.0 / (1.0 + jnp.exp(-a)))
    o_ref[0] = jnp.dot(s, w_ref[0], preferred_element_type=F32, precision=HIGHEST) + b_ref[0]


def _ada(cond, w_ada, b_ada):
    depth, d, six_d = w_ada.shape
    return pl.pallas_call(
        _ada_kernel,
        grid=(depth, six_d // ADA_TN),
        in_specs=[
            pl.BlockSpec((MOD_ROWS, d), lambda l, j: (0, 0)),
            pl.BlockSpec((1, d, ADA_TN), lambda l, j: (l, 0, j)),
            pl.BlockSpec((1, 1, ADA_TN), lambda l, j: (l, 0, j)),
        ],
        out_specs=pl.BlockSpec((1, MOD_ROWS, ADA_TN), lambda l, j: (l, 0, j)),
        out_shape=jax.ShapeDtypeStruct((depth, MOD_ROWS, six_d), F32),
        compiler_params=_params(("arbitrary", "arbitrary")),
        name="ada",
    )(cond, w_ada, b_ada.reshape(depth, 1, six_d))


def _even_in_kernel(x_ref, sh_ref, sc_ref, g_ref, w_ref, cos_ref, sa_ref, sb_ref,
                    qx_ref, k_ref, v_ref, u_ref, z_ref):
    tm = x_ref.shape[0]
    sub = min(tm, EVEN_SUB)
    scale = HEAD_DIM ** -0.5
    low = lax.broadcasted_iota(jnp.int32, (sub, LANES), 1) < HEAD_DIM
    heads_per_kv = N_Q_HEADS // N_KV_HEADS
    u0 = Q_DIM + 2 * KV_DIM

    for r0 in range(0, tm, sub):
        rows = slice(r0, r0 + sub)
        h = _rms_mod(x_ref[rows, :], g_ref[...], sh_ref[...], sc_ref[...])
        p = jnp.dot(h.astype(BF16), w_ref[...], preferred_element_type=F32)
        cos, sa, sb = cos_ref[rows, :], sa_ref[rows, :], sb_ref[rows, :]

        def rope(t):
            return t * cos + pltpu.roll(t, LANES - 16, 1) * sa + pltpu.roll(t, 16, 1) * sb

        for cblk in range(Q_DIM // LANES):
            t = rope(p[:, cblk * LANES:(cblk + 1) * LANES]) * scale
            sw = pltpu.roll(t, HEAD_DIM, 1)
            zero = jnp.zeros_like(t)
            if (2 * cblk) // heads_per_kv == 0:
                first, second = jnp.where(low, t, zero), jnp.where(low, sw, zero)
            else:
                first, second = jnp.where(low, zero, sw), jnp.where(low, zero, t)
            qx_ref[rows, (2 * cblk) * LANES:(2 * cblk + 1) * LANES] = first.astype(BF16)
            qx_ref[rows, (2 * cblk + 1) * LANES:(2 * cblk + 2) * LANES] = second.astype(BF16)

        k_ref[rows, :] = rope(p[:, Q_DIM:Q_DIM + KV_DIM]).astype(BF16)
        v_ref[rows, :] = p[:, Q_DIM + KV_DIM:Q_DIM + 2 * KV_DIM].astype(BF16)
        u_ref[rows, :] = p[:, u0:u0 + SG_WIDTH]
        z_ref[rows, :] = p[:, u0 + SG_WIDTH:u0 + 2 * SG_WIDTH]


def _even_in(x2d, mods, layer, mod_row_fn, g, w_bf, tabs, tab_blocks, tm):
    n, d = x2d.shape
    ein = w_bf.shape[1]
    cos, sa, sb = tabs
    row = lambda i: layer * MOD_ROWS + mod_row_fn(i)
    tab_spec = pl.BlockSpec((tm, LANES), lambda i: (i % tab_blocks, 0))
    qx_dim = N_Q_HEADS * LANES
    return pl.pallas_call(
        _even_in_kernel,
        grid=(n // tm,),
        in_specs=[
            pl.BlockSpec((tm, d), lambda i: (i, 0)),
            pl.BlockSpec((None, 1, d), lambda i: (row(i), 0, 0)),
            pl.BlockSpec((None, 1, d), lambda i: (row(i), 0, 1)),
            pl.BlockSpec((1, d), lambda i: (0, 0)),
            pl.BlockSpec((d, ein), lambda i: (0, 0)),
            tab_spec, tab_spec, tab_spec,
        ],
        out_specs=[
            pl.BlockSpec((tm, qx_dim), lambda i: (i, 0)),
            pl.BlockSpec((tm, KV_DIM), lambda i: (i, 0)),
            pl.BlockSpec((tm, KV_DIM), lambda i: (i, 0)),
            pl.BlockSpec((tm, SG_WIDTH), lambda i: (i, 0)),
            pl.BlockSpec((tm, SG_WIDTH), lambda i: (i, 0)),
        ],
        out_shape=[
            jax.ShapeDtypeStruct((n, qx_dim), BF16),
            jax.ShapeDtypeStruct((n, KV_DIM), BF16),
            jax.ShapeDtypeStruct((n, KV_DIM), BF16),
            jax.ShapeDtypeStruct((n, SG_WIDTH), F32),
            jax.ShapeDtypeStruct((n, SG_WIDTH), F32),
        ],
        compiler_params=_params(("parallel",)),
        name="even_in",
    )(x2d, mods, mods, g, w_bf, cos, sa, sb)


def _gelu(x):
    return 0.5 * x * (1.0 + lax.erf(x * (2.0 ** -0.5)))


def _even_mix_kernel(sink_ref, lat_ref, qx_ref, km_ref, kp_ref, kn_ref, vm_ref, vp_ref, vn_ref,
                     kc_ref, vc_ref, u_ref, z_ref, gsgu_ref, wsp_ref, bsp_ref, wout_ref, gate_ref,
                     sh2_ref, sc2_ref, g2_ref, wr_ref, br_ref,
                     o_ref, h_ref, mi_ref, wc_ref, cnt_ref,
                     kband, vband, mixin, carry, upper, *, tiles_per_seq):
    i = pl.program_id(0)
    tq = qx_ref.shape[0]

    @pl.when(i == 0)
    def _():
        _route_init(carry, upper)

    blk = ATT_BLOCK
    nsub = tq // blk
    n_ctx = kc_ref.shape[0]
    first = (i % tiles_per_seq) == 0
    last = (i % tiles_per_seq) == tiles_per_seq - 1

    kband[0:blk] = kp_ref[...]
    kband[blk:blk + tq] = km_ref[...]
    kband[blk + tq:] = kn_ref[...]
    vband[0:blk] = vp_ref[...]
    vband[blk:blk + tq] = vm_ref[...]
    vband[blk + tq:] = vn_ref[...]

    rows = N_Q_HEADS * blk
    tok = lax.broadcasted_iota(jnp.int32, (rows, blk), 0) & (blk - 1)
    col = lax.broadcasted_iota(jnp.int32, (rows, blk), 1)
    tri_prev = col >= tok
    tri_next = col <= tok
    head = lax.broadcasted_iota(jnp.int32, (rows, 1), 0) // blk
    sink_col = jnp.zeros((rows, 1), F32)
    for hd in range(N_Q_HEADS):
        sink_col = jnp.where(head == hd, sink_ref[hd], sink_col)
    lane_low = lax.broadcasted_iota(jnp.int32, (blk, LANES), 1) < HEAD_DIM
    ones = jnp.ones((n_ctx + 3 * blk, LANES), BF16)
    nt = (((1,), (1,)), ((), ()))

    def sub_block(j, c):
        r0 = pl.multiple_of(j * blk, blk)
        ok_prev = jnp.logical_not(jnp.logical_and(first, j == 0))
        ok_next = jnp.logical_not(jnp.logical_and(last, j == nsub - 1))
        qs = jnp.concatenate([qx_ref[pl.ds(r0, blk), hd * LANES:(hd + 1) * LANES]
                              for hd in range(N_Q_HEADS)], axis=0)
        kall = jnp.concatenate([kc_ref[...], kband[pl.ds(r0, 3 * blk), :]], axis=0)
        vall = jnp.concatenate([vc_ref[...], vband[pl.ds(r0, 3 * blk), :]], axis=0)
        s = lax.dot_general(qs, kall, nt, preferred_element_type=F32)
        c0 = n_ctx
        s = jnp.concatenate([
            s[:, :c0],
            jnp.where(jnp.logical_and(tri_prev, ok_prev), s[:, c0:c0 + blk], NEG_INF),
            s[:, c0 + blk:c0 + 2 * blk],
            jnp.where(jnp.logical_and(tri_next, ok_next), s[:, c0 + 2 * blk:], NEG_INF),
        ], axis=1)
        m = jnp.maximum(jnp.max(s, axis=-1, keepdims=True), sink_col)
        p = jnp.exp(s - m).astype(BF16)
        o = jnp.dot(p, jnp.concatenate([vall, ones], axis=1), preferred_element_type=F32)
        att = o[:, :LANES] / (o[:, LANES:] + jnp.exp(sink_col - m))
        half = N_Q_HEADS // 2
        for hd in range(half):
            pair = jnp.where(lane_low, att[hd * blk:(hd + 1) * blk], att[(hd + half) * blk:(hd + half + 1) * blk])
            mixin[pl.ds(r0, blk), hd * LANES:(hd + 1) * LANES] = pair.astype(BF16)

        ug = _gelu(u_ref[pl.ds(r0, blk), :])
        zg = _gelu(z_ref[pl.ds(r0, blk), :])
        mu = jnp.mean(zg, axis=-1, keepdims=True)
        zc = zg - mu
        zn = zc * lax.rsqrt(jnp.mean(zc * zc, axis=-1, keepdims=True) + EPS) * gsgu_ref[...]
        for pair in range(SG_GROUPS // 2):
            zp = zn[:, pair * LANES:(pair + 1) * LANES]
            zero = jnp.zeros_like(zp)
            lo = jnp.where(lane_low, zp, zero).astype(BF16)
            hi = jnp.where(lane_low, zero, zp).astype(BF16)
            sg = (jnp.dot(wsp_ref[pair], jnp.concatenate([lo, hi], axis=0), preferred_element_type=F32)
                  + bsp_ref[:, pair * LANES:(pair + 1) * LANES])
            mixin[pl.ds(r0, blk), Q_DIM + pair * LANES:Q_DIM + (pair + 1) * LANES] = (
                ug[:, pair * LANES:(pair + 1) * LANES] * sg).astype(BF16)
        return c

    lax.fori_loop(0, nsub, sub_block, 0, unroll=True)
    for r0 in range(0, tq, ROUTE_SUB):
        rows = slice(r0, r0 + ROUTE_SUB)
        mix = jnp.dot(mixin[rows, :], wout_ref[...], preferred_element_type=F32)
        lat = lat_ref[rows, :] + gate_ref[...] * mix
        o_ref[rows, :] = lat
        _route_tile(lat, sh2_ref[...], sc2_ref[...], g2_ref[...], wr_ref, br_ref, carry, upper,
                    h_ref.at[rows, :], mi_ref.at[:, rows], wc_ref.at[rows, :], cnt_ref)


def _even_mix(lat, qx, k, v, kc, vc, u, z, sink, g_sgu, wsp_bf, bsp_full, wout_bf, mods, seq, ctx_len,
              g2, wr_t, br_t, tok0, n):
    d = lat.shape[1]
    tq = ATT_TQ
    tiles_per_seq = seq // tq
    sub = tq // ATT_BLOCK
    nblk = lat.shape[0] // ATT_BLOCK
    t0 = tok0 // tq
    main = lambda w: pl.BlockSpec((tq, w), lambda i: (t0 + i, 0))
    prev = pl.BlockSpec((ATT_BLOCK, KV_DIM), lambda i: (jnp.maximum((t0 + i) * sub - 1, 0), 0))
    nxt = pl.BlockSpec((ATT_BLOCK, KV_DIM), lambda i: (jnp.minimum((t0 + i + 1) * sub, nblk - 1), 0))
    batch = lambda t: (t0 + t) // tiles_per_seq
    ctxs = pl.BlockSpec((ctx_len, KV_DIM), lambda i: (batch(i), 0))
    const = lambda shape: pl.BlockSpec(shape, lambda i: (0,) * len(shape), pipeline_mode=pl.Buffered(1))
    r_in, r_out, r_shapes, r_scratch = _route_specs(n, d, tq, lambda i: i, batch, ROUTE_SUB)
    return pl.pallas_call(
        functools.partial(_even_mix_kernel, tiles_per_seq=tiles_per_seq),
        grid=(n // tq,),
        in_specs=[
            pl.BlockSpec(memory_space=pltpu.SMEM),
            main(d), main(qx.shape[1]),
            main(KV_DIM), prev, nxt,
            main(KV_DIM), prev, nxt,
            ctxs, ctxs,
            main(SG_WIDTH), main(SG_WIDTH),
            const((1, SG_WIDTH)), const(wsp_bf.shape), const(bsp_full.shape), const(wout_bf.shape),
            pl.BlockSpec((None, 1, d), lambda i: (batch(i), 0, 2)),
        ] + r_in,
        out_specs=[pl.BlockSpec((tq, d), lambda i: (i, 0))] + r_out,
        out_shape=[jax.ShapeDtypeStruct((n, d), F32)] + r_shapes,
        scratch_shapes=[
            pltpu.VMEM((tq + 2 * ATT_BLOCK, KV_DIM), BF16),
            pltpu.VMEM((tq + 2 * ATT_BLOCK, KV_DIM), BF16),
            pltpu.VMEM((tq, Q_DIM + SG_WIDTH), BF16),
        ] + r_scratch,
        compiler_params=_params(("arbitrary",)),
        name="even_mix",
    )(sink, lat, qx, k, k, k, v, v, v, kc, vc, u, z, g_sgu, wsp_bf, bsp_full, wout_bf, mods,
      mods, mods, g2, wr_t, br_t)


def _odd_mix_kernel(x_ref, e0_ref, e1_ref, ew_ref, egate_ref, sh_ref, sc_ref, gate_ref, g_ref,
                    win_ref, cw_ref, wout_ref, sh2_ref, sc2_ref, g2_ref, wr_ref, br_ref,
                    o_ref, h_ref, mi_ref, wc_ref, cnt_ref,
                    y_s, bg_s, tail_s, x_s, carry, upper, *, tiles_per_seq):
    i = pl.program_id(0)
    n_tiles = pl.num_programs(0) - 1
    tm, d = x_ref.shape
    cur = i % 2
    prv = 1 - cur

    @pl.when(i == 0)
    def _():
        _route_init(carry, upper)

    tail_s[...] = y_s[cur, tm - 8:tm, :]

    @pl.when(i < n_tiles)
    def _():
        for r0 in range(0, tm, ODD_SUB):
            rows = slice(r0, r0 + ODD_SUB)
            ew = ew_ref[rows, :]
            moe = ew[:, 0:1] * _unpack_rows(e0_ref[rows, :]) + ew[:, 1:2] * _unpack_rows(e1_ref[rows, :])
            x = x_ref[rows, :] + egate_ref[...] * moe
            x_s[cur, rows, :] = x
            h = _rms_mod(x, g_ref[...], sh_ref[...], sc_ref[...])
            p = jnp.dot(h.astype(BF16), win_ref[...], preferred_element_type=F32)
            bg_s[cur, rows, :] = p[:, 0:d]
            y_s[cur, rows, :] = p[:, d:2 * d] * p[:, 2 * d:3 * d]

    @pl.when(i >= 1)
    def _():
        t_prev = i - 1
        first = (t_prev % tiles_per_seq) == 0
        last = (t_prev % tiles_per_seq) == tiles_per_seq - 1
        y = y_s[prv]
        left = jnp.where(first, 0.0, tail_s[7:8, :])
        right = jnp.where(last, 0.0, y_s[cur, 0:1, :])
        ridx = lax.broadcasted_iota(jnp.int32, (tm, d), 0)
        y_dn = jnp.where(ridx == 0, left, pltpu.roll(y, 1, 0))
        y_up = jnp.where(ridx == tm - 1, right, pltpu.roll(y, tm - 1, 0))
        conv = y_dn * cw_ref[0:1, :] + y * cw_ref[1:2, :] + y_up * cw_ref[2:3, :]
        mix = jnp.dot((bg_s[prv] * conv).astype(BF16), wout_ref[...], preferred_element_type=F32)
        lat = x_s[prv] + gate_ref[...] * mix
        o_ref[...] = lat
        _route_tile(lat, sh2_ref[...], sc2_ref[...], g2_ref[...], wr_ref, br_ref, carry, upper,
                    h_ref, mi_ref, wc_ref, cnt_ref)


def _odd_mix(lat, moe_rows, mods, layer, g, win_bf, conv_w8, wout_bf, seq, g2, wr_t, br_t, batch0):
    yg, ewcol = moe_rows
    n, d = lat.shape
    tm = ODD_TM
    nt = n // tm
    tiles_per_seq = seq // tm
    cur = lambda i: jnp.minimum(i, nt - 1)
    prv = lambda i: jnp.maximum(i - 1, 0)
    row = lambda t: layer * MOD_ROWS + batch0 + t // tiles_per_seq
    const = lambda shape: pl.BlockSpec(shape, lambda i: (0,) * len(shape), pipeline_mode=pl.Buffered(1))
    r_in, r_out, r_shapes, r_scratch = _route_specs(n, d, tm, prv, row, tm)
    return pl.pallas_call(
        functools.partial(_odd_mix_kernel, tiles_per_seq=tiles_per_seq),
        grid=(nt + 1,),
        in_specs=[
            pl.BlockSpec((tm, d), lambda i: (cur(i), 0)),
            pl.BlockSpec((tm, d // 2), lambda i: (cur(i), 0)),
            pl.BlockSpec((tm, d // 2), lambda i: (nt + cur(i), 0)),
            pl.BlockSpec((tm, LANES), lambda i: (cur(i), 0)),
            pl.BlockSpec((None, 1, d), lambda i: (row(cur(i)) - MOD_ROWS, 0, 5)),
            pl.BlockSpec((None, 1, d), lambda i: (row(cur(i)), 0, 0)),
            pl.BlockSpec((None, 1, d), lambda i: (row(cur(i)), 0, 1)),
            pl.BlockSpec((None, 1, d), lambda i: (row(prv(i)), 0, 2)),
            const((1, d)), const(win_bf.shape), const(conv_w8.shape), const(wout_bf.shape),
        ] + r_in,
        out_specs=[pl.BlockSpec((tm, d), lambda i: (prv(i), 0))] + r_out,
        out_shape=[jax.ShapeDtypeStruct((n, d), F32)] + r_shapes,
        scratch_shapes=[
            pltpu.VMEM((2, tm, d), F32),
            pltpu.VMEM((2, tm, d), F32),
            pltpu.VMEM((8, d), F32),
            pltpu.VMEM((2, tm, d), F32),
        ] + r_scratch,
        compiler_params=_params(("arbitrary",)),
        name="odd_mix",
    )(lat, yg, yg, ewcol, mods, mods, mods, mods, g, win_bf, conv_w8, wout_bf, mods, mods, g2, wr_t, br_t)


def _route_init(carry, upper):
    tm = upper.shape[0]
    carry[...] = jnp.zeros_like(carry)
    r_i = lax.broadcasted_iota(jnp.int32, (tm, tm), 0)
    c_i = lax.broadcasted_iota(jnp.int32, (tm, tm), 1)
    upper[...] = jnp.where(r_i < c_i, 1.0, 0.0).astype(BF16)


def _route_tile(lat, sh, sc, g, wr_ref, br_ref, carry, upper, h_ref, mi_ref, wc_ref, cnt_ref):
    tm = lat.shape[0]
    epg = EXPERTS_PER_GROUP
    h = _rms_mod(lat, g, sh, sc)
    h_hi = h.astype(BF16)
    h_hi_f = h_hi.astype(F32)
    h_ref[...] = _pack_rounded(h_hi_f)
    h_lo = (h - h_hi_f).astype(BF16)
    w = wr_ref[...]
    w1 = w.astype(BF16).astype(F32)
    r1 = w - w1
    w2 = r1.astype(BF16).astype(F32)
    w3 = r1 - w2
    nt = (((1,), (1,)), ((), ()))
    nr = w.shape[0]
    w123 = jnp.concatenate([w1, w2, w3, jnp.zeros((8, w.shape[1]), F32)], axis=0).astype(BF16)
    w12 = jnp.concatenate([w1, w2], axis=0).astype(BF16)
    p_hi = lax.dot_general(w123, h_hi, nt, preferred_element_type=F32)
    p_lo = lax.dot_general(w12, h_lo, nt, preferred_element_type=F32)
    lg = ((p_hi[2 * nr:3 * nr] + p_lo[nr:2 * nr]) + (p_hi[nr:2 * nr] + p_lo[0:nr])) + p_hi[0:nr] + br_ref[...]
    io8 = lax.broadcasted_iota(jnp.int32, (epg, tm), 0)
    gl = lg[0:epg]
    gmax = jnp.max(gl, axis=0, keepdims=True)
    g_idx = jnp.min(jnp.where(gl == gmax, io8, epg), axis=0, keepdims=True)
    g_w = 1.0 / jnp.sum(jnp.exp(gl - gmax), axis=0, keepdims=True)
    e_sel = lg[epg:2 * epg]
    for gi in range(1, N_GROUPS):
        e_sel = jnp.where(g_idx == gi, lg[(gi + 1) * epg:(gi + 2) * epg], e_sel)
    v0 = jnp.max(e_sel, axis=0, keepdims=True)
    i0 = jnp.min(jnp.where(e_sel == v0, io8, epg), axis=0, keepdims=True)
    rest = jnp.where(io8 == i0, -jnp.inf, e_sel)
    v1 = jnp.max(rest, axis=0, keepdims=True)
    i1 = jnp.min(jnp.where(rest == v1, io8, epg), axis=0, keepdims=True)
    t = jnp.exp(v1 - v0)
    w0 = g_w / (1.0 + t)
    w1 = g_w * t / (1.0 + t)
    e0 = g_idx * epg + i0
    e1 = g_idx * epg + i1

    io32 = lax.broadcasted_iota(jnp.int32, (N_EXPERTS, tm), 0)
    hit0 = io32 == e0
    hit1 = io32 == e1
    onehot = jnp.where(hit0 | hit1, 1.0, 0.0)
    cum = jnp.dot(onehot.astype(BF16), upper[...], preferred_element_type=F32) + carry[...]
    rank0 = jnp.sum(jnp.where(hit0, cum, 0.0), axis=0, keepdims=True).astype(jnp.int32)
    rank1 = jnp.sum(jnp.where(hit1, cum, 0.0), axis=0, keepdims=True).astype(jnp.int32)
    carry[...] = carry[...] + jnp.sum(onehot, axis=1, keepdims=True)
    cnt_ref[...] = jnp.broadcast_to(carry[...], cnt_ref.shape)

    mi_ref[...] = jnp.where(io8 == 0, e0, jnp.where(io8 == 1, e1, jnp.where(io8 == 2, rank0,
                            jnp.where(io8 == 3, rank1, 0))))
    io128 = lax.broadcasted_iota(jnp.int32, (LANES, tm), 0)
    wrow = jnp.where(io128 == 0, w0, jnp.where(io128 == 1, w1, 0.0))
    wc_ref[...] = wrow.T


def _route_specs(n, d, tm, tile_of, mods_row, route_rows):
    in_specs = [
        pl.BlockSpec((None, 1, d), lambda i: (mods_row(tile_of(i)), 0, 3)),
        pl.BlockSpec((None, 1, d), lambda i: (mods_row(tile_of(i)), 0, 4)),
        pl.BlockSpec((1, d), lambda i: (0, 0), pipeline_mode=pl.Buffered(1)),
        pl.BlockSpec((ROUTER_ROWS, d), lambda i: (0, 0), pipeline_mode=pl.Buffered(1)),
        pl.BlockSpec((ROUTER_ROWS, 1), lambda i: (0, 0), pipeline_mode=pl.Buffered(1)),
    ]
    out_specs = [
        pl.BlockSpec((tm, d // 2), lambda i: (tile_of(i), 0)),
        pl.BlockSpec((8, tm), lambda i: (0, tile_of(i))),
        pl.BlockSpec((tm, LANES), lambda i: (tile_of(i), 0)),
        pl.BlockSpec((N_EXPERTS, LANES), lambda i: (0, 0)),
    ]
    out_shapes = [
        jax.ShapeDtypeStruct((n, d // 2), jnp.uint32),
        jax.ShapeDtypeStruct((8, n), jnp.int32),
        jax.ShapeDtypeStruct((n, LANES), F32),
        jax.ShapeDtypeStruct((N_EXPERTS, LANES), F32),
    ]
    scratch = [pltpu.VMEM((N_EXPERTS, 1), F32), pltpu.VMEM((route_rows, route_rows), BF16)]
    return in_specs, out_specs, out_shapes, scratch


def _plan_kernel(cnt_ref, mi_ref, dest_ref, be_ref, runs_ref, nv_ref, nu_ref, ps_ref, *, n_blocks):
    bm = MOE_BM

    def per_expert(e, carry):
        blk0, n_runs = carry
        cnt = cnt_ref[e]
        nb = (cnt + bm - 1) // bm
        ps_ref[e] = blk0 * bm

        def fill(b, c):
            be_ref[b] = e
            nv_ref[b] = jnp.minimum(cnt - (b - blk0) * bm, bm)
            return c

        lax.fori_loop(blk0, blk0 + nb, fill, 0)

        @pl.when(nb > 0)
        def _():
            runs_ref[n_runs] = e

        return blk0 + nb, n_runs + jnp.where(nb > 0, 1, 0)

    n_used, n_runs = lax.fori_loop(0, N_EXPERTS, per_expert, (0, 0))
    nu_ref[0] = n_used
    nu_ref[1] = n_runs
    last_e = be_ref[jnp.maximum(n_used - 1, 0)]

    def fill_tail(b, c):
        be_ref[b] = last_e
        nv_ref[b] = 0
        return c

    lax.fori_loop(n_used, n_blocks, fill_tail, 0)

    def fill_runs(k, c):
        runs_ref[k] = last_e
        return c

    lax.fori_loop(n_runs, N_EXPERTS, fill_runs, 0)

    e01 = mi_ref[0:2, :]
    dest = mi_ref[2:4, :]
    for e in range(N_EXPERTS):
        dest = dest + jnp.where(e01 == e, ps_ref[e], 0)
    dest_ref[...] = dest


def _plan(counts, meta_i, n_blocks):
    n = meta_i.shape[1]
    return pl.pallas_call(
        functools.partial(_plan_kernel, n_blocks=n_blocks),
        in_specs=[pl.BlockSpec(memory_space=pltpu.SMEM), pl.BlockSpec(memory_space=pltpu.VMEM)],
        out_specs=[pl.BlockSpec(memory_space=pltpu.VMEM)] + [pl.BlockSpec(memory_space=pltpu.SMEM)] * 4,
        out_shape=[
            jax.ShapeDtypeStruct((2, n), jnp.int32),
            jax.ShapeDtypeStruct((n_blocks,), jnp.int32),
            jax.ShapeDtypeStruct((N_EXPERTS,), jnp.int32),
            jax.ShapeDtypeStruct((n_blocks,), jnp.int32),
            jax.ShapeDtypeStruct((2,), jnp.int32),
        ],
        scratch_shapes=[pltpu.SMEM((N_EXPERTS,), jnp.int32)],
        compiler_params=pltpu.CompilerParams(vmem_limit_bytes=VMEM_LIMIT),
        name="plan",
    )(counts, meta_i)


def _sc_mesh():
    return plsc.VectorSubcoreMesh(core_axis_name="c", subcore_axis_name="s",
                                  num_cores=SC_CORES, num_subcores=SC_SUBCORES)


def _sc_worker():
    return lax.axis_index("s") * SC_CORES + lax.axis_index("c")


def _sc_dispatch(h2, dest, n_rows):
    n, d = h2.shape
    c = SC_CHUNK
    per_w = n // SC_WORKERS
    nchunk = per_w // c
    idx = dest.reshape(2, SC_WORKERS, nchunk, c)

    @functools.partial(
        pl.kernel, mesh=_sc_mesh(), out_type=jax.ShapeDtypeStruct((n_rows, d), h2.dtype),
        scratch_types=[pltpu.VMEM((nchunk, c), jnp.int32), pltpu.VMEM((nchunk, c), jnp.int32),
                       pltpu.VMEM((2, c, d), h2.dtype),
                       pltpu.SemaphoreType.DMA((2,)), pltpu.SemaphoreType.DMA((2,))])
    def k(h_hbm, idx_hbm, xb_hbm, idx0_v, idx1_v, rows_v, gsem, ssem):
        wid = _sc_worker()
        base = wid * per_w
        idx_v = (idx0_v, idx1_v)
        for kk in range(2):
            pltpu.sync_copy(idx_hbm.at[kk, wid], idx_v[kk])

        def get(j, slot):
            return pltpu.make_async_copy(h_hbm.at[pl.ds(base + j * c, c)], rows_v.at[slot], gsem.at[slot])

        def put(j, slot, kk):
            return pltpu.make_async_copy(rows_v.at[slot], xb_hbm.at[idx_v[kk].at[j]], ssem.at[slot])

        get(0, 0).start()

        @pl.loop(0, nchunk, step=2)
        def _(j):
            for slot in range(2):
                jj = j + slot
                get(jj, slot).wait()

                @pl.when(jj >= 1)
                def _():
                    for kk in range(2):
                        put(jj - 1, 1 - slot, kk).wait()

                @pl.when(jj + 1 < nchunk)
                def _():
                    get(jj + 1, 1 - slot).start()

                for kk in range(2):
                    put(jj, slot, kk).start()

        for kk in range(2):
            put(nchunk - 1, (nchunk - 1) % 2, kk).wait()

    return k(h2, idx)


def _sc_gather(y, dest):
    d = y.shape[1]
    total = dest.shape[0] * dest.shape[1]
    c = SC_CHUNK
    per_w = total // SC_WORKERS
    nchunk = per_w // c
    idx = dest.reshape(SC_WORKERS, nchunk, c)

    @functools.partial(
        pl.kernel, mesh=_sc_mesh(), out_type=jax.ShapeDtypeStruct((total, d), y.dtype),
        scratch_types=[pltpu.VMEM((nchunk, c), jnp.int32), pltpu.VMEM((2, c, d), y.dtype),
                       pltpu.SemaphoreType.DMA((2,)), pltpu.SemaphoreType.DMA((2,))])
    def k(y_hbm, idx_hbm, out_hbm, idx_v, rows_v, gsem, ssem):
        wid = _sc_worker()
        base = wid * per_w
        pltpu.sync_copy(idx_hbm.at[wid], idx_v)

        def get(j, slot):
            return pltpu.make_async_copy(y_hbm.at[idx_v.at[j]], rows_v.at[slot], gsem.at[slot])

        def put(j, slot):
            return pltpu.make_async_copy(rows_v.at[slot], out_hbm.at[pl.ds(base + j * c, c)], ssem.at[slot])

        get(0, 0).start()

        @pl.loop(0, nchunk, step=2)
        def _(j):
            for slot in range(2):
                jj = j + slot
                get(jj, slot).wait()

                @pl.when(jj >= 1)
                def _():
                    put(jj - 1, 1 - slot).wait()

                @pl.when(jj + 1 < nchunk)
                def _():
                    get(jj + 1, 1 - slot).start()

                put(jj, slot).start()

        put(nchunk - 1, (nchunk - 1) % 2).wait()

    return k(y, idx)


def _expert_kernel(be_ref, runs_ref, nv_ref, nu_ref, x_ref, wg_hbm, wu_hbm, wd_hbm, y_ref,
                   wgu_s, wd_s, stg_g, stg_u, stg_d, run_s, sems, *, layer):
    b = pl.program_id(0)
    hid = stg_g.shape[2]
    e = be_ref[b]
    n_runs = nu_ref[1]
    changed = jnp.logical_or(b == 0, e != be_ref[jnp.maximum(b - 1, 0)])

    def fetch(run):
        expert = runs_ref[run]
        slot = run % WEIGHT_SLOTS
        return (pltpu.make_async_copy(wg_hbm.at[layer, expert], stg_g.at[slot], sems.at[slot]),
                pltpu.make_async_copy(wu_hbm.at[layer, expert], stg_u.at[slot], sems.at[slot]),
                pltpu.make_async_copy(wd_hbm.at[layer, expert], stg_d.at[slot], sems.at[slot]))

    @pl.when(b == 0)
    def _():
        for r in range(WEIGHT_SLOTS - 1):
            @pl.when(r < n_runs)
            def _():
                for cp in fetch(r):
                    cp.start(priority=WEIGHT_DMA_PRIORITY)

    @pl.when(changed)
    def _():
        run = jnp.where(b == 0, 0, run_s[0] + 1)
        run_s[0] = run
        ahead = run + WEIGHT_SLOTS - 1

        @pl.when(ahead < n_runs)
        def _():
            for cp in fetch(ahead):
                cp.start(priority=WEIGHT_DMA_PRIORITY)

        for cp in fetch(run):
            cp.wait()

        slot = run % WEIGHT_SLOTS
        wgu_s[:, 0:hid] = stg_g[slot].astype(BF16)
        wgu_s[:, hid:2 * hid] = stg_u[slot].astype(BF16)
        wd_s[...] = stg_d[slot].astype(BF16)

    bm, dp = x_ref.shape
    nv = nv_ref[b]
    in_use = b < nu_ref[0]

    def run(rows):
        live = lax.broadcasted_iota(jnp.int32, (rows, dp), 0) < nv
        x = _unpack_rows(jnp.where(live, x_ref[0:rows, :], jnp.uint32(0)))
        gu = jnp.dot(x.astype(BF16), wgu_s[...], preferred_element_type=F32)
        gate = gu[:, 0:hid]
        act = gate * (1.0 / (1.0 + jnp.exp(-gate))) * gu[:, hid:2 * hid]
        y_ref[0:rows, :] = _pack_rows(jnp.dot(act.astype(BF16), wd_s[...], preferred_element_type=F32))

    n_quanta = bm // MOE_QUANTUM
    for q in range(1, n_quanta + 1):
        rows = q * MOE_QUANTUM

        @pl.when(jnp.logical_and(in_use, jnp.logical_and(nv > rows - MOE_QUANTUM, nv <= rows)))
        def _(rows=rows):
            run(rows)
            if rows < bm:
                y_ref[rows:bm, :] = jnp.zeros((bm - rows, dp), y_ref.dtype)

    @pl.when(jnp.logical_not(in_use))
    def _():
        y_ref[...] = jnp.zeros_like(y_ref)


def _experts(block_e, runs, n_valid, n_used, xb, w_gate, w_up, w_down, layer):
    n_rows, dp = xb.shape
    d, hid = w_gate.shape[2], w_gate.shape[3]
    bm = MOE_BM
    n_blocks = n_rows // bm
    hbm = pl.BlockSpec(memory_space=pl.ANY)
    return pl.pallas_call(
        functools.partial(_expert_kernel, layer=layer),
        grid_spec=pltpu.PrefetchScalarGridSpec(
            num_scalar_prefetch=4,
            grid=(n_blocks,),
            in_specs=[
                pl.BlockSpec((bm, dp), lambda b, be, nx, nv, nu: (jnp.minimum(b, nu[0] - 1), 0)),
                hbm, hbm, hbm,
            ],
            out_specs=pl.BlockSpec((bm, dp), lambda b, be, nx, nv, nu: (b, 0)),
            scratch_shapes=[
                pltpu.VMEM((d, 2 * hid), BF16), pltpu.VMEM((hid, d), BF16),
                pltpu.VMEM((WEIGHT_SLOTS, d, hid), F32), pltpu.VMEM((WEIGHT_SLOTS, d, hid), F32),
                pltpu.VMEM((WEIGHT_SLOTS, hid, d), F32),
                pltpu.SMEM((1,), jnp.int32), pltpu.SemaphoreType.DMA((WEIGHT_SLOTS,)),
            ],
        ),
        out_shape=jax.ShapeDtypeStruct((n_rows, dp), jnp.uint32),
        compiler_params=_params(("arbitrary",)),
        name="experts",
    )(block_e, runs, n_valid, n_used, xb, w_gate, w_up, w_down)


def _combine_kernel(lat_ref, y0_ref, y1_ref, wc_ref, gate_ref, gf_ref, *rest):
    o_ref = rest[-1]
    wc = wc_ref[...]
    moe = wc[:, 0:1] * _unpack_rows(y0_ref[...]) + wc[:, 1:2] * _unpack_rows(y1_ref[...])
    out = lat_ref[...] + gate_ref[...] * moe
    ms = jnp.mean(out * out, axis=-1, keepdims=True)
    o_ref[...] = out * lax.rsqrt(ms + EPS) * gf_ref[...]


def _combine(lat, yg, wcol, mods, layer, batch0, g_final, seq, out_rows, tok0, prev_out):
    n, d = lat.shape
    tm = COMBINE_TM
    nt = n // tm
    t0 = tok0 // tm
    tiles_per_seq = seq // tm
    row = lambda i: layer * MOD_ROWS + batch0 + i // tiles_per_seq
    in_specs = [
        pl.BlockSpec((tm, d), lambda i: (i, 0)),
        pl.BlockSpec((tm, d // 2), lambda i: (i, 0)),
        pl.BlockSpec((tm, d // 2), lambda i: (nt + i, 0)),
        pl.BlockSpec((tm, LANES), lambda i: (i, 0)),
        pl.BlockSpec((None, 1, d), lambda i: (row(i), 0, 5)),
        pl.BlockSpec((1, d), lambda i: (0, 0)),
    ]
    args = [lat, yg, yg, wcol, mods, g_final]
    aliases = {}
    if prev_out is not None:
        in_specs.append(pl.BlockSpec(memory_space=pl.ANY))
        args.append(prev_out)
        aliases = {len(args) - 1: 0}
    return pl.pallas_call(
        _combine_kernel,
        grid=(nt,),
        in_specs=in_specs,
        out_specs=pl.BlockSpec((tm, d), lambda i: (t0 + i, 0)),
        out_shape=jax.ShapeDtypeStruct((out_rows, d), F32),
        input_output_aliases=aliases,
        compiler_params=_params(("parallel",)),
        name="combine",
    )(*args)


def _moe_rows(routed, layer, w_gate, w_up, w_down):
    h2, meta_i, wcol, counts = routed
    n = h2.shape[0]
    n_blocks = (2 * n) // MOE_BM + N_EXPERTS
    dest, block_e, runs, n_valid, n_used = _plan(counts[:, 0].astype(jnp.int32), meta_i, n_blocks)
    xb = _sc_dispatch(h2, dest, n_blocks * MOE_BM)
    yb = _experts(block_e, runs, n_valid, n_used, xb, w_gate, w_up, w_down, layer)
    return _sc_gather(yb, dest), wcol


def _rope_tables(seq):
    quarter = HEAD_DIM // 4
    pos = jnp.arange(seq, dtype=F32)
    row_ids = jnp.floor(pos / GRID_W)
    col_ids = pos - row_ids * GRID_W
    inv = ROPE_BASE ** (-jnp.arange(quarter, dtype=F32) / quarter)
    ang_r = row_ids[:, None] * inv
    ang_c = col_ids[:, None] * inv
    zero = jnp.zeros_like(ang_r)
    cos = jnp.concatenate([jnp.cos(ang_r), jnp.cos(ang_r), jnp.cos(ang_c), jnp.cos(ang_c)], axis=-1)
    sa = jnp.concatenate([-jnp.sin(ang_r), zero, -jnp.sin(ang_c), zero], axis=-1)
    sb = jnp.concatenate([zero, jnp.sin(ang_r), zero, jnp.sin(ang_c)], axis=-1)
    rep = LANES // HEAD_DIM
    return tuple(jnp.tile(t, (1, rep)) for t in (cos, sa, sb))


def _router_weights(w_rg, b_rg, w_re, b_re):
    d = w_rg.shape[0]
    pad = EXPERTS_PER_GROUP - N_GROUPS
    wr_t = jnp.concatenate([w_rg.T, jnp.zeros((pad, d), F32), w_re.T], axis=0)
    br_t = jnp.concatenate([b_rg, jnp.full((pad,), NEG_INF, F32), b_re])[:, None]
    return wr_t, br_t


def kernel(x, c, ctx, c_ctx, w_ada, b_ada, g_norm1, g_norm2, g_final, w_in_even, attn_sink, g_sgu,
           w_spatial, b_spatial, w_out_even, w_in_odd, conv_w, w_out_odd, w_router_group,
           b_router_group, w_router_expert, b_router_expert, w_gate, w_up, w_down):
    b, s, d = x.shape
    n = b * s
    n_ctx = ctx.shape[1]
    depth = w_ada.shape[0]
    assert depth == 2 and b + 1 <= MOD_ROWS

    cond = jnp.concatenate([c, c_ctx[None, :], jnp.zeros((MOD_ROWS - b - 1, d), F32)], axis=0)
    mods = _ada(cond, w_ada, b_ada).reshape(depth * MOD_ROWS, 1, 6 * d)
    gf = g_final[None, :]

    lat = x.reshape(n, d)
    w_in_bf = w_in_even[0].astype(BF16)
    tabs = _rope_tables(s)
    qx, k, v, u, z = _even_in(lat, mods, 0, lambda i: i // (s // EVEN_TM), g_norm1[0][None, :], w_in_bf,
                              tabs, s // EVEN_TM, EVEN_TM)
    ones = jnp.ones((n_ctx, LANES), F32)
    zeros = jnp.zeros((n_ctx, LANES), F32)
    _, kc, vc, _, _ = _even_in(ctx.reshape(b * n_ctx, d), mods, 0, lambda i: b, g_norm1[0][None, :],
                               w_in_bf, (ones, zeros, zeros), 1, n_ctx)
    bsp_full = jnp.repeat(b_spatial[0].T, HEAD_DIM, axis=1)
    half = N_Q_HEADS // 2
    w_att = w_out_even[0][:Q_DIM].reshape(2, half, HEAD_DIM, d).transpose(1, 0, 2, 3).reshape(Q_DIM, d)
    w_out_bf = jnp.concatenate([w_att, w_out_even[0][Q_DIM:]], axis=0).astype(BF16)
    conv_w8 = jnp.concatenate([conv_w[0], jnp.zeros((8 - conv_w.shape[1], d), F32)], axis=0)
    w_in_odd_bf = w_in_odd[0].astype(BF16)
    w_out_odd_bf = w_out_odd[0].astype(BF16)
    wsp_bf = jnp.concatenate([w_spatial[0][0::2], w_spatial[0][1::2]], axis=-1).astype(BF16)
    wr0, br0 = _router_weights(w_router_group[0], b_router_group[0], w_router_expert[0], b_router_expert[0])
    wr1, br1 = _router_weights(w_router_group[1], b_router_group[1], w_router_expert[1], b_router_expert[1])

    part = n // MOE_PARTS
    out = None
    for p in range(MOE_PARTS):
        tok0 = p * part
        batch0 = tok0 // s
        lat_p, *routed = _even_mix(lat, qx, k, v, kc, vc, u, z, attn_sink[0], g_sgu[0][None, :],
                                   wsp_bf, bsp_full, w_out_bf, mods, s, n_ctx,
                                   g_norm2[0][None, :], wr0, br0, tok0, part)
        moe_rows = _moe_rows(routed, 0, w_gate, w_up, w_down)
        lat_p, *routed = _odd_mix(lat_p, moe_rows, mods, 1, g_norm1[1][None, :], w_in_odd_bf, conv_w8,
                                  w_out_odd_bf, s, g_norm2[1][None, :], wr1, br1, batch0)
        yg, wcol = _moe_rows(routed, 1, w_gate, w_up, w_down)
        out = _combine(lat_p, yg, wcol, mods, 1, batch0, gf, s, n, tok0, out)
    return out.reshape(b, s, d)
```
